```python
import math
import jax
import jax.numpy as jnp
from jax import lax
import numpy as np


D_MODEL = 1024
BATCH = 8
SEQ = 4096
DEPTH = 2

MEM_LEN = 256
MIX_W = 512
N_BRANCH = 3
S5_GROUP_CH = 16
S5_GROUPS = MIX_W // S5_GROUP_CH
S5_STATE = 64
GLA_HEADS = 4
GLA_DK = 64
GLA_DV = 128
GLA_GATE_RANK = 16
GLA_TAU = 16.0
GLA_CHUNK = 64
ATT_GROUPS = ((128, 1), (512, 4), (2048, 16))
ATT_HEADS_PER_GROUP = 4
N_ATT_HEADS = ATT_HEADS_PER_GROUP * len(ATT_GROUPS)
ATT_HEAD_DIM = MIX_W // ATT_HEADS_PER_GROUP
ATT_BLOCK = 128
ALIBI_MAX_EXP = 8.0
X_HEADS = 4
X_HEAD_DIM = D_MODEL // X_HEADS
D_FF = 4 * D_MODEL
RMS_EPS = 1e-6
PROJ_WIDTHS = (MIX_W,
               GLA_HEADS * GLA_DK,
               GLA_HEADS * GLA_DK,
               GLA_HEADS * GLA_DV,
               GLA_GATE_RANK,
               GLA_HEADS * GLA_DV,
               N_ATT_HEADS * ATT_HEAD_DIM,
               N_ATT_HEADS * ATT_HEAD_DIM,
               N_ATT_HEADS * ATT_HEAD_DIM,
               N_BRANCH * D_MODEL)
D_IN = sum(PROJ_WIDTHS)

kernel_name = 'hybrid_s5_gla_dilated_block'


def rms_norm(x, gain):
    xf = x.astype(jnp.float32)
    y = xf * lax.rsqrt(jnp.mean(xf * xf, axis=-1, keepdims=True) + RMS_EPS)
    return (y * gain.astype(jnp.float32)).astype(x.dtype)


def alibi_slopes(n):
    return 2.0 ** (-ALIBI_MAX_EXP * jnp.arange(1, n + 1, dtype=jnp.float32) / n)


def s5_branch(u, a_re, a_im, log_step, b_re, b_im, c_re, c_im, d_skip, w_glu, b_glu):
    f32 = jnp.float32
    bsz, seq, _ = u.shape
    uf = u.astype(f32)
    lam = lax.complex(a_re.astype(f32), a_im.astype(f32))
    step = jnp.exp(log_step.astype(f32))[:, None]
    lam_bar = jnp.exp(lam * step)
    b_bar = ((lam_bar - 1.0) / lam)[:, :, None] * lax.complex(b_re.astype(f32), b_im.astype(f32))
    ug = uf.reshape(bsz, seq, S5_GROUPS, S5_GROUP_CH).astype(jnp.complex64)
    bu = jnp.einsum('gpc,bsgc->bsgp', b_bar, ug)
    a_seq = jnp.broadcast_to(lam_bar, (1, seq) + lam_bar.shape)

    def combine(left, right):
        a_l, b_l = left
        a_r, b_r = right
        return a_r * a_l, a_r * b_l + b_r

    _, states = lax.associative_scan(combine, (a_seq, bu), axis=1)
    c_mat = lax.complex(c_re.astype(f32), c_im.astype(f32))
    y = jnp.einsum('gcp,bsgp->bsgc', c_mat, states).real.reshape(bsz, seq, MIX_W)
    y = y + d_skip.astype(f32) * uf
    g = jax.nn.gelu(y)
    return (g * jax.nn.sigmoid(g @ w_glu.astype(f32) + b_glu.astype(f32))).astype(u.dtype)


def gla_chunked(q, k, v, log_a):
    bsz, seq, nh, dk = q.shape
    dv = v.shape[-1]
    nc = seq // GLA_CHUNK

    def chunks(t):
        return t.reshape(bsz, nc, GLA_CHUNK, nh, t.shape[-1]).transpose(1, 0, 3, 2, 4)

    causal = jnp.tril(jnp.ones((GLA_CHUNK, GLA_CHUNK), dtype=bool))

    def step(state, inp):
        qc, kc, vc, lac = inp
        b = jnp.cumsum(lac, axis=2)
        inter = jnp.einsum('bhcd,bhde->bhce', qc * jnp.exp(b), state)
        diff = b[:, :, :, None, :] - b[:, :, None, :, :]
        decay = jnp.exp(jnp.where(causal[:, :, None], diff, -jnp.inf))
        scores = jnp.einsum('bhijd,bhjd->bhij', qc[:, :, :, None, :] * decay, kc)
        intra = jnp.einsum('bhij,bhje->bhie', scores, vc)
        b_end = b[:, :, -1:, :]
        new_state = jnp.exp(b_end[:, :, 0, :])[..., None] * state + jnp.einsum('bhjd,bhje->bhde', kc * jnp.exp(b_end - b), vc)
        return new_state, inter + intra

    state0 = jnp.zeros((bsz, nh, dk, dv), jnp.float32)
    _, out = lax.scan(step, state0, (chunks(q), chunks(k), chunks(v), chunks(log_a)))
    return out.transpose(1, 0, 3, 2, 4).reshape(bsz, seq, nh, dv)


def gla_branch(q, k, v, g_lr, r, w_gate, b_gate, g_out):
    f32 = jnp.float32
    bsz, seq, _ = q.shape
    qh = q.astype(f32).reshape(bsz, seq, GLA_HEADS, GLA_DK) * GLA_DK ** -0.5
    kh = k.astype(f32).reshape(bsz, seq, GLA_HEADS, GLA_DK)
    vh = v.astype(f32).reshape(bsz, seq, GLA_HEADS, GLA_DV)
    log_a = jax.nn.log_sigmoid(g_lr.astype(f32) @ w_gate.astype(f32) + b_gate.astype(f32)) / GLA_TAU
    log_a = log_a.reshape(bsz, seq, GLA_HEADS, GLA_DK)
    o = rms_norm(gla_chunked(qh, kh, vh, log_a), g_out)
    return (o.reshape(bsz, seq, MIX_W) * jax.nn.silu(r.astype(f32))).astype(q.dtype)


def dilated_window_attention(q, k, v, slopes, window, dilation):
    bsz, seq, hg, dh = q.shape
    sub_len = seq // dilation
    n_blk = -(-sub_len // ATT_BLOCK)
    pad_len = n_blk * ATT_BLOCK
    steps = window // dilation

    def to_sub(t):
        t = t.reshape(bsz, sub_len, dilation, hg, dh).transpose(0, 2, 1, 3, 4).reshape(bsz * dilation, sub_len, hg, dh)
        return jnp.pad(t, ((0, 0), (0, pad_len - sub_len), (0, 0), (0, 0)))

    def kv_blocks(t):
        tp = jnp.pad(t, ((0, 0), (ATT_BLOCK, 0), (0, 0), (0, 0)))
        prev = tp[:, :pad_len].reshape(-1, n_blk, ATT_BLOCK, hg, dh)
        cur = tp[:, ATT_BLOCK:].reshape(-1, n_blk, ATT_BLOCK, hg, dh)
        return jnp.concatenate([prev, cur], axis=2)

    qb = to_sub(q).reshape(-1, n_blk, ATT_BLOCK, hg, dh)
    kb = kv_blocks(to_sub(k))
    vb = kv_blocks(to_sub(v))
    scores = jnp.einsum('rnqhd,rnkhd->rnhqk', qb, kb) * dh ** -0.5
    blk = jnp.arange(n_blk)[:, None] * ATT_BLOCK
    q_pos = blk + jnp.arange(ATT_BLOCK)[None, :]
    k_pos = blk - ATT_BLOCK + jnp.arange(2 * ATT_BLOCK)[None, :]
    dist = q_pos[:, :, None] - k_pos[:, None, :]
    valid = (dist >= 0) & (dist <= steps) & (k_pos[:, None, :] >= 0)
    bias = -slopes[None, :, None, None] * (dilation * dist).astype(jnp.float32)[:, None]
    logits = jnp.where(valid[:, None], scores + bias, -jnp.inf)
    lse = jax.nn.logsumexp(logits, axis=-1)
    probs = jnp.exp(logits - lse[..., None])
    out = jnp.einsum('rnhqk,rnkhd->rnqhd', probs, vb).reshape(bsz, dilation, pad_len, hg, dh)[:, :, :sub_len]
    out = out.transpose(0, 2, 1, 3, 4).reshape(bsz, seq, hg, dh)
    lse = lse.transpose(0, 1, 3, 2).reshape(bsz, dilation, pad_len, hg)[:, :, :sub_len]
    lse = lse.transpose(0, 2, 1, 3).reshape(bsz, seq, hg)
    return out, lse


def dilated_branch(q, k, v):
    f32 = jnp.float32
    bsz, seq, _ = q.shape
    qh = q.astype(f32).reshape(bsz, seq, N_ATT_HEADS, ATT_HEAD_DIM)
    kh = k.astype(f32).reshape(bsz, seq, N_ATT_HEADS, ATT_HEAD_DIM)
    vh = v.astype(f32).reshape(bsz, seq, N_ATT_HEADS, ATT_HEAD_DIM)
    slopes = alibi_slopes(N_ATT_HEADS)
    outs, lses = [], []
    for g, (window, dilation) in enumerate(ATT_GROUPS):
        hs = slice(g * ATT_HEADS_PER_GROUP, (g + 1) * ATT_HEADS_PER_GROUP)
        o, lse = dilated_window_attention(qh[:, :, hs], kh[:, :, hs], vh[:, :, hs], slopes[hs], window, dilation)
        outs.append(o)
        lses.append(lse)
    outs = jnp.stack(outs, axis=2)
    weights = jax.nn.softmax(jnp.stack(lses, axis=2), axis=2)
    return jnp.sum(weights[..., None] * outs, axis=2).reshape(bsz, seq, MIX_W).astype(q.dtype)


def cross_attention(h, mem_n, w_q, w_kv, w_o):
    bsz, seq, _ = h.shape
    q = (h @ w_q).reshape(bsz, seq, X_HEADS, X_HEAD_DIM)
    k, v = jnp.split(mem_n @ w_kv, 2, axis=-1)
    k = k.reshape(bsz, MEM_LEN, X_HEADS, X_HEAD_DIM)
    v = v.reshape(bsz, MEM_LEN, X_HEADS, X_HEAD_DIM)
    scores = jnp.einsum('bshd,bmhd->bhsm', q, k).astype(jnp.float32) * X_HEAD_DIM ** -0.5
    probs = jax.nn.softmax(scores, axis=-1).astype(v.dtype)
    o = jnp.einsum('bhsm,bmhd->bshd', probs, v).reshape(bsz, seq, D_MODEL)
    return o @ w_o


def squared_relu_mlp(h, w_up, w_down):
    return jnp.square(jax.nn.relu(h @ w_up)) @ w_down


def _fwd_setup_inputs(seed: int = 0) -> dict:
    key = jax.random.key(seed)
    ks = jax.random.split(key, 32)
    f32 = jnp.float32

    def nrm(k, shape, scale):
        return jax.random.normal(k, shape, f32) * scale

    def gain(k, shape):
        return 1.0 + 0.01 * jax.random.normal(k, shape, f32)

    return {
        'x': nrm(ks[0], (BATCH, SEQ, D_MODEL), 1.0),
        'mem': nrm(ks[1], (BATCH, MEM_LEN, D_MODEL), 1.0),
        'g_mix': gain(ks[2], (DEPTH, D_MODEL)),
        'w_in': nrm(ks[3], (DEPTH, D_MODEL, D_IN), D_MODEL ** -0.5),
        's5_a_re': -0.5 + nrm(ks[4], (DEPTH, S5_GROUPS, S5_STATE), 0.01),
        's5_a_im': math.pi * jnp.arange(S5_STATE, dtype=f32)[None, None, :] + nrm(ks[5], (DEPTH, S5_GROUPS, S5_STATE), 0.01),
        's5_log_step': jax.random.uniform(ks[6], (DEPTH, S5_GROUPS), f32, math.log(1e-3), math.log(1e-1)),
        's5_b_re': nrm(ks[7], (DEPTH, S5_GROUPS, S5_STATE, S5_GROUP_CH), (2 * S5_GROUP_CH) ** -0.5),
        's5_b_im': nrm(ks[8], (DEPTH, S5_GROUPS, S5_STATE, S5_GROUP_CH), (2 * S5_GROUP_CH) ** -0.5),
        's5_c_re': nrm(ks[9], (DEPTH, S5_GROUPS, S5_GROUP_CH, S5_STATE), S5_STATE ** -0.25),
        's5_c_im': nrm(ks[10], (DEPTH, S5_GROUPS, S5_GROUP_CH, S5_STATE), S5_STATE ** -0.25),
        's5_d': nrm(ks[11], (DEPTH, MIX_W), 1.0),
        'w_glu': nrm(ks[12], (DEPTH, MIX_W, MIX_W), MIX_W ** -0.5),
        'b_glu': nrm(ks[13], (DEPTH, MIX_W), 0.01),
        'w_gla_gate': nrm(ks[14], (DEPTH, GLA_GATE_RANK, GLA_HEADS * GLA_DK), GLA_GATE_RANK ** -0.5),
        'b_gla_gate': nrm(ks[15], (DEPTH, GLA_HEADS * GLA_DK), 0.01),
        'g_gla_out': gain(ks[16], (DEPTH, GLA_DV)),
        'w_branch': nrm(ks[17], (DEPTH, N_BRANCH, MIX_W, D_MODEL), MIX_W ** -0.5),
        'w_out': nrm(ks[18], (DEPTH, D_MODEL, D_MODEL), D_MODEL ** -0.5),
        'g_mem': gain(ks[19], (D_MODEL,)),
        'g_cross': gain(ks[20], (DEPTH, D_MODEL)),
        'w_xq': nrm(ks[21], (DEPTH, D_MODEL, D_MODEL), D_MODEL ** -0.5),
        'w_xkv': nrm(ks[22], (DEPTH, D_MODEL, 2 * D_MODEL), D_MODEL ** -0.5),
        'w_xo': nrm(ks[23], (DEPTH, D_MODEL, D_MODEL), D_MODEL ** -0.5),
        'g_mlp': gain(ks[24], (DEPTH, D_MODEL)),
        'w_up': nrm(ks[25], (DEPTH, D_MODEL, D_FF), D_MODEL ** -0.5),
        'w_down': nrm(ks[26], (DEPTH, D_FF, D_MODEL), D_FF ** -0.5),
        'g_final': gain(ks[27], (D_MODEL,)),
    }


def _fwd_reference(x, mem, g_mix, w_in, s5_a_re, s5_a_im, s5_log_step, s5_b_re, s5_b_im, s5_c_re, s5_c_im, s5_d,
              w_glu, b_glu, w_gla_gate, b_gla_gate, g_gla_out, w_branch, w_out, g_mem, g_cross, w_xq, w_xkv,
              w_xo, g_mlp, w_up, w_down, g_final):
    bsz, seq, _ = x.shape
    mem_n = rms_norm(mem, g_mem)
    split_points = [int(p) for p in np.cumsum(PROJ_WIDTHS)[:-1]]
    for l in range(DEPTH):
        h = rms_norm(x, g_mix[l])
        (u_s5, q_gla, k_gla, v_gla, lr_gla, r_gla, q_att, k_att, v_att, gate_logits) = jnp.split(h @ w_in[l], split_points, axis=-1)
        y_a = s5_branch(u_s5, s5_a_re[l], s5_a_im[l], s5_log_step[l], s5_b_re[l], s5_b_im[l], s5_c_re[l], s5_c_im[l],
                        s5_d[l], w_glu[l], b_glu[l])
        y_b = gla_branch(q_gla, k_gla, v_gla, lr_gla, r_gla, w_gla_gate[l], b_gla_gate[l], g_gla_out[l])
        y_c = dilated_branch(q_att, k_att, v_att)
        branches = jnp.stack([y_a, y_b, y_c], axis=2)
        gates = jax.nn.sigmoid(gate_logits.reshape(bsz, seq, N_BRANCH, D_MODEL))
        merged = jnp.sum(gates * jnp.einsum('bsnc,ncd->bsnd', branches, w_branch[l]), axis=2)
        x = x + merged @ w_out[l]
        x = x + cross_attention(rms_norm(x, g_cross[l]), mem_n, w_xq[l], w_xkv[l], w_xo[l])
        x = x + squared_relu_mlp(rms_norm(x, g_mlp[l]), w_up[l], w_down[l])
    return rms_norm(x, g_final)


import jax as _jax
import jax.numpy as _jnp

TWIN_FORMAT = 'train_step'
FWD_PARAMS = ['x', 'mem', 'g_mix', 'w_in', 's5_a_re', 's5_a_im', 's5_log_step', 's5_b_re', 's5_b_im', 's5_c_re', 's5_c_im', 's5_d', 'w_glu', 'b_glu', 'w_gla_gate', 'b_gla_gate', 'g_gla_out', 'w_branch', 'w_out', 'g_mem', 'g_cross', 'w_xq', 'w_xkv', 'w_xo', 'g_mlp', 'w_up', 'w_down', 'g_final']
TWIN_WEIGHTS = ['g_mix', 'w_in', 's5_a_re', 's5_a_im', 's5_log_step', 's5_b_re', 's5_b_im', 's5_c_re', 's5_c_im', 's5_d', 'w_glu', 'b_glu', 'w_gla_gate', 'b_gla_gate', 'g_gla_out', 'w_branch', 'w_out', 'g_mem', 'g_cross', 'w_xq', 'w_xkv', 'w_xo', 'g_mlp', 'w_up', 'w_down', 'g_final']
TWIN_DIFF_INPUT = 'x'
TWIN_INPUTS = ['x', 'mem', 'g_mix', 'w_in', 's5_a_re', 's5_a_im', 's5_log_step', 's5_b_re', 's5_b_im', 's5_c_re', 's5_c_im', 's5_d', 'w_glu', 'b_glu', 'w_gla_gate', 'b_gla_gate', 'g_gla_out', 'w_branch', 'w_out', 'g_mem', 'g_cross', 'w_xq', 'w_xkv', 'w_xo', 'g_mlp', 'w_up', 'w_down', 'g_final', 'loss_target', 'm_g_mix', 'm_w_in', 'm_s5_a_re', 'm_s5_a_im', 'm_s5_log_step', 'm_s5_b_re', 'm_s5_b_im', 'm_s5_c_re', 'm_s5_c_im', 'm_s5_d', 'm_w_glu', 'm_b_glu', 'm_w_gla_gate', 'm_b_gla_gate', 'm_g_gla_out', 'm_w_branch', 'm_w_out', 'm_g_mem', 'm_g_cross', 'm_w_xq', 'm_w_xkv', 'm_w_xo', 'm_g_mlp', 'm_w_up', 'm_w_down', 'm_g_final', 'v_g_mix', 'v_w_in', 'v_s5_a_re', 'v_s5_a_im', 'v_s5_log_step', 'v_s5_b_re', 'v_s5_b_im', 'v_s5_c_re', 'v_s5_c_im', 'v_s5_d', 'v_w_glu', 'v_b_glu', 'v_w_gla_gate', 'v_b_gla_gate', 'v_g_gla_out', 'v_w_branch', 'v_w_out', 'v_g_mem', 'v_g_cross', 'v_w_xq', 'v_w_xkv', 'v_w_xo', 'v_g_mlp', 'v_w_up', 'v_w_down', 'v_g_final']
TWIN_OUTPUTS = ['loss', 'grad_x', 'grad_g_mix', 'grad_w_in', 'grad_s5_a_re', 'grad_s5_a_im', 'grad_s5_log_step', 'grad_s5_b_re', 'grad_s5_b_im', 'grad_s5_c_re', 'grad_s5_c_im', 'grad_s5_d', 'grad_w_glu', 'grad_b_glu', 'grad_w_gla_gate', 'grad_b_gla_gate', 'grad_g_gla_out', 'grad_w_branch', 'grad_w_out', 'grad_g_mem', 'grad_g_cross', 'grad_w_xq', 'grad_w_xkv', 'grad_w_xo', 'grad_g_mlp', 'grad_w_up', 'grad_w_down', 'grad_g_final', 'delta_g_mix', 'delta_w_in', 'delta_s5_a_re', 'delta_s5_a_im', 'delta_s5_log_step', 'delta_s5_b_re', 'delta_s5_b_im', 'delta_s5_c_re', 'delta_s5_c_im', 'delta_s5_d', 'delta_w_glu', 'delta_b_glu', 'delta_w_gla_gate', 'delta_b_gla_gate', 'delta_g_gla_out', 'delta_w_branch', 'delta_w_out', 'delta_g_mem', 'delta_g_cross', 'delta_w_xq', 'delta_w_xkv', 'delta_w_xo', 'delta_g_mlp', 'delta_w_up', 'delta_w_down', 'delta_g_final', 'new_m_g_mix', 'new_m_w_in', 'new_m_s5_a_re', 'new_m_s5_a_im', 'new_m_s5_log_step', 'new_m_s5_b_re', 'new_m_s5_b_im', 'new_m_s5_c_re', 'new_m_s5_c_im', 'new_m_s5_d', 'new_m_w_glu', 'new_m_b_glu', 'new_m_w_gla_gate', 'new_m_b_gla_gate', 'new_m_g_gla_out', 'new_m_w_branch', 'new_m_w_out', 'new_m_g_mem', 'new_m_g_cross', 'new_m_w_xq', 'new_m_w_xkv', 'new_m_w_xo', 'new_m_g_mlp', 'new_m_w_up', 'new_m_w_down', 'new_m_g_final', 'new_v_g_mix', 'new_v_w_in', 'new_v_s5_a_re', 'new_v_s5_a_im', 'new_v_s5_log_step', 'new_v_s5_b_re', 'new_v_s5_b_im', 'new_v_s5_c_re', 'new_v_s5_c_im', 'new_v_s5_d', 'new_v_w_glu', 'new_v_b_glu', 'new_v_w_gla_gate', 'new_v_b_gla_gate', 'new_v_g_gla_out', 'new_v_w_branch', 'new_v_w_out', 'new_v_g_mem', 'new_v_g_cross', 'new_v_w_xq', 'new_v_w_xkv', 'new_v_w_xo', 'new_v_g_mlp', 'new_v_w_up', 'new_v_w_down', 'new_v_g_final']
TWIN_LEAF_KINDS = {'loss': 'loss', 'grad_x': 'grad_x', 'grad_g_mix': 'grad_w', 'grad_w_in': 'grad_w', 'grad_s5_a_re': 'grad_w', 'grad_s5_a_im': 'grad_w', 'grad_s5_log_step': 'grad_w', 'grad_s5_b_re': 'grad_w', 'grad_s5_b_im': 'grad_w', 'grad_s5_c_re': 'grad_w', 'grad_s5_c_im': 'grad_w', 'grad_s5_d': 'grad_w', 'grad_w_glu': 'grad_w', 'grad_b_glu': 'grad_w', 'grad_w_gla_gate': 'grad_w', 'grad_b_gla_gate': 'grad_w', 'grad_g_gla_out': 'grad_w', 'grad_w_branch': 'grad_w', 'grad_w_out': 'grad_w', 'grad_g_mem': 'grad_w', 'grad_g_cross': 'grad_w', 'grad_w_xq': 'grad_w', 'grad_w_xkv': 'grad_w', 'grad_w_xo': 'grad_w', 'grad_g_mlp': 'grad_w', 'grad_w_up': 'grad_w', 'grad_w_down': 'grad_w', 'grad_g_final': 'grad_w', 'delta_g_mix': 'delta_w', 'delta_w_in': 'delta_w', 'delta_s5_a_re': 'delta_w', 'delta_s5_a_im': 'delta_w', 'delta_s5_log_step': 'delta_w', 'delta_s5_b_re': 'delta_w', 'delta_s5_b_im': 'delta_w', 'delta_s5_c_re': 'delta_w', 'delta_s5_c_im': 'delta_w', 'delta_s5_d': 'delta_w', 'delta_w_glu': 'delta_w', 'delta_b_glu': 'delta_w', 'delta_w_gla_gate': 'delta_w', 'delta_b_gla_gate': 'delta_w', 'delta_g_gla_out': 'delta_w', 'delta_w_branch': 'delta_w', 'delta_w_out': 'delta_w', 'delta_g_mem': 'delta_w', 'delta_g_cross': 'delta_w', 'delta_w_xq': 'delta_w', 'delta_w_xkv': 'delta_w', 'delta_w_xo': 'delta_w', 'delta_g_mlp': 'delta_w', 'delta_w_up': 'delta_w', 'delta_w_down': 'delta_w', 'delta_g_final': 'delta_w', 'new_m_g_mix': 'new_m', 'new_m_w_in': 'new_m', 'new_m_s5_a_re': 'new_m', 'new_m_s5_a_im': 'new_m', 'new_m_s5_log_step': 'new_m', 'new_m_s5_b_re': 'new_m', 'new_m_s5_b_im': 'new_m', 'new_m_s5_c_re': 'new_m', 'new_m_s5_c_im': 'new_m', 'new_m_s5_d': 'new_m', 'new_m_w_glu': 'new_m', 'new_m_b_glu': 'new_m', 'new_m_w_gla_gate': 'new_m', 'new_m_b_gla_gate': 'new_m', 'new_m_g_gla_out': 'new_m', 'new_m_w_branch': 'new_m', 'new_m_w_out': 'new_m', 'new_m_g_mem': 'new_m', 'new_m_g_cross': 'new_m', 'new_m_w_xq': 'new_m', 'new_m_w_xkv': 'new_m', 'new_m_w_xo': 'new_m', 'new_m_g_mlp': 'new_m', 'new_m_w_up': 'new_m', 'new_m_w_down': 'new_m', 'new_m_g_final': 'new_m', 'new_v_g_mix': 'new_v', 'new_v_w_in': 'new_v', 'new_v_s5_a_re': 'new_v', 'new_v_s5_a_im': 'new_v', 'new_v_s5_log_step': 'new_v', 'new_v_s5_b_re': 'new_v', 'new_v_s5_b_im': 'new_v', 'new_v_s5_c_re': 'new_v', 'new_v_s5_c_im': 'new_v', 'new_v_s5_d': 'new_v', 'new_v_w_glu': 'new_v', 'new_v_b_glu': 'new_v', 'new_v_w_gla_gate': 'new_v', 'new_v_b_gla_gate': 'new_v', 'new_v_g_gla_out': 'new_v', 'new_v_w_branch': 'new_v', 'new_v_w_out': 'new_v', 'new_v_g_mem': 'new_v', 'new_v_g_cross': 'new_v', 'new_v_w_xq': 'new_v', 'new_v_w_xkv': 'new_v', 'new_v_w_xo': 'new_v', 'new_v_g_mlp': 'new_v', 'new_v_w_up': 'new_v', 'new_v_w_down': 'new_v', 'new_v_g_final': 'new_v'}


def _forward(args):
    return _fwd_reference(*[args[k] for k in FWD_PARAMS])


def _output_shape():
    out = _jax.eval_shape(lambda: _forward(_fwd_setup_inputs(0)))
    return out.shape, out.dtype

N_MICROBATCH = 1
ADAM_LR = 0.001
ADAM_B1 = 0.9
ADAM_B2 = 0.999
ADAM_EPS = 1e-08
ADAM_WD = 0.01
ADAM_STEP = 10
PER_EXAMPLE_BATCH_AXIS = {'x': 0, 'mem': 0, 'loss_target': 0}
SHARED_INPUTS = []
_WEIGHT_DTYPES = {'g_mix': _jnp.float32, 'w_in': _jnp.float32, 's5_a_re': _jnp.float32, 's5_a_im': _jnp.float32, 's5_log_step': _jnp.float32, 's5_b_re': _jnp.float32, 's5_b_im': _jnp.float32, 's5_c_re': _jnp.float32, 's5_c_im': _jnp.float32, 's5_d': _jnp.float32, 'w_glu': _jnp.float32, 'b_glu': _jnp.float32, 'w_gla_gate': _jnp.float32, 'b_gla_gate': _jnp.float32, 'g_gla_out': _jnp.float32, 'w_branch': _jnp.float32, 'w_out': _jnp.float32, 'g_mem': _jnp.float32, 'g_cross': _jnp.float32, 'w_xq': _jnp.float32, 'w_xkv': _jnp.float32, 'w_xo': _jnp.float32, 'g_mlp': _jnp.float32, 'w_up': _jnp.float32, 'w_down': _jnp.float32, 'g_final': _jnp.float32}
MOMENT_SCALE = {'g_mix': 1.237183e-01, 'w_in': 4.058574e-02, 's5_a_re': 1.641289e-02, 's5_a_im': 1.145301e-02, 's5_log_step': 7.684621e+00, 's5_b_re': 7.291078e-03, 's5_b_im': 7.359505e-03, 's5_c_re': 3.584300e-03, 's5_c_im': 3.503911e-03, 's5_d': 4.888542e-02, 'w_glu': 1.433471e-02, 'b_glu': 2.020262e-02, 'w_gla_gate': 1.592814e-02, 'b_gla_gate': 5.115681e-02, 'g_gla_out': 1.494910e-01, 'w_branch': 3.915357e-02, 'w_out': 6.760437e-02, 'g_mem': 3.850126e-02, 'g_cross': 1.692381e-02, 'w_xq': 1.684602e-02, 'w_xkv': 1.718337e-02, 'w_xo': 1.738920e-02, 'g_mlp': 1.520754e-01, 'w_up': 7.398001e-02, 'w_down': 1.337587e-01, 'g_final': 3.247301e+01}


def _to_microbatches(a, axis):
    t = _jnp.moveaxis(a, axis, 0)
    t = t.reshape((N_MICROBATCH, t.shape[0] // N_MICROBATCH) + t.shape[1:])
    return _jnp.moveaxis(t, 1, axis + 1)


def setup_inputs(seed: int = 0) -> dict:
    inp = _fwd_setup_inputs(seed)
    key = _jax.random.fold_in(_jax.random.key(seed), 7919)
    shape, _ = _output_shape()
    out = dict(inp)
    out["loss_target"] = _jax.random.normal(_jax.random.fold_in(key, 0), shape, _jnp.float32)
    for i, name in enumerate(TWIN_WEIGHTS):
        w = inp[name].astype(_jnp.float32)
        if MOMENT_SCALE is None:
            s = _jnp.sqrt(_jnp.mean(_jnp.square(w)) + 1e-30)
        else:
            s = MOMENT_SCALE[name]
        km, kv = _jax.random.split(_jax.random.fold_in(key, i + 1))
        out[name] = w
        out["m_" + name] = s * _jax.random.normal(km, w.shape, _jnp.float32)
        out["v_" + name] = (s * s) * _jax.random.uniform(kv, w.shape, _jnp.float32, 0.5, 1.5)
    if N_MICROBATCH > 1:
        for name, axis in PER_EXAMPLE_BATCH_AXIS.items():
            out[name] = _to_microbatches(out[name], axis)
    return {'x': out['x'], 'mem': out['mem'], 'g_mix': out['g_mix'], 'w_in': out['w_in'], 's5_a_re': out['s5_a_re'], 's5_a_im': out['s5_a_im'], 's5_log_step': out['s5_log_step'], 's5_b_re': out['s5_b_re'], 's5_b_im': out['s5_b_im'], 's5_c_re': out['s5_c_re'], 's5_c_im': out['s5_c_im'], 's5_d': out['s5_d'], 'w_glu': out['w_glu'], 'b_glu': out['b_glu'], 'w_gla_gate': out['w_gla_gate'], 'b_gla_gate': out['b_gla_gate'], 'g_gla_out': out['g_gla_out'], 'w_branch': out['w_branch'], 'w_out': out['w_out'], 'g_mem': out['g_mem'], 'g_cross': out['g_cross'], 'w_xq': out['w_xq'], 'w_xkv': out['w_xkv'], 'w_xo': out['w_xo'], 'g_mlp': out['g_mlp'], 'w_up': out['w_up'], 'w_down': out['w_down'], 'g_final': out['g_final'], 'loss_target': out['loss_target'], 'm_g_mix': out['m_g_mix'], 'm_w_in': out['m_w_in'], 'm_s5_a_re': out['m_s5_a_re'], 'm_s5_a_im': out['m_s5_a_im'], 'm_s5_log_step': out['m_s5_log_step'], 'm_s5_b_re': out['m_s5_b_re'], 'm_s5_b_im': out['m_s5_b_im'], 'm_s5_c_re': out['m_s5_c_re'], 'm_s5_c_im': out['m_s5_c_im'], 'm_s5_d': out['m_s5_d'], 'm_w_glu': out['m_w_glu'], 'm_b_glu': out['m_b_glu'], 'm_w_gla_gate': out['m_w_gla_gate'], 'm_b_gla_gate': out['m_b_gla_gate'], 'm_g_gla_out': out['m_g_gla_out'], 'm_w_branch': out['m_w_branch'], 'm_w_out': out['m_w_out'], 'm_g_mem': out['m_g_mem'], 'm_g_cross': out['m_g_cross'], 'm_w_xq': out['m_w_xq'], 'm_w_xkv': out['m_w_xkv'], 'm_w_xo': out['m_w_xo'], 'm_g_mlp': out['m_g_mlp'], 'm_w_up': out['m_w_up'], 'm_w_down': out['m_w_down'], 'm_g_final': out['m_g_final'], 'v_g_mix': out['v_g_mix'], 'v_w_in': out['v_w_in'], 'v_s5_a_re': out['v_s5_a_re'], 'v_s5_a_im': out['v_s5_a_im'], 'v_s5_log_step': out['v_s5_log_step'], 'v_s5_b_re': out['v_s5_b_re'], 'v_s5_b_im': out['v_s5_b_im'], 'v_s5_c_re': out['v_s5_c_re'], 'v_s5_c_im': out['v_s5_c_im'], 'v_s5_d': out['v_s5_d'], 'v_w_glu': out['v_w_glu'], 'v_b_glu': out['v_b_glu'], 'v_w_gla_gate': out['v_w_gla_gate'], 'v_b_gla_gate': out['v_b_gla_gate'], 'v_g_gla_out': out['v_g_gla_out'], 'v_w_branch': out['v_w_branch'], 'v_w_out': out['v_w_out'], 'v_g_mem': out['v_g_mem'], 'v_g_cross': out['v_g_cross'], 'v_w_xq': out['v_w_xq'], 'v_w_xkv': out['v_w_xkv'], 'v_w_xo': out['v_w_xo'], 'v_g_mlp': out['v_g_mlp'], 'v_w_up': out['v_w_up'], 'v_w_down': out['v_w_down'], 'v_g_final': out['v_g_final']}


def _loss(weights, diff, rest, loss_target):
    with _jax.named_scope("forward"):
        args = {**rest, TWIN_DIFF_INPUT: diff, **{k: w.astype(_WEIGHT_DTYPES[k]) for k, w in weights.items()}}
        y = _forward(args)
    with _jax.named_scope("loss_head"):
        err = _jnp.square(y.astype(_jnp.float32) - loss_target)
        return 0.5 * _jnp.sum(_jnp.mean(err, axis=-1)) if err.ndim else 0.5 * err


def _adamw(w, g, m, v):
    m = ADAM_B1 * m + (1.0 - ADAM_B1) * g
    v = ADAM_B2 * v + (1.0 - ADAM_B2) * _jnp.square(g)
    m_hat = m / (1.0 - ADAM_B1 ** ADAM_STEP)
    v_hat = v / (1.0 - ADAM_B2 ** ADAM_STEP)
    delta = -ADAM_LR * (m_hat / (_jnp.sqrt(v_hat) + ADAM_EPS) + ADAM_WD * w)
    return delta, m, v


def reference(x, mem, g_mix, w_in, s5_a_re, s5_a_im, s5_log_step, s5_b_re, s5_b_im, s5_c_re, s5_c_im, s5_d, w_glu, b_glu, w_gla_gate, b_gla_gate, g_gla_out, w_branch, w_out, g_mem, g_cross, w_xq, w_xkv, w_xo, g_mlp, w_up, w_down, g_final, loss_target, m_g_mix, m_w_in, m_s5_a_re, m_s5_a_im, m_s5_log_step, m_s5_b_re, m_s5_b_im, m_s5_c_re, m_s5_c_im, m_s5_d, m_w_glu, m_b_glu, m_w_gla_gate, m_b_gla_gate, m_g_gla_out, m_w_branch, m_w_out, m_g_mem, m_g_cross, m_w_xq, m_w_xkv, m_w_xo, m_g_mlp, m_w_up, m_w_down, m_g_final, v_g_mix, v_w_in, v_s5_a_re, v_s5_a_im, v_s5_log_step, v_s5_b_re, v_s5_b_im, v_s5_c_re, v_s5_c_im, v_s5_d, v_w_glu, v_b_glu, v_w_gla_gate, v_b_gla_gate, v_g_gla_out, v_w_branch, v_w_out, v_g_mem, v_g_cross, v_w_xq, v_w_xkv, v_w_xo, v_g_mlp, v_w_up, v_w_down, v_g_final):
    given = dict(x=x, mem=mem, g_mix=g_mix, w_in=w_in, s5_a_re=s5_a_re, s5_a_im=s5_a_im, s5_log_step=s5_log_step, s5_b_re=s5_b_re, s5_b_im=s5_b_im, s5_c_re=s5_c_re, s5_c_im=s5_c_im, s5_d=s5_d, w_glu=w_glu, b_glu=b_glu, w_gla_gate=w_gla_gate, b_gla_gate=b_gla_gate, g_gla_out=g_gla_out, w_branch=w_branch, w_out=w_out, g_mem=g_mem, g_cross=g_cross, w_xq=w_xq, w_xkv=w_xkv, w_xo=w_xo, g_mlp=g_mlp, w_up=w_up, w_down=w_down, g_final=g_final, loss_target=loss_target, m_g_mix=m_g_mix, m_w_in=m_w_in, m_s5_a_re=m_s5_a_re, m_s5_a_im=m_s5_a_im, m_s5_log_step=m_s5_log_step, m_s5_b_re=m_s5_b_re, m_s5_b_im=m_s5_b_im, m_s5_c_re=m_s5_c_re, m_s5_c_im=m_s5_c_im, m_s5_d=m_s5_d, m_w_glu=m_w_glu, m_b_glu=m_b_glu, m_w_gla_gate=m_w_gla_gate, m_b_gla_gate=m_b_gla_gate, m_g_gla_out=m_g_gla_out, m_w_branch=m_w_branch, m_w_out=m_w_out, m_g_mem=m_g_mem, m_g_cross=m_g_cross, m_w_xq=m_w_xq, m_w_xkv=m_w_xkv, m_w_xo=m_w_xo, m_g_mlp=m_g_mlp, m_w_up=m_w_up, m_w_down=m_w_down, m_g_final=m_g_final, v_g_mix=v_g_mix, v_w_in=v_w_in, v_s5_a_re=v_s5_a_re, v_s5_a_im=v_s5_a_im, v_s5_log_step=v_s5_log_step, v_s5_b_re=v_s5_b_re, v_s5_b_im=v_s5_b_im, v_s5_c_re=v_s5_c_re, v_s5_c_im=v_s5_c_im, v_s5_d=v_s5_d, v_w_glu=v_w_glu, v_b_glu=v_b_glu, v_w_gla_gate=v_w_gla_gate, v_b_gla_gate=v_b_gla_gate, v_g_gla_out=v_g_gla_out, v_w_branch=v_w_branch, v_w_out=v_w_out, v_g_mem=v_g_mem, v_g_cross=v_g_cross, v_w_xq=v_w_xq, v_w_xkv=v_w_xkv, v_w_xo=v_w_xo, v_g_mlp=v_g_mlp, v_w_up=v_w_up, v_w_down=v_w_down, v_g_final=v_g_final)
    weights = {n: given[n] for n in TWIN_WEIGHTS}
    shared = {n: given[n] for n in SHARED_INPUTS}
    per_example = {n: given[n] for n in ['x', 'mem']}
    grad_fn = _jax.value_and_grad(_loss, argnums=(0, 1))

    def one_microbatch(ex, loss_target):
        ex = dict(ex)
        diff = ex.pop(TWIN_DIFF_INPUT)
        return grad_fn(weights, diff, {**shared, **ex}, loss_target)

    if N_MICROBATCH == 1:
        loss, (grad_w, grad_x) = one_microbatch(per_example, given["loss_target"])
    else:
        def body(carry, xs):
            loss_sum, grad_sum = carry
            l_k, (gw_k, gx_k) = one_microbatch(xs[0], xs[1])
            with _jax.named_scope("update"):
                return (loss_sum + l_k, _jax.tree.map(_jnp.add, grad_sum, gw_k)), gx_k

        init = (_jnp.zeros((), _jnp.float32), _jax.tree.map(_jnp.zeros_like, weights))
        (loss, grad_w), grad_x = _jax.lax.scan(body, init, (per_example, given["loss_target"]))
    with _jax.named_scope("update"):
        delta_w, new_m, new_v = {}, {}, {}
        for n in TWIN_WEIGHTS:
            delta_w[n], new_m[n], new_v[n] = _adamw(weights[n], grad_w[n], given["m_" + n], given["v_" + n])
    return (loss, grad_x, *[grad_w[n] for n in TWIN_WEIGHTS], *[delta_w[n] for n in TWIN_WEIGHTS],
            *[new_m[n] for n in TWIN_WEIGHTS], *[new_v[n] for n in TWIN_WEIGHTS])
```

```python
import functools
import math

import jax
import jax.numpy as jnp
import numpy as np
from jax import lax
from jax.experimental import pallas as pl
from jax.experimental.pallas import tpu as pltpu

F32 = jnp.float32
BF16 = jnp.bfloat16

VMEM_LIMIT_BYTES = 56 * 1024 * 1024


def _cparams(sem):
    return pltpu.CompilerParams(dimension_semantics=sem, vmem_limit_bytes=VMEM_LIMIT_BYTES)


def _dot(a, b, ca=1, cb=0):
    return lax.dot_general(a.astype(BF16), b.astype(BF16), (((ca,), (cb,)), ((), ())), preferred_element_type=F32)


def _pick(n, prefs):
    for p in prefs:
        if n % p == 0:
            return p
    return n


def matmul(a, b, *, ta=False, tb=False, out_dtype=F32, name="mm", a_col=None, b_col=None):
    a_off, a_w = a_col if a_col is not None else (0, a.shape[1])
    b_off, b_w = b_col if b_col is not None else (0, b.shape[1])
    if ta:
        kk, m = a.shape[0], a_w
    else:
        m, kk = a.shape[0], a_w
    if tb:
        n, kb = b.shape[0], b_w
    else:
        kb, n = b.shape[0], b_w
    assert kk == kb, (a.shape, b.shape, ta, tb)
    tm = _pick(m, (512, 256, 128))
    tn = _pick(n, (1024, 512, 256, 128))
    tk = _pick(kk, (1024, 512, 256, 128))
    nk = kk // tk
    if ta:
        assert a_off % tm == 0
        a_spec = pl.BlockSpec((tk, tm), lambda i, j, k, o=a_off // tm: (k, i + o))
    else:
        assert a_off % tk == 0
        a_spec = pl.BlockSpec((tm, tk), lambda i, j, k, o=a_off // tk: (i, k + o))
    if tb:
        assert b_off % tk == 0
        b_spec = pl.BlockSpec((tn, tk), lambda i, j, k, o=b_off // tk: (j, k + o))
    else:
        assert b_off % tn == 0
        b_spec = pl.BlockSpec((tk, tn), lambda i, j, k, o=b_off // tn: (k, j + o))

    def body(a_ref, b_ref, o_ref, acc_ref):
        k = pl.program_id(2)
        p = _dot(a_ref[...], b_ref[...], 0 if ta else 1, 1 if tb else 0)

        @pl.when(k == 0)
        def _():
            acc_ref[...] = p

        @pl.when(k > 0)
        def _():
            acc_ref[...] += p

        @pl.when(k == nk - 1)
        def _():
            o_ref[...] = acc_ref[...].astype(o_ref.dtype)

    return pl.pallas_call(
        body,
        name=name,
        grid=(m // tm, n // tn, nk),
        in_specs=[a_spec, b_spec],
        out_specs=pl.BlockSpec((tm, tn), lambda i, j, k: (i, j)),
        out_shape=jax.ShapeDtypeStruct((m, n), out_dtype),
        scratch_shapes=[pltpu.VMEM((tm, tn), F32)],
        compiler_params=_cparams(("parallel", "parallel", "arbitrary")),
    )(a, b)


RMS_EPS = 1e-6
ROW_TILE = 512


def _rms_fwd(x, g):
    r = lax.rsqrt(jnp.mean(x * x, axis=-1, keepdims=True) + RMS_EPS)
    return x * r * g


def _rms_bwd(x, g, dh):
    r = lax.rsqrt(jnp.mean(x * x, axis=-1, keepdims=True) + RMS_EPS)
    xh = x * r
    dg = jnp.sum(dh * xh, axis=0, keepdims=True)
    dxh = dh * g
    dx = r * (dxh - xh * jnp.mean(dxh * xh, axis=-1, keepdims=True))
    return dx, dg


def _accum(ref, val, first):
    @pl.when(first)
    def _():
        ref[...] = val

    @pl.when(jnp.logical_not(first))
    def _():
        ref[...] += val


def rmsnorm(x, g, *, out_dtype=BF16, name="rmsnorm"):
    s, d = x.shape
    tm = _pick(s, (ROW_TILE, 256, 128, 8))

    def body(x_ref, g_ref, o_ref):
        o_ref[...] = _rms_fwd(x_ref[...], g_ref[...]).astype(o_ref.dtype)

    return pl.pallas_call(
        body,
        name=name,
        grid=(s // tm,),
        in_specs=[pl.BlockSpec((tm, d), lambda i: (i, 0)), pl.BlockSpec((1, d), lambda i: (0, 0))],
        out_specs=pl.BlockSpec((tm, d), lambda i: (i, 0)),
        out_shape=jax.ShapeDtypeStruct((s, d), out_dtype),
        compiler_params=_cparams(("arbitrary",)),
    )(x, g.reshape(1, d))


def rmsnorm_bwd(x, g, dhs, dres, *, name="rmsnorm_bwd"):
    s, d = x.shape
    tm = _pick(s, (ROW_TILE, 256, 128, 8))
    n = len(dhs)

    def body(x_ref, g_ref, *refs):
        dh_refs, dres_ref, dx_ref, dg_ref = refs[:n], refs[n], refs[n + 1], refs[n + 2]
        dh = dh_refs[0][...].astype(F32)
        for r in dh_refs[1:]:
            dh = dh + r[...].astype(F32)
        dx, dg = _rms_bwd(x_ref[...], g_ref[...], dh)
        dx_ref[...] = dres_ref[...] + dx
        _accum(dg_ref, dg, pl.program_id(0) == 0)

    row = pl.BlockSpec((tm, d), lambda i: (i, 0))
    vec = pl.BlockSpec((1, d), lambda i: (0, 0))
    return pl.pallas_call(
        body,
        name=name,
        grid=(s // tm,),
        in_specs=[row, vec] + [row] * n + [row],
        out_specs=[row, vec],
        out_shape=[jax.ShapeDtypeStruct((s, d), F32), jax.ShapeDtypeStruct((1, d), F32)],
        compiler_params=_cparams(("arbitrary",)),
    )(x, g.reshape(1, d), *dhs, dres)


def loss_head(x, g, target, *, name="loss_head"):
    s, d = x.shape
    tm = _pick(s, (ROW_TILE, 256, 128, 8))

    def body(x_ref, g_ref, t_ref, l_ref, dx_ref, dg_ref):
        x_, g_ = x_ref[...], g_ref[...]
        e = _rms_fwd(x_, g_) - t_ref[...]
        part = 0.5 * jnp.sum(jnp.sum(e * e, axis=-1, keepdims=True), axis=0, keepdims=True) / d
        dx, dg = _rms_bwd(x_, g_, e * (1.0 / d))
        dx_ref[...] = dx
        first = pl.program_id(0) == 0
        _accum(dg_ref, dg, first)
        _accum(l_ref, jnp.broadcast_to(part, (8, 128)), first)

    row = pl.BlockSpec((tm, d), lambda i: (i, 0))
    vec = pl.BlockSpec((1, d), lambda i: (0, 0))
    return pl.pallas_call(
        body,
        name=name,
        grid=(s // tm,),
        in_specs=[row, vec, row],
        out_specs=[pl.BlockSpec((8, 128), lambda i: (0, 0)), row, vec],
        out_shape=[jax.ShapeDtypeStruct((8, 128), F32), jax.ShapeDtypeStruct((s, d), F32), jax.ShapeDtypeStruct((1, d), F32)],
        compiler_params=_cparams(("arbitrary",)),
    )(x, g.reshape(1, d), target)


FF_TILE = 1024


def mlp_fwd(x, g, w_up, w_down, *, name="mlp_fwd"):
    s, d = x.shape
    ff = w_up.shape[1]
    tm, tf = _pick(s, (ROW_TILE, 256, 128)), _pick(ff, (FF_TILE, 512, 256, 128))
    nf = ff // tf

    def body(x_ref, g_ref, wu_ref, wd_ref, o_ref, h_ref, acc_ref):
        f = pl.program_id(1)

        @pl.when(f == 0)
        def _():
            h_ref[...] = _rms_fwd(x_ref[...], g_ref[...]).astype(BF16)
            acc_ref[...] = x_ref[...]

        up = jnp.maximum(_dot(h_ref[...], wu_ref[...]), 0.0)
        acc_ref[...] += _dot(up * up, wd_ref[...])

        @pl.when(f == nf - 1)
        def _():
            o_ref[...] = acc_ref[...]

    return pl.pallas_call(
        body,
        name=name,
        grid=(s // tm, nf),
        in_specs=[
            pl.BlockSpec((tm, d), lambda i, f: (i, 0)),
            pl.BlockSpec((1, d), lambda i, f: (0, 0)),
            pl.BlockSpec((d, tf), lambda i, f: (0, f)),
            pl.BlockSpec((tf, d), lambda i, f: (f, 0)),
        ],
        out_specs=pl.BlockSpec((tm, d), lambda i, f: (i, 0)),
        out_shape=jax.ShapeDtypeStruct((s, d), F32),
        scratch_shapes=[pltpu.VMEM((tm, d), BF16), pltpu.VMEM((tm, d), F32)],
        compiler_params=_cparams(("arbitrary", "arbitrary")),
    )(x, g.reshape(1, d), w_up, w_down)


def mlp_bwd(x, g, w_up, w_down, dy, *, name="mlp_bwd"):
    s, d = x.shape
    ff = w_up.shape[1]
    tm, tf = _pick(s, (ROW_TILE, 256, 128)), _pick(ff, (FF_TILE, 512, 256, 128))
    nf = ff // tf

    def body(x_ref, g_ref, wu_ref, wd_ref, dy_ref, dx_ref, dg_ref, h_ref, dup_ref, act_ref, dyb_ref, acc_ref):
        i, f = pl.program_id(0), pl.program_id(1)

        @pl.when(f == 0)
        def _():
            h_ref[...] = _rms_fwd(x_ref[...], g_ref[...]).astype(BF16)
            dyb_ref[...] = dy_ref[...].astype(BF16)
            acc_ref[...] = jnp.zeros_like(acc_ref)

        up = jnp.maximum(_dot(h_ref[...], wu_ref[...]), 0.0)
        dact = _dot(dyb_ref[...], wd_ref[...], 1, 1)
        dup = (2.0 * up * dact).astype(BF16)
        dup_ref[...] = dup
        act_ref[...] = (up * up).astype(BF16)
        acc_ref[...] += _dot(dup, wu_ref[...], 1, 1)

        @pl.when(f == nf - 1)
        def _():
            dx, dg = _rms_bwd(x_ref[...], g_ref[...], acc_ref[...])
            dx_ref[...] = dy_ref[...] + dx
            _accum(dg_ref, dg, i == 0)

    row = pl.BlockSpec((tm, d), lambda i, f: (i, 0))
    vec = pl.BlockSpec((1, d), lambda i, f: (0, 0))
    wide = pl.BlockSpec((tm, tf), lambda i, f: (i, f))
    return pl.pallas_call(
        body,
        name=name,
        grid=(s // tm, nf),
        in_specs=[row, vec, pl.BlockSpec((d, tf), lambda i, f: (0, f)), pl.BlockSpec((tf, d), lambda i, f: (f, 0)), row],
        out_specs=[row, vec, row, wide, wide],
        out_shape=[
            jax.ShapeDtypeStruct((s, d), F32),
            jax.ShapeDtypeStruct((1, d), F32),
            jax.ShapeDtypeStruct((s, d), BF16),
            jax.ShapeDtypeStruct((s, ff), BF16),
            jax.ShapeDtypeStruct((s, ff), BF16),
        ],
        scratch_shapes=[pltpu.VMEM((tm, d), BF16), pltpu.VMEM((tm, d), F32)],
        compiler_params=_cparams(("arbitrary", "arbitrary")),
    )(x, g.reshape(1, d), w_up, w_down, dy)


X_HEADS = 4


def _softmax_rows(s):
    m = jnp.max(s, axis=-1, keepdims=True)
    e = jnp.exp(s - m)
    return e / jnp.sum(e, axis=-1, keepdims=True)


def cross_fwd(x, g, wq, wo, kv, *, name="cross_fwd"):
    s, d = x.shape
    ml = kv.shape[0]
    dh = d // X_HEADS
    tm = _pick(s, (ROW_TILE, 256, 128))
    scale = dh**-0.5

    def body(x_ref, g_ref, wq_ref, wo_ref, kv_ref, o_ref):
        x_ = x_ref[...]
        q = _dot(_rms_fwd(x_, g_ref[...]), wq_ref[...])
        outs = []
        for hd in range(X_HEADS):
            kh = kv_ref[:, hd * dh : (hd + 1) * dh]
            vh = kv_ref[:, d + hd * dh : d + (hd + 1) * dh]
            p = _softmax_rows(_dot(q[:, hd * dh : (hd + 1) * dh], kh, 1, 1) * scale)
            outs.append(_dot(p, vh))
        o_ref[...] = x_ + _dot(jnp.concatenate(outs, axis=-1), wo_ref[...])

    row = pl.BlockSpec((tm, d), lambda i: (i, 0))
    full = lambda shp: pl.BlockSpec(shp, lambda i: (0, 0))
    return pl.pallas_call(
        body,
        name=name,
        grid=(s // tm,),
        in_specs=[row, full((1, d)), full((d, d)), full((d, d)), full((ml, 2 * d))],
        out_specs=row,
        out_shape=jax.ShapeDtypeStruct((s, d), F32),
        compiler_params=_cparams(("arbitrary",)),
    )(x, g.reshape(1, d), wq, wo, kv)


def cross_bwd(x, g, wq, wo, kv, dy, *, name="cross_bwd"):
    s, d = x.shape
    ml = kv.shape[0]
    dh = d // X_HEADS
    tm = _pick(s, (ROW_TILE, 256, 128))
    scale = dh**-0.5

    def body(x_ref, g_ref, wq_ref, wo_ref, kv_ref, dy_ref, dx_ref, dg_ref, h_ref, dq_ref, o_ref, dkv_ref):
        first = pl.program_id(0) == 0
        x_, g_ = x_ref[...], g_ref[...]
        h = _rms_fwd(x_, g_).astype(BF16)
        h_ref[...] = h
        q = _dot(h, wq_ref[...])
        dy_ = dy_ref[...]
        do = _dot(dy_, wo_ref[...], 1, 1)
        outs, dqs, dks, dvs = [], [], [], []
        for hd in range(X_HEADS):
            sl = slice(hd * dh, (hd + 1) * dh)
            kh = kv_ref[:, sl]
            vh = kv_ref[:, d + hd * dh : d + (hd + 1) * dh]
            qh = q[:, sl]
            p = _softmax_rows(_dot(qh, kh, 1, 1) * scale)
            outs.append(_dot(p, vh))
            doh = do[:, sl]
            dp = _dot(doh, vh, 1, 1)
            ds = p * (dp - jnp.sum(dp * p, axis=-1, keepdims=True)) * scale
            dqs.append(_dot(ds, kh))
            dks.append(_dot(ds, qh, 0, 0))
            dvs.append(_dot(p, doh, 0, 0))
        o_ref[...] = jnp.concatenate(outs, axis=-1).astype(BF16)
        dq = jnp.concatenate(dqs, axis=-1).astype(BF16)
        dq_ref[...] = dq
        _accum(dkv_ref, jnp.concatenate(dks + dvs, axis=-1), first)
        dx, dg = _rms_bwd(x_, g_, _dot(dq, wq_ref[...], 1, 1))
        dx_ref[...] = dy_ + dx
        _accum(dg_ref, dg, first)

    row = pl.BlockSpec((tm, d), lambda i: (i, 0))
    full = lambda shp: pl.BlockSpec(shp, lambda i: (0, 0))
    rowb = jax.ShapeDtypeStruct((s, d), BF16)
    return pl.pallas_call(
        body,
        name=name,
        grid=(s // tm,),
        in_specs=[row, full((1, d)), full((d, d)), full((d, d)), full((ml, 2 * d)), row],
        out_specs=[row, full((1, d)), row, row, row, full((ml, 2 * d))],
        out_shape=[jax.ShapeDtypeStruct((s, d), F32), jax.ShapeDtypeStruct((1, d), F32), rowb, rowb, rowb,
                   jax.ShapeDtypeStruct((ml, 2 * d), F32)],
        compiler_params=_cparams(("arbitrary",)),
    )(x, g.reshape(1, d), wq, wo, kv, dy)


N_BRANCH = 3
MIX_W = 512
ATT_DH = 128
ATT_HG = 4
MERGE_TILE = 256


def _group_weights(l0, l1, l2):
    m = jnp.maximum(jnp.maximum(l0, l1), l2)
    e = [jnp.exp(l0 - m), jnp.exp(l1 - m), jnp.exp(l2 - m)]
    inv = 1.0 / (e[0] + e[1] + e[2])
    return [t * inv for t in e]


def _sigmoid(x):
    return 1.0 / (1.0 + jnp.exp(-x))


def merge_fwd(x, proj, ya, yb, outs, lses, wb, wout, *, name="merge_fwd"):
    s, d = x.shape
    tm = _pick(s, (MERGE_TILE, 128))

    def body(x_ref, g0, g1, g2, ya_ref, yb_ref, o0, o1, o2, l0, l1, l2, wb_ref, wo_ref, x1_ref, mg_ref, yc_ref):
        w = _group_weights(l0[...], l1[...], l2[...])
        yc = w[0] * o0[...] + w[1] * o1[...] + w[2] * o2[...]
        yc_ref[...] = yc.astype(BF16)
        merged = None
        for n, (y, gl) in enumerate(((ya_ref[...], g0), (yb_ref[...], g1), (yc, g2))):
            t = _sigmoid(gl[...]) * _dot(y, wb_ref[n])
            merged = t if merged is None else merged + t
        mb = merged.astype(BF16)
        mg_ref[...] = mb
        x1_ref[...] = x_ref[...] + _dot(mb, wo_ref[...])

    row = pl.BlockSpec((tm, d), lambda i: (i, 0))
    half = pl.BlockSpec((tm, MIX_W), lambda i: (i, 0))
    gate = [pl.BlockSpec((tm, d), lambda i, n=n: (i, n)) for n in range(N_BRANCH)]
    return pl.pallas_call(
        body,
        name=name,
        grid=(s // tm,),
        in_specs=[row] + gate + [half] * 8
        + [pl.BlockSpec((N_BRANCH, MIX_W, d), lambda i: (0, 0, 0)), pl.BlockSpec((d, d), lambda i: (0, 0))],
        out_specs=[row, row, half],
        out_shape=[jax.ShapeDtypeStruct((s, d), F32), jax.ShapeDtypeStruct((s, d), BF16), jax.ShapeDtypeStruct((s, MIX_W), BF16)],
        compiler_params=_cparams(("arbitrary",)),
    )(x, proj, proj, proj, ya, yb, *outs, *lses, wb, wout)


def merge_bwd(dx1, proj, ya, yb, yc, outs, lses, wb, wout, *, name="merge_bwd"):
    s, d = dx1.shape
    tm = _pick(s, (MERGE_TILE, 128))

    def body(dx_ref, g0, g1, g2, ya_ref, yb_ref, yc_ref, o0, o1, o2, l0, l1, l2, wb_ref, wo_ref,
             dgl_ref, dz0, dz1, dz2, dya_ref, dyb_ref, do0, do1, do2, dl0, dl1, dl2):
        dm = _dot(dx_ref[...], wo_ref[...], 1, 1)
        dys = []
        for n, (y, gl, dz_ref) in enumerate(((ya_ref, g0, dz0), (yb_ref, g1, dz1), (yc_ref, g2, dz2))):
            z = _dot(y[...], wb_ref[n])
            sg = _sigmoid(gl[...])
            dz = (dm * sg).astype(BF16)
            dz_ref[...] = dz
            dgl_ref[:, n * d : (n + 1) * d] = (dm * z * sg * (1.0 - sg)).astype(BF16)
            dys.append(_dot(dz, wb_ref[n], 1, 1))
        dya_ref[...] = dys[0]
        dyb_ref[...] = dys[1]
        dyc = dys[2]
        w = _group_weights(l0[...], l1[...], l2[...])
        o = [o0[...], o1[...], o2[...]]
        for gi, r in enumerate((do0, do1, do2)):
            r[...] = w[gi] * dyc
        for hd in range(ATT_HG):
            sl = slice(hd * ATT_DH, (hd + 1) * ATT_DH)
            t = [jnp.sum(dyc[:, sl] * o[gi][:, sl], axis=-1, keepdims=True) for gi in range(3)]
            wh = [w[gi][:, sl] for gi in range(3)]
            tbar = wh[0] * t[0] + wh[1] * t[1] + wh[2] * t[2]
            for gi, r in enumerate((dl0, dl1, dl2)):
                r[:, sl] = wh[gi] * (t[gi] - tbar)

    row = pl.BlockSpec((tm, d), lambda i: (i, 0))
    half = pl.BlockSpec((tm, MIX_W), lambda i: (i, 0))
    gate = [pl.BlockSpec((tm, d), lambda i, n=n: (i, n)) for n in range(N_BRANCH)]
    rb = jax.ShapeDtypeStruct((s, d), BF16)
    hf = jax.ShapeDtypeStruct((s, MIX_W), F32)
    return pl.pallas_call(
        body,
        name=name,
        grid=(s // tm,),
        in_specs=[row] + gate + [half] * 9
        + [pl.BlockSpec((N_BRANCH, MIX_W, d), lambda i: (0, 0, 0)), pl.BlockSpec((d, d), lambda i: (0, 0))],
        out_specs=[pl.BlockSpec((tm, N_BRANCH * d), lambda i: (i, 0)), row, row, row] + [half] * 8,
        out_shape=[jax.ShapeDtypeStruct((s, N_BRANCH * d), BF16), rb, rb, rb] + [hf] * 8,
        compiler_params=_cparams(("arbitrary",)),
    )(dx1, proj, proj, proj, ya, yb, yc, *outs, *lses, wb, wout)


ATT_BLOCK = 128
ATT_GROUPS = ((128, 1), (512, 4), (2048, 16))
N_ATT_HEADS = ATT_HG * len(ATT_GROUPS)
ALIBI_MAX_EXP = 8.0
MASKED = -1e30


def _att_slopes(group):
    return [2.0 ** (-ALIBI_MAX_EXP * (group * ATT_HG + h + 1) / N_ATT_HEADS) for h in range(ATT_HG)]


def _att_scores(q, kp, kc, slope_dil, has_prev):
    scale = ATT_DH**-0.5
    qi = lax.broadcasted_iota(jnp.int32, (ATT_BLOCK, ATT_BLOCK), 0)
    kj = lax.broadcasted_iota(jnp.int32, (ATT_BLOCK, ATT_BLOCK), 1)
    dist = qi - kj
    sc = _dot(q, kc, 1, 1) * scale - slope_dil * dist.astype(F32)
    sc = jnp.where(dist >= 0, sc, MASKED)
    sp = _dot(q, kp, 1, 1) * scale - slope_dil * (dist + ATT_BLOCK).astype(F32)
    sp = jnp.where(jnp.logical_and(dist <= 0, has_prev), sp, MASKED)
    return sp, sc


def attn_fwd(qkv, col, group, *, name="attn_fwd"):
    s = qkv.shape[0]
    window, dil = ATT_GROUPS[group]
    assert window // dil == ATT_BLOCK and s % (dil * ATT_BLOCK) == 0 and col % MIX_W == 0
    nb = s // dil // ATT_BLOCK
    nblk = s // ATT_BLOCK
    slopes = _att_slopes(group)
    c0 = col // MIX_W

    def body(q_ref, kp_ref, kc_ref, vp_ref, vc_ref, o_ref, l_ref):
        n = pl.program_id(0)
        has_prev = (n % nb) != 0
        for hd in range(ATT_HG):
            sl = slice(hd * ATT_DH, (hd + 1) * ATT_DH)
            sp, sc = _att_scores(q_ref[:, sl], kp_ref[:, sl], kc_ref[:, sl], slopes[hd] * dil, has_prev)
            m = jnp.maximum(jnp.max(sp, axis=-1, keepdims=True), jnp.max(sc, axis=-1, keepdims=True))
            ep, ec = jnp.exp(sp - m), jnp.exp(sc - m)
            den = jnp.sum(ep, axis=-1, keepdims=True) + jnp.sum(ec, axis=-1, keepdims=True)
            inv = 1.0 / den
            o_ref[:, sl] = _dot(ep * inv, vp_ref[:, sl]) + _dot(ec * inv, vc_ref[:, sl])
            l_ref[:, sl] = jnp.broadcast_to(m + jnp.log(den), (ATT_BLOCK, ATT_DH))

    cur = lambda c: pl.BlockSpec((ATT_BLOCK, MIX_W), lambda n, c=c: (n, c))
    prev = lambda c: pl.BlockSpec((ATT_BLOCK, MIX_W), lambda n, c=c: (jnp.maximum(n - 1, 0), c))
    blk = pl.BlockSpec((ATT_BLOCK, MIX_W), lambda n: (n, 0))
    return pl.pallas_call(
        body,
        name=name,
        grid=(nblk,),
        in_specs=[cur(c0), prev(c0 + 1), cur(c0 + 1), prev(c0 + 2), cur(c0 + 2)],
        out_specs=[blk, blk],
        out_shape=[jax.ShapeDtypeStruct((s, MIX_W), F32)] * 2,
        compiler_params=_cparams(("arbitrary",)),
    )(qkv, qkv, qkv, qkv, qkv)


def attn_bwd(qkv, col, group, out, lse, dout, dlse, *, name="attn_bwd"):
    s = qkv.shape[0]
    window, dil = ATT_GROUPS[group]
    nb = s // dil // ATT_BLOCK
    nblk = s // ATT_BLOCK
    slopes = _att_slopes(group)
    c0 = col // MIX_W
    scale = ATT_DH**-0.5

    def body(q_ref, kp_ref, kc_ref, vp_ref, vc_ref, o_ref, l_ref, do_ref, dl_ref, dq_ref, dk_ref, dv_ref, acck, accv):
        n = pl.program_id(0)
        live = n < nblk
        has_prev = jnp.logical_and((n % nb) != 0, live)

        @pl.when(n == 0)
        def _():
            acck[...] = jnp.zeros_like(acck)
            accv[...] = jnp.zeros_like(accv)

        for hd in range(ATT_HG):
            sl = slice(hd * ATT_DH, (hd + 1) * ATT_DH)
            q, kp, kc, vp, vc = q_ref[:, sl], kp_ref[:, sl], kc_ref[:, sl], vp_ref[:, sl], vc_ref[:, sl]
            do = do_ref[:, sl]
            sp, sc = _att_scores(q, kp, kc, slopes[hd] * dil, has_prev)
            l = l_ref[:, sl][:, :1]
            pp, pc = jnp.exp(sp - l), jnp.exp(sc - l)
            corr = dl_ref[:, sl][:, :1] - jnp.sum(do * o_ref[:, sl], axis=-1, keepdims=True)
            dsp = pp * (_dot(do, vp, 1, 1) + corr) * scale
            dsc = pc * (_dot(do, vc, 1, 1) + corr) * scale
            dqh = (_dot(dsp, kp) + _dot(dsc, kc)).astype(BF16)

            @pl.when(live)
            def _(dqh=dqh, sl=sl):
                dq_ref[:, sl] = dqh

            dk_ref[:, sl] = (acck[:, sl] + _dot(dsp, q, 0, 0)).astype(BF16)
            dv_ref[:, sl] = (accv[:, sl] + _dot(pp, do, 0, 0)).astype(BF16)
            acck[:, sl] = jnp.where(live, _dot(dsc, q, 0, 0), 0.0)
            accv[:, sl] = jnp.where(live, _dot(pc, do, 0, 0), 0.0)

    last = nblk - 1
    cur = lambda c: pl.BlockSpec((ATT_BLOCK, MIX_W), lambda n, c=c: (jnp.minimum(n, last), c))
    prev = lambda c: pl.BlockSpec((ATT_BLOCK, MIX_W), lambda n, c=c: (jnp.clip(n - 1, 0, last), c))
    sds = jax.ShapeDtypeStruct((s, MIX_W), BF16)
    return pl.pallas_call(
        body,
        name=name,
        grid=(nblk + 1,),
        in_specs=[cur(c0), prev(c0 + 1), cur(c0 + 1), prev(c0 + 2), cur(c0 + 2), cur(0), cur(0), cur(0), cur(0)],
        out_specs=[cur(0), prev(0), prev(0)],
        out_shape=[sds, sds, sds],
        scratch_shapes=[pltpu.VMEM((ATT_BLOCK, MIX_W), F32), pltpu.VMEM((ATT_BLOCK, MIX_W), F32)],
        compiler_params=_cparams(("arbitrary",)),
    )(qkv, qkv, qkv, qkv, qkv, out, lse, dout, dlse)


GLA_HEADS = 4
GLA_DK = 64
GLA_DV = 128
GLA_CHUNK = 64
GLA_TAU = 16.0
GLA_QK = GLA_HEADS * GLA_DK
LR_PAD = 128


def _dot_exact(a, b, ca=1, cb=0):
    return lax.dot_general(a, b, (((ca,), (cb,)), ((), ())), precision=lax.Precision.HIGHEST, preferred_element_type=F32)


def _gla_chunk(q, k, v, lr, r, state, wg, bg, go):
    c = GLA_CHUNK
    z = _dot(lr, wg) + bg
    la = (jnp.minimum(z, 0.0) - jnp.log(1.0 + jnp.exp(-jnp.abs(z)))) * (1.0 / GLA_TAU)
    ri = lax.broadcasted_iota(jnp.int32, (c, c), 0)
    ci = lax.broadcasted_iota(jnp.int32, (c, c), 1)
    causal = ri >= ci
    b = _dot_exact(causal.astype(F32), la)
    bmid = b[c // 2 : c // 2 + 1, :]
    bend = b[c - 1 : c, :]
    qs = q * GLA_DK**-0.5
    q_in = qs * jnp.exp(b)
    q_mid = qs * jnp.exp(b - bmid)
    k_mid = k * jnp.exp(bmid - b)
    k_end = k * jnp.exp(bend - b)
    ys, upd = [], []
    for h in range(GLA_HEADS):
        sk = slice(h * GLA_DK, (h + 1) * GLA_DK)
        sv = slice(h * GLA_DV, (h + 1) * GLA_DV)
        vh = v[:, sv]
        inter = _dot(q_in[:, sk], state[:, sk], 1, 1)
        sc = jnp.where(causal, _dot_exact(q_mid[:, sk], k_mid[:, sk], 1, 1), 0.0)
        o = inter + _dot(sc, vh)
        o = o * lax.rsqrt(jnp.mean(o * o, axis=-1, keepdims=True) + RMS_EPS) * go
        rh = r[:, sv]
        ys.append(o * rh * _sigmoid(rh))
        upd.append(_dot(vh, k_end[:, sk], 0, 0))
    new_state = jnp.exp(bend) * state + jnp.concatenate(upd, axis=-1)
    return jnp.concatenate(ys, axis=-1), new_state


def _gla_in_specs(cols, rev, nc):
    c = GLA_CHUNK
    row = (lambda i: nc - 1 - i) if rev else (lambda i: i)
    qc, kc, vc, lc, rc = cols
    assert qc % GLA_QK == 0 and kc % GLA_QK == 0 and vc % MIX_W == 0 and rc % MIX_W == 0 and lc % LR_PAD == 0
    return [
        pl.BlockSpec((c, GLA_QK), lambda i: (row(i), qc // GLA_QK)),
        pl.BlockSpec((c, GLA_QK), lambda i: (row(i), kc // GLA_QK)),
        pl.BlockSpec((c, MIX_W), lambda i: (row(i), vc // MIX_W)),
        pl.BlockSpec((c, LR_PAD), lambda i: (row(i), lc // LR_PAD)),
        pl.BlockSpec((c, MIX_W), lambda i: (row(i), rc // MIX_W)),
    ], row


def gla_fwd(proj, cols, wg, bg, go, *, name="gla_fwd"):
    s = proj.shape[0]
    nc = s // GLA_CHUNK
    specs, row = _gla_in_specs(cols, False, nc)

    def body(q_ref, k_ref, v_ref, lr_ref, r_ref, wg_ref, bg_ref, go_ref, y_ref, st_ref, state):
        @pl.when(pl.program_id(0) == 0)
        def _():
            state[...] = jnp.zeros_like(state)

        st = state[...]
        st_ref[...] = st
        y, new = _gla_chunk(q_ref[...], k_ref[...], v_ref[...], lr_ref[...], r_ref[...], st, wg_ref[...], bg_ref[...], go_ref[...])
        y_ref[...] = y
        state[...] = new

    full = lambda shp: pl.BlockSpec(shp, lambda i: (0, 0))
    return pl.pallas_call(
        body,
        name=name,
        grid=(nc,),
        in_specs=specs + [full((LR_PAD, GLA_QK)), full((1, GLA_QK)), full((1, GLA_DV))],
        out_specs=[pl.BlockSpec((GLA_CHUNK, MIX_W), lambda i: (i, 0)), pl.BlockSpec((GLA_DV, GLA_QK), lambda i: (i, 0))],
        out_shape=[jax.ShapeDtypeStruct((s, MIX_W), F32), jax.ShapeDtypeStruct((nc * GLA_DV, GLA_QK), F32)],
        scratch_shapes=[pltpu.VMEM((GLA_DV, GLA_QK), F32)],
        compiler_params=_cparams(("arbitrary",)),
    )(proj, proj, proj, proj, proj, wg, bg, go)


def gla_bwd(proj, cols, wg, bg, go, states, dy, *, name="gla_bwd"):
    s = proj.shape[0]
    nc = s // GLA_CHUNK
    specs, row = _gla_in_specs(cols, True, nc)

    def body(q_ref, k_ref, v_ref, lr_ref, r_ref, wg_ref, bg_ref, go_ref, st_ref, dy_ref,
             dq_ref, dk_ref, dv_ref, dlr_ref, dr_ref, dwg_ref, dbg_ref, dgo_ref, dstate):
        first = pl.program_id(0) == 0

        @pl.when(first)
        def _():
            dstate[...] = jnp.zeros_like(dstate)

        _, vjp = jax.vjp(_gla_chunk, q_ref[...], k_ref[...], v_ref[...], lr_ref[...], r_ref[...], st_ref[...],
                         wg_ref[...].astype(F32), bg_ref[...], go_ref[...])
        dq, dk, dv, dlr, dr, dst, dwg, dbg, dgo = vjp((dy_ref[...], dstate[...]))
        dq_ref[...] = dq.astype(BF16)
        dk_ref[...] = dk.astype(BF16)
        dv_ref[...] = dv.astype(BF16)
        dlr_ref[...] = dlr.astype(BF16)
        dr_ref[...] = dr.astype(BF16)
        dstate[...] = dst
        _accum(dwg_ref, dwg, first)
        _accum(dbg_ref, dbg, first)
        _accum(dgo_ref, dgo, first)

    c = GLA_CHUNK
    full = lambda shp: pl.BlockSpec(shp, lambda i: (0, 0))
    rows = lambda w: pl.BlockSpec((c, w), lambda i: (row(i), 0))
    return pl.pallas_call(
        body,
        name=name,
        grid=(nc,),
        in_specs=specs + [full((LR_PAD, GLA_QK)), full((1, GLA_QK)), full((1, GLA_DV)),
                          pl.BlockSpec((GLA_DV, GLA_QK), lambda i: (row(i), 0)), rows(MIX_W)],
        out_specs=[rows(GLA_QK), rows(GLA_QK), rows(MIX_W), rows(LR_PAD), rows(MIX_W),
                   full((LR_PAD, GLA_QK)), full((1, GLA_QK)), full((1, GLA_DV))],
        out_shape=[jax.ShapeDtypeStruct((s, GLA_QK), BF16), jax.ShapeDtypeStruct((s, GLA_QK), BF16),
                   jax.ShapeDtypeStruct((s, MIX_W), BF16), jax.ShapeDtypeStruct((s, LR_PAD), BF16),
                   jax.ShapeDtypeStruct((s, MIX_W), BF16), jax.ShapeDtypeStruct((LR_PAD, GLA_QK), F32),
                   jax.ShapeDtypeStruct((1, GLA_QK), F32), jax.ShapeDtypeStruct((1, GLA_DV), F32)],
        scratch_shapes=[pltpu.VMEM((GLA_DV, GLA_QK), F32)],
        compiler_params=_cparams(("arbitrary",)),
    )(proj, proj, proj, proj, proj, wg, bg, go, states, dy)


S5_G = 32
S5_P = 64
S5_C = 16
S5_N = S5_G * S5_P
S5_TC = 256
SUB = 8


def _s5_disc(a_re, a_im, ls, b_re, b_im):
    step = jnp.exp(ls)
    mag = jnp.exp(a_re * step)
    lr, li = mag * jnp.cos(a_im * step), mag * jnp.sin(a_im * step)
    inv = 1.0 / (a_re * a_re + a_im * a_im)
    nr, ni = lr - 1.0, li
    cr, ci = (nr * a_re + ni * a_im) * inv, (ni * a_re - nr * a_im) * inv
    return lr, li, cr * b_re - ci * b_im, cr * b_im + ci * b_re


def s5_disc_fwd(a_re, a_im, ls, b_re, b_im, *, name="s5_disc"):
    def body(ar, ai, l, br, bi, o0, o1, o2, o3):
        for o, val in zip((o0, o1, o2, o3), _s5_disc(ar[...], ai[...], l[...], br[...], bi[...])):
            o[...] = val

    sds = jax.ShapeDtypeStruct(a_re.shape, F32)
    return pl.pallas_call(body, name=name, out_shape=[sds] * 4)(a_re, a_im, ls, b_re, b_im)


def s5_disc_bwd(a_re, a_im, ls, b_re, b_im, cts, *, name="s5_disc_bwd"):
    def body(ar, ai, l, br, bi, c0, c1, c2, c3, dar, dai, dl, dbr, dbi):
        _, vjp = jax.vjp(_s5_disc, ar[...], ai[...], l[...], br[...], bi[...])
        g = vjp((c0[...], c1[...], c2[...], c3[...]))
        for o, val in zip((dar, dai, dl), g[:3]):
            o[...] = jnp.sum(val, axis=-1, keepdims=True)
        dbr[...] = g[3]
        dbi[...] = g[4]

    col = jax.ShapeDtypeStruct((a_re.shape[0], 1), F32)
    sds = jax.ShapeDtypeStruct(a_re.shape, F32)
    return pl.pallas_call(body, name=name, out_shape=[col, col, col, sds, sds])(a_re, a_im, ls, b_re, b_im, *cts)


def _gelu(y):
    return 0.5 * y * (1.0 + jnp.tanh(0.7978845608028654 * (y + 0.044715 * (y * y * y))))


def _s5_powers(lam, conj):
    lr, li = lam[:, :S5_N], lam[:, S5_N:]
    if conj:
        li = -li
    rows, pr, pi = [], lr, li
    for _ in range(SUB):
        rows.append((pr, pi))
        pr, pi = pr * lr - pi * li, pr * li + pi * lr
    return rows


def _s5_table(rows, reverse):
    ridx = lax.broadcasted_iota(jnp.int32, (SUB, S5_N), 0)
    tr = jnp.zeros((SUB, S5_N), F32)
    ti = jnp.zeros((SUB, S5_N), F32)
    for i in range(SUB):
        pr, pi = rows[SUB - 1 - i] if reverse else rows[i]
        tr = jnp.where(ridx == i, pr, tr)
        ti = jnp.where(ridx == i, pi, ti)
    return tr, ti


def _s5_scan(buf, lam, carry_ref, reverse):
    tc = buf.shape[0]
    nblk = tc // SUB
    rows = _s5_powers(lam, reverse)
    tr, ti = _s5_table(rows, reverse)
    ridx = lax.broadcasted_iota(jnp.int32, (SUB, S5_N), 0)

    def block(j, carry):
        jj = (nblk - 1 - j) if reverse else j
        at = pl.ds(pl.multiple_of(jj * SUB, SUB), SUB)
        re, im = buf[at, :S5_N], buf[at, S5_N:]
        for sft, (pr, pi) in ((1, rows[0]), (2, rows[1]), (4, rows[3])):
            if reverse:
                keep = ridx < SUB - sft
                sre, sim = pltpu.roll(re, SUB - sft, 0), pltpu.roll(im, SUB - sft, 0)
            else:
                keep = ridx >= sft
                sre, sim = pltpu.roll(re, sft, 0), pltpu.roll(im, sft, 0)
            sre, sim = jnp.where(keep, sre, 0.0), jnp.where(keep, sim, 0.0)
            re, im = re + pr * sre - pi * sim, im + pr * sim + pi * sre
        cr, ci = carry
        re, im = re + tr * cr - ti * ci, im + tr * ci + ti * cr
        buf[at, :S5_N] = re
        buf[at, S5_N:] = im
        edge = 0 if reverse else SUB - 1
        return re[edge : edge + 1, :], im[edge : edge + 1, :]

    c0 = (carry_ref[0:1, :S5_N], carry_ref[0:1, S5_N:])
    cr, ci = lax.fori_loop(0, nblk, block, c0)
    carry_ref[:, :S5_N] = jnp.broadcast_to(cr, (SUB, S5_N))
    carry_ref[:, S5_N:] = jnp.broadcast_to(ci, (SUB, S5_N))


def s5_fwd(proj, ucol, bd, cd, lam, dskip, wglu, bglu, *, name="s5_fwd"):
    s = proj.shape[0]
    tc = _pick(s, (S5_TC, 128, 64, 8))
    assert ucol % MIX_W == 0

    def body(u_ref, bd_ref, cd_ref, lam_ref, d_ref, w_ref, b_ref, y_ref, xs_ref, carry):
        @pl.when(pl.program_id(0) == 0)
        def _():
            carry[...] = jnp.zeros_like(carry)

        u = u_ref[...]
        xs_ref[...] = _dot(u, bd_ref[...])
        _s5_scan(xs_ref, lam_ref[...], carry, False)
        g = _gelu(_dot(xs_ref[...], cd_ref[...]) + d_ref[...] * u)
        y_ref[...] = g * _sigmoid(_dot(g, w_ref[...]) + b_ref[...])

    full = lambda shp: pl.BlockSpec(shp, lambda i: (0, 0))
    return pl.pallas_call(
        body,
        name=name,
        grid=(s // tc,),
        in_specs=[pl.BlockSpec((tc, MIX_W), lambda i: (i, ucol // MIX_W)), full((MIX_W, 2 * S5_N)), full((2 * S5_N, MIX_W)),
                  full((1, 2 * S5_N)), full((1, MIX_W)), full((MIX_W, MIX_W)), full((1, MIX_W))],
        out_specs=[pl.BlockSpec((tc, MIX_W), lambda i: (i, 0)), pl.BlockSpec((tc, 2 * S5_N), lambda i: (i, 0))],
        out_shape=[jax.ShapeDtypeStruct((s, MIX_W), F32), jax.ShapeDtypeStruct((s, 2 * S5_N), F32)],
        scratch_shapes=[pltpu.VMEM((SUB, 2 * S5_N), F32)],
        compiler_params=_cparams(("arbitrary",)),
    )(proj, bd, cd, lam, dskip, wglu, bglu)


def s5_bwd(proj, ucol, xs, bd, cd, lam, dskip, wglu, bglu, dya, *, name="s5_bwd"):
    s = proj.shape[0]
    tc = _pick(s, (S5_TC, 128, 64, 8))
    nch = s // tc
    per = tc // SUB

    def body(u_ref, xs_ref, xp_ref, bd_ref, cd_ref, lam_ref, d_ref, w_ref, b_ref, dya_ref,
             du_ref, adj_ref, g_ref, dpre_ref, dy_ref, dlam_ref, dd_ref, db_ref, buf, carry, lacc):
        i = pl.program_id(0)
        first = i == 0

        @pl.when(first)
        def _():
            carry[...] = jnp.zeros_like(carry)
            lacc[...] = jnp.zeros_like(lacc)

        u, x, dya_ = u_ref[...], xs_ref[...], dya_ref[...]
        y = _dot(x, cd_ref[...]) + d_ref[...] * u
        g, gelu_vjp = jax.vjp(_gelu, y)
        sg = _sigmoid(_dot(g, w_ref[...]) + b_ref[...])
        dpre = dya_ * g * sg * (1.0 - sg)
        (dy,) = gelu_vjp(dya_ * sg + _dot(dpre, w_ref[...], 1, 1))
        g_ref[...] = g.astype(BF16)
        dpre_ref[...] = dpre.astype(BF16)
        dy_ref[...] = dy.astype(BF16)
        _accum(db_ref, jnp.sum(dpre, axis=0, keepdims=True), first)
        _accum(dd_ref, jnp.sum(dy * u, axis=0, keepdims=True), first)
        buf[...] = _dot(dy, cd_ref[...], 1, 1)
        _s5_scan(buf, lam_ref[...], carry, True)
        a = buf[...]
        adj_ref[...] = a.astype(BF16)
        du_ref[...] = (dy * d_ref[...] + _dot(a, bd_ref[...], 1, 1)).astype(BF16)
        before = jnp.where(i == nch - 1, 0.0, xp_ref[SUB - 1 : SUB, :])
        ridx = lax.broadcasted_iota(jnp.int32, (tc, 2 * S5_N), 0)
        xprev = jnp.where(ridx == 0, before, pltpu.roll(x, 1, 0))
        ar, ai, xr, xi = a[:, :S5_N], a[:, S5_N:], xprev[:, :S5_N], xprev[:, S5_N:]
        lacc[:, :S5_N] += jnp.sum((ar * xr + ai * xi).reshape(per, SUB, S5_N), axis=0)
        lacc[:, S5_N:] += jnp.sum((ai * xr - ar * xi).reshape(per, SUB, S5_N), axis=0)

        @pl.when(i == nch - 1)
        def _():
            dlam_ref[...] = jnp.sum(lacc[...], axis=0, keepdims=True)

    rev = lambda i: nch - 1 - i
    full = lambda shp: pl.BlockSpec(shp, lambda i: (0, 0))
    rows = lambda w: pl.BlockSpec((tc, w), lambda i: (rev(i), 0))
    hb = jax.ShapeDtypeStruct((s, MIX_W), BF16)
    return pl.pallas_call(
        body,
        name=name,
        grid=(nch,),
        in_specs=[pl.BlockSpec((tc, MIX_W), lambda i: (rev(i), ucol // MIX_W)), rows(2 * S5_N),
                  pl.BlockSpec((SUB, 2 * S5_N), lambda i: (jnp.maximum(rev(i) * per - 1, 0), 0)),
                  full((MIX_W, 2 * S5_N)), full((2 * S5_N, MIX_W)), full((1, 2 * S5_N)), full((1, MIX_W)),
                  full((MIX_W, MIX_W)), full((1, MIX_W)), rows(MIX_W)],
        out_specs=[rows(MIX_W), rows(2 * S5_N), rows(MIX_W), rows(MIX_W), rows(MIX_W),
                   full((1, 2 * S5_N)), full((1, MIX_W)), full((1, MIX_W))],
        out_shape=[hb, jax.ShapeDtypeStruct((s, 2 * S5_N), BF16), hb, hb, hb,
                   jax.ShapeDtypeStruct((1, 2 * S5_N), F32), jax.ShapeDtypeStruct((1, MIX_W), F32),
                   jax.ShapeDtypeStruct((1, MIX_W), F32)],
        scratch_shapes=[pltpu.VMEM((tc, 2 * S5_N), F32), pltpu.VMEM((SUB, 2 * S5_N), F32), pltpu.VMEM((SUB, 2 * S5_N), F32)],
        compiler_params=_cparams(("arbitrary",)),
    )(proj, xs, xs, bd, cd, lam, dskip, wglu, bglu, dya)


def _bcast16(a):
    return jnp.broadcast_to(a.reshape(S5_N, 1), (S5_N, S5_C))


def _blockdiag(blocks):
    g, r, c = blocks.shape
    eye = jnp.eye(g, dtype=blocks.dtype)
    return (eye[:, None, :, None] * blocks[:, :, None, :]).reshape(g * r, g * c)


def _blockdiag_extract(dense, r, c):
    g = dense.shape[0] // r
    return jnp.einsum("grgc->grc", dense.reshape(g, r, g, c))


def s5_prepare(a_re, a_im, log_step, b_re, b_im, c_re, c_im):
    disc_in = (_bcast16(a_re), _bcast16(a_im), _bcast16(jnp.broadcast_to(log_step[:, None], (S5_G, S5_P))),
               b_re.reshape(S5_N, S5_C), b_im.reshape(S5_N, S5_C))
    lr, li, bbr, bbi = s5_disc_fwd(*disc_in)
    lam = jnp.concatenate([lr[:, 0], li[:, 0]]).reshape(1, 2 * S5_N)
    to_blocks = lambda t: _blockdiag(t.reshape(S5_G, S5_P, S5_C).transpose(0, 2, 1))
    bd = jnp.concatenate([to_blocks(bbr), to_blocks(bbi)], axis=1).astype(BF16)
    cd = jnp.concatenate([_blockdiag(c_re.transpose(0, 2, 1)), -_blockdiag(c_im.transpose(0, 2, 1))], axis=0).astype(BF16)
    return disc_in, lam, bd, cd


def s5_param_grads(disc_in, dlam, dbd, dcd):
    first_col = lambda v: jnp.pad(v.reshape(S5_N, 1), ((0, 0), (0, S5_C - 1)))
    from_blocks = lambda t: _blockdiag_extract(t, S5_C, S5_P).transpose(0, 2, 1).reshape(S5_N, S5_C)
    cts = (first_col(dlam[0, :S5_N]), first_col(dlam[0, S5_N:]), from_blocks(dbd[:, :S5_N]), from_blocks(dbd[:, S5_N:]))
    dar, dai, dls, dbr, dbi = s5_disc_bwd(*disc_in, cts)
    dcr = _blockdiag_extract(dcd[:S5_N], S5_P, S5_C).transpose(0, 2, 1)
    dci = -_blockdiag_extract(dcd[S5_N:], S5_P, S5_C).transpose(0, 2, 1)
    return (dar.reshape(S5_G, S5_P), dai.reshape(S5_G, S5_P), dls.reshape(S5_G, S5_P).sum(axis=1),
            dbr.reshape(S5_G, S5_P, S5_C), dbi.reshape(S5_G, S5_P, S5_C), dcr, dci)


N_DEV = 8
MESH_ID = pl.DeviceIdType.MESH
ANY = pl.BlockSpec(memory_space=pl.ANY)


def _me():
    return lax.axis_index("x"), lax.axis_index("y"), lax.axis_index("c")


def all_gather(blocks, *, by_core=False, name="all_gather"):
    na = len(blocks)
    shapes = [b.shape[1:] if by_core else b.shape for b in blocks]

    def body(*refs):
        ins, outs = refs[:na], refs[na : 2 * na]
        send_sems, recv_sems, local_sems = refs[2 * na :]
        x, y, c = _me()
        me, sibling = (x, y, c), (x, y, 1 - c)
        chips = [(1 - x, y), (x, 1 - y), (1 - x, 1 - y)]
        slot = lambda p: 4 * p[0] + 2 * p[1] + p[2]

        def copy(a, k, block, to, src=None):
            dst = outs[a].at[slot(block)]
            return pltpu.make_async_remote_copy(src_ref=dst if src is None else src, dst_ref=dst, send_sem=send_sems.at[a * 7 + k],
                                                recv_sem=recv_sems.at[a * 7 + k], device_id=to, device_id_type=MESH_ID)

        mine = [ins[a].at[c] if by_core else ins[a] for a in range(na)]
        local = [pltpu.make_async_copy(mine[a], outs[a].at[slot(me)], local_sems.at[a]) for a in range(na)]
        sends = []
        for a in range(na):
            local[a].start()
            first = [copy(a, 0, me, sibling, src=mine[a])]
            first += [copy(a, 1 + j, me, (*chip, c), src=mine[a]) for j, chip in enumerate(chips)]
            for cp in first:
                cp.start()
            sends += first
        for a in range(na):
            for j, chip in enumerate(chips):
                copy(a, 1 + j, (*chip, c), me).wait_recv()
                fwd = copy(a, 4 + j, (*chip, c), sibling)
                fwd.start()
                sends.append(fwd)
        for a in range(na):
            copy(a, 0, sibling, me).wait_recv()
            for j, chip in enumerate(chips):
                copy(a, 4 + j, (*chip, 1 - c), me).wait_recv()
        for cp in sends:
            cp.wait_send()
        for cp in local:
            cp.wait()

    return pl.pallas_call(
        body,
        name=name,
        in_specs=[ANY] * na,
        out_specs=[ANY] * na,
        out_shape=[jax.ShapeDtypeStruct((N_DEV, *shp), b.dtype) for shp, b in zip(shapes, blocks)],
        scratch_shapes=[pltpu.SemaphoreType.DMA((na * 7,)), pltpu.SemaphoreType.DMA((na * 7,)), pltpu.SemaphoreType.DMA((na,))],
    )(*blocks)


def all_to_all(arrs, *, name="all_to_all"):
    na = len(arrs)

    def body(*refs):
        ins, outs = refs[:na], refs[na : 2 * na]
        send_sems, recv_sems, local_sems = refs[2 * na :]
        x, y, c = _me()
        my = 4 * x + 2 * y + c
        peers = []
        for k in range(1, N_DEV):
            fx, fy, fc = (k >> 2) & 1, (k >> 1) & 1, k & 1
            px, py, pc = (x + fx) % 2, (y + fy) % 2, (c + fc) % 2
            peers.append(((px, py, pc), 4 * px + 2 * py + pc))
        local = [pltpu.make_async_copy(ins[a].at[my], outs[a].at[my], local_sems.at[a]) for a in range(na)]
        sends = []
        for a in range(na):
            local[a].start()
            for k, (peer, pidx) in enumerate(peers):
                cp = pltpu.make_async_remote_copy(src_ref=ins[a].at[pidx], dst_ref=outs[a].at[my], send_sem=send_sems.at[a * 7 + k],
                                                  recv_sem=recv_sems.at[a * 7 + k], device_id=peer, device_id_type=MESH_ID)
                cp.start()
                sends.append(cp)
        for a in range(na):
            for k, (peer, pidx) in enumerate(peers):
                pltpu.make_async_remote_copy(src_ref=ins[a].at[my], dst_ref=outs[a].at[pidx], send_sem=send_sems.at[a * 7 + k],
                                             recv_sem=recv_sems.at[a * 7 + k], device_id=peer, device_id_type=MESH_ID).wait_recv()
        for cp in sends:
            cp.wait_send()
        for cp in local:
            cp.wait()

    return pl.pallas_call(
        body,
        name=name,
        in_specs=[ANY] * na,
        out_specs=[ANY] * na,
        out_shape=[jax.ShapeDtypeStruct(a.shape, a.dtype) for a in arrs],
        scratch_shapes=[pltpu.SemaphoreType.DMA((na * 7,)), pltpu.SemaphoreType.DMA((na * 7,)), pltpu.SemaphoreType.DMA((na,))],
    )(*arrs)


def pair_by_core(arrs, *, name="pair_by_core"):
    na = len(arrs)

    def body(*refs):
        ins, outs = refs[:na], refs[na : 2 * na]
        send_sems, recv_sems, local_sems = refs[2 * na :]
        x, y, c = _me()
        local, sends = [], []
        for a in range(na):
            lc = pltpu.make_async_copy(ins[a], outs[a].at[c], local_sems.at[a])
            lc.start()
            local.append(lc)
            cp = pltpu.make_async_remote_copy(src_ref=ins[a], dst_ref=outs[a].at[c], send_sem=send_sems.at[a],
                                              recv_sem=recv_sems.at[a], device_id=(x, y, 1 - c), device_id_type=MESH_ID)
            cp.start()
            sends.append(cp)
        for a in range(na):
            pltpu.make_async_remote_copy(src_ref=ins[a], dst_ref=outs[a].at[1 - c], send_sem=send_sems.at[a],
                                         recv_sem=recv_sems.at[a], device_id=(x, y, 1 - c), device_id_type=MESH_ID).wait_recv()
        for cp in sends:
            cp.wait_send()
        for cp in local:
            cp.wait()

    return pl.pallas_call(
        body,
        name=name,
        in_specs=[ANY] * na,
        out_specs=[ANY] * na,
        out_shape=[jax.ShapeDtypeStruct((2, *a.shape), a.dtype) for a in arrs],
        scratch_shapes=[pltpu.SemaphoreType.DMA((na,)), pltpu.SemaphoreType.DMA((na,)), pltpu.SemaphoreType.DMA((na,))],
    )(*arrs)


def _row_tile(rows, row_bytes, budget):
    best = None
    for t in range(8, rows + 1, 8):
        if rows % t == 0 and t * row_bytes <= budget:
            best = t
    return best or rows


def sum_blocks(a, *, name="sum_blocks"):
    n, r, c = a.shape
    tm = _row_tile(r, n * c * a.dtype.itemsize, 4 << 20)

    def body(a_ref, o_ref):
        acc = a_ref[0].astype(F32)
        for k in range(1, n):
            acc = acc + a_ref[k].astype(F32)
        o_ref[...] = acc

    return pl.pallas_call(
        body,
        name=name,
        grid=(r // tm,),
        in_specs=[pl.BlockSpec((n, tm, c), lambda i: (0, i, 0))],
        out_specs=pl.BlockSpec((tm, c), lambda i: (i, 0)),
        out_shape=jax.ShapeDtypeStruct((r, c), F32),
        compiler_params=_cparams(("arbitrary",)),
    )(a)


ADAM_LR = 0.001
ADAM_B1 = 0.9
ADAM_B2 = 0.999
ADAM_EPS = 1e-08
ADAM_WD = 0.01
ADAM_STEP = 10


def adamw(w, g, m, v, *, name="adamw"):
    r, c = w.shape
    tm = _row_tile(r, c * 4, 1 << 20)

    def body(w_ref, g_ref, m_ref, v_ref, d_ref, nm_ref, nv_ref):
        g_ = g_ref[...]
        m_ = ADAM_B1 * m_ref[...] + (1.0 - ADAM_B1) * g_
        v_ = ADAM_B2 * v_ref[...] + (1.0 - ADAM_B2) * (g_ * g_)
        m_hat = m_ / (1.0 - ADAM_B1**ADAM_STEP)
        v_hat = v_ / (1.0 - ADAM_B2**ADAM_STEP)
        d_ref[...] = -ADAM_LR * (m_hat / (jnp.sqrt(v_hat) + ADAM_EPS) + ADAM_WD * w_ref[...])
        nm_ref[...] = m_
        nv_ref[...] = v_

    blk = pl.BlockSpec((tm, c), lambda i: (i, 0))
    sds = jax.ShapeDtypeStruct((r, c), F32)
    return pl.pallas_call(
        body, name=name, grid=(r // tm,), in_specs=[blk] * 4, out_specs=[blk] * 3, out_shape=[sds] * 3,
        compiler_params=_cparams(("arbitrary",)),
    )(w, g, m, v)


WEIGHTS = ["g_mix", "w_in", "s5_a_re", "s5_a_im", "s5_log_step", "s5_b_re", "s5_b_im", "s5_c_re", "s5_c_im", "s5_d", "w_glu",
           "b_glu", "w_gla_gate", "b_gla_gate", "g_gla_out", "w_branch", "w_out", "g_mem", "g_cross", "w_xq", "w_xkv", "w_xo",
           "g_mlp", "w_up", "w_down", "g_final"]
SHARDED = {"w_in": 1, "w_glu": 0, "w_branch": 2, "w_out": 0, "w_xq": 0, "w_xkv": 1, "w_xo": 0, "w_up": 1, "w_down": 0}
DEPTH = 2
N_CHIPS = 4
D_IN = 9744
C_U, C_QG, C_KG, C_VG, C_LR, C_RG, C_QA, C_KA, C_VA, C_GATE = 0, 512, 768, 1024, 1536, 1552, 2064, 3600, 5136, 6672
A_GATE, A_U, A_QG, A_KG, A_VG, A_RG, A_Q0, A_LR, A_W = 0, 3072, 3584, 3840, 4096, 4608, 5120, 6656, 7168
GLA_COLS = (A_QG, A_KG, A_VG, A_LR, A_RG)


def _split_w_in(w):
    att = lambda g: [w[:, c + MIX_W * g : c + MIX_W * (g + 1)] for c in (C_QA, C_KA, C_VA)]
    a = jnp.concatenate([w[:, C_GATE:D_IN], w[:, C_U:C_QG], w[:, C_QG:C_KG], w[:, C_KG:C_VG], w[:, C_VG:C_LR], w[:, C_RG:C_QA],
                         *att(0), w[:, C_LR:C_RG], jnp.zeros((w.shape[0], A_W - A_LR - 16), w.dtype)], axis=1)
    return a, jnp.concatenate(att(1), axis=1), jnp.concatenate(att(2), axis=1)


def _join_w_in(a, b, c):
    att = lambda k: [a[:, A_Q0 + MIX_W * k : A_Q0 + MIX_W * (k + 1)], b[:, MIX_W * k : MIX_W * (k + 1)], c[:, MIX_W * k : MIX_W * (k + 1)]]
    return jnp.concatenate([a[:, A_U:A_QG], a[:, A_QG:A_KG], a[:, A_KG:A_VG], a[:, A_VG:A_RG], a[:, A_LR : A_LR + 16], a[:, A_RG:A_Q0],
                            *att(0), *att(1), *att(2), a[:, A_GATE:A_U]], axis=1)


def _by_residue(a, d):
    s = a.shape[0]
    return a if d == 1 else a.reshape(s // d, d, -1).transpose(1, 0, 2).reshape(s, -1)


def _in_order(a, d):
    s = a.shape[0]
    return a if d == 1 else a.reshape(d, s // d, -1).transpose(1, 0, 2).reshape(s, -1)


def _pack(arrs):
    flat = jnp.concatenate([a.reshape(-1) for a in arrs])
    n = flat.shape[0]
    rows = -(-n // (8 * 128)) * 8
    return jnp.pad(flat, (0, rows * 128 - n)).reshape(rows, 128)


def _unpack(packed, like):
    flat, out, o = packed.reshape(-1), [], 0
    for a in like:
        n = math.prod(a.shape)
        out.append(flat[o : o + n].reshape(a.shape))
        o += n
    return out


def _layer_fwd(x0, p, kv):
    h = rmsnorm(x0, p["g_mix"], name="mix_norm")
    hs = [h, _by_residue(h, 4), _by_residue(h, 16)]
    proj = [matmul(hs[g], p["w_in_seg"][g], name="in_proj") for g in range(3)]
    ya, xs = s5_fwd(proj[0], A_U, p["bd"], p["cd"], p["lam"], p["s5_d"], p["w_glu"], p["b_glu"])
    yb, gst = gla_fwd(proj[0], GLA_COLS, p["w_gate"], p["b_gla_gate"], p["g_gla_out"])
    att = [attn_fwd(proj[g], A_Q0 if g == 0 else 0, g) for g in range(3)]
    outs = [_in_order(att[g][0], ATT_GROUPS[g][1]) for g in range(3)]
    lses = [_in_order(att[g][1], ATT_GROUPS[g][1]) for g in range(3)]
    x1, merged, yc = merge_fwd(x0, proj[0], ya, yb, outs, lses, p["w_branch"], p["w_out"])
    x2 = cross_fwd(x1, p["g_cross"], p["w_xq"], p["w_xo"], kv)
    x3 = mlp_fwd(x2, p["g_mlp"], p["w_up"], p["w_down"])
    saved = dict(x0=x0, x1=x1, x2=x2, hs=hs, proj=proj, ya=ya, xs=xs, yb=yb, gst=gst, att=att, outs=outs, lses=lses,
                 merged=merged, yc=yc)
    return x3, saved


def _layer_bwd(dx3, p, kv, memn, sv):
    g = {}
    tn = lambda a, b, **kw: matmul(a, b, ta=True, out_dtype=BF16, name="wgrad", **kw)
    dx2, g["g_mlp"], h3, dup, act = mlp_bwd(sv["x2"], p["g_mlp"], p["w_up"], p["w_down"], dx3)
    g["w_up"], g["w_down"] = tn(h3, dup), tn(act, dx3)
    dx1, g["g_cross"], h2, dq, o, dkv = cross_bwd(sv["x1"], p["g_cross"], p["w_xq"], p["w_xo"], kv, dx2)
    g["w_xq"], g["w_xo"], g["w_xkv"] = tn(h2, dq), tn(o, dx2), tn(memn, dkv)
    dmemn = matmul(dkv, p["w_xkv"], tb=True, name="dmem")
    proj = sv["proj"]
    r = merge_bwd(dx1, proj[0], sv["ya"], sv["yb"], sv["yc"], sv["outs"], sv["lses"], p["w_branch"], p["w_out"])
    dgl, dz, dya, dyb, douts, dlses = r[0], r[1:4], r[4], r[5], r[6:9], r[9:12]
    g["w_branch"] = jnp.stack([tn(y, dz[n]) for n, y in enumerate((sv["ya"], sv["yb"], sv["yc"]))])
    g["w_out"] = tn(sv["merged"], dx1)
    datt = []
    for k in range(3):
        dil = ATT_GROUPS[k][1]
        datt.append(attn_bwd(proj[k], A_Q0 if k == 0 else 0, k, sv["att"][k][0], sv["att"][k][1],
                             _by_residue(douts[k], dil), _by_residue(dlses[k], dil)))
    du, adj, gg, dpre, dy, dlam, g["s5_d"], g["b_glu"] = s5_bwd(proj[0], A_U, sv["xs"], p["bd"], p["cd"], p["lam"], p["s5_d"],
                                                                 p["w_glu"], p["b_glu"], dya)
    dbd = matmul(proj[0], adj, ta=True, a_col=(A_U, MIX_W), name="s5_dbd")
    dcd = matmul(sv["xs"], dy, ta=True, name="s5_dcd")
    g["w_glu"] = tn(gg, dpre)
    (g["s5_a_re"], g["s5_a_im"], g["s5_log_step"], g["s5_b_re"], g["s5_b_im"], g["s5_c_re"],
     g["s5_c_im"]) = s5_param_grads(p["disc_in"], dlam, dbd, dcd)
    dqg, dkg, dvg, dlr, drg, dwg, g["b_gla_gate"], g["g_gla_out"] = gla_bwd(proj[0], GLA_COLS, p["w_gate"], p["b_gla_gate"],
                                                                             p["g_gla_out"], sv["gst"], dyb)
    g["w_gla_gate"] = dwg[:GLA_GATE_RANK]
    s = dx3.shape[0]
    dproj = [jnp.concatenate([dgl, du, dqg, dkg, dvg, drg, *datt[0], dlr, jnp.zeros((s, A_W - A_LR - LR_PAD), BF16)], axis=1),
             jnp.concatenate(datt[1], axis=1), jnp.concatenate(datt[2], axis=1)]
    g["w_in"] = _join_w_in(*[tn(sv["hs"][k], dproj[k]) for k in range(3)])
    dhs = [_in_order(matmul(dproj[k], p["w_in_seg"][k], tb=True, name="in_proj_bwd"), ATT_GROUPS[k][1]) for k in range(3)]
    dx0, g["g_mix"] = rmsnorm_bwd(sv["x0"], p["g_mix"], dhs, dx1, name="mix_norm_bwd")
    return dx0, g, dmemn


GLA_GATE_RANK = 16


def kernel(x, mem, *rest):
    nw = len(WEIGHTS)
    w = dict(zip(WEIGHTS, rest[:nw]))
    target = rest[nw]
    m = dict(zip(WEIGHTS, rest[nw + 1 : 2 * nw + 1]))
    v = dict(zip(WEIGHTS, rest[2 * nw + 1 : 3 * nw + 1]))
    x0, memx, target = x[0], mem[0], target[0]
    chip = 2 * lax.axis_index("x") + lax.axis_index("y")

    big = list(SHARDED)
    gate_pad = jnp.pad(w["w_gla_gate"], ((0, 0), (0, 0), (0, LR_PAD - w["w_gla_gate"].shape[2])))
    gathered = all_gather([w[n].astype(BF16) for n in big] + [gate_pad], by_core=True, name="gather_weights")
    gathered = [t.reshape(N_CHIPS, DEPTH, *t.shape[1:]) for t in gathered]
    full = {n: [jnp.concatenate([t[j, l] for j in range(N_CHIPS)], axis=SHARDED[n]) for l in range(DEPTH)]
            for n, t in zip(big, gathered)}
    gate_full = [jnp.concatenate([gathered[-1][j, l][:, : w["w_gla_gate"].shape[2]] for j in range(N_CHIPS)], axis=1)
                 for l in range(DEPTH)]

    memn = rmsnorm(memx, w["g_mem"], name="mem_norm")
    params, kvs = [], []
    for l in range(DEPTH):
        p = {n: full[n][l] for n in big if n != "w_in"}
        p["w_in_seg"] = _split_w_in(full["w_in"][l])
        p["w_gate"] = jnp.pad(gate_full[l], ((0, LR_PAD - GLA_GATE_RANK), (0, 0))).astype(BF16)
        for n in ("g_mix", "g_cross", "g_mlp"):
            p[n] = w[n][l]
        for n in ("s5_d", "b_glu", "b_gla_gate", "g_gla_out"):
            p[n] = w[n][l].reshape(1, -1)
        p["disc_in"], p["lam"], p["bd"], p["cd"] = s5_prepare(*[w[n][l] for n in WEIGHTS[2:9]])
        params.append(p)
        kvs.append(matmul(memn, p["w_xkv"], out_dtype=BF16, name="mem_kv"))

    xl, saved = x0, []
    for l in range(DEPTH):
        xl, sv = _layer_fwd(xl, params[l], kvs[l])
        saved.append(sv)
    loss8, dx, dg_final = loss_head(xl, w["g_final"], target)
    grads, dmem = [None] * DEPTH, []
    for l in reversed(range(DEPTH)):
        dx, grads[l], dm = _layer_bwd(dx, params[l], kvs[l], memn, saved[l])
        dmem.append(dm)
    _, dg_mem = rmsnorm_bwd(memx, w["g_mem"], dmem, jnp.zeros_like(memx), name="mem_norm_bwd")

    def blocks(n):
        ax = SHARDED[n]
        pieces = [[jnp.split(grads[l][n], N_CHIPS, axis=ax)[j] for l in range(DEPTH)] for j in range(N_CHIPS)]
        return jnp.stack([jnp.stack(pj) for pj in pieces]).reshape(N_DEV, *pieces[0][0].shape)

    landed = all_to_all([blocks(n) for n in big], name="exchange_grads")
    reduced = [sum_blocks(t.reshape(N_DEV, -1, t.shape[-1]), name="sum_grads") for t in landed]
    paired = pair_by_core(reduced, name="pair_layers")
    out_g = {n: t.reshape(w[n].shape) for n, t in zip(big, paired)}

    small = [n for n in WEIGHTS if n not in SHARDED]
    local_small = []
    for n in small:
        if n == "g_mem":
            local_small.append(dg_mem.reshape(w[n].shape))
        elif n == "g_final":
            local_small.append(dg_final.reshape(w[n].shape))
        elif n == "w_gla_gate":
            local_small.append(jnp.stack([grads[l][n] for l in range(DEPTH)]))
        else:
            local_small.append(jnp.stack([grads[l][n].reshape(w[n].shape[1:]) for l in range(DEPTH)]))
    packed = _pack(local_small + [loss8[0, :1]])
    (every,) = all_gather([packed], name="gather_small")
    summed = _unpack(sum_blocks(every, name="sum_small"), local_small + [loss8[0, :1]])
    loss = summed[-1].reshape(())
    for n, t in zip(small, summed[:-1]):
        if n == "w_gla_gate":
            t = lax.dynamic_slice_in_dim(t, chip * w[n].shape[2], w[n].shape[2], axis=2)
        out_g[n] = t

    delta, new_m, new_v = {}, {}, {}
    for n in big:
        c = w[n].shape[-1]
        r = [t.reshape(w[n].shape) for t in adamw(w[n].reshape(-1, c), out_g[n].reshape(-1, c), m[n].reshape(-1, c),
                                                  v[n].reshape(-1, c), name="adamw")]
        delta[n], new_m[n], new_v[n] = r
    like = [w[n] for n in small]
    r = adamw(_pack(like), _pack([out_g[n] for n in small]), _pack([m[n] for n in small]), _pack([v[n] for n in small]),
              name="adamw_small")
    for d, t in zip((delta, new_m, new_v), r):
        d.update(zip(small, _unpack(t, like)))
    return (loss, dx[None], *[out_g[n] for n in WEIGHTS], *[delta[n] for n in WEIGHTS], *[new_m[n] for n in WEIGHTS],
            *[new_v[n] for n in WEIGHTS])
```

```python
import functools
import math

import jax
import jax.numpy as jnp
import numpy as np
from jax import lax
from jax.experimental import pallas as pl
from jax.experimental.pallas import tpu as pltpu

F32 = jnp.float32
BF16 = jnp.bfloat16

VMEM_LIMIT_BYTES = 56 * 1024 * 1024


def _cparams(sem):
    return pltpu.CompilerParams(dimension_semantics=sem, vmem_limit_bytes=VMEM_LIMIT_BYTES)


def _dot(a, b, ca=1, cb=0):
    return lax.dot_general(a.astype(BF16), b.astype(BF16), (((ca,), (cb,)), ((), ())), preferred_element_type=F32)


def _pick(n, prefs):
    for p in prefs:
        if n % p == 0:
            return p
    return n


def matmul(a, b, *, ta=False, tb=False, out_dtype=F32, name="mm", a_col=None, b_col=None):
    a_off, a_w = a_col if a_col is not None else (0, a.shape[1])
    b_off, b_w = b_col if b_col is not None else (0, b.shape[1])
    if ta:
        kk, m = a.shape[0], a_w
    else:
        m, kk = a.shape[0], a_w
    if tb:
        n, kb = b.shape[0], b_w
    else:
        kb, n = b.shape[0], b_w
    assert kk == kb, (a.shape, b.shape, ta, tb)
    tm = _pick(m, (512, 256, 128))
    tn = _pick(n, (1024, 512, 256, 128))
    tk = _pick(kk, (1024, 512, 256, 128))
    nk = kk // tk
    if ta:
        assert a_off % tm == 0
        a_spec = pl.BlockSpec((tk, tm), lambda i, j, k, o=a_off // tm: (k, i + o))
    else:
        assert a_off % tk == 0
        a_spec = pl.BlockSpec((tm, tk), lambda i, j, k, o=a_off // tk: (i, k + o))
    if tb:
        assert b_off % tk == 0
        b_spec = pl.BlockSpec((tn, tk), lambda i, j, k, o=b_off // tk: (j, k + o))
    else:
        assert b_off % tn == 0
        b_spec = pl.BlockSpec((tk, tn), lambda i, j, k, o=b_off // tn: (k, j + o))

    def body(a_ref, b_ref, o_ref, acc_ref):
        k = pl.program_id(2)
        p = _dot(a_ref[...], b_ref[...], 0 if ta else 1, 1 if tb else 0)

        @pl.when(k == 0)
        def _():
            acc_ref[...] = p

        @pl.when(k > 0)
        def _():
            acc_ref[...] += p

        @pl.when(k == nk - 1)
        def _():
            o_ref[...] = acc_ref[...].astype(o_ref.dtype)

    return pl.pallas_call(
        body,
        name=name,
        grid=(m // tm, n // tn, nk),
        in_specs=[a_spec, b_spec],
        out_specs=pl.BlockSpec((tm, tn), lambda i, j, k: (i, j)),
        out_shape=jax.ShapeDtypeStruct((m, n), out_dtype),
        scratch_shapes=[pltpu.VMEM((tm, tn), F32)],
        compiler_params=_cparams(("parallel", "parallel", "arbitrary")),
    )(a, b)


RMS_EPS = 1e-6
ROW_TILE = 512


def _rms_fwd(x, g):
    r = lax.rsqrt(jnp.mean(x * x, axis=-1, keepdims=True) + RMS_EPS)
    return x * r * g


def _rms_bwd(x, g, dh):
    r = lax.rsqrt(jnp.mean(x * x, axis=-1, keepdims=True) + RMS_EPS)
    xh = x * r
    dg = jnp.sum(dh * xh, axis=0, keepdims=True)
    dxh = dh * g
    dx = r * (dxh - xh * jnp.mean(dxh * xh, axis=-1, keepdims=True))
    return dx, dg


def _accum(ref, val, first):
    @pl.when(first)
    def _():
        ref[...] = val

    @pl.when(jnp.logical_not(first))
    def _():
        ref[...] += val


def rmsnorm(x, g, *, out_dtype=BF16, name="rmsnorm"):
    s, d = x.shape
    tm = _pick(s, (ROW_TILE, 256, 128, 8))

    def body(x_ref, g_ref, o_ref):
        o_ref[...] = _rms_fwd(x_ref[...], g_ref[...]).astype(o_ref.dtype)

    return pl.pallas_call(
        body,
        name=name,
        grid=(s // tm,),
        in_specs=[pl.BlockSpec((tm, d), lambda i: (i, 0)), pl.BlockSpec((1, d), lambda i: (0, 0))],
        out_specs=pl.BlockSpec((tm, d), lambda i: (i, 0)),
        out_shape=jax.ShapeDtypeStruct((s, d), out_dtype),
        compiler_params=_cparams(("arbitrary",)),
    )(x, g.reshape(1, d))


def rmsnorm_bwd(x, g, dhs, dres, *, name="rmsnorm_bwd"):
    s, d = x.shape
    tm = _pick(s, (ROW_TILE, 256, 128, 8))
    n = len(dhs)

    def body(x_ref, g_ref, *refs):
        dh_refs, dres_ref, dx_ref, dg_ref = refs[:n], refs[n], refs[n + 1], refs[n + 2]
        dh = dh_refs[0][...].astype(F32)
        for r in dh_refs[1:]:
            dh = dh + r[...].astype(F32)
        dx, dg = _rms_bwd(x_ref[...], g_ref[...], dh)
        dx_ref[...] = dres_ref[...] + dx
        _accum(dg_ref, dg, pl.program_id(0) == 0)

    row = pl.BlockSpec((tm, d), lambda i: (i, 0))
    vec = pl.BlockSpec((1, d), lambda i: (0, 0))
    return pl.pallas_call(
        body,
        name=name,
        grid=(s // tm,),
        in_specs=[row, vec] + [row] * n + [row],
        out_specs=[row, vec],
        out_shape=[jax.ShapeDtypeStruct((s, d), F32), jax.ShapeDtypeStruct((1, d), F32)],
        compiler_params=_cparams(("arbitrary",)),
    )(x, g.reshape(1, d), *dhs, dres)


def loss_head(x, g, target, *, name="loss_head"):
    s, d = x.shape
    tm = _pick(s, (ROW_TILE, 256, 128, 8))

    def body(x_ref, g_ref, t_ref, l_ref, dx_ref, dg_ref):
        x_, g_ = x_ref[...], g_ref[...]
        e = _rms_fwd(x_, g_) - t_ref[...]
        part = 0.5 * jnp.sum(jnp.sum(e * e, axis=-1, keepdims=True), axis=0, keepdims=True) / d
        dx, dg = _rms_bwd(x_, g_, e * (1.0 / d))
        dx_ref[...] = dx
        first = pl.program_id(0) == 0
        _accum(dg_ref, dg, first)
        _accum(l_ref, jnp.broadcast_to(part, (8, 128)), first)

    row = pl.BlockSpec((tm, d), lambda i: (i, 0))
    vec = pl.BlockSpec((1, d), lambda i: (0, 0))
    return pl.pallas_call(
        body,
        name=name,
        grid=(s // tm,),
        in_specs=[row, vec, row],
        out_specs=[pl.BlockSpec((8, 128), lambda i: (0, 0)), row, vec],
        out_shape=[jax.ShapeDtypeStruct((8, 128), F32), jax.ShapeDtypeStruct((s, d), F32), jax.ShapeDtypeStruct((1, d), F32)],
        compiler_params=_cparams(("arbitrary",)),
    )(x, g.reshape(1, d), target)


FF_TILE = 1024


def mlp_fwd(x, g, w_up, w_down, *, name="mlp_fwd"):
    s, d = x.shape
    ff = w_up.shape[1]
    tm, tf = _pick(s, (ROW_TILE, 256, 128)), _pick(ff, (FF_TILE, 512, 256, 128))
    nf = ff // tf

    def body(x_ref, g_ref, wu_ref, wd_ref, o_ref, h_ref, acc_ref):
        f = pl.program_id(1)

        @pl.when(f == 0)
        def _():
            h_ref[...] = _rms_fwd(x_ref[...], g_ref[...]).astype(BF16)
            acc_ref[...] = x_ref[...]

        up = jnp.maximum(_dot(h_ref[...], wu_ref[...]), 0.0)
        acc_ref[...] += _dot(up * up, wd_ref[...])

        @pl.when(f == nf - 1)
        def _():
            o_ref[...] = acc_ref[...]

    return pl.pallas_call(
        body,
        name=name,
        grid=(s // tm, nf),
        in_specs=[
            pl.BlockSpec((tm, d), lambda i, f: (i, 0)),
            pl.BlockSpec((1, d), lambda i, f: (0, 0)),
            pl.BlockSpec((d, tf), lambda i, f: (0, f)),
            pl.BlockSpec((tf, d), lambda i, f: (f, 0)),
        ],
        out_specs=pl.BlockSpec((tm, d), lambda i, f: (i, 0)),
        out_shape=jax.ShapeDtypeStruct((s, d), F32),
        scratch_shapes=[pltpu.VMEM((tm, d), BF16), pltpu.VMEM((tm, d), F32)],
        compiler_params=_cparams(("arbitrary", "arbitrary")),
    )(x, g.reshape(1, d), w_up, w_down)


def mlp_bwd(x, g, w_up, w_down, dy, *, name="mlp_bwd"):
    s, d = x.shape
    ff = w_up.shape[1]
    tm, tf = _pick(s, (ROW_TILE, 256, 128)), _pick(ff, (FF_TILE, 512, 256, 128))
    nf = ff // tf

    def body(x_ref, g_ref, wu_ref, wd_ref, dy_ref, dx_ref, dg_ref, h_ref, dup_ref, act_ref, dyb_ref, acc_ref):
        i, f = pl.program_id(0), pl.program_id(1)

        @pl.when(f == 0)
        def _():
            h_ref[...] = _rms_fwd(x_ref[...], g_ref[...]).astype(BF16)
            dyb_ref[...] = dy_ref[...].astype(BF16)
            acc_ref[...] = jnp.zeros_like(acc_ref)

        up = jnp.maximum(_dot(h_ref[...], wu_ref[...]), 0.0)
        dact = _dot(dyb_ref[...], wd_ref[...], 1, 1)
        dup = (2.0 * up * dact).astype(BF16)
        dup_ref[...] = dup
        act_ref[...] = (up * up).astype(BF16)
        acc_ref[...] += _dot(dup, wu_ref[...], 1, 1)

        @pl.when(f == nf - 1)
        def _():
            dx, dg = _rms_bwd(x_ref[...], g_ref[...], acc_ref[...])
            dx_ref[...] = dy_ref[...] + dx
            _accum(dg_ref, dg, i == 0)

    row = pl.BlockSpec((tm, d), lambda i, f: (i, 0))
    vec = pl.BlockSpec((1, d), lambda i, f: (0, 0))
    wide = pl.BlockSpec((tm, tf), lambda i, f: (i, f))
    return pl.pallas_call(
        body,
        name=name,
        grid=(s // tm, nf),
        in_specs=[row, vec, pl.BlockSpec((d, tf), lambda i, f: (0, f)), pl.BlockSpec((tf, d), lambda i, f: (f, 0)), row],
        out_specs=[row, vec, row, wide, wide],
        out_shape=[
            jax.ShapeDtypeStruct((s, d), F32),
            jax.ShapeDtypeStruct((1, d), F32),
            jax.ShapeDtypeStruct((s, d), BF16),
            jax.ShapeDtypeStruct((s, ff), BF16),
            jax.ShapeDtypeStruct((s, ff), BF16),
        ],
        scratch_shapes=[pltpu.VMEM((tm, d), BF16), pltpu.VMEM((tm, d), F32)],
        compiler_params=_cparams(("arbitrary", "arbitrary")),
    )(x, g.reshape(1, d), w_up, w_down, dy)


X_HEADS = 4


def _softmax_rows(s):
    m = jnp.max(s, axis=-1, keepdims=True)
    e = jnp.exp(s - m)
    return e / jnp.sum(e, axis=-1, keepdims=True)


def cross_fwd(x, g, wq, wo, kv, *, name="cross_fwd"):
    s, d = x.shape
    ml = kv.shape[0]
    dh = d // X_HEADS
    tm = _pick(s, (ROW_TILE, 256, 128))
    scale = dh**-0.5

    def body(x_ref, g_ref, wq_ref, wo_ref, kv_ref, o_ref):
        x_ = x_ref[...]
        q = _dot(_rms_fwd(x_, g_ref[...]), wq_ref[...])
        outs = []
        for hd in range(X_HEADS):
            kh = kv_ref[:, hd * dh : (hd + 1) * dh]
            vh = kv_ref[:, d + hd * dh : d + (hd + 1) * dh]
            p = _softmax_rows(_dot(q[:, hd * dh : (hd + 1) * dh], kh, 1, 1) * scale)
            outs.append(_dot(p, vh))
        o_ref[...] = x_ + _dot(jnp.concatenate(outs, axis=-1), wo_ref[...])

    row = pl.BlockSpec((tm, d), lambda i: (i, 0))
    full = lambda shp: pl.BlockSpec(shp, lambda i: (0, 0))
    return pl.pallas_call(
        body,
        name=name,
        grid=(s // tm,),
        in_specs=[row, full((1, d)), full((d, d)), full((d, d)), full((ml, 2 * d))],
        out_specs=row,
        out_shape=jax.ShapeDtypeStruct((s, d), F32),
        compiler_params=_cparams(("arbitrary",)),
    )(x, g.reshape(1, d), wq, wo, kv)


def cross_bwd(x, g, wq, wo, kv, dy, *, name="cross_bwd"):
    s, d = x.shape
    ml = kv.shape[0]
    dh = d // X_HEADS
    tm = _pick(s, (ROW_TILE, 256, 128))
    scale = dh**-0.5

    def body(x_ref, g_ref, wq_ref, wo_ref, kv_ref, dy_ref, dx_ref, dg_ref, h_ref, dq_ref, o_ref, dkv_ref):
        first = pl.program_id(0) == 0
        x_, g_ = x_ref[...], g_ref[...]
        h = _rms_fwd(x_, g_).astype(BF16)
        h_ref[...] = h
        q = _dot(h, wq_ref[...])
        dy_ = dy_ref[...]
        do = _dot(dy_, wo_ref[...], 1, 1)
        outs, dqs, dks, dvs = [], [], [], []
        for hd in range(X_HEADS):
            sl = slice(hd * dh, (hd + 1) * dh)
            kh = kv_ref[:, sl]
            vh = kv_ref[:, d + hd * dh : d + (hd + 1) * dh]
            qh = q[:, sl]
            p = _softmax_rows(_dot(qh, kh, 1, 1) * scale)
            outs.append(_dot(p, vh))
            doh = do[:, sl]
            dp = _dot(doh, vh, 1, 1)
            ds = p * (dp - jnp.sum(dp * p, axis=-1, keepdims=True)) * scale
            dqs.append(_dot(ds, kh))
            dks.append(_dot(ds, qh, 0, 0))
            dvs.append(_dot(p, doh, 0, 0))
        o_ref[...] = jnp.concatenate(outs, axis=-1).astype(BF16)
        dq = jnp.concatenate(dqs, axis=-1).astype(BF16)
        dq_ref[...] = dq
        _accum(dkv_ref, jnp.concatenate(dks + dvs, axis=-1), first)
        dx, dg = _rms_bwd(x_, g_, _dot(dq, wq_ref[...], 1, 1))
        dx_ref[...] = dy_ + dx
        _accum(dg_ref, dg, first)

    row = pl.BlockSpec((tm, d), lambda i: (i, 0))
    full = lambda shp: pl.BlockSpec(shp, lambda i: (0, 0))
    rowb = jax.ShapeDtypeStruct((s, d), BF16)
    return pl.pallas_call(
        body,
        name=name,
        grid=(s // tm,),
        in_specs=[row, full((1, d)), full((d, d)), full((d, d)), full((ml, 2 * d)), row],
        out_specs=[row, full((1, d)), row, row, row, full((ml, 2 * d))],
        out_shape=[jax.ShapeDtypeStruct((s, d), F32), jax.ShapeDtypeStruct((1, d), F32), rowb, rowb, rowb,
                   jax.ShapeDtypeStruct((ml, 2 * d), F32)],
        compiler_params=_cparams(("arbitrary",)),
    )(x, g.reshape(1, d), wq, wo, kv, dy)


N_BRANCH = 3
MIX_W = 512
ATT_DH = 128
ATT_HG = 4
MERGE_TILE = 256


def _group_weights(l0, l1, l2):
    m = jnp.maximum(jnp.maximum(l0, l1), l2)
    e = [jnp.exp(l0 - m), jnp.exp(l1 - m), jnp.exp(l2 - m)]
    inv = 1.0 / (e[0] + e[1] + e[2])
    return [t * inv for t in e]


def _sigmoid(x):
    return 1.0 / (1.0 + jnp.exp(-x))


def merge_fwd(x, proj, ya, yb, outs, lses, wb, wout, *, name="merge_fwd"):
    s, d = x.shape
    tm = _pick(s, (MERGE_TILE, 128))

    def body(x_ref, g0, g1, g2, ya_ref, yb_ref, o0, o1, o2, l0, l1, l2, wb_ref, wo_ref, x1_ref, mg_ref, yc_ref):
        w = _group_weights(l0[...], l1[...], l2[...])
        yc = w[0] * o0[...] + w[1] * o1[...] + w[2] * o2[...]
        yc_ref[...] = yc.astype(BF16)
        merged = None
        for n, (y, gl) in enumerate(((ya_ref[...], g0), (yb_ref[...], g1), (yc, g2))):
            t = _sigmoid(gl[...]) * _dot(y, wb_ref[n])
            merged = t if merged is None else merged + t
        mb = merged.astype(BF16)
        mg_ref[...] = mb
        x1_ref[...] = x_ref[...] + _dot(mb, wo_ref[...])

    row = pl.BlockSpec((tm, d), lambda i: (i, 0))
    half = pl.BlockSpec((tm, MIX_W), lambda i: (i, 0))
    gate = [pl.BlockSpec((tm, d), lambda i, n=n: (i, n)) for n in range(N_BRANCH)]
    return pl.pallas_call(
        body,
        name=name,
        grid=(s // tm,),
        in_specs=[row] + gate + [half] * 8
        + [pl.BlockSpec((N_BRANCH, MIX_W, d), lambda i: (0, 0, 0)), pl.BlockSpec((d, d), lambda i: (0, 0))],
        out_specs=[row, row, half],
        out_shape=[jax.ShapeDtypeStruct((s, d), F32), jax.ShapeDtypeStruct((s, d), BF16), jax.ShapeDtypeStruct((s, MIX_W), BF16)],
        compiler_params=_cparams(("arbitrary",)),
    )(x, proj, proj, proj, ya, yb, *outs, *lses, wb, wout)


def merge_bwd(dx1, proj, ya, yb, yc, outs, lses, wb, wout, *, name="merge_bwd"):
    s, d = dx1.shape
    tm = _pick(s, (MERGE_TILE, 128))

    def body(dx_ref, g0, g1, g2, ya_ref, yb_ref, yc_ref, o0, o1, o2, l0, l1, l2, wb_ref, wo_ref,
             dgl_ref, dz0, dz1, dz2, dya_ref, dyb_ref, do0, do1, do2, dl0, dl1, dl2):
        dm = _dot(dx_ref[...], wo_ref[...], 1, 1)
        dys = []
        for n, (y, gl, dz_ref) in enumerate(((ya_ref, g0, dz0), (yb_ref, g1, dz1), (yc_ref, g2, dz2))):
            z = _dot(y[...], wb_ref[n])
            sg = _sigmoid(gl[...])
            dz = (dm * sg).astype(BF16)
            dz_ref[...] = dz
            dgl_ref[:, n * d : (n + 1) * d] = (dm * z * sg * (1.0 - sg)).astype(BF16)
            dys.append(_dot(dz, wb_ref[n], 1, 1))
        dya_ref[...] = dys[0]
        dyb_ref[...] = dys[1]
        dyc = dys[2]
        w = _group_weights(l0[...], l1[...], l2[...])
        o = [o0[...], o1[...], o2[...]]
        for gi, r in enumerate((do0, do1, do2)):
            r[...] = w[gi] * dyc
        for hd in range(ATT_HG):
            sl = slice(hd * ATT_DH, (hd + 1) * ATT_DH)
            t = [jnp.sum(dyc[:, sl] * o[gi][:, sl], axis=-1, keepdims=True) for gi in range(3)]
            wh = [w[gi][:, sl] for gi in range(3)]
            tbar = wh[0] * t[0] + wh[1] * t[1] + wh[2] * t[2]
            for gi, r in enumerate((dl0, dl1, dl2)):
                r[:, sl] = wh[gi] * (t[gi] - tbar)

    row = pl.BlockSpec((tm, d), lambda i: (i, 0))
    half = pl.BlockSpec((tm, MIX_W), lambda i: (i, 0))
    gate = [pl.BlockSpec((tm, d), lambda i, n=n: (i, n)) for n in range(N_BRANCH)]
    rb = jax.ShapeDtypeStruct((s, d), BF16)
    hf = jax.ShapeDtypeStruct((s, MIX_W), F32)
    return pl.pallas_call(
        body,
        name=name,
        grid=(s // tm,),
        in_specs=[row] + gate + [half] * 9
        + [pl.BlockSpec((N_BRANCH, MIX_W, d), lambda i: (0, 0, 0)), pl.BlockSpec((d, d), lambda i: (0, 0))],
        out_specs=[pl.BlockSpec((tm, N_BRANCH * d), lambda i: (i, 0)), row, row, row] + [half] * 8,
        out_shape=[jax.ShapeDtypeStruct((s, N_BRANCH * d), BF16), rb, rb, rb] + [hf] * 8,
        compiler_params=_cparams(("arbitrary",)),
    )(dx1, proj, proj, proj, ya, yb, yc, *outs, *lses, wb, wout)


ATT_BLOCK = 128
ATT_GROUPS = ((128, 1), (512, 4), (2048, 16))
N_ATT_HEADS = ATT_HG * len(ATT_GROUPS)
ALIBI_MAX_EXP = 8.0
MASKED = -1e30


def _att_slopes(group):
    return [2.0 ** (-ALIBI_MAX_EXP * (group * ATT_HG + h + 1) / N_ATT_HEADS) for h in range(ATT_HG)]


def _att_scores(q, kp, kc, slope_dil, has_prev):
    scale = ATT_DH**-0.5
    qi = lax.broadcasted_iota(jnp.int32, (ATT_BLOCK, ATT_BLOCK), 0)
    kj = lax.broadcasted_iota(jnp.int32, (ATT_BLOCK, ATT_BLOCK), 1)
    dist = qi - kj
    sc = _dot(q, kc, 1, 1) * scale - slope_dil * dist.astype(F32)
    sc = jnp.where(dist >= 0, sc, MASKED)
    sp = _dot(q, kp, 1, 1) * scale - slope_dil * (dist + ATT_BLOCK).astype(F32)
    sp = jnp.where(jnp.logical_and(dist <= 0, has_prev), sp, MASKED)
    return sp, sc


def attn_fwd(qkv, col, group, *, name="attn_fwd"):
    s = qkv.shape[0]
    window, dil = ATT_GROUPS[group]
    assert window // dil == ATT_BLOCK and s % (dil * ATT_BLOCK) == 0 and col % MIX_W == 0
    nb = s // dil // ATT_BLOCK
    nblk = s // ATT_BLOCK
    slopes = _att_slopes(group)
    c0 = col // MIX_W

    def body(q_ref, kp_ref, kc_ref, vp_ref, vc_ref, o_ref, l_ref):
        n = pl.program_id(0)
        has_prev = (n % nb) != 0
        for hd in range(ATT_HG):
            sl = slice(hd * ATT_DH, (hd + 1) * ATT_DH)
            sp, sc = _att_scores(q_ref[:, sl], kp_ref[:, sl], kc_ref[:, sl], slopes[hd] * dil, has_prev)
            m = jnp.maximum(jnp.max(sp, axis=-1, keepdims=True), jnp.max(sc, axis=-1, keepdims=True))
            ep, ec = jnp.exp(sp - m), jnp.exp(sc - m)
            den = jnp.sum(ep, axis=-1, keepdims=True) + jnp.sum(ec, axis=-1, keepdims=True)
            inv = 1.0 / den
            o_ref[:, sl] = _dot(ep * inv, vp_ref[:, sl]) + _dot(ec * inv, vc_ref[:, sl])
            l_ref[:, sl] = jnp.broadcast_to(m + jnp.log(den), (ATT_BLOCK, ATT_DH))

    cur = lambda c: pl.BlockSpec((ATT_BLOCK, MIX_W), lambda n, c=c: (n, c))
    prev = lambda c: pl.BlockSpec((ATT_BLOCK, MIX_W), lambda n, c=c: (jnp.maximum(n - 1, 0), c))
    blk = pl.BlockSpec((ATT_BLOCK, MIX_W), lambda n: (n, 0))
    return pl.pallas_call(
        body,
        name=name,
        grid=(nblk,),
        in_specs=[cur(c0), prev(c0 + 1), cur(c0 + 1), prev(c0 + 2), cur(c0 + 2)],
        out_specs=[blk, blk],
        out_shape=[jax.ShapeDtypeStruct((s, MIX_W), F32)] * 2,
        compiler_params=_cparams(("arbitrary",)),
    )(qkv, qkv, qkv, qkv, qkv)


def attn_bwd(qkv, col, group, out, lse, dout, dlse, *, name="attn_bwd"):
    s = qkv.shape[0]
    window, dil = ATT_GROUPS[group]
    nb = s // dil // ATT_BLOCK
    nblk = s // ATT_BLOCK
    slopes = _att_slopes(group)
    c0 = col // MIX_W
    scale = ATT_DH**-0.5

    def body(q_ref, kp_ref, kc_ref, vp_ref, vc_ref, o_ref, l_ref, do_ref, dl_ref, dq_ref, dk_ref, dv_ref, acck, accv):
        n = pl.program_id(0)
        live = n < nblk
        has_prev = jnp.logical_and((n % nb) != 0, live)

        @pl.when(n == 0)
        def _():
            acck[...] = jnp.zeros_like(acck)
            accv[...] = jnp.zeros_like(accv)

        for hd in range(ATT_HG):
            sl = slice(hd * ATT_DH, (hd + 1) * ATT_DH)
            q, kp, kc, vp, vc = q_ref[:, sl], kp_ref[:, sl], kc_ref[:, sl], vp_ref[:, sl], vc_ref[:, sl]
            do = do_ref[:, sl]
            sp, sc = _att_scores(q, kp, kc, slopes[hd] * dil, has_prev)
            l = l_ref[:, sl][:, :1]
            pp, pc = jnp.exp(sp - l), jnp.exp(sc - l)
            corr = dl_ref[:, sl][:, :1] - jnp.sum(do * o_ref[:, sl], axis=-1, keepdims=True)
            dsp = pp * (_dot(do, vp, 1, 1) + corr) * scale
            dsc = pc * (_dot(do, vc, 1, 1) + corr) * scale
            dqh = (_dot(dsp, kp) + _dot(dsc, kc)).astype(BF16)

            @pl.when(live)
            def _(dqh=dqh, sl=sl):
                dq_ref[:, sl] = dqh

            dk_ref[:, sl] = (acck[:, sl] + _dot(dsp, q, 0, 0)).astype(BF16)
            dv_ref[:, sl] = (accv[:, sl] + _dot(pp, do, 0, 0)).astype(BF16)
            acck[:, sl] = jnp.where(live, _dot(dsc, q, 0, 0), 0.0)
            accv[:, sl] = jnp.where(live, _dot(pc, do, 0, 0), 0.0)

    last = nblk - 1
    cur = lambda c: pl.BlockSpec((ATT_BLOCK, MIX_W), lambda n, c=c: (jnp.minimum(n, last), c))
    prev = lambda c: pl.BlockSpec((ATT_BLOCK, MIX_W), lambda n, c=c: (jnp.clip(n - 1, 0, last), c))
    sds = jax.ShapeDtypeStruct((s, MIX_W), BF16)
    return pl.pallas_call(
        body,
        name=name,
        grid=(nblk + 1,),
        in_specs=[cur(c0), prev(c0 + 1), cur(c0 + 1), prev(c0 + 2), cur(c0 + 2), cur(0), cur(0), cur(0), cur(0)],
        out_specs=[cur(0), prev(0), prev(0)],
        out_shape=[sds, sds, sds],
        scratch_shapes=[pltpu.VMEM((ATT_BLOCK, MIX_W), F32), pltpu.VMEM((ATT_BLOCK, MIX_W), F32)],
        compiler_params=_cparams(("arbitrary",)),
    )(qkv, qkv, qkv, qkv, qkv, out, lse, dout, dlse)


GLA_HEADS = 4
GLA_DK = 64
GLA_DV = 128
GLA_CHUNK = 64
GLA_TAU = 16.0
GLA_QK = GLA_HEADS * GLA_DK
LR_PAD = 128


def _dot_exact(a, b, ca=1, cb=0):
    return lax.dot_general(a, b, (((ca,), (cb,)), ((), ())), precision=lax.Precision.HIGHEST, preferred_element_type=F32)


def _gla_chunk(q, k, v, lr, r, state, wg, bg, go):
    c = GLA_CHUNK
    z = _dot(lr, wg) + bg
    la = (jnp.minimum(z, 0.0) - jnp.log(1.0 + jnp.exp(-jnp.abs(z)))) * (1.0 / GLA_TAU)
    ri = lax.broadcasted_iota(jnp.int32, (c, c), 0)
    ci = lax.broadcasted_iota(jnp.int32, (c, c), 1)
    causal = ri >= ci
    b = _dot_exact(causal.astype(F32), la)
    bmid = b[c // 2 : c // 2 + 1, :]
    bend = b[c - 1 : c, :]
    qs = q * GLA_DK**-0.5
    q_in = qs * jnp.exp(b)
    q_mid = qs * jnp.exp(b - bmid)
    k_mid = k * jnp.exp(bmid - b)
    k_end = k * jnp.exp(bend - b)
    ys, upd = [], []
    for h in range(GLA_HEADS):
        sk = slice(h * GLA_DK, (h + 1) * GLA_DK)
        sv = slice(h * GLA_DV, (h + 1) * GLA_DV)
        vh = v[:, sv]
        inter = _dot(q_in[:, sk], state[:, sk], 1, 1)
        sc = jnp.where(causal, _dot_exact(q_mid[:, sk], k_mid[:, sk], 1, 1), 0.0)
        o = inter + _dot(sc, vh)
        o = o * lax.rsqrt(jnp.mean(o * o, axis=-1, keepdims=True) + RMS_EPS) * go
        rh = r[:, sv]
        ys.append(o * rh * _sigmoid(rh))
        upd.append(_dot(vh, k_end[:, sk], 0, 0))
    new_state = jnp.exp(bend) * state + jnp.concatenate(upd, axis=-1)
    return jnp.concatenate(ys, axis=-1), new_state


def _gla_in_specs(cols, rev, nc):
    c = GLA_CHUNK
    row = (lambda i: nc - 1 - i) if rev else (lambda i: i)
    qc, kc, vc, lc, rc = cols
    assert qc % GLA_QK == 0 and kc % GLA_QK == 0 and vc % MIX_W == 0 and rc % MIX_W == 0 and lc % LR_PAD == 0
    return [
        pl.BlockSpec((c, GLA_QK), lambda i: (row(i), qc // GLA_QK)),
        pl.BlockSpec((c, GLA_QK), lambda i: (row(i), kc // GLA_QK)),
        pl.BlockSpec((c, MIX_W), lambda i: (row(i), vc // MIX_W)),
        pl.BlockSpec((c, LR_PAD), lambda i: (row(i), lc // LR_PAD)),
        pl.BlockSpec((c, MIX_W), lambda i: (row(i), rc // MIX_W)),
    ], row


def gla_fwd(proj, cols, wg, bg, go, *, name="gla_fwd"):
    s = proj.shape[0]
    nc = s // GLA_CHUNK
    specs, row = _gla_in_specs(cols, False, nc)

    def body(q_ref, k_ref, v_ref, lr_ref, r_ref, wg_ref, bg_ref, go_ref, y_ref, st_ref, state):
        @pl.when(pl.program_id(0) == 0)
        def _():
            state[...] = jnp.zeros_like(state)

        st = state[...]
        st_ref[...] = st
        y, new = _gla_chunk(q_ref[...], k_ref[...], v_ref[...], lr_ref[...], r_ref[...], st, wg_ref[...], bg_ref[...], go_ref[...])
        y_ref[...] = y
        state[...] = new

    full = lambda shp: pl.BlockSpec(shp, lambda i: (0, 0))
    return pl.pallas_call(
        body,
        name=name,
        grid=(nc,),
        in_specs=specs + [full((LR_PAD, GLA_QK)), full((1, GLA_QK)), full((1, GLA_DV))],
        out_specs=[pl.BlockSpec((GLA_CHUNK, MIX_W), lambda i: (i, 0)), pl.BlockSpec((GLA_DV, GLA_QK), lambda i: (i, 0))],
        out_shape=[jax.ShapeDtypeStruct((s, MIX_W), F32), jax.ShapeDtypeStruct((nc * GLA_DV, GLA_QK), F32)],
        scratch_shapes=[pltpu.VMEM((GLA_DV, GLA_QK), F32)],
        compiler_params=_cparams(("arbitrary",)),
    )(proj, proj, proj, proj, proj, wg, bg, go)


def gla_bwd(proj, cols, wg, bg, go, states, dy, *, name="gla_bwd"):
    s = proj.shape[0]
    nc = s // GLA_CHUNK
    specs, row = _gla_in_specs(cols, True, nc)

    def body(q_ref, k_ref, v_ref, lr_ref, r_ref, wg_ref, bg_ref, go_ref, st_ref, dy_ref,
             dq_ref, dk_ref, dv_ref, dlr_ref, dr_ref, dwg_ref, dbg_ref, dgo_ref, dstate):
        first = pl.program_id(0) == 0

        @pl.when(first)
        def _():
            dstate[...] = jnp.zeros_like(dstate)

        _, vjp = jax.vjp(_gla_chunk, q_ref[...], k_ref[...], v_ref[...], lr_ref[...], r_ref[...], st_ref[...],
                         wg_ref[...].astype(F32), bg_ref[...], go_ref[...])
        dq, dk, dv, dlr, dr, dst, dwg, dbg, dgo = vjp((dy_ref[...], dstate[...]))
        dq_ref[...] = dq.astype(BF16)
        dk_ref[...] = dk.astype(BF16)
        dv_ref[...] = dv.astype(BF16)
        dlr_ref[...] = dlr.astype(BF16)
        dr_ref[...] = dr.astype(BF16)
        dstate[...] = dst
        _accum(dwg_ref, dwg, first)
        _accum(dbg_ref, dbg, first)
        _accum(dgo_ref, dgo, first)

    c = GLA_CHUNK
    full = lambda shp: pl.BlockSpec(shp, lambda i: (0, 0))
    rows = lambda w: pl.BlockSpec((c, w), lambda i: (row(i), 0))
    return pl.pallas_call(
        body,
        name=name,
        grid=(nc,),
        in_specs=specs + [full((LR_PAD, GLA_QK)), full((1, GLA_QK)), full((1, GLA_DV)),
                          pl.BlockSpec((GLA_DV, GLA_QK), lambda i: (row(i), 0)), rows(MIX_W)],
        out_specs=[rows(GLA_QK), rows(GLA_QK), rows(MIX_W), rows(LR_PAD), rows(MIX_W),
                   full((LR_PAD, GLA_QK)), full((1, GLA_QK)), full((1, GLA_DV))],
        out_shape=[jax.ShapeDtypeStruct((s, GLA_QK), BF16), jax.ShapeDtypeStruct((s, GLA_QK), BF16),
                   jax.ShapeDtypeStruct((s, MIX_W), BF16), jax.ShapeDtypeStruct((s, LR_PAD), BF16),
                   jax.ShapeDtypeStruct((s, MIX_W), BF16), jax.ShapeDtypeStruct((LR_PAD, GLA_QK), F32),
                   jax.ShapeDtypeStruct((1, GLA_QK), F32), jax.ShapeDtypeStruct((1, GLA_DV), F32)],
        scratch_shapes=[pltpu.VMEM((GLA_DV, GLA_QK), F32)],
        compiler_params=_cparams(("arbitrary",)),
    )(proj, proj, proj, proj, proj, wg, bg, go, states, dy)


S5_G = 32
S5_P = 64
S5_C = 16
S5_N = S5_G * S5_P
S5_TC = 256
SUB = 8


def _s5_disc(a_re, a_im, ls, b_re, b_im):
    step = jnp.exp(ls)
    mag = jnp.exp(a_re * step)
    lr, li = mag * jnp.cos(a_im * step), mag * jnp.sin(a_im * step)
    inv = 1.0 / (a_re * a_re + a_im * a_im)
    nr, ni = lr - 1.0, li
    cr, ci = (nr * a_re + ni * a_im) * inv, (ni * a_re - nr * a_im) * inv
    return lr, li, cr * b_re - ci * b_im, cr * b_im + ci * b_re


def s5_disc_fwd(a_re, a_im, ls, b_re, b_im, *, name="s5_disc"):
    def body(ar, ai, l, br, bi, o0, o1, o2, o3):
        for o, val in zip((o0, o1, o2, o3), _s5_disc(ar[...], ai[...], l[...], br[...], bi[...])):
            o[...] = val

    sds = jax.ShapeDtypeStruct(a_re.shape, F32)
    return pl.pallas_call(body, name=name, out_shape=[sds] * 4)(a_re, a_im, ls, b_re, b_im)


def s5_disc_bwd(a_re, a_im, ls, b_re, b_im, cts, *, name="s5_disc_bwd"):
    def body(ar, ai, l, br, bi, c0, c1, c2, c3, dar, dai, dl, dbr, dbi):
        _, vjp = jax.vjp(_s5_disc, ar[...], ai[...], l[...], br[...], bi[...])
        g = vjp((c0[...], c1[...], c2[...], c3[...]))
        for o, val in zip((dar, dai, dl), g[:3]):
            o[...] = jnp.sum(val, axis=-1, keepdims=True)
        dbr[...] = g[3]
        dbi[...] = g[4]

    col = jax.ShapeDtypeStruct((a_re.shape[0], 1), F32)
    sds = jax.ShapeDtypeStruct(a_re.shape, F32)
    return pl.pallas_call(body, name=name, out_shape=[col, col, col, sds, sds])(a_re, a_im, ls, b_re, b_im, *cts)


def _gelu(y):
    return 0.5 * y * (1.0 + jnp.tanh(0.7978845608028654 * (y + 0.044715 * (y * y * y))))


def _s5_powers(lam, conj):
    lr, li = lam[:, :S5_N], lam[:, S5_N:]
    if conj:
        li = -li
    rows, pr, pi = [], lr, li
    for _ in range(SUB):
        rows.append((pr, pi))
        pr, pi = pr * lr - pi * li, pr * li + pi * lr
    return rows


def _s5_table(rows, reverse):
    ridx = lax.broadcasted_iota(jnp.int32, (SUB, S5_N), 0)
    tr = jnp.zeros((SUB, S5_N), F32)
    ti = jnp.zeros((SUB, S5_N), F32)
    for i in range(SUB):
        pr, pi = rows[SUB - 1 - i] if reverse else rows[i]
        tr = jnp.where(ridx == i, pr, tr)
        ti = jnp.where(ridx == i, pi, ti)
    return tr, ti


def _s5_scan(buf, lam, carry_ref, reverse):
    tc = buf.shape[0]
    nblk = tc // SUB
    rows = _s5_powers(lam, reverse)
    tr, ti = _s5_table(rows, reverse)
    ridx = lax.broadcasted_iota(jnp.int32, (SUB, S5_N), 0)

    def block(j, carry):
        jj = (nblk - 1 - j) if reverse else j
        at = pl.ds(pl.multiple_of(jj * SUB, SUB), SUB)
        re, im = buf[at, :S5_N], buf[at, S5_N:]
        for sft, (pr, pi) in ((1, rows[0]), (2, rows[1]), (4, rows[3])):
            if reverse:
                keep = ridx < SUB - sft
                sre, sim = pltpu.roll(re, SUB - sft, 0), pltpu.roll(im, SUB - sft, 0)
            else:
                keep = ridx >= sft
                sre, sim = pltpu.roll(re, sft, 0), pltpu.roll(im, sft, 0)
            sre, sim = jnp.where(keep, sre, 0.0), jnp.where(keep, sim, 0.0)
            re, im = re + pr * sre - pi * sim, im + pr * sim + pi * sre
        cr, ci = carry
        re, im = re + tr * cr - ti * ci, im + tr * ci + ti * cr
        buf[at, :S5_N] = re
        buf[at, S5_N:] = im
        edge = 0 if reverse else SUB - 1
        return re[edge : edge + 1, :], im[edge : edge + 1, :]

    c0 = (carry_ref[0:1, :S5_N], carry_ref[0:1, S5_N:])
    cr, ci = lax.fori_loop(0, nblk, block, c0)
    carry_ref[:, :S5_N] = jnp.broadcast_to(cr, (SUB, S5_N))
    carry_ref[:, S5_N:] = jnp.broadcast_to(ci, (SUB, S5_N))


def s5_fwd(proj, ucol, bd, cd, lam, dskip, wglu, bglu, *, name="s5_fwd"):
    s = proj.shape[0]
    tc = _pick(s, (S5_TC, 128, 64, 8))
    assert ucol % MIX_W == 0

    def body(u_ref, bd_ref, cd_ref, lam_ref, d_ref, w_ref, b_ref, y_ref, xs_ref, carry):
        @pl.when(pl.program_id(0) == 0)
        def _():
            carry[...] = jnp.zeros_like(carry)

        u = u_ref[...]
        xs_ref[...] = _dot(u, bd_ref[...])
        _s5_scan(xs_ref, lam_ref[...], carry, False)
        g = _gelu(_dot(xs_ref[...], cd_ref[...]) + d_ref[...] * u)
        y_ref[...] = g * _sigmoid(_dot(g, w_ref[...]) + b_ref[...])

    full = lambda shp: pl.BlockSpec(shp, lambda i: (0, 0))
    return pl.pallas_call(
        body,
        name=name,
        grid=(s // tc,),
        in_specs=[pl.BlockSpec((tc, MIX_W), lambda i: (i, ucol // MIX_W)), full((MIX_W, 2 * S5_N)), full((2 * S5_N, MIX_W)),
                  full((1, 2 * S5_N)), full((1, MIX_W)), full((MIX_W, MIX_W)), full((1, MIX_W))],
        out_specs=[pl.BlockSpec((tc, MIX_W), lambda i: (i, 0)), pl.BlockSpec((tc, 2 * S5_N), lambda i: (i, 0))],
        out_shape=[jax.ShapeDtypeStruct((s, MIX_W), F32), jax.ShapeDtypeStruct((s, 2 * S5_N), F32)],
        scratch_shapes=[pltpu.VMEM((SUB, 2 * S5_N), F32)],
        compiler_params=_cparams(("arbitrary",)),
    )(proj, bd, cd, lam, dskip, wglu, bglu)


def s5_bwd(proj, ucol, xs, bd, cd, lam, dskip, wglu, bglu, dya, *, name="s5_bwd"):
    s = proj.shape[0]
    tc = _pick(s, (S5_TC, 128, 64, 8))
    nch = s // tc
    per = tc // SUB

    def body(u_ref, xs_ref, xp_ref, bd_ref, cd_ref, lam_ref, d_ref, w_ref, b_ref, dya_ref,
             du_ref, adj_ref, g_ref, dpre_ref, dy_ref, dlam_ref, dd_ref, db_ref, buf, carry, lacc):
        i = pl.program_id(0)
        first = i == 0

        @pl.when(first)
        def _():
            carry[...] = jnp.zeros_like(carry)
            lacc[...] = jnp.zeros_like(lacc)

        u, x, dya_ = u_ref[...], xs_ref[...], dya_ref[...]
        y = _dot(x, cd_ref[...]) + d_ref[...] * u
        g, gelu_vjp = jax.vjp(_gelu, y)
        sg = _sigmoid(_dot(g, w_ref[...]) + b_ref[...])
        dpre = dya_ * g * sg * (1.0 - sg)
        (dy,) = gelu_vjp(dya_ * sg + _dot(dpre, w_ref[...], 1, 1))
        g_ref[...] = g.astype(BF16)
        dpre_ref[...] = dpre.astype(BF16)
        dy_ref[...] = dy.astype(BF16)
        _accum(db_ref, jnp.sum(dpre, axis=0, keepdims=True), first)
        _accum(dd_ref, jnp.sum(dy * u, axis=0, keepdims=True), first)
        buf[...] = _dot(dy, cd_ref[...], 1, 1)
        _s5_scan(buf, lam_ref[...], carry, True)
        a = buf[...]
        adj_ref[...] = a.astype(BF16)
        du_ref[...] = (dy * d_ref[...] + _dot(a, bd_ref[...], 1, 1)).astype(BF16)
        before = jnp.where(i == nch - 1, 0.0, xp_ref[SUB - 1 : SUB, :])
        ridx = lax.broadcasted_iota(jnp.int32, (tc, 2 * S5_N), 0)
        xprev = jnp.where(ridx == 0, before, pltpu.roll(x, 1, 0))
        ar, ai, xr, xi = a[:, :S5_N], a[:, S5_N:], xprev[:, :S5_N], xprev[:, S5_N:]
        lacc[:, :S5_N] += jnp.sum((ar * xr + ai * xi).reshape(per, SUB, S5_N), axis=0)
        lacc[:, S5_N:] += jnp.sum((ai * xr - ar * xi).reshape(per, SUB, S5_N), axis=0)

        @pl.when(i == nch - 1)
        def _():
            dlam_ref[...] = jnp.sum(lacc[...], axis=0, keepdims=True)

    rev = lambda i: nch - 1 - i
    full = lambda shp: pl.BlockSpec(shp, lambda i: (0, 0))
    rows = lambda w: pl.BlockSpec((tc, w), lambda i: (rev(i), 0))
    hb = jax.ShapeDtypeStruct((s, MIX_W), BF16)
    return pl.pallas_call(
        body,
        name=name,
        grid=(nch,),
        in_specs=[pl.BlockSpec((tc, MIX_W), lambda i: (rev(i), ucol // MIX_W)), rows(2 * S5_N),
                  pl.BlockSpec((SUB, 2 * S5_N), lambda i: (jnp.maximum(rev(i) * per - 1, 0), 0)),
                  full((MIX_W, 2 * S5_N)), full((2 * S5_N, MIX_W)), full((1, 2 * S5_N)), full((1, MIX_W)),
                  full((MIX_W, MIX_W)), full((1, MIX_W)), rows(MIX_W)],
        out_specs=[rows(MIX_W), rows(2 * S5_N), rows(MIX_W), rows(MIX_W), rows(MIX_W),
                   full((1, 2 * S5_N)), full((1, MIX_W)), full((1, MIX_W))],
        out_shape=[hb, jax.ShapeDtypeStruct((s, 2 * S5_N), BF16), hb, hb, hb,
                   jax.ShapeDtypeStruct((1, 2 * S5_N), F32), jax.ShapeDtypeStruct((1, MIX_W), F32),
                   jax.ShapeDtypeStruct((1, MIX_W), F32)],
        scratch_shapes=[pltpu.VMEM((tc, 2 * S5_N), F32), pltpu.VMEM((SUB, 2 * S5_N), F32), pltpu.VMEM((SUB, 2 * S5_N), F32)],
        compiler_params=_cparams(("arbitrary",)),
    )(proj, xs, xs, bd, cd, lam, dskip, wglu, bglu, dya)


def _bcast16(a):
    return jnp.broadcast_to(a.reshape(S5_N, 1), (S5_N, S5_C))


def _blockdiag(blocks):
    g, r, c = blocks.shape
    eye = jnp.eye(g, dtype=blocks.dtype)
    return (eye[:, None, :, None] * blocks[:, :, None, :]).reshape(g * r, g * c)


def _blockdiag_extract(dense, r, c):
    g = dense.shape[0] // r
    return jnp.einsum("grgc->grc", dense.reshape(g, r, g, c))


def s5_prepare(a_re, a_im, log_step, b_re, b_im, c_re, c_im):
    disc_in = (_bcast16(a_re), _bcast16(a_im), _bcast16(jnp.broadcast_to(log_step[:, None], (S5_G, S5_P))),
               b_re.reshape(S5_N, S5_C), b_im.reshape(S5_N, S5_C))
    lr, li, bbr, bbi = s5_disc_fwd(*disc_in)
    lam = jnp.concatenate([lr[:, 0], li[:, 0]]).reshape(1, 2 * S5_N)
    to_blocks = lambda t: _blockdiag(t.reshape(S5_G, S5_P, S5_C).transpose(0, 2, 1))
    bd = jnp.concatenate([to_blocks(bbr), to_blocks(bbi)], axis=1).astype(BF16)
    cd = jnp.concatenate([_blockdiag(c_re.transpose(0, 2, 1)), -_blockdiag(c_im.transpose(0, 2, 1))], axis=0).astype(BF16)
    return disc_in, lam, bd, cd


def s5_param_grads(disc_in, dlam, dbd, dcd):
    first_col = lambda v: jnp.pad(v.reshape(S5_N, 1), ((0, 0), (0, S5_C - 1)))
    from_blocks = lambda t: _blockdiag_extract(t, S5_C, S5_P).transpose(0, 2, 1).reshape(S5_N, S5_C)
    cts = (first_col(dlam[0, :S5_N]), first_col(dlam[0, S5_N:]), from_blocks(dbd[:, :S5_N]), from_blocks(dbd[:, S5_N:]))
    dar, dai, dls, dbr, dbi = s5_disc_bwd(*disc_in, cts)
    dcr = _blockdiag_extract(dcd[:S5_N], S5_P, S5_C).transpose(0, 2, 1)
    dci = -_blockdiag_extract(dcd[S5_N:], S5_P, S5_C).transpose(0, 2, 1)
    return (dar.reshape(S5_G, S5_P), dai.reshape(S5_G, S5_P), dls.reshape(S5_G, S5_P).sum(axis=1),
            dbr.reshape(S5_G, S5_P, S5_C), dbi.reshape(S5_G, S5_P, S5_C), dcr, dci)


N_DEV = 8
MESH_ID = pl.DeviceIdType.MESH
ANY = pl.BlockSpec(memory_space=pl.ANY)


def _me():
    return lax.axis_index("x"), lax.axis_index("y"), lax.axis_index("c")


def all_gather(blocks, *, by_core=False, name="all_gather"):
    na = len(blocks)
    shapes = [b.shape[1:] if by_core else b.shape for b in blocks]

    def body(*refs):
        ins, outs = refs[:na], refs[na : 2 * na]
        send_sems, recv_sems, local_sems = refs[2 * na :]
        x, y, c = _me()
        me, sibling = (x, y, c), (x, y, 1 - c)
        chips = [(1 - x, y), (x, 1 - y), (1 - x, 1 - y)]
        slot = lambda p: 4 * p[0] + 2 * p[1] + p[2]

        def copy(a, k, block, to, src=None):
            dst = outs[a].at[slot(block)]
            return pltpu.make_async_remote_copy(src_ref=dst if src is None else src, dst_ref=dst, send_sem=send_sems.at[a * 7 + k],
                                                recv_sem=recv_sems.at[a * 7 + k], device_id=to, device_id_type=MESH_ID)

        mine = [ins[a].at[c] if by_core else ins[a] for a in range(na)]
        local = [pltpu.make_async_copy(mine[a], outs[a].at[slot(me)], local_sems.at[a]) for a in range(na)]
        sends = []
        for a in range(na):
            local[a].start()
            first = [copy(a, 0, me, sibling, src=mine[a])]
            first += [copy(a, 1 + j, me, (*chip, c), src=mine[a]) for j, chip in enumerate(chips)]
            for cp in first:
                cp.start()
            sends += first
        for a in range(na):
            for j, chip in enumerate(chips):
                copy(a, 1 + j, (*chip, c), me).wait_recv()
                fwd = copy(a, 4 + j, (*chip, c), sibling)
                fwd.start()
                sends.append(fwd)
        for a in range(na):
            copy(a, 0, sibling, me).wait_recv()
            for j, chip in enumerate(chips):
                copy(a, 4 + j, (*chip, 1 - c), me).wait_recv()
        for cp in sends:
            cp.wait_send()
        for cp in local:
            cp.wait()

    return pl.pallas_call(
        body,
        name=name,
        in_specs=[ANY] * na,
        out_specs=[ANY] * na,
        out_shape=[jax.ShapeDtypeStruct((N_DEV, *shp), b.dtype) for shp, b in zip(shapes, blocks)],
        scratch_shapes=[pltpu.SemaphoreType.DMA((na * 7,)), pltpu.SemaphoreType.DMA((na * 7,)), pltpu.SemaphoreType.DMA((na,))],
    )(*blocks)


D2D_PIECES = 8


def _slabs(rows):
    n = D2D_PIECES if rows % (16 * D2D_PIECES) == 0 else 1
    return [pl.ds(i * (rows // n), rows // n) for i in range(n)]


def pair_by_core(arrs, *, name="pair_by_core"):
    na = len(arrs)
    slabs = [_slabs(a.shape[0]) for a in arrs]
    base = np.cumsum([0] + [len(s) for s in slabs])

    def body(*refs):
        ins, outs = refs[:na], refs[na : 2 * na]
        send_sems, recv_sems, local_sems = refs[2 * na :]
        x, y, c = _me()
        local, sends, recvs = [], [], []
        for a in range(na):
            for i, sl in enumerate(slabs[a]):
                k = int(base[a]) + i
                lc = pltpu.make_async_copy(ins[a].at[sl], outs[a].at[c, sl], local_sems.at[k])
                lc.start()
                local.append(lc)
                cp = pltpu.make_async_remote_copy(src_ref=ins[a].at[sl], dst_ref=outs[a].at[c, sl], send_sem=send_sems.at[k],
                                                  recv_sem=recv_sems.at[k], device_id=(x, y, 1 - c), device_id_type=MESH_ID)
                cp.start()
                sends.append(cp)
                recvs.append(pltpu.make_async_remote_copy(src_ref=ins[a].at[sl], dst_ref=outs[a].at[1 - c, sl],
                                                          send_sem=send_sems.at[k], recv_sem=recv_sems.at[k],
                                                          device_id=(x, y, 1 - c), device_id_type=MESH_ID))
        for cp in recvs:
            cp.wait_recv()
        for cp in sends:
            cp.wait_send()
        for cp in local:
            cp.wait()

    nsem = int(base[-1])
    return pl.pallas_call(
        body,
        name=name,
        in_specs=[ANY] * na,
        out_specs=[ANY] * na,
        out_shape=[jax.ShapeDtypeStruct((2, *a.shape), a.dtype) for a in arrs],
        scratch_shapes=[pltpu.SemaphoreType.DMA((nsem,)), pltpu.SemaphoreType.DMA((nsem,)), pltpu.SemaphoreType.DMA((nsem,))],
    )(*arrs)


def sibling_swap(arrs, *, name="sibling_swap"):
    na = len(arrs)
    pieces = [[(j, sl) for j in range(a.shape[0]) for sl in _halves(a.shape[1])] for a in arrs]
    base = np.cumsum([0] + [len(p) for p in pieces])

    def body(*refs):
        ins, outs = refs[:na], refs[na : 2 * na]
        send_sems, recv_sems = refs[2 * na :]
        x, y, c = _me()
        sends, recvs = [], []
        for a in range(na):
            for i, (j, sl) in enumerate(pieces[a]):
                k = int(base[a]) + i
                cp = pltpu.make_async_remote_copy(src_ref=ins[a].at[j, sl], dst_ref=outs[a].at[j, sl], send_sem=send_sems.at[k],
                                                  recv_sem=recv_sems.at[k], device_id=(x, y, 1 - c), device_id_type=MESH_ID)
                cp.start()
                sends.append(cp)
        for cp in sends:
            cp.wait_recv()
        for cp in sends:
            cp.wait_send()

    nsem = int(base[-1])
    return pl.pallas_call(
        body,
        name=name,
        in_specs=[ANY] * na,
        out_specs=[ANY] * na,
        out_shape=[jax.ShapeDtypeStruct(a.shape, a.dtype) for a in arrs],
        scratch_shapes=[pltpu.SemaphoreType.DMA((nsem,)), pltpu.SemaphoreType.DMA((nsem,))],
    )(*arrs)


def _halves(rows):
    return [pl.ds(0, rows // 2), pl.ds(rows // 2, rows // 2)] if rows % 32 == 0 else [pl.ds(0, rows)]


def chip_exchange(arrs, *, name="chip_exchange"):
    na = len(arrs)

    def body(*refs):
        ins, outs = refs[:na], refs[na : 2 * na]
        send_sems, recv_sems, local_sems = refs[2 * na :]
        x, y, c = _me()
        my = 2 * x + y
        peers = []
        for k in range(1, N_CHIPS):
            px, py = (x + ((k >> 1) & 1)) % 2, (y + (k & 1)) % 2
            peers.append(((px, py, c), 2 * px + py))
        local = [pltpu.make_async_copy(ins[a].at[my], outs[a].at[my], local_sems.at[a]) for a in range(na)]
        sends = []
        for a in range(na):
            local[a].start()
            for k, (peer, pidx) in enumerate(peers):
                cp = pltpu.make_async_remote_copy(src_ref=ins[a].at[pidx], dst_ref=outs[a].at[my], send_sem=send_sems.at[a * 3 + k],
                                                  recv_sem=recv_sems.at[a * 3 + k], device_id=peer, device_id_type=MESH_ID)
                cp.start()
                sends.append(cp)
        for a in range(na):
            for k, (peer, pidx) in enumerate(peers):
                pltpu.make_async_remote_copy(src_ref=ins[a].at[my], dst_ref=outs[a].at[pidx], send_sem=send_sems.at[a * 3 + k],
                                             recv_sem=recv_sems.at[a * 3 + k], device_id=peer, device_id_type=MESH_ID).wait_recv()
        for cp in sends:
            cp.wait_send()
        for cp in local:
            cp.wait()

    return pl.pallas_call(
        body,
        name=name,
        in_specs=[ANY] * na,
        out_specs=[ANY] * na,
        out_shape=[jax.ShapeDtypeStruct(a.shape, a.dtype) for a in arrs],
        scratch_shapes=[pltpu.SemaphoreType.DMA((na * 3,)), pltpu.SemaphoreType.DMA((na * 3,)), pltpu.SemaphoreType.DMA((na,))],
    )(*arrs)


def add_pair(a, b, *, name="add_pair"):
    n, r, c = a.shape
    tm = _row_tile(r, n * c * 2, 2 << 20)

    def body(a_ref, b_ref, o_ref):
        o_ref[...] = (a_ref[...].astype(F32) + b_ref[...].astype(F32)).astype(o_ref.dtype)

    blk = pl.BlockSpec((n, tm, c), lambda i: (0, i, 0))
    return pl.pallas_call(
        body, name=name, grid=(r // tm,), in_specs=[blk, blk], out_specs=blk, out_shape=jax.ShapeDtypeStruct(a.shape, a.dtype),
        compiler_params=_cparams(("arbitrary",)),
    )(a, b)


def _row_tile(rows, row_bytes, budget):
    best = None
    for t in range(8, rows + 1, 8):
        if rows % t == 0 and t * row_bytes <= budget:
            best = t
    return best or rows


def sum_blocks(a, *, name="sum_blocks"):
    n, r, c = a.shape
    tm = _row_tile(r, n * c * a.dtype.itemsize, 4 << 20)

    def body(a_ref, o_ref):
        acc = a_ref[0].astype(F32)
        for k in range(1, n):
            acc = acc + a_ref[k].astype(F32)
        o_ref[...] = acc

    return pl.pallas_call(
        body,
        name=name,
        grid=(r // tm,),
        in_specs=[pl.BlockSpec((n, tm, c), lambda i: (0, i, 0))],
        out_specs=pl.BlockSpec((tm, c), lambda i: (i, 0)),
        out_shape=jax.ShapeDtypeStruct((r, c), F32),
        compiler_params=_cparams(("arbitrary",)),
    )(a)


ADAM_LR = 0.001
ADAM_B1 = 0.9
ADAM_B2 = 0.999
ADAM_EPS = 1e-08
ADAM_WD = 0.01
ADAM_STEP = 10


def adamw(w, g, m, v, *, name="adamw"):
    r, c = w.shape
    tm = _row_tile(r, c * 4, 1 << 20)

    def body(w_ref, g_ref, m_ref, v_ref, d_ref, nm_ref, nv_ref):
        g_ = g_ref[...]
        m_ = ADAM_B1 * m_ref[...] + (1.0 - ADAM_B1) * g_
        v_ = ADAM_B2 * v_ref[...] + (1.0 - ADAM_B2) * (g_ * g_)
        m_hat = m_ / (1.0 - ADAM_B1**ADAM_STEP)
        v_hat = v_ / (1.0 - ADAM_B2**ADAM_STEP)
        d_ref[...] = -ADAM_LR * (m_hat / (jnp.sqrt(v_hat) + ADAM_EPS) + ADAM_WD * w_ref[...])
        nm_ref[...] = m_
        nv_ref[...] = v_

    blk = pl.BlockSpec((tm, c), lambda i: (i, 0))
    sds = jax.ShapeDtypeStruct((r, c), F32)
    return pl.pallas_call(
        body, name=name, grid=(r // tm,), in_specs=[blk] * 4, out_specs=[blk] * 3, out_shape=[sds] * 3,
        compiler_params=_cparams(("arbitrary",)),
    )(w, g, m, v)


WEIGHTS = ["g_mix", "w_in", "s5_a_re", "s5_a_im", "s5_log_step", "s5_b_re", "s5_b_im", "s5_c_re", "s5_c_im", "s5_d", "w_glu",
           "b_glu", "w_gla_gate", "b_gla_gate", "g_gla_out", "w_branch", "w_out", "g_mem", "g_cross", "w_xq", "w_xkv", "w_xo",
           "g_mlp", "w_up", "w_down", "g_final"]
SHARDED = {"w_in": 1, "w_glu": 0, "w_branch": 2, "w_out": 0, "w_xq": 0, "w_xkv": 1, "w_xo": 0, "w_up": 1, "w_down": 0}
DEPTH = 2
N_CHIPS = 4
D_IN = 9744
C_U, C_QG, C_KG, C_VG, C_LR, C_RG, C_QA, C_KA, C_VA, C_GATE = 0, 512, 768, 1024, 1536, 1552, 2064, 3600, 5136, 6672
A_GATE, A_U, A_QG, A_KG, A_VG, A_RG, A_Q0, A_LR, A_W = 0, 3072, 3584, 3840, 4096, 4608, 5120, 6656, 7168
GLA_COLS = (A_QG, A_KG, A_VG, A_LR, A_RG)


def _split_w_in(w):
    att = lambda g: [w[:, c + MIX_W * g : c + MIX_W * (g + 1)] for c in (C_QA, C_KA, C_VA)]
    a = jnp.concatenate([w[:, C_GATE:D_IN], w[:, C_U:C_QG], w[:, C_QG:C_KG], w[:, C_KG:C_VG], w[:, C_VG:C_LR], w[:, C_RG:C_QA],
                         *att(0), w[:, C_LR:C_RG], jnp.zeros((w.shape[0], A_W - A_LR - 16), w.dtype)], axis=1)
    return a, jnp.concatenate(att(1), axis=1), jnp.concatenate(att(2), axis=1)


def _join_w_in(a, b, c):
    att = lambda k: [a[:, A_Q0 + MIX_W * k : A_Q0 + MIX_W * (k + 1)], b[:, MIX_W * k : MIX_W * (k + 1)], c[:, MIX_W * k : MIX_W * (k + 1)]]
    return jnp.concatenate([a[:, A_U:A_QG], a[:, A_QG:A_KG], a[:, A_KG:A_VG], a[:, A_VG:A_RG], a[:, A_LR : A_LR + 16], a[:, A_RG:A_Q0],
                            *att(0), *att(1), *att(2), a[:, A_GATE:A_U]], axis=1)


def _by_residue(a, d):
    s = a.shape[0]
    return a if d == 1 else a.reshape(s // d, d, -1).transpose(1, 0, 2).reshape(s, -1)


def _in_order(a, d):
    s = a.shape[0]
    return a if d == 1 else a.reshape(d, s // d, -1).transpose(1, 0, 2).reshape(s, -1)


def _pack(arrs):
    flat = jnp.concatenate([a.reshape(-1) for a in arrs])
    n = flat.shape[0]
    rows = -(-n // (256 * 128)) * 256
    return jnp.pad(flat, (0, rows * 128 - n)).reshape(rows, 128)


def _unpack(packed, like):
    flat, out, o = packed.reshape(-1), [], 0
    for a in like:
        n = math.prod(a.shape)
        out.append(flat[o : o + n].reshape(a.shape))
        o += n
    return out


def _layer_fwd(x0, p, kv):
    h = rmsnorm(x0, p["g_mix"], name="mix_norm")
    hs = [h, _by_residue(h, 4), _by_residue(h, 16)]
    proj = [matmul(hs[g], p["w_in_seg"][g], name="in_proj") for g in range(3)]
    ya, xs = s5_fwd(proj[0], A_U, p["bd"], p["cd"], p["lam"], p["s5_d"], p["w_glu"], p["b_glu"])
    yb, gst = gla_fwd(proj[0], GLA_COLS, p["w_gate"], p["b_gla_gate"], p["g_gla_out"])
    att = [attn_fwd(proj[g], A_Q0 if g == 0 else 0, g) for g in range(3)]
    outs = [_in_order(att[g][0], ATT_GROUPS[g][1]) for g in range(3)]
    lses = [_in_order(att[g][1], ATT_GROUPS[g][1]) for g in range(3)]
    x1, merged, yc = merge_fwd(x0, proj[0], ya, yb, outs, lses, p["w_branch"], p["w_out"])
    x2 = cross_fwd(x1, p["g_cross"], p["w_xq"], p["w_xo"], kv)
    x3 = mlp_fwd(x2, p["g_mlp"], p["w_up"], p["w_down"])
    saved = dict(x0=x0, x1=x1, x2=x2, hs=hs, proj=proj, ya=ya, xs=xs, yb=yb, gst=gst, att=att, outs=outs, lses=lses,
                 merged=merged, yc=yc)
    return x3, saved


def _layer_bwd(dx3, p, kv, memn, sv):
    g = {}
    tn = lambda a, b, **kw: matmul(a, b, ta=True, out_dtype=BF16, name="wgrad", **kw)
    dx2, g["g_mlp"], h3, dup, act = mlp_bwd(sv["x2"], p["g_mlp"], p["w_up"], p["w_down"], dx3)
    g["w_up"], g["w_down"] = tn(h3, dup), tn(act, dx3)
    dx1, g["g_cross"], h2, dq, o, dkv = cross_bwd(sv["x1"], p["g_cross"], p["w_xq"], p["w_xo"], kv, dx2)
    g["w_xq"], g["w_xo"], g["w_xkv"] = tn(h2, dq), tn(o, dx2), tn(memn, dkv)
    dmemn = matmul(dkv, p["w_xkv"], tb=True, name="dmem")
    proj = sv["proj"]
    r = merge_bwd(dx1, proj[0], sv["ya"], sv["yb"], sv["yc"], sv["outs"], sv["lses"], p["w_branch"], p["w_out"])
    dgl, dz, dya, dyb, douts, dlses = r[0], r[1:4], r[4], r[5], r[6:9], r[9:12]
    g["w_branch"] = jnp.stack([tn(y, dz[n]) for n, y in enumerate((sv["ya"], sv["yb"], sv["yc"]))])
    g["w_out"] = tn(sv["merged"], dx1)
    datt = []
    for k in range(3):
        dil = ATT_GROUPS[k][1]
        datt.append(attn_bwd(proj[k], A_Q0 if k == 0 else 0, k, sv["att"][k][0], sv["att"][k][1],
                             _by_residue(douts[k], dil), _by_residue(dlses[k], dil)))
    du, adj, gg, dpre, dy, dlam, g["s5_d"], g["b_glu"] = s5_bwd(proj[0], A_U, sv["xs"], p["bd"], p["cd"], p["lam"], p["s5_d"],
                                                                 p["w_glu"], p["b_glu"], dya)
    dbd = matmul(proj[0], adj, ta=True, a_col=(A_U, MIX_W), name="s5_dbd")
    dcd = matmul(sv["xs"], dy, ta=True, name="s5_dcd")
    g["w_glu"] = tn(gg, dpre)
    (g["s5_a_re"], g["s5_a_im"], g["s5_log_step"], g["s5_b_re"], g["s5_b_im"], g["s5_c_re"],
     g["s5_c_im"]) = s5_param_grads(p["disc_in"], dlam, dbd, dcd)
    dqg, dkg, dvg, dlr, drg, dwg, g["b_gla_gate"], g["g_gla_out"] = gla_bwd(proj[0], GLA_COLS, p["w_gate"], p["b_gla_gate"],
                                                                             p["g_gla_out"], sv["gst"], dyb)
    g["w_gla_gate"] = dwg[:GLA_GATE_RANK]
    s = dx3.shape[0]
    dproj = [jnp.concatenate([dgl, du, dqg, dkg, dvg, drg, *datt[0], dlr, jnp.zeros((s, A_W - A_LR - LR_PAD), BF16)], axis=1),
             jnp.concatenate(datt[1], axis=1), jnp.concatenate(datt[2], axis=1)]
    g["w_in"] = _join_w_in(*[tn(sv["hs"][k], dproj[k]) for k in range(3)])
    dhs = [_in_order(matmul(dproj[k], p["w_in_seg"][k], tb=True, name="in_proj_bwd"), ATT_GROUPS[k][1]) for k in range(3)]
    dx0, g["g_mix"] = rmsnorm_bwd(sv["x0"], p["g_mix"], dhs, dx1, name="mix_norm_bwd")
    return dx0, g, dmemn


GLA_GATE_RANK = 16


def kernel(x, mem, *rest):
    nw = len(WEIGHTS)
    w = dict(zip(WEIGHTS, rest[:nw]))
    target = rest[nw]
    m = dict(zip(WEIGHTS, rest[nw + 1 : 2 * nw + 1]))
    v = dict(zip(WEIGHTS, rest[2 * nw + 1 : 3 * nw + 1]))
    x0, memx, target = x[0], mem[0], target[0]
    chip = 2 * lax.axis_index("x") + lax.axis_index("y")

    big = list(SHARDED)
    gate_pad = jnp.pad(w["w_gla_gate"], ((0, 0), (0, 0), (0, LR_PAD - w["w_gla_gate"].shape[2])))
    gathered = all_gather([w[n].astype(BF16) for n in big] + [gate_pad], by_core=True, name="gather_weights")
    gathered = [t.reshape(N_CHIPS, DEPTH, *t.shape[1:]) for t in gathered]
    full = {n: [jnp.concatenate([t[j, l] for j in range(N_CHIPS)], axis=SHARDED[n]) for l in range(DEPTH)]
            for n, t in zip(big, gathered)}
    gate_full = [jnp.concatenate([gathered[-1][j, l][:, : w["w_gla_gate"].shape[2]] for j in range(N_CHIPS)], axis=1)
                 for l in range(DEPTH)]

    memn = rmsnorm(memx, w["g_mem"], name="mem_norm")
    params, kvs = [], []
    for l in range(DEPTH):
        p = {n: full[n][l] for n in big if n != "w_in"}
        p["w_in_seg"] = _split_w_in(full["w_in"][l])
        p["w_gate"] = jnp.pad(gate_full[l], ((0, LR_PAD - GLA_GATE_RANK), (0, 0))).astype(BF16)
        for n in ("g_mix", "g_cross", "g_mlp"):
            p[n] = w[n][l]
        for n in ("s5_d", "b_glu", "b_gla_gate", "g_gla_out"):
            p[n] = w[n][l].reshape(1, -1)
        p["disc_in"], p["lam"], p["bd"], p["cd"] = s5_prepare(*[w[n][l] for n in WEIGHTS[2:9]])
        params.append(p)
        kvs.append(matmul(memn, p["w_xkv"], out_dtype=BF16, name="mem_kv"))

    xl, saved = x0, []
    for l in range(DEPTH):
        xl, sv = _layer_fwd(xl, params[l], kvs[l])
        saved.append(sv)
    loss8, dx, dg_final = loss_head(xl, w["g_final"], target)
    grads, dmem = [None] * DEPTH, []
    for l in reversed(range(DEPTH)):
        dx, grads[l], dm = _layer_bwd(dx, params[l], kvs[l], memn, saved[l])
        dmem.append(dm)
    _, dg_mem = rmsnorm_bwd(memx, w["g_mem"], dmem, jnp.zeros_like(memx), name="mem_norm_bwd")

    core = lax.axis_index("c")

    def halves(n):
        ax = SHARDED[n]
        keep, send = [], []
        for j in range(N_CHIPS):
            p0, p1 = [jnp.split(grads[l][n], N_CHIPS, axis=ax)[j] for l in range(DEPTH)]
            keep.append(jnp.where(core == 0, p0, p1))
            send.append(jnp.where(core == 0, p1, p0))
        flat = lambda ps: jnp.stack(ps).reshape(N_CHIPS, -1, ps[0].shape[-1])
        return flat(keep), flat(send)

    kept, sent = zip(*[halves(n) for n in big])
    theirs = sibling_swap(list(sent), name="swap_layers")
    chip_sums = [add_pair(a, b, name="add_cores") for a, b in zip(kept, theirs)]
    landed = chip_exchange(chip_sums, name="exchange_grads")
    reduced = [sum_blocks(t, name="sum_grads") for t in landed]
    paired = pair_by_core(reduced, name="pair_layers")
    out_g = {n: t.reshape(w[n].shape) for n, t in zip(big, paired)}

    small = [n for n in WEIGHTS if n not in SHARDED]
    local_small = []
    for n in small:
        if n == "g_mem":
            local_small.append(dg_mem.reshape(w[n].shape))
        elif n == "g_final":
            local_small.append(dg_final.reshape(w[n].shape))
        elif n == "w_gla_gate":
            local_small.append(jnp.stack([grads[l][n] for l in range(DEPTH)]))
        else:
            local_small.append(jnp.stack([grads[l][n].reshape(w[n].shape[1:]) for l in range(DEPTH)]))
    packed = _pack(local_small + [loss8[0, :1]])
    (every,) = all_gather([packed], name="gather_small")
    summed = _unpack(sum_blocks(every, name="sum_small"), local_small + [loss8[0, :1]])
    loss = summed[-1].reshape(())
    for n, t in zip(small, summed[:-1]):
        if n == "w_gla_gate":
            t = lax.dynamic_slice_in_dim(t, chip * w[n].shape[2], w[n].shape[2], axis=2)
        out_g[n] = t

    delta, new_m, new_v = {}, {}, {}
    for n in big:
        c = w[n].shape[-1]
        r = [t.reshape(w[n].shape) for t in adamw(w[n].reshape(-1, c), out_g[n].reshape(-1, c), m[n].reshape(-1, c),
                                                  v[n].reshape(-1, c), name="adamw")]
        delta[n], new_m[n], new_v[n] = r
    like = [w[n] for n in small]
    r = adamw(_pack(like), _pack([out_g[n] for n in small]), _pack([m[n] for n in small]), _pack([v[n] for n in small]),
              name="adamw_small")
    for d, t in zip((delta, new_m, new_v), r):
        d.update(zip(small, _unpack(t, like)))
    return (loss, dx[None], *[out_g[n] for n in WEIGHTS], *[delta[n] for n in WEIGHTS], *[new_m[n] for n in WEIGHTS],
            *[new_v[n] for n in WEIGHTS])
```

```python
import functools
import math

import jax
import jax.numpy as jnp
import numpy as np
from jax import lax
from jax.experimental import pallas as pl
from jax.experimental.pallas import tpu as pltpu

F32 = jnp.float32
BF16 = jnp.bfloat16

VMEM_LIMIT_BYTES = 56 * 1024 * 1024
MATMUL_VMEM_BYTES = 36 * 1024 * 1024


def _cparams(sem):
    return pltpu.CompilerParams(dimension_semantics=sem, vmem_limit_bytes=VMEM_LIMIT_BYTES)


def _dot(a, b, ca=1, cb=0):
    return lax.dot_general(a.astype(BF16), b.astype(BF16), (((ca,), (cb,)), ((), ())), preferred_element_type=F32)


def _pick(n, prefs):
    for p in prefs:
        if n % p == 0:
            return p
    return n


def matmul(a, b, *, ta=False, tb=False, out_dtype=F32, name="mm", a_col=None, b_col=None):
    a_off, a_w = a_col if a_col is not None else (0, a.shape[1])
    b_off, b_w = b_col if b_col is not None else (0, b.shape[1])
    if ta:
        kk, m = a.shape[0], a_w
    else:
        m, kk = a.shape[0], a_w
    if tb:
        n, kb = b.shape[0], b_w
    else:
        kb, n = b.shape[0], b_w
    assert kk == kb, (a.shape, b.shape, ta, tb)
    tm = _pick(m, (512, 256, 128))
    tn = _pick(n, (1024, 512, 256, 128))
    sa, sb, so = a.dtype.itemsize, b.dtype.itemsize, jnp.dtype(out_dtype).itemsize
    fits = lambda t: 2 * (tm * t * sa + t * tn * sb) + tm * tn * (4 + 2 * so) <= MATMUL_VMEM_BYTES
    tk = max([t for t in range(128, kk + 1, 128) if kk % t == 0 and fits(t)] or [_pick(kk, (128,))])
    nk = kk // tk
    if ta:
        assert a_off % tm == 0
        a_spec = pl.BlockSpec((tk, tm), lambda i, j, k, o=a_off // tm: (k, i + o))
    else:
        assert a_off % tk == 0
        a_spec = pl.BlockSpec((tm, tk), lambda i, j, k, o=a_off // tk: (i, k + o))
    if tb:
        assert b_off % tk == 0
        b_spec = pl.BlockSpec((tn, tk), lambda i, j, k, o=b_off // tk: (j, k + o))
    else:
        assert b_off % tn == 0
        b_spec = pl.BlockSpec((tk, tn), lambda i, j, k, o=b_off // tn: (k, j + o))

    def body(a_ref, b_ref, o_ref, *acc):
        k = pl.program_id(2)
        p = _dot(a_ref[...], b_ref[...], 0 if ta else 1, 1 if tb else 0)
        if nk == 1:
            o_ref[...] = p.astype(o_ref.dtype)
            return
        (acc_ref,) = acc

        @pl.when(k == 0)
        def _():
            acc_ref[...] = p

        @pl.when(jnp.logical_and(k > 0, k < nk - 1))
        def _():
            acc_ref[...] += p

        @pl.when(k == nk - 1)
        def _():
            o_ref[...] = (acc_ref[...] + p).astype(o_ref.dtype)

    return pl.pallas_call(
        body,
        name=name,
        grid=(m // tm, n // tn, nk),
        in_specs=[a_spec, b_spec],
        out_specs=pl.BlockSpec((tm, tn), lambda i, j, k: (i, j)),
        out_shape=jax.ShapeDtypeStruct((m, n), out_dtype),
        scratch_shapes=[pltpu.VMEM((tm, tn), F32)] if nk > 1 else [],
        compiler_params=_cparams(("parallel", "parallel", "arbitrary")),
    )(a, b)


RMS_EPS = 1e-6
ROW_TILE = 512


def _rms_fwd(x, g):
    r = lax.rsqrt(jnp.mean(x * x, axis=-1, keepdims=True) + RMS_EPS)
    return x * r * g


def _rms_bwd(x, g, dh):
    r = lax.rsqrt(jnp.mean(x * x, axis=-1, keepdims=True) + RMS_EPS)
    xh = x * r
    dg = jnp.sum(dh * xh, axis=0, keepdims=True)
    dxh = dh * g
    dx = r * (dxh - xh * jnp.mean(dxh * xh, axis=-1, keepdims=True))
    return dx, dg


def _accum(ref, val, first):
    @pl.when(first)
    def _():
        ref[...] = val

    @pl.when(jnp.logical_not(first))
    def _():
        ref[...] += val


def rmsnorm(x, g, *, out_dtype=BF16, name="rmsnorm"):
    s, d = x.shape
    tm = _pick(s, (ROW_TILE, 256, 128, 8))

    def body(x_ref, g_ref, o_ref):
        o_ref[...] = _rms_fwd(x_ref[...], g_ref[...]).astype(o_ref.dtype)

    return pl.pallas_call(
        body,
        name=name,
        grid=(s // tm,),
        in_specs=[pl.BlockSpec((tm, d), lambda i: (i, 0)), pl.BlockSpec((1, d), lambda i: (0, 0))],
        out_specs=pl.BlockSpec((tm, d), lambda i: (i, 0)),
        out_shape=jax.ShapeDtypeStruct((s, d), out_dtype),
        compiler_params=_cparams(("arbitrary",)),
    )(x, g.reshape(1, d))


def rmsnorm_bwd(x, g, dhs, dres, *, name="rmsnorm_bwd"):
    s, d = x.shape
    tm = _pick(s, (ROW_TILE, 256, 128, 8))
    n = len(dhs)

    def body(x_ref, g_ref, *refs):
        dh_refs, dres_ref, dx_ref, dg_ref = refs[:n], refs[n], refs[n + 1], refs[n + 2]
        dh = dh_refs[0][...].astype(F32)
        for r in dh_refs[1:]:
            dh = dh + r[...].astype(F32)
        dx, dg = _rms_bwd(x_ref[...], g_ref[...], dh)
        dx_ref[...] = dres_ref[...] + dx
        _accum(dg_ref, dg, pl.program_id(0) == 0)

    row = pl.BlockSpec((tm, d), lambda i: (i, 0))
    vec = pl.BlockSpec((1, d), lambda i: (0, 0))
    return pl.pallas_call(
        body,
        name=name,
        grid=(s // tm,),
        in_specs=[row, vec] + [row] * n + [row],
        out_specs=[row, vec],
        out_shape=[jax.ShapeDtypeStruct((s, d), F32), jax.ShapeDtypeStruct((1, d), F32)],
        compiler_params=_cparams(("arbitrary",)),
    )(x, g.reshape(1, d), *dhs, dres)


def loss_head(x, g, target, *, name="loss_head"):
    s, d = x.shape
    tm = _pick(s, (ROW_TILE, 256, 128, 8))

    def body(x_ref, g_ref, t_ref, l_ref, dx_ref, dg_ref):
        x_, g_ = x_ref[...], g_ref[...]
        e = _rms_fwd(x_, g_) - t_ref[...]
        part = 0.5 * jnp.sum(jnp.sum(e * e, axis=-1, keepdims=True), axis=0, keepdims=True) / d
        dx, dg = _rms_bwd(x_, g_, e * (1.0 / d))
        dx_ref[...] = dx
        first = pl.program_id(0) == 0
        _accum(dg_ref, dg, first)
        _accum(l_ref, jnp.broadcast_to(part, (8, 128)), first)

    row = pl.BlockSpec((tm, d), lambda i: (i, 0))
    vec = pl.BlockSpec((1, d), lambda i: (0, 0))
    return pl.pallas_call(
        body,
        name=name,
        grid=(s // tm,),
        in_specs=[row, vec, row],
        out_specs=[pl.BlockSpec((8, 128), lambda i: (0, 0)), row, vec],
        out_shape=[jax.ShapeDtypeStruct((8, 128), F32), jax.ShapeDtypeStruct((s, d), F32), jax.ShapeDtypeStruct((1, d), F32)],
        compiler_params=_cparams(("arbitrary",)),
    )(x, g.reshape(1, d), target)


FF_TILE = 1024


def mlp_fwd(x, g, w_up, w_down, *, name="mlp_fwd"):
    s, d = x.shape
    ff = w_up.shape[1]
    tm, tf = _pick(s, (ROW_TILE, 256, 128)), _pick(ff, (FF_TILE, 512, 256, 128))
    nf = ff // tf

    def body(x_ref, g_ref, wu_ref, wd_ref, o_ref, h_ref, acc_ref):
        f = pl.program_id(1)

        @pl.when(f == 0)
        def _():
            h_ref[...] = _rms_fwd(x_ref[...], g_ref[...]).astype(BF16)
            acc_ref[...] = x_ref[...]

        up = jnp.maximum(_dot(h_ref[...], wu_ref[...]), 0.0)
        acc_ref[...] += _dot(up * up, wd_ref[...])

        @pl.when(f == nf - 1)
        def _():
            o_ref[...] = acc_ref[...]

    return pl.pallas_call(
        body,
        name=name,
        grid=(s // tm, nf),
        in_specs=[
            pl.BlockSpec((tm, d), lambda i, f: (i, 0)),
            pl.BlockSpec((1, d), lambda i, f: (0, 0)),
            pl.BlockSpec((d, tf), lambda i, f: (0, f)),
            pl.BlockSpec((tf, d), lambda i, f: (f, 0)),
        ],
        out_specs=pl.BlockSpec((tm, d), lambda i, f: (i, 0)),
        out_shape=jax.ShapeDtypeStruct((s, d), F32),
        scratch_shapes=[pltpu.VMEM((tm, d), BF16), pltpu.VMEM((tm, d), F32)],
        compiler_params=_cparams(("arbitrary", "arbitrary")),
    )(x, g.reshape(1, d), w_up, w_down)


def mlp_bwd(x, g, w_up, w_down, dy, *, name="mlp_bwd"):
    s, d = x.shape
    ff = w_up.shape[1]
    tm, tf = _pick(s, (ROW_TILE, 256, 128)), _pick(ff, (FF_TILE, 512, 256, 128))
    nf = ff // tf

    def body(x_ref, g_ref, wu_ref, wd_ref, dy_ref, dx_ref, dg_ref, h_ref, dup_ref, act_ref, dyb_ref, acc_ref):
        i, f = pl.program_id(0), pl.program_id(1)

        @pl.when(f == 0)
        def _():
            h_ref[...] = _rms_fwd(x_ref[...], g_ref[...]).astype(BF16)
            dyb_ref[...] = dy_ref[...].astype(BF16)
            acc_ref[...] = jnp.zeros_like(acc_ref)

        up = jnp.maximum(_dot(h_ref[...], wu_ref[...]), 0.0)
        dact = _dot(dyb_ref[...], wd_ref[...], 1, 1)
        dup = (2.0 * up * dact).astype(BF16)
        dup_ref[...] = dup
        act_ref[...] = (up * up).astype(BF16)
        acc_ref[...] += _dot(dup, wu_ref[...], 1, 1)

        @pl.when(f == nf - 1)
        def _():
            dx, dg = _rms_bwd(x_ref[...], g_ref[...], acc_ref[...])
            dx_ref[...] = dy_ref[...] + dx
            _accum(dg_ref, dg, i == 0)

    row = pl.BlockSpec((tm, d), lambda i, f: (i, 0))
    vec = pl.BlockSpec((1, d), lambda i, f: (0, 0))
    wide = pl.BlockSpec((tm, tf), lambda i, f: (i, f))
    return pl.pallas_call(
        body,
        name=name,
        grid=(s // tm, nf),
        in_specs=[row, vec, pl.BlockSpec((d, tf), lambda i, f: (0, f)), pl.BlockSpec((tf, d), lambda i, f: (f, 0)), row],
        out_specs=[row, vec, row, wide, wide],
        out_shape=[
            jax.ShapeDtypeStruct((s, d), F32),
            jax.ShapeDtypeStruct((1, d), F32),
            jax.ShapeDtypeStruct((s, d), BF16),
            jax.ShapeDtypeStruct((s, ff), BF16),
            jax.ShapeDtypeStruct((s, ff), BF16),
        ],
        scratch_shapes=[pltpu.VMEM((tm, d), BF16), pltpu.VMEM((tm, d), F32)],
        compiler_params=_cparams(("arbitrary", "arbitrary")),
    )(x, g.reshape(1, d), w_up, w_down, dy)


X_HEADS = 4


def _softmax_rows(s):
    m = jnp.max(s, axis=-1, keepdims=True)
    e = jnp.exp(s - m)
    return e / jnp.sum(e, axis=-1, keepdims=True)


def cross_fwd(x, g, wq, wo, kv, *, name="cross_fwd"):
    s, d = x.shape
    ml = kv.shape[0]
    dh = d // X_HEADS
    tm = _pick(s, (ROW_TILE, 256, 128))
    scale = dh**-0.5

    def body(x_ref, g_ref, wq_ref, wo_ref, kv_ref, o_ref):
        x_ = x_ref[...]
        q = _dot(_rms_fwd(x_, g_ref[...]), wq_ref[...])
        outs = []
        for hd in range(X_HEADS):
            kh = kv_ref[:, hd * dh : (hd + 1) * dh]
            vh = kv_ref[:, d + hd * dh : d + (hd + 1) * dh]
            p = _softmax_rows(_dot(q[:, hd * dh : (hd + 1) * dh], kh, 1, 1) * scale)
            outs.append(_dot(p, vh))
        o_ref[...] = x_ + _dot(jnp.concatenate(outs, axis=-1), wo_ref[...])

    row = pl.BlockSpec((tm, d), lambda i: (i, 0))
    full = lambda shp: pl.BlockSpec(shp, lambda i: (0, 0))
    return pl.pallas_call(
        body,
        name=name,
        grid=(s // tm,),
        in_specs=[row, full((1, d)), full((d, d)), full((d, d)), full((ml, 2 * d))],
        out_specs=row,
        out_shape=jax.ShapeDtypeStruct((s, d), F32),
        compiler_params=_cparams(("arbitrary",)),
    )(x, g.reshape(1, d), wq, wo, kv)


def cross_bwd(x, g, wq, wo, kv, dy, *, name="cross_bwd"):
    s, d = x.shape
    ml = kv.shape[0]
    dh = d // X_HEADS
    tm = _pick(s, (ROW_TILE, 256, 128))
    scale = dh**-0.5

    def body(x_ref, g_ref, wq_ref, wo_ref, kv_ref, dy_ref, dx_ref, dg_ref, h_ref, dq_ref, o_ref, dkv_ref):
        first = pl.program_id(0) == 0
        x_, g_ = x_ref[...], g_ref[...]
        h = _rms_fwd(x_, g_).astype(BF16)
        h_ref[...] = h
        q = _dot(h, wq_ref[...])
        dy_ = dy_ref[...]
        do = _dot(dy_, wo_ref[...], 1, 1)
        outs, dqs, dks, dvs = [], [], [], []
        for hd in range(X_HEADS):
            sl = slice(hd * dh, (hd + 1) * dh)
            kh = kv_ref[:, sl]
            vh = kv_ref[:, d + hd * dh : d + (hd + 1) * dh]
            qh = q[:, sl]
            p = _softmax_rows(_dot(qh, kh, 1, 1) * scale)
            outs.append(_dot(p, vh))
            doh = do[:, sl]
            dp = _dot(doh, vh, 1, 1)
            ds = p * (dp - jnp.sum(dp * p, axis=-1, keepdims=True)) * scale
            dqs.append(_dot(ds, kh))
            dks.append(_dot(ds, qh, 0, 0))
            dvs.append(_dot(p, doh, 0, 0))
        o_ref[...] = jnp.concatenate(outs, axis=-1).astype(BF16)
        dq = jnp.concatenate(dqs, axis=-1).astype(BF16)
        dq_ref[...] = dq
        _accum(dkv_ref, jnp.concatenate(dks + dvs, axis=-1), first)
        dx, dg = _rms_bwd(x_, g_, _dot(dq, wq_ref[...], 1, 1))
        dx_ref[...] = dy_ + dx
        _accum(dg_ref, dg, first)

    row = pl.BlockSpec((tm, d), lambda i: (i, 0))
    full = lambda shp: pl.BlockSpec(shp, lambda i: (0, 0))
    rowb = jax.ShapeDtypeStruct((s, d), BF16)
    return pl.pallas_call(
        body,
        name=name,
        grid=(s // tm,),
        in_specs=[row, full((1, d)), full((d, d)), full((d, d)), full((ml, 2 * d)), row],
        out_specs=[row, full((1, d)), row, row, row, full((ml, 2 * d))],
        out_shape=[jax.ShapeDtypeStruct((s, d), F32), jax.ShapeDtypeStruct((1, d), F32), rowb, rowb, rowb,
                   jax.ShapeDtypeStruct((ml, 2 * d), F32)],
        compiler_params=_cparams(("arbitrary",)),
    )(x, g.reshape(1, d), wq, wo, kv, dy)


N_BRANCH = 3
MIX_W = 512
ATT_DH = 128
ATT_HG = 4
MERGE_TILE = 256


def _group_weights(l0, l1, l2):
    m = jnp.maximum(jnp.maximum(l0, l1), l2)
    e = [jnp.exp(l0 - m), jnp.exp(l1 - m), jnp.exp(l2 - m)]
    inv = 1.0 / (e[0] + e[1] + e[2])
    return [t * inv for t in e]


def _sigmoid(x):
    return 1.0 / (1.0 + jnp.exp(-x))


def merge_fwd(x, proj, ya, yb, outs, lses, wb, wout, *, name="merge_fwd"):
    s, d = x.shape
    tm = _pick(s, (MERGE_TILE, 128))

    def body(x_ref, g0, g1, g2, ya_ref, yb_ref, o0, o1, o2, l0, l1, l2, wb_ref, wo_ref, x1_ref, mg_ref, yc_ref):
        w = _group_weights(l0[...], l1[...], l2[...])
        yc = w[0] * o0[...] + w[1] * o1[...] + w[2] * o2[...]
        yc_ref[...] = yc.astype(BF16)
        merged = None
        for n, (y, gl) in enumerate(((ya_ref[...], g0), (yb_ref[...], g1), (yc, g2))):
            t = _sigmoid(gl[...]) * _dot(y, wb_ref[n])
            merged = t if merged is None else merged + t
        mb = merged.astype(BF16)
        mg_ref[...] = mb
        x1_ref[...] = x_ref[...] + _dot(mb, wo_ref[...])

    row = pl.BlockSpec((tm, d), lambda i: (i, 0))
    half = pl.BlockSpec((tm, MIX_W), lambda i: (i, 0))
    gate = [pl.BlockSpec((tm, d), lambda i, n=n: (i, n)) for n in range(N_BRANCH)]
    return pl.pallas_call(
        body,
        name=name,
        grid=(s // tm,),
        in_specs=[row] + gate + [half] * 8
        + [pl.BlockSpec((N_BRANCH, MIX_W, d), lambda i: (0, 0, 0)), pl.BlockSpec((d, d), lambda i: (0, 0))],
        out_specs=[row, row, half],
        out_shape=[jax.ShapeDtypeStruct((s, d), F32), jax.ShapeDtypeStruct((s, d), BF16), jax.ShapeDtypeStruct((s, MIX_W), BF16)],
        compiler_params=_cparams(("arbitrary",)),
    )(x, proj, proj, proj, ya, yb, *outs, *lses, wb, wout)


def merge_bwd(dx1, proj, ya, yb, yc, outs, lses, wb, wout, *, name="merge_bwd"):
    s, d = dx1.shape
    tm = _pick(s, (MERGE_TILE, 128))

    def body(dx_ref, g0, g1, g2, ya_ref, yb_ref, yc_ref, o0, o1, o2, l0, l1, l2, wb_ref, wo_ref,
             dgl_ref, dz0, dz1, dz2, dya_ref, dyb_ref, do0, do1, do2, dl0, dl1, dl2):
        dm = _dot(dx_ref[...], wo_ref[...], 1, 1)
        dys = []
        for n, (y, gl, dz_ref) in enumerate(((ya_ref, g0, dz0), (yb_ref, g1, dz1), (yc_ref, g2, dz2))):
            z = _dot(y[...], wb_ref[n])
            sg = _sigmoid(gl[...])
            dz = (dm * sg).astype(BF16)
            dz_ref[...] = dz
            dgl_ref[:, n * d : (n + 1) * d] = (dm * z * sg * (1.0 - sg)).astype(BF16)
            dys.append(_dot(dz, wb_ref[n], 1, 1))
        dya_ref[...] = dys[0]
        dyb_ref[...] = dys[1]
        dyc = dys[2]
        w = _group_weights(l0[...], l1[...], l2[...])
        o = [o0[...], o1[...], o2[...]]
        for gi, r in enumerate((do0, do1, do2)):
            r[...] = w[gi] * dyc
        for hd in range(ATT_HG):
            sl = slice(hd * ATT_DH, (hd + 1) * ATT_DH)
            t = [jnp.sum(dyc[:, sl] * o[gi][:, sl], axis=-1, keepdims=True) for gi in range(3)]
            wh = [w[gi][:, sl] for gi in range(3)]
            tbar = wh[0] * t[0] + wh[1] * t[1] + wh[2] * t[2]
            for gi, r in enumerate((dl0, dl1, dl2)):
                r[:, sl] = wh[gi] * (t[gi] - tbar)

    row = pl.BlockSpec((tm, d), lambda i: (i, 0))
    half = pl.BlockSpec((tm, MIX_W), lambda i: (i, 0))
    gate = [pl.BlockSpec((tm, d), lambda i, n=n: (i, n)) for n in range(N_BRANCH)]
    rb = jax.ShapeDtypeStruct((s, d), BF16)
    hf = jax.ShapeDtypeStruct((s, MIX_W), F32)
    return pl.pallas_call(
        body,
        name=name,
        grid=(s // tm,),
        in_specs=[row] + gate + [half] * 9
        + [pl.BlockSpec((N_BRANCH, MIX_W, d), lambda i: (0, 0, 0)), pl.BlockSpec((d, d), lambda i: (0, 0))],
        out_specs=[pl.BlockSpec((tm, N_BRANCH * d), lambda i: (i, 0)), row, row, row] + [half] * 8,
        out_shape=[jax.ShapeDtypeStruct((s, N_BRANCH * d), BF16), rb, rb, rb] + [hf] * 8,
        compiler_params=_cparams(("arbitrary",)),
    )(dx1, proj, proj, proj, ya, yb, yc, *outs, *lses, wb, wout)


ATT_BLOCK = 128
ATT_GROUPS = ((128, 1), (512, 4), (2048, 16))
N_ATT_HEADS = ATT_HG * len(ATT_GROUPS)
ALIBI_MAX_EXP = 8.0
MASKED = -1e30


def _att_slopes(group):
    return [2.0 ** (-ALIBI_MAX_EXP * (group * ATT_HG + h + 1) / N_ATT_HEADS) for h in range(ATT_HG)]


def _att_scores(q, kp, kc, slope_dil, has_prev):
    scale = ATT_DH**-0.5
    qi = lax.broadcasted_iota(jnp.int32, (ATT_BLOCK, ATT_BLOCK), 0)
    kj = lax.broadcasted_iota(jnp.int32, (ATT_BLOCK, ATT_BLOCK), 1)
    dist = qi - kj
    sc = _dot(q, kc, 1, 1) * scale - slope_dil * dist.astype(F32)
    sc = jnp.where(dist >= 0, sc, MASKED)
    sp = _dot(q, kp, 1, 1) * scale - slope_dil * (dist + ATT_BLOCK).astype(F32)
    sp = jnp.where(jnp.logical_and(dist <= 0, has_prev), sp, MASKED)
    return sp, sc


def attn_fwd(qkv, col, group, *, name="attn_fwd"):
    s = qkv.shape[0]
    window, dil = ATT_GROUPS[group]
    assert window // dil == ATT_BLOCK and s % (dil * ATT_BLOCK) == 0 and col % MIX_W == 0
    nb = s // dil // ATT_BLOCK
    nblk = s // ATT_BLOCK
    slopes = _att_slopes(group)
    c0 = col // MIX_W

    def body(q_ref, kp_ref, kc_ref, vp_ref, vc_ref, o_ref, l_ref):
        n = pl.program_id(0)
        has_prev = (n % nb) != 0
        for hd in range(ATT_HG):
            sl = slice(hd * ATT_DH, (hd + 1) * ATT_DH)
            sp, sc = _att_scores(q_ref[:, sl], kp_ref[:, sl], kc_ref[:, sl], slopes[hd] * dil, has_prev)
            m = jnp.maximum(jnp.max(sp, axis=-1, keepdims=True), jnp.max(sc, axis=-1, keepdims=True))
            ep, ec = jnp.exp(sp - m), jnp.exp(sc - m)
            den = jnp.sum(ep, axis=-1, keepdims=True) + jnp.sum(ec, axis=-1, keepdims=True)
            inv = 1.0 / den
            o_ref[:, sl] = _dot(ep * inv, vp_ref[:, sl]) + _dot(ec * inv, vc_ref[:, sl])
            l_ref[:, sl] = jnp.broadcast_to(m + jnp.log(den), (ATT_BLOCK, ATT_DH))

    cur = lambda c: pl.BlockSpec((ATT_BLOCK, MIX_W), lambda n, c=c: (n, c))
    prev = lambda c: pl.BlockSpec((ATT_BLOCK, MIX_W), lambda n, c=c: (jnp.maximum(n - 1, 0), c))
    blk = pl.BlockSpec((ATT_BLOCK, MIX_W), lambda n: (n, 0))
    return pl.pallas_call(
        body,
        name=name,
        grid=(nblk,),
        in_specs=[cur(c0), prev(c0 + 1), cur(c0 + 1), prev(c0 + 2), cur(c0 + 2)],
        out_specs=[blk, blk],
        out_shape=[jax.ShapeDtypeStruct((s, MIX_W), F32)] * 2,
        compiler_params=_cparams(("arbitrary",)),
    )(qkv, qkv, qkv, qkv, qkv)


def attn_bwd(qkv, col, group, out, lse, dout, dlse, *, name="attn_bwd"):
    s = qkv.shape[0]
    window, dil = ATT_GROUPS[group]
    nb = s // dil // ATT_BLOCK
    nblk = s // ATT_BLOCK
    slopes = _att_slopes(group)
    c0 = col // MIX_W
    scale = ATT_DH**-0.5

    def body(q_ref, kp_ref, kc_ref, vp_ref, vc_ref, o_ref, l_ref, do_ref, dl_ref, dq_ref, dk_ref, dv_ref, acck, accv):
        n = pl.program_id(0)
        live = n < nblk
        has_prev = jnp.logical_and((n % nb) != 0, live)

        @pl.when(n == 0)
        def _():
            acck[...] = jnp.zeros_like(acck)
            accv[...] = jnp.zeros_like(accv)

        for hd in range(ATT_HG):
            sl = slice(hd * ATT_DH, (hd + 1) * ATT_DH)
            q, kp, kc, vp, vc = q_ref[:, sl], kp_ref[:, sl], kc_ref[:, sl], vp_ref[:, sl], vc_ref[:, sl]
            do = do_ref[:, sl]
            sp, sc = _att_scores(q, kp, kc, slopes[hd] * dil, has_prev)
            l = l_ref[:, sl][:, :1]
            pp, pc = jnp.exp(sp - l), jnp.exp(sc - l)
            corr = dl_ref[:, sl][:, :1] - jnp.sum(do * o_ref[:, sl], axis=-1, keepdims=True)
            dsp = pp * (_dot(do, vp, 1, 1) + corr) * scale
            dsc = pc * (_dot(do, vc, 1, 1) + corr) * scale
            dqh = (_dot(dsp, kp) + _dot(dsc, kc)).astype(BF16)

            @pl.when(live)
            def _(dqh=dqh, sl=sl):
                dq_ref[:, sl] = dqh

            dk_ref[:, sl] = (acck[:, sl] + _dot(dsp, q, 0, 0)).astype(BF16)
            dv_ref[:, sl] = (accv[:, sl] + _dot(pp, do, 0, 0)).astype(BF16)
            acck[:, sl] = jnp.where(live, _dot(dsc, q, 0, 0), 0.0)
            accv[:, sl] = jnp.where(live, _dot(pc, do, 0, 0), 0.0)

    last = nblk - 1
    cur = lambda c: pl.BlockSpec((ATT_BLOCK, MIX_W), lambda n, c=c: (jnp.minimum(n, last), c))
    prev = lambda c: pl.BlockSpec((ATT_BLOCK, MIX_W), lambda n, c=c: (jnp.clip(n - 1, 0, last), c))
    sds = jax.ShapeDtypeStruct((s, MIX_W), BF16)
    return pl.pallas_call(
        body,
        name=name,
        grid=(nblk + 1,),
        in_specs=[cur(c0), prev(c0 + 1), cur(c0 + 1), prev(c0 + 2), cur(c0 + 2), cur(0), cur(0), cur(0), cur(0)],
        out_specs=[cur(0), prev(0), prev(0)],
        out_shape=[sds, sds, sds],
        scratch_shapes=[pltpu.VMEM((ATT_BLOCK, MIX_W), F32), pltpu.VMEM((ATT_BLOCK, MIX_W), F32)],
        compiler_params=_cparams(("arbitrary",)),
    )(qkv, qkv, qkv, qkv, qkv, out, lse, dout, dlse)


GLA_HEADS = 4
GLA_DK = 64
GLA_DV = 128
GLA_CHUNK = 64
GLA_TAU = 16.0
GLA_QK = GLA_HEADS * GLA_DK
LR_PAD = 128


def _dot_exact(a, b, ca=1, cb=0):
    return lax.dot_general(a, b, (((ca,), (cb,)), ((), ())), precision=lax.Precision.HIGHEST, preferred_element_type=F32)


def _gla_chunk(q, k, v, lr, r, state, wg, bg, go):
    c = GLA_CHUNK
    z = _dot(lr, wg) + bg
    la = (jnp.minimum(z, 0.0) - jnp.log(1.0 + jnp.exp(-jnp.abs(z)))) * (1.0 / GLA_TAU)
    ri = lax.broadcasted_iota(jnp.int32, (c, c), 0)
    ci = lax.broadcasted_iota(jnp.int32, (c, c), 1)
    causal = ri >= ci
    b = _dot_exact(causal.astype(F32), la)
    bmid = b[c // 2 : c // 2 + 1, :]
    bend = b[c - 1 : c, :]
    qs = q * GLA_DK**-0.5
    q_in = qs * jnp.exp(b)
    q_mid = qs * jnp.exp(b - bmid)
    k_mid = k * jnp.exp(bmid - b)
    k_end = k * jnp.exp(bend - b)
    ys, upd = [], []
    for h in range(GLA_HEADS):
        sk = slice(h * GLA_DK, (h + 1) * GLA_DK)
        sv = slice(h * GLA_DV, (h + 1) * GLA_DV)
        vh = v[:, sv]
        inter = _dot(q_in[:, sk], state[:, sk], 1, 1)
        sc = jnp.where(causal, _dot_exact(q_mid[:, sk], k_mid[:, sk], 1, 1), 0.0)
        o = inter + _dot(sc, vh)
        o = o * lax.rsqrt(jnp.mean(o * o, axis=-1, keepdims=True) + RMS_EPS) * go
        rh = r[:, sv]
        ys.append(o * rh * _sigmoid(rh))
        upd.append(_dot(vh, k_end[:, sk], 0, 0))
    new_state = jnp.exp(bend) * state + jnp.concatenate(upd, axis=-1)
    return jnp.concatenate(ys, axis=-1), new_state


def _gla_in_specs(cols, rev, nc):
    c = GLA_CHUNK
    row = (lambda i: nc - 1 - i) if rev else (lambda i: i)
    qc, kc, vc, lc, rc = cols
    assert qc % GLA_QK == 0 and kc % GLA_QK == 0 and vc % MIX_W == 0 and rc % MIX_W == 0 and lc % LR_PAD == 0
    return [
        pl.BlockSpec((c, GLA_QK), lambda i: (row(i), qc // GLA_QK)),
        pl.BlockSpec((c, GLA_QK), lambda i: (row(i), kc // GLA_QK)),
        pl.BlockSpec((c, MIX_W), lambda i: (row(i), vc // MIX_W)),
        pl.BlockSpec((c, LR_PAD), lambda i: (row(i), lc // LR_PAD)),
        pl.BlockSpec((c, MIX_W), lambda i: (row(i), rc // MIX_W)),
    ], row


def gla_fwd(proj, cols, wg, bg, go, *, name="gla_fwd"):
    s = proj.shape[0]
    nc = s // GLA_CHUNK
    specs, row = _gla_in_specs(cols, False, nc)

    def body(q_ref, k_ref, v_ref, lr_ref, r_ref, wg_ref, bg_ref, go_ref, y_ref, st_ref, state):
        @pl.when(pl.program_id(0) == 0)
        def _():
            state[...] = jnp.zeros_like(state)

        st = state[...]
        st_ref[...] = st
        y, new = _gla_chunk(q_ref[...], k_ref[...], v_ref[...], lr_ref[...], r_ref[...], st, wg_ref[...], bg_ref[...], go_ref[...])
        y_ref[...] = y
        state[...] = new

    full = lambda shp: pl.BlockSpec(shp, lambda i: (0, 0))
    return pl.pallas_call(
        body,
        name=name,
        grid=(nc,),
        in_specs=specs + [full((LR_PAD, GLA_QK)), full((1, GLA_QK)), full((1, GLA_DV))],
        out_specs=[pl.BlockSpec((GLA_CHUNK, MIX_W), lambda i: (i, 0)), pl.BlockSpec((GLA_DV, GLA_QK), lambda i: (i, 0))],
        out_shape=[jax.ShapeDtypeStruct((s, MIX_W), F32), jax.ShapeDtypeStruct((nc * GLA_DV, GLA_QK), F32)],
        scratch_shapes=[pltpu.VMEM((GLA_DV, GLA_QK), F32)],
        compiler_params=_cparams(("arbitrary",)),
    )(proj, proj, proj, proj, proj, wg, bg, go)


def gla_bwd(proj, cols, wg, bg, go, states, dy, *, name="gla_bwd"):
    s = proj.shape[0]
    nc = s // GLA_CHUNK
    specs, row = _gla_in_specs(cols, True, nc)

    def body(q_ref, k_ref, v_ref, lr_ref, r_ref, wg_ref, bg_ref, go_ref, st_ref, dy_ref,
             dq_ref, dk_ref, dv_ref, dlr_ref, dr_ref, dwg_ref, dbg_ref, dgo_ref, dstate):
        first = pl.program_id(0) == 0

        @pl.when(first)
        def _():
            dstate[...] = jnp.zeros_like(dstate)

        _, vjp = jax.vjp(_gla_chunk, q_ref[...], k_ref[...], v_ref[...], lr_ref[...], r_ref[...], st_ref[...],
                         wg_ref[...].astype(F32), bg_ref[...], go_ref[...])
        dq, dk, dv, dlr, dr, dst, dwg, dbg, dgo = vjp((dy_ref[...], dstate[...]))
        dq_ref[...] = dq.astype(BF16)
        dk_ref[...] = dk.astype(BF16)
        dv_ref[...] = dv.astype(BF16)
        dlr_ref[...] = dlr.astype(BF16)
        dr_ref[...] = dr.astype(BF16)
        dstate[...] = dst
        _accum(dwg_ref, dwg, first)
        _accum(dbg_ref, dbg, first)
        _accum(dgo_ref, dgo, first)

    c = GLA_CHUNK
    full = lambda shp: pl.BlockSpec(shp, lambda i: (0, 0))
    rows = lambda w: pl.BlockSpec((c, w), lambda i: (row(i), 0))
    return pl.pallas_call(
        body,
        name=name,
        grid=(nc,),
        in_specs=specs + [full((LR_PAD, GLA_QK)), full((1, GLA_QK)), full((1, GLA_DV)),
                          pl.BlockSpec((GLA_DV, GLA_QK), lambda i: (row(i), 0)), rows(MIX_W)],
        out_specs=[rows(GLA_QK), rows(GLA_QK), rows(MIX_W), rows(LR_PAD), rows(MIX_W),
                   full((LR_PAD, GLA_QK)), full((1, GLA_QK)), full((1, GLA_DV))],
        out_shape=[jax.ShapeDtypeStruct((s, GLA_QK), BF16), jax.ShapeDtypeStruct((s, GLA_QK), BF16),
                   jax.ShapeDtypeStruct((s, MIX_W), BF16), jax.ShapeDtypeStruct((s, LR_PAD), BF16),
                   jax.ShapeDtypeStruct((s, MIX_W), BF16), jax.ShapeDtypeStruct((LR_PAD, GLA_QK), F32),
                   jax.ShapeDtypeStruct((1, GLA_QK), F32), jax.ShapeDtypeStruct((1, GLA_DV), F32)],
        scratch_shapes=[pltpu.VMEM((GLA_DV, GLA_QK), F32)],
        compiler_params=_cparams(("arbitrary",)),
    )(proj, proj, proj, proj, proj, wg, bg, go, states, dy)


S5_G = 32
S5_P = 64
S5_C = 16
S5_N = S5_G * S5_P
S5_TC = 256
SUB = 8


def _s5_disc(a_re, a_im, ls, b_re, b_im):
    step = jnp.exp(ls)
    mag = jnp.exp(a_re * step)
    lr, li = mag * jnp.cos(a_im * step), mag * jnp.sin(a_im * step)
    inv = 1.0 / (a_re * a_re + a_im * a_im)
    nr, ni = lr - 1.0, li
    cr, ci = (nr * a_re + ni * a_im) * inv, (ni * a_re - nr * a_im) * inv
    return lr, li, cr * b_re - ci * b_im, cr * b_im + ci * b_re


def s5_disc_fwd(a_re, a_im, ls, b_re, b_im, *, name="s5_disc"):
    def body(ar, ai, l, br, bi, o0, o1, o2, o3):
        for o, val in zip((o0, o1, o2, o3), _s5_disc(ar[...], ai[...], l[...], br[...], bi[...])):
            o[...] = val

    sds = jax.ShapeDtypeStruct(a_re.shape, F32)
    return pl.pallas_call(body, name=name, out_shape=[sds] * 4)(a_re, a_im, ls, b_re, b_im)


def s5_disc_bwd(a_re, a_im, ls, b_re, b_im, cts, *, name="s5_disc_bwd"):
    def body(ar, ai, l, br, bi, c0, c1, c2, c3, dar, dai, dl, dbr, dbi):
        _, vjp = jax.vjp(_s5_disc, ar[...], ai[...], l[...], br[...], bi[...])
        g = vjp((c0[...], c1[...], c2[...], c3[...]))
        for o, val in zip((dar, dai, dl), g[:3]):
            o[...] = jnp.sum(val, axis=-1, keepdims=True)
        dbr[...] = g[3]
        dbi[...] = g[4]

    col = jax.ShapeDtypeStruct((a_re.shape[0], 1), F32)
    sds = jax.ShapeDtypeStruct(a_re.shape, F32)
    return pl.pallas_call(body, name=name, out_shape=[col, col, col, sds, sds])(a_re, a_im, ls, b_re, b_im, *cts)


def _gelu(y):
    return 0.5 * y * (1.0 + jnp.tanh(0.7978845608028654 * (y + 0.044715 * (y * y * y))))


def _s5_powers(lam, conj):
    lr, li = lam[:, :S5_N], lam[:, S5_N:]
    if conj:
        li = -li
    rows, pr, pi = [], lr, li
    for _ in range(SUB):
        rows.append((pr, pi))
        pr, pi = pr * lr - pi * li, pr * li + pi * lr
    return rows


def _s5_table(rows, reverse):
    ridx = lax.broadcasted_iota(jnp.int32, (SUB, S5_N), 0)
    tr = jnp.zeros((SUB, S5_N), F32)
    ti = jnp.zeros((SUB, S5_N), F32)
    for i in range(SUB):
        pr, pi = rows[SUB - 1 - i] if reverse else rows[i]
        tr = jnp.where(ridx == i, pr, tr)
        ti = jnp.where(ridx == i, pi, ti)
    return tr, ti


def _s5_scan(buf, lam, carry_ref, reverse):
    tc = buf.shape[0]
    nblk = tc // SUB
    rows = _s5_powers(lam, reverse)
    tr, ti = _s5_table(rows, reverse)
    ridx = lax.broadcasted_iota(jnp.int32, (SUB, S5_N), 0)

    def block(j, carry):
        jj = (nblk - 1 - j) if reverse else j
        at = pl.ds(pl.multiple_of(jj * SUB, SUB), SUB)
        re, im = buf[at, :S5_N], buf[at, S5_N:]
        for sft, (pr, pi) in ((1, rows[0]), (2, rows[1]), (4, rows[3])):
            if reverse:
                keep = ridx < SUB - sft
                sre, sim = pltpu.roll(re, SUB - sft, 0), pltpu.roll(im, SUB - sft, 0)
            else:
                keep = ridx >= sft
                sre, sim = pltpu.roll(re, sft, 0), pltpu.roll(im, sft, 0)
            sre, sim = jnp.where(keep, sre, 0.0), jnp.where(keep, sim, 0.0)
            re, im = re + pr * sre - pi * sim, im + pr * sim + pi * sre
        cr, ci = carry
        re, im = re + tr * cr - ti * ci, im + tr * ci + ti * cr
        buf[at, :S5_N] = re
        buf[at, S5_N:] = im
        edge = 0 if reverse else SUB - 1
        return re[edge : edge + 1, :], im[edge : edge + 1, :]

    c0 = (carry_ref[0:1, :S5_N], carry_ref[0:1, S5_N:])
    cr, ci = lax.fori_loop(0, nblk, block, c0)
    carry_ref[:, :S5_N] = jnp.broadcast_to(cr, (SUB, S5_N))
    carry_ref[:, S5_N:] = jnp.broadcast_to(ci, (SUB, S5_N))


def s5_fwd(proj, ucol, bd, cd, lam, dskip, wglu, bglu, *, name="s5_fwd"):
    s = proj.shape[0]
    tc = _pick(s, (S5_TC, 128, 64, 8))
    assert ucol % MIX_W == 0

    def body(u_ref, bd_ref, cd_ref, lam_ref, d_ref, w_ref, b_ref, y_ref, xs_ref, carry):
        @pl.when(pl.program_id(0) == 0)
        def _():
            carry[...] = jnp.zeros_like(carry)

        u = u_ref[...]
        xs_ref[...] = _dot(u, bd_ref[...])
        _s5_scan(xs_ref, lam_ref[...], carry, False)
        g = _gelu(_dot(xs_ref[...], cd_ref[...]) + d_ref[...] * u)
        y_ref[...] = g * _sigmoid(_dot(g, w_ref[...]) + b_ref[...])

    full = lambda shp: pl.BlockSpec(shp, lambda i: (0, 0))
    return pl.pallas_call(
        body,
        name=name,
        grid=(s // tc,),
        in_specs=[pl.BlockSpec((tc, MIX_W), lambda i: (i, ucol // MIX_W)), full((MIX_W, 2 * S5_N)), full((2 * S5_N, MIX_W)),
                  full((1, 2 * S5_N)), full((1, MIX_W)), full((MIX_W, MIX_W)), full((1, MIX_W))],
        out_specs=[pl.BlockSpec((tc, MIX_W), lambda i: (i, 0)), pl.BlockSpec((tc, 2 * S5_N), lambda i: (i, 0))],
        out_shape=[jax.ShapeDtypeStruct((s, MIX_W), F32), jax.ShapeDtypeStruct((s, 2 * S5_N), F32)],
        scratch_shapes=[pltpu.VMEM((SUB, 2 * S5_N), F32)],
        compiler_params=_cparams(("arbitrary",)),
    )(proj, bd, cd, lam, dskip, wglu, bglu)


def s5_bwd(proj, ucol, xs, bd, cd, lam, dskip, wglu, bglu, dya, *, name="s5_bwd"):
    s = proj.shape[0]
    tc = _pick(s, (S5_TC, 128, 64, 8))
    nch = s // tc
    per = tc // SUB

    def body(u_ref, xs_ref, xp_ref, bd_ref, cd_ref, lam_ref, d_ref, w_ref, b_ref, dya_ref,
             du_ref, adj_ref, g_ref, dpre_ref, dy_ref, dlam_ref, dd_ref, db_ref, buf, carry, lacc):
        i = pl.program_id(0)
        first = i == 0

        @pl.when(first)
        def _():
            carry[...] = jnp.zeros_like(carry)
            lacc[...] = jnp.zeros_like(lacc)

        u, x, dya_ = u_ref[...], xs_ref[...], dya_ref[...]
        y = _dot(x, cd_ref[...]) + d_ref[...] * u
        g, gelu_vjp = jax.vjp(_gelu, y)
        sg = _sigmoid(_dot(g, w_ref[...]) + b_ref[...])
        dpre = dya_ * g * sg * (1.0 - sg)
        (dy,) = gelu_vjp(dya_ * sg + _dot(dpre, w_ref[...], 1, 1))
        g_ref[...] = g.astype(BF16)
        dpre_ref[...] = dpre.astype(BF16)
        dy_ref[...] = dy.astype(BF16)
        _accum(db_ref, jnp.sum(dpre, axis=0, keepdims=True), first)
        _accum(dd_ref, jnp.sum(dy * u, axis=0, keepdims=True), first)
        buf[...] = _dot(dy, cd_ref[...], 1, 1)
        _s5_scan(buf, lam_ref[...], carry, True)
        a = buf[...]
        adj_ref[...] = a.astype(BF16)
        du_ref[...] = (dy * d_ref[...] + _dot(a, bd_ref[...], 1, 1)).astype(BF16)
        before = jnp.where(i == nch - 1, 0.0, xp_ref[SUB - 1 : SUB, :])
        ridx = lax.broadcasted_iota(jnp.int32, (tc, 2 * S5_N), 0)
        xprev = jnp.where(ridx == 0, before, pltpu.roll(x, 1, 0))
        ar, ai, xr, xi = a[:, :S5_N], a[:, S5_N:], xprev[:, :S5_N], xprev[:, S5_N:]
        lacc[:, :S5_N] += jnp.sum((ar * xr + ai * xi).reshape(per, SUB, S5_N), axis=0)
        lacc[:, S5_N:] += jnp.sum((ai * xr - ar * xi).reshape(per, SUB, S5_N), axis=0)

        @pl.when(i == nch - 1)
        def _():
            dlam_ref[...] = jnp.sum(lacc[...], axis=0, keepdims=True)

    rev = lambda i: nch - 1 - i
    full = lambda shp: pl.BlockSpec(shp, lambda i: (0, 0))
    rows = lambda w: pl.BlockSpec((tc, w), lambda i: (rev(i), 0))
    hb = jax.ShapeDtypeStruct((s, MIX_W), BF16)
    return pl.pallas_call(
        body,
        name=name,
        grid=(nch,),
        in_specs=[pl.BlockSpec((tc, MIX_W), lambda i: (rev(i), ucol // MIX_W)), rows(2 * S5_N),
                  pl.BlockSpec((SUB, 2 * S5_N), lambda i: (jnp.maximum(rev(i) * per - 1, 0), 0)),
                  full((MIX_W, 2 * S5_N)), full((2 * S5_N, MIX_W)), full((1, 2 * S5_N)), full((1, MIX_W)),
                  full((MIX_W, MIX_W)), full((1, MIX_W)), rows(MIX_W)],
        out_specs=[rows(MIX_W), rows(2 * S5_N), rows(MIX_W), rows(MIX_W), rows(MIX_W),
                   full((1, 2 * S5_N)), full((1, MIX_W)), full((1, MIX_W))],
        out_shape=[hb, jax.ShapeDtypeStruct((s, 2 * S5_N), BF16), hb, hb, hb,
                   jax.ShapeDtypeStruct((1, 2 * S5_N), F32), jax.ShapeDtypeStruct((1, MIX_W), F32),
                   jax.ShapeDtypeStruct((1, MIX_W), F32)],
        scratch_shapes=[pltpu.VMEM((tc, 2 * S5_N), F32), pltpu.VMEM((SUB, 2 * S5_N), F32), pltpu.VMEM((SUB, 2 * S5_N), F32)],
        compiler_params=_cparams(("arbitrary",)),
    )(proj, xs, xs, bd, cd, lam, dskip, wglu, bglu, dya)


def _bcast16(a):
    return jnp.broadcast_to(a.reshape(S5_N, 1), (S5_N, S5_C))


def _blockdiag(blocks):
    g, r, c = blocks.shape
    eye = jnp.eye(g, dtype=blocks.dtype)
    return (eye[:, None, :, None] * blocks[:, :, None, :]).reshape(g * r, g * c)


def _blockdiag_extract(dense, r, c):
    g = dense.shape[0] // r
    return jnp.einsum("grgc->grc", dense.reshape(g, r, g, c))


def s5_prepare(a_re, a_im, log_step, b_re, b_im, c_re, c_im):
    disc_in = (_bcast16(a_re), _bcast16(a_im), _bcast16(jnp.broadcast_to(log_step[:, None], (S5_G, S5_P))),
               b_re.reshape(S5_N, S5_C), b_im.reshape(S5_N, S5_C))
    lr, li, bbr, bbi = s5_disc_fwd(*disc_in)
    lam = jnp.concatenate([lr[:, 0], li[:, 0]]).reshape(1, 2 * S5_N)
    to_blocks = lambda t: _blockdiag(t.reshape(S5_G, S5_P, S5_C).transpose(0, 2, 1))
    bd = jnp.concatenate([to_blocks(bbr), to_blocks(bbi)], axis=1).astype(BF16)
    cd = jnp.concatenate([_blockdiag(c_re.transpose(0, 2, 1)), -_blockdiag(c_im.transpose(0, 2, 1))], axis=0).astype(BF16)
    return disc_in, lam, bd, cd


def s5_param_grads(disc_in, dlam, dbd, dcd):
    first_col = lambda v: jnp.pad(v.reshape(S5_N, 1), ((0, 0), (0, S5_C - 1)))
    from_blocks = lambda t: _blockdiag_extract(t, S5_C, S5_P).transpose(0, 2, 1).reshape(S5_N, S5_C)
    cts = (first_col(dlam[0, :S5_N]), first_col(dlam[0, S5_N:]), from_blocks(dbd[:, :S5_N]), from_blocks(dbd[:, S5_N:]))
    dar, dai, dls, dbr, dbi = s5_disc_bwd(*disc_in, cts)
    dcr = _blockdiag_extract(dcd[:S5_N], S5_P, S5_C).transpose(0, 2, 1)
    dci = -_blockdiag_extract(dcd[S5_N:], S5_P, S5_C).transpose(0, 2, 1)
    return (dar.reshape(S5_G, S5_P), dai.reshape(S5_G, S5_P), dls.reshape(S5_G, S5_P).sum(axis=1),
            dbr.reshape(S5_G, S5_P, S5_C), dbi.reshape(S5_G, S5_P, S5_C), dcr, dci)


N_DEV = 8
MESH_ID = pl.DeviceIdType.MESH
ANY = pl.BlockSpec(memory_space=pl.ANY)


def _me():
    return lax.axis_index("x"), lax.axis_index("y"), lax.axis_index("c")


def all_gather(blocks, *, by_core=False, name="all_gather"):
    na = len(blocks)
    shapes = [b.shape[1:] if by_core else b.shape for b in blocks]

    def body(*refs):
        ins, outs = refs[:na], refs[na : 2 * na]
        send_sems, recv_sems = refs[2 * na :]
        x, y, c = _me()
        me, sibling = (x, y, c), (x, y, 1 - c)
        chips = [(1 - x, y), (x, 1 - y), (1 - x, 1 - y)]
        slot = lambda p: 4 * p[0] + 2 * p[1] + p[2]

        def copy(a, k, block, to, src=None):
            dst = outs[a].at[slot(block)]
            return pltpu.make_async_remote_copy(src_ref=dst if src is None else src, dst_ref=dst, send_sem=send_sems.at[a * 7 + k],
                                                recv_sem=recv_sems.at[a * 7 + k], device_id=to, device_id_type=MESH_ID)

        mine = [ins[a].at[c] if by_core else ins[a] for a in range(na)]
        sends = []
        for a in range(na):
            first = [copy(a, 0, me, sibling, src=mine[a])]
            first += [copy(a, 1 + j, me, (*chip, c), src=mine[a]) for j, chip in enumerate(chips)]
            for cp in first:
                cp.start()
            sends += first
        for a in range(na):
            for j, chip in enumerate(chips):
                copy(a, 1 + j, (*chip, c), me).wait_recv()
                fwd = copy(a, 4 + j, (*chip, c), sibling)
                fwd.start()
                sends.append(fwd)
        for a in range(na):
            copy(a, 0, sibling, me).wait_recv()
            for j, chip in enumerate(chips):
                copy(a, 4 + j, (*chip, 1 - c), me).wait_recv()
        for cp in sends:
            cp.wait_send()

    outs = pl.pallas_call(
        body,
        name=name,
        in_specs=[ANY] * na,
        out_specs=[ANY] * na,
        out_shape=[jax.ShapeDtypeStruct((N_DEV, *shp), b.dtype) for shp, b in zip(shapes, blocks)],
        scratch_shapes=[pltpu.SemaphoreType.DMA((na * 7,)), pltpu.SemaphoreType.DMA((na * 7,))],
    )(*blocks)
    x, y, c = _me()
    own = [lax.dynamic_index_in_dim(b, c, 0, keepdims=False) if by_core else b for b in blocks]
    return [lax.dynamic_update_index_in_dim(o, b, 4 * x + 2 * y + c, 0) for o, b in zip(outs, own)]


D2D_PIECES = 8


def _slabs(rows):
    n = D2D_PIECES if rows % (16 * D2D_PIECES) == 0 else 1
    return [pl.ds(i * (rows // n), rows // n) for i in range(n)]


def sibling_swap(arrs, *, name="sibling_swap"):
    na = len(arrs)
    pieces = [[(j, sl) for j in range(a.shape[0]) for sl in (_slabs(a.shape[1]) if a.shape[0] == 1 else _halves(a.shape[1]))]
              for a in arrs]
    base = np.cumsum([0] + [len(p) for p in pieces])

    def body(*refs):
        ins, outs = refs[:na], refs[na : 2 * na]
        send_sems, recv_sems = refs[2 * na :]
        x, y, c = _me()
        sends, recvs = [], []
        for a in range(na):
            for i, (j, sl) in enumerate(pieces[a]):
                k = int(base[a]) + i
                cp = pltpu.make_async_remote_copy(src_ref=ins[a].at[j, sl], dst_ref=outs[a].at[j, sl], send_sem=send_sems.at[k],
                                                  recv_sem=recv_sems.at[k], device_id=(x, y, 1 - c), device_id_type=MESH_ID)
                cp.start()
                sends.append(cp)
        for cp in sends:
            cp.wait_recv()
        for cp in sends:
            cp.wait_send()

    nsem = int(base[-1])
    return pl.pallas_call(
        body,
        name=name,
        in_specs=[ANY] * na,
        out_specs=[ANY] * na,
        out_shape=[jax.ShapeDtypeStruct(a.shape, a.dtype) for a in arrs],
        scratch_shapes=[pltpu.SemaphoreType.DMA((nsem,)), pltpu.SemaphoreType.DMA((nsem,))],
    )(*arrs)


def _halves(rows):
    return [pl.ds(0, rows // 2), pl.ds(rows // 2, rows // 2)] if rows % 32 == 0 else [pl.ds(0, rows)]


def chip_exchange(arrs, *, name="chip_exchange"):
    na = len(arrs)

    def body(*refs):
        ins, outs = refs[:na], refs[na : 2 * na]
        send_sems, recv_sems = refs[2 * na :]
        x, y, c = _me()
        my = 2 * x + y
        peers = []
        for k in range(1, N_CHIPS):
            px, py = (x + ((k >> 1) & 1)) % 2, (y + (k & 1)) % 2
            peers.append(((px, py, c), 2 * px + py))
        sends = []
        for a in range(na):
            for k, (peer, pidx) in enumerate(peers):
                cp = pltpu.make_async_remote_copy(src_ref=ins[a].at[pidx], dst_ref=outs[a].at[my], send_sem=send_sems.at[a * 3 + k],
                                                  recv_sem=recv_sems.at[a * 3 + k], device_id=peer, device_id_type=MESH_ID)
                cp.start()
                sends.append(cp)
        for a in range(na):
            for k, (peer, pidx) in enumerate(peers):
                pltpu.make_async_remote_copy(src_ref=ins[a].at[my], dst_ref=outs[a].at[pidx], send_sem=send_sems.at[a * 3 + k],
                                             recv_sem=recv_sems.at[a * 3 + k], device_id=peer, device_id_type=MESH_ID).wait_recv()
        for cp in sends:
            cp.wait_send()

    outs = pl.pallas_call(
        body,
        name=name,
        in_specs=[ANY] * na,
        out_specs=[ANY] * na,
        out_shape=[jax.ShapeDtypeStruct(a.shape, a.dtype) for a in arrs],
        scratch_shapes=[pltpu.SemaphoreType.DMA((na * 3,)), pltpu.SemaphoreType.DMA((na * 3,))],
    )(*arrs)
    chip = 2 * lax.axis_index("x") + lax.axis_index("y")
    return [lax.dynamic_update_index_in_dim(o, lax.dynamic_index_in_dim(a, chip, 0, keepdims=False), chip, 0)
            for o, a in zip(outs, arrs)]


def add_pair(a, b, *, name="add_pair"):
    n, r, c = a.shape
    tm = _row_tile(r, n * c * 2, 2 << 20)

    def body(a_ref, b_ref, o_ref):
        o_ref[...] = (a_ref[...].astype(F32) + b_ref[...].astype(F32)).astype(o_ref.dtype)

    blk = pl.BlockSpec((n, tm, c), lambda i: (0, i, 0))
    return pl.pallas_call(
        body, name=name, grid=(r // tm,), in_specs=[blk, blk], out_specs=blk, out_shape=jax.ShapeDtypeStruct(a.shape, a.dtype),
        compiler_params=_cparams(("arbitrary",)),
    )(a, b)


def _row_tile(rows, row_bytes, budget):
    best = None
    for t in range(8, rows + 1, 8):
        if rows % t == 0 and t * row_bytes <= budget:
            best = t
    return best or rows


def sum_blocks(a, *, name="sum_blocks"):
    n, r, c = a.shape
    tm = _row_tile(r, n * c * a.dtype.itemsize, 4 << 20)

    def body(a_ref, o_ref):
        acc = a_ref[0].astype(F32)
        for k in range(1, n):
            acc = acc + a_ref[k].astype(F32)
        o_ref[...] = acc

    return pl.pallas_call(
        body,
        name=name,
        grid=(r // tm,),
        in_specs=[pl.BlockSpec((n, tm, c), lambda i: (0, i, 0))],
        out_specs=pl.BlockSpec((tm, c), lambda i: (i, 0)),
        out_shape=jax.ShapeDtypeStruct((r, c), F32),
        compiler_params=_cparams(("arbitrary",)),
    )(a)


ADAM_LR = 0.001
ADAM_B1 = 0.9
ADAM_B2 = 0.999
ADAM_EPS = 1e-08
ADAM_WD = 0.01
ADAM_STEP = 10


def adamw(w, g, m, v, *, name="adamw"):
    r, c = w.shape
    tm = _row_tile(r, c * 4, 1 << 20)

    def body(w_ref, g_ref, m_ref, v_ref, d_ref, nm_ref, nv_ref):
        g_ = g_ref[...]
        m_ = ADAM_B1 * m_ref[...] + (1.0 - ADAM_B1) * g_
        v_ = ADAM_B2 * v_ref[...] + (1.0 - ADAM_B2) * (g_ * g_)
        m_hat = m_ / (1.0 - ADAM_B1**ADAM_STEP)
        v_hat = v_ / (1.0 - ADAM_B2**ADAM_STEP)
        d_ref[...] = -ADAM_LR * (m_hat / (jnp.sqrt(v_hat) + ADAM_EPS) + ADAM_WD * w_ref[...])
        nm_ref[...] = m_
        nv_ref[...] = v_

    blk = pl.BlockSpec((tm, c), lambda i: (i, 0))
    sds = jax.ShapeDtypeStruct((r, c), F32)
    return pl.pallas_call(
        body, name=name, grid=(r // tm,), in_specs=[blk] * 4, out_specs=[blk] * 3, out_shape=[sds] * 3,
        compiler_params=_cparams(("arbitrary",)),
    )(w, g, m, v)


WEIGHTS = ["g_mix", "w_in", "s5_a_re", "s5_a_im", "s5_log_step", "s5_b_re", "s5_b_im", "s5_c_re", "s5_c_im", "s5_d", "w_glu",
           "b_glu", "w_gla_gate", "b_gla_gate", "g_gla_out", "w_branch", "w_out", "g_mem", "g_cross", "w_xq", "w_xkv", "w_xo",
           "g_mlp", "w_up", "w_down", "g_final"]
SHARDED = {"w_in": 1, "w_glu": 0, "w_branch": 2, "w_out": 0, "w_xq": 0, "w_xkv": 1, "w_xo": 0, "w_up": 1, "w_down": 0}
DEPTH = 2
N_CHIPS = 4
D_IN = 9744
C_U, C_QG, C_KG, C_VG, C_LR, C_RG, C_QA, C_KA, C_VA, C_GATE = 0, 512, 768, 1024, 1536, 1552, 2064, 3600, 5136, 6672
A_GATE, A_U, A_QG, A_KG, A_VG, A_RG, A_Q0, A_LR, A_W = 0, 3072, 3584, 3840, 4096, 4608, 5120, 6656, 7168
GLA_COLS = (A_QG, A_KG, A_VG, A_LR, A_RG)


def _split_w_in(w):
    att = lambda g: [w[:, c + MIX_W * g : c + MIX_W * (g + 1)] for c in (C_QA, C_KA, C_VA)]
    a = jnp.concatenate([w[:, C_GATE:D_IN], w[:, C_U:C_QG], w[:, C_QG:C_KG], w[:, C_KG:C_VG], w[:, C_VG:C_LR], w[:, C_RG:C_QA],
                         *att(0), w[:, C_LR:C_RG], jnp.zeros((w.shape[0], A_W - A_LR - 16), w.dtype)], axis=1)
    return a, jnp.concatenate(att(1), axis=1), jnp.concatenate(att(2), axis=1)


def _join_w_in(a, b, c):
    att = lambda k: [a[:, A_Q0 + MIX_W * k : A_Q0 + MIX_W * (k + 1)], b[:, MIX_W * k : MIX_W * (k + 1)], c[:, MIX_W * k : MIX_W * (k + 1)]]
    return jnp.concatenate([a[:, A_U:A_QG], a[:, A_QG:A_KG], a[:, A_KG:A_VG], a[:, A_VG:A_RG], a[:, A_LR : A_LR + 16], a[:, A_RG:A_Q0],
                            *att(0), *att(1), *att(2), a[:, A_GATE:A_U]], axis=1)


def _by_residue(a, d):
    s = a.shape[0]
    return a if d == 1 else a.reshape(s // d, d, -1).transpose(1, 0, 2).reshape(s, -1)


def _in_order(a, d):
    s = a.shape[0]
    return a if d == 1 else a.reshape(d, s // d, -1).transpose(1, 0, 2).reshape(s, -1)


def _pack(arrs):
    flat = jnp.concatenate([a.reshape(-1) for a in arrs])
    n = flat.shape[0]
    rows = -(-n // (256 * 128)) * 256
    return jnp.pad(flat, (0, rows * 128 - n)).reshape(rows, 128)


def _unpack(packed, like):
    flat, out, o = packed.reshape(-1), [], 0
    for a in like:
        n = math.prod(a.shape)
        out.append(flat[o : o + n].reshape(a.shape))
        o += n
    return out


def _layer_fwd(x0, p, kv):
    h = rmsnorm(x0, p["g_mix"], name="mix_norm")
    hs = [h, _by_residue(h, 4), _by_residue(h, 16)]
    proj = [matmul(hs[g], p["w_in_seg"][g], name="in_proj") for g in range(3)]
    ya, xs = s5_fwd(proj[0], A_U, p["bd"], p["cd"], p["lam"], p["s5_d"], p["w_glu"], p["b_glu"])
    yb, gst = gla_fwd(proj[0], GLA_COLS, p["w_gate"], p["b_gla_gate"], p["g_gla_out"])
    att = [attn_fwd(proj[g], A_Q0 if g == 0 else 0, g) for g in range(3)]
    outs = [_in_order(att[g][0], ATT_GROUPS[g][1]) for g in range(3)]
    lses = [_in_order(att[g][1], ATT_GROUPS[g][1]) for g in range(3)]
    x1, merged, yc = merge_fwd(x0, proj[0], ya, yb, outs, lses, p["w_branch"], p["w_out"])
    x2 = cross_fwd(x1, p["g_cross"], p["w_xq"], p["w_xo"], kv)
    x3 = mlp_fwd(x2, p["g_mlp"], p["w_up"], p["w_down"])
    saved = dict(x0=x0, x1=x1, x2=x2, hs=hs, proj=proj, ya=ya, xs=xs, yb=yb, gst=gst, att=att, outs=outs, lses=lses,
                 merged=merged, yc=yc)
    return x3, saved


def _layer_bwd(dx3, p, kv, memn, sv):
    g = {}
    tn = lambda a, b, **kw: matmul(a, b, ta=True, out_dtype=BF16, name="wgrad", **kw)
    dx2, g["g_mlp"], h3, dup, act = mlp_bwd(sv["x2"], p["g_mlp"], p["w_up"], p["w_down"], dx3)
    g["w_up"], g["w_down"] = tn(h3, dup), tn(act, dx3)
    dx1, g["g_cross"], h2, dq, o, dkv = cross_bwd(sv["x1"], p["g_cross"], p["w_xq"], p["w_xo"], kv, dx2)
    g["w_xq"], g["w_xo"], g["w_xkv"] = tn(h2, dq), tn(o, dx2), tn(memn, dkv)
    dmemn = matmul(dkv, p["w_xkv"], tb=True, name="dmem")
    proj = sv["proj"]
    r = merge_bwd(dx1, proj[0], sv["ya"], sv["yb"], sv["yc"], sv["outs"], sv["lses"], p["w_branch"], p["w_out"])
    dgl, dz, dya, dyb, douts, dlses = r[0], r[1:4], r[4], r[5], r[6:9], r[9:12]
    g["w_branch"] = jnp.stack([tn(y, dz[n]) for n, y in enumerate((sv["ya"], sv["yb"], sv["yc"]))])
    g["w_out"] = tn(sv["merged"], dx1)
    datt = []
    for k in range(3):
        dil = ATT_GROUPS[k][1]
        datt.append(attn_bwd(proj[k], A_Q0 if k == 0 else 0, k, sv["att"][k][0], sv["att"][k][1],
                             _by_residue(douts[k], dil), _by_residue(dlses[k], dil)))
    du, adj, gg, dpre, dy, dlam, g["s5_d"], g["b_glu"] = s5_bwd(proj[0], A_U, sv["xs"], p["bd"], p["cd"], p["lam"], p["s5_d"],
                                                                 p["w_glu"], p["b_glu"], dya)
    dbd = matmul(proj[0], adj, ta=True, a_col=(A_U, MIX_W), name="s5_dbd")
    dcd = matmul(sv["xs"], dy, ta=True, name="s5_dcd")
    g["w_glu"] = tn(gg, dpre)
    (g["s5_a_re"], g["s5_a_im"], g["s5_log_step"], g["s5_b_re"], g["s5_b_im"], g["s5_c_re"],
     g["s5_c_im"]) = s5_param_grads(p["disc_in"], dlam, dbd, dcd)
    dqg, dkg, dvg, dlr, drg, dwg, g["b_gla_gate"], g["g_gla_out"] = gla_bwd(proj[0], GLA_COLS, p["w_gate"], p["b_gla_gate"],
                                                                             p["g_gla_out"], sv["gst"], dyb)
    g["w_gla_gate"] = dwg[:GLA_GATE_RANK]
    s = dx3.shape[0]
    dproj = [jnp.concatenate([dgl, du, dqg, dkg, dvg, drg, *datt[0], dlr, jnp.zeros((s, A_W - A_LR - LR_PAD), BF16)], axis=1),
             jnp.concatenate(datt[1], axis=1), jnp.concatenate(datt[2], axis=1)]
    g["w_in"] = _join_w_in(*[tn(sv["hs"][k], dproj[k]) for k in range(3)])
    dhs = [_in_order(matmul(dproj[k], p["w_in_seg"][k], tb=True, name="in_proj_bwd"), ATT_GROUPS[k][1]) for k in range(3)]
    dx0, g["g_mix"] = rmsnorm_bwd(sv["x0"], p["g_mix"], dhs, dx1, name="mix_norm_bwd")
    return dx0, g, dmemn


GLA_GATE_RANK = 16


def kernel(x, mem, *rest):
    nw = len(WEIGHTS)
    w = dict(zip(WEIGHTS, rest[:nw]))
    target = rest[nw]
    m = dict(zip(WEIGHTS, rest[nw + 1 : 2 * nw + 1]))
    v = dict(zip(WEIGHTS, rest[2 * nw + 1 : 3 * nw + 1]))
    x0, memx, target = x[0], mem[0], target[0]
    chip = 2 * lax.axis_index("x") + lax.axis_index("y")

    big = list(SHARDED)
    gate_pad = jnp.pad(w["w_gla_gate"], ((0, 0), (0, 0), (0, LR_PAD - w["w_gla_gate"].shape[2])))
    gathered = all_gather([w[n].astype(BF16) for n in big] + [gate_pad], by_core=True, name="gather_weights")
    gathered = [t.reshape(N_CHIPS, DEPTH, *t.shape[1:]) for t in gathered]
    full = {n: [jnp.concatenate([t[j, l] for j in range(N_CHIPS)], axis=SHARDED[n]) for l in range(DEPTH)]
            for n, t in zip(big, gathered)}
    gate_full = [jnp.concatenate([gathered[-1][j, l][:, : w["w_gla_gate"].shape[2]] for j in range(N_CHIPS)], axis=1)
                 for l in range(DEPTH)]

    memn = rmsnorm(memx, w["g_mem"], name="mem_norm")
    params, kvs = [], []
    for l in range(DEPTH):
        p = {n: full[n][l] for n in big if n != "w_in"}
        p["w_in_seg"] = _split_w_in(full["w_in"][l])
        p["w_gate"] = jnp.pad(gate_full[l], ((0, LR_PAD - GLA_GATE_RANK), (0, 0))).astype(BF16)
        for n in ("g_mix", "g_cross", "g_mlp"):
            p[n] = w[n][l]
        for n in ("s5_d", "b_glu", "b_gla_gate", "g_gla_out"):
            p[n] = w[n][l].reshape(1, -1)
        p["disc_in"], p["lam"], p["bd"], p["cd"] = s5_prepare(*[w[n][l] for n in WEIGHTS[2:9]])
        params.append(p)
        kvs.append(matmul(memn, p["w_xkv"], out_dtype=BF16, name="mem_kv"))

    xl, saved = x0, []
    for l in range(DEPTH):
        xl, sv = _layer_fwd(xl, params[l], kvs[l])
        saved.append(sv)
    loss8, dx, dg_final = loss_head(xl, w["g_final"], target)
    grads, dmem = [None] * DEPTH, []
    for l in reversed(range(DEPTH)):
        dx, grads[l], dm = _layer_bwd(dx, params[l], kvs[l], memn, saved[l])
        dmem.append(dm)
    _, dg_mem = rmsnorm_bwd(memx, w["g_mem"], dmem, jnp.zeros_like(memx), name="mem_norm_bwd")

    core = lax.axis_index("c")

    def halves(n):
        ax = SHARDED[n]
        keep, send = [], []
        for j in range(N_CHIPS):
            p0, p1 = [jnp.split(grads[l][n], N_CHIPS, axis=ax)[j] for l in range(DEPTH)]
            keep.append(jnp.where(core == 0, p0, p1))
            send.append(jnp.where(core == 0, p1, p0))
        flat = lambda ps: jnp.stack(ps).reshape(N_CHIPS, -1, ps[0].shape[-1])
        return flat(keep), flat(send)

    kept, sent = zip(*[halves(n) for n in big])
    theirs = sibling_swap(list(sent), name="swap_layers")
    chip_sums = [add_pair(a, b, name="add_cores") for a, b in zip(kept, theirs)]
    landed = chip_exchange(chip_sums, name="exchange_grads")
    reduced = [sum_blocks(t, name="sum_grads") for t in landed]
    other = sibling_swap([t[None] for t in reduced], name="pair_layers")
    out_g = {}
    for n, mine, theirs in zip(big, reduced, other):
        both = jnp.where(core == 0, jnp.stack([mine, theirs[0]]), jnp.stack([theirs[0], mine]))
        out_g[n] = both.reshape(w[n].shape)

    small = [n for n in WEIGHTS if n not in SHARDED]
    local_small = []
    for n in small:
        if n == "g_mem":
            local_small.append(dg_mem.reshape(w[n].shape))
        elif n == "g_final":
            local_small.append(dg_final.reshape(w[n].shape))
        elif n == "w_gla_gate":
            local_small.append(jnp.stack([grads[l][n] for l in range(DEPTH)]))
        else:
            local_small.append(jnp.stack([grads[l][n].reshape(w[n].shape[1:]) for l in range(DEPTH)]))
    packed = _pack(local_small + [loss8[0, :1]])
    (every,) = all_gather([packed], name="gather_small")
    summed = _unpack(sum_blocks(every, name="sum_small"), local_small + [loss8[0, :1]])
    loss = summed[-1].reshape(())
    for n, t in zip(small, summed[:-1]):
        if n == "w_gla_gate":
            t = lax.dynamic_slice_in_dim(t, chip * w[n].shape[2], w[n].shape[2], axis=2)
        out_g[n] = t

    delta, new_m, new_v = {}, {}, {}
    for n in big:
        c = w[n].shape[-1]
        r = [t.reshape(w[n].shape) for t in adamw(w[n].reshape(-1, c), out_g[n].reshape(-1, c), m[n].reshape(-1, c),
                                                  v[n].reshape(-1, c), name="adamw")]
        delta[n], new_m[n], new_v[n] = r
    like = [w[n] for n in small]
    r = adamw(_pack(like), _pack([out_g[n] for n in small]), _pack([m[n] for n in small]), _pack([v[n] for n in small]),
              name="adamw_small")
    for d, t in zip((delta, new_m, new_v), r):
        d.update(zip(small, _unpack(t, like)))
    return (loss, dx[None], *[out_g[n] for n in WEIGHTS], *[delta[n] for n in WEIGHTS], *[new_m[n] for n in WEIGHTS],
            *[new_v[n] for n in WEIGHTS])
```

```python
import functools
import math

import jax
import jax.numpy as jnp
import numpy as np
from jax import lax
from jax.experimental import pallas as pl
from jax.experimental.pallas import tpu as pltpu

F32 = jnp.float32
BF16 = jnp.bfloat16

VMEM_LIMIT_BYTES = 56 * 1024 * 1024
MATMUL_VMEM_BYTES = 36 * 1024 * 1024


def _cparams(sem):
    return pltpu.CompilerParams(dimension_semantics=sem, vmem_limit_bytes=VMEM_LIMIT_BYTES)


def _dot(a, b, ca=1, cb=0):
    return lax.dot_general(a.astype(BF16), b.astype(BF16), (((ca,), (cb,)), ((), ())), preferred_element_type=F32)


def _pick(n, prefs):
    for p in prefs:
        if n % p == 0:
            return p
    return n


def matmul(a, b, *, ta=False, tb=False, out_dtype=F32, name="mm", a_col=None, b_col=None):
    a_off, a_w = a_col if a_col is not None else (0, a.shape[1])
    b_off, b_w = b_col if b_col is not None else (0, b.shape[1])
    if ta:
        kk, m = a.shape[0], a_w
    else:
        m, kk = a.shape[0], a_w
    if tb:
        n, kb = b.shape[0], b_w
    else:
        kb, n = b.shape[0], b_w
    assert kk == kb, (a.shape, b.shape, ta, tb)
    tm = _pick(m, (512, 256, 128))
    tn = _pick(n, (1024, 512, 256, 128))
    sa, sb, so = a.dtype.itemsize, b.dtype.itemsize, jnp.dtype(out_dtype).itemsize
    fits = lambda t: 2 * (tm * t * sa + t * tn * sb) + tm * tn * (4 + 2 * so) <= MATMUL_VMEM_BYTES
    tk = max([t for t in range(128, kk + 1, 128) if kk % t == 0 and fits(t)] or [_pick(kk, (128,))])
    nk = kk // tk
    a_bytes, b_bytes = m * kk * sa, kk * n * sb
    n_outer = nk == 1 and b_bytes + a_bytes * (n // tn) < a_bytes + b_bytes * (m // tm)
    ij = (lambda g0, g1: (g1, g0)) if n_outer else (lambda g0, g1: (g0, g1))
    if ta:
        assert a_off % tm == 0
        a_spec = pl.BlockSpec((tk, tm), lambda g0, g1, k, o=a_off // tm: (k, ij(g0, g1)[0] + o))
    else:
        assert a_off % tk == 0
        a_spec = pl.BlockSpec((tm, tk), lambda g0, g1, k, o=a_off // tk: (ij(g0, g1)[0], k + o))
    if tb:
        assert b_off % tk == 0
        b_spec = pl.BlockSpec((tn, tk), lambda g0, g1, k, o=b_off // tk: (ij(g0, g1)[1], k + o))
    else:
        assert b_off % tn == 0
        b_spec = pl.BlockSpec((tk, tn), lambda g0, g1, k, o=b_off // tn: (k, ij(g0, g1)[1] + o))

    def body(a_ref, b_ref, o_ref, *acc):
        k = pl.program_id(2)
        p = _dot(a_ref[...], b_ref[...], 0 if ta else 1, 1 if tb else 0)
        if nk == 1:
            o_ref[...] = p.astype(o_ref.dtype)
            return
        (acc_ref,) = acc

        @pl.when(k == 0)
        def _():
            acc_ref[...] = p

        @pl.when(jnp.logical_and(k > 0, k < nk - 1))
        def _():
            acc_ref[...] += p

        @pl.when(k == nk - 1)
        def _():
            o_ref[...] = (acc_ref[...] + p).astype(o_ref.dtype)

    return pl.pallas_call(
        body,
        name=name,
        grid=(n // tn, m // tm, nk) if n_outer else (m // tm, n // tn, nk),
        in_specs=[a_spec, b_spec],
        out_specs=pl.BlockSpec((tm, tn), lambda g0, g1, k: ij(g0, g1)),
        out_shape=jax.ShapeDtypeStruct((m, n), out_dtype),
        scratch_shapes=[pltpu.VMEM((tm, tn), F32)] if nk > 1 else [],
        compiler_params=_cparams(("parallel", "parallel", "arbitrary")),
    )(a, b)


RMS_EPS = 1e-6
ROW_TILE = 512


def _rms_fwd(x, g):
    r = lax.rsqrt(jnp.mean(x * x, axis=-1, keepdims=True) + RMS_EPS)
    return x * r * g


def _rms_bwd(x, g, dh):
    r = lax.rsqrt(jnp.mean(x * x, axis=-1, keepdims=True) + RMS_EPS)
    xh = x * r
    dg = jnp.sum(dh * xh, axis=0, keepdims=True)
    dxh = dh * g
    dx = r * (dxh - xh * jnp.mean(dxh * xh, axis=-1, keepdims=True))
    return dx, dg


def _accum(ref, val, first):
    @pl.when(first)
    def _():
        ref[...] = val

    @pl.when(jnp.logical_not(first))
    def _():
        ref[...] += val


def rmsnorm(x, g, *, out_dtype=BF16, name="rmsnorm"):
    s, d = x.shape
    tm = _pick(s, (ROW_TILE, 256, 128, 8))

    def body(x_ref, g_ref, o_ref):
        o_ref[...] = _rms_fwd(x_ref[...], g_ref[...]).astype(o_ref.dtype)

    return pl.pallas_call(
        body,
        name=name,
        grid=(s // tm,),
        in_specs=[pl.BlockSpec((tm, d), lambda i: (i, 0)), pl.BlockSpec((1, d), lambda i: (0, 0))],
        out_specs=pl.BlockSpec((tm, d), lambda i: (i, 0)),
        out_shape=jax.ShapeDtypeStruct((s, d), out_dtype),
        compiler_params=_cparams(("arbitrary",)),
    )(x, g.reshape(1, d))


def rmsnorm_bwd(x, g, dhs, dres, *, name="rmsnorm_bwd"):
    s, d = x.shape
    tm = _pick(s, (ROW_TILE, 256, 128, 8))
    n = len(dhs)

    def body(x_ref, g_ref, *refs):
        dh_refs, dres_ref, dx_ref, dg_ref = refs[:n], refs[n], refs[n + 1], refs[n + 2]
        dh = dh_refs[0][...].astype(F32)
        for r in dh_refs[1:]:
            dh = dh + r[...].astype(F32)
        dx, dg = _rms_bwd(x_ref[...], g_ref[...], dh)
        dx_ref[...] = dres_ref[...] + dx
        _accum(dg_ref, dg, pl.program_id(0) == 0)

    row = pl.BlockSpec((tm, d), lambda i: (i, 0))
    vec = pl.BlockSpec((1, d), lambda i: (0, 0))
    return pl.pallas_call(
        body,
        name=name,
        grid=(s // tm,),
        in_specs=[row, vec] + [row] * n + [row],
        out_specs=[row, vec],
        out_shape=[jax.ShapeDtypeStruct((s, d), F32), jax.ShapeDtypeStruct((1, d), F32)],
        compiler_params=_cparams(("arbitrary",)),
    )(x, g.reshape(1, d), *dhs, dres)


def loss_head(x, g, target, *, name="loss_head"):
    s, d = x.shape
    tm = _pick(s, (ROW_TILE, 256, 128, 8))

    def body(x_ref, g_ref, t_ref, l_ref, dx_ref, dg_ref):
        x_, g_ = x_ref[...], g_ref[...]
        e = _rms_fwd(x_, g_) - t_ref[...]
        part = 0.5 * jnp.sum(jnp.sum(e * e, axis=-1, keepdims=True), axis=0, keepdims=True) / d
        dx, dg = _rms_bwd(x_, g_, e * (1.0 / d))
        dx_ref[...] = dx
        first = pl.program_id(0) == 0
        _accum(dg_ref, dg, first)
        _accum(l_ref, jnp.broadcast_to(part, (8, 128)), first)

    row = pl.BlockSpec((tm, d), lambda i: (i, 0))
    vec = pl.BlockSpec((1, d), lambda i: (0, 0))
    return pl.pallas_call(
        body,
        name=name,
        grid=(s // tm,),
        in_specs=[row, vec, row],
        out_specs=[pl.BlockSpec((8, 128), lambda i: (0, 0)), row, vec],
        out_shape=[jax.ShapeDtypeStruct((8, 128), F32), jax.ShapeDtypeStruct((s, d), F32), jax.ShapeDtypeStruct((1, d), F32)],
        compiler_params=_cparams(("arbitrary",)),
    )(x, g.reshape(1, d), target)


FF_TILE = 1024


def mlp_fwd(x, g, w_up, w_down, *, name="mlp_fwd"):
    s, d = x.shape
    ff = w_up.shape[1]
    tm, tf = _pick(s, (ROW_TILE, 256, 128)), _pick(ff, (FF_TILE, 512, 256, 128))
    nf = ff // tf

    def body(x_ref, g_ref, wu_ref, wd_ref, o_ref, h_ref, acc_ref):
        f = pl.program_id(1)

        @pl.when(f == 0)
        def _():
            h_ref[...] = _rms_fwd(x_ref[...], g_ref[...]).astype(BF16)
            acc_ref[...] = x_ref[...]

        up = jnp.maximum(_dot(h_ref[...], wu_ref[...]), 0.0)
        acc_ref[...] += _dot(up * up, wd_ref[...])

        @pl.when(f == nf - 1)
        def _():
            o_ref[...] = acc_ref[...]

    return pl.pallas_call(
        body,
        name=name,
        grid=(s // tm, nf),
        in_specs=[
            pl.BlockSpec((tm, d), lambda i, f: (i, 0)),
            pl.BlockSpec((1, d), lambda i, f: (0, 0)),
            pl.BlockSpec((d, tf), lambda i, f: (0, f)),
            pl.BlockSpec((tf, d), lambda i, f: (f, 0)),
        ],
        out_specs=pl.BlockSpec((tm, d), lambda i, f: (i, 0)),
        out_shape=jax.ShapeDtypeStruct((s, d), F32),
        scratch_shapes=[pltpu.VMEM((tm, d), BF16), pltpu.VMEM((tm, d), F32)],
        compiler_params=_cparams(("arbitrary", "arbitrary")),
    )(x, g.reshape(1, d), w_up, w_down)


def mlp_bwd(x, g, w_up, w_down, dy, *, name="mlp_bwd"):
    s, d = x.shape
    ff = w_up.shape[1]
    tm, tf = _pick(s, (ROW_TILE, 256, 128)), _pick(ff, (FF_TILE, 512, 256, 128))
    nf = ff // tf

    def body(x_ref, g_ref, wu_ref, wd_ref, dy_ref, dx_ref, dg_ref, h_ref, dup_ref, act_ref, dyb_ref, acc_ref):
        i, f = pl.program_id(0), pl.program_id(1)

        @pl.when(f == 0)
        def _():
            h_ref[...] = _rms_fwd(x_ref[...], g_ref[...]).astype(BF16)
            dyb_ref[...] = dy_ref[...].astype(BF16)
            acc_ref[...] = jnp.zeros_like(acc_ref)

        up = jnp.maximum(_dot(h_ref[...], wu_ref[...]), 0.0)
        dact = _dot(dyb_ref[...], wd_ref[...], 1, 1)
        dup = (2.0 * up * dact).astype(BF16)
        dup_ref[...] = dup
        act_ref[...] = (up * up).astype(BF16)
        acc_ref[...] += _dot(dup, wu_ref[...], 1, 1)

        @pl.when(f == nf - 1)
        def _():
            dx, dg = _rms_bwd(x_ref[...], g_ref[...], acc_ref[...])
            dx_ref[...] = dy_ref[...] + dx
            _accum(dg_ref, dg, i == 0)

    row = pl.BlockSpec((tm, d), lambda i, f: (i, 0))
    vec = pl.BlockSpec((1, d), lambda i, f: (0, 0))
    wide = pl.BlockSpec((tm, tf), lambda i, f: (i, f))
    return pl.pallas_call(
        body,
        name=name,
        grid=(s // tm, nf),
        in_specs=[row, vec, pl.BlockSpec((d, tf), lambda i, f: (0, f)), pl.BlockSpec((tf, d), lambda i, f: (f, 0)), row],
        out_specs=[row, vec, row, wide, wide],
        out_shape=[
            jax.ShapeDtypeStruct((s, d), F32),
            jax.ShapeDtypeStruct((1, d), F32),
            jax.ShapeDtypeStruct((s, d), BF16),
            jax.ShapeDtypeStruct((s, ff), BF16),
            jax.ShapeDtypeStruct((s, ff), BF16),
        ],
        scratch_shapes=[pltpu.VMEM((tm, d), BF16), pltpu.VMEM((tm, d), F32)],
        compiler_params=_cparams(("arbitrary", "arbitrary")),
    )(x, g.reshape(1, d), w_up, w_down, dy)


X_HEADS = 4


def _softmax_rows(s):
    m = jnp.max(s, axis=-1, keepdims=True)
    e = jnp.exp(s - m)
    return e / jnp.sum(e, axis=-1, keepdims=True)


def cross_fwd(x, g, wq, wo, kv, *, name="cross_fwd"):
    s, d = x.shape
    ml = kv.shape[0]
    dh = d // X_HEADS
    tm = _pick(s, (ROW_TILE, 256, 128))
    scale = dh**-0.5

    def body(x_ref, g_ref, wq_ref, wo_ref, kv_ref, o_ref):
        x_ = x_ref[...]
        q = _dot(_rms_fwd(x_, g_ref[...]), wq_ref[...])
        outs = []
        for hd in range(X_HEADS):
            kh = kv_ref[:, hd * dh : (hd + 1) * dh]
            vh = kv_ref[:, d + hd * dh : d + (hd + 1) * dh]
            p = _softmax_rows(_dot(q[:, hd * dh : (hd + 1) * dh], kh, 1, 1) * scale)
            outs.append(_dot(p, vh))
        o_ref[...] = x_ + _dot(jnp.concatenate(outs, axis=-1), wo_ref[...])

    row = pl.BlockSpec((tm, d), lambda i: (i, 0))
    full = lambda shp: pl.BlockSpec(shp, lambda i: (0, 0))
    return pl.pallas_call(
        body,
        name=name,
        grid=(s // tm,),
        in_specs=[row, full((1, d)), full((d, d)), full((d, d)), full((ml, 2 * d))],
        out_specs=row,
        out_shape=jax.ShapeDtypeStruct((s, d), F32),
        compiler_params=_cparams(("arbitrary",)),
    )(x, g.reshape(1, d), wq, wo, kv)


def cross_bwd(x, g, wq, wo, kv, dy, *, name="cross_bwd"):
    s, d = x.shape
    ml = kv.shape[0]
    dh = d // X_HEADS
    tm = _pick(s, (ROW_TILE, 256, 128))
    scale = dh**-0.5

    def body(x_ref, g_ref, wq_ref, wo_ref, kv_ref, dy_ref, dx_ref, dg_ref, h_ref, dq_ref, o_ref, dkv_ref):
        first = pl.program_id(0) == 0
        x_, g_ = x_ref[...], g_ref[...]
        h = _rms_fwd(x_, g_).astype(BF16)
        h_ref[...] = h
        q = _dot(h, wq_ref[...])
        dy_ = dy_ref[...]
        do = _dot(dy_, wo_ref[...], 1, 1)
        outs, dqs, dks, dvs = [], [], [], []
        for hd in range(X_HEADS):
            sl = slice(hd * dh, (hd + 1) * dh)
            kh = kv_ref[:, sl]
            vh = kv_ref[:, d + hd * dh : d + (hd + 1) * dh]
            qh = q[:, sl]
            p = _softmax_rows(_dot(qh, kh, 1, 1) * scale)
            outs.append(_dot(p, vh))
            doh = do[:, sl]
            dp = _dot(doh, vh, 1, 1)
            ds = p * (dp - jnp.sum(dp * p, axis=-1, keepdims=True)) * scale
            dqs.append(_dot(ds, kh))
            dks.append(_dot(ds, qh, 0, 0))
            dvs.append(_dot(p, doh, 0, 0))
        o_ref[...] = jnp.concatenate(outs, axis=-1).astype(BF16)
        dq = jnp.concatenate(dqs, axis=-1).astype(BF16)
        dq_ref[...] = dq
        _accum(dkv_ref, jnp.concatenate(dks + dvs, axis=-1), first)
        dx, dg = _rms_bwd(x_, g_, _dot(dq, wq_ref[...], 1, 1))
        dx_ref[...] = dy_ + dx
        _accum(dg_ref, dg, first)

    row = pl.BlockSpec((tm, d), lambda i: (i, 0))
    full = lambda shp: pl.BlockSpec(shp, lambda i: (0, 0))
    rowb = jax.ShapeDtypeStruct((s, d), BF16)
    return pl.pallas_call(
        body,
        name=name,
        grid=(s // tm,),
        in_specs=[row, full((1, d)), full((d, d)), full((d, d)), full((ml, 2 * d)), row],
        out_specs=[row, full((1, d)), row, row, row, full((ml, 2 * d))],
        out_shape=[jax.ShapeDtypeStruct((s, d), F32), jax.ShapeDtypeStruct((1, d), F32), rowb, rowb, rowb,
                   jax.ShapeDtypeStruct((ml, 2 * d), F32)],
        compiler_params=_cparams(("arbitrary",)),
    )(x, g.reshape(1, d), wq, wo, kv, dy)


N_BRANCH = 3
MIX_W = 512
ATT_DH = 128
ATT_HG = 4
MERGE_TILE = 256


def _group_weights(l0, l1, l2):
    m = jnp.maximum(jnp.maximum(l0, l1), l2)
    e = [jnp.exp(l0 - m), jnp.exp(l1 - m), jnp.exp(l2 - m)]
    inv = 1.0 / (e[0] + e[1] + e[2])
    return [t * inv for t in e]


def _sigmoid(x):
    return 1.0 / (1.0 + jnp.exp(-x))


def merge_fwd(x, proj, ya, yb, outs, lses, wb, wout, *, name="merge_fwd"):
    s, d = x.shape
    tm = _pick(s, (MERGE_TILE, 128))

    def body(x_ref, g0, g1, g2, ya_ref, yb_ref, o0, o1, o2, l0, l1, l2, wb_ref, wo_ref, x1_ref, mg_ref, yc_ref):
        w = _group_weights(l0[...], l1[...], l2[...])
        yc = w[0] * o0[...] + w[1] * o1[...] + w[2] * o2[...]
        yc_ref[...] = yc.astype(BF16)
        merged = None
        for n, (y, gl) in enumerate(((ya_ref[...], g0), (yb_ref[...], g1), (yc, g2))):
            t = _sigmoid(gl[...]) * _dot(y, wb_ref[n])
            merged = t if merged is None else merged + t
        mb = merged.astype(BF16)
        mg_ref[...] = mb
        x1_ref[...] = x_ref[...] + _dot(mb, wo_ref[...])

    row = pl.BlockSpec((tm, d), lambda i: (i, 0))
    half = pl.BlockSpec((tm, MIX_W), lambda i: (i, 0))
    gate = [pl.BlockSpec((tm, d), lambda i, n=n: (i, n)) for n in range(N_BRANCH)]
    return pl.pallas_call(
        body,
        name=name,
        grid=(s // tm,),
        in_specs=[row] + gate + [half] * 8
        + [pl.BlockSpec((N_BRANCH, MIX_W, d), lambda i: (0, 0, 0)), pl.BlockSpec((d, d), lambda i: (0, 0))],
        out_specs=[row, row, half],
        out_shape=[jax.ShapeDtypeStruct((s, d), F32), jax.ShapeDtypeStruct((s, d), BF16), jax.ShapeDtypeStruct((s, MIX_W), BF16)],
        compiler_params=_cparams(("arbitrary",)),
    )(x, proj, proj, proj, ya, yb, *outs, *lses, wb, wout)


def merge_bwd(dx1, proj, ya, yb, yc, outs, lses, wb, wout, *, name="merge_bwd"):
    s, d = dx1.shape
    tm = _pick(s, (MERGE_TILE, 128))

    def body(dx_ref, g0, g1, g2, ya_ref, yb_ref, yc_ref, o0, o1, o2, l0, l1, l2, wb_ref, wo_ref,
             dgl_ref, dz0, dz1, dz2, dya_ref, dyb_ref, do0, do1, do2, dl0, dl1, dl2):
        dm = _dot(dx_ref[...], wo_ref[...], 1, 1)
        dys = []
        for n, (y, gl, dz_ref) in enumerate(((ya_ref, g0, dz0), (yb_ref, g1, dz1), (yc_ref, g2, dz2))):
            z = _dot(y[...], wb_ref[n])
            sg = _sigmoid(gl[...])
            dz = (dm * sg).astype(BF16)
            dz_ref[...] = dz
            dgl_ref[:, n * d : (n + 1) * d] = (dm * z * sg * (1.0 - sg)).astype(BF16)
            dys.append(_dot(dz, wb_ref[n], 1, 1))
        dya_ref[...] = dys[0]
        dyb_ref[...] = dys[1]
        dyc = dys[2]
        w = _group_weights(l0[...], l1[...], l2[...])
        o = [o0[...], o1[...], o2[...]]
        for gi, r in enumerate((do0, do1, do2)):
            r[...] = w[gi] * dyc
        for hd in range(ATT_HG):
            sl = slice(hd * ATT_DH, (hd + 1) * ATT_DH)
            t = [jnp.sum(dyc[:, sl] * o[gi][:, sl], axis=-1, keepdims=True) for gi in range(3)]
            wh = [w[gi][:, sl] for gi in range(3)]
            tbar = wh[0] * t[0] + wh[1] * t[1] + wh[2] * t[2]
            for gi, r in enumerate((dl0, dl1, dl2)):
                r[:, sl] = wh[gi] * (t[gi] - tbar)

    row = pl.BlockSpec((tm, d), lambda i: (i, 0))
    half = pl.BlockSpec((tm, MIX_W), lambda i: (i, 0))
    gate = [pl.BlockSpec((tm, d), lambda i, n=n: (i, n)) for n in range(N_BRANCH)]
    rb = jax.ShapeDtypeStruct((s, d), BF16)
    hf = jax.ShapeDtypeStruct((s, MIX_W), F32)
    return pl.pallas_call(
        body,
        name=name,
        grid=(s // tm,),
        in_specs=[row] + gate + [half] * 9
        + [pl.BlockSpec((N_BRANCH, MIX_W, d), lambda i: (0, 0, 0)), pl.BlockSpec((d, d), lambda i: (0, 0))],
        out_specs=[pl.BlockSpec((tm, N_BRANCH * d), lambda i: (i, 0)), row, row, row] + [half] * 8,
        out_shape=[jax.ShapeDtypeStruct((s, N_BRANCH * d), BF16), rb, rb, rb] + [hf] * 8,
        compiler_params=_cparams(("arbitrary",)),
    )(dx1, proj, proj, proj, ya, yb, yc, *outs, *lses, wb, wout)


ATT_BLOCK = 128
ATT_STEP = 4 * ATT_BLOCK
ATT_GROUPS = ((128, 1), (512, 4), (2048, 16))
N_ATT_HEADS = ATT_HG * len(ATT_GROUPS)
ALIBI_MAX_EXP = 8.0
MASKED = -1e30


def _att_slopes(group):
    return [2.0 ** (-ALIBI_MAX_EXP * (group * ATT_HG + h + 1) / N_ATT_HEADS) for h in range(ATT_HG)]


def _att_scores(q, kp, kc, slope_dil, has_prev):
    scale = ATT_DH**-0.5
    qi = lax.broadcasted_iota(jnp.int32, (ATT_BLOCK, ATT_BLOCK), 0)
    kj = lax.broadcasted_iota(jnp.int32, (ATT_BLOCK, ATT_BLOCK), 1)
    dist = qi - kj
    sc = _dot(q, kc, 1, 1) * scale - slope_dil * dist.astype(F32)
    sc = jnp.where(dist >= 0, sc, MASKED)
    sp = _dot(q, kp, 1, 1) * scale - slope_dil * (dist + ATT_BLOCK).astype(F32)
    sp = jnp.where(jnp.logical_and(dist <= 0, has_prev), sp, MASKED)
    return sp, sc


def attn_fwd(qkv, col, group, *, name="attn_fwd"):
    s = qkv.shape[0]
    window, dil = ATT_GROUPS[group]
    assert window // dil == ATT_BLOCK and s % (dil * ATT_BLOCK) == 0 and col % MIX_W == 0
    nb = s // dil // ATT_BLOCK
    nq = ATT_STEP // ATT_BLOCK
    ng = s // ATT_STEP
    slopes = _att_slopes(group)
    c0 = col // MIX_W

    def body(q_ref, kp_ref, kc_ref, vp_ref, vc_ref, o_ref, l_ref):
        n = pl.program_id(0)
        for i in range(nq):
            r = slice(i * ATT_BLOCK, (i + 1) * ATT_BLOCK)
            rp = slice((i - 1) * ATT_BLOCK, i * ATT_BLOCK)
            has_prev = ((n * nq + i) % nb) != 0
            for hd in range(ATT_HG):
                sl = slice(hd * ATT_DH, (hd + 1) * ATT_DH)
                kp = kp_ref[:, sl] if i == 0 else kc_ref[rp, sl]
                vp = vp_ref[:, sl] if i == 0 else vc_ref[rp, sl]
                sp, sc = _att_scores(q_ref[r, sl], kp, kc_ref[r, sl], slopes[hd] * dil, has_prev)
                m = jnp.maximum(jnp.max(sp, axis=-1, keepdims=True), jnp.max(sc, axis=-1, keepdims=True))
                ep, ec = jnp.exp(sp - m), jnp.exp(sc - m)
                den = jnp.sum(ep, axis=-1, keepdims=True) + jnp.sum(ec, axis=-1, keepdims=True)
                inv = 1.0 / den
                o_ref[r, sl] = _dot(ep * inv, vp) + _dot(ec * inv, vc_ref[r, sl])
                l_ref[r, sl] = jnp.broadcast_to(m + jnp.log(den), (ATT_BLOCK, ATT_DH))

    cur = lambda c: pl.BlockSpec((ATT_STEP, MIX_W), lambda n, c=c: (n, c))
    prev = lambda c: pl.BlockSpec((ATT_BLOCK, MIX_W), lambda n, c=c: (jnp.maximum(n * nq - 1, 0), c))
    blk = pl.BlockSpec((ATT_STEP, MIX_W), lambda n: (n, 0))
    return pl.pallas_call(
        body,
        name=name,
        grid=(ng,),
        in_specs=[cur(c0), prev(c0 + 1), cur(c0 + 1), prev(c0 + 2), cur(c0 + 2)],
        out_specs=[blk, blk],
        out_shape=[jax.ShapeDtypeStruct((s, MIX_W), F32)] * 2,
        compiler_params=_cparams(("arbitrary",)),
    )(qkv, qkv, qkv, qkv, qkv)


def attn_bwd(qkv, col, group, out, lse, dout, dlse, *, name="attn_bwd"):
    s = qkv.shape[0]
    window, dil = ATT_GROUPS[group]
    nb = s // dil // ATT_BLOCK
    nq = ATT_STEP // ATT_BLOCK
    ng = s // ATT_STEP
    slopes = _att_slopes(group)
    c0 = col // MIX_W
    scale = ATT_DH**-0.5
    tail = slice((nq - 1) * ATT_BLOCK, nq * ATT_BLOCK)

    def body(q_ref, kp_ref, kc_ref, vp_ref, vc_ref, o_ref, l_ref, do_ref, dl_ref, dq_ref, dk_ref, dv_ref, acck, accv):
        n = pl.program_id(0)
        live = n < ng

        @pl.when(n == 0)
        def _():
            acck[...] = jnp.zeros_like(acck)
            accv[...] = jnp.zeros_like(accv)

        dk_ref[: (nq - 1) * ATT_BLOCK, :] = acck[: (nq - 1) * ATT_BLOCK, :].astype(BF16)
        dv_ref[: (nq - 1) * ATT_BLOCK, :] = accv[: (nq - 1) * ATT_BLOCK, :].astype(BF16)
        for i in range(nq):
            r = slice(i * ATT_BLOCK, (i + 1) * ATT_BLOCK)
            rp = slice((i - 1) * ATT_BLOCK, i * ATT_BLOCK)
            has_prev = jnp.logical_and(((n * nq + i) % nb) != 0, live)
            for hd in range(ATT_HG):
                sl = slice(hd * ATT_DH, (hd + 1) * ATT_DH)
                q, kc, vc, do = q_ref[r, sl], kc_ref[r, sl], vc_ref[r, sl], do_ref[r, sl]
                kp = kp_ref[:, sl] if i == 0 else kc_ref[rp, sl]
                vp = vp_ref[:, sl] if i == 0 else vc_ref[rp, sl]
                sp, sc = _att_scores(q, kp, kc, slopes[hd] * dil, has_prev)
                l = l_ref[r, sl][:, :1]
                pp, pc = jnp.exp(sp - l), jnp.exp(sc - l)
                corr = dl_ref[r, sl][:, :1] - jnp.sum(do * o_ref[r, sl], axis=-1, keepdims=True)
                dsp = pp * (_dot(do, vp, 1, 1) + corr) * scale
                dsc = pc * (_dot(do, vc, 1, 1) + corr) * scale
                dqh = (_dot(dsp, kp) + _dot(dsc, kc)).astype(BF16)

                @pl.when(live)
                def _(dqh=dqh, r=r, sl=sl):
                    dq_ref[r, sl] = dqh

                to_prev_k, to_prev_v = _dot(dsp, q, 0, 0), _dot(pp, do, 0, 0)
                if i == 0:
                    dk_ref[tail, sl] = (acck[tail, sl] + to_prev_k).astype(BF16)
                    dv_ref[tail, sl] = (accv[tail, sl] + to_prev_v).astype(BF16)
                else:
                    acck[rp, sl] += to_prev_k
                    accv[rp, sl] += to_prev_v
                acck[r, sl] = jnp.where(live, _dot(dsc, q, 0, 0), 0.0)
                accv[r, sl] = jnp.where(live, _dot(pc, do, 0, 0), 0.0)

    last = ng - 1
    cur = lambda c: pl.BlockSpec((ATT_STEP, MIX_W), lambda n, c=c: (jnp.minimum(n, last), c))
    prev = lambda c: pl.BlockSpec((ATT_BLOCK, MIX_W), lambda n, c=c: (jnp.clip(n * nq - 1, 0, ng * nq - 1), c))
    behind = pl.BlockSpec((ATT_STEP, MIX_W), lambda n: (jnp.clip(n - 1, 0, last), 0))
    sds = jax.ShapeDtypeStruct((s, MIX_W), BF16)
    return pl.pallas_call(
        body,
        name=name,
        grid=(ng + 1,),
        in_specs=[cur(c0), prev(c0 + 1), cur(c0 + 1), prev(c0 + 2), cur(c0 + 2), cur(0), cur(0), cur(0), cur(0)],
        out_specs=[cur(0), behind, behind],
        out_shape=[sds, sds, sds],
        scratch_shapes=[pltpu.VMEM((ATT_STEP, MIX_W), F32), pltpu.VMEM((ATT_STEP, MIX_W), F32)],
        compiler_params=_cparams(("arbitrary",)),
    )(qkv, qkv, qkv, qkv, qkv, out, lse, dout, dlse)


GLA_HEADS = 4
GLA_DK = 64
GLA_DV = 128
GLA_CHUNK = 64
GLA_TAU = 16.0
GLA_QK = GLA_HEADS * GLA_DK
LR_PAD = 128


def _dot_exact(a, b, ca=1, cb=0):
    return lax.dot_general(a, b, (((ca,), (cb,)), ((), ())), precision=lax.Precision.HIGHEST, preferred_element_type=F32)


GLA_STEP = 4 * GLA_CHUNK


def _gla_group(q, k, v, lr, r, state, wg, bg, go):
    c, n = GLA_CHUNK, q.shape[0]
    z = _dot(lr, wg) + bg
    la = (jnp.minimum(z, 0.0) - jnp.log(1.0 + jnp.exp(-jnp.abs(z)))) * (1.0 / GLA_TAU)
    ri = lax.broadcasted_iota(jnp.int32, (n, n), 0)
    ci = lax.broadcasted_iota(jnp.int32, (n, n), 1)
    shift = c.bit_length() - 1
    in_chunk = jnp.logical_and(ri >= ci, jnp.right_shift(ri, shift) == jnp.right_shift(ci, shift))
    ball = _dot_exact(in_chunk.astype(F32), la)
    ri = lax.broadcasted_iota(jnp.int32, (c, c), 0)
    ci = lax.broadcasted_iota(jnp.int32, (c, c), 1)
    causal = ri >= ci
    qs = q * GLA_DK**-0.5
    ys = []
    for j in range(n // c):
        rows = slice(j * c, (j + 1) * c)
        b = ball[rows]
        bmid = b[c // 2 : c // 2 + 1, :]
        bend = b[c - 1 : c, :]
        q_in = qs[rows] * jnp.exp(b)
        q_mid = qs[rows] * jnp.exp(b - bmid)
        k_mid = k[rows] * jnp.exp(bmid - b)
        k_end = k[rows] * jnp.exp(bend - b)
        yh, upd = [], []
        for h in range(GLA_HEADS):
            sk = slice(h * GLA_DK, (h + 1) * GLA_DK)
            sv = slice(h * GLA_DV, (h + 1) * GLA_DV)
            vh = v[rows, sv]
            inter = _dot(q_in[:, sk], state[:, sk], 1, 1)
            sc = jnp.where(causal, _dot_exact(q_mid[:, sk], k_mid[:, sk], 1, 1), 0.0)
            o = inter + _dot(sc, vh)
            o = o * lax.rsqrt(jnp.mean(o * o, axis=-1, keepdims=True) + RMS_EPS) * go
            rh = r[rows, sv]
            yh.append(o * rh * _sigmoid(rh))
            upd.append(_dot(vh, k_end[:, sk], 0, 0))
        state = jnp.exp(bend) * state + jnp.concatenate(upd, axis=-1)
        ys.append(jnp.concatenate(yh, axis=-1))
    return jnp.concatenate(ys, axis=0), state


def _gla_in_specs(cols, rev, nc):
    c = GLA_STEP
    row = (lambda i: nc - 1 - i) if rev else (lambda i: i)
    qc, kc, vc, lc, rc = cols
    assert qc % GLA_QK == 0 and kc % GLA_QK == 0 and vc % MIX_W == 0 and rc % MIX_W == 0 and lc % LR_PAD == 0
    return [
        pl.BlockSpec((c, GLA_QK), lambda i: (row(i), qc // GLA_QK)),
        pl.BlockSpec((c, GLA_QK), lambda i: (row(i), kc // GLA_QK)),
        pl.BlockSpec((c, MIX_W), lambda i: (row(i), vc // MIX_W)),
        pl.BlockSpec((c, LR_PAD), lambda i: (row(i), lc // LR_PAD)),
        pl.BlockSpec((c, MIX_W), lambda i: (row(i), rc // MIX_W)),
    ], row


def gla_fwd(proj, cols, wg, bg, go, *, name="gla_fwd"):
    s = proj.shape[0]
    nc = s // GLA_STEP
    specs, row = _gla_in_specs(cols, False, nc)

    def body(q_ref, k_ref, v_ref, lr_ref, r_ref, wg_ref, bg_ref, go_ref, y_ref, st_ref, state):
        @pl.when(pl.program_id(0) == 0)
        def _():
            state[...] = jnp.zeros_like(state)

        st = state[...]
        st_ref[...] = st
        y, new = _gla_group(q_ref[...], k_ref[...], v_ref[...], lr_ref[...], r_ref[...], st, wg_ref[...], bg_ref[...], go_ref[...])
        y_ref[...] = y
        state[...] = new

    full = lambda shp: pl.BlockSpec(shp, lambda i: (0, 0))
    return pl.pallas_call(
        body,
        name=name,
        grid=(nc,),
        in_specs=specs + [full((LR_PAD, GLA_QK)), full((1, GLA_QK)), full((1, GLA_DV))],
        out_specs=[pl.BlockSpec((GLA_STEP, MIX_W), lambda i: (i, 0)), pl.BlockSpec((GLA_DV, GLA_QK), lambda i: (i, 0))],
        out_shape=[jax.ShapeDtypeStruct((s, MIX_W), F32), jax.ShapeDtypeStruct((nc * GLA_DV, GLA_QK), F32)],
        scratch_shapes=[pltpu.VMEM((GLA_DV, GLA_QK), F32)],
        compiler_params=_cparams(("arbitrary",)),
    )(proj, proj, proj, proj, proj, wg, bg, go)


def gla_bwd(proj, cols, wg, bg, go, states, dy, *, name="gla_bwd"):
    s = proj.shape[0]
    nc = s // GLA_STEP
    specs, row = _gla_in_specs(cols, True, nc)

    def body(q_ref, k_ref, v_ref, lr_ref, r_ref, wg_ref, bg_ref, go_ref, st_ref, dy_ref,
             dq_ref, dk_ref, dv_ref, dlr_ref, dr_ref, dwg_ref, dbg_ref, dgo_ref, dstate):
        first = pl.program_id(0) == 0

        @pl.when(first)
        def _():
            dstate[...] = jnp.zeros_like(dstate)

        _, vjp = jax.vjp(_gla_group, q_ref[...], k_ref[...], v_ref[...], lr_ref[...], r_ref[...], st_ref[...],
                         wg_ref[...].astype(F32), bg_ref[...], go_ref[...])
        dq, dk, dv, dlr, dr, dst, dwg, dbg, dgo = vjp((dy_ref[...], dstate[...]))
        dq_ref[...] = dq.astype(BF16)
        dk_ref[...] = dk.astype(BF16)
        dv_ref[...] = dv.astype(BF16)
        dlr_ref[...] = dlr.astype(BF16)
        dr_ref[...] = dr.astype(BF16)
        dstate[...] = dst
        _accum(dwg_ref, dwg, first)
        _accum(dbg_ref, dbg, first)
        _accum(dgo_ref, dgo, first)

    c = GLA_STEP
    full = lambda shp: pl.BlockSpec(shp, lambda i: (0, 0))
    rows = lambda w: pl.BlockSpec((c, w), lambda i: (row(i), 0))
    return pl.pallas_call(
        body,
        name=name,
        grid=(nc,),
        in_specs=specs + [full((LR_PAD, GLA_QK)), full((1, GLA_QK)), full((1, GLA_DV)),
                          pl.BlockSpec((GLA_DV, GLA_QK), lambda i: (row(i), 0)), rows(MIX_W)],
        out_specs=[rows(GLA_QK), rows(GLA_QK), rows(MIX_W), rows(LR_PAD), rows(MIX_W),
                   full((LR_PAD, GLA_QK)), full((1, GLA_QK)), full((1, GLA_DV))],
        out_shape=[jax.ShapeDtypeStruct((s, GLA_QK), BF16), jax.ShapeDtypeStruct((s, GLA_QK), BF16),
                   jax.ShapeDtypeStruct((s, MIX_W), BF16), jax.ShapeDtypeStruct((s, LR_PAD), BF16),
                   jax.ShapeDtypeStruct((s, MIX_W), BF16), jax.ShapeDtypeStruct((LR_PAD, GLA_QK), F32),
                   jax.ShapeDtypeStruct((1, GLA_QK), F32), jax.ShapeDtypeStruct((1, GLA_DV), F32)],
        scratch_shapes=[pltpu.VMEM((GLA_DV, GLA_QK), F32)],
        compiler_params=_cparams(("arbitrary",)),
    )(proj, proj, proj, proj, proj, wg, bg, go, states, dy)


S5_G = 32
S5_P = 64
S5_C = 16
S5_N = S5_G * S5_P
S5_TC = 256
SUB = 8


def _s5_disc(a_re, a_im, ls, b_re, b_im):
    step = jnp.exp(ls)
    mag = jnp.exp(a_re * step)
    lr, li = mag * jnp.cos(a_im * step), mag * jnp.sin(a_im * step)
    inv = 1.0 / (a_re * a_re + a_im * a_im)
    nr, ni = lr - 1.0, li
    cr, ci = (nr * a_re + ni * a_im) * inv, (ni * a_re - nr * a_im) * inv
    return lr, li, cr * b_re - ci * b_im, cr * b_im + ci * b_re


def s5_disc_fwd(a_re, a_im, ls, b_re, b_im, *, name="s5_disc"):
    def body(ar, ai, l, br, bi, o0, o1, o2, o3):
        for o, val in zip((o0, o1, o2, o3), _s5_disc(ar[...], ai[...], l[...], br[...], bi[...])):
            o[...] = val

    sds = jax.ShapeDtypeStruct(a_re.shape, F32)
    return pl.pallas_call(body, name=name, out_shape=[sds] * 4)(a_re, a_im, ls, b_re, b_im)


def s5_disc_bwd(a_re, a_im, ls, b_re, b_im, cts, *, name="s5_disc_bwd"):
    def body(ar, ai, l, br, bi, c0, c1, c2, c3, dar, dai, dl, dbr, dbi):
        _, vjp = jax.vjp(_s5_disc, ar[...], ai[...], l[...], br[...], bi[...])
        g = vjp((c0[...], c1[...], c2[...], c3[...]))
        for o, val in zip((dar, dai, dl), g[:3]):
            o[...] = jnp.sum(val, axis=-1, keepdims=True)
        dbr[...] = g[3]
        dbi[...] = g[4]

    col = jax.ShapeDtypeStruct((a_re.shape[0], 1), F32)
    sds = jax.ShapeDtypeStruct(a_re.shape, F32)
    return pl.pallas_call(body, name=name, out_shape=[col, col, col, sds, sds])(a_re, a_im, ls, b_re, b_im, *cts)


def _gelu(y):
    return 0.5 * y * (1.0 + jnp.tanh(0.7978845608028654 * (y + 0.044715 * (y * y * y))))


def _s5_powers(lam, conj):
    lr, li = lam[:, :S5_N], lam[:, S5_N:]
    if conj:
        li = -li
    rows, pr, pi = [], lr, li
    for _ in range(SUB):
        rows.append((pr, pi))
        pr, pi = pr * lr - pi * li, pr * li + pi * lr
    return rows


def _s5_table(rows, reverse):
    ridx = lax.broadcasted_iota(jnp.int32, (SUB, S5_N), 0)
    tr = jnp.zeros((SUB, S5_N), F32)
    ti = jnp.zeros((SUB, S5_N), F32)
    for i in range(SUB):
        pr, pi = rows[SUB - 1 - i] if reverse else rows[i]
        tr = jnp.where(ridx == i, pr, tr)
        ti = jnp.where(ridx == i, pi, ti)
    return tr, ti


def _s5_scan(buf, lam, carry_ref, reverse):
    tc = buf.shape[0]
    nblk = tc // SUB
    rows = _s5_powers(lam, reverse)
    tr, ti = _s5_table(rows, reverse)
    ridx = lax.broadcasted_iota(jnp.int32, (SUB, S5_N), 0)

    def block(j, carry):
        jj = (nblk - 1 - j) if reverse else j
        at = pl.ds(pl.multiple_of(jj * SUB, SUB), SUB)
        re, im = buf[at, :S5_N], buf[at, S5_N:]
        for sft, (pr, pi) in ((1, rows[0]), (2, rows[1]), (4, rows[3])):
            if reverse:
                keep = ridx < SUB - sft
                sre, sim = pltpu.roll(re, SUB - sft, 0), pltpu.roll(im, SUB - sft, 0)
            else:
                keep = ridx >= sft
                sre, sim = pltpu.roll(re, sft, 0), pltpu.roll(im, sft, 0)
            sre, sim = jnp.where(keep, sre, 0.0), jnp.where(keep, sim, 0.0)
            re, im = re + pr * sre - pi * sim, im + pr * sim + pi * sre
        cr, ci = carry
        re, im = re + tr * cr - ti * ci, im + tr * ci + ti * cr
        buf[at, :S5_N] = re
        buf[at, S5_N:] = im
        edge = 0 if reverse else SUB - 1
        return re[edge : edge + 1, :], im[edge : edge + 1, :]

    c0 = (carry_ref[0:1, :S5_N], carry_ref[0:1, S5_N:])
    cr, ci = lax.fori_loop(0, nblk, block, c0)
    carry_ref[:, :S5_N] = jnp.broadcast_to(cr, (SUB, S5_N))
    carry_ref[:, S5_N:] = jnp.broadcast_to(ci, (SUB, S5_N))


def s5_fwd(proj, ucol, bd, cd, lam, dskip, wglu, bglu, *, name="s5_fwd"):
    s = proj.shape[0]
    tc = _pick(s, (S5_TC, 128, 64, 8))
    assert ucol % MIX_W == 0

    def body(u_ref, bd_ref, cd_ref, lam_ref, d_ref, w_ref, b_ref, y_ref, xs_ref, carry):
        @pl.when(pl.program_id(0) == 0)
        def _():
            carry[...] = jnp.zeros_like(carry)

        u = u_ref[...]
        xs_ref[...] = _dot(u, bd_ref[...])
        _s5_scan(xs_ref, lam_ref[...], carry, False)
        g = _gelu(_dot(xs_ref[...], cd_ref[...]) + d_ref[...] * u)
        y_ref[...] = g * _sigmoid(_dot(g, w_ref[...]) + b_ref[...])

    full = lambda shp: pl.BlockSpec(shp, lambda i: (0, 0))
    return pl.pallas_call(
        body,
        name=name,
        grid=(s // tc,),
        in_specs=[pl.BlockSpec((tc, MIX_W), lambda i: (i, ucol // MIX_W)), full((MIX_W, 2 * S5_N)), full((2 * S5_N, MIX_W)),
                  full((1, 2 * S5_N)), full((1, MIX_W)), full((MIX_W, MIX_W)), full((1, MIX_W))],
        out_specs=[pl.BlockSpec((tc, MIX_W), lambda i: (i, 0)), pl.BlockSpec((tc, 2 * S5_N), lambda i: (i, 0))],
        out_shape=[jax.ShapeDtypeStruct((s, MIX_W), F32), jax.ShapeDtypeStruct((s, 2 * S5_N), F32)],
        scratch_shapes=[pltpu.VMEM((SUB, 2 * S5_N), F32)],
        compiler_params=_cparams(("arbitrary",)),
    )(proj, bd, cd, lam, dskip, wglu, bglu)


def s5_bwd(proj, ucol, xs, bd, cd, lam, dskip, wglu, bglu, dya, *, name="s5_bwd"):
    s = proj.shape[0]
    tc = _pick(s, (S5_TC, 128, 64, 8))
    nch = s // tc
    per = tc // SUB

    def body(u_ref, xs_ref, xp_ref, bd_ref, cd_ref, lam_ref, d_ref, w_ref, b_ref, dya_ref,
             du_ref, adj_ref, g_ref, dpre_ref, dy_ref, dlam_ref, dd_ref, db_ref, buf, carry, lacc):
        i = pl.program_id(0)
        first = i == 0

        @pl.when(first)
        def _():
            carry[...] = jnp.zeros_like(carry)
            lacc[...] = jnp.zeros_like(lacc)

        u, x, dya_ = u_ref[...], xs_ref[...], dya_ref[...]
        y = _dot(x, cd_ref[...]) + d_ref[...] * u
        g, gelu_vjp = jax.vjp(_gelu, y)
        sg = _sigmoid(_dot(g, w_ref[...]) + b_ref[...])
        dpre = dya_ * g * sg * (1.0 - sg)
        (dy,) = gelu_vjp(dya_ * sg + _dot(dpre, w_ref[...], 1, 1))
        g_ref[...] = g.astype(BF16)
        dpre_ref[...] = dpre.astype(BF16)
        dy_ref[...] = dy.astype(BF16)
        _accum(db_ref, jnp.sum(dpre, axis=0, keepdims=True), first)
        _accum(dd_ref, jnp.sum(dy * u, axis=0, keepdims=True), first)
        buf[...] = _dot(dy, cd_ref[...], 1, 1)
        _s5_scan(buf, lam_ref[...], carry, True)
        a = buf[...]
        adj_ref[...] = a.astype(BF16)
        du_ref[...] = (dy * d_ref[...] + _dot(a, bd_ref[...], 1, 1)).astype(BF16)
        before = jnp.where(i == nch - 1, 0.0, xp_ref[SUB - 1 : SUB, :])
        ridx = lax.broadcasted_iota(jnp.int32, (tc, 2 * S5_N), 0)
        xprev = jnp.where(ridx == 0, before, pltpu.roll(x, 1, 0))
        ar, ai, xr, xi = a[:, :S5_N], a[:, S5_N:], xprev[:, :S5_N], xprev[:, S5_N:]
        lacc[:, :S5_N] += jnp.sum((ar * xr + ai * xi).reshape(per, SUB, S5_N), axis=0)
        lacc[:, S5_N:] += jnp.sum((ai * xr - ar * xi).reshape(per, SUB, S5_N), axis=0)

        @pl.when(i == nch - 1)
        def _():
            dlam_ref[...] = jnp.sum(lacc[...], axis=0, keepdims=True)

    rev = lambda i: nch - 1 - i
    full = lambda shp: pl.BlockSpec(shp, lambda i: (0, 0))
    rows = lambda w: pl.BlockSpec((tc, w), lambda i: (rev(i), 0))
    hb = jax.ShapeDtypeStruct((s, MIX_W), BF16)
    return pl.pallas_call(
        body,
        name=name,
        grid=(nch,),
        in_specs=[pl.BlockSpec((tc, MIX_W), lambda i: (rev(i), ucol // MIX_W)), rows(2 * S5_N),
                  pl.BlockSpec((SUB, 2 * S5_N), lambda i: (jnp.maximum(rev(i) * per - 1, 0), 0)),
                  full((MIX_W, 2 * S5_N)), full((2 * S5_N, MIX_W)), full((1, 2 * S5_N)), full((1, MIX_W)),
                  full((MIX_W, MIX_W)), full((1, MIX_W)), rows(MIX_W)],
        out_specs=[rows(MIX_W), rows(2 * S5_N), rows(MIX_W), rows(MIX_W), rows(MIX_W),
                   full((1, 2 * S5_N)), full((1, MIX_W)), full((1, MIX_W))],
        out_shape=[hb, jax.ShapeDtypeStruct((s, 2 * S5_N), BF16), hb, hb, hb,
                   jax.ShapeDtypeStruct((1, 2 * S5_N), F32), jax.ShapeDtypeStruct((1, MIX_W), F32),
                   jax.ShapeDtypeStruct((1, MIX_W), F32)],
        scratch_shapes=[pltpu.VMEM((tc, 2 * S5_N), F32), pltpu.VMEM((SUB, 2 * S5_N), F32), pltpu.VMEM((SUB, 2 * S5_N), F32)],
        compiler_params=_cparams(("arbitrary",)),
    )(proj, xs, xs, bd, cd, lam, dskip, wglu, bglu, dya)


def _bcast16(a):
    return jnp.broadcast_to(a.reshape(S5_N, 1), (S5_N, S5_C))


def _blockdiag(blocks):
    g, r, c = blocks.shape
    eye = jnp.eye(g, dtype=blocks.dtype)
    return (eye[:, None, :, None] * blocks[:, :, None, :]).reshape(g * r, g * c)


def _blockdiag_extract(dense, r, c):
    g = dense.shape[0] // r
    return jnp.einsum("grgc->grc", dense.reshape(g, r, g, c))


def s5_prepare(a_re, a_im, log_step, b_re, b_im, c_re, c_im):
    disc_in = (_bcast16(a_re), _bcast16(a_im), _bcast16(jnp.broadcast_to(log_step[:, None], (S5_G, S5_P))),
               b_re.reshape(S5_N, S5_C), b_im.reshape(S5_N, S5_C))
    lr, li, bbr, bbi = s5_disc_fwd(*disc_in)
    lam = jnp.concatenate([lr[:, 0], li[:, 0]]).reshape(1, 2 * S5_N)
    to_blocks = lambda t: _blockdiag(t.reshape(S5_G, S5_P, S5_C).transpose(0, 2, 1))
    bd = jnp.concatenate([to_blocks(bbr), to_blocks(bbi)], axis=1).astype(BF16)
    cd = jnp.concatenate([_blockdiag(c_re.transpose(0, 2, 1)), -_blockdiag(c_im.transpose(0, 2, 1))], axis=0).astype(BF16)
    return disc_in, lam, bd, cd


def s5_param_grads(disc_in, dlam, dbd, dcd):
    first_col = lambda v: jnp.pad(v.reshape(S5_N, 1), ((0, 0), (0, S5_C - 1)))
    from_blocks = lambda t: _blockdiag_extract(t, S5_C, S5_P).transpose(0, 2, 1).reshape(S5_N, S5_C)
    cts = (first_col(dlam[0, :S5_N]), first_col(dlam[0, S5_N:]), from_blocks(dbd[:, :S5_N]), from_blocks(dbd[:, S5_N:]))
    dar, dai, dls, dbr, dbi = s5_disc_bwd(*disc_in, cts)
    dcr = _blockdiag_extract(dcd[:S5_N], S5_P, S5_C).transpose(0, 2, 1)
    dci = -_blockdiag_extract(dcd[S5_N:], S5_P, S5_C).transpose(0, 2, 1)
    return (dar.reshape(S5_G, S5_P), dai.reshape(S5_G, S5_P), dls.reshape(S5_G, S5_P).sum(axis=1),
            dbr.reshape(S5_G, S5_P, S5_C), dbi.reshape(S5_G, S5_P, S5_C), dcr, dci)


N_DEV = 8
MESH_ID = pl.DeviceIdType.MESH
ANY = pl.BlockSpec(memory_space=pl.ANY)


def _me():
    return lax.axis_index("x"), lax.axis_index("y"), lax.axis_index("c")


def all_gather(blocks, *, by_core=False, name="all_gather"):
    na = len(blocks)
    shapes = [b.shape[1:] if by_core else b.shape for b in blocks]

    def body(*refs):
        ins, outs = refs[:na], refs[na : 2 * na]
        send_sems, recv_sems = refs[2 * na :]
        x, y, c = _me()
        me, sibling = (x, y, c), (x, y, 1 - c)
        chips = [(1 - x, y), (x, 1 - y), (1 - x, 1 - y)]
        slot = lambda p: 4 * p[0] + 2 * p[1] + p[2]

        def copy(a, k, block, to, src=None):
            dst = outs[a].at[slot(block)]
            return pltpu.make_async_remote_copy(src_ref=dst if src is None else src, dst_ref=dst, send_sem=send_sems.at[a * 7 + k],
                                                recv_sem=recv_sems.at[a * 7 + k], device_id=to, device_id_type=MESH_ID)

        mine = [ins[a].at[c] if by_core else ins[a] for a in range(na)]
        sends = []
        for a in range(na):
            first = [copy(a, 0, me, sibling, src=mine[a])]
            first += [copy(a, 1 + j, me, (*chip, c), src=mine[a]) for j, chip in enumerate(chips)]
            for cp in first:
                cp.start()
            sends += first
        for a in range(na):
            for j, chip in enumerate(chips):
                copy(a, 1 + j, (*chip, c), me).wait_recv()
                fwd = copy(a, 4 + j, (*chip, c), sibling)
                fwd.start()
                sends.append(fwd)
        for a in range(na):
            copy(a, 0, sibling, me).wait_recv()
            for j, chip in enumerate(chips):
                copy(a, 4 + j, (*chip, 1 - c), me).wait_recv()
        for cp in sends:
            cp.wait_send()

    outs = pl.pallas_call(
        body,
        name=name,
        in_specs=[ANY] * na,
        out_specs=[ANY] * na,
        out_shape=[jax.ShapeDtypeStruct((N_DEV, *shp), b.dtype) for shp, b in zip(shapes, blocks)],
        scratch_shapes=[pltpu.SemaphoreType.DMA((na * 7,)), pltpu.SemaphoreType.DMA((na * 7,))],
    )(*blocks)
    x, y, c = _me()
    own = [lax.dynamic_index_in_dim(b, c, 0, keepdims=False) if by_core else b for b in blocks]
    return [lax.dynamic_update_index_in_dim(o, b, 4 * x + 2 * y + c, 0) for o, b in zip(outs, own)]


D2D_PIECES = 8


def _slabs(rows):
    n = D2D_PIECES if rows % (16 * D2D_PIECES) == 0 else 1
    return [pl.ds(i * (rows // n), rows // n) for i in range(n)]


def sibling_swap(arrs, *, name="sibling_swap"):
    na = len(arrs)
    pieces = [[(j, sl) for j in range(a.shape[0]) for sl in (_slabs(a.shape[1]) if a.shape[0] == 1 else _halves(a.shape[1]))]
              for a in arrs]
    base = np.cumsum([0] + [len(p) for p in pieces])

    def body(*refs):
        ins, outs = refs[:na], refs[na : 2 * na]
        send_sems, recv_sems = refs[2 * na :]
        x, y, c = _me()
        sends, recvs = [], []
        for a in range(na):
            for i, (j, sl) in enumerate(pieces[a]):
                k = int(base[a]) + i
                cp = pltpu.make_async_remote_copy(src_ref=ins[a].at[j, sl], dst_ref=outs[a].at[j, sl], send_sem=send_sems.at[k],
                                                  recv_sem=recv_sems.at[k], device_id=(x, y, 1 - c), device_id_type=MESH_ID)
                cp.start()
                sends.append(cp)
        for cp in sends:
            cp.wait_recv()
        for cp in sends:
            cp.wait_send()

    nsem = int(base[-1])
    return pl.pallas_call(
        body,
        name=name,
        in_specs=[ANY] * na,
        out_specs=[ANY] * na,
        out_shape=[jax.ShapeDtypeStruct(a.shape, a.dtype) for a in arrs],
        scratch_shapes=[pltpu.SemaphoreType.DMA((nsem,)), pltpu.SemaphoreType.DMA((nsem,))],
    )(*arrs)


def _halves(rows):
    return [pl.ds(0, rows // 2), pl.ds(rows // 2, rows // 2)] if rows % 32 == 0 else [pl.ds(0, rows)]


def chip_exchange(arrs, *, name="chip_exchange"):
    na = len(arrs)

    def body(*refs):
        ins, outs = refs[:na], refs[na : 2 * na]
        send_sems, recv_sems = refs[2 * na :]
        x, y, c = _me()
        my = 2 * x + y
        peers = []
        for k in range(1, N_CHIPS):
            px, py = (x + ((k >> 1) & 1)) % 2, (y + (k & 1)) % 2
            peers.append(((px, py, c), 2 * px + py))
        sends = []
        for a in range(na):
            for k, (peer, pidx) in enumerate(peers):
                cp = pltpu.make_async_remote_copy(src_ref=ins[a].at[pidx], dst_ref=outs[a].at[my], send_sem=send_sems.at[a * 3 + k],
                                                  recv_sem=recv_sems.at[a * 3 + k], device_id=peer, device_id_type=MESH_ID)
                cp.start()
                sends.append(cp)
        for a in range(na):
            for k, (peer, pidx) in enumerate(peers):
                pltpu.make_async_remote_copy(src_ref=ins[a].at[my], dst_ref=outs[a].at[pidx], send_sem=send_sems.at[a * 3 + k],
                                             recv_sem=recv_sems.at[a * 3 + k], device_id=peer, device_id_type=MESH_ID).wait_recv()
        for cp in sends:
            cp.wait_send()

    outs = pl.pallas_call(
        body,
        name=name,
        in_specs=[ANY] * na,
        out_specs=[ANY] * na,
        out_shape=[jax.ShapeDtypeStruct(a.shape, a.dtype) for a in arrs],
        scratch_shapes=[pltpu.SemaphoreType.DMA((na * 3,)), pltpu.SemaphoreType.DMA((na * 3,))],
    )(*arrs)
    chip = 2 * lax.axis_index("x") + lax.axis_index("y")
    return [lax.dynamic_update_index_in_dim(o, lax.dynamic_index_in_dim(a, chip, 0, keepdims=False), chip, 0)
            for o, a in zip(outs, arrs)]


def add_pair(a, b, *, name="add_pair"):
    n, r, c = a.shape
    tm = _row_tile(r, n * c * 2, 2 << 20)

    def body(a_ref, b_ref, o_ref):
        o_ref[...] = (a_ref[...].astype(F32) + b_ref[...].astype(F32)).astype(o_ref.dtype)

    blk = pl.BlockSpec((n, tm, c), lambda i: (0, i, 0))
    return pl.pallas_call(
        body, name=name, grid=(r // tm,), in_specs=[blk, blk], out_specs=blk, out_shape=jax.ShapeDtypeStruct(a.shape, a.dtype),
        compiler_params=_cparams(("arbitrary",)),
    )(a, b)


def _row_tile(rows, row_bytes, budget):
    best = None
    for t in range(8, rows + 1, 8):
        if rows % t == 0 and t * row_bytes <= budget:
            best = t
    return best or rows


def sum_blocks(a, *, name="sum_blocks"):
    n, r, c = a.shape
    tm = _row_tile(r, n * c * a.dtype.itemsize, 4 << 20)

    def body(a_ref, o_ref):
        acc = a_ref[0].astype(F32)
        for k in range(1, n):
            acc = acc + a_ref[k].astype(F32)
        o_ref[...] = acc

    return pl.pallas_call(
        body,
        name=name,
        grid=(r // tm,),
        in_specs=[pl.BlockSpec((n, tm, c), lambda i: (0, i, 0))],
        out_specs=pl.BlockSpec((tm, c), lambda i: (i, 0)),
        out_shape=jax.ShapeDtypeStruct((r, c), F32),
        compiler_params=_cparams(("arbitrary",)),
    )(a)


ADAM_LR = 0.001
ADAM_B1 = 0.9
ADAM_B2 = 0.999
ADAM_EPS = 1e-08
ADAM_WD = 0.01
ADAM_STEP = 10


def adamw(w, g, m, v, *, name="adamw"):
    r, c = w.shape
    tm = _row_tile(r, c * 4, 1 << 20)

    def body(w_ref, g_ref, m_ref, v_ref, d_ref, nm_ref, nv_ref):
        g_ = g_ref[...]
        m_ = ADAM_B1 * m_ref[...] + (1.0 - ADAM_B1) * g_
        v_ = ADAM_B2 * v_ref[...] + (1.0 - ADAM_B2) * (g_ * g_)
        m_hat = m_ / (1.0 - ADAM_B1**ADAM_STEP)
        v_hat = v_ / (1.0 - ADAM_B2**ADAM_STEP)
        d_ref[...] = -ADAM_LR * (m_hat / (jnp.sqrt(v_hat) + ADAM_EPS) + ADAM_WD * w_ref[...])
        nm_ref[...] = m_
        nv_ref[...] = v_

    blk = pl.BlockSpec((tm, c), lambda i: (i, 0))
    sds = jax.ShapeDtypeStruct((r, c), F32)
    return pl.pallas_call(
        body, name=name, grid=(r // tm,), in_specs=[blk] * 4, out_specs=[blk] * 3, out_shape=[sds] * 3,
        compiler_params=_cparams(("arbitrary",)),
    )(w, g, m, v)


WEIGHTS = ["g_mix", "w_in", "s5_a_re", "s5_a_im", "s5_log_step", "s5_b_re", "s5_b_im", "s5_c_re", "s5_c_im", "s5_d", "w_glu",
           "b_glu", "w_gla_gate", "b_gla_gate", "g_gla_out", "w_branch", "w_out", "g_mem", "g_cross", "w_xq", "w_xkv", "w_xo",
           "g_mlp", "w_up", "w_down", "g_final"]
SHARDED = {"w_in": 1, "w_glu": 0, "w_branch": 2, "w_out": 0, "w_xq": 0, "w_xkv": 1, "w_xo": 0, "w_up": 1, "w_down": 0}
DEPTH = 2
N_CHIPS = 4
D_IN = 9744
C_U, C_QG, C_KG, C_VG, C_LR, C_RG, C_QA, C_KA, C_VA, C_GATE = 0, 512, 768, 1024, 1536, 1552, 2064, 3600, 5136, 6672
A_GATE, A_U, A_QG, A_KG, A_VG, A_RG, A_Q0, A_LR, A_W = 0, 3072, 3584, 3840, 4096, 4608, 5120, 6656, 7168
GLA_COLS = (A_QG, A_KG, A_VG, A_LR, A_RG)


def _split_w_in(w):
    att = lambda g: [w[:, c + MIX_W * g : c + MIX_W * (g + 1)] for c in (C_QA, C_KA, C_VA)]
    a = jnp.concatenate([w[:, C_GATE:D_IN], w[:, C_U:C_QG], w[:, C_QG:C_KG], w[:, C_KG:C_VG], w[:, C_VG:C_LR], w[:, C_RG:C_QA],
                         *att(0), w[:, C_LR:C_RG], jnp.zeros((w.shape[0], A_W - A_LR - 16), w.dtype)], axis=1)
    return a, jnp.concatenate(att(1), axis=1), jnp.concatenate(att(2), axis=1)


def _join_w_in(a, b, c):
    att = lambda k: [a[:, A_Q0 + MIX_W * k : A_Q0 + MIX_W * (k + 1)], b[:, MIX_W * k : MIX_W * (k + 1)], c[:, MIX_W * k : MIX_W * (k + 1)]]
    return jnp.concatenate([a[:, A_U:A_QG], a[:, A_QG:A_KG], a[:, A_KG:A_VG], a[:, A_VG:A_RG], a[:, A_LR : A_LR + 16], a[:, A_RG:A_Q0],
                            *att(0), *att(1), *att(2), a[:, A_GATE:A_U]], axis=1)


def _by_residue(a, d):
    s = a.shape[0]
    return a if d == 1 else a.reshape(s // d, d, -1).transpose(1, 0, 2).reshape(s, -1)


def _in_order(a, d):
    s = a.shape[0]
    return a if d == 1 else a.reshape(d, s // d, -1).transpose(1, 0, 2).reshape(s, -1)


def _pack(arrs):
    flat = jnp.concatenate([a.reshape(-1) for a in arrs])
    n = flat.shape[0]
    rows = -(-n // (256 * 128)) * 256
    return jnp.pad(flat, (0, rows * 128 - n)).reshape(rows, 128)


def _unpack(packed, like):
    flat, out, o = packed.reshape(-1), [], 0
    for a in like:
        n = math.prod(a.shape)
        out.append(flat[o : o + n].reshape(a.shape))
        o += n
    return out


def _layer_fwd(x0, p, kv):
    h = rmsnorm(x0, p["g_mix"], name="mix_norm")
    hs = [h, _by_residue(h, 4), _by_residue(h, 16)]
    proj = [matmul(hs[g], p["w_in_seg"][g], name="in_proj") for g in range(3)]
    ya, xs = s5_fwd(proj[0], A_U, p["bd"], p["cd"], p["lam"], p["s5_d"], p["w_glu"], p["b_glu"])
    yb, gst = gla_fwd(proj[0], GLA_COLS, p["w_gate"], p["b_gla_gate"], p["g_gla_out"])
    att = [attn_fwd(proj[g], A_Q0 if g == 0 else 0, g) for g in range(3)]
    outs = [_in_order(att[g][0], ATT_GROUPS[g][1]) for g in range(3)]
    lses = [_in_order(att[g][1], ATT_GROUPS[g][1]) for g in range(3)]
    x1, merged, yc = merge_fwd(x0, proj[0], ya, yb, outs, lses, p["w_branch"], p["w_out"])
    x2 = cross_fwd(x1, p["g_cross"], p["w_xq"], p["w_xo"], kv)
    x3 = mlp_fwd(x2, p["g_mlp"], p["w_up"], p["w_down"])
    saved = dict(x0=x0, x1=x1, x2=x2, hs=hs, proj=proj, ya=ya, xs=xs, yb=yb, gst=gst, att=att, outs=outs, lses=lses,
                 merged=merged, yc=yc)
    return x3, saved


def _layer_bwd(dx3, p, kv, memn, sv):
    g = {}
    tn = lambda a, b, **kw: matmul(a, b, ta=True, out_dtype=BF16, name="wgrad", **kw)
    dx2, g["g_mlp"], h3, dup, act = mlp_bwd(sv["x2"], p["g_mlp"], p["w_up"], p["w_down"], dx3)
    g["w_up"], g["w_down"] = tn(h3, dup), tn(act, dx3)
    dx1, g["g_cross"], h2, dq, o, dkv = cross_bwd(sv["x1"], p["g_cross"], p["w_xq"], p["w_xo"], kv, dx2)
    g["w_xq"], g["w_xo"], g["w_xkv"] = tn(h2, dq), tn(o, dx2), tn(memn, dkv)
    dmemn = matmul(dkv, p["w_xkv"], tb=True, name="dmem")
    proj = sv["proj"]
    r = merge_bwd(dx1, proj[0], sv["ya"], sv["yb"], sv["yc"], sv["outs"], sv["lses"], p["w_branch"], p["w_out"])
    dgl, dz, dya, dyb, douts, dlses = r[0], r[1:4], r[4], r[5], r[6:9], r[9:12]
    g["w_branch"] = jnp.stack([tn(y, dz[n]) for n, y in enumerate((sv["ya"], sv["yb"], sv["yc"]))])
    g["w_out"] = tn(sv["merged"], dx1)
    datt = []
    for k in range(3):
        dil = ATT_GROUPS[k][1]
        datt.append(attn_bwd(proj[k], A_Q0 if k == 0 else 0, k, sv["att"][k][0], sv["att"][k][1],
                             _by_residue(douts[k], dil), _by_residue(dlses[k], dil)))
    du, adj, gg, dpre, dy, dlam, g["s5_d"], g["b_glu"] = s5_bwd(proj[0], A_U, sv["xs"], p["bd"], p["cd"], p["lam"], p["s5_d"],
                                                                 p["w_glu"], p["b_glu"], dya)
    dbd = matmul(proj[0], adj, ta=True, a_col=(A_U, MIX_W), name="s5_dbd")
    dcd = matmul(sv["xs"], dy, ta=True, name="s5_dcd")
    g["w_glu"] = tn(gg, dpre)
    (g["s5_a_re"], g["s5_a_im"], g["s5_log_step"], g["s5_b_re"], g["s5_b_im"], g["s5_c_re"],
     g["s5_c_im"]) = s5_param_grads(p["disc_in"], dlam, dbd, dcd)
    dqg, dkg, dvg, dlr, drg, dwg, g["b_gla_gate"], g["g_gla_out"] = gla_bwd(proj[0], GLA_COLS, p["w_gate"], p["b_gla_gate"],
                                                                             p["g_gla_out"], sv["gst"], dyb)
    g["w_gla_gate"] = dwg[:GLA_GATE_RANK]
    s = dx3.shape[0]
    dproj = [jnp.concatenate([dgl, du, dqg, dkg, dvg, drg, *datt[0], dlr, jnp.zeros((s, A_W - A_LR - LR_PAD), BF16)], axis=1),
             jnp.concatenate(datt[1], axis=1), jnp.concatenate(datt[2], axis=1)]
    g["w_in"] = _join_w_in(*[tn(sv["hs"][k], dproj[k]) for k in range(3)])
    dhs = [_in_order(matmul(dproj[k], p["w_in_seg"][k], tb=True, name="in_proj_bwd"), ATT_GROUPS[k][1]) for k in range(3)]
    dx0, g["g_mix"] = rmsnorm_bwd(sv["x0"], p["g_mix"], dhs, dx1, name="mix_norm_bwd")
    return dx0, g, dmemn


GLA_GATE_RANK = 16


def kernel(x, mem, *rest):
    nw = len(WEIGHTS)
    w = dict(zip(WEIGHTS, rest[:nw]))
    target = rest[nw]
    m = dict(zip(WEIGHTS, rest[nw + 1 : 2 * nw + 1]))
    v = dict(zip(WEIGHTS, rest[2 * nw + 1 : 3 * nw + 1]))
    x0, memx, target = x[0], mem[0], target[0]
    chip = 2 * lax.axis_index("x") + lax.axis_index("y")

    big = list(SHARDED)
    gate_pad = jnp.pad(w["w_gla_gate"], ((0, 0), (0, 0), (0, LR_PAD - w["w_gla_gate"].shape[2])))
    gathered = all_gather([w[n].astype(BF16) for n in big] + [gate_pad], by_core=True, name="gather_weights")
    gathered = [t.reshape(N_CHIPS, DEPTH, *t.shape[1:]) for t in gathered]
    full = {n: [jnp.concatenate([t[j, l] for j in range(N_CHIPS)], axis=SHARDED[n]) for l in range(DEPTH)]
            for n, t in zip(big, gathered)}
    gate_full = [jnp.concatenate([gathered[-1][j, l][:, : w["w_gla_gate"].shape[2]] for j in range(N_CHIPS)], axis=1)
                 for l in range(DEPTH)]

    memn = rmsnorm(memx, w["g_mem"], name="mem_norm")
    params, kvs = [], []
    for l in range(DEPTH):
        p = {n: full[n][l] for n in big if n != "w_in"}
        p["w_in_seg"] = _split_w_in(full["w_in"][l])
        p["w_gate"] = jnp.pad(gate_full[l], ((0, LR_PAD - GLA_GATE_RANK), (0, 0))).astype(BF16)
        for n in ("g_mix", "g_cross", "g_mlp"):
            p[n] = w[n][l]
        for n in ("s5_d", "b_glu", "b_gla_gate", "g_gla_out"):
            p[n] = w[n][l].reshape(1, -1)
        p["disc_in"], p["lam"], p["bd"], p["cd"] = s5_prepare(*[w[n][l] for n in WEIGHTS[2:9]])
        params.append(p)
        kvs.append(matmul(memn, p["w_xkv"], out_dtype=BF16, name="mem_kv"))

    xl, saved = x0, []
    for l in range(DEPTH):
        xl, sv = _layer_fwd(xl, params[l], kvs[l])
        saved.append(sv)
    loss8, dx, dg_final = loss_head(xl, w["g_final"], target)
    grads, dmem = [None] * DEPTH, []
    for l in reversed(range(DEPTH)):
        dx, grads[l], dm = _layer_bwd(dx, params[l], kvs[l], memn, saved[l])
        dmem.append(dm)
    _, dg_mem = rmsnorm_bwd(memx, w["g_mem"], dmem, jnp.zeros_like(memx), name="mem_norm_bwd")

    core = lax.axis_index("c")

    def halves(n):
        ax = SHARDED[n]
        keep, send = [], []
        for j in range(N_CHIPS):
            p0, p1 = [jnp.split(grads[l][n], N_CHIPS, axis=ax)[j] for l in range(DEPTH)]
            keep.append(jnp.where(core == 0, p0, p1))
            send.append(jnp.where(core == 0, p1, p0))
        flat = lambda ps: jnp.stack(ps).reshape(N_CHIPS, -1, ps[0].shape[-1])
        return flat(keep), flat(send)

    kept, sent = zip(*[halves(n) for n in big])
    theirs = sibling_swap(list(sent), name="swap_layers")
    chip_sums = [add_pair(a, b, name="add_cores") for a, b in zip(kept, theirs)]
    landed = chip_exchange(chip_sums, name="exchange_grads")
    reduced = [sum_blocks(t, name="sum_grads") for t in landed]
    other = sibling_swap([t[None] for t in reduced], name="pair_layers")
    out_g = {}
    for n, mine, theirs in zip(big, reduced, other):
        both = jnp.where(core == 0, jnp.stack([mine, theirs[0]]), jnp.stack([theirs[0], mine]))
        out_g[n] = both.reshape(w[n].shape)

    small = [n for n in WEIGHTS if n not in SHARDED]
    local_small = []
    for n in small:
        if n == "g_mem":
            local_small.append(dg_mem.reshape(w[n].shape))
        elif n == "g_final":
            local_small.append(dg_final.reshape(w[n].shape))
        elif n == "w_gla_gate":
            local_small.append(jnp.stack([grads[l][n] for l in range(DEPTH)]))
        else:
            local_small.append(jnp.stack([grads[l][n].reshape(w[n].shape[1:]) for l in range(DEPTH)]))
    packed = _pack(local_small + [loss8[0, :1]])
    (every,) = all_gather([packed], name="gather_small")
    summed = _unpack(sum_blocks(every, name="sum_small"), local_small + [loss8[0, :1]])
    loss = summed[-1].reshape(())
    for n, t in zip(small, summed[:-1]):
        if n == "w_gla_gate":
            t = lax.dynamic_slice_in_dim(t, chip * w[n].shape[2], w[n].shape[2], axis=2)
        out_g[n] = t

    delta, new_m, new_v = {}, {}, {}
    for n in big:
        c = w[n].shape[-1]
        r = [t.reshape(w[n].shape) for t in adamw(w[n].reshape(-1, c), out_g[n].reshape(-1, c), m[n].reshape(-1, c),
                                                  v[n].reshape(-1, c), name="adamw")]
        delta[n], new_m[n], new_v[n] = r
    like = [w[n] for n in small]
    r = adamw(_pack(like), _pack([out_g[n] for n in small]), _pack([m[n] for n in small]), _pack([v[n] for n in small]),
              name="adamw_small")
    for d, t in zip((delta, new_m, new_v), r):
        d.update(zip(small, _unpack(t, like)))
    return (loss, dx[None], *[out_g[n] for n in WEIGHTS], *[delta[n] for n in WEIGHTS], *[new_m[n] for n in WEIGHTS],
            *[new_v[n] for n in WEIGHTS])
```

```python
import functools
import math

import jax
import jax.numpy as jnp
import numpy as np
from jax import lax
from jax.experimental import pallas as pl
from jax.experimental.pallas import tpu as pltpu

F32 = jnp.float32
BF16 = jnp.bfloat16

VMEM_LIMIT_BYTES = 56 * 1024 * 1024
MATMUL_VMEM_BYTES = 36 * 1024 * 1024


def _cparams(sem):
    return pltpu.CompilerParams(dimension_semantics=sem, vmem_limit_bytes=VMEM_LIMIT_BYTES)


def _dot(a, b, ca=1, cb=0):
    return lax.dot_general(a.astype(BF16), b.astype(BF16), (((ca,), (cb,)), ((), ())), preferred_element_type=F32)


def _pick(n, prefs):
    for p in prefs:
        if n % p == 0:
            return p
    return n


def matmul(a, b, *, ta=False, tb=False, out_dtype=F32, name="mm", a_col=None, b_col=None):
    a_off, a_w = a_col if a_col is not None else (0, a.shape[1])
    b_off, b_w = b_col if b_col is not None else (0, b.shape[1])
    if ta:
        kk, m = a.shape[0], a_w
    else:
        m, kk = a.shape[0], a_w
    if tb:
        n, kb = b.shape[0], b_w
    else:
        kb, n = b.shape[0], b_w
    assert kk == kb, (a.shape, b.shape, ta, tb)
    tm = _pick(m, (512, 256, 128))
    tn = _pick(n, (1024, 512, 256, 128))
    sa, sb, so = a.dtype.itemsize, b.dtype.itemsize, jnp.dtype(out_dtype).itemsize
    fits = lambda t: 2 * (tm * t * sa + t * tn * sb) + tm * tn * (4 + 2 * so) <= MATMUL_VMEM_BYTES
    tk = max([t for t in range(128, kk + 1, 128) if kk % t == 0 and fits(t)] or [_pick(kk, (128,))])
    nk = kk // tk
    a_bytes, b_bytes = m * kk * sa, kk * n * sb
    n_outer = nk == 1 and b_bytes + a_bytes * (n // tn) < a_bytes + b_bytes * (m // tm)
    ij = (lambda g0, g1: (g1, g0)) if n_outer else (lambda g0, g1: (g0, g1))
    if ta:
        assert a_off % tm == 0
        a_spec = pl.BlockSpec((tk, tm), lambda g0, g1, k, o=a_off // tm: (k, ij(g0, g1)[0] + o))
    else:
        assert a_off % tk == 0
        a_spec = pl.BlockSpec((tm, tk), lambda g0, g1, k, o=a_off // tk: (ij(g0, g1)[0], k + o))
    if tb:
        assert b_off % tk == 0
        b_spec = pl.BlockSpec((tn, tk), lambda g0, g1, k, o=b_off // tk: (ij(g0, g1)[1], k + o))
    else:
        assert b_off % tn == 0
        b_spec = pl.BlockSpec((tk, tn), lambda g0, g1, k, o=b_off // tn: (k, ij(g0, g1)[1] + o))

    def body(a_ref, b_ref, o_ref, *acc):
        k = pl.program_id(2)
        p = _dot(a_ref[...], b_ref[...], 0 if ta else 1, 1 if tb else 0)
        if nk == 1:
            o_ref[...] = p.astype(o_ref.dtype)
            return
        (acc_ref,) = acc

        @pl.when(k == 0)
        def _():
            acc_ref[...] = p

        @pl.when(jnp.logical_and(k > 0, k < nk - 1))
        def _():
            acc_ref[...] += p

        @pl.when(k == nk - 1)
        def _():
            o_ref[...] = (acc_ref[...] + p).astype(o_ref.dtype)

    return pl.pallas_call(
        body,
        name=name,
        grid=(n // tn, m // tm, nk) if n_outer else (m // tm, n // tn, nk),
        in_specs=[a_spec, b_spec],
        out_specs=pl.BlockSpec((tm, tn), lambda g0, g1, k: ij(g0, g1)),
        out_shape=jax.ShapeDtypeStruct((m, n), out_dtype),
        scratch_shapes=[pltpu.VMEM((tm, tn), F32)] if nk > 1 else [],
        compiler_params=_cparams(("parallel", "parallel", "arbitrary")),
    )(a, b)


RMS_EPS = 1e-6
ROW_TILE = 512


def _rms_fwd(x, g):
    r = lax.rsqrt(jnp.mean(x * x, axis=-1, keepdims=True) + RMS_EPS)
    return x * r * g


def _rms_bwd(x, g, dh):
    r = lax.rsqrt(jnp.mean(x * x, axis=-1, keepdims=True) + RMS_EPS)
    xh = x * r
    dg = jnp.sum(dh * xh, axis=0, keepdims=True)
    dxh = dh * g
    dx = r * (dxh - xh * jnp.mean(dxh * xh, axis=-1, keepdims=True))
    return dx, dg


def _accum(ref, val, first):
    @pl.when(first)
    def _():
        ref[...] = val

    @pl.when(jnp.logical_not(first))
    def _():
        ref[...] += val


def rmsnorm(x, g, *, out_dtype=BF16, name="rmsnorm"):
    s, d = x.shape
    tm = _pick(s, (ROW_TILE, 256, 128, 8))

    def body(x_ref, g_ref, o_ref):
        o_ref[...] = _rms_fwd(x_ref[...], g_ref[...]).astype(o_ref.dtype)

    return pl.pallas_call(
        body,
        name=name,
        grid=(s // tm,),
        in_specs=[pl.BlockSpec((tm, d), lambda i: (i, 0)), pl.BlockSpec((1, d), lambda i: (0, 0))],
        out_specs=pl.BlockSpec((tm, d), lambda i: (i, 0)),
        out_shape=jax.ShapeDtypeStruct((s, d), out_dtype),
        compiler_params=_cparams(("arbitrary",)),
    )(x, g.reshape(1, d))


def rmsnorm_bwd(x, g, dhs, dres, *, name="rmsnorm_bwd"):
    s, d = x.shape
    tm = _pick(s, (ROW_TILE, 256, 128, 8))
    n = len(dhs)

    def body(x_ref, g_ref, *refs):
        dh_refs, dres_ref, dx_ref, dg_ref = refs[:n], refs[n], refs[n + 1], refs[n + 2]
        dh = dh_refs[0][...].astype(F32)
        for r in dh_refs[1:]:
            dh = dh + r[...].astype(F32)
        dx, dg = _rms_bwd(x_ref[...], g_ref[...], dh)
        dx_ref[...] = dres_ref[...] + dx
        _accum(dg_ref, dg, pl.program_id(0) == 0)

    row = pl.BlockSpec((tm, d), lambda i: (i, 0))
    vec = pl.BlockSpec((1, d), lambda i: (0, 0))
    return pl.pallas_call(
        body,
        name=name,
        grid=(s // tm,),
        in_specs=[row, vec] + [row] * n + [row],
        out_specs=[row, vec],
        out_shape=[jax.ShapeDtypeStruct((s, d), F32), jax.ShapeDtypeStruct((1, d), F32)],
        compiler_params=_cparams(("arbitrary",)),
    )(x, g.reshape(1, d), *dhs, dres)


def loss_head(x, g, target, *, name="loss_head"):
    s, d = x.shape
    tm = _pick(s, (ROW_TILE, 256, 128, 8))

    def body(x_ref, g_ref, t_ref, l_ref, dx_ref, dg_ref):
        x_, g_ = x_ref[...], g_ref[...]
        e = _rms_fwd(x_, g_) - t_ref[...]
        part = 0.5 * jnp.sum(jnp.sum(e * e, axis=-1, keepdims=True), axis=0, keepdims=True) / d
        dx, dg = _rms_bwd(x_, g_, e * (1.0 / d))
        dx_ref[...] = dx
        first = pl.program_id(0) == 0
        _accum(dg_ref, dg, first)
        _accum(l_ref, jnp.broadcast_to(part, (8, 128)), first)

    row = pl.BlockSpec((tm, d), lambda i: (i, 0))
    vec = pl.BlockSpec((1, d), lambda i: (0, 0))
    return pl.pallas_call(
        body,
        name=name,
        grid=(s // tm,),
        in_specs=[row, vec, row],
        out_specs=[pl.BlockSpec((8, 128), lambda i: (0, 0)), row, vec],
        out_shape=[jax.ShapeDtypeStruct((8, 128), F32), jax.ShapeDtypeStruct((s, d), F32), jax.ShapeDtypeStruct((1, d), F32)],
        compiler_params=_cparams(("arbitrary",)),
    )(x, g.reshape(1, d), target)


FF_TILE = 1024


def mlp_fwd(x, g, w_up, w_down, *, name="mlp_fwd"):
    s, d = x.shape
    ff = w_up.shape[1]
    tm, tf = _pick(s, (ROW_TILE, 256, 128)), _pick(ff, (FF_TILE, 512, 256, 128))
    nf = ff // tf

    def body(x_ref, g_ref, wu_ref, wd_ref, o_ref, h_ref, acc_ref):
        f = pl.program_id(1)

        @pl.when(f == 0)
        def _():
            h_ref[...] = _rms_fwd(x_ref[...], g_ref[...]).astype(BF16)
            acc_ref[...] = x_ref[...]

        up = jnp.maximum(_dot(h_ref[...], wu_ref[...]), 0.0)
        acc_ref[...] += _dot(up * up, wd_ref[...])

        @pl.when(f == nf - 1)
        def _():
            o_ref[...] = acc_ref[...]

    return pl.pallas_call(
        body,
        name=name,
        grid=(s // tm, nf),
        in_specs=[
            pl.BlockSpec((tm, d), lambda i, f: (i, 0)),
            pl.BlockSpec((1, d), lambda i, f: (0, 0)),
            pl.BlockSpec((d, tf), lambda i, f: (0, f)),
            pl.BlockSpec((tf, d), lambda i, f: (f, 0)),
        ],
        out_specs=pl.BlockSpec((tm, d), lambda i, f: (i, 0)),
        out_shape=jax.ShapeDtypeStruct((s, d), F32),
        scratch_shapes=[pltpu.VMEM((tm, d), BF16), pltpu.VMEM((tm, d), F32)],
        compiler_params=_cparams(("arbitrary", "arbitrary")),
    )(x, g.reshape(1, d), w_up, w_down)


def mlp_bwd(x, g, w_up, w_down, dy, *, name="mlp_bwd"):
    s, d = x.shape
    ff = w_up.shape[1]
    tm, tf = _pick(s, (ROW_TILE, 256, 128)), _pick(ff, (FF_TILE, 512, 256, 128))
    nf = ff // tf

    def body(x_ref, g_ref, wu_ref, wd_ref, dy_ref, dx_ref, dg_ref, h_ref, dup_ref, act_ref, dyb_ref, acc_ref):
        i, f = pl.program_id(0), pl.program_id(1)

        @pl.when(f == 0)
        def _():
            h_ref[...] = _rms_fwd(x_ref[...], g_ref[...]).astype(BF16)
            dyb_ref[...] = dy_ref[...].astype(BF16)
            acc_ref[...] = jnp.zeros_like(acc_ref)

        up = jnp.maximum(_dot(h_ref[...], wu_ref[...]), 0.0)
        dact = _dot(dyb_ref[...], wd_ref[...], 1, 1)
        dup = (2.0 * up * dact).astype(BF16)
        dup_ref[...] = dup
        act_ref[...] = (up * up).astype(BF16)
        acc_ref[...] += _dot(dup, wu_ref[...], 1, 1)

        @pl.when(f == nf - 1)
        def _():
            dx, dg = _rms_bwd(x_ref[...], g_ref[...], acc_ref[...])
            dx_ref[...] = dy_ref[...] + dx
            _accum(dg_ref, dg, i == 0)

    row = pl.BlockSpec((tm, d), lambda i, f: (i, 0))
    vec = pl.BlockSpec((1, d), lambda i, f: (0, 0))
    wide = pl.BlockSpec((tm, tf), lambda i, f: (i, f))
    return pl.pallas_call(
        body,
        name=name,
        grid=(s // tm, nf),
        in_specs=[row, vec, pl.BlockSpec((d, tf), lambda i, f: (0, f)), pl.BlockSpec((tf, d), lambda i, f: (f, 0)), row],
        out_specs=[row, vec, row, wide, wide],
        out_shape=[
            jax.ShapeDtypeStruct((s, d), F32),
            jax.ShapeDtypeStruct((1, d), F32),
            jax.ShapeDtypeStruct((s, d), BF16),
            jax.ShapeDtypeStruct((s, ff), BF16),
            jax.ShapeDtypeStruct((s, ff), BF16),
        ],
        scratch_shapes=[pltpu.VMEM((tm, d), BF16), pltpu.VMEM((tm, d), F32)],
        compiler_params=_cparams(("arbitrary", "arbitrary")),
    )(x, g.reshape(1, d), w_up, w_down, dy)


X_HEADS = 4


def _softmax_rows(s):
    m = jnp.max(s, axis=-1, keepdims=True)
    e = jnp.exp(s - m)
    return e / jnp.sum(e, axis=-1, keepdims=True)


def cross_fwd(x, g, wq, wo, kv, *, name="cross_fwd"):
    s, d = x.shape
    ml = kv.shape[0]
    dh = d // X_HEADS
    tm = _pick(s, (ROW_TILE, 256, 128))
    scale = dh**-0.5

    def body(x_ref, g_ref, wq_ref, wo_ref, kv_ref, o_ref):
        x_ = x_ref[...]
        q = _dot(_rms_fwd(x_, g_ref[...]), wq_ref[...])
        outs = []
        for hd in range(X_HEADS):
            kh = kv_ref[:, hd * dh : (hd + 1) * dh]
            vh = kv_ref[:, d + hd * dh : d + (hd + 1) * dh]
            p = _softmax_rows(_dot(q[:, hd * dh : (hd + 1) * dh], kh, 1, 1) * scale)
            outs.append(_dot(p, vh))
        o_ref[...] = x_ + _dot(jnp.concatenate(outs, axis=-1), wo_ref[...])

    row = pl.BlockSpec((tm, d), lambda i: (i, 0))
    full = lambda shp: pl.BlockSpec(shp, lambda i: (0, 0))
    return pl.pallas_call(
        body,
        name=name,
        grid=(s // tm,),
        in_specs=[row, full((1, d)), full((d, d)), full((d, d)), full((ml, 2 * d))],
        out_specs=row,
        out_shape=jax.ShapeDtypeStruct((s, d), F32),
        compiler_params=_cparams(("arbitrary",)),
    )(x, g.reshape(1, d), wq, wo, kv)


def cross_bwd(x, g, wq, wo, kv, dy, *, name="cross_bwd"):
    s, d = x.shape
    ml = kv.shape[0]
    dh = d // X_HEADS
    tm = _pick(s, (ROW_TILE, 256, 128))
    scale = dh**-0.5

    def body(x_ref, g_ref, wq_ref, wo_ref, kv_ref, dy_ref, dx_ref, dg_ref, h_ref, dq_ref, o_ref, dkv_ref):
        first = pl.program_id(0) == 0
        x_, g_ = x_ref[...], g_ref[...]
        h = _rms_fwd(x_, g_).astype(BF16)
        h_ref[...] = h
        q = _dot(h, wq_ref[...])
        dy_ = dy_ref[...]
        do = _dot(dy_, wo_ref[...], 1, 1)
        outs, dqs, dks, dvs = [], [], [], []
        for hd in range(X_HEADS):
            sl = slice(hd * dh, (hd + 1) * dh)
            kh = kv_ref[:, sl]
            vh = kv_ref[:, d + hd * dh : d + (hd + 1) * dh]
            qh = q[:, sl]
            p = _softmax_rows(_dot(qh, kh, 1, 1) * scale)
            outs.append(_dot(p, vh))
            doh = do[:, sl]
            dp = _dot(doh, vh, 1, 1)
            ds = p * (dp - jnp.sum(dp * p, axis=-1, keepdims=True)) * scale
            dqs.append(_dot(ds, kh))
            dks.append(_dot(ds, qh, 0, 0))
            dvs.append(_dot(p, doh, 0, 0))
        o_ref[...] = jnp.concatenate(outs, axis=-1).astype(BF16)
        dq = jnp.concatenate(dqs, axis=-1).astype(BF16)
        dq_ref[...] = dq
        dx, dg = _rms_bwd(x_, g_, _dot(dq, wq_ref[...], 1, 1))
        dx_ref[...] = dy_ + dx
        _accum(dkv_ref, jnp.concatenate(dks + dvs, axis=-1), first)
        _accum(dg_ref, dg, first)

    row = pl.BlockSpec((tm, d), lambda i: (i, 0))
    full = lambda shp: pl.BlockSpec(shp, lambda i: (0, 0))
    rowb = jax.ShapeDtypeStruct((s, d), BF16)
    return pl.pallas_call(
        body,
        name=name,
        grid=(s // tm,),
        in_specs=[row, full((1, d)), full((d, d)), full((d, d)), full((ml, 2 * d)), row],
        out_specs=[row, full((1, d)), row, row, row, full((ml, 2 * d))],
        out_shape=[jax.ShapeDtypeStruct((s, d), F32), jax.ShapeDtypeStruct((1, d), F32), rowb, rowb, rowb,
                   jax.ShapeDtypeStruct((ml, 2 * d), F32)],
        compiler_params=_cparams(("arbitrary",)),
    )(x, g.reshape(1, d), wq, wo, kv, dy)


N_BRANCH = 3
MIX_W = 512
ATT_DH = 128
ATT_HG = 4
MERGE_TILE = 256


def _group_weights(l0, l1, l2):
    m = jnp.maximum(jnp.maximum(l0, l1), l2)
    e = [jnp.exp(l0 - m), jnp.exp(l1 - m), jnp.exp(l2 - m)]
    inv = 1.0 / (e[0] + e[1] + e[2])
    return [t * inv for t in e]


def _sigmoid(x):
    return 1.0 / (1.0 + jnp.exp(-x))


def merge_fwd(x, proj, ya, yb, outs, lses, wb, wout, *, name="merge_fwd"):
    s, d = x.shape
    tm = _pick(s, (MERGE_TILE, 128))

    def body(x_ref, g0, g1, g2, ya_ref, yb_ref, o0, o1, o2, l0, l1, l2, wb_ref, wo_ref, x1_ref, mg_ref, yc_ref):
        w = _group_weights(l0[...], l1[...], l2[...])
        yc = w[0] * o0[...] + w[1] * o1[...] + w[2] * o2[...]
        yc_ref[...] = yc.astype(BF16)
        merged = None
        for n, (y, gl) in enumerate(((ya_ref[...], g0), (yb_ref[...], g1), (yc, g2))):
            t = _sigmoid(gl[...]) * _dot(y, wb_ref[n])
            merged = t if merged is None else merged + t
        mb = merged.astype(BF16)
        mg_ref[...] = mb
        x1_ref[...] = x_ref[...] + _dot(mb, wo_ref[...])

    row = pl.BlockSpec((tm, d), lambda i: (i, 0))
    half = pl.BlockSpec((tm, MIX_W), lambda i: (i, 0))
    gate = [pl.BlockSpec((tm, d), lambda i, n=n: (i, n)) for n in range(N_BRANCH)]
    return pl.pallas_call(
        body,
        name=name,
        grid=(s // tm,),
        in_specs=[row] + gate + [half] * 8
        + [pl.BlockSpec((N_BRANCH, MIX_W, d), lambda i: (0, 0, 0)), pl.BlockSpec((d, d), lambda i: (0, 0))],
        out_specs=[row, row, half],
        out_shape=[jax.ShapeDtypeStruct((s, d), F32), jax.ShapeDtypeStruct((s, d), BF16), jax.ShapeDtypeStruct((s, MIX_W), BF16)],
        compiler_params=_cparams(("arbitrary",)),
    )(x, proj, proj, proj, ya, yb, *outs, *lses, wb, wout)


def merge_bwd(dx1, proj, ya, yb, yc, outs, lses, wb, wout, *, name="merge_bwd"):
    s, d = dx1.shape
    tm = _pick(s, (MERGE_TILE, 128))

    def body(dx_ref, g0, g1, g2, ya_ref, yb_ref, yc_ref, o0, o1, o2, l0, l1, l2, wb_ref, wo_ref,
             dgl_ref, dz0, dz1, dz2, dya_ref, dyb_ref, do0, do1, do2, dl0, dl1, dl2):
        dm = _dot(dx_ref[...], wo_ref[...], 1, 1)
        dys = []
        for n, (y, gl, dz_ref) in enumerate(((ya_ref, g0, dz0), (yb_ref, g1, dz1), (yc_ref, g2, dz2))):
            z = _dot(y[...], wb_ref[n])
            sg = _sigmoid(gl[...])
            dz = (dm * sg).astype(BF16)
            dz_ref[...] = dz
            dgl_ref[:, n * d : (n + 1) * d] = (dm * z * sg * (1.0 - sg)).astype(BF16)
            dys.append(_dot(dz, wb_ref[n], 1, 1))
        dya_ref[...] = dys[0]
        dyb_ref[...] = dys[1]
        dyc = dys[2]
        w = _group_weights(l0[...], l1[...], l2[...])
        o = [o0[...], o1[...], o2[...]]
        for gi, r in enumerate((do0, do1, do2)):
            r[...] = w[gi] * dyc
        for hd in range(ATT_HG):
            sl = slice(hd * ATT_DH, (hd + 1) * ATT_DH)
            t = [jnp.sum(dyc[:, sl] * o[gi][:, sl], axis=-1, keepdims=True) for gi in range(3)]
            wh = [w[gi][:, sl] for gi in range(3)]
            tbar = wh[0] * t[0] + wh[1] * t[1] + wh[2] * t[2]
            for gi, r in enumerate((dl0, dl1, dl2)):
                r[:, sl] = wh[gi] * (t[gi] - tbar)

    row = pl.BlockSpec((tm, d), lambda i: (i, 0))
    half = pl.BlockSpec((tm, MIX_W), lambda i: (i, 0))
    gate = [pl.BlockSpec((tm, d), lambda i, n=n: (i, n)) for n in range(N_BRANCH)]
    rb = jax.ShapeDtypeStruct((s, d), BF16)
    hf = jax.ShapeDtypeStruct((s, MIX_W), F32)
    return pl.pallas_call(
        body,
        name=name,
        grid=(s // tm,),
        in_specs=[row] + gate + [half] * 9
        + [pl.BlockSpec((N_BRANCH, MIX_W, d), lambda i: (0, 0, 0)), pl.BlockSpec((d, d), lambda i: (0, 0))],
        out_specs=[pl.BlockSpec((tm, N_BRANCH * d), lambda i: (i, 0)), row, row, row] + [half] * 8,
        out_shape=[jax.ShapeDtypeStruct((s, N_BRANCH * d), BF16), rb, rb, rb] + [hf] * 8,
        compiler_params=_cparams(("arbitrary",)),
    )(dx1, proj, proj, proj, ya, yb, yc, *outs, *lses, wb, wout)


ATT_BLOCK = 128
ATT_STEP = 4 * ATT_BLOCK
ATT_GROUPS = ((128, 1), (512, 4), (2048, 16))
N_ATT_HEADS = ATT_HG * len(ATT_GROUPS)
ALIBI_MAX_EXP = 8.0
MASKED = -1e30


def _att_slopes(group):
    return [2.0 ** (-ALIBI_MAX_EXP * (group * ATT_HG + h + 1) / N_ATT_HEADS) for h in range(ATT_HG)]


def _att_scores(q, kk, slope_dil, has_prev):
    scale = ATT_DH**-0.5
    qi = lax.broadcasted_iota(jnp.int32, (ATT_BLOCK, 2 * ATT_BLOCK), 0)
    kj = lax.broadcasted_iota(jnp.int32, (ATT_BLOCK, 2 * ATT_BLOCK), 1)
    dist = qi + ATT_BLOCK - kj
    valid = jnp.logical_and(jnp.logical_and(dist >= 0, dist <= ATT_BLOCK), jnp.logical_or(kj >= ATT_BLOCK, has_prev))
    return jnp.where(valid, _dot(q, kk, 1, 1) * scale - slope_dil * dist.astype(F32), MASKED)


def _att_window(kp_ref, kc_ref, buf):
    buf[:ATT_BLOCK, :] = kp_ref[...]
    buf[ATT_BLOCK:, :] = kc_ref[...]


def attn_fwd(qkv, col, group, *, name="attn_fwd"):
    s = qkv.shape[0]
    window, dil = ATT_GROUPS[group]
    assert window // dil == ATT_BLOCK and s % (dil * ATT_BLOCK) == 0 and col % MIX_W == 0
    nb = s // dil // ATT_BLOCK
    nq = ATT_STEP // ATT_BLOCK
    ng = s // ATT_STEP
    slopes = _att_slopes(group)
    c0 = col // MIX_W

    def body(q_ref, kp_ref, kc_ref, vp_ref, vc_ref, o_ref, l_ref, kbuf, vbuf):
        n = pl.program_id(0)
        _att_window(kp_ref, kc_ref, kbuf)
        _att_window(vp_ref, vc_ref, vbuf)
        for i in range(nq):
            r = slice(i * ATT_BLOCK, (i + 1) * ATT_BLOCK)
            win = slice(i * ATT_BLOCK, (i + 2) * ATT_BLOCK)
            has_prev = ((n * nq + i) % nb) != 0
            for hd in range(ATT_HG):
                sl = slice(hd * ATT_DH, (hd + 1) * ATT_DH)
                sc = _att_scores(q_ref[r, sl], kbuf[win, sl], slopes[hd] * dil, has_prev)
                m = jnp.max(sc, axis=-1, keepdims=True)
                e = jnp.exp(sc - m)
                den = jnp.sum(e, axis=-1, keepdims=True)
                o_ref[r, sl] = _dot(e * (1.0 / den), vbuf[win, sl])
                l_ref[r, sl] = jnp.broadcast_to(m + jnp.log(den), (ATT_BLOCK, ATT_DH))

    cur = lambda c: pl.BlockSpec((ATT_STEP, MIX_W), lambda n, c=c: (n, c))
    prev = lambda c: pl.BlockSpec((ATT_BLOCK, MIX_W), lambda n, c=c: (jnp.maximum(n * nq - 1, 0), c))
    blk = pl.BlockSpec((ATT_STEP, MIX_W), lambda n: (n, 0))
    return pl.pallas_call(
        body,
        name=name,
        grid=(ng,),
        in_specs=[cur(c0), prev(c0 + 1), cur(c0 + 1), prev(c0 + 2), cur(c0 + 2)],
        out_specs=[blk, blk],
        out_shape=[jax.ShapeDtypeStruct((s, MIX_W), F32)] * 2,
        scratch_shapes=[pltpu.VMEM((ATT_BLOCK + ATT_STEP, MIX_W), F32)] * 2,
        compiler_params=_cparams(("arbitrary",)),
    )(qkv, qkv, qkv, qkv, qkv)


def attn_bwd(qkv, col, group, out, lse, dout, dlse, *, name="attn_bwd"):
    s = qkv.shape[0]
    window, dil = ATT_GROUPS[group]
    nb = s // dil // ATT_BLOCK
    nq = ATT_STEP // ATT_BLOCK
    ng = s // ATT_STEP
    slopes = _att_slopes(group)
    c0 = col // MIX_W
    scale = ATT_DH**-0.5
    tail = slice((nq - 1) * ATT_BLOCK, nq * ATT_BLOCK)

    def body(q_ref, kp_ref, kc_ref, vp_ref, vc_ref, o_ref, l_ref, do_ref, dl_ref, dq_ref, dk_ref, dv_ref, acck, accv,
             kbuf, vbuf):
        n = pl.program_id(0)
        live = n < ng

        @pl.when(n == 0)
        def _():
            acck[...] = jnp.zeros_like(acck)
            accv[...] = jnp.zeros_like(accv)

        _att_window(kp_ref, kc_ref, kbuf)
        _att_window(vp_ref, vc_ref, vbuf)
        dk_ref[: (nq - 1) * ATT_BLOCK, :] = acck[: (nq - 1) * ATT_BLOCK, :].astype(BF16)
        dv_ref[: (nq - 1) * ATT_BLOCK, :] = accv[: (nq - 1) * ATT_BLOCK, :].astype(BF16)
        for i in range(nq):
            r = slice(i * ATT_BLOCK, (i + 1) * ATT_BLOCK)
            rp = slice((i - 1) * ATT_BLOCK, i * ATT_BLOCK)
            win = slice(i * ATT_BLOCK, (i + 2) * ATT_BLOCK)
            has_prev = ((jnp.minimum(n, ng - 1) * nq + i) % nb) != 0
            for hd in range(ATT_HG):
                sl = slice(hd * ATT_DH, (hd + 1) * ATT_DH)
                q, do, kk, vv = q_ref[r, sl], do_ref[r, sl], kbuf[win, sl], vbuf[win, sl]
                p = jnp.exp(_att_scores(q, kk, slopes[hd] * dil, has_prev) - l_ref[r, sl][:, :1])
                corr = dl_ref[r, sl][:, :1] - jnp.sum(do * o_ref[r, sl], axis=-1, keepdims=True)
                ds = p * (_dot(do, vv, 1, 1) + corr) * scale
                dq_ref[r, sl] = _dot(ds, kk).astype(BF16)
                dkk = jnp.where(live, _dot(ds, q, 0, 0), 0.0)
                dvv = jnp.where(live, _dot(p, do, 0, 0), 0.0)
                if i == 0:
                    dk_ref[tail, sl] = (acck[tail, sl] + dkk[:ATT_BLOCK]).astype(BF16)
                    dv_ref[tail, sl] = (accv[tail, sl] + dvv[:ATT_BLOCK]).astype(BF16)
                else:
                    acck[rp, sl] += dkk[:ATT_BLOCK]
                    accv[rp, sl] += dvv[:ATT_BLOCK]
                acck[r, sl] = dkk[ATT_BLOCK:]
                accv[r, sl] = dvv[ATT_BLOCK:]

    last = ng - 1
    cur = lambda c: pl.BlockSpec((ATT_STEP, MIX_W), lambda n, c=c: (jnp.minimum(n, last), c))
    prev = lambda c: pl.BlockSpec((ATT_BLOCK, MIX_W), lambda n, c=c: (jnp.maximum(jnp.minimum(n, last) * nq - 1, 0), c))
    behind = pl.BlockSpec((ATT_STEP, MIX_W), lambda n: (jnp.clip(n - 1, 0, last), 0))
    sds = jax.ShapeDtypeStruct((s, MIX_W), BF16)
    return pl.pallas_call(
        body,
        name=name,
        grid=(ng + 1,),
        in_specs=[cur(c0), prev(c0 + 1), cur(c0 + 1), prev(c0 + 2), cur(c0 + 2), cur(0), cur(0), cur(0), cur(0)],
        out_specs=[cur(0), behind, behind],
        out_shape=[sds, sds, sds],
        scratch_shapes=[pltpu.VMEM((ATT_STEP, MIX_W), F32)] * 2 + [pltpu.VMEM((ATT_BLOCK + ATT_STEP, MIX_W), F32)] * 2,
        compiler_params=_cparams(("arbitrary",)),
    )(qkv, qkv, qkv, qkv, qkv, out, lse, dout, dlse)


GLA_HEADS = 4
GLA_DK = 64
GLA_DV = 128
GLA_CHUNK = 64
GLA_TAU = 16.0
GLA_QK = GLA_HEADS * GLA_DK
LR_PAD = 128


def _dot_exact(a, b, ca=1, cb=0):
    return lax.dot_general(a, b, (((ca,), (cb,)), ((), ())), precision=lax.Precision.HIGHEST, preferred_element_type=F32)


GLA_STEP = 4 * GLA_CHUNK


def _gla_group(q, k, v, lr, r, state, wg, bg, go):
    c, n = GLA_CHUNK, q.shape[0]
    z = _dot(lr, wg) + bg
    la = (jnp.minimum(z, 0.0) - jnp.log(1.0 + jnp.exp(-jnp.abs(z)))) * (1.0 / GLA_TAU)
    ri = lax.broadcasted_iota(jnp.int32, (n, n), 0)
    ci = lax.broadcasted_iota(jnp.int32, (n, n), 1)
    shift = c.bit_length() - 1
    in_chunk = jnp.logical_and(ri >= ci, jnp.right_shift(ri, shift) == jnp.right_shift(ci, shift))
    ball = _dot_exact(in_chunk.astype(F32), la)
    ri = lax.broadcasted_iota(jnp.int32, (c, c), 0)
    ci = lax.broadcasted_iota(jnp.int32, (c, c), 1)
    causal = ri >= ci
    qs = q * GLA_DK**-0.5
    ys = []
    for j in range(n // c):
        rows = slice(j * c, (j + 1) * c)
        b = ball[rows]
        bmid = b[c // 2 : c // 2 + 1, :]
        bend = b[c - 1 : c, :]
        q_in = qs[rows] * jnp.exp(b)
        q_mid = qs[rows] * jnp.exp(b - bmid)
        k_mid = k[rows] * jnp.exp(bmid - b)
        k_end = k[rows] * jnp.exp(bend - b)
        yh, upd = [], []
        for h in range(GLA_HEADS):
            sk = slice(h * GLA_DK, (h + 1) * GLA_DK)
            sv = slice(h * GLA_DV, (h + 1) * GLA_DV)
            vh = v[rows, sv]
            inter = _dot(q_in[:, sk], state[:, sk], 1, 1)
            sc = jnp.where(causal, _dot_exact(q_mid[:, sk], k_mid[:, sk], 1, 1), 0.0)
            o = inter + _dot(sc, vh)
            o = o * lax.rsqrt(jnp.mean(o * o, axis=-1, keepdims=True) + RMS_EPS) * go
            rh = r[rows, sv]
            yh.append(o * rh * _sigmoid(rh))
            upd.append(_dot(vh, k_end[:, sk], 0, 0))
        state = jnp.exp(bend) * state + jnp.concatenate(upd, axis=-1)
        ys.append(jnp.concatenate(yh, axis=-1))
    return jnp.concatenate(ys, axis=0), state


def _gla_in_specs(cols, rev, nc):
    c = GLA_STEP
    row = (lambda i: nc - 1 - i) if rev else (lambda i: i)
    qc, kc, vc, lc, rc = cols
    assert qc % GLA_QK == 0 and kc % GLA_QK == 0 and vc % MIX_W == 0 and rc % MIX_W == 0 and lc % LR_PAD == 0
    return [
        pl.BlockSpec((c, GLA_QK), lambda i: (row(i), qc // GLA_QK)),
        pl.BlockSpec((c, GLA_QK), lambda i: (row(i), kc // GLA_QK)),
        pl.BlockSpec((c, MIX_W), lambda i: (row(i), vc // MIX_W)),
        pl.BlockSpec((c, LR_PAD), lambda i: (row(i), lc // LR_PAD)),
        pl.BlockSpec((c, MIX_W), lambda i: (row(i), rc // MIX_W)),
    ], row


def gla_fwd(proj, cols, wg, bg, go, *, name="gla_fwd"):
    s = proj.shape[0]
    nc = s // GLA_STEP
    specs, row = _gla_in_specs(cols, False, nc)

    def body(q_ref, k_ref, v_ref, lr_ref, r_ref, wg_ref, bg_ref, go_ref, y_ref, st_ref, state):
        @pl.when(pl.program_id(0) == 0)
        def _():
            state[...] = jnp.zeros_like(state)

        st = state[...]
        st_ref[...] = st
        y, new = _gla_group(q_ref[...], k_ref[...], v_ref[...], lr_ref[...], r_ref[...], st, wg_ref[...], bg_ref[...], go_ref[...])
        y_ref[...] = y
        state[...] = new

    full = lambda shp: pl.BlockSpec(shp, lambda i: (0, 0))
    return pl.pallas_call(
        body,
        name=name,
        grid=(nc,),
        in_specs=specs + [full((LR_PAD, GLA_QK)), full((1, GLA_QK)), full((1, GLA_DV))],
        out_specs=[pl.BlockSpec((GLA_STEP, MIX_W), lambda i: (i, 0)), pl.BlockSpec((GLA_DV, GLA_QK), lambda i: (i, 0))],
        out_shape=[jax.ShapeDtypeStruct((s, MIX_W), F32), jax.ShapeDtypeStruct((nc * GLA_DV, GLA_QK), F32)],
        scratch_shapes=[pltpu.VMEM((GLA_DV, GLA_QK), F32)],
        compiler_params=_cparams(("arbitrary",)),
    )(proj, proj, proj, proj, proj, wg, bg, go)


def gla_bwd(proj, cols, wg, bg, go, states, dy, *, name="gla_bwd"):
    s = proj.shape[0]
    nc = s // GLA_STEP
    specs, row = _gla_in_specs(cols, True, nc)

    def body(q_ref, k_ref, v_ref, lr_ref, r_ref, wg_ref, bg_ref, go_ref, st_ref, dy_ref,
             dq_ref, dk_ref, dv_ref, dlr_ref, dr_ref, dwg_ref, dbg_ref, dgo_ref, dstate):
        first = pl.program_id(0) == 0

        @pl.when(first)
        def _():
            dstate[...] = jnp.zeros_like(dstate)

        _, vjp = jax.vjp(_gla_group, q_ref[...], k_ref[...], v_ref[...], lr_ref[...], r_ref[...], st_ref[...],
                         wg_ref[...].astype(F32), bg_ref[...], go_ref[...])
        dq, dk, dv, dlr, dr, dst, dwg, dbg, dgo = vjp((dy_ref[...], dstate[...]))
        dq_ref[...] = dq.astype(BF16)
        dk_ref[...] = dk.astype(BF16)
        dv_ref[...] = dv.astype(BF16)
        dlr_ref[...] = dlr.astype(BF16)
        dr_ref[...] = dr.astype(BF16)
        dstate[...] = dst
        _accum(dwg_ref, dwg, first)
        _accum(dbg_ref, dbg, first)
        _accum(dgo_ref, dgo, first)

    c = GLA_STEP
    full = lambda shp: pl.BlockSpec(shp, lambda i: (0, 0))
    rows = lambda w: pl.BlockSpec((c, w), lambda i: (row(i), 0))
    return pl.pallas_call(
        body,
        name=name,
        grid=(nc,),
        in_specs=specs + [full((LR_PAD, GLA_QK)), full((1, GLA_QK)), full((1, GLA_DV)),
                          pl.BlockSpec((GLA_DV, GLA_QK), lambda i: (row(i), 0)), rows(MIX_W)],
        out_specs=[rows(GLA_QK), rows(GLA_QK), rows(MIX_W), rows(LR_PAD), rows(MIX_W),
                   full((LR_PAD, GLA_QK)), full((1, GLA_QK)), full((1, GLA_DV))],
        out_shape=[jax.ShapeDtypeStruct((s, GLA_QK), BF16), jax.ShapeDtypeStruct((s, GLA_QK), BF16),
                   jax.ShapeDtypeStruct((s, MIX_W), BF16), jax.ShapeDtypeStruct((s, LR_PAD), BF16),
                   jax.ShapeDtypeStruct((s, MIX_W), BF16), jax.ShapeDtypeStruct((LR_PAD, GLA_QK), F32),
                   jax.ShapeDtypeStruct((1, GLA_QK), F32), jax.ShapeDtypeStruct((1, GLA_DV), F32)],
        scratch_shapes=[pltpu.VMEM((GLA_DV, GLA_QK), F32)],
        compiler_params=_cparams(("arbitrary",)),
    )(proj, proj, proj, proj, proj, wg, bg, go, states, dy)


S5_G = 32
S5_P = 64
S5_C = 16
S5_N = S5_G * S5_P
S5_TC = 256
SUB = 8


def _s5_disc(a_re, a_im, ls, b_re, b_im):
    step = jnp.exp(ls)
    mag = jnp.exp(a_re * step)
    lr, li = mag * jnp.cos(a_im * step), mag * jnp.sin(a_im * step)
    inv = 1.0 / (a_re * a_re + a_im * a_im)
    nr, ni = lr - 1.0, li
    cr, ci = (nr * a_re + ni * a_im) * inv, (ni * a_re - nr * a_im) * inv
    return lr, li, cr * b_re - ci * b_im, cr * b_im + ci * b_re


def s5_disc_fwd(a_re, a_im, ls, b_re, b_im, *, name="s5_disc"):
    def body(ar, ai, l, br, bi, o0, o1, o2, o3):
        for o, val in zip((o0, o1, o2, o3), _s5_disc(ar[...], ai[...], l[...], br[...], bi[...])):
            o[...] = val

    sds = jax.ShapeDtypeStruct(a_re.shape, F32)
    return pl.pallas_call(body, name=name, out_shape=[sds] * 4)(a_re, a_im, ls, b_re, b_im)


def s5_disc_bwd(a_re, a_im, ls, b_re, b_im, cts, *, name="s5_disc_bwd"):
    def body(ar, ai, l, br, bi, c0, c1, c2, c3, dar, dai, dl, dbr, dbi):
        _, vjp = jax.vjp(_s5_disc, ar[...], ai[...], l[...], br[...], bi[...])
        g = vjp((c0[...], c1[...], c2[...], c3[...]))
        for o, val in zip((dar, dai, dl), g[:3]):
            o[...] = jnp.sum(val, axis=-1, keepdims=True)
        dbr[...] = g[3]
        dbi[...] = g[4]

    col = jax.ShapeDtypeStruct((a_re.shape[0], 1), F32)
    sds = jax.ShapeDtypeStruct(a_re.shape, F32)
    return pl.pallas_call(body, name=name, out_shape=[col, col, col, sds, sds])(a_re, a_im, ls, b_re, b_im, *cts)


def _gelu(y):
    return 0.5 * y * (1.0 + jnp.tanh(0.7978845608028654 * (y + 0.044715 * (y * y * y))))


def _s5_powers(lam, conj):
    lr, li = lam[:, :S5_N], lam[:, S5_N:]
    if conj:
        li = -li
    rows, pr, pi = [], lr, li
    for _ in range(SUB):
        rows.append((pr, pi))
        pr, pi = pr * lr - pi * li, pr * li + pi * lr
    return rows


def _s5_table(rows, reverse):
    ridx = lax.broadcasted_iota(jnp.int32, (SUB, S5_N), 0)
    tr = jnp.zeros((SUB, S5_N), F32)
    ti = jnp.zeros((SUB, S5_N), F32)
    for i in range(SUB):
        pr, pi = rows[SUB - 1 - i] if reverse else rows[i]
        tr = jnp.where(ridx == i, pr, tr)
        ti = jnp.where(ridx == i, pi, ti)
    return tr, ti


def _s5_scan(buf, lam, carry_ref, reverse):
    tc = buf.shape[0]
    nblk = tc // SUB
    rows = _s5_powers(lam, reverse)
    tr, ti = _s5_table(rows, reverse)
    ridx = lax.broadcasted_iota(jnp.int32, (SUB, S5_N), 0)
    steps = []
    for sft, (pr, pi) in ((1, rows[0]), (2, rows[1]), (4, rows[3])):
        keep = (ridx < SUB - sft) if reverse else (ridx >= sft)
        steps.append((SUB - sft if reverse else sft, jnp.where(keep, pr, 0.0), jnp.where(keep, pi, 0.0)))

    def block(j, carry):
        jj = (nblk - 1 - j) if reverse else j
        at = pl.ds(pl.multiple_of(jj * SUB, SUB), SUB)
        re, im = buf[at, :S5_N], buf[at, S5_N:]
        for rot, pr, pi in steps:
            sre, sim = pltpu.roll(re, rot, 0), pltpu.roll(im, rot, 0)
            re, im = re + pr * sre - pi * sim, im + pr * sim + pi * sre
        cr, ci = carry
        re, im = re + tr * cr - ti * ci, im + tr * ci + ti * cr
        buf[at, :S5_N] = re
        buf[at, S5_N:] = im
        edge = 0 if reverse else SUB - 1
        return re[edge : edge + 1, :], im[edge : edge + 1, :]

    c0 = (carry_ref[0:1, :S5_N], carry_ref[0:1, S5_N:])
    cr, ci = lax.fori_loop(0, nblk, block, c0)
    carry_ref[:, :S5_N] = jnp.broadcast_to(cr, (SUB, S5_N))
    carry_ref[:, S5_N:] = jnp.broadcast_to(ci, (SUB, S5_N))


def s5_fwd(proj, ucol, bd, cd, lam, dskip, wglu, bglu, *, name="s5_fwd"):
    s = proj.shape[0]
    tc = _pick(s, (S5_TC, 128, 64, 8))
    assert ucol % MIX_W == 0

    def body(u_ref, bd_ref, cd_ref, lam_ref, d_ref, w_ref, b_ref, y_ref, xs_ref, carry):
        @pl.when(pl.program_id(0) == 0)
        def _():
            carry[...] = jnp.zeros_like(carry)

        u = u_ref[...]
        xs_ref[...] = _dot(u, bd_ref[...])
        _s5_scan(xs_ref, lam_ref[...], carry, False)
        g = _gelu(_dot(xs_ref[...], cd_ref[...]) + d_ref[...] * u)
        y_ref[...] = g * _sigmoid(_dot(g, w_ref[...]) + b_ref[...])

    full = lambda shp: pl.BlockSpec(shp, lambda i: (0, 0))
    return pl.pallas_call(
        body,
        name=name,
        grid=(s // tc,),
        in_specs=[pl.BlockSpec((tc, MIX_W), lambda i: (i, ucol // MIX_W)), full((MIX_W, 2 * S5_N)), full((2 * S5_N, MIX_W)),
                  full((1, 2 * S5_N)), full((1, MIX_W)), full((MIX_W, MIX_W)), full((1, MIX_W))],
        out_specs=[pl.BlockSpec((tc, MIX_W), lambda i: (i, 0)), pl.BlockSpec((tc, 2 * S5_N), lambda i: (i, 0))],
        out_shape=[jax.ShapeDtypeStruct((s, MIX_W), F32), jax.ShapeDtypeStruct((s, 2 * S5_N), F32)],
        scratch_shapes=[pltpu.VMEM((SUB, 2 * S5_N), F32)],
        compiler_params=_cparams(("arbitrary",)),
    )(proj, bd, cd, lam, dskip, wglu, bglu)


def s5_bwd(proj, ucol, xs, bd, cd, lam, dskip, wglu, bglu, dya, *, name="s5_bwd"):
    s = proj.shape[0]
    tc = _pick(s, (S5_TC, 128, 64, 8))
    nch = s // tc
    per = tc // SUB

    def body(u_ref, xs_ref, xp_ref, bd_ref, cd_ref, lam_ref, d_ref, w_ref, b_ref, dya_ref,
             du_ref, adj_ref, g_ref, dpre_ref, dy_ref, dlam_ref, dd_ref, db_ref, buf, carry, lacc):
        i = pl.program_id(0)
        first = i == 0

        @pl.when(first)
        def _():
            carry[...] = jnp.zeros_like(carry)
            lacc[...] = jnp.zeros_like(lacc)

        u, x, dya_ = u_ref[...], xs_ref[...], dya_ref[...]
        y = _dot(x, cd_ref[...]) + d_ref[...] * u
        g, gelu_vjp = jax.vjp(_gelu, y)
        sg = _sigmoid(_dot(g, w_ref[...]) + b_ref[...])
        dpre = dya_ * g * sg * (1.0 - sg)
        (dy,) = gelu_vjp(dya_ * sg + _dot(dpre, w_ref[...], 1, 1))
        g_ref[...] = g.astype(BF16)
        dpre_ref[...] = dpre.astype(BF16)
        dy_ref[...] = dy.astype(BF16)
        db_part = jnp.sum(dpre, axis=0, keepdims=True)
        dd_part = jnp.sum(dy * u, axis=0, keepdims=True)
        buf[...] = _dot(dy, cd_ref[...], 1, 1)
        _s5_scan(buf, lam_ref[...], carry, True)
        a = buf[...]
        adj_ref[...] = a.astype(BF16)
        du_ref[...] = (dy * d_ref[...] + _dot(a, bd_ref[...], 1, 1)).astype(BF16)
        before = jnp.where(i == nch - 1, 0.0, xp_ref[SUB - 1 : SUB, :])
        ridx = lax.broadcasted_iota(jnp.int32, (tc, 2 * S5_N), 0)
        xprev = jnp.where(ridx == 0, before, pltpu.roll(x, 1, 0))
        ar, ai, xr, xi = a[:, :S5_N], a[:, S5_N:], xprev[:, :S5_N], xprev[:, S5_N:]
        lacc[:, :S5_N] += jnp.sum((ar * xr + ai * xi).reshape(per, SUB, S5_N), axis=0)
        lacc[:, S5_N:] += jnp.sum((ai * xr - ar * xi).reshape(per, SUB, S5_N), axis=0)
        _accum(db_ref, db_part, first)
        _accum(dd_ref, dd_part, first)

        @pl.when(i == nch - 1)
        def _():
            dlam_ref[...] = jnp.sum(lacc[...], axis=0, keepdims=True)

    rev = lambda i: nch - 1 - i
    full = lambda shp: pl.BlockSpec(shp, lambda i: (0, 0))
    rows = lambda w: pl.BlockSpec((tc, w), lambda i: (rev(i), 0))
    hb = jax.ShapeDtypeStruct((s, MIX_W), BF16)
    return pl.pallas_call(
        body,
        name=name,
        grid=(nch,),
        in_specs=[pl.BlockSpec((tc, MIX_W), lambda i: (rev(i), ucol // MIX_W)), rows(2 * S5_N),
                  pl.BlockSpec((SUB, 2 * S5_N), lambda i: (jnp.maximum(rev(i) * per - 1, 0), 0)),
                  full((MIX_W, 2 * S5_N)), full((2 * S5_N, MIX_W)), full((1, 2 * S5_N)), full((1, MIX_W)),
                  full((MIX_W, MIX_W)), full((1, MIX_W)), rows(MIX_W)],
        out_specs=[rows(MIX_W), rows(2 * S5_N), rows(MIX_W), rows(MIX_W), rows(MIX_W),
                   full((1, 2 * S5_N)), full((1, MIX_W)), full((1, MIX_W))],
        out_shape=[hb, jax.ShapeDtypeStruct((s, 2 * S5_N), BF16), hb, hb, hb,
                   jax.ShapeDtypeStruct((1, 2 * S5_N), F32), jax.ShapeDtypeStruct((1, MIX_W), F32),
                   jax.ShapeDtypeStruct((1, MIX_W), F32)],
        scratch_shapes=[pltpu.VMEM((tc, 2 * S5_N), F32), pltpu.VMEM((SUB, 2 * S5_N), F32), pltpu.VMEM((SUB, 2 * S5_N), F32)],
        compiler_params=_cparams(("arbitrary",)),
    )(proj, xs, xs, bd, cd, lam, dskip, wglu, bglu, dya)


def _bcast16(a):
    return jnp.broadcast_to(a.reshape(S5_N, 1), (S5_N, S5_C))


def _blockdiag(blocks):
    g, r, c = blocks.shape
    eye = jnp.eye(g, dtype=blocks.dtype)
    return (eye[:, None, :, None] * blocks[:, :, None, :]).reshape(g * r, g * c)


def _blockdiag_extract(dense, r, c):
    g = dense.shape[0] // r
    return jnp.einsum("grgc->grc", dense.reshape(g, r, g, c))


def s5_prepare(a_re, a_im, log_step, b_re, b_im, c_re, c_im):
    disc_in = (_bcast16(a_re), _bcast16(a_im), _bcast16(jnp.broadcast_to(log_step[:, None], (S5_G, S5_P))),
               b_re.reshape(S5_N, S5_C), b_im.reshape(S5_N, S5_C))
    lr, li, bbr, bbi = s5_disc_fwd(*disc_in)
    lam = jnp.concatenate([lr[:, 0], li[:, 0]]).reshape(1, 2 * S5_N)
    to_blocks = lambda t: _blockdiag(t.reshape(S5_G, S5_P, S5_C).transpose(0, 2, 1))
    bd = jnp.concatenate([to_blocks(bbr), to_blocks(bbi)], axis=1).astype(BF16)
    cd = jnp.concatenate([_blockdiag(c_re.transpose(0, 2, 1)), -_blockdiag(c_im.transpose(0, 2, 1))], axis=0).astype(BF16)
    return disc_in, lam, bd, cd


def s5_param_grads(disc_in, dlam, dbd, dcd):
    first_col = lambda v: jnp.pad(v.reshape(S5_N, 1), ((0, 0), (0, S5_C - 1)))
    from_blocks = lambda t: _blockdiag_extract(t, S5_C, S5_P).transpose(0, 2, 1).reshape(S5_N, S5_C)
    cts = (first_col(dlam[0, :S5_N]), first_col(dlam[0, S5_N:]), from_blocks(dbd[:, :S5_N]), from_blocks(dbd[:, S5_N:]))
    dar, dai, dls, dbr, dbi = s5_disc_bwd(*disc_in, cts)
    dcr = _blockdiag_extract(dcd[:S5_N], S5_P, S5_C).transpose(0, 2, 1)
    dci = -_blockdiag_extract(dcd[S5_N:], S5_P, S5_C).transpose(0, 2, 1)
    return (dar.reshape(S5_G, S5_P), dai.reshape(S5_G, S5_P), dls.reshape(S5_G, S5_P).sum(axis=1),
            dbr.reshape(S5_G, S5_P, S5_C), dbi.reshape(S5_G, S5_P, S5_C), dcr, dci)


N_DEV = 8
MESH_ID = pl.DeviceIdType.MESH
ANY = pl.BlockSpec(memory_space=pl.ANY)


def _me():
    return lax.axis_index("x"), lax.axis_index("y"), lax.axis_index("c")


def all_gather(blocks, *, by_core=False, name="all_gather"):
    na = len(blocks)
    shapes = [b.shape[1:] if by_core else b.shape for b in blocks]

    def body(*refs):
        ins, outs = refs[:na], refs[na : 2 * na]
        send_sems, recv_sems = refs[2 * na :]
        x, y, c = _me()
        me, sibling = (x, y, c), (x, y, 1 - c)
        chips = [(1 - x, y), (x, 1 - y), (1 - x, 1 - y)]
        slot = lambda p: 4 * p[0] + 2 * p[1] + p[2]

        def copy(a, k, block, to, src=None):
            dst = outs[a].at[slot(block)]
            return pltpu.make_async_remote_copy(src_ref=dst if src is None else src, dst_ref=dst, send_sem=send_sems.at[a * 7 + k],
                                                recv_sem=recv_sems.at[a * 7 + k], device_id=to, device_id_type=MESH_ID)

        mine = [ins[a].at[c] if by_core else ins[a] for a in range(na)]
        sends = []
        for a in range(na):
            first = [copy(a, 0, me, sibling, src=mine[a])]
            first += [copy(a, 1 + j, me, (*chip, c), src=mine[a]) for j, chip in enumerate(chips)]
            for cp in first:
                cp.start()
            sends += first
        for a in range(na):
            for j, chip in enumerate(chips):
                copy(a, 1 + j, (*chip, c), me).wait_recv()
                fwd = copy(a, 4 + j, (*chip, c), sibling)
                fwd.start()
                sends.append(fwd)
        for a in range(na):
            copy(a, 0, sibling, me).wait_recv()
            for j, chip in enumerate(chips):
                copy(a, 4 + j, (*chip, 1 - c), me).wait_recv()
        for cp in sends:
            cp.wait_send()

    outs = pl.pallas_call(
        body,
        name=name,
        in_specs=[ANY] * na,
        out_specs=[ANY] * na,
        out_shape=[jax.ShapeDtypeStruct((N_DEV, *shp), b.dtype) for shp, b in zip(shapes, blocks)],
        scratch_shapes=[pltpu.SemaphoreType.DMA((na * 7,)), pltpu.SemaphoreType.DMA((na * 7,))],
    )(*blocks)
    x, y, c = _me()
    own = [lax.dynamic_index_in_dim(b, c, 0, keepdims=False) if by_core else b for b in blocks]
    return [lax.dynamic_update_index_in_dim(o, b, 4 * x + 2 * y + c, 0) for o, b in zip(outs, own)]


D2D_PIECES = 8


def _slabs(rows):
    n = D2D_PIECES if rows % (16 * D2D_PIECES) == 0 else 1
    return [pl.ds(i * (rows // n), rows // n) for i in range(n)]


def sibling_swap(arrs, *, name="sibling_swap"):
    na = len(arrs)
    pieces = [[(j, sl) for j in range(a.shape[0]) for sl in (_slabs(a.shape[1]) if a.shape[0] == 1 else _halves(a.shape[1]))]
              for a in arrs]
    base = np.cumsum([0] + [len(p) for p in pieces])

    def body(*refs):
        ins, outs = refs[:na], refs[na : 2 * na]
        send_sems, recv_sems = refs[2 * na :]
        x, y, c = _me()
        sends, recvs = [], []
        for a in range(na):
            for i, (j, sl) in enumerate(pieces[a]):
                k = int(base[a]) + i
                cp = pltpu.make_async_remote_copy(src_ref=ins[a].at[j, sl], dst_ref=outs[a].at[j, sl], send_sem=send_sems.at[k],
                                                  recv_sem=recv_sems.at[k], device_id=(x, y, 1 - c), device_id_type=MESH_ID)
                cp.start()
                sends.append(cp)
        for cp in sends:
            cp.wait_recv()
        for cp in sends:
            cp.wait_send()

    nsem = int(base[-1])
    return pl.pallas_call(
        body,
        name=name,
        in_specs=[ANY] * na,
        out_specs=[ANY] * na,
        out_shape=[jax.ShapeDtypeStruct(a.shape, a.dtype) for a in arrs],
        scratch_shapes=[pltpu.SemaphoreType.DMA((nsem,)), pltpu.SemaphoreType.DMA((nsem,))],
    )(*arrs)


def _halves(rows):
    return [pl.ds(0, rows // 2), pl.ds(rows // 2, rows // 2)] if rows % 32 == 0 else [pl.ds(0, rows)]


def chip_exchange(arrs, *, name="chip_exchange"):
    na = len(arrs)

    def body(*refs):
        ins, outs = refs[:na], refs[na : 2 * na]
        send_sems, recv_sems = refs[2 * na :]
        x, y, c = _me()
        my = 2 * x + y
        peers = []
        for k in range(1, N_CHIPS):
            px, py = (x + ((k >> 1) & 1)) % 2, (y + (k & 1)) % 2
            peers.append(((px, py, c), 2 * px + py))
        sends = []
        for a in range(na):
            for k, (peer, pidx) in enumerate(peers):
                cp = pltpu.make_async_remote_copy(src_ref=ins[a].at[pidx], dst_ref=outs[a].at[my], send_sem=send_sems.at[a * 3 + k],
                                                  recv_sem=recv_sems.at[a * 3 + k], device_id=peer, device_id_type=MESH_ID)
                cp.start()
                sends.append(cp)
        for a in range(na):
            for k, (peer, pidx) in enumerate(peers):
                pltpu.make_async_remote_copy(src_ref=ins[a].at[my], dst_ref=outs[a].at[pidx], send_sem=send_sems.at[a * 3 + k],
                                             recv_sem=recv_sems.at[a * 3 + k], device_id=peer, device_id_type=MESH_ID).wait_recv()
        for cp in sends:
            cp.wait_send()

    outs = pl.pallas_call(
        body,
        name=name,
        in_specs=[ANY] * na,
        out_specs=[ANY] * na,
        out_shape=[jax.ShapeDtypeStruct(a.shape, a.dtype) for a in arrs],
        scratch_shapes=[pltpu.SemaphoreType.DMA((na * 3,)), pltpu.SemaphoreType.DMA((na * 3,))],
    )(*arrs)
    chip = 2 * lax.axis_index("x") + lax.axis_index("y")
    return [lax.dynamic_update_index_in_dim(o, lax.dynamic_index_in_dim(a, chip, 0, keepdims=False), chip, 0)
            for o, a in zip(outs, arrs)]


def add_pair(a, b, *, name="add_pair"):
    n, r, c = a.shape
    tm = _row_tile(r, n * c * 2, 2 << 20)

    def body(a_ref, b_ref, o_ref):
        o_ref[...] = (a_ref[...].astype(F32) + b_ref[...].astype(F32)).astype(o_ref.dtype)

    blk = pl.BlockSpec((n, tm, c), lambda i: (0, i, 0))
    return pl.pallas_call(
        body, name=name, grid=(r // tm,), in_specs=[blk, blk], out_specs=blk, out_shape=jax.ShapeDtypeStruct(a.shape, a.dtype),
        compiler_params=_cparams(("arbitrary",)),
    )(a, b)


def _row_tile(rows, row_bytes, budget):
    best = None
    for t in range(8, rows + 1, 8):
        if rows % t == 0 and t * row_bytes <= budget:
            best = t
    return best or rows


def sum_blocks(a, *, name="sum_blocks"):
    n, r, c = a.shape
    tm = _row_tile(r, n * c * a.dtype.itemsize, 4 << 20)

    def body(a_ref, o_ref):
        acc = a_ref[0].astype(F32)
        for k in range(1, n):
            acc = acc + a_ref[k].astype(F32)
        o_ref[...] = acc

    return pl.pallas_call(
        body,
        name=name,
        grid=(r // tm,),
        in_specs=[pl.BlockSpec((n, tm, c), lambda i: (0, i, 0))],
        out_specs=pl.BlockSpec((tm, c), lambda i: (i, 0)),
        out_shape=jax.ShapeDtypeStruct((r, c), F32),
        compiler_params=_cparams(("arbitrary",)),
    )(a)


ADAM_LR = 0.001
ADAM_B1 = 0.9
ADAM_B2 = 0.999
ADAM_EPS = 1e-08
ADAM_WD = 0.01
ADAM_STEP = 10


def adamw(w, g, m, v, *, name="adamw"):
    r, c = w.shape
    tm = _row_tile(r, c * 4, 1 << 20)

    def body(w_ref, g_ref, m_ref, v_ref, d_ref, nm_ref, nv_ref):
        g_ = g_ref[...]
        m_ = ADAM_B1 * m_ref[...] + (1.0 - ADAM_B1) * g_
        v_ = ADAM_B2 * v_ref[...] + (1.0 - ADAM_B2) * (g_ * g_)
        m_hat = m_ / (1.0 - ADAM_B1**ADAM_STEP)
        v_hat = v_ / (1.0 - ADAM_B2**ADAM_STEP)
        d_ref[...] = -ADAM_LR * (m_hat / (jnp.sqrt(v_hat) + ADAM_EPS) + ADAM_WD * w_ref[...])
        nm_ref[...] = m_
        nv_ref[...] = v_

    blk = pl.BlockSpec((tm, c), lambda i: (i, 0))
    sds = jax.ShapeDtypeStruct((r, c), F32)
    return pl.pallas_call(
        body, name=name, grid=(r // tm,), in_specs=[blk] * 4, out_specs=[blk] * 3, out_shape=[sds] * 3,
        compiler_params=_cparams(("arbitrary",)),
    )(w, g, m, v)


WEIGHTS = ["g_mix", "w_in", "s5_a_re", "s5_a_im", "s5_log_step", "s5_b_re", "s5_b_im", "s5_c_re", "s5_c_im", "s5_d", "w_glu",
           "b_glu", "w_gla_gate", "b_gla_gate", "g_gla_out", "w_branch", "w_out", "g_mem", "g_cross", "w_xq", "w_xkv", "w_xo",
           "g_mlp", "w_up", "w_down", "g_final"]
SHARDED = {"w_in": 1, "w_glu": 0, "w_branch": 2, "w_out": 0, "w_xq": 0, "w_xkv": 1, "w_xo": 0, "w_up": 1, "w_down": 0}
DEPTH = 2
N_CHIPS = 4
D_IN = 9744
C_U, C_QG, C_KG, C_VG, C_LR, C_RG, C_QA, C_KA, C_VA, C_GATE = 0, 512, 768, 1024, 1536, 1552, 2064, 3600, 5136, 6672
A_GATE, A_U, A_QG, A_KG, A_VG, A_RG, A_Q0, A_LR, A_W = 0, 3072, 3584, 3840, 4096, 4608, 5120, 6656, 7168
GLA_COLS = (A_QG, A_KG, A_VG, A_LR, A_RG)


def _split_w_in(w):
    att = lambda g: [w[:, c + MIX_W * g : c + MIX_W * (g + 1)] for c in (C_QA, C_KA, C_VA)]
    a = jnp.concatenate([w[:, C_GATE:D_IN], w[:, C_U:C_QG], w[:, C_QG:C_KG], w[:, C_KG:C_VG], w[:, C_VG:C_LR], w[:, C_RG:C_QA],
                         *att(0), w[:, C_LR:C_RG], jnp.zeros((w.shape[0], A_W - A_LR - 16), w.dtype)], axis=1)
    return a, jnp.concatenate(att(1), axis=1), jnp.concatenate(att(2), axis=1)


def _join_w_in(a, b, c):
    att = lambda k: [a[:, A_Q0 + MIX_W * k : A_Q0 + MIX_W * (k + 1)], b[:, MIX_W * k : MIX_W * (k + 1)], c[:, MIX_W * k : MIX_W * (k + 1)]]
    return jnp.concatenate([a[:, A_U:A_QG], a[:, A_QG:A_KG], a[:, A_KG:A_VG], a[:, A_VG:A_RG], a[:, A_LR : A_LR + 16], a[:, A_RG:A_Q0],
                            *att(0), *att(1), *att(2), a[:, A_GATE:A_U]], axis=1)


def _by_residue(a, d):
    s = a.shape[0]
    return a if d == 1 else a.reshape(s // d, d, -1).transpose(1, 0, 2).reshape(s, -1)


def _in_order(a, d):
    s = a.shape[0]
    return a if d == 1 else a.reshape(d, s // d, -1).transpose(1, 0, 2).reshape(s, -1)


def _pack(arrs):
    flat = jnp.concatenate([a.reshape(-1) for a in arrs])
    n = flat.shape[0]
    rows = -(-n // (256 * 128)) * 256
    return jnp.pad(flat, (0, rows * 128 - n)).reshape(rows, 128)


def _unpack(packed, like):
    flat, out, o = packed.reshape(-1), [], 0
    for a in like:
        n = math.prod(a.shape)
        out.append(flat[o : o + n].reshape(a.shape))
        o += n
    return out


def _layer_fwd(x0, p, kv):
    h = rmsnorm(x0, p["g_mix"], name="mix_norm")
    hs = [h, _by_residue(h, 4), _by_residue(h, 16)]
    proj = [matmul(hs[g], p["w_in_seg"][g], name="in_proj") for g in range(3)]
    ya, xs = s5_fwd(proj[0], A_U, p["bd"], p["cd"], p["lam"], p["s5_d"], p["w_glu"], p["b_glu"])
    yb, gst = gla_fwd(proj[0], GLA_COLS, p["w_gate"], p["b_gla_gate"], p["g_gla_out"])
    att = [attn_fwd(proj[g], A_Q0 if g == 0 else 0, g) for g in range(3)]
    outs = [_in_order(att[g][0], ATT_GROUPS[g][1]) for g in range(3)]
    lses = [_in_order(att[g][1], ATT_GROUPS[g][1]) for g in range(3)]
    x1, merged, yc = merge_fwd(x0, proj[0], ya, yb, outs, lses, p["w_branch"], p["w_out"])
    x2 = cross_fwd(x1, p["g_cross"], p["w_xq"], p["w_xo"], kv)
    x3 = mlp_fwd(x2, p["g_mlp"], p["w_up"], p["w_down"])
    saved = dict(x0=x0, x1=x1, x2=x2, hs=hs, proj=proj, ya=ya, xs=xs, yb=yb, gst=gst, att=att, outs=outs, lses=lses,
                 merged=merged, yc=yc)
    return x3, saved


def _layer_bwd(dx3, p, kv, memn, sv):
    g = {}
    tn = lambda a, b, **kw: matmul(a, b, ta=True, out_dtype=BF16, name="wgrad", **kw)
    dx2, g["g_mlp"], h3, dup, act = mlp_bwd(sv["x2"], p["g_mlp"], p["w_up"], p["w_down"], dx3)
    g["w_up"], g["w_down"] = tn(h3, dup), tn(act, dx3)
    dx1, g["g_cross"], h2, dq, o, dkv = cross_bwd(sv["x1"], p["g_cross"], p["w_xq"], p["w_xo"], kv, dx2)
    g["w_xq"], g["w_xo"], g["w_xkv"] = tn(h2, dq), tn(o, dx2), tn(memn, dkv)
    dmemn = matmul(dkv, p["w_xkv"], tb=True, name="dmem")
    proj = sv["proj"]
    r = merge_bwd(dx1, proj[0], sv["ya"], sv["yb"], sv["yc"], sv["outs"], sv["lses"], p["w_branch"], p["w_out"])
    dgl, dz, dya, dyb, douts, dlses = r[0], r[1:4], r[4], r[5], r[6:9], r[9:12]
    g["w_branch"] = jnp.stack([tn(y, dz[n]) for n, y in enumerate((sv["ya"], sv["yb"], sv["yc"]))])
    g["w_out"] = tn(sv["merged"], dx1)
    datt = []
    for k in range(3):
        dil = ATT_GROUPS[k][1]
        datt.append(attn_bwd(proj[k], A_Q0 if k == 0 else 0, k, sv["att"][k][0], sv["att"][k][1],
                             _by_residue(douts[k], dil), _by_residue(dlses[k], dil)))
    du, adj, gg, dpre, dy, dlam, g["s5_d"], g["b_glu"] = s5_bwd(proj[0], A_U, sv["xs"], p["bd"], p["cd"], p["lam"], p["s5_d"],
                                                                 p["w_glu"], p["b_glu"], dya)
    dbd = matmul(proj[0], adj, ta=True, a_col=(A_U, MIX_W), name="s5_dbd")
    dcd = matmul(sv["xs"], dy, ta=True, name="s5_dcd")
    g["w_glu"] = tn(gg, dpre)
    (g["s5_a_re"], g["s5_a_im"], g["s5_log_step"], g["s5_b_re"], g["s5_b_im"], g["s5_c_re"],
     g["s5_c_im"]) = s5_param_grads(p["disc_in"], dlam, dbd, dcd)
    dqg, dkg, dvg, dlr, drg, dwg, g["b_gla_gate"], g["g_gla_out"] = gla_bwd(proj[0], GLA_COLS, p["w_gate"], p["b_gla_gate"],
                                                                             p["g_gla_out"], sv["gst"], dyb)
    g["w_gla_gate"] = dwg[:GLA_GATE_RANK]
    s = dx3.shape[0]
    dproj = [jnp.concatenate([dgl, du, dqg, dkg, dvg, drg, *datt[0], dlr, jnp.zeros((s, A_W - A_LR - LR_PAD), BF16)], axis=1),
             jnp.concatenate(datt[1], axis=1), jnp.concatenate(datt[2], axis=1)]
    g["w_in"] = _join_w_in(*[tn(sv["hs"][k], dproj[k]) for k in range(3)])
    dhs = [_in_order(matmul(dproj[k], p["w_in_seg"][k], tb=True, name="in_proj_bwd"), ATT_GROUPS[k][1]) for k in range(3)]
    dx0, g["g_mix"] = rmsnorm_bwd(sv["x0"], p["g_mix"], dhs, dx1, name="mix_norm_bwd")
    return dx0, g, dmemn


GLA_GATE_RANK = 16


def kernel(x, mem, *rest):
    nw = len(WEIGHTS)
    w = dict(zip(WEIGHTS, rest[:nw]))
    target = rest[nw]
    m = dict(zip(WEIGHTS, rest[nw + 1 : 2 * nw + 1]))
    v = dict(zip(WEIGHTS, rest[2 * nw + 1 : 3 * nw + 1]))
    x0, memx, target = x[0], mem[0], target[0]
    chip = 2 * lax.axis_index("x") + lax.axis_index("y")

    big = list(SHARDED)
    gate_pad = jnp.pad(w["w_gla_gate"], ((0, 0), (0, 0), (0, LR_PAD - w["w_gla_gate"].shape[2])))
    gathered = all_gather([w[n].astype(BF16) for n in big] + [gate_pad], by_core=True, name="gather_weights")
    gathered = [t.reshape(N_CHIPS, DEPTH, *t.shape[1:]) for t in gathered]
    full = {n: [jnp.concatenate([t[j, l] for j in range(N_CHIPS)], axis=SHARDED[n]) for l in range(DEPTH)]
            for n, t in zip(big, gathered)}
    gate_full = [jnp.concatenate([gathered[-1][j, l][:, : w["w_gla_gate"].shape[2]] for j in range(N_CHIPS)], axis=1)
                 for l in range(DEPTH)]

    memn = rmsnorm(memx, w["g_mem"], name="mem_norm")
    params, kvs = [], []
    for l in range(DEPTH):
        p = {n: full[n][l] for n in big if n != "w_in"}
        p["w_in_seg"] = _split_w_in(full["w_in"][l])
        p["w_gate"] = jnp.pad(gate_full[l], ((0, LR_PAD - GLA_GATE_RANK), (0, 0))).astype(BF16)
        for n in ("g_mix", "g_cross", "g_mlp"):
            p[n] = w[n][l]
        for n in ("s5_d", "b_glu", "b_gla_gate", "g_gla_out"):
            p[n] = w[n][l].reshape(1, -1)
        p["disc_in"], p["lam"], p["bd"], p["cd"] = s5_prepare(*[w[n][l] for n in WEIGHTS[2:9]])
        params.append(p)
        kvs.append(matmul(memn, p["w_xkv"], out_dtype=BF16, name="mem_kv"))

    xl, saved = x0, []
    for l in range(DEPTH):
        xl, sv = _layer_fwd(xl, params[l], kvs[l])
        saved.append(sv)
    loss8, dx, dg_final = loss_head(xl, w["g_final"], target)
    grads, dmem = [None] * DEPTH, []
    for l in reversed(range(DEPTH)):
        dx, grads[l], dm = _layer_bwd(dx, params[l], kvs[l], memn, saved[l])
        dmem.append(dm)
    _, dg_mem = rmsnorm_bwd(memx, w["g_mem"], dmem, jnp.zeros_like(memx), name="mem_norm_bwd")

    core = lax.axis_index("c")

    def halves(n):
        ax = SHARDED[n]
        keep, send = [], []
        for j in range(N_CHIPS):
            p0, p1 = [jnp.split(grads[l][n], N_CHIPS, axis=ax)[j] for l in range(DEPTH)]
            keep.append(jnp.where(core == 0, p0, p1))
            send.append(jnp.where(core == 0, p1, p0))
        flat = lambda ps: jnp.stack(ps).reshape(N_CHIPS, -1, ps[0].shape[-1])
        return flat(keep), flat(send)

    kept, sent = zip(*[halves(n) for n in big])
    theirs = sibling_swap(list(sent), name="swap_layers")
    chip_sums = [add_pair(a, b, name="add_cores") for a, b in zip(kept, theirs)]
    landed = chip_exchange(chip_sums, name="exchange_grads")
    reduced = [sum_blocks(t, name="sum_grads") for t in landed]
    other = sibling_swap([t[None] for t in reduced], name="pair_layers")
    out_g = {}
    for n, mine, theirs in zip(big, reduced, other):
        both = jnp.where(core == 0, jnp.stack([mine, theirs[0]]), jnp.stack([theirs[0], mine]))
        out_g[n] = both.reshape(w[n].shape)

    small = [n for n in WEIGHTS if n not in SHARDED]
    local_small = []
    for n in small:
        if n == "g_mem":
            local_small.append(dg_mem.reshape(w[n].shape))
        elif n == "g_final":
            local_small.append(dg_final.reshape(w[n].shape))
        elif n == "w_gla_gate":
            local_small.append(jnp.stack([grads[l][n] for l in range(DEPTH)]))
        else:
            local_small.append(jnp.stack([grads[l][n].reshape(w[n].shape[1:]) for l in range(DEPTH)]))
    packed = _pack(local_small + [loss8[0, :1]])
    (every,) = all_gather([packed], name="gather_small")
    summed = _unpack(sum_blocks(every, name="sum_small"), local_small + [loss8[0, :1]])
    loss = summed[-1].reshape(())
    for n, t in zip(small, summed[:-1]):
        if n == "w_gla_gate":
            t = lax.dynamic_slice_in_dim(t, chip * w[n].shape[2], w[n].shape[2], axis=2)
        out_g[n] = t

    delta, new_m, new_v = {}, {}, {}
    for n in big:
        c = w[n].shape[-1]
        r = [t.reshape(w[n].shape) for t in adamw(w[n].reshape(-1, c), out_g[n].reshape(-1, c), m[n].reshape(-1, c),
                                                  v[n].reshape(-1, c), name="adamw")]
        delta[n], new_m[n], new_v[n] = r
    like = [w[n] for n in small]
    r = adamw(_pack(like), _pack([out_g[n] for n in small]), _pack([m[n] for n in small]), _pack([v[n] for n in small]),
              name="adamw_small")
    for d, t in zip((delta, new_m, new_v), r):
        d.update(zip(small, _unpack(t, like)))
    return (loss, dx[None], *[out_g[n] for n in WEIGHTS], *[delta[n] for n in WEIGHTS], *[new_m[n] for n in WEIGHTS],
            *[new_v[n] for n in WEIGHTS])
```

```python
import functools
import math

import jax
import jax.numpy as jnp
import numpy as np
from jax import lax
from jax.experimental import pallas as pl
from jax.experimental.pallas import tpu as pltpu

F32 = jnp.float32
BF16 = jnp.bfloat16

VMEM_LIMIT_BYTES = 56 * 1024 * 1024
MATMUL_VMEM_BYTES = 36 * 1024 * 1024


def _cparams(sem):
    return pltpu.CompilerParams(dimension_semantics=sem, vmem_limit_bytes=VMEM_LIMIT_BYTES)


def _dot(a, b, ca=1, cb=0):
    return lax.dot_general(a.astype(BF16), b.astype(BF16), (((ca,), (cb,)), ((), ())), preferred_element_type=F32)


def _pick(n, prefs):
    for p in prefs:
        if n % p == 0:
            return p
    return n


def matmul(a, b, *, ta=False, tb=False, out_dtype=F32, name="mm", a_col=None, b_col=None):
    a_off, a_w = a_col if a_col is not None else (0, a.shape[1])
    b_off, b_w = b_col if b_col is not None else (0, b.shape[1])
    if ta:
        kk, m = a.shape[0], a_w
    else:
        m, kk = a.shape[0], a_w
    if tb:
        n, kb = b.shape[0], b_w
    else:
        kb, n = b.shape[0], b_w
    assert kk == kb, (a.shape, b.shape, ta, tb)
    tm = _pick(m, (512, 256, 128))
    tn = _pick(n, (1024, 512, 256, 128))
    sa, sb, so = a.dtype.itemsize, b.dtype.itemsize, jnp.dtype(out_dtype).itemsize
    fits = lambda t: 2 * (tm * t * sa + t * tn * sb) + tm * tn * (4 + 2 * so) <= MATMUL_VMEM_BYTES
    tk = max([t for t in range(128, kk + 1, 128) if kk % t == 0 and fits(t)] or [_pick(kk, (128,))])
    nk = kk // tk
    a_bytes, b_bytes = m * kk * sa, kk * n * sb
    n_outer = nk == 1 and b_bytes + a_bytes * (n // tn) < a_bytes + b_bytes * (m // tm)
    ij = (lambda g0, g1: (g1, g0)) if n_outer else (lambda g0, g1: (g0, g1))
    if ta:
        assert a_off % tm == 0
        a_spec = pl.BlockSpec((tk, tm), lambda g0, g1, k, o=a_off // tm: (k, ij(g0, g1)[0] + o))
    else:
        assert a_off % tk == 0
        a_spec = pl.BlockSpec((tm, tk), lambda g0, g1, k, o=a_off // tk: (ij(g0, g1)[0], k + o))
    if tb:
        assert b_off % tk == 0
        b_spec = pl.BlockSpec((tn, tk), lambda g0, g1, k, o=b_off // tk: (ij(g0, g1)[1], k + o))
    else:
        assert b_off % tn == 0
        b_spec = pl.BlockSpec((tk, tn), lambda g0, g1, k, o=b_off // tn: (k, ij(g0, g1)[1] + o))

    def body(a_ref, b_ref, o_ref, *acc):
        k = pl.program_id(2)
        p = _dot(a_ref[...], b_ref[...], 0 if ta else 1, 1 if tb else 0)
        if nk == 1:
            o_ref[...] = p.astype(o_ref.dtype)
            return
        (acc_ref,) = acc

        @pl.when(k == 0)
        def _():
            acc_ref[...] = p

        @pl.when(jnp.logical_and(k > 0, k < nk - 1))
        def _():
            acc_ref[...] += p

        @pl.when(k == nk - 1)
        def _():
            o_ref[...] = (acc_ref[...] + p).astype(o_ref.dtype)

    return pl.pallas_call(
        body,
        name=name,
        grid=(n // tn, m // tm, nk) if n_outer else (m // tm, n // tn, nk),
        in_specs=[a_spec, b_spec],
        out_specs=pl.BlockSpec((tm, tn), lambda g0, g1, k: ij(g0, g1)),
        out_shape=jax.ShapeDtypeStruct((m, n), out_dtype),
        scratch_shapes=[pltpu.VMEM((tm, tn), F32)] if nk > 1 else [],
        compiler_params=_cparams(("parallel", "parallel", "arbitrary")),
    )(a, b)


RMS_EPS = 1e-6
ROW_TILE = 512


def _rms_fwd(x, g):
    r = lax.rsqrt(jnp.mean(x * x, axis=-1, keepdims=True) + RMS_EPS)
    return x * r * g


def _rms_bwd(x, g, dh):
    r = lax.rsqrt(jnp.mean(x * x, axis=-1, keepdims=True) + RMS_EPS)
    xh = x * r
    dg = jnp.sum(dh * xh, axis=0, keepdims=True)
    dxh = dh * g
    dx = r * (dxh - xh * jnp.mean(dxh * xh, axis=-1, keepdims=True))
    return dx, dg


def _accum(ref, val, first):
    @pl.when(first)
    def _():
        ref[...] = val

    @pl.when(jnp.logical_not(first))
    def _():
        ref[...] += val


def rmsnorm(x, g, *, out_dtype=BF16, name="rmsnorm"):
    s, d = x.shape
    tm = _pick(s, (ROW_TILE, 256, 128, 8))

    def body(x_ref, g_ref, o_ref):
        o_ref[...] = _rms_fwd(x_ref[...], g_ref[...]).astype(o_ref.dtype)

    return pl.pallas_call(
        body,
        name=name,
        grid=(s // tm,),
        in_specs=[pl.BlockSpec((tm, d), lambda i: (i, 0)), pl.BlockSpec((1, d), lambda i: (0, 0))],
        out_specs=pl.BlockSpec((tm, d), lambda i: (i, 0)),
        out_shape=jax.ShapeDtypeStruct((s, d), out_dtype),
        compiler_params=_cparams(("arbitrary",)),
    )(x, g.reshape(1, d))


def rmsnorm_bwd(x, g, dhs, dres, *, name="rmsnorm_bwd"):
    s, d = x.shape
    tm = _pick(s, (ROW_TILE, 256, 128, 8))
    n = len(dhs)

    def body(x_ref, g_ref, *refs):
        dh_refs, dres_ref, dx_ref, dg_ref = refs[:n], refs[n], refs[n + 1], refs[n + 2]
        dh = dh_refs[0][...].astype(F32)
        for r in dh_refs[1:]:
            dh = dh + r[...].astype(F32)
        dx, dg = _rms_bwd(x_ref[...], g_ref[...], dh)
        dx_ref[...] = dres_ref[...] + dx
        _accum(dg_ref, dg, pl.program_id(0) == 0)

    row = pl.BlockSpec((tm, d), lambda i: (i, 0))
    vec = pl.BlockSpec((1, d), lambda i: (0, 0))
    return pl.pallas_call(
        body,
        name=name,
        grid=(s // tm,),
        in_specs=[row, vec] + [row] * n + [row],
        out_specs=[row, vec],
        out_shape=[jax.ShapeDtypeStruct((s, d), F32), jax.ShapeDtypeStruct((1, d), F32)],
        compiler_params=_cparams(("arbitrary",)),
    )(x, g.reshape(1, d), *dhs, dres)


def loss_head(x, g, target, *, name="loss_head"):
    s, d = x.shape
    tm = _pick(s, (ROW_TILE, 256, 128, 8))

    def body(x_ref, g_ref, t_ref, l_ref, dx_ref, dg_ref):
        x_, g_ = x_ref[...], g_ref[...]
        e = _rms_fwd(x_, g_) - t_ref[...]
        part = 0.5 * jnp.sum(jnp.sum(e * e, axis=-1, keepdims=True), axis=0, keepdims=True) / d
        dx, dg = _rms_bwd(x_, g_, e * (1.0 / d))
        dx_ref[...] = dx
        first = pl.program_id(0) == 0
        _accum(dg_ref, dg, first)
        _accum(l_ref, jnp.broadcast_to(part, (8, 128)), first)

    row = pl.BlockSpec((tm, d), lambda i: (i, 0))
    vec = pl.BlockSpec((1, d), lambda i: (0, 0))
    return pl.pallas_call(
        body,
        name=name,
        grid=(s // tm,),
        in_specs=[row, vec, row],
        out_specs=[pl.BlockSpec((8, 128), lambda i: (0, 0)), row, vec],
        out_shape=[jax.ShapeDtypeStruct((8, 128), F32), jax.ShapeDtypeStruct((s, d), F32), jax.ShapeDtypeStruct((1, d), F32)],
        compiler_params=_cparams(("arbitrary",)),
    )(x, g.reshape(1, d), target)


FF_TILE = 1024


def mlp_fwd(x, g, w_up, w_down, *, name="mlp_fwd"):
    s, d = x.shape
    ff = w_up.shape[1]
    tm, tf = _pick(s, (ROW_TILE, 256, 128)), _pick(ff, (FF_TILE, 512, 256, 128))
    nf = ff // tf

    def body(x_ref, g_ref, wu_ref, wd_ref, o_ref, h_ref, acc_ref):
        f = pl.program_id(1)

        @pl.when(f == 0)
        def _():
            h_ref[...] = _rms_fwd(x_ref[...], g_ref[...]).astype(BF16)
            acc_ref[...] = x_ref[...]

        up = jnp.maximum(_dot(h_ref[...], wu_ref[...]), 0.0)
        acc_ref[...] += _dot(up * up, wd_ref[...])

        @pl.when(f == nf - 1)
        def _():
            o_ref[...] = acc_ref[...]

    return pl.pallas_call(
        body,
        name=name,
        grid=(s // tm, nf),
        in_specs=[
            pl.BlockSpec((tm, d), lambda i, f: (i, 0)),
            pl.BlockSpec((1, d), lambda i, f: (0, 0)),
            pl.BlockSpec((d, tf), lambda i, f: (0, f)),
            pl.BlockSpec((tf, d), lambda i, f: (f, 0)),
        ],
        out_specs=pl.BlockSpec((tm, d), lambda i, f: (i, 0)),
        out_shape=jax.ShapeDtypeStruct((s, d), F32),
        scratch_shapes=[pltpu.VMEM((tm, d), BF16), pltpu.VMEM((tm, d), F32)],
        compiler_params=_cparams(("arbitrary", "arbitrary")),
    )(x, g.reshape(1, d), w_up, w_down)


def mlp_bwd(x, g, w_up, w_down, dy, *, name="mlp_bwd"):
    s, d = x.shape
    ff = w_up.shape[1]
    tm, tf = _pick(s, (ROW_TILE, 256, 128)), _pick(ff, (FF_TILE, 512, 256, 128))
    nf = ff // tf

    def body(x_ref, g_ref, wu_ref, wd_ref, dy_ref, dx_ref, dg_ref, h_ref, dup_ref, act_ref, dyb_ref, acc_ref):
        i, f = pl.program_id(0), pl.program_id(1)

        @pl.when(f == 0)
        def _():
            h_ref[...] = _rms_fwd(x_ref[...], g_ref[...]).astype(BF16)
            dyb_ref[...] = dy_ref[...].astype(BF16)
            acc_ref[...] = jnp.zeros_like(acc_ref)

        up = jnp.maximum(_dot(h_ref[...], wu_ref[...]), 0.0)
        dact = _dot(dyb_ref[...], wd_ref[...], 1, 1)
        dup = (2.0 * up * dact).astype(BF16)
        dup_ref[...] = dup
        act_ref[...] = (up * up).astype(BF16)
        acc_ref[...] += _dot(dup, wu_ref[...], 1, 1)

        @pl.when(f == nf - 1)
        def _():
            dx, dg = _rms_bwd(x_ref[...], g_ref[...], acc_ref[...])
            dx_ref[...] = dy_ref[...] + dx
            _accum(dg_ref, dg, i == 0)

    row = pl.BlockSpec((tm, d), lambda i, f: (i, 0))
    vec = pl.BlockSpec((1, d), lambda i, f: (0, 0))
    wide = pl.BlockSpec((tm, tf), lambda i, f: (i, f))
    return pl.pallas_call(
        body,
        name=name,
        grid=(s // tm, nf),
        in_specs=[row, vec, pl.BlockSpec((d, tf), lambda i, f: (0, f)), pl.BlockSpec((tf, d), lambda i, f: (f, 0)), row],
        out_specs=[row, vec, row, wide, wide],
        out_shape=[
            jax.ShapeDtypeStruct((s, d), F32),
            jax.ShapeDtypeStruct((1, d), F32),
            jax.ShapeDtypeStruct((s, d), BF16),
            jax.ShapeDtypeStruct((s, ff), BF16),
            jax.ShapeDtypeStruct((s, ff), BF16),
        ],
        scratch_shapes=[pltpu.VMEM((tm, d), BF16), pltpu.VMEM((tm, d), F32)],
        compiler_params=_cparams(("arbitrary", "arbitrary")),
    )(x, g.reshape(1, d), w_up, w_down, dy)


X_HEADS = 4


def _softmax_rows(s):
    m = jnp.max(s, axis=-1, keepdims=True)
    e = jnp.exp(s - m)
    return e / jnp.sum(e, axis=-1, keepdims=True)


def cross_fwd(x, g, wq, wo, kv, *, name="cross_fwd"):
    s, d = x.shape
    ml = kv.shape[0]
    dh = d // X_HEADS
    tm = _pick(s, (ROW_TILE, 256, 128))
    scale = dh**-0.5

    def body(x_ref, g_ref, wq_ref, wo_ref, kv_ref, o_ref):
        x_ = x_ref[...]
        q = _dot(_rms_fwd(x_, g_ref[...]), wq_ref[...])
        outs = []
        for hd in range(X_HEADS):
            kh = kv_ref[:, hd * dh : (hd + 1) * dh]
            vh = kv_ref[:, d + hd * dh : d + (hd + 1) * dh]
            p = _softmax_rows(_dot(q[:, hd * dh : (hd + 1) * dh], kh, 1, 1) * scale)
            outs.append(_dot(p, vh))
        o_ref[...] = x_ + _dot(jnp.concatenate(outs, axis=-1), wo_ref[...])

    row = pl.BlockSpec((tm, d), lambda i: (i, 0))
    full = lambda shp: pl.BlockSpec(shp, lambda i: (0, 0))
    return pl.pallas_call(
        body,
        name=name,
        grid=(s // tm,),
        in_specs=[row, full((1, d)), full((d, d)), full((d, d)), full((ml, 2 * d))],
        out_specs=row,
        out_shape=jax.ShapeDtypeStruct((s, d), F32),
        compiler_params=_cparams(("arbitrary",)),
    )(x, g.reshape(1, d), wq, wo, kv)


def cross_bwd(x, g, wq, wo, kv, dy, *, name="cross_bwd"):
    s, d = x.shape
    ml = kv.shape[0]
    dh = d // X_HEADS
    tm = _pick(s, (ROW_TILE, 256, 128))
    scale = dh**-0.5

    def body(x_ref, g_ref, wq_ref, wo_ref, kv_ref, dy_ref, dx_ref, dg_ref, h_ref, dq_ref, o_ref, dkv_ref):
        first = pl.program_id(0) == 0
        x_, g_ = x_ref[...], g_ref[...]
        h = _rms_fwd(x_, g_).astype(BF16)
        h_ref[...] = h
        q = _dot(h, wq_ref[...])
        dy_ = dy_ref[...]
        do = _dot(dy_, wo_ref[...], 1, 1)
        outs, dqs, dks, dvs = [], [], [], []
        for hd in range(X_HEADS):
            sl = slice(hd * dh, (hd + 1) * dh)
            kh = kv_ref[:, sl]
            vh = kv_ref[:, d + hd * dh : d + (hd + 1) * dh]
            qh = q[:, sl]
            p = _softmax_rows(_dot(qh, kh, 1, 1) * scale)
            outs.append(_dot(p, vh))
            doh = do[:, sl]
            dp = _dot(doh, vh, 1, 1)
            ds = p * (dp - jnp.sum(dp * p, axis=-1, keepdims=True)) * scale
            dqs.append(_dot(ds, kh))
            dks.append(_dot(ds, qh, 0, 0))
            dvs.append(_dot(p, doh, 0, 0))
        o_ref[...] = jnp.concatenate(outs, axis=-1).astype(BF16)
        dq = jnp.concatenate(dqs, axis=-1).astype(BF16)
        dq_ref[...] = dq
        dx, dg = _rms_bwd(x_, g_, _dot(dq, wq_ref[...], 1, 1))
        dx_ref[...] = dy_ + dx
        _accum(dkv_ref, jnp.concatenate(dks + dvs, axis=-1), first)
        _accum(dg_ref, dg, first)

    row = pl.BlockSpec((tm, d), lambda i: (i, 0))
    full = lambda shp: pl.BlockSpec(shp, lambda i: (0, 0))
    rowb = jax.ShapeDtypeStruct((s, d), BF16)
    return pl.pallas_call(
        body,
        name=name,
        grid=(s // tm,),
        in_specs=[row, full((1, d)), full((d, d)), full((d, d)), full((ml, 2 * d)), row],
        out_specs=[row, full((1, d)), row, row, row, full((ml, 2 * d))],
        out_shape=[jax.ShapeDtypeStruct((s, d), F32), jax.ShapeDtypeStruct((1, d), F32), rowb, rowb, rowb,
                   jax.ShapeDtypeStruct((ml, 2 * d), F32)],
        compiler_params=_cparams(("arbitrary",)),
    )(x, g.reshape(1, d), wq, wo, kv, dy)


N_BRANCH = 3
MIX_W = 512
ATT_DH = 128
ATT_HG = 4
MERGE_TILE = 256


def _group_weights(l0, l1, l2):
    m = jnp.maximum(jnp.maximum(l0, l1), l2)
    e = [jnp.exp(l0 - m), jnp.exp(l1 - m), jnp.exp(l2 - m)]
    inv = 1.0 / (e[0] + e[1] + e[2])
    return [t * inv for t in e]


def _sigmoid(x):
    return 1.0 / (1.0 + jnp.exp(-x))


def merge_fwd(x, proj, ya, yb, outs, lses, wb, wout, *, name="merge_fwd"):
    s, d = x.shape
    tm = _pick(s, (MERGE_TILE, 128))

    def body(x_ref, g0, g1, g2, ya_ref, yb_ref, o0, o1, o2, l0, l1, l2, wb_ref, wo_ref, x1_ref, mg_ref, yc_ref):
        w = _group_weights(l0[...], l1[...], l2[...])
        yc = w[0] * o0[...] + w[1] * o1[...] + w[2] * o2[...]
        yc_ref[...] = yc.astype(BF16)
        merged = None
        for n, (y, gl) in enumerate(((ya_ref[...], g0), (yb_ref[...], g1), (yc, g2))):
            t = _sigmoid(gl[...]) * _dot(y, wb_ref[n])
            merged = t if merged is None else merged + t
        mb = merged.astype(BF16)
        mg_ref[...] = mb
        x1_ref[...] = x_ref[...] + _dot(mb, wo_ref[...])

    row = pl.BlockSpec((tm, d), lambda i: (i, 0))
    half = pl.BlockSpec((tm, MIX_W), lambda i: (i, 0))
    gate = [pl.BlockSpec((tm, d), lambda i, n=n: (i, n)) for n in range(N_BRANCH)]
    return pl.pallas_call(
        body,
        name=name,
        grid=(s // tm,),
        in_specs=[row] + gate + [half] * 8
        + [pl.BlockSpec((N_BRANCH, MIX_W, d), lambda i: (0, 0, 0)), pl.BlockSpec((d, d), lambda i: (0, 0))],
        out_specs=[row, row, half],
        out_shape=[jax.ShapeDtypeStruct((s, d), F32), jax.ShapeDtypeStruct((s, d), BF16), jax.ShapeDtypeStruct((s, MIX_W), BF16)],
        compiler_params=_cparams(("arbitrary",)),
    )(x, proj, proj, proj, ya, yb, *outs, *lses, wb, wout)


def merge_bwd(dx1, proj, ya, yb, yc, outs, lses, wb, wout, *, name="merge_bwd"):
    s, d = dx1.shape
    tm = _pick(s, (MERGE_TILE, 128))

    def body(dx_ref, g0, g1, g2, ya_ref, yb_ref, yc_ref, o0, o1, o2, l0, l1, l2, wb_ref, wo_ref,
             dgl_ref, dz0, dz1, dz2, dya_ref, dyb_ref, do0, do1, do2, dl0, dl1, dl2):
        dm = _dot(dx_ref[...], wo_ref[...], 1, 1)
        dys = []
        for n, (y, gl, dz_ref) in enumerate(((ya_ref, g0, dz0), (yb_ref, g1, dz1), (yc_ref, g2, dz2))):
            z = _dot(y[...], wb_ref[n])
            sg = _sigmoid(gl[...])
            dz = (dm * sg).astype(BF16)
            dz_ref[...] = dz
            dgl_ref[:, n * d : (n + 1) * d] = (dm * z * sg * (1.0 - sg)).astype(BF16)
            dys.append(_dot(dz, wb_ref[n], 1, 1))
        dya_ref[...] = dys[0]
        dyb_ref[...] = dys[1]
        dyc = dys[2]
        w = _group_weights(l0[...], l1[...], l2[...])
        o = [o0[...], o1[...], o2[...]]
        for gi, r in enumerate((do0, do1, do2)):
            r[...] = w[gi] * dyc
        for hd in range(ATT_HG):
            sl = slice(hd * ATT_DH, (hd + 1) * ATT_DH)
            t = [jnp.sum(dyc[:, sl] * o[gi][:, sl], axis=-1, keepdims=True) for gi in range(3)]
            wh = [w[gi][:, sl] for gi in range(3)]
            tbar = wh[0] * t[0] + wh[1] * t[1] + wh[2] * t[2]
            for gi, r in enumerate((dl0, dl1, dl2)):
                r[:, sl] = wh[gi] * (t[gi] - tbar)

    row = pl.BlockSpec((tm, d), lambda i: (i, 0))
    half = pl.BlockSpec((tm, MIX_W), lambda i: (i, 0))
    gate = [pl.BlockSpec((tm, d), lambda i, n=n: (i, n)) for n in range(N_BRANCH)]
    rb = jax.ShapeDtypeStruct((s, d), BF16)
    hf = jax.ShapeDtypeStruct((s, MIX_W), F32)
    return pl.pallas_call(
        body,
        name=name,
        grid=(s // tm,),
        in_specs=[row] + gate + [half] * 9
        + [pl.BlockSpec((N_BRANCH, MIX_W, d), lambda i: (0, 0, 0)), pl.BlockSpec((d, d), lambda i: (0, 0))],
        out_specs=[pl.BlockSpec((tm, N_BRANCH * d), lambda i: (i, 0)), row, row, row] + [half] * 8,
        out_shape=[jax.ShapeDtypeStruct((s, N_BRANCH * d), BF16), rb, rb, rb] + [hf] * 8,
        compiler_params=_cparams(("arbitrary",)),
    )(dx1, proj, proj, proj, ya, yb, yc, *outs, *lses, wb, wout)


ATT_BLOCK = 128
ATT_STEP = 4 * ATT_BLOCK
ATT_GROUPS = ((128, 1), (512, 4), (2048, 16))
N_ATT_HEADS = ATT_HG * len(ATT_GROUPS)
ALIBI_MAX_EXP = 8.0
MASKED = -1e30


def _att_slopes(group):
    return [2.0 ** (-ALIBI_MAX_EXP * (group * ATT_HG + h + 1) / N_ATT_HEADS) for h in range(ATT_HG)]


def _att_scores(q, kk, slope_dil, has_prev):
    scale = ATT_DH**-0.5
    qi = lax.broadcasted_iota(jnp.int32, (ATT_BLOCK, 2 * ATT_BLOCK), 0)
    kj = lax.broadcasted_iota(jnp.int32, (ATT_BLOCK, 2 * ATT_BLOCK), 1)
    dist = qi + ATT_BLOCK - kj
    valid = jnp.logical_and(jnp.logical_and(dist >= 0, dist <= ATT_BLOCK), jnp.logical_or(kj >= ATT_BLOCK, has_prev))
    return jnp.where(valid, _dot(q, kk, 1, 1) * scale - slope_dil * dist.astype(F32), MASKED)


def _att_window(kp_ref, kc_ref, buf):
    buf[:ATT_BLOCK, :] = kp_ref[...]
    buf[ATT_BLOCK:, :] = kc_ref[...]


def attn_fwd(qkv, col, group, *, name="attn_fwd"):
    s = qkv.shape[0]
    window, dil = ATT_GROUPS[group]
    assert window // dil == ATT_BLOCK and s % (dil * ATT_BLOCK) == 0 and col % MIX_W == 0
    nb = s // dil // ATT_BLOCK
    nq = ATT_STEP // ATT_BLOCK
    ng = s // ATT_STEP
    slopes = _att_slopes(group)
    c0 = col // MIX_W

    def body(q_ref, kp_ref, kc_ref, vp_ref, vc_ref, o_ref, l_ref, kbuf, vbuf):
        n = pl.program_id(0)
        _att_window(kp_ref, kc_ref, kbuf)
        _att_window(vp_ref, vc_ref, vbuf)
        for i in range(nq):
            r = slice(i * ATT_BLOCK, (i + 1) * ATT_BLOCK)
            win = slice(i * ATT_BLOCK, (i + 2) * ATT_BLOCK)
            has_prev = ((n * nq + i) % nb) != 0
            for hd in range(ATT_HG):
                sl = slice(hd * ATT_DH, (hd + 1) * ATT_DH)
                sc = _att_scores(q_ref[r, sl], kbuf[win, sl], slopes[hd] * dil, has_prev)
                m = jnp.max(sc, axis=-1, keepdims=True)
                e = jnp.exp(sc - m)
                den = jnp.sum(e, axis=-1, keepdims=True)
                o_ref[r, sl] = _dot(e * (1.0 / den), vbuf[win, sl])
                l_ref[r, sl] = jnp.broadcast_to(m + jnp.log(den), (ATT_BLOCK, ATT_DH))

    cur = lambda c: pl.BlockSpec((ATT_STEP, MIX_W), lambda n, c=c: (n, c))
    prev = lambda c: pl.BlockSpec((ATT_BLOCK, MIX_W), lambda n, c=c: (jnp.maximum(n * nq - 1, 0), c))
    blk = pl.BlockSpec((ATT_STEP, MIX_W), lambda n: (n, 0))
    return pl.pallas_call(
        body,
        name=name,
        grid=(ng,),
        in_specs=[cur(c0), prev(c0 + 1), cur(c0 + 1), prev(c0 + 2), cur(c0 + 2)],
        out_specs=[blk, blk],
        out_shape=[jax.ShapeDtypeStruct((s, MIX_W), F32)] * 2,
        scratch_shapes=[pltpu.VMEM((ATT_BLOCK + ATT_STEP, MIX_W), F32)] * 2,
        compiler_params=_cparams(("arbitrary",)),
    )(qkv, qkv, qkv, qkv, qkv)


def attn_bwd(qkv, col, group, out, lse, dout, dlse, *, name="attn_bwd"):
    s = qkv.shape[0]
    window, dil = ATT_GROUPS[group]
    nb = s // dil // ATT_BLOCK
    nq = ATT_STEP // ATT_BLOCK
    ng = s // ATT_STEP
    slopes = _att_slopes(group)
    c0 = col // MIX_W
    scale = ATT_DH**-0.5
    tail = slice((nq - 1) * ATT_BLOCK, nq * ATT_BLOCK)

    def body(q_ref, kp_ref, kc_ref, vp_ref, vc_ref, o_ref, l_ref, do_ref, dl_ref, dq_ref, dk_ref, dv_ref, acck, accv,
             kbuf, vbuf):
        n = pl.program_id(0)
        live = n < ng

        @pl.when(n == 0)
        def _():
            acck[...] = jnp.zeros_like(acck)
            accv[...] = jnp.zeros_like(accv)

        _att_window(kp_ref, kc_ref, kbuf)
        _att_window(vp_ref, vc_ref, vbuf)
        dk_ref[: (nq - 1) * ATT_BLOCK, :] = acck[: (nq - 1) * ATT_BLOCK, :].astype(BF16)
        dv_ref[: (nq - 1) * ATT_BLOCK, :] = accv[: (nq - 1) * ATT_BLOCK, :].astype(BF16)
        for i in range(nq):
            r = slice(i * ATT_BLOCK, (i + 1) * ATT_BLOCK)
            rp = slice((i - 1) * ATT_BLOCK, i * ATT_BLOCK)
            win = slice(i * ATT_BLOCK, (i + 2) * ATT_BLOCK)
            has_prev = ((jnp.minimum(n, ng - 1) * nq + i) % nb) != 0
            for hd in range(ATT_HG):
                sl = slice(hd * ATT_DH, (hd + 1) * ATT_DH)
                q, do, kk, vv = q_ref[r, sl], do_ref[r, sl], kbuf[win, sl], vbuf[win, sl]
                p = jnp.exp(_att_scores(q, kk, slopes[hd] * dil, has_prev) - l_ref[r, sl][:, :1])
                corr = dl_ref[r, sl][:, :1] - jnp.sum(do * o_ref[r, sl], axis=-1, keepdims=True)
                ds = p * (_dot(do, vv, 1, 1) + corr) * scale
                dq_ref[r, sl] = _dot(ds, kk).astype(BF16)
                dkk = jnp.where(live, _dot(ds, q, 0, 0), 0.0)
                dvv = jnp.where(live, _dot(p, do, 0, 0), 0.0)
                if i == 0:
                    dk_ref[tail, sl] = (acck[tail, sl] + dkk[:ATT_BLOCK]).astype(BF16)
                    dv_ref[tail, sl] = (accv[tail, sl] + dvv[:ATT_BLOCK]).astype(BF16)
                else:
                    acck[rp, sl] += dkk[:ATT_BLOCK]
                    accv[rp, sl] += dvv[:ATT_BLOCK]
                acck[r, sl] = dkk[ATT_BLOCK:]
                accv[r, sl] = dvv[ATT_BLOCK:]

    last = ng - 1
    cur = lambda c: pl.BlockSpec((ATT_STEP, MIX_W), lambda n, c=c: (jnp.minimum(n, last), c))
    prev = lambda c: pl.BlockSpec((ATT_BLOCK, MIX_W), lambda n, c=c: (jnp.maximum(jnp.minimum(n, last) * nq - 1, 0), c))
    behind = pl.BlockSpec((ATT_STEP, MIX_W), lambda n: (jnp.clip(n - 1, 0, last), 0))
    sds = jax.ShapeDtypeStruct((s, MIX_W), BF16)
    return pl.pallas_call(
        body,
        name=name,
        grid=(ng + 1,),
        in_specs=[cur(c0), prev(c0 + 1), cur(c0 + 1), prev(c0 + 2), cur(c0 + 2), cur(0), cur(0), cur(0), cur(0)],
        out_specs=[cur(0), behind, behind],
        out_shape=[sds, sds, sds],
        scratch_shapes=[pltpu.VMEM((ATT_STEP, MIX_W), F32)] * 2 + [pltpu.VMEM((ATT_BLOCK + ATT_STEP, MIX_W), F32)] * 2,
        compiler_params=_cparams(("arbitrary",)),
    )(qkv, qkv, qkv, qkv, qkv, out, lse, dout, dlse)


GLA_HEADS = 4
GLA_DK = 64
GLA_DV = 128
GLA_CHUNK = 64
GLA_TAU = 16.0
GLA_QK = GLA_HEADS * GLA_DK
LR_PAD = 128


def _dot_exact(a, b, ca=1, cb=0):
    return lax.dot_general(a, b, (((ca,), (cb,)), ((), ())), precision=lax.Precision.HIGHEST, preferred_element_type=F32)


GLA_STEP = 4 * GLA_CHUNK


def _gla_group(q, k, v, lr, r, state, wg, bg, go):
    c, n = GLA_CHUNK, q.shape[0]
    z = _dot(lr, wg) + bg
    la = (jnp.minimum(z, 0.0) - jnp.log(1.0 + jnp.exp(-jnp.abs(z)))) * (1.0 / GLA_TAU)
    ri = lax.broadcasted_iota(jnp.int32, (n, n), 0)
    ci = lax.broadcasted_iota(jnp.int32, (n, n), 1)
    shift = c.bit_length() - 1
    in_chunk = jnp.logical_and(ri >= ci, jnp.right_shift(ri, shift) == jnp.right_shift(ci, shift))
    ball = _dot_exact(in_chunk.astype(F32), la)
    ri = lax.broadcasted_iota(jnp.int32, (c, c), 0)
    ci = lax.broadcasted_iota(jnp.int32, (c, c), 1)
    causal = ri >= ci
    qs = q * GLA_DK**-0.5
    ys = []
    for j in range(n // c):
        rows = slice(j * c, (j + 1) * c)
        b = ball[rows]
        bmid = b[c // 2 : c // 2 + 1, :]
        bend = b[c - 1 : c, :]
        q_in = qs[rows] * jnp.exp(b)
        q_mid = qs[rows] * jnp.exp(b - bmid)
        k_mid = k[rows] * jnp.exp(bmid - b)
        k_end = k[rows] * jnp.exp(bend - b)
        yh, upd = [], []
        for h in range(GLA_HEADS):
            sk = slice(h * GLA_DK, (h + 1) * GLA_DK)
            sv = slice(h * GLA_DV, (h + 1) * GLA_DV)
            vh = v[rows, sv]
            inter = _dot(q_in[:, sk], state[:, sk], 1, 1)
            sc = jnp.where(causal, _dot_exact(q_mid[:, sk], k_mid[:, sk], 1, 1), 0.0)
            o = inter + _dot(sc, vh)
            o = o * lax.rsqrt(jnp.mean(o * o, axis=-1, keepdims=True) + RMS_EPS) * go
            rh = r[rows, sv]
            yh.append(o * rh * _sigmoid(rh))
            upd.append(_dot(vh, k_end[:, sk], 0, 0))
        state = jnp.exp(bend) * state + jnp.concatenate(upd, axis=-1)
        ys.append(jnp.concatenate(yh, axis=-1))
    return jnp.concatenate(ys, axis=0), state


def _gla_in_specs(cols, rev, nc):
    c = GLA_STEP
    row = (lambda i: nc - 1 - i) if rev else (lambda i: i)
    qc, kc, vc, lc, rc = cols
    assert qc % GLA_QK == 0 and kc % GLA_QK == 0 and vc % MIX_W == 0 and rc % MIX_W == 0 and lc % LR_PAD == 0
    return [
        pl.BlockSpec((c, GLA_QK), lambda i: (row(i), qc // GLA_QK)),
        pl.BlockSpec((c, GLA_QK), lambda i: (row(i), kc // GLA_QK)),
        pl.BlockSpec((c, MIX_W), lambda i: (row(i), vc // MIX_W)),
        pl.BlockSpec((c, LR_PAD), lambda i: (row(i), lc // LR_PAD)),
        pl.BlockSpec((c, MIX_W), lambda i: (row(i), rc // MIX_W)),
    ], row


def gla_fwd(proj, cols, wg, bg, go, *, name="gla_fwd"):
    s = proj.shape[0]
    nc = s // GLA_STEP
    specs, row = _gla_in_specs(cols, False, nc)

    def body(q_ref, k_ref, v_ref, lr_ref, r_ref, wg_ref, bg_ref, go_ref, y_ref, st_ref, state):
        @pl.when(pl.program_id(0) == 0)
        def _():
            state[...] = jnp.zeros_like(state)

        st = state[...]
        st_ref[...] = st
        y, new = _gla_group(q_ref[...], k_ref[...], v_ref[...], lr_ref[...], r_ref[...], st, wg_ref[...], bg_ref[...], go_ref[...])
        y_ref[...] = y
        state[...] = new

    full = lambda shp: pl.BlockSpec(shp, lambda i: (0, 0))
    return pl.pallas_call(
        body,
        name=name,
        grid=(nc,),
        in_specs=specs + [full((LR_PAD, GLA_QK)), full((1, GLA_QK)), full((1, GLA_DV))],
        out_specs=[pl.BlockSpec((GLA_STEP, MIX_W), lambda i: (i, 0)), pl.BlockSpec((GLA_DV, GLA_QK), lambda i: (i, 0))],
        out_shape=[jax.ShapeDtypeStruct((s, MIX_W), F32), jax.ShapeDtypeStruct((nc * GLA_DV, GLA_QK), F32)],
        scratch_shapes=[pltpu.VMEM((GLA_DV, GLA_QK), F32)],
        compiler_params=_cparams(("arbitrary",)),
    )(proj, proj, proj, proj, proj, wg, bg, go)


def gla_bwd(proj, cols, wg, bg, go, states, dy, *, name="gla_bwd"):
    s = proj.shape[0]
    nc = s // GLA_STEP
    specs, row = _gla_in_specs(cols, True, nc)

    def body(q_ref, k_ref, v_ref, lr_ref, r_ref, wg_ref, bg_ref, go_ref, st_ref, dy_ref,
             dq_ref, dk_ref, dv_ref, dlr_ref, dr_ref, dwg_ref, dbg_ref, dgo_ref, dstate):
        first = pl.program_id(0) == 0

        @pl.when(first)
        def _():
            dstate[...] = jnp.zeros_like(dstate)

        _, vjp = jax.vjp(_gla_group, q_ref[...], k_ref[...], v_ref[...], lr_ref[...], r_ref[...], st_ref[...],
                         wg_ref[...].astype(F32), bg_ref[...], go_ref[...])
        dq, dk, dv, dlr, dr, dst, dwg, dbg, dgo = vjp((dy_ref[...], dstate[...]))
        dq_ref[...] = dq.astype(BF16)
        dk_ref[...] = dk.astype(BF16)
        dv_ref[...] = dv.astype(BF16)
        dlr_ref[...] = dlr.astype(BF16)
        dr_ref[...] = dr.astype(BF16)
        dstate[...] = dst
        _accum(dwg_ref, dwg, first)
        _accum(dbg_ref, dbg, first)
        _accum(dgo_ref, dgo, first)

    c = GLA_STEP
    full = lambda shp: pl.BlockSpec(shp, lambda i: (0, 0))
    rows = lambda w: pl.BlockSpec((c, w), lambda i: (row(i), 0))
    return pl.pallas_call(
        body,
        name=name,
        grid=(nc,),
        in_specs=specs + [full((LR_PAD, GLA_QK)), full((1, GLA_QK)), full((1, GLA_DV)),
                          pl.BlockSpec((GLA_DV, GLA_QK), lambda i: (row(i), 0)), rows(MIX_W)],
        out_specs=[rows(GLA_QK), rows(GLA_QK), rows(MIX_W), rows(LR_PAD), rows(MIX_W),
                   full((LR_PAD, GLA_QK)), full((1, GLA_QK)), full((1, GLA_DV))],
        out_shape=[jax.ShapeDtypeStruct((s, GLA_QK), BF16), jax.ShapeDtypeStruct((s, GLA_QK), BF16),
                   jax.ShapeDtypeStruct((s, MIX_W), BF16), jax.ShapeDtypeStruct((s, LR_PAD), BF16),
                   jax.ShapeDtypeStruct((s, MIX_W), BF16), jax.ShapeDtypeStruct((LR_PAD, GLA_QK), F32),
                   jax.ShapeDtypeStruct((1, GLA_QK), F32), jax.ShapeDtypeStruct((1, GLA_DV), F32)],
        scratch_shapes=[pltpu.VMEM((GLA_DV, GLA_QK), F32)],
        compiler_params=_cparams(("arbitrary",)),
    )(proj, proj, proj, proj, proj, wg, bg, go, states, dy)


S5_G = 32
S5_P = 64
S5_C = 16
S5_N = S5_G * S5_P
S5_TC = 256
SUB = 8


def _s5_disc(a_re, a_im, ls, b_re, b_im):
    step = jnp.exp(ls)
    mag = jnp.exp(a_re * step)
    lr, li = mag * jnp.cos(a_im * step), mag * jnp.sin(a_im * step)
    inv = 1.0 / (a_re * a_re + a_im * a_im)
    nr, ni = lr - 1.0, li
    cr, ci = (nr * a_re + ni * a_im) * inv, (ni * a_re - nr * a_im) * inv
    return lr, li, cr * b_re - ci * b_im, cr * b_im + ci * b_re


def s5_disc_fwd(a_re, a_im, ls, b_re, b_im, *, name="s5_disc"):
    def body(ar, ai, l, br, bi, o0, o1, o2, o3):
        for o, val in zip((o0, o1, o2, o3), _s5_disc(ar[...], ai[...], l[...], br[...], bi[...])):
            o[...] = val

    sds = jax.ShapeDtypeStruct(a_re.shape, F32)
    return pl.pallas_call(body, name=name, out_shape=[sds] * 4)(a_re, a_im, ls, b_re, b_im)


def s5_disc_bwd(a_re, a_im, ls, b_re, b_im, cts, *, name="s5_disc_bwd"):
    def body(ar, ai, l, br, bi, c0, c1, c2, c3, dar, dai, dl, dbr, dbi):
        _, vjp = jax.vjp(_s5_disc, ar[...], ai[...], l[...], br[...], bi[...])
        g = vjp((c0[...], c1[...], c2[...], c3[...]))
        for o, val in zip((dar, dai, dl), g[:3]):
            o[...] = jnp.sum(val, axis=-1, keepdims=True)
        dbr[...] = g[3]
        dbi[...] = g[4]

    col = jax.ShapeDtypeStruct((a_re.shape[0], 1), F32)
    sds = jax.ShapeDtypeStruct(a_re.shape, F32)
    return pl.pallas_call(body, name=name, out_shape=[col, col, col, sds, sds])(a_re, a_im, ls, b_re, b_im, *cts)


def _gelu(y):
    return 0.5 * y * (1.0 + jnp.tanh(0.7978845608028654 * (y + 0.044715 * (y * y * y))))


def _s5_powers(lam, conj):
    lr, li = lam[:, :S5_N], lam[:, S5_N:]
    if conj:
        li = -li
    rows, pr, pi = [], lr, li
    for _ in range(SUB):
        rows.append((pr, pi))
        pr, pi = pr * lr - pi * li, pr * li + pi * lr
    return rows


def _s5_table(rows, reverse):
    ridx = lax.broadcasted_iota(jnp.int32, (SUB, S5_N), 0)
    tr = jnp.zeros((SUB, S5_N), F32)
    ti = jnp.zeros((SUB, S5_N), F32)
    for i in range(SUB):
        pr, pi = rows[SUB - 1 - i] if reverse else rows[i]
        tr = jnp.where(ridx == i, pr, tr)
        ti = jnp.where(ridx == i, pi, ti)
    return tr, ti


def _s5_scan(buf, lam, carry_ref, reverse):
    tc = buf.shape[0]
    nblk = tc // SUB
    rows = _s5_powers(lam, reverse)
    tr, ti = _s5_table(rows, reverse)
    ridx = lax.broadcasted_iota(jnp.int32, (SUB, S5_N), 0)
    steps = []
    for sft, (pr, pi) in ((1, rows[0]), (2, rows[1]), (4, rows[3])):
        keep = (ridx < SUB - sft) if reverse else (ridx >= sft)
        steps.append((SUB - sft if reverse else sft, jnp.where(keep, pr, 0.0), jnp.where(keep, pi, 0.0)))

    def block(j, carry):
        jj = (nblk - 1 - j) if reverse else j
        at = pl.ds(pl.multiple_of(jj * SUB, SUB), SUB)
        re, im = buf[at, :S5_N], buf[at, S5_N:]
        for rot, pr, pi in steps:
            sre, sim = pltpu.roll(re, rot, 0), pltpu.roll(im, rot, 0)
            re, im = re + pr * sre - pi * sim, im + pr * sim + pi * sre
        cr, ci = carry
        re, im = re + tr * cr - ti * ci, im + tr * ci + ti * cr
        buf[at, :S5_N] = re
        buf[at, S5_N:] = im
        edge = 0 if reverse else SUB - 1
        return re[edge : edge + 1, :], im[edge : edge + 1, :]

    c0 = (carry_ref[0:1, :S5_N], carry_ref[0:1, S5_N:])
    cr, ci = lax.fori_loop(0, nblk, block, c0)
    carry_ref[:, :S5_N] = jnp.broadcast_to(cr, (SUB, S5_N))
    carry_ref[:, S5_N:] = jnp.broadcast_to(ci, (SUB, S5_N))


def s5_fwd(proj, ucol, bd, cd, lam, dskip, wglu, bglu, *, name="s5_fwd"):
    s = proj.shape[0]
    tc = _pick(s, (S5_TC, 128, 64, 8))
    assert ucol % MIX_W == 0

    def body(u_ref, bd_ref, cd_ref, lam_ref, d_ref, w_ref, b_ref, y_ref, xs_ref, carry):
        @pl.when(pl.program_id(0) == 0)
        def _():
            carry[...] = jnp.zeros_like(carry)

        u = u_ref[...]
        xs_ref[...] = _dot(u, bd_ref[...])
        _s5_scan(xs_ref, lam_ref[...], carry, False)
        g = _gelu(_dot(xs_ref[...], cd_ref[...]) + d_ref[...] * u)
        y_ref[...] = g * _sigmoid(_dot(g, w_ref[...]) + b_ref[...])

    full = lambda shp: pl.BlockSpec(shp, lambda i: (0, 0))
    return pl.pallas_call(
        body,
        name=name,
        grid=(s // tc,),
        in_specs=[pl.BlockSpec((tc, MIX_W), lambda i: (i, ucol // MIX_W)), full((MIX_W, 2 * S5_N)), full((2 * S5_N, MIX_W)),
                  full((1, 2 * S5_N)), full((1, MIX_W)), full((MIX_W, MIX_W)), full((1, MIX_W))],
        out_specs=[pl.BlockSpec((tc, MIX_W), lambda i: (i, 0)), pl.BlockSpec((tc, 2 * S5_N), lambda i: (i, 0))],
        out_shape=[jax.ShapeDtypeStruct((s, MIX_W), F32), jax.ShapeDtypeStruct((s, 2 * S5_N), F32)],
        scratch_shapes=[pltpu.VMEM((SUB, 2 * S5_N), F32)],
        compiler_params=_cparams(("arbitrary",)),
    )(proj, bd, cd, lam, dskip, wglu, bglu)


def s5_bwd(proj, ucol, xs, bd, cd, lam, dskip, wglu, bglu, dya, *, name="s5_bwd"):
    s = proj.shape[0]
    tc = _pick(s, (S5_TC, 128, 64, 8))
    nch = s // tc
    per = tc // SUB

    def body(u_ref, xs_ref, xp_ref, bd_ref, cd_ref, lam_ref, d_ref, w_ref, b_ref, dya_ref,
             du_ref, adj_ref, g_ref, dpre_ref, dy_ref, dlam_ref, dd_ref, db_ref, buf, carry, lacc):
        i = pl.program_id(0)
        first = i == 0

        @pl.when(first)
        def _():
            carry[...] = jnp.zeros_like(carry)
            lacc[...] = jnp.zeros_like(lacc)

        u, x, dya_ = u_ref[...], xs_ref[...], dya_ref[...]
        y = _dot(x, cd_ref[...]) + d_ref[...] * u
        g, gelu_vjp = jax.vjp(_gelu, y)
        sg = _sigmoid(_dot(g, w_ref[...]) + b_ref[...])
        dpre = dya_ * g * sg * (1.0 - sg)
        (dy,) = gelu_vjp(dya_ * sg + _dot(dpre, w_ref[...], 1, 1))
        g_ref[...] = g.astype(BF16)
        dpre_ref[...] = dpre.astype(BF16)
        dy_ref[...] = dy.astype(BF16)
        db_part = jnp.sum(dpre, axis=0, keepdims=True)
        dd_part = jnp.sum(dy * u, axis=0, keepdims=True)
        buf[...] = _dot(dy, cd_ref[...], 1, 1)
        _s5_scan(buf, lam_ref[...], carry, True)
        a = buf[...]
        adj_ref[...] = a.astype(BF16)
        du_ref[...] = (dy * d_ref[...] + _dot(a, bd_ref[...], 1, 1)).astype(BF16)
        before = jnp.where(i == nch - 1, 0.0, xp_ref[SUB - 1 : SUB, :])
        ridx = lax.broadcasted_iota(jnp.int32, (tc, 2 * S5_N), 0)
        xprev = jnp.where(ridx == 0, before, pltpu.roll(x, 1, 0))
        ar, ai, xr, xi = a[:, :S5_N], a[:, S5_N:], xprev[:, :S5_N], xprev[:, S5_N:]
        lacc[:, :S5_N] += jnp.sum((ar * xr + ai * xi).reshape(per, SUB, S5_N), axis=0)
        lacc[:, S5_N:] += jnp.sum((ai * xr - ar * xi).reshape(per, SUB, S5_N), axis=0)
        _accum(db_ref, db_part, first)
        _accum(dd_ref, dd_part, first)

        @pl.when(i == nch - 1)
        def _():
            dlam_ref[...] = jnp.sum(lacc[...], axis=0, keepdims=True)

    rev = lambda i: nch - 1 - i
    full = lambda shp: pl.BlockSpec(shp, lambda i: (0, 0))
    rows = lambda w: pl.BlockSpec((tc, w), lambda i: (rev(i), 0))
    hb = jax.ShapeDtypeStruct((s, MIX_W), BF16)
    return pl.pallas_call(
        body,
        name=name,
        grid=(nch,),
        in_specs=[pl.BlockSpec((tc, MIX_W), lambda i: (rev(i), ucol // MIX_W)), rows(2 * S5_N),
                  pl.BlockSpec((SUB, 2 * S5_N), lambda i: (jnp.maximum(rev(i) * per - 1, 0), 0)),
                  full((MIX_W, 2 * S5_N)), full((2 * S5_N, MIX_W)), full((1, 2 * S5_N)), full((1, MIX_W)),
                  full((MIX_W, MIX_W)), full((1, MIX_W)), rows(MIX_W)],
        out_specs=[rows(MIX_W), rows(2 * S5_N), rows(MIX_W), rows(MIX_W), rows(MIX_W),
                   full((1, 2 * S5_N)), full((1, MIX_W)), full((1, MIX_W))],
        out_shape=[hb, jax.ShapeDtypeStruct((s, 2 * S5_N), BF16), hb, hb, hb,
                   jax.ShapeDtypeStruct((1, 2 * S5_N), F32), jax.ShapeDtypeStruct((1, MIX_W), F32),
                   jax.ShapeDtypeStruct((1, MIX_W), F32)],
        scratch_shapes=[pltpu.VMEM((tc, 2 * S5_N), F32), pltpu.VMEM((SUB, 2 * S5_N), F32), pltpu.VMEM((SUB, 2 * S5_N), F32)],
        compiler_params=_cparams(("arbitrary",)),
    )(proj, xs, xs, bd, cd, lam, dskip, wglu, bglu, dya)


def _bcast16(a):
    return jnp.broadcast_to(a.reshape(S5_N, 1), (S5_N, S5_C))


def _blockdiag(blocks):
    g, r, c = blocks.shape
    eye = jnp.eye(g, dtype=blocks.dtype)
    return (eye[:, None, :, None] * blocks[:, :, None, :]).reshape(g * r, g * c)


def _blockdiag_extract(dense, r, c):
    g = dense.shape[0] // r
    return jnp.einsum("grgc->grc", dense.reshape(g, r, g, c))


def s5_prepare(a_re, a_im, log_step, b_re, b_im, c_re, c_im):
    disc_in = (_bcast16(a_re), _bcast16(a_im), _bcast16(jnp.broadcast_to(log_step[:, None], (S5_G, S5_P))),
               b_re.reshape(S5_N, S5_C), b_im.reshape(S5_N, S5_C))
    lr, li, bbr, bbi = s5_disc_fwd(*disc_in)
    lam = jnp.concatenate([lr[:, 0], li[:, 0]]).reshape(1, 2 * S5_N)
    to_blocks = lambda t: _blockdiag(t.reshape(S5_G, S5_P, S5_C).transpose(0, 2, 1))
    bd = jnp.concatenate([to_blocks(bbr), to_blocks(bbi)], axis=1).astype(BF16)
    cd = jnp.concatenate([_blockdiag(c_re.transpose(0, 2, 1)), -_blockdiag(c_im.transpose(0, 2, 1))], axis=0).astype(BF16)
    return disc_in, lam, bd, cd


def s5_param_grads(disc_in, dlam, dbd, dcd):
    first_col = lambda v: jnp.pad(v.reshape(S5_N, 1), ((0, 0), (0, S5_C - 1)))
    from_blocks = lambda t: _blockdiag_extract(t, S5_C, S5_P).transpose(0, 2, 1).reshape(S5_N, S5_C)
    cts = (first_col(dlam[0, :S5_N]), first_col(dlam[0, S5_N:]), from_blocks(dbd[:, :S5_N]), from_blocks(dbd[:, S5_N:]))
    dar, dai, dls, dbr, dbi = s5_disc_bwd(*disc_in, cts)
    dcr = _blockdiag_extract(dcd[:S5_N], S5_P, S5_C).transpose(0, 2, 1)
    dci = -_blockdiag_extract(dcd[S5_N:], S5_P, S5_C).transpose(0, 2, 1)
    return (dar.reshape(S5_G, S5_P), dai.reshape(S5_G, S5_P), dls.reshape(S5_G, S5_P).sum(axis=1),
            dbr.reshape(S5_G, S5_P, S5_C), dbi.reshape(S5_G, S5_P, S5_C), dcr, dci)


N_DEV = 8
MESH_ID = pl.DeviceIdType.MESH
ANY = pl.BlockSpec(memory_space=pl.ANY)


def _me():
    return lax.axis_index("x"), lax.axis_index("y"), lax.axis_index("c")


def all_gather(blocks, *, by_core=False, name="all_gather"):
    na = len(blocks)
    shapes = [b.shape[1:] if by_core else b.shape for b in blocks]

    def body(*refs):
        ins, outs = refs[:na], refs[na : 2 * na]
        send_sems, recv_sems = refs[2 * na :]
        x, y, c = _me()
        me, sibling = (x, y, c), (x, y, 1 - c)
        xn, yn, diag = (1 - x, y, c), (x, 1 - y, c), (1 - x, 1 - y, c)
        slot = lambda p: 4 * p[0] + 2 * p[1] + p[2]

        def copy(a, k, block, to, half=None, src=None):
            dst = outs[a].at[slot(block)]
            if half is not None:
                rows = shapes[a][0] // 2
                dst = dst.at[pl.ds(half * rows, rows)]
            return pltpu.make_async_remote_copy(src_ref=dst if src is None else src, dst_ref=dst, send_sem=send_sems.at[a * 9 + k],
                                                recv_sem=recv_sems.at[a * 9 + k], device_id=to, device_id_type=MESH_ID)

        mine = [ins[a].at[c] if by_core else ins[a] for a in range(na)]
        sends = []

        def go(cp):
            cp.start()
            sends.append(cp)

        for a in range(na):
            go(copy(a, 0, me, sibling, src=mine[a]))
            go(copy(a, 1, me, xn, src=mine[a]))
            go(copy(a, 2, me, yn, src=mine[a]))
        for a in range(na):
            copy(a, 1, xn, me).wait_recv()
            go(copy(a, 3, xn, yn, half=0))
            go(copy(a, 5, xn, sibling))
            copy(a, 2, yn, me).wait_recv()
            go(copy(a, 4, yn, xn, half=1))
            go(copy(a, 6, yn, sibling))
        for a in range(na):
            copy(a, 3, diag, me, half=0).wait_recv()
            go(copy(a, 7, diag, sibling, half=0))
            copy(a, 4, diag, me, half=1).wait_recv()
            go(copy(a, 8, diag, sibling, half=1))
        for a in range(na):
            copy(a, 0, sibling, me).wait_recv()
            copy(a, 5, (1 - x, y, 1 - c), me).wait_recv()
            copy(a, 6, (x, 1 - y, 1 - c), me).wait_recv()
            copy(a, 7, (1 - x, 1 - y, 1 - c), me, half=0).wait_recv()
            copy(a, 8, (1 - x, 1 - y, 1 - c), me, half=1).wait_recv()
        for cp in sends:
            cp.wait_send()

    assert all(shp[0] % (64 // b.dtype.itemsize) == 0 for shp, b in zip(shapes, blocks)), shapes
    outs = pl.pallas_call(
        body,
        name=name,
        in_specs=[ANY] * na,
        out_specs=[ANY] * na,
        out_shape=[jax.ShapeDtypeStruct((N_DEV, *shp), b.dtype) for shp, b in zip(shapes, blocks)],
        scratch_shapes=[pltpu.SemaphoreType.DMA((na * 9,)), pltpu.SemaphoreType.DMA((na * 9,))],
    )(*blocks)
    x, y, c = _me()
    own = [lax.dynamic_index_in_dim(b, c, 0, keepdims=False) if by_core else b for b in blocks]
    return [lax.dynamic_update_index_in_dim(o, b, 4 * x + 2 * y + c, 0) for o, b in zip(outs, own)]


D2D_PIECES = 8


def _slabs(rows):
    n = D2D_PIECES if rows % (16 * D2D_PIECES) == 0 else 1
    return [pl.ds(i * (rows // n), rows // n) for i in range(n)]


def sibling_swap(arrs, *, name="sibling_swap"):
    na = len(arrs)
    pieces = [[(j, sl) for j in range(a.shape[0]) for sl in (_slabs(a.shape[1]) if a.shape[0] == 1 else _halves(a.shape[1]))]
              for a in arrs]
    base = np.cumsum([0] + [len(p) for p in pieces])

    def body(*refs):
        ins, outs = refs[:na], refs[na : 2 * na]
        send_sems, recv_sems = refs[2 * na :]
        x, y, c = _me()
        sends, recvs = [], []
        for a in range(na):
            for i, (j, sl) in enumerate(pieces[a]):
                k = int(base[a]) + i
                cp = pltpu.make_async_remote_copy(src_ref=ins[a].at[j, sl], dst_ref=outs[a].at[j, sl], send_sem=send_sems.at[k],
                                                  recv_sem=recv_sems.at[k], device_id=(x, y, 1 - c), device_id_type=MESH_ID)
                cp.start()
                sends.append(cp)
        for cp in sends:
            cp.wait_recv()
        for cp in sends:
            cp.wait_send()

    nsem = int(base[-1])
    return pl.pallas_call(
        body,
        name=name,
        in_specs=[ANY] * na,
        out_specs=[ANY] * na,
        out_shape=[jax.ShapeDtypeStruct(a.shape, a.dtype) for a in arrs],
        scratch_shapes=[pltpu.SemaphoreType.DMA((nsem,)), pltpu.SemaphoreType.DMA((nsem,))],
    )(*arrs)


def _halves(rows):
    return [pl.ds(0, rows // 2), pl.ds(rows // 2, rows // 2)] if rows % 32 == 0 else [pl.ds(0, rows)]


def chip_exchange(arrs, *, name="chip_exchange"):
    na = len(arrs)

    def body(*refs):
        ins, outs = refs[:na], refs[na : 2 * na]
        send_sems, recv_sems = refs[2 * na :]
        x, y, c = _me()
        my = 2 * x + y
        peers = []
        for k in range(1, N_CHIPS):
            px, py = (x + ((k >> 1) & 1)) % 2, (y + (k & 1)) % 2
            peers.append(((px, py, c), 2 * px + py))
        sends = []
        for a in range(na):
            for k, (peer, pidx) in enumerate(peers):
                cp = pltpu.make_async_remote_copy(src_ref=ins[a].at[pidx], dst_ref=outs[a].at[my], send_sem=send_sems.at[a * 3 + k],
                                                  recv_sem=recv_sems.at[a * 3 + k], device_id=peer, device_id_type=MESH_ID)
                cp.start()
                sends.append(cp)
        for a in range(na):
            for k, (peer, pidx) in enumerate(peers):
                pltpu.make_async_remote_copy(src_ref=ins[a].at[my], dst_ref=outs[a].at[pidx], send_sem=send_sems.at[a * 3 + k],
                                             recv_sem=recv_sems.at[a * 3 + k], device_id=peer, device_id_type=MESH_ID).wait_recv()
        for cp in sends:
            cp.wait_send()

    outs = pl.pallas_call(
        body,
        name=name,
        in_specs=[ANY] * na,
        out_specs=[ANY] * na,
        out_shape=[jax.ShapeDtypeStruct(a.shape, a.dtype) for a in arrs],
        scratch_shapes=[pltpu.SemaphoreType.DMA((na * 3,)), pltpu.SemaphoreType.DMA((na * 3,))],
    )(*arrs)
    chip = 2 * lax.axis_index("x") + lax.axis_index("y")
    return [lax.dynamic_update_index_in_dim(o, lax.dynamic_index_in_dim(a, chip, 0, keepdims=False), chip, 0)
            for o, a in zip(outs, arrs)]


def add_pair(a, b, *, name="add_pair"):
    n, r, c = a.shape
    tm = _row_tile(r, n * c * 2, 2 << 20)

    def body(a_ref, b_ref, o_ref):
        o_ref[...] = (a_ref[...].astype(F32) + b_ref[...].astype(F32)).astype(o_ref.dtype)

    blk = pl.BlockSpec((n, tm, c), lambda i: (0, i, 0))
    return pl.pallas_call(
        body, name=name, grid=(r // tm,), in_specs=[blk, blk], out_specs=blk, out_shape=jax.ShapeDtypeStruct(a.shape, a.dtype),
        compiler_params=_cparams(("arbitrary",)),
    )(a, b)


def _row_tile(rows, row_bytes, budget):
    best = None
    for t in range(8, rows + 1, 8):
        if rows % t == 0 and t * row_bytes <= budget:
            best = t
    return best or rows


def sum_blocks(a, *, name="sum_blocks"):
    n, r, c = a.shape
    tm = _row_tile(r, n * c * a.dtype.itemsize, 4 << 20)

    def body(a_ref, o_ref):
        acc = a_ref[0].astype(F32)
        for k in range(1, n):
            acc = acc + a_ref[k].astype(F32)
        o_ref[...] = acc

    return pl.pallas_call(
        body,
        name=name,
        grid=(r // tm,),
        in_specs=[pl.BlockSpec((n, tm, c), lambda i: (0, i, 0))],
        out_specs=pl.BlockSpec((tm, c), lambda i: (i, 0)),
        out_shape=jax.ShapeDtypeStruct((r, c), F32),
        compiler_params=_cparams(("arbitrary",)),
    )(a)


ADAM_LR = 0.001
ADAM_B1 = 0.9
ADAM_B2 = 0.999
ADAM_EPS = 1e-08
ADAM_WD = 0.01
ADAM_STEP = 10


def adamw(w, g, m, v, *, name="adamw"):
    r, c = w.shape
    tm = _row_tile(r, c * 4, 1 << 20)

    def body(w_ref, g_ref, m_ref, v_ref, d_ref, nm_ref, nv_ref):
        g_ = g_ref[...]
        m_ = ADAM_B1 * m_ref[...] + (1.0 - ADAM_B1) * g_
        v_ = ADAM_B2 * v_ref[...] + (1.0 - ADAM_B2) * (g_ * g_)
        m_hat = m_ / (1.0 - ADAM_B1**ADAM_STEP)
        v_hat = v_ / (1.0 - ADAM_B2**ADAM_STEP)
        d_ref[...] = -ADAM_LR * (m_hat / (jnp.sqrt(v_hat) + ADAM_EPS) + ADAM_WD * w_ref[...])
        nm_ref[...] = m_
        nv_ref[...] = v_

    blk = pl.BlockSpec((tm, c), lambda i: (i, 0))
    sds = jax.ShapeDtypeStruct((r, c), F32)
    return pl.pallas_call(
        body, name=name, grid=(r // tm,), in_specs=[blk] * 4, out_specs=[blk] * 3, out_shape=[sds] * 3,
        compiler_params=_cparams(("arbitrary",)),
    )(w, g, m, v)


WEIGHTS = ["g_mix", "w_in", "s5_a_re", "s5_a_im", "s5_log_step", "s5_b_re", "s5_b_im", "s5_c_re", "s5_c_im", "s5_d", "w_glu",
           "b_glu", "w_gla_gate", "b_gla_gate", "g_gla_out", "w_branch", "w_out", "g_mem", "g_cross", "w_xq", "w_xkv", "w_xo",
           "g_mlp", "w_up", "w_down", "g_final"]
SHARDED = {"w_in": 1, "w_glu": 0, "w_branch": 2, "w_out": 0, "w_xq": 0, "w_xkv": 1, "w_xo": 0, "w_up": 1, "w_down": 0}
DEPTH = 2
N_CHIPS = 4
D_IN = 9744
C_U, C_QG, C_KG, C_VG, C_LR, C_RG, C_QA, C_KA, C_VA, C_GATE = 0, 512, 768, 1024, 1536, 1552, 2064, 3600, 5136, 6672
A_GATE, A_U, A_QG, A_KG, A_VG, A_RG, A_Q0, A_LR, A_W = 0, 3072, 3584, 3840, 4096, 4608, 5120, 6656, 7168
GLA_COLS = (A_QG, A_KG, A_VG, A_LR, A_RG)


def _split_w_in(w):
    att = lambda g: [w[:, c + MIX_W * g : c + MIX_W * (g + 1)] for c in (C_QA, C_KA, C_VA)]
    a = jnp.concatenate([w[:, C_GATE:D_IN], w[:, C_U:C_QG], w[:, C_QG:C_KG], w[:, C_KG:C_VG], w[:, C_VG:C_LR], w[:, C_RG:C_QA],
                         *att(0), w[:, C_LR:C_RG], jnp.zeros((w.shape[0], A_W - A_LR - 16), w.dtype)], axis=1)
    return a, jnp.concatenate(att(1), axis=1), jnp.concatenate(att(2), axis=1)


def _join_w_in(a, b, c):
    att = lambda k: [a[:, A_Q0 + MIX_W * k : A_Q0 + MIX_W * (k + 1)], b[:, MIX_W * k : MIX_W * (k + 1)], c[:, MIX_W * k : MIX_W * (k + 1)]]
    return jnp.concatenate([a[:, A_U:A_QG], a[:, A_QG:A_KG], a[:, A_KG:A_VG], a[:, A_VG:A_RG], a[:, A_LR : A_LR + 16], a[:, A_RG:A_Q0],
                            *att(0), *att(1), *att(2), a[:, A_GATE:A_U]], axis=1)


def _by_residue(a, d):
    s = a.shape[0]
    return a if d == 1 else a.reshape(s // d, d, -1).transpose(1, 0, 2).reshape(s, -1)


def _in_order(a, d):
    s = a.shape[0]
    return a if d == 1 else a.reshape(d, s // d, -1).transpose(1, 0, 2).reshape(s, -1)


def _pack(arrs):
    flat = jnp.concatenate([a.reshape(-1) for a in arrs])
    n = flat.shape[0]
    rows = -(-n // (256 * 128)) * 256
    return jnp.pad(flat, (0, rows * 128 - n)).reshape(rows, 128)


def _unpack(packed, like):
    flat, out, o = packed.reshape(-1), [], 0
    for a in like:
        n = math.prod(a.shape)
        out.append(flat[o : o + n].reshape(a.shape))
        o += n
    return out


def _layer_fwd(x0, p, kv):
    h = rmsnorm(x0, p["g_mix"], name="mix_norm")
    hs = [h, _by_residue(h, 4), _by_residue(h, 16)]
    proj = [matmul(hs[g], p["w_in_seg"][g], name="in_proj") for g in range(3)]
    ya, xs = s5_fwd(proj[0], A_U, p["bd"], p["cd"], p["lam"], p["s5_d"], p["w_glu"], p["b_glu"])
    yb, gst = gla_fwd(proj[0], GLA_COLS, p["w_gate"], p["b_gla_gate"], p["g_gla_out"])
    att = [attn_fwd(proj[g], A_Q0 if g == 0 else 0, g) for g in range(3)]
    outs = [_in_order(att[g][0], ATT_GROUPS[g][1]) for g in range(3)]
    lses = [_in_order(att[g][1], ATT_GROUPS[g][1]) for g in range(3)]
    x1, merged, yc = merge_fwd(x0, proj[0], ya, yb, outs, lses, p["w_branch"], p["w_out"])
    x2 = cross_fwd(x1, p["g_cross"], p["w_xq"], p["w_xo"], kv)
    x3 = mlp_fwd(x2, p["g_mlp"], p["w_up"], p["w_down"])
    saved = dict(x0=x0, x1=x1, x2=x2, hs=hs, proj=proj, ya=ya, xs=xs, yb=yb, gst=gst, att=att, outs=outs, lses=lses,
                 merged=merged, yc=yc)
    return x3, saved


def _layer_bwd(dx3, p, kv, memn, sv):
    g = {}
    tn = lambda a, b, **kw: matmul(a, b, ta=True, out_dtype=BF16, name="wgrad", **kw)
    dx2, g["g_mlp"], h3, dup, act = mlp_bwd(sv["x2"], p["g_mlp"], p["w_up"], p["w_down"], dx3)
    g["w_up"], g["w_down"] = tn(h3, dup), tn(act, dx3)
    dx1, g["g_cross"], h2, dq, o, dkv = cross_bwd(sv["x1"], p["g_cross"], p["w_xq"], p["w_xo"], kv, dx2)
    g["w_xq"], g["w_xo"], g["w_xkv"] = tn(h2, dq), tn(o, dx2), tn(memn, dkv)
    dmemn = matmul(dkv, p["w_xkv"], tb=True, name="dmem")
    proj = sv["proj"]
    r = merge_bwd(dx1, proj[0], sv["ya"], sv["yb"], sv["yc"], sv["outs"], sv["lses"], p["w_branch"], p["w_out"])
    dgl, dz, dya, dyb, douts, dlses = r[0], r[1:4], r[4], r[5], r[6:9], r[9:12]
    g["w_branch"] = jnp.stack([tn(y, dz[n]) for n, y in enumerate((sv["ya"], sv["yb"], sv["yc"]))])
    g["w_out"] = tn(sv["merged"], dx1)
    datt = []
    for k in range(3):
        dil = ATT_GROUPS[k][1]
        datt.append(attn_bwd(proj[k], A_Q0 if k == 0 else 0, k, sv["att"][k][0], sv["att"][k][1],
                             _by_residue(douts[k], dil), _by_residue(dlses[k], dil)))
    du, adj, gg, dpre, dy, dlam, g["s5_d"], g["b_glu"] = s5_bwd(proj[0], A_U, sv["xs"], p["bd"], p["cd"], p["lam"], p["s5_d"],
                                                                 p["w_glu"], p["b_glu"], dya)
    dbd = matmul(proj[0], adj, ta=True, a_col=(A_U, MIX_W), name="s5_dbd")
    dcd = matmul(sv["xs"], dy, ta=True, name="s5_dcd")
    g["w_glu"] = tn(gg, dpre)
    (g["s5_a_re"], g["s5_a_im"], g["s5_log_step"], g["s5_b_re"], g["s5_b_im"], g["s5_c_re"],
     g["s5_c_im"]) = s5_param_grads(p["disc_in"], dlam, dbd, dcd)
    dqg, dkg, dvg, dlr, drg, dwg, g["b_gla_gate"], g["g_gla_out"] = gla_bwd(proj[0], GLA_COLS, p["w_gate"], p["b_gla_gate"],
                                                                             p["g_gla_out"], sv["gst"], dyb)
    g["w_gla_gate"] = dwg[:GLA_GATE_RANK]
    s = dx3.shape[0]
    dproj = [jnp.concatenate([dgl, du, dqg, dkg, dvg, drg, *datt[0], dlr, jnp.zeros((s, A_W - A_LR - LR_PAD), BF16)], axis=1),
             jnp.concatenate(datt[1], axis=1), jnp.concatenate(datt[2], axis=1)]
    g["w_in"] = _join_w_in(*[tn(sv["hs"][k], dproj[k]) for k in range(3)])
    dhs = [_in_order(matmul(dproj[k], p["w_in_seg"][k], tb=True, name="in_proj_bwd"), ATT_GROUPS[k][1]) for k in range(3)]
    dx0, g["g_mix"] = rmsnorm_bwd(sv["x0"], p["g_mix"], dhs, dx1, name="mix_norm_bwd")
    return dx0, g, dmemn


GLA_GATE_RANK = 16


def kernel(x, mem, *rest):
    nw = len(WEIGHTS)
    w = dict(zip(WEIGHTS, rest[:nw]))
    target = rest[nw]
    m = dict(zip(WEIGHTS, rest[nw + 1 : 2 * nw + 1]))
    v = dict(zip(WEIGHTS, rest[2 * nw + 1 : 3 * nw + 1]))
    x0, memx, target = x[0], mem[0], target[0]
    chip = 2 * lax.axis_index("x") + lax.axis_index("y")

    big = list(SHARDED)
    gate_pad = jnp.pad(w["w_gla_gate"], ((0, 0), (0, 0), (0, LR_PAD - w["w_gla_gate"].shape[2])))
    flat2 = lambda t: t.reshape(DEPTH, -1, t.shape[-1])
    gathered = all_gather([flat2(w[n].astype(BF16)) for n in big] + [gate_pad], by_core=True, name="gather_weights")
    gathered = [t.reshape(N_CHIPS, DEPTH, *s.shape[1:]) for t, s in zip(gathered, [w[n] for n in big] + [gate_pad])]
    full = {n: [jnp.concatenate([t[j, l] for j in range(N_CHIPS)], axis=SHARDED[n]) for l in range(DEPTH)]
            for n, t in zip(big, gathered)}
    gate_full = [jnp.concatenate([gathered[-1][j, l][:, : w["w_gla_gate"].shape[2]] for j in range(N_CHIPS)], axis=1)
                 for l in range(DEPTH)]

    memn = rmsnorm(memx, w["g_mem"], name="mem_norm")
    params, kvs = [], []
    for l in range(DEPTH):
        p = {n: full[n][l] for n in big if n != "w_in"}
        p["w_in_seg"] = _split_w_in(full["w_in"][l])
        p["w_gate"] = jnp.pad(gate_full[l], ((0, LR_PAD - GLA_GATE_RANK), (0, 0))).astype(BF16)
        for n in ("g_mix", "g_cross", "g_mlp"):
            p[n] = w[n][l]
        for n in ("s5_d", "b_glu", "b_gla_gate", "g_gla_out"):
            p[n] = w[n][l].reshape(1, -1)
        p["disc_in"], p["lam"], p["bd"], p["cd"] = s5_prepare(*[w[n][l] for n in WEIGHTS[2:9]])
        params.append(p)
        kvs.append(matmul(memn, p["w_xkv"], out_dtype=BF16, name="mem_kv"))

    xl, saved = x0, []
    for l in range(DEPTH):
        xl, sv = _layer_fwd(xl, params[l], kvs[l])
        saved.append(sv)
    loss8, dx, dg_final = loss_head(xl, w["g_final"], target)
    grads, dmem = [None] * DEPTH, []
    for l in reversed(range(DEPTH)):
        dx, grads[l], dm = _layer_bwd(dx, params[l], kvs[l], memn, saved[l])
        dmem.append(dm)
    _, dg_mem = rmsnorm_bwd(memx, w["g_mem"], dmem, jnp.zeros_like(memx), name="mem_norm_bwd")

    core = lax.axis_index("c")

    def halves(n):
        ax = SHARDED[n]
        keep, send = [], []
        for j in range(N_CHIPS):
            p0, p1 = [jnp.split(grads[l][n], N_CHIPS, axis=ax)[j] for l in range(DEPTH)]
            keep.append(jnp.where(core == 0, p0, p1))
            send.append(jnp.where(core == 0, p1, p0))
        flat = lambda ps: jnp.stack(ps).reshape(N_CHIPS, -1, ps[0].shape[-1])
        return flat(keep), flat(send)

    kept, sent = zip(*[halves(n) for n in big])
    theirs = sibling_swap(list(sent), name="swap_layers")
    chip_sums = [add_pair(a, b, name="add_cores") for a, b in zip(kept, theirs)]
    landed = chip_exchange(chip_sums, name="exchange_grads")
    reduced = [sum_blocks(t, name="sum_grads") for t in landed]
    other = sibling_swap([t[None] for t in reduced], name="pair_layers")
    out_g = {}
    for n, mine, theirs in zip(big, reduced, other):
        both = jnp.where(core == 0, jnp.stack([mine, theirs[0]]), jnp.stack([theirs[0], mine]))
        out_g[n] = both.reshape(w[n].shape)

    small = [n for n in WEIGHTS if n not in SHARDED]
    local_small = []
    for n in small:
        if n == "g_mem":
            local_small.append(dg_mem.reshape(w[n].shape))
        elif n == "g_final":
            local_small.append(dg_final.reshape(w[n].shape))
        elif n == "w_gla_gate":
            local_small.append(jnp.stack([grads[l][n] for l in range(DEPTH)]))
        else:
            local_small.append(jnp.stack([grads[l][n].reshape(w[n].shape[1:]) for l in range(DEPTH)]))
    packed = _pack(local_small + [loss8[0, :1]])
    (every,) = all_gather([packed], name="gather_small")
    summed = _unpack(sum_blocks(every, name="sum_small"), local_small + [loss8[0, :1]])
    loss = summed[-1].reshape(())
    for n, t in zip(small, summed[:-1]):
        if n == "w_gla_gate":
            t = lax.dynamic_slice_in_dim(t, chip * w[n].shape[2], w[n].shape[2], axis=2)
        out_g[n] = t

    delta, new_m, new_v = {}, {}, {}
    for n in big:
        c = w[n].shape[-1]
        r = [t.reshape(w[n].shape) for t in adamw(w[n].reshape(-1, c), out_g[n].reshape(-1, c), m[n].reshape(-1, c),
                                                  v[n].reshape(-1, c), name="adamw")]
        delta[n], new_m[n], new_v[n] = r
    like = [w[n] for n in small]
    r = adamw(_pack(like), _pack([out_g[n] for n in small]), _pack([m[n] for n in small]), _pack([v[n] for n in small]),
              name="adamw_small")
    for d, t in zip((delta, new_m, new_v), r):
        d.update(zip(small, _unpack(t, like)))
    return (loss, dx[None], *[out_g[n] for n in WEIGHTS], *[delta[n] for n in WEIGHTS], *[new_m[n] for n in WEIGHTS],
            *[new_v[n] for n in WEIGHTS])
```

```python
import functools
import math

import jax
import jax.numpy as jnp
import numpy as np
from jax import lax
from jax.experimental import pallas as pl
from jax.experimental.pallas import tpu as pltpu

F32 = jnp.float32
BF16 = jnp.bfloat16

VMEM_LIMIT_BYTES = 56 * 1024 * 1024
MATMUL_VMEM_BYTES = 36 * 1024 * 1024


def _cparams(sem):
    return pltpu.CompilerParams(dimension_semantics=sem, vmem_limit_bytes=VMEM_LIMIT_BYTES)


def _dot(a, b, ca=1, cb=0):
    return lax.dot_general(a.astype(BF16), b.astype(BF16), (((ca,), (cb,)), ((), ())), preferred_element_type=F32)


def _pick(n, prefs):
    for p in prefs:
        if n % p == 0:
            return p
    return n


def matmul(a, b, *, ta=False, tb=False, out_dtype=F32, name="mm", a_col=None, b_col=None):
    a_off, a_w = a_col if a_col is not None else (0, a.shape[1])
    b_off, b_w = b_col if b_col is not None else (0, b.shape[1])
    if ta:
        kk, m = a.shape[0], a_w
    else:
        m, kk = a.shape[0], a_w
    if tb:
        n, kb = b.shape[0], b_w
    else:
        kb, n = b.shape[0], b_w
    assert kk == kb, (a.shape, b.shape, ta, tb)
    tm = _pick(m, (512, 256, 128))
    tn = _pick(n, (1024, 512, 256, 128))
    sa, sb, so = a.dtype.itemsize, b.dtype.itemsize, jnp.dtype(out_dtype).itemsize
    fits = lambda t: 2 * (tm * t * sa + t * tn * sb) + tm * tn * (4 + 2 * so) <= MATMUL_VMEM_BYTES
    tk = max([t for t in range(128, kk + 1, 128) if kk % t == 0 and fits(t)] or [_pick(kk, (128,))])
    nk = kk // tk
    a_bytes, b_bytes = m * kk * sa, kk * n * sb
    n_outer = nk == 1 and b_bytes + a_bytes * (n // tn) < a_bytes + b_bytes * (m // tm)
    ij = (lambda g0, g1: (g1, g0)) if n_outer else (lambda g0, g1: (g0, g1))
    if ta:
        assert a_off % tm == 0
        a_spec = pl.BlockSpec((tk, tm), lambda g0, g1, k, o=a_off // tm: (k, ij(g0, g1)[0] + o))
    else:
        assert a_off % tk == 0
        a_spec = pl.BlockSpec((tm, tk), lambda g0, g1, k, o=a_off // tk: (ij(g0, g1)[0], k + o))
    if tb:
        assert b_off % tk == 0
        b_spec = pl.BlockSpec((tn, tk), lambda g0, g1, k, o=b_off // tk: (ij(g0, g1)[1], k + o))
    else:
        assert b_off % tn == 0
        b_spec = pl.BlockSpec((tk, tn), lambda g0, g1, k, o=b_off // tn: (k, ij(g0, g1)[1] + o))

    def body(a_ref, b_ref, o_ref, *acc):
        k = pl.program_id(2)
        p = _dot(a_ref[...], b_ref[...], 0 if ta else 1, 1 if tb else 0)
        if nk == 1:
            o_ref[...] = p.astype(o_ref.dtype)
            return
        (acc_ref,) = acc

        @pl.when(k == 0)
        def _():
            acc_ref[...] = p

        @pl.when(jnp.logical_and(k > 0, k < nk - 1))
        def _():
            acc_ref[...] += p

        @pl.when(k == nk - 1)
        def _():
            o_ref[...] = (acc_ref[...] + p).astype(o_ref.dtype)

    return pl.pallas_call(
        body,
        name=name,
        grid=(n // tn, m // tm, nk) if n_outer else (m // tm, n // tn, nk),
        in_specs=[a_spec, b_spec],
        out_specs=pl.BlockSpec((tm, tn), lambda g0, g1, k: ij(g0, g1)),
        out_shape=jax.ShapeDtypeStruct((m, n), out_dtype),
        scratch_shapes=[pltpu.VMEM((tm, tn), F32)] if nk > 1 else [],
        compiler_params=_cparams(("parallel", "parallel", "arbitrary")),
    )(a, b)


RMS_EPS = 1e-6
ROW_TILE = 512


def _rms_fwd(x, g):
    r = lax.rsqrt(jnp.mean(x * x, axis=-1, keepdims=True) + RMS_EPS)
    return x * r * g


def _rms_bwd(x, g, dh):
    r = lax.rsqrt(jnp.mean(x * x, axis=-1, keepdims=True) + RMS_EPS)
    xh = x * r
    dg = jnp.sum(dh * xh, axis=0, keepdims=True)
    dxh = dh * g
    dx = r * (dxh - xh * jnp.mean(dxh * xh, axis=-1, keepdims=True))
    return dx, dg


def _accum(ref, val, first):
    @pl.when(first)
    def _():
        ref[...] = val

    @pl.when(jnp.logical_not(first))
    def _():
        ref[...] += val


def rmsnorm(x, g, *, out_dtype=BF16, name="rmsnorm"):
    s, d = x.shape
    tm = _pick(s, (ROW_TILE, 256, 128, 8))

    def body(x_ref, g_ref, o_ref):
        o_ref[...] = _rms_fwd(x_ref[...], g_ref[...]).astype(o_ref.dtype)

    return pl.pallas_call(
        body,
        name=name,
        grid=(s // tm,),
        in_specs=[pl.BlockSpec((tm, d), lambda i: (i, 0)), pl.BlockSpec((1, d), lambda i: (0, 0))],
        out_specs=pl.BlockSpec((tm, d), lambda i: (i, 0)),
        out_shape=jax.ShapeDtypeStruct((s, d), out_dtype),
        compiler_params=_cparams(("arbitrary",)),
    )(x, g.reshape(1, d))


def rmsnorm_bwd(x, g, dhs, dres, *, name="rmsnorm_bwd"):
    s, d = x.shape
    tm = _pick(s, (ROW_TILE, 256, 128, 8))
    n = len(dhs)

    def body(x_ref, g_ref, *refs):
        dh_refs, dres_ref, dx_ref, dg_ref = refs[:n], refs[n], refs[n + 1], refs[n + 2]
        dh = dh_refs[0][...].astype(F32)
        for r in dh_refs[1:]:
            dh = dh + r[...].astype(F32)
        dx, dg = _rms_bwd(x_ref[...], g_ref[...], dh)
        dx_ref[...] = dres_ref[...] + dx
        _accum(dg_ref, dg, pl.program_id(0) == 0)

    row = pl.BlockSpec((tm, d), lambda i: (i, 0))
    vec = pl.BlockSpec((1, d), lambda i: (0, 0))
    return pl.pallas_call(
        body,
        name=name,
        grid=(s // tm,),
        in_specs=[row, vec] + [row] * n + [row],
        out_specs=[row, vec],
        out_shape=[jax.ShapeDtypeStruct((s, d), F32), jax.ShapeDtypeStruct((1, d), F32)],
        compiler_params=_cparams(("arbitrary",)),
    )(x, g.reshape(1, d), *dhs, dres)


def loss_head(x, g, target, *, name="loss_head"):
    s, d = x.shape
    tm = _pick(s, (ROW_TILE, 256, 128, 8))

    def body(x_ref, g_ref, t_ref, l_ref, dx_ref, dg_ref):
        x_, g_ = x_ref[...], g_ref[...]
        e = _rms_fwd(x_, g_) - t_ref[...]
        part = 0.5 * jnp.sum(jnp.sum(e * e, axis=-1, keepdims=True), axis=0, keepdims=True) / d
        dx, dg = _rms_bwd(x_, g_, e * (1.0 / d))
        dx_ref[...] = dx
        first = pl.program_id(0) == 0
        _accum(dg_ref, dg, first)
        _accum(l_ref, jnp.broadcast_to(part, (8, 128)), first)

    row = pl.BlockSpec((tm, d), lambda i: (i, 0))
    vec = pl.BlockSpec((1, d), lambda i: (0, 0))
    return pl.pallas_call(
        body,
        name=name,
        grid=(s // tm,),
        in_specs=[row, vec, row],
        out_specs=[pl.BlockSpec((8, 128), lambda i: (0, 0)), row, vec],
        out_shape=[jax.ShapeDtypeStruct((8, 128), F32), jax.ShapeDtypeStruct((s, d), F32), jax.ShapeDtypeStruct((1, d), F32)],
        compiler_params=_cparams(("arbitrary",)),
    )(x, g.reshape(1, d), target)


FF_TILE = 1024


def mlp_fwd(x, g, w_up, w_down, *, name="mlp_fwd"):
    s, d = x.shape
    ff = w_up.shape[1]
    tm, tf = _pick(s, (ROW_TILE, 256, 128)), _pick(ff, (FF_TILE, 512, 256, 128))
    nf = ff // tf

    def body(x_ref, g_ref, wu_ref, wd_ref, o_ref, h_ref, acc_ref):
        f = pl.program_id(1)

        @pl.when(f == 0)
        def _():
            h_ref[...] = _rms_fwd(x_ref[...], g_ref[...]).astype(BF16)
            acc_ref[...] = x_ref[...]

        up = jnp.maximum(_dot(h_ref[...], wu_ref[...]), 0.0)
        acc_ref[...] += _dot(up * up, wd_ref[...])

        @pl.when(f == nf - 1)
        def _():
            o_ref[...] = acc_ref[...]

    return pl.pallas_call(
        body,
        name=name,
        grid=(s // tm, nf),
        in_specs=[
            pl.BlockSpec((tm, d), lambda i, f: (i, 0)),
            pl.BlockSpec((1, d), lambda i, f: (0, 0)),
            pl.BlockSpec((d, tf), lambda i, f: (0, f)),
            pl.BlockSpec((tf, d), lambda i, f: (f, 0)),
        ],
        out_specs=pl.BlockSpec((tm, d), lambda i, f: (i, 0)),
        out_shape=jax.ShapeDtypeStruct((s, d), F32),
        scratch_shapes=[pltpu.VMEM((tm, d), BF16), pltpu.VMEM((tm, d), F32)],
        compiler_params=_cparams(("arbitrary", "arbitrary")),
    )(x, g.reshape(1, d), w_up, w_down)


def mlp_bwd(x, g, w_up, w_down, dy, *, name="mlp_bwd"):
    s, d = x.shape
    ff = w_up.shape[1]
    tm, tf = _pick(s, (ROW_TILE, 256, 128)), _pick(ff, (FF_TILE, 512, 256, 128))
    nf = ff // tf

    def body(x_ref, g_ref, wu_ref, wd_ref, dy_ref, dx_ref, dg_ref, h_ref, dup_ref, act_ref, dyb_ref, acc_ref):
        i, f = pl.program_id(0), pl.program_id(1)

        @pl.when(f == 0)
        def _():
            h_ref[...] = _rms_fwd(x_ref[...], g_ref[...]).astype(BF16)
            dyb_ref[...] = dy_ref[...].astype(BF16)
            acc_ref[...] = jnp.zeros_like(acc_ref)

        up = jnp.maximum(_dot(h_ref[...], wu_ref[...]), 0.0)
        dact = _dot(dyb_ref[...], wd_ref[...], 1, 1)
        dup = (2.0 * up * dact).astype(BF16)
        dup_ref[...] = dup
        act_ref[...] = (up * up).astype(BF16)
        acc_ref[...] += _dot(dup, wu_ref[...], 1, 1)

        @pl.when(f == nf - 1)
        def _():
            dx, dg = _rms_bwd(x_ref[...], g_ref[...], acc_ref[...])
            dx_ref[...] = dy_ref[...] + dx
            _accum(dg_ref, dg, i == 0)

    row = pl.BlockSpec((tm, d), lambda i, f: (i, 0))
    vec = pl.BlockSpec((1, d), lambda i, f: (0, 0))
    wide = pl.BlockSpec((tm, tf), lambda i, f: (i, f))
    return pl.pallas_call(
        body,
        name=name,
        grid=(s // tm, nf),
        in_specs=[row, vec, pl.BlockSpec((d, tf), lambda i, f: (0, f)), pl.BlockSpec((tf, d), lambda i, f: (f, 0)), row],
        out_specs=[row, vec, row, wide, wide],
        out_shape=[
            jax.ShapeDtypeStruct((s, d), F32),
            jax.ShapeDtypeStruct((1, d), F32),
            jax.ShapeDtypeStruct((s, d), BF16),
            jax.ShapeDtypeStruct((s, ff), BF16),
            jax.ShapeDtypeStruct((s, ff), BF16),
        ],
        scratch_shapes=[pltpu.VMEM((tm, d), BF16), pltpu.VMEM((tm, d), F32)],
        compiler_params=_cparams(("arbitrary", "arbitrary")),
    )(x, g.reshape(1, d), w_up, w_down, dy)


X_HEADS = 4


def _softmax_rows(s):
    m = jnp.max(s, axis=-1, keepdims=True)
    e = jnp.exp(s - m)
    return e / jnp.sum(e, axis=-1, keepdims=True)


def cross_fwd(x, g, wq, wo, kv, *, name="cross_fwd"):
    s, d = x.shape
    ml = kv.shape[0]
    dh = d // X_HEADS
    tm = _pick(s, (ROW_TILE, 256, 128))
    scale = dh**-0.5

    def body(x_ref, g_ref, wq_ref, wo_ref, kv_ref, o_ref):
        x_ = x_ref[...]
        q = _dot(_rms_fwd(x_, g_ref[...]), wq_ref[...])
        outs = []
        for hd in range(X_HEADS):
            kh = kv_ref[:, hd * dh : (hd + 1) * dh]
            vh = kv_ref[:, d + hd * dh : d + (hd + 1) * dh]
            p = _softmax_rows(_dot(q[:, hd * dh : (hd + 1) * dh], kh, 1, 1) * scale)
            outs.append(_dot(p, vh))
        o_ref[...] = x_ + _dot(jnp.concatenate(outs, axis=-1), wo_ref[...])

    row = pl.BlockSpec((tm, d), lambda i: (i, 0))
    full = lambda shp: pl.BlockSpec(shp, lambda i: (0, 0))
    return pl.pallas_call(
        body,
        name=name,
        grid=(s // tm,),
        in_specs=[row, full((1, d)), full((d, d)), full((d, d)), full((ml, 2 * d))],
        out_specs=row,
        out_shape=jax.ShapeDtypeStruct((s, d), F32),
        compiler_params=_cparams(("arbitrary",)),
    )(x, g.reshape(1, d), wq, wo, kv)


def cross_bwd(x, g, wq, wo, kv, dy, *, name="cross_bwd"):
    s, d = x.shape
    ml = kv.shape[0]
    dh = d // X_HEADS
    tm = _pick(s, (ROW_TILE, 256, 128))
    scale = dh**-0.5

    def body(x_ref, g_ref, wq_ref, wo_ref, kv_ref, dy_ref, dx_ref, dg_ref, h_ref, dq_ref, o_ref, dkv_ref):
        first = pl.program_id(0) == 0
        x_, g_ = x_ref[...], g_ref[...]
        h = _rms_fwd(x_, g_).astype(BF16)
        h_ref[...] = h
        q = _dot(h, wq_ref[...])
        dy_ = dy_ref[...]
        do = _dot(dy_, wo_ref[...], 1, 1)
        outs, dqs, dks, dvs = [], [], [], []
        for hd in range(X_HEADS):
            sl = slice(hd * dh, (hd + 1) * dh)
            kh = kv_ref[:, sl]
            vh = kv_ref[:, d + hd * dh : d + (hd + 1) * dh]
            qh = q[:, sl]
            p = _softmax_rows(_dot(qh, kh, 1, 1) * scale)
            outs.append(_dot(p, vh))
            doh = do[:, sl]
            dp = _dot(doh, vh, 1, 1)
            ds = p * (dp - jnp.sum(dp * p, axis=-1, keepdims=True)) * scale
            dqs.append(_dot(ds, kh))
            dks.append(_dot(ds, qh, 0, 0))
            dvs.append(_dot(p, doh, 0, 0))
        o_ref[...] = jnp.concatenate(outs, axis=-1).astype(BF16)
        dq = jnp.concatenate(dqs, axis=-1).astype(BF16)
        dq_ref[...] = dq
        dx, dg = _rms_bwd(x_, g_, _dot(dq, wq_ref[...], 1, 1))
        dx_ref[...] = dy_ + dx
        _accum(dkv_ref, jnp.concatenate(dks + dvs, axis=-1), first)
        _accum(dg_ref, dg, first)

    row = pl.BlockSpec((tm, d), lambda i: (i, 0))
    full = lambda shp: pl.BlockSpec(shp, lambda i: (0, 0))
    rowb = jax.ShapeDtypeStruct((s, d), BF16)
    return pl.pallas_call(
        body,
        name=name,
        grid=(s // tm,),
        in_specs=[row, full((1, d)), full((d, d)), full((d, d)), full((ml, 2 * d)), row],
        out_specs=[row, full((1, d)), row, row, row, full((ml, 2 * d))],
        out_shape=[jax.ShapeDtypeStruct((s, d), F32), jax.ShapeDtypeStruct((1, d), F32), rowb, rowb, rowb,
                   jax.ShapeDtypeStruct((ml, 2 * d), F32)],
        compiler_params=_cparams(("arbitrary",)),
    )(x, g.reshape(1, d), wq, wo, kv, dy)


N_BRANCH = 3
MIX_W = 512
ATT_DH = 128
ATT_HG = 4
MERGE_TILE = 256


def _group_weights(l0, l1, l2):
    m = jnp.maximum(jnp.maximum(l0, l1), l2)
    e = [jnp.exp(l0 - m), jnp.exp(l1 - m), jnp.exp(l2 - m)]
    inv = 1.0 / (e[0] + e[1] + e[2])
    return [t * inv for t in e]


def _sigmoid(x):
    return 0.5 * (jnp.tanh(0.5 * x) + 1.0)


def merge_fwd(x, proj, ya, yb, outs, lses, wb, wout, *, name="merge_fwd"):
    s, d = x.shape
    tm = _pick(s, (MERGE_TILE, 128))

    def body(x_ref, g0, g1, g2, ya_ref, yb_ref, o0, o1, o2, l0, l1, l2, wb_ref, wo_ref, x1_ref, mg_ref, yc_ref):
        w = _group_weights(l0[...], l1[...], l2[...])
        o = (o0[...], o1[...], o2[...])
        heads = []
        for hd in range(ATT_HG):
            sl = slice(hd * ATT_DH, (hd + 1) * ATT_DH)
            heads.append(sum(w[gi][:, hd : hd + 1] * o[gi][:, sl] for gi in range(3)))
        yc = jnp.concatenate(heads, axis=-1)
        yc_ref[...] = yc.astype(BF16)
        merged = None
        for n, (y, gl) in enumerate(((ya_ref[...], g0), (yb_ref[...], g1), (yc, g2))):
            t = _sigmoid(gl[...]) * _dot(y, wb_ref[n])
            merged = t if merged is None else merged + t
        mb = merged.astype(BF16)
        mg_ref[...] = mb
        x1_ref[...] = x_ref[...] + _dot(mb, wo_ref[...])

    row = pl.BlockSpec((tm, d), lambda i: (i, 0))
    half = pl.BlockSpec((tm, MIX_W), lambda i: (i, 0))
    perhead = pl.BlockSpec((tm, ATT_HG), lambda i: (i, 0))
    gate = [pl.BlockSpec((tm, d), lambda i, n=n: (i, n)) for n in range(N_BRANCH)]
    return pl.pallas_call(
        body,
        name=name,
        grid=(s // tm,),
        in_specs=[row] + gate + [half] * 5 + [perhead] * 3
        + [pl.BlockSpec((N_BRANCH, MIX_W, d), lambda i: (0, 0, 0)), pl.BlockSpec((d, d), lambda i: (0, 0))],
        out_specs=[row, row, half],
        out_shape=[jax.ShapeDtypeStruct((s, d), F32), jax.ShapeDtypeStruct((s, d), BF16), jax.ShapeDtypeStruct((s, MIX_W), BF16)],
        compiler_params=_cparams(("arbitrary",)),
    )(x, proj, proj, proj, ya, yb, *outs, *lses, wb, wout)


def merge_bwd(dx1, proj, ya, yb, yc, outs, lses, wb, wout, *, name="merge_bwd"):
    s, d = dx1.shape
    tm = _pick(s, (MERGE_TILE, 128))

    def body(dx_ref, g0, g1, g2, ya_ref, yb_ref, yc_ref, o0, o1, o2, l0, l1, l2, wb_ref, wo_ref,
             dgl_ref, dz0, dz1, dz2, dya_ref, dyb_ref, do0, do1, do2, dl0, dl1, dl2):
        dm = _dot(dx_ref[...], wo_ref[...], 1, 1)
        dys = []
        for n, (y, gl, dz_ref) in enumerate(((ya_ref, g0, dz0), (yb_ref, g1, dz1), (yc_ref, g2, dz2))):
            z = _dot(y[...], wb_ref[n])
            sg = _sigmoid(gl[...])
            dz = (dm * sg).astype(BF16)
            dz_ref[...] = dz
            dgl_ref[:, n * d : (n + 1) * d] = (dm * z * sg * (1.0 - sg)).astype(BF16)
            dys.append(_dot(dz, wb_ref[n], 1, 1))
        dya_ref[...] = dys[0]
        dyb_ref[...] = dys[1]
        dyc = dys[2]
        w = _group_weights(l0[...], l1[...], l2[...])
        o = [o0[...], o1[...], o2[...]]
        for hd in range(ATT_HG):
            sl = slice(hd * ATT_DH, (hd + 1) * ATT_DH)
            col = slice(hd, hd + 1)
            wh = [w[gi][:, col] for gi in range(3)]
            for gi, r in enumerate((do0, do1, do2)):
                r[:, sl] = wh[gi] * dyc[:, sl]
            t = [jnp.sum(dyc[:, sl] * o[gi][:, sl], axis=-1, keepdims=True) for gi in range(3)]
            tbar = wh[0] * t[0] + wh[1] * t[1] + wh[2] * t[2]
            for gi, r in enumerate((dl0, dl1, dl2)):
                r[:, col] = wh[gi] * (t[gi] - tbar)

    row = pl.BlockSpec((tm, d), lambda i: (i, 0))
    half = pl.BlockSpec((tm, MIX_W), lambda i: (i, 0))
    perhead = pl.BlockSpec((tm, ATT_HG), lambda i: (i, 0))
    gate = [pl.BlockSpec((tm, d), lambda i, n=n: (i, n)) for n in range(N_BRANCH)]
    rb = jax.ShapeDtypeStruct((s, d), BF16)
    hf = jax.ShapeDtypeStruct((s, MIX_W), F32)
    ph = jax.ShapeDtypeStruct((s, ATT_HG), F32)
    return pl.pallas_call(
        body,
        name=name,
        grid=(s // tm,),
        in_specs=[row] + gate + [half] * 6 + [perhead] * 3
        + [pl.BlockSpec((N_BRANCH, MIX_W, d), lambda i: (0, 0, 0)), pl.BlockSpec((d, d), lambda i: (0, 0))],
        out_specs=[pl.BlockSpec((tm, N_BRANCH * d), lambda i: (i, 0)), row, row, row] + [half] * 5 + [perhead] * 3,
        out_shape=[jax.ShapeDtypeStruct((s, N_BRANCH * d), BF16), rb, rb, rb] + [hf] * 5 + [ph] * 3,
        compiler_params=_cparams(("arbitrary",)),
    )(dx1, proj, proj, proj, ya, yb, yc, *outs, *lses, wb, wout)


ATT_BLOCK = 128
ATT_STEP = 4 * ATT_BLOCK
ATT_GROUPS = ((128, 1), (512, 4), (2048, 16))
N_ATT_HEADS = ATT_HG * len(ATT_GROUPS)
ALIBI_MAX_EXP = 8.0
MASKED = -1e30


def _att_slopes(group):
    return [2.0 ** (-ALIBI_MAX_EXP * (group * ATT_HG + h + 1) / N_ATT_HEADS) for h in range(ATT_HG)]


def _att_scores(q, kk, slope_dil, has_prev):
    scale = ATT_DH**-0.5
    qi = lax.broadcasted_iota(jnp.int32, (ATT_BLOCK, 2 * ATT_BLOCK), 0)
    kj = lax.broadcasted_iota(jnp.int32, (ATT_BLOCK, 2 * ATT_BLOCK), 1)
    dist = qi + ATT_BLOCK - kj
    valid = jnp.logical_and(jnp.logical_and(dist >= 0, dist <= ATT_BLOCK), jnp.logical_or(kj >= ATT_BLOCK, has_prev))
    return jnp.where(valid, _dot(q, kk, 1, 1) * scale - slope_dil * dist.astype(F32), MASKED)


def _att_window(kp_ref, kc_ref, buf):
    buf[:ATT_BLOCK, :] = kp_ref[...]
    buf[ATT_BLOCK:, :] = kc_ref[...]


def attn_fwd(qkv, col, group, *, name="attn_fwd"):
    s = qkv.shape[0]
    window, dil = ATT_GROUPS[group]
    assert window // dil == ATT_BLOCK and s % (dil * ATT_BLOCK) == 0 and col % MIX_W == 0
    nb = s // dil // ATT_BLOCK
    nq = ATT_STEP // ATT_BLOCK
    ng = s // ATT_STEP
    slopes = _att_slopes(group)
    c0 = col // MIX_W

    def body(q_ref, kp_ref, kc_ref, vp_ref, vc_ref, o_ref, l_ref, kbuf, vbuf):
        n = pl.program_id(0)
        _att_window(kp_ref, kc_ref, kbuf)
        _att_window(vp_ref, vc_ref, vbuf)
        for i in range(nq):
            r = slice(i * ATT_BLOCK, (i + 1) * ATT_BLOCK)
            win = slice(i * ATT_BLOCK, (i + 2) * ATT_BLOCK)
            has_prev = ((n * nq + i) % nb) != 0
            for hd in range(ATT_HG):
                sl = slice(hd * ATT_DH, (hd + 1) * ATT_DH)
                sc = _att_scores(q_ref[r, sl], kbuf[win, sl], slopes[hd] * dil, has_prev)
                m = jnp.max(sc, axis=-1, keepdims=True)
                e = jnp.exp(sc - m)
                den = jnp.sum(e, axis=-1, keepdims=True)
                o_ref[r, sl] = _dot(e * (1.0 / den), vbuf[win, sl])
                l_ref[r, hd : hd + 1] = m + jnp.log(den)

    cur = lambda c: pl.BlockSpec((ATT_STEP, MIX_W), lambda n, c=c: (n, c))
    prev = lambda c: pl.BlockSpec((ATT_BLOCK, MIX_W), lambda n, c=c: (jnp.maximum(n * nq - 1, 0), c))
    blk = pl.BlockSpec((ATT_STEP, MIX_W), lambda n: (n, 0))
    return pl.pallas_call(
        body,
        name=name,
        grid=(ng,),
        in_specs=[cur(c0), prev(c0 + 1), cur(c0 + 1), prev(c0 + 2), cur(c0 + 2)],
        out_specs=[blk, pl.BlockSpec((ATT_STEP, ATT_HG), lambda n: (n, 0))],
        out_shape=[jax.ShapeDtypeStruct((s, MIX_W), F32), jax.ShapeDtypeStruct((s, ATT_HG), F32)],
        scratch_shapes=[pltpu.VMEM((ATT_BLOCK + ATT_STEP, MIX_W), F32)] * 2,
        compiler_params=_cparams(("arbitrary",)),
    )(qkv, qkv, qkv, qkv, qkv)


def attn_bwd(qkv, col, group, out, lse, dout, dlse, *, name="attn_bwd"):
    s = qkv.shape[0]
    window, dil = ATT_GROUPS[group]
    nb = s // dil // ATT_BLOCK
    nq = ATT_STEP // ATT_BLOCK
    ng = s // ATT_STEP
    slopes = _att_slopes(group)
    c0 = col // MIX_W
    scale = ATT_DH**-0.5
    tail = slice((nq - 1) * ATT_BLOCK, nq * ATT_BLOCK)

    def body(q_ref, kp_ref, kc_ref, vp_ref, vc_ref, o_ref, l_ref, do_ref, dl_ref, dq_ref, dk_ref, dv_ref, acck, accv,
             kbuf, vbuf):
        n = pl.program_id(0)
        live = n < ng

        @pl.when(n == 0)
        def _():
            acck[...] = jnp.zeros_like(acck)
            accv[...] = jnp.zeros_like(accv)

        _att_window(kp_ref, kc_ref, kbuf)
        _att_window(vp_ref, vc_ref, vbuf)
        dk_ref[: (nq - 1) * ATT_BLOCK, :] = acck[: (nq - 1) * ATT_BLOCK, :].astype(BF16)
        dv_ref[: (nq - 1) * ATT_BLOCK, :] = accv[: (nq - 1) * ATT_BLOCK, :].astype(BF16)
        for i in range(nq):
            r = slice(i * ATT_BLOCK, (i + 1) * ATT_BLOCK)
            rp = slice((i - 1) * ATT_BLOCK, i * ATT_BLOCK)
            win = slice(i * ATT_BLOCK, (i + 2) * ATT_BLOCK)
            has_prev = ((jnp.minimum(n, ng - 1) * nq + i) % nb) != 0
            for hd in range(ATT_HG):
                sl = slice(hd * ATT_DH, (hd + 1) * ATT_DH)
                q, do, kk, vv = q_ref[r, sl], do_ref[r, sl], kbuf[win, sl], vbuf[win, sl]
                p = jnp.exp(_att_scores(q, kk, slopes[hd] * dil, has_prev) - l_ref[r, hd : hd + 1])
                corr = dl_ref[r, hd : hd + 1] - jnp.sum(do * o_ref[r, sl], axis=-1, keepdims=True)
                ds = p * (_dot(do, vv, 1, 1) + corr) * scale
                dq_ref[r, sl] = _dot(ds, kk).astype(BF16)
                dkk = jnp.where(live, _dot(ds, q, 0, 0), 0.0)
                dvv = jnp.where(live, _dot(p, do, 0, 0), 0.0)
                if i == 0:
                    dk_ref[tail, sl] = (acck[tail, sl] + dkk[:ATT_BLOCK]).astype(BF16)
                    dv_ref[tail, sl] = (accv[tail, sl] + dvv[:ATT_BLOCK]).astype(BF16)
                else:
                    acck[rp, sl] += dkk[:ATT_BLOCK]
                    accv[rp, sl] += dvv[:ATT_BLOCK]
                acck[r, sl] = dkk[ATT_BLOCK:]
                accv[r, sl] = dvv[ATT_BLOCK:]

    last = ng - 1
    cur = lambda c: pl.BlockSpec((ATT_STEP, MIX_W), lambda n, c=c: (jnp.minimum(n, last), c))
    prev = lambda c: pl.BlockSpec((ATT_BLOCK, MIX_W), lambda n, c=c: (jnp.maximum(jnp.minimum(n, last) * nq - 1, 0), c))
    behind = pl.BlockSpec((ATT_STEP, MIX_W), lambda n: (jnp.clip(n - 1, 0, last), 0))
    perhead = pl.BlockSpec((ATT_STEP, ATT_HG), lambda n: (jnp.minimum(n, last), 0))
    sds = jax.ShapeDtypeStruct((s, MIX_W), BF16)
    return pl.pallas_call(
        body,
        name=name,
        grid=(ng + 1,),
        in_specs=[cur(c0), prev(c0 + 1), cur(c0 + 1), prev(c0 + 2), cur(c0 + 2), cur(0), perhead, cur(0), perhead],
        out_specs=[cur(0), behind, behind],
        out_shape=[sds, sds, sds],
        scratch_shapes=[pltpu.VMEM((ATT_STEP, MIX_W), F32)] * 2 + [pltpu.VMEM((ATT_BLOCK + ATT_STEP, MIX_W), F32)] * 2,
        compiler_params=_cparams(("arbitrary",)),
    )(qkv, qkv, qkv, qkv, qkv, out, lse, dout, dlse)


GLA_HEADS = 4
GLA_DK = 64
GLA_DV = 128
GLA_CHUNK = 64
GLA_TAU = 16.0
GLA_QK = GLA_HEADS * GLA_DK
LR_PAD = 128


def _dot_exact(a, b, ca=1, cb=0):
    return lax.dot_general(a, b, (((ca,), (cb,)), ((), ())), precision=lax.Precision.HIGHEST, preferred_element_type=F32)


GLA_STEP = 4 * GLA_CHUNK


def _gla_group(q, k, v, lr, r, state, wg, bg, go):
    c, n = GLA_CHUNK, q.shape[0]
    z = _dot(lr, wg) + bg
    la = (jnp.minimum(z, 0.0) - jnp.log(1.0 + jnp.exp(-jnp.abs(z)))) * (1.0 / GLA_TAU)
    ri = lax.broadcasted_iota(jnp.int32, (n, n), 0)
    ci = lax.broadcasted_iota(jnp.int32, (n, n), 1)
    shift = c.bit_length() - 1
    in_chunk = jnp.logical_and(ri >= ci, jnp.right_shift(ri, shift) == jnp.right_shift(ci, shift))
    ball = _dot_exact(in_chunk.astype(F32), la)
    ri = lax.broadcasted_iota(jnp.int32, (c, c), 0)
    ci = lax.broadcasted_iota(jnp.int32, (c, c), 1)
    causal = ri >= ci
    qs = q * GLA_DK**-0.5
    ys = []
    for j in range(n // c):
        rows = slice(j * c, (j + 1) * c)
        b = ball[rows]
        bmid = b[c // 2 : c // 2 + 1, :]
        bend = b[c - 1 : c, :]
        q_in = qs[rows] * jnp.exp(b)
        q_mid = qs[rows] * jnp.exp(b - bmid)
        k_mid = k[rows] * jnp.exp(bmid - b)
        k_end = k[rows] * jnp.exp(bend - b)
        yh, upd = [], []
        for h in range(GLA_HEADS):
            sk = slice(h * GLA_DK, (h + 1) * GLA_DK)
            sv = slice(h * GLA_DV, (h + 1) * GLA_DV)
            vh = v[rows, sv]
            inter = _dot(q_in[:, sk], state[:, sk], 1, 1)
            sc = jnp.where(causal, _dot_exact(q_mid[:, sk], k_mid[:, sk], 1, 1), 0.0)
            o = inter + _dot(sc, vh)
            o = o * lax.rsqrt(jnp.mean(o * o, axis=-1, keepdims=True) + RMS_EPS) * go
            rh = r[rows, sv]
            yh.append(o * rh * _sigmoid(rh))
            upd.append(_dot(vh, k_end[:, sk], 0, 0))
        state = jnp.exp(bend) * state + jnp.concatenate(upd, axis=-1)
        ys.append(jnp.concatenate(yh, axis=-1))
    return jnp.concatenate(ys, axis=0), state


def _gla_in_specs(cols, rev, nc):
    c = GLA_STEP
    row = (lambda i: nc - 1 - i) if rev else (lambda i: i)
    qc, kc, vc, lc, rc = cols
    assert qc % GLA_QK == 0 and kc % GLA_QK == 0 and vc % MIX_W == 0 and rc % MIX_W == 0 and lc % LR_PAD == 0
    return [
        pl.BlockSpec((c, GLA_QK), lambda i: (row(i), qc // GLA_QK)),
        pl.BlockSpec((c, GLA_QK), lambda i: (row(i), kc // GLA_QK)),
        pl.BlockSpec((c, MIX_W), lambda i: (row(i), vc // MIX_W)),
        pl.BlockSpec((c, LR_PAD), lambda i: (row(i), lc // LR_PAD)),
        pl.BlockSpec((c, MIX_W), lambda i: (row(i), rc // MIX_W)),
    ], row


def gla_fwd(proj, cols, wg, bg, go, *, name="gla_fwd"):
    s = proj.shape[0]
    nc = s // GLA_STEP
    specs, row = _gla_in_specs(cols, False, nc)

    def body(q_ref, k_ref, v_ref, lr_ref, r_ref, wg_ref, bg_ref, go_ref, y_ref, st_ref, state):
        @pl.when(pl.program_id(0) == 0)
        def _():
            state[...] = jnp.zeros_like(state)

        st = state[...]
        st_ref[...] = st
        y, new = _gla_group(q_ref[...], k_ref[...], v_ref[...], lr_ref[...], r_ref[...], st, wg_ref[...], bg_ref[...], go_ref[...])
        y_ref[...] = y
        state[...] = new

    full = lambda shp: pl.BlockSpec(shp, lambda i: (0, 0))
    return pl.pallas_call(
        body,
        name=name,
        grid=(nc,),
        in_specs=specs + [full((LR_PAD, GLA_QK)), full((1, GLA_QK)), full((1, GLA_DV))],
        out_specs=[pl.BlockSpec((GLA_STEP, MIX_W), lambda i: (i, 0)), pl.BlockSpec((GLA_DV, GLA_QK), lambda i: (i, 0))],
        out_shape=[jax.ShapeDtypeStruct((s, MIX_W), F32), jax.ShapeDtypeStruct((nc * GLA_DV, GLA_QK), F32)],
        scratch_shapes=[pltpu.VMEM((GLA_DV, GLA_QK), F32)],
        compiler_params=_cparams(("arbitrary",)),
    )(proj, proj, proj, proj, proj, wg, bg, go)


def gla_bwd(proj, cols, wg, bg, go, states, dy, *, name="gla_bwd"):
    s = proj.shape[0]
    nc = s // GLA_STEP
    specs, row = _gla_in_specs(cols, True, nc)

    def body(q_ref, k_ref, v_ref, lr_ref, r_ref, wg_ref, bg_ref, go_ref, st_ref, dy_ref,
             dq_ref, dk_ref, dv_ref, dlr_ref, dr_ref, dwg_ref, dbg_ref, dgo_ref, dstate):
        first = pl.program_id(0) == 0

        @pl.when(first)
        def _():
            dstate[...] = jnp.zeros_like(dstate)

        _, vjp = jax.vjp(_gla_group, q_ref[...], k_ref[...], v_ref[...], lr_ref[...], r_ref[...], st_ref[...],
                         wg_ref[...].astype(F32), bg_ref[...], go_ref[...])
        dq, dk, dv, dlr, dr, dst, dwg, dbg, dgo = vjp((dy_ref[...], dstate[...]))
        dq_ref[...] = dq.astype(BF16)
        dk_ref[...] = dk.astype(BF16)
        dv_ref[...] = dv.astype(BF16)
        dlr_ref[...] = dlr.astype(BF16)
        dr_ref[...] = dr.astype(BF16)
        dstate[...] = dst
        _accum(dwg_ref, dwg, first)
        _accum(dbg_ref, dbg, first)
        _accum(dgo_ref, dgo, first)

    c = GLA_STEP
    full = lambda shp: pl.BlockSpec(shp, lambda i: (0, 0))
    rows = lambda w: pl.BlockSpec((c, w), lambda i: (row(i), 0))
    return pl.pallas_call(
        body,
        name=name,
        grid=(nc,),
        in_specs=specs + [full((LR_PAD, GLA_QK)), full((1, GLA_QK)), full((1, GLA_DV)),
                          pl.BlockSpec((GLA_DV, GLA_QK), lambda i: (row(i), 0)), rows(MIX_W)],
        out_specs=[rows(GLA_QK), rows(GLA_QK), rows(MIX_W), rows(LR_PAD), rows(MIX_W),
                   full((LR_PAD, GLA_QK)), full((1, GLA_QK)), full((1, GLA_DV))],
        out_shape=[jax.ShapeDtypeStruct((s, GLA_QK), BF16), jax.ShapeDtypeStruct((s, GLA_QK), BF16),
                   jax.ShapeDtypeStruct((s, MIX_W), BF16), jax.ShapeDtypeStruct((s, LR_PAD), BF16),
                   jax.ShapeDtypeStruct((s, MIX_W), BF16), jax.ShapeDtypeStruct((LR_PAD, GLA_QK), F32),
                   jax.ShapeDtypeStruct((1, GLA_QK), F32), jax.ShapeDtypeStruct((1, GLA_DV), F32)],
        scratch_shapes=[pltpu.VMEM((GLA_DV, GLA_QK), F32)],
        compiler_params=_cparams(("arbitrary",)),
    )(proj, proj, proj, proj, proj, wg, bg, go, states, dy)


S5_G = 32
S5_P = 64
S5_C = 16
S5_N = S5_G * S5_P
S5_TC = 256
SUB = 8


def _s5_disc(a_re, a_im, ls, b_re, b_im):
    step = jnp.exp(ls)
    mag = jnp.exp(a_re * step)
    lr, li = mag * jnp.cos(a_im * step), mag * jnp.sin(a_im * step)
    inv = 1.0 / (a_re * a_re + a_im * a_im)
    nr, ni = lr - 1.0, li
    cr, ci = (nr * a_re + ni * a_im) * inv, (ni * a_re - nr * a_im) * inv
    return lr, li, cr * b_re - ci * b_im, cr * b_im + ci * b_re


def s5_disc_fwd(a_re, a_im, ls, b_re, b_im, *, name="s5_disc"):
    def body(ar, ai, l, br, bi, o0, o1, o2, o3):
        for o, val in zip((o0, o1, o2, o3), _s5_disc(ar[...], ai[...], l[...], br[...], bi[...])):
            o[...] = val

    sds = jax.ShapeDtypeStruct(a_re.shape, F32)
    return pl.pallas_call(body, name=name, out_shape=[sds] * 4)(a_re, a_im, ls, b_re, b_im)


def s5_disc_bwd(a_re, a_im, ls, b_re, b_im, cts, *, name="s5_disc_bwd"):
    def body(ar, ai, l, br, bi, c0, c1, c2, c3, dar, dai, dl, dbr, dbi):
        _, vjp = jax.vjp(_s5_disc, ar[...], ai[...], l[...], br[...], bi[...])
        g = vjp((c0[...], c1[...], c2[...], c3[...]))
        for o, val in zip((dar, dai, dl), g[:3]):
            o[...] = jnp.sum(val, axis=-1, keepdims=True)
        dbr[...] = g[3]
        dbi[...] = g[4]

    col = jax.ShapeDtypeStruct((a_re.shape[0], 1), F32)
    sds = jax.ShapeDtypeStruct(a_re.shape, F32)
    return pl.pallas_call(body, name=name, out_shape=[col, col, col, sds, sds])(a_re, a_im, ls, b_re, b_im, *cts)


def _gelu(y):
    return 0.5 * y * (1.0 + jnp.tanh(0.7978845608028654 * (y + 0.044715 * (y * y * y))))


def _s5_powers(lam, conj):
    lr, li = lam[:, :S5_N], lam[:, S5_N:]
    if conj:
        li = -li
    rows, pr, pi = [], lr, li
    for _ in range(SUB):
        rows.append((pr, pi))
        pr, pi = pr * lr - pi * li, pr * li + pi * lr
    return rows


def _s5_table(rows, reverse):
    ridx = lax.broadcasted_iota(jnp.int32, (SUB, S5_N), 0)
    tr = jnp.zeros((SUB, S5_N), F32)
    ti = jnp.zeros((SUB, S5_N), F32)
    for i in range(SUB):
        pr, pi = rows[SUB - 1 - i] if reverse else rows[i]
        tr = jnp.where(ridx == i, pr, tr)
        ti = jnp.where(ridx == i, pi, ti)
    return tr, ti


def _s5_scan(buf, lam, carry_ref, reverse):
    tc = buf.shape[0]
    nblk = tc // SUB
    rows = _s5_powers(lam, reverse)
    tr, ti = _s5_table(rows, reverse)
    ridx = lax.broadcasted_iota(jnp.int32, (SUB, S5_N), 0)
    steps = []
    for sft, (pr, pi) in ((1, rows[0]), (2, rows[1]), (4, rows[3])):
        keep = (ridx < SUB - sft) if reverse else (ridx >= sft)
        steps.append((SUB - sft if reverse else sft, jnp.where(keep, pr, 0.0), jnp.where(keep, pi, 0.0)))

    def block(j, carry):
        jj = (nblk - 1 - j) if reverse else j
        at = pl.ds(pl.multiple_of(jj * SUB, SUB), SUB)
        re, im = buf[at, :S5_N], buf[at, S5_N:]
        for rot, pr, pi in steps:
            sre, sim = pltpu.roll(re, rot, 0), pltpu.roll(im, rot, 0)
            re, im = re + pr * sre - pi * sim, im + pr * sim + pi * sre
        cr, ci = carry
        re, im = re + tr * cr - ti * ci, im + tr * ci + ti * cr
        buf[at, :S5_N] = re
        buf[at, S5_N:] = im
        edge = 0 if reverse else SUB - 1
        return re[edge : edge + 1, :], im[edge : edge + 1, :]

    c0 = (carry_ref[0:1, :S5_N], carry_ref[0:1, S5_N:])
    cr, ci = lax.fori_loop(0, nblk, block, c0)
    carry_ref[:, :S5_N] = jnp.broadcast_to(cr, (SUB, S5_N))
    carry_ref[:, S5_N:] = jnp.broadcast_to(ci, (SUB, S5_N))


def s5_fwd(proj, ucol, bd, cd, lam, dskip, wglu, bglu, *, name="s5_fwd"):
    s = proj.shape[0]
    tc = _pick(s, (S5_TC, 128, 64, 8))
    assert ucol % MIX_W == 0

    def body(u_ref, bd_ref, cd_ref, lam_ref, d_ref, w_ref, b_ref, y_ref, xs_ref, carry):
        @pl.when(pl.program_id(0) == 0)
        def _():
            carry[...] = jnp.zeros_like(carry)

        u = u_ref[...]
        xs_ref[...] = _dot(u, bd_ref[...])
        _s5_scan(xs_ref, lam_ref[...], carry, False)
        g = _gelu(_dot(xs_ref[...], cd_ref[...]) + d_ref[...] * u)
        y_ref[...] = g * _sigmoid(_dot(g, w_ref[...]) + b_ref[...])

    full = lambda shp: pl.BlockSpec(shp, lambda i: (0, 0))
    return pl.pallas_call(
        body,
        name=name,
        grid=(s // tc,),
        in_specs=[pl.BlockSpec((tc, MIX_W), lambda i: (i, ucol // MIX_W)), full((MIX_W, 2 * S5_N)), full((2 * S5_N, MIX_W)),
                  full((1, 2 * S5_N)), full((1, MIX_W)), full((MIX_W, MIX_W)), full((1, MIX_W))],
        out_specs=[pl.BlockSpec((tc, MIX_W), lambda i: (i, 0)), pl.BlockSpec((tc, 2 * S5_N), lambda i: (i, 0))],
        out_shape=[jax.ShapeDtypeStruct((s, MIX_W), F32), jax.ShapeDtypeStruct((s, 2 * S5_N), F32)],
        scratch_shapes=[pltpu.VMEM((SUB, 2 * S5_N), F32)],
        compiler_params=_cparams(("arbitrary",)),
    )(proj, bd, cd, lam, dskip, wglu, bglu)


def s5_bwd(proj, ucol, xs, bd, cd, lam, dskip, wglu, bglu, dya, *, name="s5_bwd"):
    s = proj.shape[0]
    tc = _pick(s, (S5_TC, 128, 64, 8))
    nch = s // tc
    per = tc // SUB

    def body(u_ref, xs_ref, xp_ref, bd_ref, cd_ref, lam_ref, d_ref, w_ref, b_ref, dya_ref,
             du_ref, adj_ref, g_ref, dpre_ref, dy_ref, dlam_ref, dd_ref, db_ref, buf, carry, lacc):
        i = pl.program_id(0)
        first = i == 0

        @pl.when(first)
        def _():
            carry[...] = jnp.zeros_like(carry)
            lacc[...] = jnp.zeros_like(lacc)

        u, x, dya_ = u_ref[...], xs_ref[...], dya_ref[...]
        y = _dot(x, cd_ref[...]) + d_ref[...] * u
        g, gelu_vjp = jax.vjp(_gelu, y)
        sg = _sigmoid(_dot(g, w_ref[...]) + b_ref[...])
        dpre = dya_ * g * sg * (1.0 - sg)
        (dy,) = gelu_vjp(dya_ * sg + _dot(dpre, w_ref[...], 1, 1))
        g_ref[...] = g.astype(BF16)
        dpre_ref[...] = dpre.astype(BF16)
        dy_ref[...] = dy.astype(BF16)
        db_part = jnp.sum(dpre, axis=0, keepdims=True)
        dd_part = jnp.sum(dy * u, axis=0, keepdims=True)
        buf[...] = _dot(dy, cd_ref[...], 1, 1)
        _s5_scan(buf, lam_ref[...], carry, True)
        a = buf[...]
        adj_ref[...] = a.astype(BF16)
        du_ref[...] = (dy * d_ref[...] + _dot(a, bd_ref[...], 1, 1)).astype(BF16)
        before = jnp.where(i == nch - 1, 0.0, xp_ref[SUB - 1 : SUB, :])
        ridx = lax.broadcasted_iota(jnp.int32, (tc, 2 * S5_N), 0)
        xprev = jnp.where(ridx == 0, before, pltpu.roll(x, 1, 0))
        ar, ai, xr, xi = a[:, :S5_N], a[:, S5_N:], xprev[:, :S5_N], xprev[:, S5_N:]
        lacc[:, :S5_N] += jnp.sum((ar * xr + ai * xi).reshape(per, SUB, S5_N), axis=0)
        lacc[:, S5_N:] += jnp.sum((ai * xr - ar * xi).reshape(per, SUB, S5_N), axis=0)
        _accum(db_ref, db_part, first)
        _accum(dd_ref, dd_part, first)

        @pl.when(i == nch - 1)
        def _():
            dlam_ref[...] = jnp.sum(lacc[...], axis=0, keepdims=True)

    rev = lambda i: nch - 1 - i
    full = lambda shp: pl.BlockSpec(shp, lambda i: (0, 0))
    rows = lambda w: pl.BlockSpec((tc, w), lambda i: (rev(i), 0))
    hb = jax.ShapeDtypeStruct((s, MIX_W), BF16)
    return pl.pallas_call(
        body,
        name=name,
        grid=(nch,),
        in_specs=[pl.BlockSpec((tc, MIX_W), lambda i: (rev(i), ucol // MIX_W)), rows(2 * S5_N),
                  pl.BlockSpec((SUB, 2 * S5_N), lambda i: (jnp.maximum(rev(i) * per - 1, 0), 0)),
                  full((MIX_W, 2 * S5_N)), full((2 * S5_N, MIX_W)), full((1, 2 * S5_N)), full((1, MIX_W)),
                  full((MIX_W, MIX_W)), full((1, MIX_W)), rows(MIX_W)],
        out_specs=[rows(MIX_W), rows(2 * S5_N), rows(MIX_W), rows(MIX_W), rows(MIX_W),
                   full((1, 2 * S5_N)), full((1, MIX_W)), full((1, MIX_W))],
        out_shape=[hb, jax.ShapeDtypeStruct((s, 2 * S5_N), BF16), hb, hb, hb,
                   jax.ShapeDtypeStruct((1, 2 * S5_N), F32), jax.ShapeDtypeStruct((1, MIX_W), F32),
                   jax.ShapeDtypeStruct((1, MIX_W), F32)],
        scratch_shapes=[pltpu.VMEM((tc, 2 * S5_N), F32), pltpu.VMEM((SUB, 2 * S5_N), F32), pltpu.VMEM((SUB, 2 * S5_N), F32)],
        compiler_params=_cparams(("arbitrary",)),
    )(proj, xs, xs, bd, cd, lam, dskip, wglu, bglu, dya)


def _bcast16(a):
    return jnp.broadcast_to(a.reshape(S5_N, 1), (S5_N, S5_C))


def _blockdiag(blocks):
    g, r, c = blocks.shape
    eye = jnp.eye(g, dtype=blocks.dtype)
    return (eye[:, None, :, None] * blocks[:, :, None, :]).reshape(g * r, g * c)


def _blockdiag_extract(dense, r, c):
    g = dense.shape[0] // r
    return jnp.einsum("grgc->grc", dense.reshape(g, r, g, c))


def s5_prepare(a_re, a_im, log_step, b_re, b_im, c_re, c_im):
    disc_in = (_bcast16(a_re), _bcast16(a_im), _bcast16(jnp.broadcast_to(log_step[:, None], (S5_G, S5_P))),
               b_re.reshape(S5_N, S5_C), b_im.reshape(S5_N, S5_C))
    lr, li, bbr, bbi = s5_disc_fwd(*disc_in)
    lam = jnp.concatenate([lr[:, 0], li[:, 0]]).reshape(1, 2 * S5_N)
    to_blocks = lambda t: _blockdiag(t.reshape(S5_G, S5_P, S5_C).transpose(0, 2, 1))
    bd = jnp.concatenate([to_blocks(bbr), to_blocks(bbi)], axis=1).astype(BF16)
    cd = jnp.concatenate([_blockdiag(c_re.transpose(0, 2, 1)), -_blockdiag(c_im.transpose(0, 2, 1))], axis=0).astype(BF16)
    return disc_in, lam, bd, cd


def s5_param_grads(disc_in, dlam, dbd, dcd):
    first_col = lambda v: jnp.pad(v.reshape(S5_N, 1), ((0, 0), (0, S5_C - 1)))
    from_blocks = lambda t: _blockdiag_extract(t, S5_C, S5_P).transpose(0, 2, 1).reshape(S5_N, S5_C)
    cts = (first_col(dlam[0, :S5_N]), first_col(dlam[0, S5_N:]), from_blocks(dbd[:, :S5_N]), from_blocks(dbd[:, S5_N:]))
    dar, dai, dls, dbr, dbi = s5_disc_bwd(*disc_in, cts)
    dcr = _blockdiag_extract(dcd[:S5_N], S5_P, S5_C).transpose(0, 2, 1)
    dci = -_blockdiag_extract(dcd[S5_N:], S5_P, S5_C).transpose(0, 2, 1)
    return (dar.reshape(S5_G, S5_P), dai.reshape(S5_G, S5_P), dls.reshape(S5_G, S5_P).sum(axis=1),
            dbr.reshape(S5_G, S5_P, S5_C), dbi.reshape(S5_G, S5_P, S5_C), dcr, dci)


N_DEV = 8
MESH_ID = pl.DeviceIdType.MESH
ANY = pl.BlockSpec(memory_space=pl.ANY)


def _me():
    return lax.axis_index("x"), lax.axis_index("y"), lax.axis_index("c")


def all_gather(blocks, *, by_core=False, name="all_gather"):
    na = len(blocks)
    shapes = [b.shape[1:] if by_core else b.shape for b in blocks]

    def body(*refs):
        ins, outs = refs[:na], refs[na : 2 * na]
        send_sems, recv_sems = refs[2 * na :]
        x, y, c = _me()
        me, sibling = (x, y, c), (x, y, 1 - c)
        xn, yn, diag = (1 - x, y, c), (x, 1 - y, c), (1 - x, 1 - y, c)
        slot = lambda p: 4 * p[0] + 2 * p[1] + p[2]

        def copy(a, k, block, to, half=None, src=None):
            dst = outs[a].at[slot(block)]
            if half is not None:
                rows = shapes[a][0] // 2
                dst = dst.at[pl.ds(half * rows, rows)]
            return pltpu.make_async_remote_copy(src_ref=dst if src is None else src, dst_ref=dst, send_sem=send_sems.at[a * 9 + k],
                                                recv_sem=recv_sems.at[a * 9 + k], device_id=to, device_id_type=MESH_ID)

        mine = [ins[a].at[c] if by_core else ins[a] for a in range(na)]
        sends = []

        def go(cp):
            cp.start()
            sends.append(cp)

        for a in range(na):
            go(copy(a, 0, me, sibling, src=mine[a]))
            go(copy(a, 1, me, xn, src=mine[a]))
            go(copy(a, 2, me, yn, src=mine[a]))
        for a in range(na):
            copy(a, 1, xn, me).wait_recv()
            go(copy(a, 3, xn, yn, half=0))
            go(copy(a, 5, xn, sibling))
            copy(a, 2, yn, me).wait_recv()
            go(copy(a, 4, yn, xn, half=1))
            go(copy(a, 6, yn, sibling))
        for a in range(na):
            copy(a, 3, diag, me, half=0).wait_recv()
            go(copy(a, 7, diag, sibling, half=0))
            copy(a, 4, diag, me, half=1).wait_recv()
            go(copy(a, 8, diag, sibling, half=1))
        for a in range(na):
            copy(a, 0, sibling, me).wait_recv()
            copy(a, 5, (1 - x, y, 1 - c), me).wait_recv()
            copy(a, 6, (x, 1 - y, 1 - c), me).wait_recv()
            copy(a, 7, (1 - x, 1 - y, 1 - c), me, half=0).wait_recv()
            copy(a, 8, (1 - x, 1 - y, 1 - c), me, half=1).wait_recv()
        for cp in sends:
            cp.wait_send()

    assert all(shp[0] % (64 // b.dtype.itemsize) == 0 for shp, b in zip(shapes, blocks)), shapes
    outs = pl.pallas_call(
        body,
        name=name,
        in_specs=[ANY] * na,
        out_specs=[ANY] * na,
        out_shape=[jax.ShapeDtypeStruct((N_DEV, *shp), b.dtype) for shp, b in zip(shapes, blocks)],
        scratch_shapes=[pltpu.SemaphoreType.DMA((na * 9,)), pltpu.SemaphoreType.DMA((na * 9,))],
    )(*blocks)
    x, y, c = _me()
    own = [lax.dynamic_index_in_dim(b, c, 0, keepdims=False) if by_core else b for b in blocks]
    return [lax.dynamic_update_index_in_dim(o, b, 4 * x + 2 * y + c, 0) for o, b in zip(outs, own)]


D2D_PIECES = 8


def _slabs(rows):
    n = D2D_PIECES if rows % (16 * D2D_PIECES) == 0 else 1
    return [pl.ds(i * (rows // n), rows // n) for i in range(n)]


def sibling_swap(arrs, *, name="sibling_swap"):
    na = len(arrs)
    pieces = [[(j, sl) for j in range(a.shape[0]) for sl in (_slabs(a.shape[1]) if a.shape[0] == 1 else _halves(a.shape[1]))]
              for a in arrs]
    base = np.cumsum([0] + [len(p) for p in pieces])

    def body(*refs):
        ins, outs = refs[:na], refs[na : 2 * na]
        send_sems, recv_sems = refs[2 * na :]
        x, y, c = _me()
        sends, recvs = [], []
        for a in range(na):
            for i, (j, sl) in enumerate(pieces[a]):
                k = int(base[a]) + i
                cp = pltpu.make_async_remote_copy(src_ref=ins[a].at[j, sl], dst_ref=outs[a].at[j, sl], send_sem=send_sems.at[k],
                                                  recv_sem=recv_sems.at[k], device_id=(x, y, 1 - c), device_id_type=MESH_ID)
                cp.start()
                sends.append(cp)
        for cp in sends:
            cp.wait_recv()
        for cp in sends:
            cp.wait_send()

    nsem = int(base[-1])
    return pl.pallas_call(
        body,
        name=name,
        in_specs=[ANY] * na,
        out_specs=[ANY] * na,
        out_shape=[jax.ShapeDtypeStruct(a.shape, a.dtype) for a in arrs],
        scratch_shapes=[pltpu.SemaphoreType.DMA((nsem,)), pltpu.SemaphoreType.DMA((nsem,))],
    )(*arrs)


def _halves(rows):
    return [pl.ds(0, rows // 2), pl.ds(rows // 2, rows // 2)] if rows % 32 == 0 else [pl.ds(0, rows)]


def chip_exchange(arrs, *, name="chip_exchange"):
    na = len(arrs)

    def body(*refs):
        ins, outs = refs[:na], refs[na : 2 * na]
        send_sems, recv_sems = refs[2 * na :]
        x, y, c = _me()
        my = 2 * x + y
        peers = []
        for k in range(1, N_CHIPS):
            px, py = (x + ((k >> 1) & 1)) % 2, (y + (k & 1)) % 2
            peers.append(((px, py, c), 2 * px + py))
        sends = []
        for a in range(na):
            for k, (peer, pidx) in enumerate(peers):
                cp = pltpu.make_async_remote_copy(src_ref=ins[a].at[pidx], dst_ref=outs[a].at[my], send_sem=send_sems.at[a * 3 + k],
                                                  recv_sem=recv_sems.at[a * 3 + k], device_id=peer, device_id_type=MESH_ID)
                cp.start()
                sends.append(cp)
        for a in range(na):
            for k, (peer, pidx) in enumerate(peers):
                pltpu.make_async_remote_copy(src_ref=ins[a].at[my], dst_ref=outs[a].at[pidx], send_sem=send_sems.at[a * 3 + k],
                                             recv_sem=recv_sems.at[a * 3 + k], device_id=peer, device_id_type=MESH_ID).wait_recv()
        for cp in sends:
            cp.wait_send()

    outs = pl.pallas_call(
        body,
        name=name,
        in_specs=[ANY] * na,
        out_specs=[ANY] * na,
        out_shape=[jax.ShapeDtypeStruct(a.shape, a.dtype) for a in arrs],
        scratch_shapes=[pltpu.SemaphoreType.DMA((na * 3,)), pltpu.SemaphoreType.DMA((na * 3,))],
    )(*arrs)
    chip = 2 * lax.axis_index("x") + lax.axis_index("y")
    return [lax.dynamic_update_index_in_dim(o, lax.dynamic_index_in_dim(a, chip, 0, keepdims=False), chip, 0)
            for o, a in zip(outs, arrs)]


def add_pair(a, b, *, name="add_pair"):
    n, r, c = a.shape
    tm = _row_tile(r, n * c * 2, 2 << 20)

    def body(a_ref, b_ref, o_ref):
        o_ref[...] = (a_ref[...].astype(F32) + b_ref[...].astype(F32)).astype(o_ref.dtype)

    blk = pl.BlockSpec((n, tm, c), lambda i: (0, i, 0))
    return pl.pallas_call(
        body, name=name, grid=(r // tm,), in_specs=[blk, blk], out_specs=blk, out_shape=jax.ShapeDtypeStruct(a.shape, a.dtype),
        compiler_params=_cparams(("arbitrary",)),
    )(a, b)


def _row_tile(rows, row_bytes, budget):
    best = None
    for t in range(8, rows + 1, 8):
        if rows % t == 0 and t * row_bytes <= budget:
            best = t
    return best or rows


def sum_blocks(a, *, name="sum_blocks"):
    n, r, c = a.shape
    tm = _row_tile(r, n * c * a.dtype.itemsize, 4 << 20)

    def body(a_ref, o_ref):
        acc = a_ref[0].astype(F32)
        for k in range(1, n):
            acc = acc + a_ref[k].astype(F32)
        o_ref[...] = acc

    return pl.pallas_call(
        body,
        name=name,
        grid=(r // tm,),
        in_specs=[pl.BlockSpec((n, tm, c), lambda i: (0, i, 0))],
        out_specs=pl.BlockSpec((tm, c), lambda i: (i, 0)),
        out_shape=jax.ShapeDtypeStruct((r, c), F32),
        compiler_params=_cparams(("arbitrary",)),
    )(a)


ADAM_LR = 0.001
ADAM_B1 = 0.9
ADAM_B2 = 0.999
ADAM_EPS = 1e-08
ADAM_WD = 0.01
ADAM_STEP = 10


def adamw(w, g, m, v, *, name="adamw"):
    r, c = w.shape
    tm = _row_tile(r, c * 4, 1 << 20)

    def body(w_ref, g_ref, m_ref, v_ref, d_ref, nm_ref, nv_ref):
        g_ = g_ref[...]
        m_ = ADAM_B1 * m_ref[...] + (1.0 - ADAM_B1) * g_
        v_ = ADAM_B2 * v_ref[...] + (1.0 - ADAM_B2) * (g_ * g_)
        m_hat = m_ / (1.0 - ADAM_B1**ADAM_STEP)
        v_hat = v_ / (1.0 - ADAM_B2**ADAM_STEP)
        d_ref[...] = -ADAM_LR * (m_hat / (jnp.sqrt(v_hat) + ADAM_EPS) + ADAM_WD * w_ref[...])
        nm_ref[...] = m_
        nv_ref[...] = v_

    blk = pl.BlockSpec((tm, c), lambda i: (i, 0))
    sds = jax.ShapeDtypeStruct((r, c), F32)
    return pl.pallas_call(
        body, name=name, grid=(r // tm,), in_specs=[blk] * 4, out_specs=[blk] * 3, out_shape=[sds] * 3,
        compiler_params=_cparams(("arbitrary",)),
    )(w, g, m, v)


WEIGHTS = ["g_mix", "w_in", "s5_a_re", "s5_a_im", "s5_log_step", "s5_b_re", "s5_b_im", "s5_c_re", "s5_c_im", "s5_d", "w_glu",
           "b_glu", "w_gla_gate", "b_gla_gate", "g_gla_out", "w_branch", "w_out", "g_mem", "g_cross", "w_xq", "w_xkv", "w_xo",
           "g_mlp", "w_up", "w_down", "g_final"]
SHARDED = {"w_in": 1, "w_glu": 0, "w_branch": 2, "w_out": 0, "w_xq": 0, "w_xkv": 1, "w_xo": 0, "w_up": 1, "w_down": 0}
DEPTH = 2
N_CHIPS = 4
D_IN = 9744
C_U, C_QG, C_KG, C_VG, C_LR, C_RG, C_QA, C_KA, C_VA, C_GATE = 0, 512, 768, 1024, 1536, 1552, 2064, 3600, 5136, 6672
A_GATE, A_U, A_QG, A_KG, A_VG, A_RG, A_Q0, A_LR, A_W = 0, 3072, 3584, 3840, 4096, 4608, 5120, 6656, 7168
GLA_COLS = (A_QG, A_KG, A_VG, A_LR, A_RG)


def _split_w_in(w):
    att = lambda g: [w[:, c + MIX_W * g : c + MIX_W * (g + 1)] for c in (C_QA, C_KA, C_VA)]
    a = jnp.concatenate([w[:, C_GATE:D_IN], w[:, C_U:C_QG], w[:, C_QG:C_KG], w[:, C_KG:C_VG], w[:, C_VG:C_LR], w[:, C_RG:C_QA],
                         *att(0), w[:, C_LR:C_RG], jnp.zeros((w.shape[0], A_W - A_LR - 16), w.dtype)], axis=1)
    return a, jnp.concatenate(att(1), axis=1), jnp.concatenate(att(2), axis=1)


def _join_w_in(a, b, c):
    att = lambda k: [a[:, A_Q0 + MIX_W * k : A_Q0 + MIX_W * (k + 1)], b[:, MIX_W * k : MIX_W * (k + 1)], c[:, MIX_W * k : MIX_W * (k + 1)]]
    return jnp.concatenate([a[:, A_U:A_QG], a[:, A_QG:A_KG], a[:, A_KG:A_VG], a[:, A_VG:A_RG], a[:, A_LR : A_LR + 16], a[:, A_RG:A_Q0],
                            *att(0), *att(1), *att(2), a[:, A_GATE:A_U]], axis=1)


def _by_residue(a, d):
    s = a.shape[0]
    return a if d == 1 else a.reshape(s // d, d, -1).transpose(1, 0, 2).reshape(s, -1)


def _in_order(a, d):
    s = a.shape[0]
    return a if d == 1 else a.reshape(d, s // d, -1).transpose(1, 0, 2).reshape(s, -1)


def _pack(arrs):
    flat = jnp.concatenate([a.reshape(-1) for a in arrs])
    n = flat.shape[0]
    rows = -(-n // (256 * 128)) * 256
    return jnp.pad(flat, (0, rows * 128 - n)).reshape(rows, 128)


def _unpack(packed, like):
    flat, out, o = packed.reshape(-1), [], 0
    for a in like:
        n = math.prod(a.shape)
        out.append(flat[o : o + n].reshape(a.shape))
        o += n
    return out


def _layer_fwd(x0, p, kv):
    h = rmsnorm(x0, p["g_mix"], name="mix_norm")
    hs = [h, _by_residue(h, 4), _by_residue(h, 16)]
    proj = [matmul(hs[g], p["w_in_seg"][g], name="in_proj") for g in range(3)]
    ya, xs = s5_fwd(proj[0], A_U, p["bd"], p["cd"], p["lam"], p["s5_d"], p["w_glu"], p["b_glu"])
    yb, gst = gla_fwd(proj[0], GLA_COLS, p["w_gate"], p["b_gla_gate"], p["g_gla_out"])
    att = [attn_fwd(proj[g], A_Q0 if g == 0 else 0, g) for g in range(3)]
    outs = [_in_order(att[g][0], ATT_GROUPS[g][1]) for g in range(3)]
    lses = [_in_order(att[g][1], ATT_GROUPS[g][1]) for g in range(3)]
    x1, merged, yc = merge_fwd(x0, proj[0], ya, yb, outs, lses, p["w_branch"], p["w_out"])
    x2 = cross_fwd(x1, p["g_cross"], p["w_xq"], p["w_xo"], kv)
    x3 = mlp_fwd(x2, p["g_mlp"], p["w_up"], p["w_down"])
    saved = dict(x0=x0, x1=x1, x2=x2, hs=hs, proj=proj, ya=ya, xs=xs, yb=yb, gst=gst, att=att, outs=outs, lses=lses,
                 merged=merged, yc=yc)
    return x3, saved


def _layer_bwd(dx3, p, kv, memn, sv):
    g = {}
    tn = lambda a, b, **kw: matmul(a, b, ta=True, out_dtype=BF16, name="wgrad", **kw)
    dx2, g["g_mlp"], h3, dup, act = mlp_bwd(sv["x2"], p["g_mlp"], p["w_up"], p["w_down"], dx3)
    g["w_up"], g["w_down"] = tn(h3, dup), tn(act, dx3)
    dx1, g["g_cross"], h2, dq, o, dkv = cross_bwd(sv["x1"], p["g_cross"], p["w_xq"], p["w_xo"], kv, dx2)
    g["w_xq"], g["w_xo"], g["w_xkv"] = tn(h2, dq), tn(o, dx2), tn(memn, dkv)
    dmemn = matmul(dkv, p["w_xkv"], tb=True, name="dmem")
    proj = sv["proj"]
    r = merge_bwd(dx1, proj[0], sv["ya"], sv["yb"], sv["yc"], sv["outs"], sv["lses"], p["w_branch"], p["w_out"])
    dgl, dz, dya, dyb, douts, dlses = r[0], r[1:4], r[4], r[5], r[6:9], r[9:12]
    g["w_branch"] = jnp.stack([tn(y, dz[n]) for n, y in enumerate((sv["ya"], sv["yb"], sv["yc"]))])
    g["w_out"] = tn(sv["merged"], dx1)
    datt = []
    for k in range(3):
        dil = ATT_GROUPS[k][1]
        datt.append(attn_bwd(proj[k], A_Q0 if k == 0 else 0, k, sv["att"][k][0], sv["att"][k][1],
                             _by_residue(douts[k], dil), _by_residue(dlses[k], dil)))
    du, adj, gg, dpre, dy, dlam, g["s5_d"], g["b_glu"] = s5_bwd(proj[0], A_U, sv["xs"], p["bd"], p["cd"], p["lam"], p["s5_d"],
                                                                 p["w_glu"], p["b_glu"], dya)
    dbd = matmul(proj[0], adj, ta=True, a_col=(A_U, MIX_W), name="s5_dbd")
    dcd = matmul(sv["xs"], dy, ta=True, name="s5_dcd")
    g["w_glu"] = tn(gg, dpre)
    (g["s5_a_re"], g["s5_a_im"], g["s5_log_step"], g["s5_b_re"], g["s5_b_im"], g["s5_c_re"],
     g["s5_c_im"]) = s5_param_grads(p["disc_in"], dlam, dbd, dcd)
    dqg, dkg, dvg, dlr, drg, dwg, g["b_gla_gate"], g["g_gla_out"] = gla_bwd(proj[0], GLA_COLS, p["w_gate"], p["b_gla_gate"],
                                                                             p["g_gla_out"], sv["gst"], dyb)
    g["w_gla_gate"] = dwg[:GLA_GATE_RANK]
    s = dx3.shape[0]
    dproj = [jnp.concatenate([dgl, du, dqg, dkg, dvg, drg, *datt[0], dlr, jnp.zeros((s, A_W - A_LR - LR_PAD), BF16)], axis=1),
             jnp.concatenate(datt[1], axis=1), jnp.concatenate(datt[2], axis=1)]
    g["w_in"] = _join_w_in(*[tn(sv["hs"][k], dproj[k]) for k in range(3)])
    dhs = [_in_order(matmul(dproj[k], p["w_in_seg"][k], tb=True, name="in_proj_bwd"), ATT_GROUPS[k][1]) for k in range(3)]
    dx0, g["g_mix"] = rmsnorm_bwd(sv["x0"], p["g_mix"], dhs, dx1, name="mix_norm_bwd")
    return dx0, g, dmemn


GLA_GATE_RANK = 16


def kernel(x, mem, *rest):
    nw = len(WEIGHTS)
    w = dict(zip(WEIGHTS, rest[:nw]))
    target = rest[nw]
    m = dict(zip(WEIGHTS, rest[nw + 1 : 2 * nw + 1]))
    v = dict(zip(WEIGHTS, rest[2 * nw + 1 : 3 * nw + 1]))
    x0, memx, target = x[0], mem[0], target[0]
    chip = 2 * lax.axis_index("x") + lax.axis_index("y")

    big = list(SHARDED)
    gate_pad = jnp.pad(w["w_gla_gate"], ((0, 0), (0, 0), (0, LR_PAD - w["w_gla_gate"].shape[2])))
    flat2 = lambda t: t.reshape(DEPTH, -1, t.shape[-1])
    gathered = all_gather([flat2(w[n].astype(BF16)) for n in big] + [gate_pad], by_core=True, name="gather_weights")
    gathered = [t.reshape(N_CHIPS, DEPTH, *s.shape[1:]) for t, s in zip(gathered, [w[n] for n in big] + [gate_pad])]
    full = {n: [jnp.concatenate([t[j, l] for j in range(N_CHIPS)], axis=SHARDED[n]) for l in range(DEPTH)]
            for n, t in zip(big, gathered)}
    gate_full = [jnp.concatenate([gathered[-1][j, l][:, : w["w_gla_gate"].shape[2]] for j in range(N_CHIPS)], axis=1)
                 for l in range(DEPTH)]

    memn = rmsnorm(memx, w["g_mem"], name="mem_norm")
    params, kvs = [], []
    for l in range(DEPTH):
        p = {n: full[n][l] for n in big if n != "w_in"}
        p["w_in_seg"] = _split_w_in(full["w_in"][l])
        p["w_gate"] = jnp.pad(gate_full[l], ((0, LR_PAD - GLA_GATE_RANK), (0, 0))).astype(BF16)
        for n in ("g_mix", "g_cross", "g_mlp"):
            p[n] = w[n][l]
        for n in ("s5_d", "b_glu", "b_gla_gate", "g_gla_out"):
            p[n] = w[n][l].reshape(1, -1)
        p["disc_in"], p["lam"], p["bd"], p["cd"] = s5_prepare(*[w[n][l] for n in WEIGHTS[2:9]])
        params.append(p)
        kvs.append(matmul(memn, p["w_xkv"], out_dtype=BF16, name="mem_kv"))

    xl, saved = x0, []
    for l in range(DEPTH):
        xl, sv = _layer_fwd(xl, params[l], kvs[l])
        saved.append(sv)
    loss8, dx, dg_final = loss_head(xl, w["g_final"], target)
    grads, dmem = [None] * DEPTH, []
    for l in reversed(range(DEPTH)):
        dx, grads[l], dm = _layer_bwd(dx, params[l], kvs[l], memn, saved[l])
        dmem.append(dm)
    _, dg_mem = rmsnorm_bwd(memx, w["g_mem"], dmem, jnp.zeros_like(memx), name="mem_norm_bwd")

    core = lax.axis_index("c")

    def halves(n):
        ax = SHARDED[n]
        keep, send = [], []
        for j in range(N_CHIPS):
            p0, p1 = [jnp.split(grads[l][n], N_CHIPS, axis=ax)[j] for l in range(DEPTH)]
            keep.append(jnp.where(core == 0, p0, p1))
            send.append(jnp.where(core == 0, p1, p0))
        flat = lambda ps: jnp.stack(ps).reshape(N_CHIPS, -1, ps[0].shape[-1])
        return flat(keep), flat(send)

    kept, sent = zip(*[halves(n) for n in big])
    theirs = sibling_swap(list(sent), name="swap_layers")
    chip_sums = [add_pair(a, b, name="add_cores") for a, b in zip(kept, theirs)]
    landed = chip_exchange(chip_sums, name="exchange_grads")
    reduced = [sum_blocks(t, name="sum_grads") for t in landed]
    other = sibling_swap([t[None] for t in reduced], name="pair_layers")
    out_g = {}
    for n, mine, theirs in zip(big, reduced, other):
        both = jnp.where(core == 0, jnp.stack([mine, theirs[0]]), jnp.stack([theirs[0], mine]))
        out_g[n] = both.reshape(w[n].shape)

    small = [n for n in WEIGHTS if n not in SHARDED]
    local_small = []
    for n in small:
        if n == "g_mem":
            local_small.append(dg_mem.reshape(w[n].shape))
        elif n == "g_final":
            local_small.append(dg_final.reshape(w[n].shape))
        elif n == "w_gla_gate":
            local_small.append(jnp.stack([grads[l][n] for l in range(DEPTH)]))
        else:
            local_small.append(jnp.stack([grads[l][n].reshape(w[n].shape[1:]) for l in range(DEPTH)]))
    packed = _pack(local_small + [loss8[0, :1]])
    (every,) = all_gather([packed], name="gather_small")
    summed = _unpack(sum_blocks(every, name="sum_small"), local_small + [loss8[0, :1]])
    loss = summed[-1].reshape(())
    for n, t in zip(small, summed[:-1]):
        if n == "w_gla_gate":
            t = lax.dynamic_slice_in_dim(t, chip * w[n].shape[2], w[n].shape[2], axis=2)
        out_g[n] = t

    delta, new_m, new_v = {}, {}, {}
    for n in big:
        c = w[n].shape[-1]
        r = [t.reshape(w[n].shape) for t in adamw(w[n].reshape(-1, c), out_g[n].reshape(-1, c), m[n].reshape(-1, c),
                                                  v[n].reshape(-1, c), name="adamw")]
        delta[n], new_m[n], new_v[n] = r
    like = [w[n] for n in small]
    r = adamw(_pack(like), _pack([out_g[n] for n in small]), _pack([m[n] for n in small]), _pack([v[n] for n in small]),
              name="adamw_small")
    for d, t in zip((delta, new_m, new_v), r):
        d.update(zip(small, _unpack(t, like)))
    return (loss, dx[None], *[out_g[n] for n in WEIGHTS], *[delta[n] for n in WEIGHTS], *[new_m[n] for n in WEIGHTS],
            *[new_v[n] for n in WEIGHTS])
```

```python
import functools
import math

import jax
import jax.numpy as jnp
import numpy as np
from jax import lax
from jax.experimental import pallas as pl
from jax.experimental.pallas import tpu as pltpu

F32 = jnp.float32
BF16 = jnp.bfloat16

VMEM_LIMIT_BYTES = 56 * 1024 * 1024
MATMUL_VMEM_BYTES = 36 * 1024 * 1024


def _cparams(sem):
    return pltpu.CompilerParams(dimension_semantics=sem, vmem_limit_bytes=VMEM_LIMIT_BYTES)


def _dot(a, b, ca=1, cb=0):
    return lax.dot_general(a.astype(BF16), b.astype(BF16), (((ca,), (cb,)), ((), ())), preferred_element_type=F32)


def _pick(n, prefs):
    for p in prefs:
        if n % p == 0:
            return p
    return n


def matmul(a, b, *, ta=False, tb=False, out_dtype=F32, name="mm", a_col=None, b_col=None):
    a_off, a_w = a_col if a_col is not None else (0, a.shape[1])
    b_off, b_w = b_col if b_col is not None else (0, b.shape[1])
    if ta:
        kk, m = a.shape[0], a_w
    else:
        m, kk = a.shape[0], a_w
    if tb:
        n, kb = b.shape[0], b_w
    else:
        kb, n = b.shape[0], b_w
    assert kk == kb, (a.shape, b.shape, ta, tb)
    tm = _pick(m, (512, 256, 128))
    tn = _pick(n, (1024, 512, 256, 128))
    sa, sb, so = a.dtype.itemsize, b.dtype.itemsize, jnp.dtype(out_dtype).itemsize
    fits = lambda t: 2 * (tm * t * sa + t * tn * sb) + tm * tn * (4 + 2 * so) <= MATMUL_VMEM_BYTES
    tk = max([t for t in range(128, kk + 1, 128) if kk % t == 0 and fits(t)] or [_pick(kk, (128,))])
    nk = kk // tk
    a_bytes, b_bytes = m * kk * sa, kk * n * sb
    n_outer = nk == 1 and b_bytes + a_bytes * (n // tn) < a_bytes + b_bytes * (m // tm)
    ij = (lambda g0, g1: (g1, g0)) if n_outer else (lambda g0, g1: (g0, g1))
    if ta:
        assert a_off % tm == 0
        a_spec = pl.BlockSpec((tk, tm), lambda g0, g1, k, o=a_off // tm: (k, ij(g0, g1)[0] + o))
    else:
        assert a_off % tk == 0
        a_spec = pl.BlockSpec((tm, tk), lambda g0, g1, k, o=a_off // tk: (ij(g0, g1)[0], k + o))
    if tb:
        assert b_off % tk == 0
        b_spec = pl.BlockSpec((tn, tk), lambda g0, g1, k, o=b_off // tk: (ij(g0, g1)[1], k + o))
    else:
        assert b_off % tn == 0
        b_spec = pl.BlockSpec((tk, tn), lambda g0, g1, k, o=b_off // tn: (k, ij(g0, g1)[1] + o))

    def body(a_ref, b_ref, o_ref, *acc):
        k = pl.program_id(2)
        p = _dot(a_ref[...], b_ref[...], 0 if ta else 1, 1 if tb else 0)
        if nk == 1:
            o_ref[...] = p.astype(o_ref.dtype)
            return
        (acc_ref,) = acc

        @pl.when(k == 0)
        def _():
            acc_ref[...] = p

        @pl.when(jnp.logical_and(k > 0, k < nk - 1))
        def _():
            acc_ref[...] += p

        @pl.when(k == nk - 1)
        def _():
            o_ref[...] = (acc_ref[...] + p).astype(o_ref.dtype)

    return pl.pallas_call(
        body,
        name=name,
        grid=(n // tn, m // tm, nk) if n_outer else (m // tm, n // tn, nk),
        in_specs=[a_spec, b_spec],
        out_specs=pl.BlockSpec((tm, tn), lambda g0, g1, k: ij(g0, g1)),
        out_shape=jax.ShapeDtypeStruct((m, n), out_dtype),
        scratch_shapes=[pltpu.VMEM((tm, tn), F32)] if nk > 1 else [],
        compiler_params=_cparams(("parallel", "parallel", "arbitrary")),
    )(a, b)


RMS_EPS = 1e-6
ROW_TILE = 512


def _rms_fwd(x, g):
    r = lax.rsqrt(jnp.mean(x * x, axis=-1, keepdims=True) + RMS_EPS)
    return x * r * g


def _rms_bwd(x, g, dh):
    r = lax.rsqrt(jnp.mean(x * x, axis=-1, keepdims=True) + RMS_EPS)
    xh = x * r
    dg = jnp.sum(dh * xh, axis=0, keepdims=True)
    dxh = dh * g
    dx = r * (dxh - xh * jnp.mean(dxh * xh, axis=-1, keepdims=True))
    return dx, dg


def _accum(ref, val, first):
    @pl.when(first)
    def _():
        ref[...] = val

    @pl.when(jnp.logical_not(first))
    def _():
        ref[...] += val


def rmsnorm(x, g, *, out_dtype=BF16, name="rmsnorm"):
    s, d = x.shape
    tm = _pick(s, (ROW_TILE, 256, 128, 8))

    def body(x_ref, g_ref, o_ref):
        o_ref[...] = _rms_fwd(x_ref[...], g_ref[...]).astype(o_ref.dtype)

    return pl.pallas_call(
        body,
        name=name,
        grid=(s // tm,),
        in_specs=[pl.BlockSpec((tm, d), lambda i: (i, 0)), pl.BlockSpec((1, d), lambda i: (0, 0))],
        out_specs=pl.BlockSpec((tm, d), lambda i: (i, 0)),
        out_shape=jax.ShapeDtypeStruct((s, d), out_dtype),
        compiler_params=_cparams(("arbitrary",)),
    )(x, g.reshape(1, d))


def rmsnorm_bwd(x, g, dhs, dres, *, name="rmsnorm_bwd"):
    s, d = x.shape
    tm = _pick(s, (ROW_TILE, 256, 128, 8))
    n = len(dhs)

    def body(x_ref, g_ref, *refs):
        dh_refs, dres_ref, dx_ref, dg_ref = refs[:n], refs[n], refs[n + 1], refs[n + 2]
        dh = dh_refs[0][...].astype(F32)
        for r in dh_refs[1:]:
            dh = dh + r[...].astype(F32)
        dx, dg = _rms_bwd(x_ref[...], g_ref[...], dh)
        dx_ref[...] = dres_ref[...] + dx
        _accum(dg_ref, dg, pl.program_id(0) == 0)

    row = pl.BlockSpec((tm, d), lambda i: (i, 0))
    vec = pl.BlockSpec((1, d), lambda i: (0, 0))
    return pl.pallas_call(
        body,
        name=name,
        grid=(s // tm,),
        in_specs=[row, vec] + [row] * n + [row],
        out_specs=[row, vec],
        out_shape=[jax.ShapeDtypeStruct((s, d), F32), jax.ShapeDtypeStruct((1, d), F32)],
        compiler_params=_cparams(("arbitrary",)),
    )(x, g.reshape(1, d), *dhs, dres)


def loss_head(x, g, target, *, name="loss_head"):
    s, d = x.shape
    tm = _pick(s, (ROW_TILE, 256, 128, 8))

    def body(x_ref, g_ref, t_ref, l_ref, dx_ref, dg_ref):
        x_, g_ = x_ref[...], g_ref[...]
        e = _rms_fwd(x_, g_) - t_ref[...]
        part = 0.5 * jnp.sum(jnp.sum(e * e, axis=-1, keepdims=True), axis=0, keepdims=True) / d
        dx, dg = _rms_bwd(x_, g_, e * (1.0 / d))
        dx_ref[...] = dx
        first = pl.program_id(0) == 0
        _accum(dg_ref, dg, first)
        _accum(l_ref, jnp.broadcast_to(part, (8, 128)), first)

    row = pl.BlockSpec((tm, d), lambda i: (i, 0))
    vec = pl.BlockSpec((1, d), lambda i: (0, 0))
    return pl.pallas_call(
        body,
        name=name,
        grid=(s // tm,),
        in_specs=[row, vec, row],
        out_specs=[pl.BlockSpec((8, 128), lambda i: (0, 0)), row, vec],
        out_shape=[jax.ShapeDtypeStruct((8, 128), F32), jax.ShapeDtypeStruct((s, d), F32), jax.ShapeDtypeStruct((1, d), F32)],
        compiler_params=_cparams(("arbitrary",)),
    )(x, g.reshape(1, d), target)


FF_TILE = 1024


def mlp_fwd(x, g, w_up, w_down, *, name="mlp_fwd"):
    s, d = x.shape
    ff = w_up.shape[1]
    tm, tf = _pick(s, (ROW_TILE, 256, 128)), _pick(ff, (FF_TILE, 512, 256, 128))
    nf = ff // tf

    def body(x_ref, g_ref, wu_ref, wd_ref, o_ref, h_ref, acc_ref):
        f = pl.program_id(1)

        @pl.when(f == 0)
        def _():
            h_ref[...] = _rms_fwd(x_ref[...], g_ref[...]).astype(BF16)
            acc_ref[...] = x_ref[...]

        up = jnp.maximum(_dot(h_ref[...], wu_ref[...]), 0.0)
        acc_ref[...] += _dot(up * up, wd_ref[...])

        @pl.when(f == nf - 1)
        def _():
            o_ref[...] = acc_ref[...]

    return pl.pallas_call(
        body,
        name=name,
        grid=(s // tm, nf),
        in_specs=[
            pl.BlockSpec((tm, d), lambda i, f: (i, 0)),
            pl.BlockSpec((1, d), lambda i, f: (0, 0)),
            pl.BlockSpec((d, tf), lambda i, f: (0, f)),
            pl.BlockSpec((tf, d), lambda i, f: (f, 0)),
        ],
        out_specs=pl.BlockSpec((tm, d), lambda i, f: (i, 0)),
        out_shape=jax.ShapeDtypeStruct((s, d), F32),
        scratch_shapes=[pltpu.VMEM((tm, d), BF16), pltpu.VMEM((tm, d), F32)],
        compiler_params=_cparams(("arbitrary", "arbitrary")),
    )(x, g.reshape(1, d), w_up, w_down)


def mlp_bwd(x, g, w_up, w_down, dy, *, name="mlp_bwd"):
    s, d = x.shape
    ff = w_up.shape[1]
    tm, tf = _pick(s, (ROW_TILE, 256, 128)), _pick(ff, (FF_TILE, 512, 256, 128))
    nf = ff // tf

    def body(x_ref, g_ref, wu_ref, wd_ref, dy_ref, dx_ref, dg_ref, h_ref, dup_ref, act_ref, dyb_ref, acc_ref):
        i, f = pl.program_id(0), pl.program_id(1)

        @pl.when(f == 0)
        def _():
            h_ref[...] = _rms_fwd(x_ref[...], g_ref[...]).astype(BF16)
            dyb_ref[...] = dy_ref[...].astype(BF16)
            acc_ref[...] = jnp.zeros_like(acc_ref)

        up = jnp.maximum(_dot(h_ref[...], wu_ref[...]), 0.0)
        dact = _dot(dyb_ref[...], wd_ref[...], 1, 1)
        dup = (2.0 * up * dact).astype(BF16)
        dup_ref[...] = dup
        act_ref[...] = (up * up).astype(BF16)
        acc_ref[...] += _dot(dup, wu_ref[...], 1, 1)

        @pl.when(f == nf - 1)
        def _():
            dx, dg = _rms_bwd(x_ref[...], g_ref[...], acc_ref[...])
            dx_ref[...] = dy_ref[...] + dx
            _accum(dg_ref, dg, i == 0)

    row = pl.BlockSpec((tm, d), lambda i, f: (i, 0))
    vec = pl.BlockSpec((1, d), lambda i, f: (0, 0))
    wide = pl.BlockSpec((tm, tf), lambda i, f: (i, f))
    return pl.pallas_call(
        body,
        name=name,
        grid=(s // tm, nf),
        in_specs=[row, vec, pl.BlockSpec((d, tf), lambda i, f: (0, f)), pl.BlockSpec((tf, d), lambda i, f: (f, 0)), row],
        out_specs=[row, vec, row, wide, wide, row],
        out_shape=[
            jax.ShapeDtypeStruct((s, d), F32),
            jax.ShapeDtypeStruct((1, d), F32),
            jax.ShapeDtypeStruct((s, d), BF16),
            jax.ShapeDtypeStruct((s, ff), BF16),
            jax.ShapeDtypeStruct((s, ff), BF16),
            jax.ShapeDtypeStruct((s, d), BF16),
        ],
        scratch_shapes=[pltpu.VMEM((tm, d), F32)],
        compiler_params=_cparams(("arbitrary", "arbitrary")),
    )(x, g.reshape(1, d), w_up, w_down, dy)


X_HEADS = 4


def _softmax_rows(s):
    m = jnp.max(s, axis=-1, keepdims=True)
    e = jnp.exp(s - m)
    return e / jnp.sum(e, axis=-1, keepdims=True)


def cross_fwd(x, g, wq, wo, kv, *, name="cross_fwd"):
    s, d = x.shape
    ml = kv.shape[0]
    dh = d // X_HEADS
    tm = _pick(s, (ROW_TILE, 256, 128))
    scale = dh**-0.5

    def body(x_ref, g_ref, wq_ref, wo_ref, kv_ref, o_ref):
        x_ = x_ref[...]
        q = _dot(_rms_fwd(x_, g_ref[...]), wq_ref[...])
        outs = []
        for hd in range(X_HEADS):
            kh = kv_ref[:, hd * dh : (hd + 1) * dh]
            vh = kv_ref[:, d + hd * dh : d + (hd + 1) * dh]
            p = _softmax_rows(_dot(q[:, hd * dh : (hd + 1) * dh], kh, 1, 1) * scale)
            outs.append(_dot(p, vh))
        o_ref[...] = x_ + _dot(jnp.concatenate(outs, axis=-1), wo_ref[...])

    row = pl.BlockSpec((tm, d), lambda i: (i, 0))
    full = lambda shp: pl.BlockSpec(shp, lambda i: (0, 0))
    return pl.pallas_call(
        body,
        name=name,
        grid=(s // tm,),
        in_specs=[row, full((1, d)), full((d, d)), full((d, d)), full((ml, 2 * d))],
        out_specs=row,
        out_shape=jax.ShapeDtypeStruct((s, d), F32),
        compiler_params=_cparams(("arbitrary",)),
    )(x, g.reshape(1, d), wq, wo, kv)


def cross_bwd(x, g, wq, wo, kv, dy, *, name="cross_bwd"):
    s, d = x.shape
    ml = kv.shape[0]
    dh = d // X_HEADS
    tm = _pick(s, (ROW_TILE, 256, 128))
    scale = dh**-0.5

    def body(x_ref, g_ref, wq_ref, wo_ref, kv_ref, dy_ref, dx_ref, dg_ref, h_ref, dq_ref, o_ref, dkv_ref):
        first = pl.program_id(0) == 0
        x_, g_ = x_ref[...], g_ref[...]
        h = _rms_fwd(x_, g_).astype(BF16)
        h_ref[...] = h
        q = _dot(h, wq_ref[...])
        dy_ = dy_ref[...]
        do = _dot(dy_, wo_ref[...], 1, 1)
        outs, dqs, dks, dvs = [], [], [], []
        for hd in range(X_HEADS):
            sl = slice(hd * dh, (hd + 1) * dh)
            kh = kv_ref[:, sl]
            vh = kv_ref[:, d + hd * dh : d + (hd + 1) * dh]
            qh = q[:, sl]
            p = _softmax_rows(_dot(qh, kh, 1, 1) * scale)
            outs.append(_dot(p, vh))
            doh = do[:, sl]
            dp = _dot(doh, vh, 1, 1)
            ds = p * (dp - jnp.sum(dp * p, axis=-1, keepdims=True)) * scale
            dqs.append(_dot(ds, kh))
            dks.append(_dot(ds, qh, 0, 0))
            dvs.append(_dot(p, doh, 0, 0))
        o_ref[...] = jnp.concatenate(outs, axis=-1).astype(BF16)
        dq = jnp.concatenate(dqs, axis=-1).astype(BF16)
        dq_ref[...] = dq
        dx, dg = _rms_bwd(x_, g_, _dot(dq, wq_ref[...], 1, 1))
        dx_ref[...] = dy_ + dx
        _accum(dkv_ref, jnp.concatenate(dks + dvs, axis=-1), first)
        _accum(dg_ref, dg, first)

    row = pl.BlockSpec((tm, d), lambda i: (i, 0))
    full = lambda shp: pl.BlockSpec(shp, lambda i: (0, 0))
    rowb = jax.ShapeDtypeStruct((s, d), BF16)
    return pl.pallas_call(
        body,
        name=name,
        grid=(s // tm,),
        in_specs=[row, full((1, d)), full((d, d)), full((d, d)), full((ml, 2 * d)), row],
        out_specs=[row, full((1, d)), row, row, row, full((ml, 2 * d))],
        out_shape=[jax.ShapeDtypeStruct((s, d), F32), jax.ShapeDtypeStruct((1, d), F32), rowb, rowb, rowb,
                   jax.ShapeDtypeStruct((ml, 2 * d), F32)],
        compiler_params=_cparams(("arbitrary",)),
    )(x, g.reshape(1, d), wq, wo, kv, dy)


N_BRANCH = 3
MIX_W = 512
ATT_DH = 128
ATT_HG = 4
MERGE_TILE = 256


def _group_weights(l0, l1, l2):
    m = jnp.maximum(jnp.maximum(l0, l1), l2)
    e = [jnp.exp(l0 - m), jnp.exp(l1 - m), jnp.exp(l2 - m)]
    inv = 1.0 / (e[0] + e[1] + e[2])
    return [t * inv for t in e]


def _sigmoid(x):
    return 0.5 * (jnp.tanh(0.5 * x) + 1.0)


def merge_fwd(x, proj, ya, yb, outs, lses, wb, wout, *, name="merge_fwd"):
    s, d = x.shape
    tm = _pick(s, (MERGE_TILE, 128))

    def body(x_ref, g0, g1, g2, ya_ref, yb_ref, o0, o1, o2, l0, l1, l2, wb_ref, wo_ref, x1_ref, mg_ref, yc_ref):
        w = _group_weights(l0[...], l1[...], l2[...])
        o = (o0[...], o1[...], o2[...])
        heads = []
        for hd in range(ATT_HG):
            sl = slice(hd * ATT_DH, (hd + 1) * ATT_DH)
            heads.append(sum(w[gi][:, hd : hd + 1] * o[gi][:, sl] for gi in range(3)))
        yc = jnp.concatenate(heads, axis=-1)
        yc_ref[...] = yc.astype(BF16)
        merged = None
        for n, (y, gl) in enumerate(((ya_ref[...], g0), (yb_ref[...], g1), (yc, g2))):
            t = _sigmoid(gl[...]) * _dot(y, wb_ref[n])
            merged = t if merged is None else merged + t
        mb = merged.astype(BF16)
        mg_ref[...] = mb
        x1_ref[...] = x_ref[...] + _dot(mb, wo_ref[...])

    row = pl.BlockSpec((tm, d), lambda i: (i, 0))
    half = pl.BlockSpec((tm, MIX_W), lambda i: (i, 0))
    perhead = pl.BlockSpec((tm, ATT_HG), lambda i: (i, 0))
    gate = [pl.BlockSpec((tm, d), lambda i, n=n: (i, n)) for n in range(N_BRANCH)]
    return pl.pallas_call(
        body,
        name=name,
        grid=(s // tm,),
        in_specs=[row] + gate + [half] * 5 + [perhead] * 3
        + [pl.BlockSpec((N_BRANCH, MIX_W, d), lambda i: (0, 0, 0)), pl.BlockSpec((d, d), lambda i: (0, 0))],
        out_specs=[row, row, half],
        out_shape=[jax.ShapeDtypeStruct((s, d), F32), jax.ShapeDtypeStruct((s, d), BF16), jax.ShapeDtypeStruct((s, MIX_W), BF16)],
        compiler_params=_cparams(("arbitrary",)),
    )(x, proj, proj, proj, ya, yb, *outs, *lses, wb, wout)


def merge_bwd(dx1, proj, ya, yb, yc, outs, lses, wb, wout, *, name="merge_bwd"):
    s, d = dx1.shape
    tm = _pick(s, (MERGE_TILE, 128))

    def body(dx_ref, g0, g1, g2, ya_ref, yb_ref, yc_ref, o0, o1, o2, l0, l1, l2, wb_ref, wo_ref,
             dgl_ref, dz0, dz1, dz2, dya_ref, dyb_ref, do0, do1, do2, dl0, dl1, dl2):
        dm = _dot(dx_ref[...], wo_ref[...], 1, 1)
        dys = []
        for n, (y, gl, dz_ref) in enumerate(((ya_ref, g0, dz0), (yb_ref, g1, dz1), (yc_ref, g2, dz2))):
            z = _dot(y[...], wb_ref[n])
            sg = _sigmoid(gl[...])
            dz = (dm * sg).astype(BF16)
            dz_ref[...] = dz
            dgl_ref[:, n * d : (n + 1) * d] = (dm * z * sg * (1.0 - sg)).astype(BF16)
            dys.append(_dot(dz, wb_ref[n], 1, 1))
        dya_ref[...] = dys[0]
        dyb_ref[...] = dys[1]
        dyc = dys[2]
        w = _group_weights(l0[...], l1[...], l2[...])
        o = [o0[...], o1[...], o2[...]]
        for hd in range(ATT_HG):
            sl = slice(hd * ATT_DH, (hd + 1) * ATT_DH)
            col = slice(hd, hd + 1)
            wh = [w[gi][:, col] for gi in range(3)]
            for gi, r in enumerate((do0, do1, do2)):
                r[:, sl] = wh[gi] * dyc[:, sl]
            t = [jnp.sum(dyc[:, sl] * o[gi][:, sl], axis=-1, keepdims=True) for gi in range(3)]
            tbar = wh[0] * t[0] + wh[1] * t[1] + wh[2] * t[2]
            for gi, r in enumerate((dl0, dl1, dl2)):
                r[:, col] = wh[gi] * (t[gi] - tbar)

    row = pl.BlockSpec((tm, d), lambda i: (i, 0))
    half = pl.BlockSpec((tm, MIX_W), lambda i: (i, 0))
    perhead = pl.BlockSpec((tm, ATT_HG), lambda i: (i, 0))
    gate = [pl.BlockSpec((tm, d), lambda i, n=n: (i, n)) for n in range(N_BRANCH)]
    rb = jax.ShapeDtypeStruct((s, d), BF16)
    hf = jax.ShapeDtypeStruct((s, MIX_W), F32)
    ph = jax.ShapeDtypeStruct((s, ATT_HG), F32)
    return pl.pallas_call(
        body,
        name=name,
        grid=(s // tm,),
        in_specs=[row] + gate + [half] * 6 + [perhead] * 3
        + [pl.BlockSpec((N_BRANCH, MIX_W, d), lambda i: (0, 0, 0)), pl.BlockSpec((d, d), lambda i: (0, 0))],
        out_specs=[pl.BlockSpec((tm, N_BRANCH * d), lambda i: (i, 0)), row, row, row] + [half] * 5 + [perhead] * 3,
        out_shape=[jax.ShapeDtypeStruct((s, N_BRANCH * d), BF16), rb, rb, rb] + [hf] * 5 + [ph] * 3,
        compiler_params=_cparams(("arbitrary",)),
    )(dx1, proj, proj, proj, ya, yb, yc, *outs, *lses, wb, wout)


ATT_BLOCK = 128
ATT_STEP = 4 * ATT_BLOCK
ATT_GROUPS = ((128, 1), (512, 4), (2048, 16))
N_ATT_HEADS = ATT_HG * len(ATT_GROUPS)
ALIBI_MAX_EXP = 8.0
MASKED = -1e30


def _att_slopes(group):
    return [2.0 ** (-ALIBI_MAX_EXP * (group * ATT_HG + h + 1) / N_ATT_HEADS) for h in range(ATT_HG)]


def _att_scores(q, kk, slope_dil, has_prev):
    scale = ATT_DH**-0.5
    qi = lax.broadcasted_iota(jnp.int32, (ATT_BLOCK, 2 * ATT_BLOCK), 0)
    kj = lax.broadcasted_iota(jnp.int32, (ATT_BLOCK, 2 * ATT_BLOCK), 1)
    dist = qi + ATT_BLOCK - kj
    valid = jnp.logical_and(jnp.logical_and(dist >= 0, dist <= ATT_BLOCK), jnp.logical_or(kj >= ATT_BLOCK, has_prev))
    return jnp.where(valid, _dot(q, kk, 1, 1) * scale - slope_dil * dist.astype(F32), MASKED)


def _att_window(kp_ref, kc_ref, buf):
    buf[:ATT_BLOCK, :] = kp_ref[...]
    buf[ATT_BLOCK:, :] = kc_ref[...]


def attn_fwd(qkv, col, group, *, name="attn_fwd"):
    s = qkv.shape[0]
    window, dil = ATT_GROUPS[group]
    assert window // dil == ATT_BLOCK and s % (dil * ATT_BLOCK) == 0 and col % MIX_W == 0
    nb = s // dil // ATT_BLOCK
    nq = ATT_STEP // ATT_BLOCK
    ng = s // ATT_STEP
    slopes = _att_slopes(group)
    c0 = col // MIX_W

    def body(q_ref, kp_ref, kc_ref, vp_ref, vc_ref, o_ref, l_ref, kbuf, vbuf):
        n = pl.program_id(0)
        _att_window(kp_ref, kc_ref, kbuf)
        _att_window(vp_ref, vc_ref, vbuf)
        for i in range(nq):
            r = slice(i * ATT_BLOCK, (i + 1) * ATT_BLOCK)
            win = slice(i * ATT_BLOCK, (i + 2) * ATT_BLOCK)
            has_prev = ((n * nq + i) % nb) != 0
            for hd in range(ATT_HG):
                sl = slice(hd * ATT_DH, (hd + 1) * ATT_DH)
                sc = _att_scores(q_ref[r, sl], kbuf[win, sl], slopes[hd] * dil, has_prev)
                m = jnp.max(sc, axis=-1, keepdims=True)
                e = jnp.exp(sc - m)
                den = jnp.sum(e, axis=-1, keepdims=True)
                o_ref[r, sl] = _dot(e * (1.0 / den), vbuf[win, sl])
                l_ref[r, hd : hd + 1] = m + jnp.log(den)

    cur = lambda c: pl.BlockSpec((ATT_STEP, MIX_W), lambda n, c=c: (n, c))
    prev = lambda c: pl.BlockSpec((ATT_BLOCK, MIX_W), lambda n, c=c: (jnp.maximum(n * nq - 1, 0), c))
    blk = pl.BlockSpec((ATT_STEP, MIX_W), lambda n: (n, 0))
    return pl.pallas_call(
        body,
        name=name,
        grid=(ng,),
        in_specs=[cur(c0), prev(c0 + 1), cur(c0 + 1), prev(c0 + 2), cur(c0 + 2)],
        out_specs=[blk, pl.BlockSpec((ATT_STEP, ATT_HG), lambda n: (n, 0))],
        out_shape=[jax.ShapeDtypeStruct((s, MIX_W), F32), jax.ShapeDtypeStruct((s, ATT_HG), F32)],
        scratch_shapes=[pltpu.VMEM((ATT_BLOCK + ATT_STEP, MIX_W), F32)] * 2,
        compiler_params=_cparams(("arbitrary",)),
    )(qkv, qkv, qkv, qkv, qkv)


def attn_bwd(qkv, col, group, out, lse, dout, dlse, *, name="attn_bwd"):
    s = qkv.shape[0]
    window, dil = ATT_GROUPS[group]
    nb = s // dil // ATT_BLOCK
    nq = ATT_STEP // ATT_BLOCK
    ng = s // ATT_STEP
    slopes = _att_slopes(group)
    c0 = col // MIX_W
    scale = ATT_DH**-0.5
    tail = slice((nq - 1) * ATT_BLOCK, nq * ATT_BLOCK)

    def body(q_ref, kp_ref, kc_ref, vp_ref, vc_ref, o_ref, l_ref, do_ref, dl_ref, dq_ref, dk_ref, dv_ref, acck, accv,
             kbuf, vbuf):
        n = pl.program_id(0)
        live = n < ng

        @pl.when(n == 0)
        def _():
            acck[...] = jnp.zeros_like(acck)
            accv[...] = jnp.zeros_like(accv)

        _att_window(kp_ref, kc_ref, kbuf)
        _att_window(vp_ref, vc_ref, vbuf)
        dk_ref[: (nq - 1) * ATT_BLOCK, :] = acck[: (nq - 1) * ATT_BLOCK, :].astype(BF16)
        dv_ref[: (nq - 1) * ATT_BLOCK, :] = accv[: (nq - 1) * ATT_BLOCK, :].astype(BF16)
        for i in range(nq):
            r = slice(i * ATT_BLOCK, (i + 1) * ATT_BLOCK)
            rp = slice((i - 1) * ATT_BLOCK, i * ATT_BLOCK)
            win = slice(i * ATT_BLOCK, (i + 2) * ATT_BLOCK)
            has_prev = ((jnp.minimum(n, ng - 1) * nq + i) % nb) != 0
            for hd in range(ATT_HG):
                sl = slice(hd * ATT_DH, (hd + 1) * ATT_DH)
                q, do, kk, vv = q_ref[r, sl], do_ref[r, sl], kbuf[win, sl], vbuf[win, sl]
                p = jnp.exp(_att_scores(q, kk, slopes[hd] * dil, has_prev) - l_ref[r, hd : hd + 1])
                corr = dl_ref[r, hd : hd + 1] - jnp.sum(do * o_ref[r, sl], axis=-1, keepdims=True)
                ds = p * (_dot(do, vv, 1, 1) + corr) * scale
                dq_ref[r, sl] = _dot(ds, kk).astype(BF16)
                dkk = jnp.where(live, _dot(ds, q, 0, 0), 0.0)
                dvv = jnp.where(live, _dot(p, do, 0, 0), 0.0)
                if i == 0:
                    dk_ref[tail, sl] = (acck[tail, sl] + dkk[:ATT_BLOCK]).astype(BF16)
                    dv_ref[tail, sl] = (accv[tail, sl] + dvv[:ATT_BLOCK]).astype(BF16)
                else:
                    acck[rp, sl] += dkk[:ATT_BLOCK]
                    accv[rp, sl] += dvv[:ATT_BLOCK]
                acck[r, sl] = dkk[ATT_BLOCK:]
                accv[r, sl] = dvv[ATT_BLOCK:]

    last = ng - 1
    cur = lambda c: pl.BlockSpec((ATT_STEP, MIX_W), lambda n, c=c: (jnp.minimum(n, last), c))
    prev = lambda c: pl.BlockSpec((ATT_BLOCK, MIX_W), lambda n, c=c: (jnp.maximum(jnp.minimum(n, last) * nq - 1, 0), c))
    behind = pl.BlockSpec((ATT_STEP, MIX_W), lambda n: (jnp.clip(n - 1, 0, last), 0))
    perhead = pl.BlockSpec((ATT_STEP, ATT_HG), lambda n: (jnp.minimum(n, last), 0))
    sds = jax.ShapeDtypeStruct((s, MIX_W), BF16)
    return pl.pallas_call(
        body,
        name=name,
        grid=(ng + 1,),
        in_specs=[cur(c0), prev(c0 + 1), cur(c0 + 1), prev(c0 + 2), cur(c0 + 2), cur(0), perhead, cur(0), perhead],
        out_specs=[cur(0), behind, behind],
        out_shape=[sds, sds, sds],
        scratch_shapes=[pltpu.VMEM((ATT_STEP, MIX_W), F32)] * 2 + [pltpu.VMEM((ATT_BLOCK + ATT_STEP, MIX_W), F32)] * 2,
        compiler_params=_cparams(("arbitrary",)),
    )(qkv, qkv, qkv, qkv, qkv, out, lse, dout, dlse)


GLA_HEADS = 4
GLA_DK = 64
GLA_DV = 128
GLA_CHUNK = 64
GLA_TAU = 16.0
GLA_QK = GLA_HEADS * GLA_DK
LR_PAD = 128


def _dot_exact(a, b, ca=1, cb=0, precision=lax.Precision.HIGHEST):
    return lax.dot_general(a, b, (((ca,), (cb,)), ((), ())), precision=precision, preferred_element_type=F32)


GLA_STEP = 4 * GLA_CHUNK


@jax.custom_vjp
def _score_dot(a, b):
    return _dot_exact(a, b, 1, 1)


def _score_dot_fwd(a, b):
    return _dot_exact(a, b, 1, 1), (a, b)


def _score_dot_bwd(res, g):
    a, b = res
    return _dot(g, b), _dot(g, a, 0, 0)


_score_dot.defvjp(_score_dot_fwd, _score_dot_bwd)


def _gla_group(q, k, v, lr, r, state, wg, bg, go):
    c, n = GLA_CHUNK, q.shape[0]
    z = _dot(lr, wg) + bg
    la = (jnp.minimum(z, 0.0) - jnp.log(1.0 + jnp.exp(-jnp.abs(z)))) * (1.0 / GLA_TAU)
    ri = lax.broadcasted_iota(jnp.int32, (n, n), 0)
    ci = lax.broadcasted_iota(jnp.int32, (n, n), 1)
    shift = c.bit_length() - 1
    in_chunk = jnp.logical_and(ri >= ci, jnp.right_shift(ri, shift) == jnp.right_shift(ci, shift))
    ball = _dot_exact(in_chunk.astype(F32), la, precision=lax.Precision.HIGH)
    ri = lax.broadcasted_iota(jnp.int32, (c, c), 0)
    ci = lax.broadcasted_iota(jnp.int32, (c, c), 1)
    causal = ri >= ci
    qs = q * GLA_DK**-0.5
    ys = []
    for j in range(n // c):
        rows = slice(j * c, (j + 1) * c)
        b = ball[rows]
        bmid = b[c // 2 : c // 2 + 1, :]
        bend = b[c - 1 : c, :]
        q_in = qs[rows] * jnp.exp(b)
        q_mid = qs[rows] * jnp.exp(b - bmid)
        k_mid = k[rows] * jnp.exp(bmid - b)
        k_end = k[rows] * jnp.exp(bend - b)
        yh, upd = [], []
        for h in range(GLA_HEADS):
            sk = slice(h * GLA_DK, (h + 1) * GLA_DK)
            sv = slice(h * GLA_DV, (h + 1) * GLA_DV)
            vh = v[rows, sv]
            inter = _dot(q_in[:, sk], state[:, sk], 1, 1)
            sc = jnp.where(causal, _score_dot(q_mid[:, sk], k_mid[:, sk]), 0.0)
            o = inter + _dot(sc, vh)
            o = o * lax.rsqrt(jnp.mean(o * o, axis=-1, keepdims=True) + RMS_EPS) * go
            rh = r[rows, sv]
            yh.append(o * rh * _sigmoid(rh))
            upd.append(_dot(vh, k_end[:, sk], 0, 0))
        state = jnp.exp(bend) * state + jnp.concatenate(upd, axis=-1)
        ys.append(jnp.concatenate(yh, axis=-1))
    return jnp.concatenate(ys, axis=0), state


def _gla_in_specs(cols, rev, nc):
    c = GLA_STEP
    row = (lambda i: nc - 1 - i) if rev else (lambda i: i)
    qc, kc, vc, lc, rc = cols
    assert qc % GLA_QK == 0 and kc % GLA_QK == 0 and vc % MIX_W == 0 and rc % MIX_W == 0 and lc % LR_PAD == 0
    return [
        pl.BlockSpec((c, GLA_QK), lambda i: (row(i), qc // GLA_QK)),
        pl.BlockSpec((c, GLA_QK), lambda i: (row(i), kc // GLA_QK)),
        pl.BlockSpec((c, MIX_W), lambda i: (row(i), vc // MIX_W)),
        pl.BlockSpec((c, LR_PAD), lambda i: (row(i), lc // LR_PAD)),
        pl.BlockSpec((c, MIX_W), lambda i: (row(i), rc // MIX_W)),
    ], row


def gla_fwd(proj, cols, wg, bg, go, *, name="gla_fwd"):
    s = proj.shape[0]
    nc = s // GLA_STEP
    specs, row = _gla_in_specs(cols, False, nc)

    def body(q_ref, k_ref, v_ref, lr_ref, r_ref, wg_ref, bg_ref, go_ref, y_ref, st_ref, state):
        @pl.when(pl.program_id(0) == 0)
        def _():
            state[...] = jnp.zeros_like(state)

        st = state[...]
        st_ref[...] = st
        y, new = _gla_group(q_ref[...], k_ref[...], v_ref[...], lr_ref[...], r_ref[...], st, wg_ref[...], bg_ref[...], go_ref[...])
        y_ref[...] = y
        state[...] = new

    full = lambda shp: pl.BlockSpec(shp, lambda i: (0, 0))
    return pl.pallas_call(
        body,
        name=name,
        grid=(nc,),
        in_specs=specs + [full((LR_PAD, GLA_QK)), full((1, GLA_QK)), full((1, GLA_DV))],
        out_specs=[pl.BlockSpec((GLA_STEP, MIX_W), lambda i: (i, 0)), pl.BlockSpec((GLA_DV, GLA_QK), lambda i: (i, 0))],
        out_shape=[jax.ShapeDtypeStruct((s, MIX_W), F32), jax.ShapeDtypeStruct((nc * GLA_DV, GLA_QK), F32)],
        scratch_shapes=[pltpu.VMEM((GLA_DV, GLA_QK), F32)],
        compiler_params=_cparams(("arbitrary",)),
    )(proj, proj, proj, proj, proj, wg, bg, go)


def gla_bwd(proj, cols, wg, bg, go, states, dy, *, name="gla_bwd"):
    s = proj.shape[0]
    nc = s // GLA_STEP
    specs, row = _gla_in_specs(cols, True, nc)

    def body(q_ref, k_ref, v_ref, lr_ref, r_ref, wg_ref, bg_ref, go_ref, st_ref, dy_ref,
             dq_ref, dk_ref, dv_ref, dlr_ref, dr_ref, dwg_ref, dbg_ref, dgo_ref, dstate):
        first = pl.program_id(0) == 0

        @pl.when(first)
        def _():
            dstate[...] = jnp.zeros_like(dstate)

        _, vjp = jax.vjp(_gla_group, q_ref[...], k_ref[...], v_ref[...], lr_ref[...], r_ref[...], st_ref[...],
                         wg_ref[...].astype(F32), bg_ref[...], go_ref[...])
        dq, dk, dv, dlr, dr, dst, dwg, dbg, dgo = vjp((dy_ref[...], dstate[...]))
        dq_ref[...] = dq.astype(BF16)
        dk_ref[...] = dk.astype(BF16)
        dv_ref[...] = dv.astype(BF16)
        dlr_ref[...] = dlr.astype(BF16)
        dr_ref[...] = dr.astype(BF16)
        dstate[...] = dst
        _accum(dwg_ref, dwg, first)
        _accum(dbg_ref, dbg, first)
        _accum(dgo_ref, dgo, first)

    c = GLA_STEP
    full = lambda shp: pl.BlockSpec(shp, lambda i: (0, 0))
    rows = lambda w: pl.BlockSpec((c, w), lambda i: (row(i), 0))
    return pl.pallas_call(
        body,
        name=name,
        grid=(nc,),
        in_specs=specs + [full((LR_PAD, GLA_QK)), full((1, GLA_QK)), full((1, GLA_DV)),
                          pl.BlockSpec((GLA_DV, GLA_QK), lambda i: (row(i), 0)), rows(MIX_W)],
        out_specs=[rows(GLA_QK), rows(GLA_QK), rows(MIX_W), rows(LR_PAD), rows(MIX_W),
                   full((LR_PAD, GLA_QK)), full((1, GLA_QK)), full((1, GLA_DV))],
        out_shape=[jax.ShapeDtypeStruct((s, GLA_QK), BF16), jax.ShapeDtypeStruct((s, GLA_QK), BF16),
                   jax.ShapeDtypeStruct((s, MIX_W), BF16), jax.ShapeDtypeStruct((s, LR_PAD), BF16),
                   jax.ShapeDtypeStruct((s, MIX_W), BF16), jax.ShapeDtypeStruct((LR_PAD, GLA_QK), F32),
                   jax.ShapeDtypeStruct((1, GLA_QK), F32), jax.ShapeDtypeStruct((1, GLA_DV), F32)],
        scratch_shapes=[pltpu.VMEM((GLA_DV, GLA_QK), F32)],
        compiler_params=_cparams(("arbitrary",)),
    )(proj, proj, proj, proj, proj, wg, bg, go, states, dy)


S5_G = 32
S5_P = 64
S5_C = 16
S5_N = S5_G * S5_P
S5_TC = 256
SUB = 8


def _s5_disc(a_re, a_im, ls, b_re, b_im):
    step = jnp.exp(ls)
    mag = jnp.exp(a_re * step)
    lr, li = mag * jnp.cos(a_im * step), mag * jnp.sin(a_im * step)
    inv = 1.0 / (a_re * a_re + a_im * a_im)
    nr, ni = lr - 1.0, li
    cr, ci = (nr * a_re + ni * a_im) * inv, (ni * a_re - nr * a_im) * inv
    return lr, li, cr * b_re - ci * b_im, cr * b_im + ci * b_re


def s5_disc_fwd(a_re, a_im, ls, b_re, b_im, *, name="s5_disc"):
    def body(ar, ai, l, br, bi, o0, o1, o2, o3):
        for o, val in zip((o0, o1, o2, o3), _s5_disc(ar[...], ai[...], l[...], br[...], bi[...])):
            o[...] = val

    sds = jax.ShapeDtypeStruct(a_re.shape, F32)
    return pl.pallas_call(body, name=name, out_shape=[sds] * 4)(a_re, a_im, ls, b_re, b_im)


def s5_disc_bwd(a_re, a_im, ls, b_re, b_im, cts, *, name="s5_disc_bwd"):
    def body(ar, ai, l, br, bi, c0, c1, c2, c3, dar, dai, dl, dbr, dbi):
        _, vjp = jax.vjp(_s5_disc, ar[...], ai[...], l[...], br[...], bi[...])
        g = vjp((c0[...], c1[...], c2[...], c3[...]))
        for o, val in zip((dar, dai, dl), g[:3]):
            o[...] = jnp.sum(val, axis=-1, keepdims=True)
        dbr[...] = g[3]
        dbi[...] = g[4]

    col = jax.ShapeDtypeStruct((a_re.shape[0], 1), F32)
    sds = jax.ShapeDtypeStruct(a_re.shape, F32)
    return pl.pallas_call(body, name=name, out_shape=[col, col, col, sds, sds])(a_re, a_im, ls, b_re, b_im, *cts)


def _gelu(y):
    return 0.5 * y * (1.0 + jnp.tanh(0.7978845608028654 * (y + 0.044715 * (y * y * y))))


def _s5_powers(lam, conj):
    lr, li = lam[:, :S5_N], lam[:, S5_N:]
    if conj:
        li = -li
    rows, pr, pi = [], lr, li
    for _ in range(SUB):
        rows.append((pr, pi))
        pr, pi = pr * lr - pi * li, pr * li + pi * lr
    return rows


def _s5_table(rows, reverse):
    ridx = lax.broadcasted_iota(jnp.int32, (SUB, S5_N), 0)
    tr = jnp.zeros((SUB, S5_N), F32)
    ti = jnp.zeros((SUB, S5_N), F32)
    for i in range(SUB):
        pr, pi = rows[SUB - 1 - i] if reverse else rows[i]
        tr = jnp.where(ridx == i, pr, tr)
        ti = jnp.where(ridx == i, pi, ti)
    return tr, ti


def _s5_scan(buf, lam, carry_ref, reverse):
    tc = buf.shape[0]
    nblk = tc // SUB
    rows = _s5_powers(lam, reverse)
    tr, ti = _s5_table(rows, reverse)
    ridx = lax.broadcasted_iota(jnp.int32, (SUB, S5_N), 0)
    steps = []
    for sft, (pr, pi) in ((1, rows[0]), (2, rows[1]), (4, rows[3])):
        keep = (ridx < SUB - sft) if reverse else (ridx >= sft)
        steps.append((SUB - sft if reverse else sft, jnp.where(keep, pr, 0.0), jnp.where(keep, pi, 0.0)))

    def block(j, carry):
        jj = (nblk - 1 - j) if reverse else j
        at = pl.ds(pl.multiple_of(jj * SUB, SUB), SUB)
        re, im = buf[at, :S5_N], buf[at, S5_N:]
        for rot, pr, pi in steps:
            sre, sim = pltpu.roll(re, rot, 0), pltpu.roll(im, rot, 0)
            re, im = re + pr * sre - pi * sim, im + pr * sim + pi * sre
        cr, ci = carry
        re, im = re + tr * cr - ti * ci, im + tr * ci + ti * cr
        buf[at, :S5_N] = re
        buf[at, S5_N:] = im
        edge = 0 if reverse else SUB - 1
        return re[edge : edge + 1, :], im[edge : edge + 1, :]

    c0 = (carry_ref[0:1, :S5_N], carry_ref[0:1, S5_N:])
    cr, ci = lax.fori_loop(0, nblk, block, c0)
    carry_ref[:, :S5_N] = jnp.broadcast_to(cr, (SUB, S5_N))
    carry_ref[:, S5_N:] = jnp.broadcast_to(ci, (SUB, S5_N))


def s5_fwd(proj, ucol, bd, cd, lam, dskip, wglu, bglu, *, name="s5_fwd"):
    s = proj.shape[0]
    tc = _pick(s, (S5_TC, 128, 64, 8))
    assert ucol % MIX_W == 0

    def body(u_ref, bd_ref, cd_ref, lam_ref, d_ref, w_ref, b_ref, y_ref, xs_ref, carry):
        @pl.when(pl.program_id(0) == 0)
        def _():
            carry[...] = jnp.zeros_like(carry)

        u = u_ref[...]
        xs_ref[...] = _dot(u, bd_ref[...])
        _s5_scan(xs_ref, lam_ref[...], carry, False)
        g = _gelu(_dot(xs_ref[...], cd_ref[...]) + d_ref[...] * u)
        y_ref[...] = g * _sigmoid(_dot(g, w_ref[...]) + b_ref[...])

    full = lambda shp: pl.BlockSpec(shp, lambda i: (0, 0))
    return pl.pallas_call(
        body,
        name=name,
        grid=(s // tc,),
        in_specs=[pl.BlockSpec((tc, MIX_W), lambda i: (i, ucol // MIX_W)), full((MIX_W, 2 * S5_N)), full((2 * S5_N, MIX_W)),
                  full((1, 2 * S5_N)), full((1, MIX_W)), full((MIX_W, MIX_W)), full((1, MIX_W))],
        out_specs=[pl.BlockSpec((tc, MIX_W), lambda i: (i, 0)), pl.BlockSpec((tc, 2 * S5_N), lambda i: (i, 0))],
        out_shape=[jax.ShapeDtypeStruct((s, MIX_W), F32), jax.ShapeDtypeStruct((s, 2 * S5_N), F32)],
        scratch_shapes=[pltpu.VMEM((SUB, 2 * S5_N), F32)],
        compiler_params=_cparams(("arbitrary",)),
    )(proj, bd, cd, lam, dskip, wglu, bglu)


def s5_bwd(proj, ucol, xs, bd, cd, lam, dskip, wglu, bglu, dya, *, name="s5_bwd"):
    s = proj.shape[0]
    tc = _pick(s, (S5_TC, 128, 64, 8))
    nch = s // tc
    per = tc // SUB

    def body(u_ref, xs_ref, xp_ref, bd_ref, cd_ref, lam_ref, d_ref, w_ref, b_ref, dya_ref,
             du_ref, adj_ref, g_ref, dpre_ref, dy_ref, dlam_ref, dd_ref, db_ref, buf, carry, lacc):
        i = pl.program_id(0)
        first = i == 0

        @pl.when(first)
        def _():
            carry[...] = jnp.zeros_like(carry)
            lacc[...] = jnp.zeros_like(lacc)

        u, x, dya_ = u_ref[...], xs_ref[...], dya_ref[...]
        y = _dot(x, cd_ref[...]) + d_ref[...] * u
        g, gelu_vjp = jax.vjp(_gelu, y)
        sg = _sigmoid(_dot(g, w_ref[...]) + b_ref[...])
        dpre = dya_ * g * sg * (1.0 - sg)
        (dy,) = gelu_vjp(dya_ * sg + _dot(dpre, w_ref[...], 1, 1))
        g_ref[...] = g.astype(BF16)
        dpre_ref[...] = dpre.astype(BF16)
        dy_ref[...] = dy.astype(BF16)
        db_part = jnp.sum(dpre, axis=0, keepdims=True)
        dd_part = jnp.sum(dy * u, axis=0, keepdims=True)
        buf[...] = _dot(dy, cd_ref[...], 1, 1)
        _s5_scan(buf, lam_ref[...], carry, True)
        a = buf[...]
        adj_ref[...] = a.astype(BF16)
        du_ref[...] = (dy * d_ref[...] + _dot(a, bd_ref[...], 1, 1)).astype(BF16)
        before = jnp.where(i == nch - 1, 0.0, xp_ref[SUB - 1 : SUB, :])
        ridx = lax.broadcasted_iota(jnp.int32, (tc, 2 * S5_N), 0)
        xprev = jnp.where(ridx == 0, before, pltpu.roll(x, 1, 0))
        ar, ai, xr, xi = a[:, :S5_N], a[:, S5_N:], xprev[:, :S5_N], xprev[:, S5_N:]
        lacc[:, :S5_N] += jnp.sum((ar * xr + ai * xi).reshape(per, SUB, S5_N), axis=0)
        lacc[:, S5_N:] += jnp.sum((ai * xr - ar * xi).reshape(per, SUB, S5_N), axis=0)
        _accum(db_ref, db_part, first)
        _accum(dd_ref, dd_part, first)

        @pl.when(i == nch - 1)
        def _():
            dlam_ref[...] = jnp.sum(lacc[...], axis=0, keepdims=True)

    rev = lambda i: nch - 1 - i
    full = lambda shp: pl.BlockSpec(shp, lambda i: (0, 0))
    rows = lambda w: pl.BlockSpec((tc, w), lambda i: (rev(i), 0))
    hb = jax.ShapeDtypeStruct((s, MIX_W), BF16)
    return pl.pallas_call(
        body,
        name=name,
        grid=(nch,),
        in_specs=[pl.BlockSpec((tc, MIX_W), lambda i: (rev(i), ucol // MIX_W)), rows(2 * S5_N),
                  pl.BlockSpec((SUB, 2 * S5_N), lambda i: (jnp.maximum(rev(i) * per - 1, 0), 0)),
                  full((MIX_W, 2 * S5_N)), full((2 * S5_N, MIX_W)), full((1, 2 * S5_N)), full((1, MIX_W)),
                  full((MIX_W, MIX_W)), full((1, MIX_W)), rows(MIX_W)],
        out_specs=[rows(MIX_W), rows(2 * S5_N), rows(MIX_W), rows(MIX_W), rows(MIX_W),
                   full((1, 2 * S5_N)), full((1, MIX_W)), full((1, MIX_W))],
        out_shape=[hb, jax.ShapeDtypeStruct((s, 2 * S5_N), BF16), hb, hb, hb,
                   jax.ShapeDtypeStruct((1, 2 * S5_N), F32), jax.ShapeDtypeStruct((1, MIX_W), F32),
                   jax.ShapeDtypeStruct((1, MIX_W), F32)],
        scratch_shapes=[pltpu.VMEM((tc, 2 * S5_N), F32), pltpu.VMEM((SUB, 2 * S5_N), F32), pltpu.VMEM((SUB, 2 * S5_N), F32)],
        compiler_params=_cparams(("arbitrary",)),
    )(proj, xs, xs, bd, cd, lam, dskip, wglu, bglu, dya)


def _bcast16(a):
    return jnp.broadcast_to(a.reshape(S5_N, 1), (S5_N, S5_C))


def _blockdiag(blocks):
    g, r, c = blocks.shape
    eye = jnp.eye(g, dtype=blocks.dtype)
    return (eye[:, None, :, None] * blocks[:, :, None, :]).reshape(g * r, g * c)


def _blockdiag_extract(dense, r, c):
    g = dense.shape[0] // r
    return jnp.einsum("grgc->grc", dense.reshape(g, r, g, c))


def s5_prepare(a_re, a_im, log_step, b_re, b_im, c_re, c_im):
    disc_in = (_bcast16(a_re), _bcast16(a_im), _bcast16(jnp.broadcast_to(log_step[:, None], (S5_G, S5_P))),
               b_re.reshape(S5_N, S5_C), b_im.reshape(S5_N, S5_C))
    lr, li, bbr, bbi = s5_disc_fwd(*disc_in)
    lam = jnp.concatenate([lr[:, 0], li[:, 0]]).reshape(1, 2 * S5_N)
    to_blocks = lambda t: _blockdiag(t.reshape(S5_G, S5_P, S5_C).transpose(0, 2, 1))
    bd = jnp.concatenate([to_blocks(bbr), to_blocks(bbi)], axis=1).astype(BF16)
    cd = jnp.concatenate([_blockdiag(c_re.transpose(0, 2, 1)), -_blockdiag(c_im.transpose(0, 2, 1))], axis=0).astype(BF16)
    return disc_in, lam, bd, cd


def s5_param_grads(disc_in, dlam, dbd, dcd):
    first_col = lambda v: jnp.pad(v.reshape(S5_N, 1), ((0, 0), (0, S5_C - 1)))
    from_blocks = lambda t: _blockdiag_extract(t, S5_C, S5_P).transpose(0, 2, 1).reshape(S5_N, S5_C)
    cts = (first_col(dlam[0, :S5_N]), first_col(dlam[0, S5_N:]), from_blocks(dbd[:, :S5_N]), from_blocks(dbd[:, S5_N:]))
    dar, dai, dls, dbr, dbi = s5_disc_bwd(*disc_in, cts)
    dcr = _blockdiag_extract(dcd[:S5_N], S5_P, S5_C).transpose(0, 2, 1)
    dci = -_blockdiag_extract(dcd[S5_N:], S5_P, S5_C).transpose(0, 2, 1)
    return (dar.reshape(S5_G, S5_P), dai.reshape(S5_G, S5_P), dls.reshape(S5_G, S5_P).sum(axis=1),
            dbr.reshape(S5_G, S5_P, S5_C), dbi.reshape(S5_G, S5_P, S5_C), dcr, dci)


N_DEV = 8
MESH_ID = pl.DeviceIdType.MESH
ANY = pl.BlockSpec(memory_space=pl.ANY)


def _me():
    return lax.axis_index("x"), lax.axis_index("y"), lax.axis_index("c")


def all_gather(blocks, *, by_core=False, name="all_gather"):
    na = len(blocks)
    shapes = [b.shape[1:] if by_core else b.shape for b in blocks]

    def body(*refs):
        ins, outs = refs[:na], refs[na : 2 * na]
        send_sems, recv_sems = refs[2 * na :]
        x, y, c = _me()
        me, sibling = (x, y, c), (x, y, 1 - c)
        xn, yn, diag = (1 - x, y, c), (x, 1 - y, c), (1 - x, 1 - y, c)
        slot = lambda p: 4 * p[0] + 2 * p[1] + p[2]

        def copy(a, k, block, to, half=None, src=None):
            dst = outs[a].at[slot(block)]
            if half is not None:
                rows = shapes[a][0] // 2
                dst = dst.at[pl.ds(half * rows, rows)]
            return pltpu.make_async_remote_copy(src_ref=dst if src is None else src, dst_ref=dst, send_sem=send_sems.at[a * 9 + k],
                                                recv_sem=recv_sems.at[a * 9 + k], device_id=to, device_id_type=MESH_ID)

        mine = [ins[a].at[c] if by_core else ins[a] for a in range(na)]
        sends = []

        def go(cp):
            cp.start()
            sends.append(cp)

        for a in range(na):
            go(copy(a, 0, me, sibling, src=mine[a]))
            go(copy(a, 1, me, xn, src=mine[a]))
            go(copy(a, 2, me, yn, src=mine[a]))
        for a in range(na):
            copy(a, 1, xn, me).wait_recv()
            go(copy(a, 3, xn, yn, half=0))
            go(copy(a, 5, xn, sibling))
            copy(a, 2, yn, me).wait_recv()
            go(copy(a, 4, yn, xn, half=1))
            go(copy(a, 6, yn, sibling))
        for a in range(na):
            copy(a, 3, diag, me, half=0).wait_recv()
            go(copy(a, 7, diag, sibling, half=0))
            copy(a, 4, diag, me, half=1).wait_recv()
            go(copy(a, 8, diag, sibling, half=1))
        for a in range(na):
            copy(a, 0, sibling, me).wait_recv()
            copy(a, 5, (1 - x, y, 1 - c), me).wait_recv()
            copy(a, 6, (x, 1 - y, 1 - c), me).wait_recv()
            copy(a, 7, (1 - x, 1 - y, 1 - c), me, half=0).wait_recv()
            copy(a, 8, (1 - x, 1 - y, 1 - c), me, half=1).wait_recv()
        for cp in sends:
            cp.wait_send()

    assert all(shp[0] % (64 // b.dtype.itemsize) == 0 for shp, b in zip(shapes, blocks)), shapes
    outs = pl.pallas_call(
        body,
        name=name,
        in_specs=[ANY] * na,
        out_specs=[ANY] * na,
        out_shape=[jax.ShapeDtypeStruct((N_DEV, *shp), b.dtype) for shp, b in zip(shapes, blocks)],
        scratch_shapes=[pltpu.SemaphoreType.DMA((na * 9,)), pltpu.SemaphoreType.DMA((na * 9,))],
    )(*blocks)
    x, y, c = _me()
    own = [lax.dynamic_index_in_dim(b, c, 0, keepdims=False) if by_core else b for b in blocks]
    return [lax.dynamic_update_index_in_dim(o, b, 4 * x + 2 * y + c, 0) for o, b in zip(outs, own)]


D2D_PIECES = 8


def _slabs(rows):
    n = D2D_PIECES if rows % (16 * D2D_PIECES) == 0 else 1
    return [pl.ds(i * (rows // n), rows // n) for i in range(n)]


def sibling_swap(arrs, *, name="sibling_swap"):
    na = len(arrs)
    pieces = [[(j, sl) for j in range(a.shape[0]) for sl in (_slabs(a.shape[1]) if a.shape[0] == 1 else _halves(a.shape[1]))]
              for a in arrs]
    base = np.cumsum([0] + [len(p) for p in pieces])

    def body(*refs):
        ins, outs = refs[:na], refs[na : 2 * na]
        send_sems, recv_sems = refs[2 * na :]
        x, y, c = _me()
        sends, recvs = [], []
        for a in range(na):
            for i, (j, sl) in enumerate(pieces[a]):
                k = int(base[a]) + i
                cp = pltpu.make_async_remote_copy(src_ref=ins[a].at[j, sl], dst_ref=outs[a].at[j, sl], send_sem=send_sems.at[k],
                                                  recv_sem=recv_sems.at[k], device_id=(x, y, 1 - c), device_id_type=MESH_ID)
                cp.start()
                sends.append(cp)
        for cp in sends:
            cp.wait_recv()
        for cp in sends:
            cp.wait_send()

    nsem = int(base[-1])
    return pl.pallas_call(
        body,
        name=name,
        in_specs=[ANY] * na,
        out_specs=[ANY] * na,
        out_shape=[jax.ShapeDtypeStruct(a.shape, a.dtype) for a in arrs],
        scratch_shapes=[pltpu.SemaphoreType.DMA((nsem,)), pltpu.SemaphoreType.DMA((nsem,))],
    )(*arrs)


def _halves(rows):
    return [pl.ds(0, rows // 2), pl.ds(rows // 2, rows // 2)] if rows % 32 == 0 else [pl.ds(0, rows)]


def chip_exchange(arrs, *, name="chip_exchange"):
    na = len(arrs)

    def body(*refs):
        ins, outs = refs[:na], refs[na : 2 * na]
        send_sems, recv_sems = refs[2 * na :]
        x, y, c = _me()
        my = 2 * x + y
        peers = []
        for k in range(1, N_CHIPS):
            px, py = (x + ((k >> 1) & 1)) % 2, (y + (k & 1)) % 2
            peers.append(((px, py, c), 2 * px + py))
        sends = []
        for a in range(na):
            for k, (peer, pidx) in enumerate(peers):
                cp = pltpu.make_async_remote_copy(src_ref=ins[a].at[pidx], dst_ref=outs[a].at[my], send_sem=send_sems.at[a * 3 + k],
                                                  recv_sem=recv_sems.at[a * 3 + k], device_id=peer, device_id_type=MESH_ID)
                cp.start()
                sends.append(cp)
        for a in range(na):
            for k, (peer, pidx) in enumerate(peers):
                pltpu.make_async_remote_copy(src_ref=ins[a].at[my], dst_ref=outs[a].at[pidx], send_sem=send_sems.at[a * 3 + k],
                                             recv_sem=recv_sems.at[a * 3 + k], device_id=peer, device_id_type=MESH_ID).wait_recv()
        for cp in sends:
            cp.wait_send()

    outs = pl.pallas_call(
        body,
        name=name,
        in_specs=[ANY] * na,
        out_specs=[ANY] * na,
        out_shape=[jax.ShapeDtypeStruct(a.shape, a.dtype) for a in arrs],
        scratch_shapes=[pltpu.SemaphoreType.DMA((na * 3,)), pltpu.SemaphoreType.DMA((na * 3,))],
    )(*arrs)
    chip = 2 * lax.axis_index("x") + lax.axis_index("y")
    return [lax.dynamic_update_index_in_dim(o, lax.dynamic_index_in_dim(a, chip, 0, keepdims=False), chip, 0)
            for o, a in zip(outs, arrs)]


def add_pair(a, b, *, name="add_pair"):
    n, r, c = a.shape
    tm = _row_tile(r, n * c * 2, 2 << 20)

    def body(a_ref, b_ref, o_ref):
        o_ref[...] = (a_ref[...].astype(F32) + b_ref[...].astype(F32)).astype(o_ref.dtype)

    blk = pl.BlockSpec((n, tm, c), lambda i: (0, i, 0))
    return pl.pallas_call(
        body, name=name, grid=(r // tm,), in_specs=[blk, blk], out_specs=blk, out_shape=jax.ShapeDtypeStruct(a.shape, a.dtype),
        compiler_params=_cparams(("arbitrary",)),
    )(a, b)


def _row_tile(rows, row_bytes, budget):
    best = None
    for t in range(8, rows + 1, 8):
        if rows % t == 0 and t * row_bytes <= budget:
            best = t
    return best or rows


def sum_blocks(a, *, name="sum_blocks"):
    n, r, c = a.shape
    tm = _row_tile(r, n * c * a.dtype.itemsize, 4 << 20)

    def body(a_ref, o_ref):
        acc = a_ref[0].astype(F32)
        for k in range(1, n):
            acc = acc + a_ref[k].astype(F32)
        o_ref[...] = acc

    return pl.pallas_call(
        body,
        name=name,
        grid=(r // tm,),
        in_specs=[pl.BlockSpec((n, tm, c), lambda i: (0, i, 0))],
        out_specs=pl.BlockSpec((tm, c), lambda i: (i, 0)),
        out_shape=jax.ShapeDtypeStruct((r, c), F32),
        compiler_params=_cparams(("arbitrary",)),
    )(a)


ADAM_LR = 0.001
ADAM_B1 = 0.9
ADAM_B2 = 0.999
ADAM_EPS = 1e-08
ADAM_WD = 0.01
ADAM_STEP = 10


def adamw(w, g, m, v, *, name="adamw"):
    r, c = w.shape
    tm = _row_tile(r, c * 4, 1 << 20)

    def body(w_ref, g_ref, m_ref, v_ref, d_ref, nm_ref, nv_ref):
        g_ = g_ref[...]
        m_ = ADAM_B1 * m_ref[...] + (1.0 - ADAM_B1) * g_
        v_ = ADAM_B2 * v_ref[...] + (1.0 - ADAM_B2) * (g_ * g_)
        m_hat = m_ / (1.0 - ADAM_B1**ADAM_STEP)
        v_hat = v_ / (1.0 - ADAM_B2**ADAM_STEP)
        d_ref[...] = -ADAM_LR * (m_hat / (jnp.sqrt(v_hat) + ADAM_EPS) + ADAM_WD * w_ref[...])
        nm_ref[...] = m_
        nv_ref[...] = v_

    blk = pl.BlockSpec((tm, c), lambda i: (i, 0))
    sds = jax.ShapeDtypeStruct((r, c), F32)
    return pl.pallas_call(
        body, name=name, grid=(r // tm,), in_specs=[blk] * 4, out_specs=[blk] * 3, out_shape=[sds] * 3,
        compiler_params=_cparams(("arbitrary",)),
    )(w, g, m, v)


WEIGHTS = ["g_mix", "w_in", "s5_a_re", "s5_a_im", "s5_log_step", "s5_b_re", "s5_b_im", "s5_c_re", "s5_c_im", "s5_d", "w_glu",
           "b_glu", "w_gla_gate", "b_gla_gate", "g_gla_out", "w_branch", "w_out", "g_mem", "g_cross", "w_xq", "w_xkv", "w_xo",
           "g_mlp", "w_up", "w_down", "g_final"]
SHARDED = {"w_in": 1, "w_glu": 0, "w_branch": 2, "w_out": 0, "w_xq": 0, "w_xkv": 1, "w_xo": 0, "w_up": 1, "w_down": 0}
DEPTH = 2
N_CHIPS = 4
D_IN = 9744
C_U, C_QG, C_KG, C_VG, C_LR, C_RG, C_QA, C_KA, C_VA, C_GATE = 0, 512, 768, 1024, 1536, 1552, 2064, 3600, 5136, 6672
A_GATE, A_U, A_QG, A_KG, A_VG, A_RG, A_Q0, A_LR, A_W = 0, 3072, 3584, 3840, 4096, 4608, 5120, 6656, 7168
GLA_COLS = (A_QG, A_KG, A_VG, A_LR, A_RG)


def _split_w_in(w):
    att = lambda g: [w[:, c + MIX_W * g : c + MIX_W * (g + 1)] for c in (C_QA, C_KA, C_VA)]
    a = jnp.concatenate([w[:, C_GATE:D_IN], w[:, C_U:C_QG], w[:, C_QG:C_KG], w[:, C_KG:C_VG], w[:, C_VG:C_LR], w[:, C_RG:C_QA],
                         *att(0), w[:, C_LR:C_RG], jnp.zeros((w.shape[0], A_W - A_LR - 16), w.dtype)], axis=1)
    return a, jnp.concatenate(att(1), axis=1), jnp.concatenate(att(2), axis=1)


def _join_w_in(a, b, c):
    att = lambda k: [a[:, A_Q0 + MIX_W * k : A_Q0 + MIX_W * (k + 1)], b[:, MIX_W * k : MIX_W * (k + 1)], c[:, MIX_W * k : MIX_W * (k + 1)]]
    return jnp.concatenate([a[:, A_U:A_QG], a[:, A_QG:A_KG], a[:, A_KG:A_VG], a[:, A_VG:A_RG], a[:, A_LR : A_LR + 16], a[:, A_RG:A_Q0],
                            *att(0), *att(1), *att(2), a[:, A_GATE:A_U]], axis=1)


def _by_residue(a, d):
    s = a.shape[0]
    return a if d == 1 else a.reshape(s // d, d, -1).transpose(1, 0, 2).reshape(s, -1)


def _in_order(a, d):
    s = a.shape[0]
    return a if d == 1 else a.reshape(d, s // d, -1).transpose(1, 0, 2).reshape(s, -1)


def _pack(arrs):
    flat = jnp.concatenate([a.reshape(-1) for a in arrs])
    n = flat.shape[0]
    rows = -(-n // (256 * 128)) * 256
    return jnp.pad(flat, (0, rows * 128 - n)).reshape(rows, 128)


def _unpack(packed, like):
    flat, out, o = packed.reshape(-1), [], 0
    for a in like:
        n = math.prod(a.shape)
        out.append(flat[o : o + n].reshape(a.shape))
        o += n
    return out


def _layer_fwd(x0, p, kv):
    h = rmsnorm(x0, p["g_mix"], name="mix_norm")
    hs = [h, _by_residue(h, 4), _by_residue(h, 16)]
    proj = [matmul(hs[g], p["w_in_seg"][g], name="in_proj") for g in range(3)]
    ya, xs = s5_fwd(proj[0], A_U, p["bd"], p["cd"], p["lam"], p["s5_d"], p["w_glu"], p["b_glu"])
    yb, gst = gla_fwd(proj[0], GLA_COLS, p["w_gate"], p["b_gla_gate"], p["g_gla_out"])
    att = [attn_fwd(proj[g], A_Q0 if g == 0 else 0, g) for g in range(3)]
    outs = [_in_order(att[g][0], ATT_GROUPS[g][1]) for g in range(3)]
    lses = [_in_order(att[g][1], ATT_GROUPS[g][1]) for g in range(3)]
    x1, merged, yc = merge_fwd(x0, proj[0], ya, yb, outs, lses, p["w_branch"], p["w_out"])
    x2 = cross_fwd(x1, p["g_cross"], p["w_xq"], p["w_xo"], kv)
    x3 = mlp_fwd(x2, p["g_mlp"], p["w_up"], p["w_down"])
    saved = dict(x0=x0, x1=x1, x2=x2, hs=hs, proj=proj, ya=ya, xs=xs, yb=yb, gst=gst, att=att, outs=outs, lses=lses,
                 merged=merged, yc=yc)
    return x3, saved


def _layer_bwd(dx3, p, kv, memn, sv):
    g = {}
    tn = lambda a, b, **kw: matmul(a, b, ta=True, out_dtype=BF16, name="wgrad", **kw)
    dx2, g["g_mlp"], h3, dup, act, dx3b = mlp_bwd(sv["x2"], p["g_mlp"], p["w_up"], p["w_down"], dx3)
    g["w_up"], g["w_down"] = tn(h3, dup), tn(act, dx3b)
    dx1, g["g_cross"], h2, dq, o, dkv = cross_bwd(sv["x1"], p["g_cross"], p["w_xq"], p["w_xo"], kv, dx2)
    g["w_xq"], g["w_xo"], g["w_xkv"] = tn(h2, dq), tn(o, dx2), tn(memn, dkv)
    dmemn = matmul(dkv, p["w_xkv"], tb=True, name="dmem")
    proj = sv["proj"]
    r = merge_bwd(dx1, proj[0], sv["ya"], sv["yb"], sv["yc"], sv["outs"], sv["lses"], p["w_branch"], p["w_out"])
    dgl, dz, dya, dyb, douts, dlses = r[0], r[1:4], r[4], r[5], r[6:9], r[9:12]
    g["w_branch"] = jnp.stack([tn(y, dz[n]) for n, y in enumerate((sv["ya"], sv["yb"], sv["yc"]))])
    g["w_out"] = tn(sv["merged"], dx1)
    datt = []
    for k in range(3):
        dil = ATT_GROUPS[k][1]
        datt.append(attn_bwd(proj[k], A_Q0 if k == 0 else 0, k, sv["att"][k][0], sv["att"][k][1],
                             _by_residue(douts[k], dil), _by_residue(dlses[k], dil)))
    du, adj, gg, dpre, dy, dlam, g["s5_d"], g["b_glu"] = s5_bwd(proj[0], A_U, sv["xs"], p["bd"], p["cd"], p["lam"], p["s5_d"],
                                                                 p["w_glu"], p["b_glu"], dya)
    dbd = matmul(proj[0], adj, ta=True, a_col=(A_U, MIX_W), name="s5_dbd")
    dcd = matmul(sv["xs"], dy, ta=True, name="s5_dcd")
    g["w_glu"] = tn(gg, dpre)
    (g["s5_a_re"], g["s5_a_im"], g["s5_log_step"], g["s5_b_re"], g["s5_b_im"], g["s5_c_re"],
     g["s5_c_im"]) = s5_param_grads(p["disc_in"], dlam, dbd, dcd)
    dqg, dkg, dvg, dlr, drg, dwg, g["b_gla_gate"], g["g_gla_out"] = gla_bwd(proj[0], GLA_COLS, p["w_gate"], p["b_gla_gate"],
                                                                             p["g_gla_out"], sv["gst"], dyb)
    g["w_gla_gate"] = dwg[:GLA_GATE_RANK]
    s = dx3.shape[0]
    dproj = [jnp.concatenate([dgl, du, dqg, dkg, dvg, drg, *datt[0], dlr, jnp.zeros((s, A_W - A_LR - LR_PAD), BF16)], axis=1),
             jnp.concatenate(datt[1], axis=1), jnp.concatenate(datt[2], axis=1)]
    g["w_in"] = _join_w_in(*[tn(sv["hs"][k], dproj[k]) for k in range(3)])
    dhs = [_in_order(matmul(dproj[k], p["w_in_seg"][k], tb=True, name="in_proj_bwd"), ATT_GROUPS[k][1]) for k in range(3)]
    dx0, g["g_mix"] = rmsnorm_bwd(sv["x0"], p["g_mix"], dhs, dx1, name="mix_norm_bwd")
    return dx0, g, dmemn


GLA_GATE_RANK = 16


def kernel(x, mem, *rest):
    nw = len(WEIGHTS)
    w = dict(zip(WEIGHTS, rest[:nw]))
    target = rest[nw]
    m = dict(zip(WEIGHTS, rest[nw + 1 : 2 * nw + 1]))
    v = dict(zip(WEIGHTS, rest[2 * nw + 1 : 3 * nw + 1]))
    x0, memx, target = x[0], mem[0], target[0]
    chip = 2 * lax.axis_index("x") + lax.axis_index("y")

    big = list(SHARDED)
    gate_pad = jnp.pad(w["w_gla_gate"], ((0, 0), (0, 0), (0, LR_PAD - w["w_gla_gate"].shape[2])))
    flat2 = lambda t: t.reshape(DEPTH, -1, t.shape[-1])
    gathered = all_gather([flat2(w[n].astype(BF16)) for n in big] + [gate_pad], by_core=True, name="gather_weights")
    gathered = [t.reshape(N_CHIPS, DEPTH, *s.shape[1:]) for t, s in zip(gathered, [w[n] for n in big] + [gate_pad])]
    full = {n: [jnp.concatenate([t[j, l] for j in range(N_CHIPS)], axis=SHARDED[n]) for l in range(DEPTH)]
            for n, t in zip(big, gathered)}
    gate_full = [jnp.concatenate([gathered[-1][j, l][:, : w["w_gla_gate"].shape[2]] for j in range(N_CHIPS)], axis=1)
                 for l in range(DEPTH)]

    memn = rmsnorm(memx, w["g_mem"], name="mem_norm")
    params, kvs = [], []
    for l in range(DEPTH):
        p = {n: full[n][l] for n in big if n != "w_in"}
        p["w_in_seg"] = _split_w_in(full["w_in"][l])
        p["w_gate"] = jnp.pad(gate_full[l], ((0, LR_PAD - GLA_GATE_RANK), (0, 0))).astype(BF16)
        for n in ("g_mix", "g_cross", "g_mlp"):
            p[n] = w[n][l]
        for n in ("s5_d", "b_glu", "b_gla_gate", "g_gla_out"):
            p[n] = w[n][l].reshape(1, -1)
        p["disc_in"], p["lam"], p["bd"], p["cd"] = s5_prepare(*[w[n][l] for n in WEIGHTS[2:9]])
        params.append(p)
        kvs.append(matmul(memn, p["w_xkv"], out_dtype=BF16, name="mem_kv"))

    xl, saved = x0, []
    for l in range(DEPTH):
        xl, sv = _layer_fwd(xl, params[l], kvs[l])
        saved.append(sv)
    loss8, dx, dg_final = loss_head(xl, w["g_final"], target)
    grads, dmem = [None] * DEPTH, []
    for l in reversed(range(DEPTH)):
        dx, grads[l], dm = _layer_bwd(dx, params[l], kvs[l], memn, saved[l])
        dmem.append(dm)
    _, dg_mem = rmsnorm_bwd(memx, w["g_mem"], dmem, jnp.zeros_like(memx), name="mem_norm_bwd")

    core = lax.axis_index("c")

    def halves(n):
        ax = SHARDED[n]
        keep, send = [], []
        for j in range(N_CHIPS):
            p0, p1 = [jnp.split(grads[l][n], N_CHIPS, axis=ax)[j] for l in range(DEPTH)]
            keep.append(jnp.where(core == 0, p0, p1))
            send.append(jnp.where(core == 0, p1, p0))
        flat = lambda ps: jnp.stack(ps).reshape(N_CHIPS, -1, ps[0].shape[-1])
        return flat(keep), flat(send)

    kept, sent = zip(*[halves(n) for n in big])
    theirs = sibling_swap(list(sent), name="swap_layers")
    chip_sums = [add_pair(a, b, name="add_cores") for a, b in zip(kept, theirs)]
    landed = chip_exchange(chip_sums, name="exchange_grads")
    reduced = [sum_blocks(t, name="sum_grads") for t in landed]
    other = sibling_swap([t[None] for t in reduced], name="pair_layers")
    out_g = {}
    for n, mine, theirs in zip(big, reduced, other):
        both = jnp.where(core == 0, jnp.stack([mine, theirs[0]]), jnp.stack([theirs[0], mine]))
        out_g[n] = both.reshape(w[n].shape)

    small = [n for n in WEIGHTS if n not in SHARDED]
    local_small = []
    for n in small:
        if n == "g_mem":
            local_small.append(dg_mem.reshape(w[n].shape))
        elif n == "g_final":
            local_small.append(dg_final.reshape(w[n].shape))
        elif n == "w_gla_gate":
            local_small.append(jnp.stack([grads[l][n] for l in range(DEPTH)]))
        else:
            local_small.append(jnp.stack([grads[l][n].reshape(w[n].shape[1:]) for l in range(DEPTH)]))
    packed = _pack(local_small + [loss8[0, :1]])
    (every,) = all_gather([packed], name="gather_small")
    summed = _unpack(sum_blocks(every, name="sum_small"), local_small + [loss8[0, :1]])
    loss = summed[-1].reshape(())
    for n, t in zip(small, summed[:-1]):
        if n == "w_gla_gate":
            t = lax.dynamic_slice_in_dim(t, chip * w[n].shape[2], w[n].shape[2], axis=2)
        out_g[n] = t

    delta, new_m, new_v = {}, {}, {}
    for n in big:
        c = w[n].shape[-1]
        r = [t.reshape(w[n].shape) for t in adamw(w[n].reshape(-1, c), out_g[n].reshape(-1, c), m[n].reshape(-1, c),
                                                  v[n].reshape(-1, c), name="adamw")]
        delta[n], new_m[n], new_v[n] = r
    like = [w[n] for n in small]
    r = adamw(_pack(like), _pack([out_g[n] for n in small]), _pack([m[n] for n in small]), _pack([v[n] for n in small]),
              name="adamw_small")
    for d, t in zip((delta, new_m, new_v), r):
        d.update(zip(small, _unpack(t, like)))
    return (loss, dx[None], *[out_g[n] for n in WEIGHTS], *[delta[n] for n in WEIGHTS], *[new_m[n] for n in WEIGHTS],
            *[new_v[n] for n in WEIGHTS])
```

```python
import functools
import math

import jax
import jax.numpy as jnp
import numpy as np
from jax import lax
from jax.experimental import pallas as pl
from jax.experimental.pallas import tpu as pltpu

F32 = jnp.float32
BF16 = jnp.bfloat16

VMEM_LIMIT_BYTES = 56 * 1024 * 1024
MATMUL_VMEM_BYTES = 36 * 1024 * 1024


def _cparams(sem):
    return pltpu.CompilerParams(dimension_semantics=sem, vmem_limit_bytes=VMEM_LIMIT_BYTES)


def _dot(a, b, ca=1, cb=0):
    return lax.dot_general(a.astype(BF16), b.astype(BF16), (((ca,), (cb,)), ((), ())), preferred_element_type=F32)


def _pick(n, prefs):
    for p in prefs:
        if n % p == 0:
            return p
    return n


def matmul(a, b, *, ta=False, tb=False, out_dtype=F32, name="mm", a_col=None, b_col=None):
    a_off, a_w = a_col if a_col is not None else (0, a.shape[1])
    b_off, b_w = b_col if b_col is not None else (0, b.shape[1])
    if ta:
        kk, m = a.shape[0], a_w
    else:
        m, kk = a.shape[0], a_w
    if tb:
        n, kb = b.shape[0], b_w
    else:
        kb, n = b.shape[0], b_w
    assert kk == kb, (a.shape, b.shape, ta, tb)
    tm = _pick(m, (512, 256, 128))
    tn = _pick(n, (1024, 512, 256, 128))
    sa, sb, so = a.dtype.itemsize, b.dtype.itemsize, jnp.dtype(out_dtype).itemsize
    fits = lambda t: 2 * (tm * t * sa + t * tn * sb) + tm * tn * (4 + 2 * so) <= MATMUL_VMEM_BYTES
    tk = max([t for t in range(128, kk + 1, 128) if kk % t == 0 and fits(t)] or [_pick(kk, (128,))])
    nk = kk // tk
    a_bytes, b_bytes = m * kk * sa, kk * n * sb
    n_outer = nk == 1 and b_bytes + a_bytes * (n // tn) < a_bytes + b_bytes * (m // tm)
    ij = (lambda g0, g1: (g1, g0)) if n_outer else (lambda g0, g1: (g0, g1))
    if ta:
        assert a_off % tm == 0
        a_spec = pl.BlockSpec((tk, tm), lambda g0, g1, k, o=a_off // tm: (k, ij(g0, g1)[0] + o))
    else:
        assert a_off % tk == 0
        a_spec = pl.BlockSpec((tm, tk), lambda g0, g1, k, o=a_off // tk: (ij(g0, g1)[0], k + o))
    if tb:
        assert b_off % tk == 0
        b_spec = pl.BlockSpec((tn, tk), lambda g0, g1, k, o=b_off // tk: (ij(g0, g1)[1], k + o))
    else:
        assert b_off % tn == 0
        b_spec = pl.BlockSpec((tk, tn), lambda g0, g1, k, o=b_off // tn: (k, ij(g0, g1)[1] + o))

    def body(a_ref, b_ref, o_ref, *acc):
        k = pl.program_id(2)
        p = _dot(a_ref[...], b_ref[...], 0 if ta else 1, 1 if tb else 0)
        if nk == 1:
            o_ref[...] = p.astype(o_ref.dtype)
            return
        (acc_ref,) = acc

        @pl.when(k == 0)
        def _():
            acc_ref[...] = p

        @pl.when(jnp.logical_and(k > 0, k < nk - 1))
        def _():
            acc_ref[...] += p

        @pl.when(k == nk - 1)
        def _():
            o_ref[...] = (acc_ref[...] + p).astype(o_ref.dtype)

    return pl.pallas_call(
        body,
        name=name,
        grid=(n // tn, m // tm, nk) if n_outer else (m // tm, n // tn, nk),
        in_specs=[a_spec, b_spec],
        out_specs=pl.BlockSpec((tm, tn), lambda g0, g1, k: ij(g0, g1)),
        out_shape=jax.ShapeDtypeStruct((m, n), out_dtype),
        scratch_shapes=[pltpu.VMEM((tm, tn), F32)] if nk > 1 else [],
        compiler_params=_cparams(("parallel", "parallel", "arbitrary")),
    )(a, b)


RMS_EPS = 1e-6
ROW_TILE = 512


def _rms_fwd(x, g):
    r = lax.rsqrt(jnp.mean(x * x, axis=-1, keepdims=True) + RMS_EPS)
    return x * r * g


def _rms_bwd(x, g, dh):
    r = lax.rsqrt(jnp.mean(x * x, axis=-1, keepdims=True) + RMS_EPS)
    xh = x * r
    dg = jnp.sum(dh * xh, axis=0, keepdims=True)
    dxh = dh * g
    dx = r * (dxh - xh * jnp.mean(dxh * xh, axis=-1, keepdims=True))
    return dx, dg


def _accum(ref, val, first):
    @pl.when(first)
    def _():
        ref[...] = val

    @pl.when(jnp.logical_not(first))
    def _():
        ref[...] += val


def rmsnorm(x, g, *, out_dtype=BF16, name="rmsnorm"):
    s, d = x.shape
    tm = _pick(s, (ROW_TILE, 256, 128, 8))

    def body(x_ref, g_ref, o_ref):
        o_ref[...] = _rms_fwd(x_ref[...], g_ref[...]).astype(o_ref.dtype)

    return pl.pallas_call(
        body,
        name=name,
        grid=(s // tm,),
        in_specs=[pl.BlockSpec((tm, d), lambda i: (i, 0)), pl.BlockSpec((1, d), lambda i: (0, 0))],
        out_specs=pl.BlockSpec((tm, d), lambda i: (i, 0)),
        out_shape=jax.ShapeDtypeStruct((s, d), out_dtype),
        compiler_params=_cparams(("arbitrary",)),
    )(x, g.reshape(1, d))


def rmsnorm_bwd(x, g, dhs, dres, *, name="rmsnorm_bwd"):
    s, d = x.shape
    tm = _pick(s, (ROW_TILE, 256, 128, 8))
    n = len(dhs)

    def body(x_ref, g_ref, *refs):
        dh_refs, dres_ref, dx_ref, dg_ref = refs[:n], refs[n], refs[n + 1], refs[n + 2]
        dh = dh_refs[0][...].astype(F32)
        for r in dh_refs[1:]:
            dh = dh + r[...].astype(F32)
        dx, dg = _rms_bwd(x_ref[...], g_ref[...], dh)
        dx_ref[...] = dres_ref[...] + dx
        _accum(dg_ref, dg, pl.program_id(0) == 0)

    row = pl.BlockSpec((tm, d), lambda i: (i, 0))
    vec = pl.BlockSpec((1, d), lambda i: (0, 0))
    return pl.pallas_call(
        body,
        name=name,
        grid=(s // tm,),
        in_specs=[row, vec] + [row] * n + [row],
        out_specs=[row, vec],
        out_shape=[jax.ShapeDtypeStruct((s, d), F32), jax.ShapeDtypeStruct((1, d), F32)],
        compiler_params=_cparams(("arbitrary",)),
    )(x, g.reshape(1, d), *dhs, dres)


def loss_head(x, g, target, *, name="loss_head"):
    s, d = x.shape
    tm = _pick(s, (ROW_TILE, 256, 128, 8))

    def body(x_ref, g_ref, t_ref, l_ref, dx_ref, dg_ref):
        x_, g_ = x_ref[...], g_ref[...]
        e = _rms_fwd(x_, g_) - t_ref[...]
        part = 0.5 * jnp.sum(jnp.sum(e * e, axis=-1, keepdims=True), axis=0, keepdims=True) / d
        dx, dg = _rms_bwd(x_, g_, e * (1.0 / d))
        dx_ref[...] = dx
        first = pl.program_id(0) == 0
        _accum(dg_ref, dg, first)
        _accum(l_ref, jnp.broadcast_to(part, (8, 128)), first)

    row = pl.BlockSpec((tm, d), lambda i: (i, 0))
    vec = pl.BlockSpec((1, d), lambda i: (0, 0))
    return pl.pallas_call(
        body,
        name=name,
        grid=(s // tm,),
        in_specs=[row, vec, row],
        out_specs=[pl.BlockSpec((8, 128), lambda i: (0, 0)), row, vec],
        out_shape=[jax.ShapeDtypeStruct((8, 128), F32), jax.ShapeDtypeStruct((s, d), F32), jax.ShapeDtypeStruct((1, d), F32)],
        compiler_params=_cparams(("arbitrary",)),
    )(x, g.reshape(1, d), target)


FF_TILE = 1024


def mlp_fwd(x, g, w_up, w_down, *, name="mlp_fwd"):
    s, d = x.shape
    ff = w_up.shape[1]
    tm, tf = _pick(s, (ROW_TILE, 256, 128)), _pick(ff, (FF_TILE, 512, 256, 128))
    nf = ff // tf

    def body(x_ref, g_ref, wu_ref, wd_ref, o_ref, h_ref, acc_ref):
        f = pl.program_id(1)

        @pl.when(f == 0)
        def _():
            h_ref[...] = _rms_fwd(x_ref[...], g_ref[...]).astype(BF16)
            acc_ref[...] = x_ref[...]

        up = jnp.maximum(_dot(h_ref[...], wu_ref[...]), 0.0)
        acc_ref[...] += _dot(up * up, wd_ref[...])

        @pl.when(f == nf - 1)
        def _():
            o_ref[...] = acc_ref[...]

    return pl.pallas_call(
        body,
        name=name,
        grid=(s // tm, nf),
        in_specs=[
            pl.BlockSpec((tm, d), lambda i, f: (i, 0)),
            pl.BlockSpec((1, d), lambda i, f: (0, 0)),
            pl.BlockSpec((d, tf), lambda i, f: (0, f)),
            pl.BlockSpec((tf, d), lambda i, f: (f, 0)),
        ],
        out_specs=pl.BlockSpec((tm, d), lambda i, f: (i, 0)),
        out_shape=jax.ShapeDtypeStruct((s, d), F32),
        scratch_shapes=[pltpu.VMEM((tm, d), BF16), pltpu.VMEM((tm, d), F32)],
        compiler_params=_cparams(("arbitrary", "arbitrary")),
    )(x, g.reshape(1, d), w_up, w_down)


def mlp_bwd(x, g, w_up, w_down, dy, *, name="mlp_bwd"):
    s, d = x.shape
    ff = w_up.shape[1]
    tm, tf = _pick(s, (ROW_TILE, 256, 128)), _pick(ff, (FF_TILE, 512, 256, 128))
    nf = ff // tf

    def body(x_ref, g_ref, wu_ref, wd_ref, dy_ref, dx_ref, dg_ref, h_ref, dup_ref, act_ref, dyb_ref, acc_ref):
        i, f = pl.program_id(0), pl.program_id(1)

        @pl.when(f == 0)
        def _():
            h_ref[...] = _rms_fwd(x_ref[...], g_ref[...]).astype(BF16)
            dyb_ref[...] = dy_ref[...].astype(BF16)
            acc_ref[...] = jnp.zeros_like(acc_ref)

        up = jnp.maximum(_dot(h_ref[...], wu_ref[...]), 0.0)
        dact = _dot(dyb_ref[...], wd_ref[...], 1, 1)
        dup = (2.0 * up * dact).astype(BF16)
        dup_ref[...] = dup
        act_ref[...] = (up * up).astype(BF16)
        acc_ref[...] += _dot(dup, wu_ref[...], 1, 1)

        @pl.when(f == nf - 1)
        def _():
            dx, dg = _rms_bwd(x_ref[...], g_ref[...], acc_ref[...])
            dx_ref[...] = dy_ref[...] + dx
            _accum(dg_ref, dg, i == 0)

    row = pl.BlockSpec((tm, d), lambda i, f: (i, 0))
    vec = pl.BlockSpec((1, d), lambda i, f: (0, 0))
    wide = pl.BlockSpec((tm, tf), lambda i, f: (i, f))
    return pl.pallas_call(
        body,
        name=name,
        grid=(s // tm, nf),
        in_specs=[row, vec, pl.BlockSpec((d, tf), lambda i, f: (0, f)), pl.BlockSpec((tf, d), lambda i, f: (f, 0)), row],
        out_specs=[row, vec, row, wide, wide, row],
        out_shape=[
            jax.ShapeDtypeStruct((s, d), F32),
            jax.ShapeDtypeStruct((1, d), F32),
            jax.ShapeDtypeStruct((s, d), BF16),
            jax.ShapeDtypeStruct((s, ff), BF16),
            jax.ShapeDtypeStruct((s, ff), BF16),
            jax.ShapeDtypeStruct((s, d), BF16),
        ],
        scratch_shapes=[pltpu.VMEM((tm, d), F32)],
        compiler_params=_cparams(("arbitrary", "arbitrary")),
    )(x, g.reshape(1, d), w_up, w_down, dy)


X_HEADS = 4


def _softmax_rows(s):
    m = jnp.max(s, axis=-1, keepdims=True)
    e = jnp.exp(s - m)
    return e / jnp.sum(e, axis=-1, keepdims=True)


def cross_fwd(x, g, wq, wo, kv, *, name="cross_fwd"):
    s, d = x.shape
    ml = kv.shape[0]
    dh = d // X_HEADS
    tm = _pick(s, (ROW_TILE, 256, 128))
    scale = dh**-0.5

    def body(x_ref, g_ref, wq_ref, wo_ref, kv_ref, o_ref):
        x_ = x_ref[...]
        q = _dot(_rms_fwd(x_, g_ref[...]), wq_ref[...])
        outs = []
        for hd in range(X_HEADS):
            kh = kv_ref[:, hd * dh : (hd + 1) * dh]
            vh = kv_ref[:, d + hd * dh : d + (hd + 1) * dh]
            p = _softmax_rows(_dot(q[:, hd * dh : (hd + 1) * dh], kh, 1, 1) * scale)
            outs.append(_dot(p, vh))
        o_ref[...] = x_ + _dot(jnp.concatenate(outs, axis=-1), wo_ref[...])

    row = pl.BlockSpec((tm, d), lambda i: (i, 0))
    full = lambda shp: pl.BlockSpec(shp, lambda i: (0, 0))
    return pl.pallas_call(
        body,
        name=name,
        grid=(s // tm,),
        in_specs=[row, full((1, d)), full((d, d)), full((d, d)), full((ml, 2 * d))],
        out_specs=row,
        out_shape=jax.ShapeDtypeStruct((s, d), F32),
        compiler_params=_cparams(("arbitrary",)),
    )(x, g.reshape(1, d), wq, wo, kv)


def cross_bwd(x, g, wq, wo, kv, dy, *, name="cross_bwd"):
    s, d = x.shape
    ml = kv.shape[0]
    dh = d // X_HEADS
    tm = _pick(s, (ROW_TILE, 256, 128))
    scale = dh**-0.5

    def body(x_ref, g_ref, wq_ref, wo_ref, kv_ref, dy_ref, dx_ref, dg_ref, h_ref, dq_ref, o_ref, dkv_ref):
        first = pl.program_id(0) == 0
        x_, g_ = x_ref[...], g_ref[...]
        h = _rms_fwd(x_, g_).astype(BF16)
        h_ref[...] = h
        q = _dot(h, wq_ref[...])
        dy_ = dy_ref[...]
        do = _dot(dy_, wo_ref[...], 1, 1)
        outs, dqs, dks, dvs = [], [], [], []
        for hd in range(X_HEADS):
            sl = slice(hd * dh, (hd + 1) * dh)
            kh = kv_ref[:, sl]
            vh = kv_ref[:, d + hd * dh : d + (hd + 1) * dh]
            qh = q[:, sl]
            p = _softmax_rows(_dot(qh, kh, 1, 1) * scale)
            outs.append(_dot(p, vh))
            doh = do[:, sl]
            dp = _dot(doh, vh, 1, 1)
            ds = p * (dp - jnp.sum(dp * p, axis=-1, keepdims=True)) * scale
            dqs.append(_dot(ds, kh))
            dks.append(_dot(ds, qh, 0, 0))
            dvs.append(_dot(p, doh, 0, 0))
        o_ref[...] = jnp.concatenate(outs, axis=-1).astype(BF16)
        dq = jnp.concatenate(dqs, axis=-1).astype(BF16)
        dq_ref[...] = dq
        dx, dg = _rms_bwd(x_, g_, _dot(dq, wq_ref[...], 1, 1))
        dx_ref[...] = dy_ + dx
        _accum(dkv_ref, jnp.concatenate(dks + dvs, axis=-1), first)
        _accum(dg_ref, dg, first)

    row = pl.BlockSpec((tm, d), lambda i: (i, 0))
    full = lambda shp: pl.BlockSpec(shp, lambda i: (0, 0))
    rowb = jax.ShapeDtypeStruct((s, d), BF16)
    return pl.pallas_call(
        body,
        name=name,
        grid=(s // tm,),
        in_specs=[row, full((1, d)), full((d, d)), full((d, d)), full((ml, 2 * d)), row],
        out_specs=[row, full((1, d)), row, row, row, full((ml, 2 * d))],
        out_shape=[jax.ShapeDtypeStruct((s, d), F32), jax.ShapeDtypeStruct((1, d), F32), rowb, rowb, rowb,
                   jax.ShapeDtypeStruct((ml, 2 * d), F32)],
        compiler_params=_cparams(("arbitrary",)),
    )(x, g.reshape(1, d), wq, wo, kv, dy)


N_BRANCH = 3
MIX_W = 512
ATT_DH = 128
ATT_HG = 4
MERGE_TILE = 256


def _group_weights(l0, l1, l2):
    m = jnp.maximum(jnp.maximum(l0, l1), l2)
    e = [jnp.exp(l0 - m), jnp.exp(l1 - m), jnp.exp(l2 - m)]
    inv = 1.0 / (e[0] + e[1] + e[2])
    return [t * inv for t in e]


def _sigmoid(x):
    return 0.5 * (jnp.tanh(0.5 * x) + 1.0)


def merge_fwd(x, proj, ya, yb, outs, lses, wb, wout, *, name="merge_fwd"):
    s, d = x.shape
    tm = _pick(s, (MERGE_TILE, 128))

    def body(x_ref, g0, g1, g2, ya_ref, yb_ref, o0, o1, o2, l0, l1, l2, wb_ref, wo_ref, x1_ref, mg_ref, yc_ref):
        w = _group_weights(l0[...], l1[...], l2[...])
        o = (o0[...], o1[...], o2[...])
        heads = []
        for hd in range(ATT_HG):
            sl = slice(hd * ATT_DH, (hd + 1) * ATT_DH)
            heads.append(sum(w[gi][:, hd : hd + 1] * o[gi][:, sl] for gi in range(3)))
        yc = jnp.concatenate(heads, axis=-1)
        yc_ref[...] = yc.astype(BF16)
        merged = None
        for n, (y, gl) in enumerate(((ya_ref[...], g0), (yb_ref[...], g1), (yc, g2))):
            t = _sigmoid(gl[...]) * _dot(y, wb_ref[n])
            merged = t if merged is None else merged + t
        mb = merged.astype(BF16)
        mg_ref[...] = mb
        x1_ref[...] = x_ref[...] + _dot(mb, wo_ref[...])

    row = pl.BlockSpec((tm, d), lambda i: (i, 0))
    half = pl.BlockSpec((tm, MIX_W), lambda i: (i, 0))
    perhead = pl.BlockSpec((tm, ATT_HG), lambda i: (i, 0))
    gate = [pl.BlockSpec((tm, d), lambda i, n=n: (i, n)) for n in range(N_BRANCH)]
    return pl.pallas_call(
        body,
        name=name,
        grid=(s // tm,),
        in_specs=[row] + gate + [half] * 5 + [perhead] * 3
        + [pl.BlockSpec((N_BRANCH, MIX_W, d), lambda i: (0, 0, 0)), pl.BlockSpec((d, d), lambda i: (0, 0))],
        out_specs=[row, row, half],
        out_shape=[jax.ShapeDtypeStruct((s, d), F32), jax.ShapeDtypeStruct((s, d), BF16), jax.ShapeDtypeStruct((s, MIX_W), BF16)],
        compiler_params=_cparams(("arbitrary",)),
    )(x, proj, proj, proj, ya, yb, *outs, *lses, wb, wout)


def merge_bwd(dx1, proj, ya, yb, yc, outs, lses, wb, wout, *, name="merge_bwd"):
    s, d = dx1.shape
    tm = _pick(s, (MERGE_TILE, 128))

    def body(dx_ref, g0, g1, g2, ya_ref, yb_ref, yc_ref, o0, o1, o2, l0, l1, l2, wb_ref, wo_ref,
             dgl_ref, dz0, dz1, dz2, dya_ref, dyb_ref, do0, do1, do2, dl0, dl1, dl2):
        dm = _dot(dx_ref[...], wo_ref[...], 1, 1)
        dys = []
        for n, (y, gl, dz_ref) in enumerate(((ya_ref, g0, dz0), (yb_ref, g1, dz1), (yc_ref, g2, dz2))):
            z = _dot(y[...], wb_ref[n])
            sg = _sigmoid(gl[...])
            dz = (dm * sg).astype(BF16)
            dz_ref[...] = dz
            dgl_ref[:, n * d : (n + 1) * d] = (dm * z * sg * (1.0 - sg)).astype(BF16)
            dys.append(_dot(dz, wb_ref[n], 1, 1))
        dya_ref[...] = dys[0]
        dyb_ref[...] = dys[1]
        dyc = dys[2]
        w = _group_weights(l0[...], l1[...], l2[...])
        o = [o0[...], o1[...], o2[...]]
        for hd in range(ATT_HG):
            sl = slice(hd * ATT_DH, (hd + 1) * ATT_DH)
            col = slice(hd, hd + 1)
            wh = [w[gi][:, col] for gi in range(3)]
            for gi, r in enumerate((do0, do1, do2)):
                r[:, sl] = wh[gi] * dyc[:, sl]
            t = [jnp.sum(dyc[:, sl] * o[gi][:, sl], axis=-1, keepdims=True) for gi in range(3)]
            tbar = wh[0] * t[0] + wh[1] * t[1] + wh[2] * t[2]
            for gi, r in enumerate((dl0, dl1, dl2)):
                r[:, col] = wh[gi] * (t[gi] - tbar)

    row = pl.BlockSpec((tm, d), lambda i: (i, 0))
    half = pl.BlockSpec((tm, MIX_W), lambda i: (i, 0))
    perhead = pl.BlockSpec((tm, ATT_HG), lambda i: (i, 0))
    gate = [pl.BlockSpec((tm, d), lambda i, n=n: (i, n)) for n in range(N_BRANCH)]
    rb = jax.ShapeDtypeStruct((s, d), BF16)
    hf = jax.ShapeDtypeStruct((s, MIX_W), F32)
    ph = jax.ShapeDtypeStruct((s, ATT_HG), F32)
    return pl.pallas_call(
        body,
        name=name,
        grid=(s // tm,),
        in_specs=[row] + gate + [half] * 6 + [perhead] * 3
        + [pl.BlockSpec((N_BRANCH, MIX_W, d), lambda i: (0, 0, 0)), pl.BlockSpec((d, d), lambda i: (0, 0))],
        out_specs=[pl.BlockSpec((tm, N_BRANCH * d), lambda i: (i, 0)), row, row, row] + [half] * 5 + [perhead] * 3,
        out_shape=[jax.ShapeDtypeStruct((s, N_BRANCH * d), BF16), rb, rb, rb] + [hf] * 5 + [ph] * 3,
        compiler_params=_cparams(("arbitrary",)),
    )(dx1, proj, proj, proj, ya, yb, yc, *outs, *lses, wb, wout)


ATT_BLOCK = 128
ATT_STEP = 4 * ATT_BLOCK
ATT_GROUPS = ((128, 1), (512, 4), (2048, 16))
N_ATT_HEADS = ATT_HG * len(ATT_GROUPS)
ALIBI_MAX_EXP = 8.0
MASKED = -1e30


def _att_slopes(group):
    return [2.0 ** (-ALIBI_MAX_EXP * (group * ATT_HG + h + 1) / N_ATT_HEADS) for h in range(ATT_HG)]


def _att_scores(q, kk, slope_dil, has_prev):
    scale = ATT_DH**-0.5
    qi = lax.broadcasted_iota(jnp.int32, (ATT_BLOCK, 2 * ATT_BLOCK), 0)
    kj = lax.broadcasted_iota(jnp.int32, (ATT_BLOCK, 2 * ATT_BLOCK), 1)
    dist = qi + ATT_BLOCK - kj
    valid = jnp.logical_and(jnp.logical_and(dist >= 0, dist <= ATT_BLOCK), jnp.logical_or(kj >= ATT_BLOCK, has_prev))
    return jnp.where(valid, _dot(q, kk, 1, 1) * scale - slope_dil * dist.astype(F32), MASKED)


def _att_window(kp_ref, kc_ref, buf):
    buf[:ATT_BLOCK, :] = kp_ref[...]
    buf[ATT_BLOCK:, :] = kc_ref[...]


def attn_fwd(qkv, col, group, *, name="attn_fwd"):
    s = qkv.shape[0]
    window, dil = ATT_GROUPS[group]
    assert window // dil == ATT_BLOCK and s % (dil * ATT_BLOCK) == 0 and col % MIX_W == 0
    nb = s // dil // ATT_BLOCK
    nq = ATT_STEP // ATT_BLOCK
    ng = s // ATT_STEP
    slopes = _att_slopes(group)
    c0 = col // MIX_W

    def body(q_ref, kp_ref, kc_ref, vp_ref, vc_ref, o_ref, l_ref, kbuf, vbuf):
        n = pl.program_id(0)
        _att_window(kp_ref, kc_ref, kbuf)
        _att_window(vp_ref, vc_ref, vbuf)
        for i in range(nq):
            r = slice(i * ATT_BLOCK, (i + 1) * ATT_BLOCK)
            win = slice(i * ATT_BLOCK, (i + 2) * ATT_BLOCK)
            has_prev = ((n * nq + i) % nb) != 0
            for hd in range(ATT_HG):
                sl = slice(hd * ATT_DH, (hd + 1) * ATT_DH)
                sc = _att_scores(q_ref[r, sl], kbuf[win, sl], slopes[hd] * dil, has_prev)
                m = jnp.max(sc, axis=-1, keepdims=True)
                e = jnp.exp(sc - m)
                den = jnp.sum(e, axis=-1, keepdims=True)
                o_ref[r, sl] = _dot(e * (1.0 / den), vbuf[win, sl])
                l_ref[r, hd : hd + 1] = m + jnp.log(den)

    cur = lambda c: pl.BlockSpec((ATT_STEP, MIX_W), lambda n, c=c: (n, c))
    prev = lambda c: pl.BlockSpec((ATT_BLOCK, MIX_W), lambda n, c=c: (jnp.maximum(n * nq - 1, 0), c))
    blk = pl.BlockSpec((ATT_STEP, MIX_W), lambda n: (n, 0))
    return pl.pallas_call(
        body,
        name=name,
        grid=(ng,),
        in_specs=[cur(c0), prev(c0 + 1), cur(c0 + 1), prev(c0 + 2), cur(c0 + 2)],
        out_specs=[blk, pl.BlockSpec((ATT_STEP, ATT_HG), lambda n: (n, 0))],
        out_shape=[jax.ShapeDtypeStruct((s, MIX_W), F32), jax.ShapeDtypeStruct((s, ATT_HG), F32)],
        scratch_shapes=[pltpu.VMEM((ATT_BLOCK + ATT_STEP, MIX_W), qkv.dtype)] * 2,
        compiler_params=_cparams(("arbitrary",)),
    )(qkv, qkv, qkv, qkv, qkv)


def attn_bwd(qkv, col, group, out, lse, dout, dlse, *, name="attn_bwd"):
    s = qkv.shape[0]
    window, dil = ATT_GROUPS[group]
    nb = s // dil // ATT_BLOCK
    nq = ATT_STEP // ATT_BLOCK
    ng = s // ATT_STEP
    slopes = _att_slopes(group)
    c0 = col // MIX_W
    scale = ATT_DH**-0.5
    tail = slice((nq - 1) * ATT_BLOCK, nq * ATT_BLOCK)

    def body(q_ref, kp_ref, kc_ref, vp_ref, vc_ref, o_ref, l_ref, do_ref, dl_ref, dq_ref, dk_ref, dv_ref, acck, accv,
             kbuf, vbuf):
        n = pl.program_id(0)
        live = n < ng

        @pl.when(n == 0)
        def _():
            acck[...] = jnp.zeros_like(acck)
            accv[...] = jnp.zeros_like(accv)

        _att_window(kp_ref, kc_ref, kbuf)
        _att_window(vp_ref, vc_ref, vbuf)
        dk_ref[: (nq - 1) * ATT_BLOCK, :] = acck[: (nq - 1) * ATT_BLOCK, :].astype(BF16)
        dv_ref[: (nq - 1) * ATT_BLOCK, :] = accv[: (nq - 1) * ATT_BLOCK, :].astype(BF16)
        for i in range(nq):
            r = slice(i * ATT_BLOCK, (i + 1) * ATT_BLOCK)
            rp = slice((i - 1) * ATT_BLOCK, i * ATT_BLOCK)
            win = slice(i * ATT_BLOCK, (i + 2) * ATT_BLOCK)
            has_prev = ((jnp.minimum(n, ng - 1) * nq + i) % nb) != 0
            for hd in range(ATT_HG):
                sl = slice(hd * ATT_DH, (hd + 1) * ATT_DH)
                q, do, kk, vv = q_ref[r, sl], do_ref[r, sl], kbuf[win, sl], vbuf[win, sl]
                p = jnp.exp(_att_scores(q, kk, slopes[hd] * dil, has_prev) - l_ref[r, hd : hd + 1])
                corr = dl_ref[r, hd : hd + 1] - jnp.sum(do * o_ref[r, sl], axis=-1, keepdims=True)
                ds = p * (_dot(do, vv, 1, 1) + corr) * scale
                dq_ref[r, sl] = _dot(ds, kk).astype(BF16)
                dkk = jnp.where(live, _dot(ds, q, 0, 0), 0.0)
                dvv = jnp.where(live, _dot(p, do, 0, 0), 0.0)
                if i == 0:
                    dk_ref[tail, sl] = (acck[tail, sl] + dkk[:ATT_BLOCK]).astype(BF16)
                    dv_ref[tail, sl] = (accv[tail, sl] + dvv[:ATT_BLOCK]).astype(BF16)
                else:
                    acck[rp, sl] += dkk[:ATT_BLOCK]
                    accv[rp, sl] += dvv[:ATT_BLOCK]
                acck[r, sl] = dkk[ATT_BLOCK:]
                accv[r, sl] = dvv[ATT_BLOCK:]

    last = ng - 1
    cur = lambda c: pl.BlockSpec((ATT_STEP, MIX_W), lambda n, c=c: (jnp.minimum(n, last), c))
    prev = lambda c: pl.BlockSpec((ATT_BLOCK, MIX_W), lambda n, c=c: (jnp.maximum(jnp.minimum(n, last) * nq - 1, 0), c))
    behind = pl.BlockSpec((ATT_STEP, MIX_W), lambda n: (jnp.clip(n - 1, 0, last), 0))
    perhead = pl.BlockSpec((ATT_STEP, ATT_HG), lambda n: (jnp.minimum(n, last), 0))
    sds = jax.ShapeDtypeStruct((s, MIX_W), BF16)
    return pl.pallas_call(
        body,
        name=name,
        grid=(ng + 1,),
        in_specs=[cur(c0), prev(c0 + 1), cur(c0 + 1), prev(c0 + 2), cur(c0 + 2), cur(0), perhead, cur(0), perhead],
        out_specs=[cur(0), behind, behind],
        out_shape=[sds, sds, sds],
        scratch_shapes=[pltpu.VMEM((ATT_STEP, MIX_W), F32)] * 2 + [pltpu.VMEM((ATT_BLOCK + ATT_STEP, MIX_W), qkv.dtype)] * 2,
        compiler_params=_cparams(("arbitrary",)),
    )(qkv, qkv, qkv, qkv, qkv, out, lse, dout, dlse)


GLA_HEADS = 4
GLA_DK = 64
GLA_DV = 128
GLA_CHUNK = 64
GLA_TAU = 16.0
GLA_QK = GLA_HEADS * GLA_DK
LR_PAD = 128


def _dot_exact(a, b, ca=1, cb=0, precision=lax.Precision.HIGHEST):
    return lax.dot_general(a, b, (((ca,), (cb,)), ((), ())), precision=precision, preferred_element_type=F32)


GLA_STEP = 4 * GLA_CHUNK


@jax.custom_vjp
def _score_dot(a, b):
    return _dot_exact(a, b, 1, 1)


def _score_dot_fwd(a, b):
    return _dot_exact(a, b, 1, 1), (a, b)


def _score_dot_bwd(res, g):
    a, b = res
    return _dot(g, b), _dot(g, a, 0, 0)


_score_dot.defvjp(_score_dot_fwd, _score_dot_bwd)


def _gla_group(q, k, v, lr, r, state, wg, bg, go):
    c, n = GLA_CHUNK, q.shape[0]
    z = _dot(lr, wg) + bg
    la = (jnp.minimum(z, 0.0) - jnp.log(1.0 + jnp.exp(-jnp.abs(z)))) * (1.0 / GLA_TAU)
    ri = lax.broadcasted_iota(jnp.int32, (n, n), 0)
    ci = lax.broadcasted_iota(jnp.int32, (n, n), 1)
    shift = c.bit_length() - 1
    in_chunk = jnp.logical_and(ri >= ci, jnp.right_shift(ri, shift) == jnp.right_shift(ci, shift))
    ball = _dot_exact(in_chunk.astype(F32), la, precision=lax.Precision.HIGH)
    ri = lax.broadcasted_iota(jnp.int32, (c, c), 0)
    ci = lax.broadcasted_iota(jnp.int32, (c, c), 1)
    causal = ri >= ci
    qs = q * GLA_DK**-0.5
    ys = []
    for j in range(n // c):
        rows = slice(j * c, (j + 1) * c)
        b = ball[rows]
        bmid = b[c // 2 : c // 2 + 1, :]
        bend = b[c - 1 : c, :]
        q_in = qs[rows] * jnp.exp(b)
        q_mid = qs[rows] * jnp.exp(b - bmid)
        k_mid = k[rows] * jnp.exp(bmid - b)
        k_end = k[rows] * jnp.exp(bend - b)
        yh, upd = [], []
        for h in range(GLA_HEADS):
            sk = slice(h * GLA_DK, (h + 1) * GLA_DK)
            sv = slice(h * GLA_DV, (h + 1) * GLA_DV)
            vh = v[rows, sv]
            inter = _dot(q_in[:, sk], state[:, sk], 1, 1)
            sc = jnp.where(causal, _score_dot(q_mid[:, sk], k_mid[:, sk]), 0.0)
            o = inter + _dot(sc, vh)
            o = o * lax.rsqrt(jnp.mean(o * o, axis=-1, keepdims=True) + RMS_EPS) * go
            rh = r[rows, sv]
            yh.append(o * rh * _sigmoid(rh))
            upd.append(_dot(vh, k_end[:, sk], 0, 0))
        state = jnp.exp(bend) * state + jnp.concatenate(upd, axis=-1)
        ys.append(jnp.concatenate(yh, axis=-1))
    return jnp.concatenate(ys, axis=0), state


def _gla_in_specs(cols, rev, nc):
    c = GLA_STEP
    row = (lambda i: nc - 1 - i) if rev else (lambda i: i)
    qc, kc, vc, lc, rc = cols
    assert qc % GLA_QK == 0 and kc % GLA_QK == 0 and vc % MIX_W == 0 and rc % MIX_W == 0 and lc % LR_PAD == 0
    return [
        pl.BlockSpec((c, GLA_QK), lambda i: (row(i), qc // GLA_QK)),
        pl.BlockSpec((c, GLA_QK), lambda i: (row(i), kc // GLA_QK)),
        pl.BlockSpec((c, MIX_W), lambda i: (row(i), vc // MIX_W)),
        pl.BlockSpec((c, LR_PAD), lambda i: (row(i), lc // LR_PAD)),
        pl.BlockSpec((c, MIX_W), lambda i: (row(i), rc // MIX_W)),
    ], row


def gla_fwd(proj, cols, wg, bg, go, *, name="gla_fwd"):
    s = proj.shape[0]
    nc = s // GLA_STEP
    specs, row = _gla_in_specs(cols, False, nc)

    def body(q_ref, k_ref, v_ref, lr_ref, r_ref, wg_ref, bg_ref, go_ref, y_ref, st_ref, state):
        @pl.when(pl.program_id(0) == 0)
        def _():
            state[...] = jnp.zeros_like(state)

        st = state[...]
        st_ref[...] = st
        y, new = _gla_group(q_ref[...], k_ref[...], v_ref[...], lr_ref[...], r_ref[...], st, wg_ref[...], bg_ref[...], go_ref[...])
        y_ref[...] = y
        state[...] = new

    full = lambda shp: pl.BlockSpec(shp, lambda i: (0, 0))
    return pl.pallas_call(
        body,
        name=name,
        grid=(nc,),
        in_specs=specs + [full((LR_PAD, GLA_QK)), full((1, GLA_QK)), full((1, GLA_DV))],
        out_specs=[pl.BlockSpec((GLA_STEP, MIX_W), lambda i: (i, 0)), pl.BlockSpec((GLA_DV, GLA_QK), lambda i: (i, 0))],
        out_shape=[jax.ShapeDtypeStruct((s, MIX_W), F32), jax.ShapeDtypeStruct((nc * GLA_DV, GLA_QK), F32)],
        scratch_shapes=[pltpu.VMEM((GLA_DV, GLA_QK), F32)],
        compiler_params=_cparams(("arbitrary",)),
    )(proj, proj, proj, proj, proj, wg, bg, go)


def gla_bwd(proj, cols, wg, bg, go, states, dy, *, name="gla_bwd"):
    s = proj.shape[0]
    nc = s // GLA_STEP
    specs, row = _gla_in_specs(cols, True, nc)

    def body(q_ref, k_ref, v_ref, lr_ref, r_ref, wg_ref, bg_ref, go_ref, st_ref, dy_ref,
             dq_ref, dk_ref, dv_ref, dlr_ref, dr_ref, dwg_ref, dbg_ref, dgo_ref, dstate):
        first = pl.program_id(0) == 0

        @pl.when(first)
        def _():
            dstate[...] = jnp.zeros_like(dstate)

        _, vjp = jax.vjp(_gla_group, q_ref[...], k_ref[...], v_ref[...], lr_ref[...], r_ref[...], st_ref[...],
                         wg_ref[...].astype(F32), bg_ref[...], go_ref[...])
        dq, dk, dv, dlr, dr, dst, dwg, dbg, dgo = vjp((dy_ref[...], dstate[...]))
        dq_ref[...] = dq.astype(BF16)
        dk_ref[...] = dk.astype(BF16)
        dv_ref[...] = dv.astype(BF16)
        dlr_ref[...] = dlr.astype(BF16)
        dr_ref[...] = dr.astype(BF16)
        dstate[...] = dst
        _accum(dwg_ref, dwg, first)
        _accum(dbg_ref, dbg, first)
        _accum(dgo_ref, dgo, first)

    c = GLA_STEP
    full = lambda shp: pl.BlockSpec(shp, lambda i: (0, 0))
    rows = lambda w: pl.BlockSpec((c, w), lambda i: (row(i), 0))
    return pl.pallas_call(
        body,
        name=name,
        grid=(nc,),
        in_specs=specs + [full((LR_PAD, GLA_QK)), full((1, GLA_QK)), full((1, GLA_DV)),
                          pl.BlockSpec((GLA_DV, GLA_QK), lambda i: (row(i), 0)), rows(MIX_W)],
        out_specs=[rows(GLA_QK), rows(GLA_QK), rows(MIX_W), rows(LR_PAD), rows(MIX_W),
                   full((LR_PAD, GLA_QK)), full((1, GLA_QK)), full((1, GLA_DV))],
        out_shape=[jax.ShapeDtypeStruct((s, GLA_QK), BF16), jax.ShapeDtypeStruct((s, GLA_QK), BF16),
                   jax.ShapeDtypeStruct((s, MIX_W), BF16), jax.ShapeDtypeStruct((s, LR_PAD), BF16),
                   jax.ShapeDtypeStruct((s, MIX_W), BF16), jax.ShapeDtypeStruct((LR_PAD, GLA_QK), F32),
                   jax.ShapeDtypeStruct((1, GLA_QK), F32), jax.ShapeDtypeStruct((1, GLA_DV), F32)],
        scratch_shapes=[pltpu.VMEM((GLA_DV, GLA_QK), F32)],
        compiler_params=_cparams(("arbitrary",)),
    )(proj, proj, proj, proj, proj, wg, bg, go, states, dy)


S5_G = 32
S5_P = 64
S5_C = 16
S5_N = S5_G * S5_P
S5_TC = 256
SUB = 8


def _s5_disc(a_re, a_im, ls, b_re, b_im):
    step = jnp.exp(ls)
    mag = jnp.exp(a_re * step)
    lr, li = mag * jnp.cos(a_im * step), mag * jnp.sin(a_im * step)
    inv = 1.0 / (a_re * a_re + a_im * a_im)
    nr, ni = lr - 1.0, li
    cr, ci = (nr * a_re + ni * a_im) * inv, (ni * a_re - nr * a_im) * inv
    return lr, li, cr * b_re - ci * b_im, cr * b_im + ci * b_re


def s5_disc_fwd(a_re, a_im, ls, b_re, b_im, *, name="s5_disc"):
    def body(ar, ai, l, br, bi, o0, o1, o2, o3):
        for o, val in zip((o0, o1, o2, o3), _s5_disc(ar[...], ai[...], l[...], br[...], bi[...])):
            o[...] = val

    sds = jax.ShapeDtypeStruct(a_re.shape, F32)
    return pl.pallas_call(body, name=name, out_shape=[sds] * 4)(a_re, a_im, ls, b_re, b_im)


def s5_disc_bwd(a_re, a_im, ls, b_re, b_im, cts, *, name="s5_disc_bwd"):
    def body(ar, ai, l, br, bi, c0, c1, c2, c3, dar, dai, dl, dbr, dbi):
        _, vjp = jax.vjp(_s5_disc, ar[...], ai[...], l[...], br[...], bi[...])
        g = vjp((c0[...], c1[...], c2[...], c3[...]))
        for o, val in zip((dar, dai, dl), g[:3]):
            o[...] = jnp.sum(val, axis=-1, keepdims=True)
        dbr[...] = g[3]
        dbi[...] = g[4]

    col = jax.ShapeDtypeStruct((a_re.shape[0], 1), F32)
    sds = jax.ShapeDtypeStruct(a_re.shape, F32)
    return pl.pallas_call(body, name=name, out_shape=[col, col, col, sds, sds])(a_re, a_im, ls, b_re, b_im, *cts)


def _gelu(y):
    return 0.5 * y * (1.0 + jnp.tanh(0.7978845608028654 * (y + 0.044715 * (y * y * y))))


def _s5_powers(lam, conj):
    lr, li = lam[:, :S5_N], lam[:, S5_N:]
    if conj:
        li = -li
    rows, pr, pi = [], lr, li
    for _ in range(SUB):
        rows.append((pr, pi))
        pr, pi = pr * lr - pi * li, pr * li + pi * lr
    return rows


def _s5_table(rows, reverse):
    ridx = lax.broadcasted_iota(jnp.int32, (SUB, S5_N), 0)
    tr = jnp.zeros((SUB, S5_N), F32)
    ti = jnp.zeros((SUB, S5_N), F32)
    for i in range(SUB):
        pr, pi = rows[SUB - 1 - i] if reverse else rows[i]
        tr = jnp.where(ridx == i, pr, tr)
        ti = jnp.where(ridx == i, pi, ti)
    return tr, ti


def _s5_scan(buf, lam, carry_ref, reverse):
    tc = buf.shape[0]
    nblk = tc // SUB
    rows = _s5_powers(lam, reverse)
    tr, ti = _s5_table(rows, reverse)
    ridx = lax.broadcasted_iota(jnp.int32, (SUB, S5_N), 0)
    steps = []
    for sft, (pr, pi) in ((1, rows[0]), (2, rows[1]), (4, rows[3])):
        keep = (ridx < SUB - sft) if reverse else (ridx >= sft)
        steps.append((SUB - sft if reverse else sft, jnp.where(keep, pr, 0.0), jnp.where(keep, pi, 0.0)))

    def block(j, carry):
        jj = (nblk - 1 - j) if reverse else j
        at = pl.ds(pl.multiple_of(jj * SUB, SUB), SUB)
        re, im = buf[at, :S5_N], buf[at, S5_N:]
        for rot, pr, pi in steps:
            sre, sim = pltpu.roll(re, rot, 0), pltpu.roll(im, rot, 0)
            re, im = re + pr * sre - pi * sim, im + pr * sim + pi * sre
        cr, ci = carry
        re, im = re + tr * cr - ti * ci, im + tr * ci + ti * cr
        buf[at, :S5_N] = re
        buf[at, S5_N:] = im
        edge = 0 if reverse else SUB - 1
        return re[edge : edge + 1, :], im[edge : edge + 1, :]

    c0 = (carry_ref[0:1, :S5_N], carry_ref[0:1, S5_N:])
    cr, ci = lax.fori_loop(0, nblk, block, c0)
    carry_ref[:, :S5_N] = jnp.broadcast_to(cr, (SUB, S5_N))
    carry_ref[:, S5_N:] = jnp.broadcast_to(ci, (SUB, S5_N))


def s5_fwd(proj, ucol, bd, cd, lam, dskip, wglu, bglu, *, name="s5_fwd"):
    s = proj.shape[0]
    tc = _pick(s, (S5_TC, 128, 64, 8))
    assert ucol % MIX_W == 0

    def body(u_ref, bd_ref, cd_ref, lam_ref, d_ref, w_ref, b_ref, y_ref, xs_ref, carry):
        @pl.when(pl.program_id(0) == 0)
        def _():
            carry[...] = jnp.zeros_like(carry)

        u = u_ref[...]
        xs_ref[...] = _dot(u, bd_ref[...])
        _s5_scan(xs_ref, lam_ref[...], carry, False)
        g = _gelu(_dot(xs_ref[...], cd_ref[...]) + d_ref[...] * u)
        y_ref[...] = g * _sigmoid(_dot(g, w_ref[...]) + b_ref[...])

    full = lambda shp: pl.BlockSpec(shp, lambda i: (0, 0))
    return pl.pallas_call(
        body,
        name=name,
        grid=(s // tc,),
        in_specs=[pl.BlockSpec((tc, MIX_W), lambda i: (i, ucol // MIX_W)), full((MIX_W, 2 * S5_N)), full((2 * S5_N, MIX_W)),
                  full((1, 2 * S5_N)), full((1, MIX_W)), full((MIX_W, MIX_W)), full((1, MIX_W))],
        out_specs=[pl.BlockSpec((tc, MIX_W), lambda i: (i, 0)), pl.BlockSpec((tc, 2 * S5_N), lambda i: (i, 0))],
        out_shape=[jax.ShapeDtypeStruct((s, MIX_W), F32), jax.ShapeDtypeStruct((s, 2 * S5_N), F32)],
        scratch_shapes=[pltpu.VMEM((SUB, 2 * S5_N), F32)],
        compiler_params=_cparams(("arbitrary",)),
    )(proj, bd, cd, lam, dskip, wglu, bglu)


def s5_bwd(proj, ucol, xs, bd, cd, lam, dskip, wglu, bglu, dya, *, name="s5_bwd"):
    s = proj.shape[0]
    tc = _pick(s, (S5_TC, 128, 64, 8))
    nch = s // tc
    per = tc // SUB

    def body(u_ref, xs_ref, xp_ref, bd_ref, cd_ref, lam_ref, d_ref, w_ref, b_ref, dya_ref,
             du_ref, adj_ref, g_ref, dpre_ref, dy_ref, dlam_ref, dd_ref, db_ref, buf, carry, lacc):
        i = pl.program_id(0)
        first = i == 0

        @pl.when(first)
        def _():
            carry[...] = jnp.zeros_like(carry)
            lacc[...] = jnp.zeros_like(lacc)

        u, x, dya_ = u_ref[...], xs_ref[...], dya_ref[...]
        y = _dot(x, cd_ref[...]) + d_ref[...] * u
        g, gelu_vjp = jax.vjp(_gelu, y)
        sg = _sigmoid(_dot(g, w_ref[...]) + b_ref[...])
        dpre = dya_ * g * sg * (1.0 - sg)
        (dy,) = gelu_vjp(dya_ * sg + _dot(dpre, w_ref[...], 1, 1))
        g_ref[...] = g.astype(BF16)
        dpre_ref[...] = dpre.astype(BF16)
        dy_ref[...] = dy.astype(BF16)
        db_part = jnp.sum(dpre, axis=0, keepdims=True)
        dd_part = jnp.sum(dy * u, axis=0, keepdims=True)
        buf[...] = _dot(dy, cd_ref[...], 1, 1)
        _s5_scan(buf, lam_ref[...], carry, True)
        a = buf[...]
        adj_ref[...] = a.astype(BF16)
        du_ref[...] = (dy * d_ref[...] + _dot(a, bd_ref[...], 1, 1)).astype(BF16)
        before = jnp.where(i == nch - 1, 0.0, xp_ref[SUB - 1 : SUB, :])
        ridx = lax.broadcasted_iota(jnp.int32, (tc, 2 * S5_N), 0)
        xprev = jnp.where(ridx == 0, before, pltpu.roll(x, 1, 0))
        ar, ai, xr, xi = a[:, :S5_N], a[:, S5_N:], xprev[:, :S5_N], xprev[:, S5_N:]
        lacc[:, :S5_N] += jnp.sum((ar * xr + ai * xi).reshape(per, SUB, S5_N), axis=0)
        lacc[:, S5_N:] += jnp.sum((ai * xr - ar * xi).reshape(per, SUB, S5_N), axis=0)
        _accum(db_ref, db_part, first)
        _accum(dd_ref, dd_part, first)

        @pl.when(i == nch - 1)
        def _():
            dlam_ref[...] = jnp.sum(lacc[...], axis=0, keepdims=True)

    rev = lambda i: nch - 1 - i
    full = lambda shp: pl.BlockSpec(shp, lambda i: (0, 0))
    rows = lambda w: pl.BlockSpec((tc, w), lambda i: (rev(i), 0))
    hb = jax.ShapeDtypeStruct((s, MIX_W), BF16)
    return pl.pallas_call(
        body,
        name=name,
        grid=(nch,),
        in_specs=[pl.BlockSpec((tc, MIX_W), lambda i: (rev(i), ucol // MIX_W)), rows(2 * S5_N),
                  pl.BlockSpec((SUB, 2 * S5_N), lambda i: (jnp.maximum(rev(i) * per - 1, 0), 0)),
                  full((MIX_W, 2 * S5_N)), full((2 * S5_N, MIX_W)), full((1, 2 * S5_N)), full((1, MIX_W)),
                  full((MIX_W, MIX_W)), full((1, MIX_W)), rows(MIX_W)],
        out_specs=[rows(MIX_W), rows(2 * S5_N), rows(MIX_W), rows(MIX_W), rows(MIX_W),
                   full((1, 2 * S5_N)), full((1, MIX_W)), full((1, MIX_W))],
        out_shape=[hb, jax.ShapeDtypeStruct((s, 2 * S5_N), BF16), hb, hb, hb,
                   jax.ShapeDtypeStruct((1, 2 * S5_N), F32), jax.ShapeDtypeStruct((1, MIX_W), F32),
                   jax.ShapeDtypeStruct((1, MIX_W), F32)],
        scratch_shapes=[pltpu.VMEM((tc, 2 * S5_N), F32), pltpu.VMEM((SUB, 2 * S5_N), F32), pltpu.VMEM((SUB, 2 * S5_N), F32)],
        compiler_params=_cparams(("arbitrary",)),
    )(proj, xs, xs, bd, cd, lam, dskip, wglu, bglu, dya)


def _bcast16(a):
    return jnp.broadcast_to(a.reshape(S5_N, 1), (S5_N, S5_C))


def _blockdiag(blocks):
    g, r, c = blocks.shape
    eye = jnp.eye(g, dtype=blocks.dtype)
    return (eye[:, None, :, None] * blocks[:, :, None, :]).reshape(g * r, g * c)


def _blockdiag_extract(dense, r, c):
    g = dense.shape[0] // r
    return jnp.einsum("grgc->grc", dense.reshape(g, r, g, c))


def s5_prepare(a_re, a_im, log_step, b_re, b_im, c_re, c_im):
    disc_in = (_bcast16(a_re), _bcast16(a_im), _bcast16(jnp.broadcast_to(log_step[:, None], (S5_G, S5_P))),
               b_re.reshape(S5_N, S5_C), b_im.reshape(S5_N, S5_C))
    lr, li, bbr, bbi = s5_disc_fwd(*disc_in)
    lam = jnp.concatenate([lr[:, 0], li[:, 0]]).reshape(1, 2 * S5_N)
    to_blocks = lambda t: _blockdiag(t.reshape(S5_G, S5_P, S5_C).transpose(0, 2, 1))
    bd = jnp.concatenate([to_blocks(bbr), to_blocks(bbi)], axis=1).astype(BF16)
    cd = jnp.concatenate([_blockdiag(c_re.transpose(0, 2, 1)), -_blockdiag(c_im.transpose(0, 2, 1))], axis=0).astype(BF16)
    return disc_in, lam, bd, cd


def s5_param_grads(disc_in, dlam, dbd, dcd):
    first_col = lambda v: jnp.pad(v.reshape(S5_N, 1), ((0, 0), (0, S5_C - 1)))
    from_blocks = lambda t: _blockdiag_extract(t, S5_C, S5_P).transpose(0, 2, 1).reshape(S5_N, S5_C)
    cts = (first_col(dlam[0, :S5_N]), first_col(dlam[0, S5_N:]), from_blocks(dbd[:, :S5_N]), from_blocks(dbd[:, S5_N:]))
    dar, dai, dls, dbr, dbi = s5_disc_bwd(*disc_in, cts)
    dcr = _blockdiag_extract(dcd[:S5_N], S5_P, S5_C).transpose(0, 2, 1)
    dci = -_blockdiag_extract(dcd[S5_N:], S5_P, S5_C).transpose(0, 2, 1)
    return (dar.reshape(S5_G, S5_P), dai.reshape(S5_G, S5_P), dls.reshape(S5_G, S5_P).sum(axis=1),
            dbr.reshape(S5_G, S5_P, S5_C), dbi.reshape(S5_G, S5_P, S5_C), dcr, dci)


N_DEV = 8
MESH_ID = pl.DeviceIdType.MESH
ANY = pl.BlockSpec(memory_space=pl.ANY)


def _me():
    return lax.axis_index("x"), lax.axis_index("y"), lax.axis_index("c")


def all_gather(blocks, *, by_core=False, name="all_gather"):
    na = len(blocks)
    shapes = [b.shape[1:] if by_core else b.shape for b in blocks]

    def body(*refs):
        ins, outs = refs[:na], refs[na : 2 * na]
        send_sems, recv_sems = refs[2 * na :]
        x, y, c = _me()
        me, sibling = (x, y, c), (x, y, 1 - c)
        xn, yn, diag = (1 - x, y, c), (x, 1 - y, c), (1 - x, 1 - y, c)
        slot = lambda p: 4 * p[0] + 2 * p[1] + p[2]

        def copy(a, k, block, to, half=None, src=None):
            dst = outs[a].at[slot(block)]
            if half is not None:
                rows = shapes[a][0] // 2
                dst = dst.at[pl.ds(half * rows, rows)]
            return pltpu.make_async_remote_copy(src_ref=dst if src is None else src, dst_ref=dst, send_sem=send_sems.at[a * 9 + k],
                                                recv_sem=recv_sems.at[a * 9 + k], device_id=to, device_id_type=MESH_ID)

        mine = [ins[a].at[c] if by_core else ins[a] for a in range(na)]
        sends = []

        def go(cp):
            cp.start()
            sends.append(cp)

        for a in range(na):
            go(copy(a, 0, me, sibling, src=mine[a]))
            go(copy(a, 1, me, xn, src=mine[a]))
            go(copy(a, 2, me, yn, src=mine[a]))
        for a in range(na):
            copy(a, 1, xn, me).wait_recv()
            go(copy(a, 3, xn, yn, half=0))
            go(copy(a, 5, xn, sibling))
            copy(a, 2, yn, me).wait_recv()
            go(copy(a, 4, yn, xn, half=1))
            go(copy(a, 6, yn, sibling))
        for a in range(na):
            copy(a, 3, diag, me, half=0).wait_recv()
            go(copy(a, 7, diag, sibling, half=0))
            copy(a, 4, diag, me, half=1).wait_recv()
            go(copy(a, 8, diag, sibling, half=1))
        for a in range(na):
            copy(a, 0, sibling, me).wait_recv()
            copy(a, 5, (1 - x, y, 1 - c), me).wait_recv()
            copy(a, 6, (x, 1 - y, 1 - c), me).wait_recv()
            copy(a, 7, (1 - x, 1 - y, 1 - c), me, half=0).wait_recv()
            copy(a, 8, (1 - x, 1 - y, 1 - c), me, half=1).wait_recv()
        for cp in sends:
            cp.wait_send()

    assert all(shp[0] % (64 // b.dtype.itemsize) == 0 for shp, b in zip(shapes, blocks)), shapes
    outs = pl.pallas_call(
        body,
        name=name,
        in_specs=[ANY] * na,
        out_specs=[ANY] * na,
        out_shape=[jax.ShapeDtypeStruct((N_DEV, *shp), b.dtype) for shp, b in zip(shapes, blocks)],
        scratch_shapes=[pltpu.SemaphoreType.DMA((na * 9,)), pltpu.SemaphoreType.DMA((na * 9,))],
    )(*blocks)
    x, y, c = _me()
    own = [lax.dynamic_index_in_dim(b, c, 0, keepdims=False) if by_core else b for b in blocks]
    return [lax.dynamic_update_index_in_dim(o, b, 4 * x + 2 * y + c, 0) for o, b in zip(outs, own)]


D2D_PIECES = 8


def _slabs(rows):
    n = D2D_PIECES if rows % (16 * D2D_PIECES) == 0 else 1
    return [pl.ds(i * (rows // n), rows // n) for i in range(n)]


def sibling_swap(arrs, *, name="sibling_swap"):
    na = len(arrs)
    pieces = [[(j, sl) for j in range(a.shape[0]) for sl in (_slabs(a.shape[1]) if a.shape[0] == 1 else _halves(a.shape[1]))]
              for a in arrs]
    base = np.cumsum([0] + [len(p) for p in pieces])

    def body(*refs):
        ins, outs = refs[:na], refs[na : 2 * na]
        send_sems, recv_sems = refs[2 * na :]
        x, y, c = _me()
        sends, recvs = [], []
        for a in range(na):
            for i, (j, sl) in enumerate(pieces[a]):
                k = int(base[a]) + i
                cp = pltpu.make_async_remote_copy(src_ref=ins[a].at[j, sl], dst_ref=outs[a].at[j, sl], send_sem=send_sems.at[k],
                                                  recv_sem=recv_sems.at[k], device_id=(x, y, 1 - c), device_id_type=MESH_ID)
                cp.start()
                sends.append(cp)
        for cp in sends:
            cp.wait_recv()
        for cp in sends:
            cp.wait_send()

    nsem = int(base[-1])
    return pl.pallas_call(
        body,
        name=name,
        in_specs=[ANY] * na,
        out_specs=[ANY] * na,
        out_shape=[jax.ShapeDtypeStruct(a.shape, a.dtype) for a in arrs],
        scratch_shapes=[pltpu.SemaphoreType.DMA((nsem,)), pltpu.SemaphoreType.DMA((nsem,))],
    )(*arrs)


def _halves(rows):
    return [pl.ds(0, rows // 2), pl.ds(rows // 2, rows // 2)] if rows % 32 == 0 else [pl.ds(0, rows)]


def chip_exchange(arrs, *, name="chip_exchange"):
    na = len(arrs)

    def body(*refs):
        ins, outs = refs[:na], refs[na : 2 * na]
        send_sems, recv_sems = refs[2 * na :]
        x, y, c = _me()
        my = 2 * x + y
        peers = []
        for k in range(1, N_CHIPS):
            px, py = (x + ((k >> 1) & 1)) % 2, (y + (k & 1)) % 2
            peers.append(((px, py, c), 2 * px + py))
        sends = []
        for a in range(na):
            for k, (peer, pidx) in enumerate(peers):
                cp = pltpu.make_async_remote_copy(src_ref=ins[a].at[pidx], dst_ref=outs[a].at[my], send_sem=send_sems.at[a * 3 + k],
                                                  recv_sem=recv_sems.at[a * 3 + k], device_id=peer, device_id_type=MESH_ID)
                cp.start()
                sends.append(cp)
        for a in range(na):
            for k, (peer, pidx) in enumerate(peers):
                pltpu.make_async_remote_copy(src_ref=ins[a].at[my], dst_ref=outs[a].at[pidx], send_sem=send_sems.at[a * 3 + k],
                                             recv_sem=recv_sems.at[a * 3 + k], device_id=peer, device_id_type=MESH_ID).wait_recv()
        for cp in sends:
            cp.wait_send()

    outs = pl.pallas_call(
        body,
        name=name,
        in_specs=[ANY] * na,
        out_specs=[ANY] * na,
        out_shape=[jax.ShapeDtypeStruct(a.shape, a.dtype) for a in arrs],
        scratch_shapes=[pltpu.SemaphoreType.DMA((na * 3,)), pltpu.SemaphoreType.DMA((na * 3,))],
    )(*arrs)
    chip = 2 * lax.axis_index("x") + lax.axis_index("y")
    return [lax.dynamic_update_index_in_dim(o, lax.dynamic_index_in_dim(a, chip, 0, keepdims=False), chip, 0)
            for o, a in zip(outs, arrs)]


def add_pair(a, b, *, name="add_pair"):
    n, r, c = a.shape
    tm = _row_tile(r, n * c * 2, 2 << 20)

    def body(a_ref, b_ref, o_ref):
        o_ref[...] = (a_ref[...].astype(F32) + b_ref[...].astype(F32)).astype(o_ref.dtype)

    blk = pl.BlockSpec((n, tm, c), lambda i: (0, i, 0))
    return pl.pallas_call(
        body, name=name, grid=(r // tm,), in_specs=[blk, blk], out_specs=blk, out_shape=jax.ShapeDtypeStruct(a.shape, a.dtype),
        compiler_params=_cparams(("arbitrary",)),
    )(a, b)


def _row_tile(rows, row_bytes, budget):
    best = None
    for t in range(8, rows + 1, 8):
        if rows % t == 0 and t * row_bytes <= budget:
            best = t
    return best or rows


def sum_blocks(a, *, name="sum_blocks"):
    n, r, c = a.shape
    tm = _row_tile(r, n * c * a.dtype.itemsize, 4 << 20)

    def body(a_ref, o_ref):
        acc = a_ref[0].astype(F32)
        for k in range(1, n):
            acc = acc + a_ref[k].astype(F32)
        o_ref[...] = acc

    return pl.pallas_call(
        body,
        name=name,
        grid=(r // tm,),
        in_specs=[pl.BlockSpec((n, tm, c), lambda i: (0, i, 0))],
        out_specs=pl.BlockSpec((tm, c), lambda i: (i, 0)),
        out_shape=jax.ShapeDtypeStruct((r, c), F32),
        compiler_params=_cparams(("arbitrary",)),
    )(a)


ADAM_LR = 0.001
ADAM_B1 = 0.9
ADAM_B2 = 0.999
ADAM_EPS = 1e-08
ADAM_WD = 0.01
ADAM_STEP = 10


def adamw(w, g, m, v, *, name="adamw"):
    r, c = w.shape[-2:]
    tm = _row_tile(r, c * 4, 1 << 20)
    if w.ndim == 3:
        blk = pl.BlockSpec((None, tm, c), lambda l, i: (l, i, 0))
        grid = (w.shape[0], r // tm)
    else:
        blk = pl.BlockSpec((tm, c), lambda i: (i, 0))
        grid = (r // tm,)

    def body(w_ref, g_ref, m_ref, v_ref, d_ref, nm_ref, nv_ref):
        g_ = g_ref[...]
        m_ = ADAM_B1 * m_ref[...] + (1.0 - ADAM_B1) * g_
        v_ = ADAM_B2 * v_ref[...] + (1.0 - ADAM_B2) * (g_ * g_)
        m_hat = m_ / (1.0 - ADAM_B1**ADAM_STEP)
        v_hat = v_ / (1.0 - ADAM_B2**ADAM_STEP)
        d_ref[...] = -ADAM_LR * (m_hat / (jnp.sqrt(v_hat) + ADAM_EPS) + ADAM_WD * w_ref[...])
        nm_ref[...] = m_
        nv_ref[...] = v_

    sds = jax.ShapeDtypeStruct(w.shape, F32)
    return pl.pallas_call(
        body, name=name, grid=grid, in_specs=[blk] * 4, out_specs=[blk] * 3, out_shape=[sds] * 3,
        compiler_params=_cparams(("arbitrary",) * len(grid)),
    )(w, g, m, v)


WEIGHTS = ["g_mix", "w_in", "s5_a_re", "s5_a_im", "s5_log_step", "s5_b_re", "s5_b_im", "s5_c_re", "s5_c_im", "s5_d", "w_glu",
           "b_glu", "w_gla_gate", "b_gla_gate", "g_gla_out", "w_branch", "w_out", "g_mem", "g_cross", "w_xq", "w_xkv", "w_xo",
           "g_mlp", "w_up", "w_down", "g_final"]
SHARDED = {"w_in": 1, "w_glu": 0, "w_branch": 2, "w_out": 0, "w_xq": 0, "w_xkv": 1, "w_xo": 0, "w_up": 1, "w_down": 0}
DEPTH = 2
N_CHIPS = 4
D_IN = 9744
C_U, C_QG, C_KG, C_VG, C_LR, C_RG, C_QA, C_KA, C_VA, C_GATE = 0, 512, 768, 1024, 1536, 1552, 2064, 3600, 5136, 6672
A_GATE, A_U, A_QG, A_KG, A_VG, A_RG, A_Q0, A_LR, A_W = 0, 3072, 3584, 3840, 4096, 4608, 5120, 6656, 7168
GLA_COLS = (A_QG, A_KG, A_VG, A_LR, A_RG)


def _split_w_in(w):
    att = lambda g: [w[:, c + MIX_W * g : c + MIX_W * (g + 1)] for c in (C_QA, C_KA, C_VA)]
    a = jnp.concatenate([w[:, C_GATE:D_IN], w[:, C_U:C_QG], w[:, C_QG:C_KG], w[:, C_KG:C_VG], w[:, C_VG:C_LR], w[:, C_RG:C_QA],
                         *att(0), w[:, C_LR:C_RG], jnp.zeros((w.shape[0], A_W - A_LR - 16), w.dtype)], axis=1)
    return a, jnp.concatenate(att(1), axis=1), jnp.concatenate(att(2), axis=1)


def _join_w_in(a, b, c):
    att = lambda k: [a[:, A_Q0 + MIX_W * k : A_Q0 + MIX_W * (k + 1)], b[:, MIX_W * k : MIX_W * (k + 1)], c[:, MIX_W * k : MIX_W * (k + 1)]]
    return jnp.concatenate([a[:, A_U:A_QG], a[:, A_QG:A_KG], a[:, A_KG:A_VG], a[:, A_VG:A_RG], a[:, A_LR : A_LR + 16], a[:, A_RG:A_Q0],
                            *att(0), *att(1), *att(2), a[:, A_GATE:A_U]], axis=1)


def _by_residue(a, d):
    s = a.shape[0]
    return a if d == 1 else a.reshape(s // d, d, -1).transpose(1, 0, 2).reshape(s, -1)


def _in_order(a, d):
    s = a.shape[0]
    return a if d == 1 else a.reshape(d, s // d, -1).transpose(1, 0, 2).reshape(s, -1)


def _pack(arrs):
    flat = jnp.concatenate([a.reshape(-1) for a in arrs])
    n = flat.shape[0]
    rows = -(-n // (256 * 128)) * 256
    return jnp.pad(flat, (0, rows * 128 - n)).reshape(rows, 128)


def _unpack(packed, like):
    flat, out, o = packed.reshape(-1), [], 0
    for a in like:
        n = math.prod(a.shape)
        out.append(flat[o : o + n].reshape(a.shape))
        o += n
    return out


def _layer_fwd(x0, p, kv):
    h = rmsnorm(x0, p["g_mix"], name="mix_norm")
    hs = [h, _by_residue(h, 4), _by_residue(h, 16)]
    proj = [matmul(hs[g], p["w_in_seg"][g], out_dtype=F32 if g == 0 else BF16, name="in_proj") for g in range(3)]
    ya, xs = s5_fwd(proj[0], A_U, p["bd"], p["cd"], p["lam"], p["s5_d"], p["w_glu"], p["b_glu"])
    yb, gst = gla_fwd(proj[0], GLA_COLS, p["w_gate"], p["b_gla_gate"], p["g_gla_out"])
    att = [attn_fwd(proj[g], A_Q0 if g == 0 else 0, g) for g in range(3)]
    outs = [_in_order(att[g][0], ATT_GROUPS[g][1]) for g in range(3)]
    lses = [_in_order(att[g][1], ATT_GROUPS[g][1]) for g in range(3)]
    x1, merged, yc = merge_fwd(x0, proj[0], ya, yb, outs, lses, p["w_branch"], p["w_out"])
    x2 = cross_fwd(x1, p["g_cross"], p["w_xq"], p["w_xo"], kv)
    x3 = mlp_fwd(x2, p["g_mlp"], p["w_up"], p["w_down"])
    saved = dict(x0=x0, x1=x1, x2=x2, hs=hs, proj=proj, ya=ya, xs=xs, yb=yb, gst=gst, att=att, outs=outs, lses=lses,
                 merged=merged, yc=yc)
    return x3, saved


def _layer_bwd(dx3, p, kv, memn, sv):
    g = {}
    tn = lambda a, b, **kw: matmul(a, b, ta=True, out_dtype=BF16, name="wgrad", **kw)
    dx2, g["g_mlp"], h3, dup, act, dx3b = mlp_bwd(sv["x2"], p["g_mlp"], p["w_up"], p["w_down"], dx3)
    g["w_up"], g["w_down"] = tn(h3, dup), tn(act, dx3b)
    dx1, g["g_cross"], h2, dq, o, dkv = cross_bwd(sv["x1"], p["g_cross"], p["w_xq"], p["w_xo"], kv, dx2)
    g["w_xq"], g["w_xo"], g["w_xkv"] = tn(h2, dq), tn(o, dx2), tn(memn, dkv)
    dmemn = matmul(dkv, p["w_xkv"], tb=True, name="dmem")
    proj = sv["proj"]
    r = merge_bwd(dx1, proj[0], sv["ya"], sv["yb"], sv["yc"], sv["outs"], sv["lses"], p["w_branch"], p["w_out"])
    dgl, dz, dya, dyb, douts, dlses = r[0], r[1:4], r[4], r[5], r[6:9], r[9:12]
    g["w_branch"] = jnp.stack([tn(y, dz[n]) for n, y in enumerate((sv["ya"], sv["yb"], sv["yc"]))])
    g["w_out"] = tn(sv["merged"], dx1)
    datt = []
    for k in range(3):
        dil = ATT_GROUPS[k][1]
        datt.append(attn_bwd(proj[k], A_Q0 if k == 0 else 0, k, sv["att"][k][0], sv["att"][k][1],
                             _by_residue(douts[k], dil), _by_residue(dlses[k], dil)))
    du, adj, gg, dpre, dy, dlam, g["s5_d"], g["b_glu"] = s5_bwd(proj[0], A_U, sv["xs"], p["bd"], p["cd"], p["lam"], p["s5_d"],
                                                                 p["w_glu"], p["b_glu"], dya)
    dbd = matmul(proj[0], adj, ta=True, a_col=(A_U, MIX_W), name="s5_dbd")
    dcd = matmul(sv["xs"], dy, ta=True, name="s5_dcd")
    g["w_glu"] = tn(gg, dpre)
    (g["s5_a_re"], g["s5_a_im"], g["s5_log_step"], g["s5_b_re"], g["s5_b_im"], g["s5_c_re"],
     g["s5_c_im"]) = s5_param_grads(p["disc_in"], dlam, dbd, dcd)
    dqg, dkg, dvg, dlr, drg, dwg, g["b_gla_gate"], g["g_gla_out"] = gla_bwd(proj[0], GLA_COLS, p["w_gate"], p["b_gla_gate"],
                                                                             p["g_gla_out"], sv["gst"], dyb)
    g["w_gla_gate"] = dwg[:GLA_GATE_RANK]
    s = dx3.shape[0]
    dproj = [jnp.concatenate([dgl, du, dqg, dkg, dvg, drg, *datt[0], dlr, jnp.zeros((s, A_W - A_LR - LR_PAD), BF16)], axis=1),
             jnp.concatenate(datt[1], axis=1), jnp.concatenate(datt[2], axis=1)]
    g["w_in"] = _join_w_in(*[tn(sv["hs"][k], dproj[k]) for k in range(3)])
    dhs = [_in_order(matmul(dproj[k], p["w_in_seg"][k], tb=True, name="in_proj_bwd"), ATT_GROUPS[k][1]) for k in range(3)]
    dx0, g["g_mix"] = rmsnorm_bwd(sv["x0"], p["g_mix"], dhs, dx1, name="mix_norm_bwd")
    return dx0, g, dmemn


GLA_GATE_RANK = 16


def kernel(x, mem, *rest):
    nw = len(WEIGHTS)
    w = dict(zip(WEIGHTS, rest[:nw]))
    target = rest[nw]
    m = dict(zip(WEIGHTS, rest[nw + 1 : 2 * nw + 1]))
    v = dict(zip(WEIGHTS, rest[2 * nw + 1 : 3 * nw + 1]))
    x0, memx, target = x[0], mem[0], target[0]
    chip = 2 * lax.axis_index("x") + lax.axis_index("y")

    big = list(SHARDED)
    gate_pad = jnp.pad(w["w_gla_gate"], ((0, 0), (0, 0), (0, LR_PAD - w["w_gla_gate"].shape[2])))
    flat2 = lambda t: t.reshape(DEPTH, -1, t.shape[-1])
    gathered = all_gather([flat2(w[n].astype(BF16)) for n in big] + [gate_pad], by_core=True, name="gather_weights")
    gathered = [t.reshape(N_CHIPS, DEPTH, *s.shape[1:]) for t, s in zip(gathered, [w[n] for n in big] + [gate_pad])]
    full = {n: [jnp.concatenate([t[j, l] for j in range(N_CHIPS)], axis=SHARDED[n]) for l in range(DEPTH)]
            for n, t in zip(big, gathered)}
    gate_full = [jnp.concatenate([gathered[-1][j, l][:, : w["w_gla_gate"].shape[2]] for j in range(N_CHIPS)], axis=1)
                 for l in range(DEPTH)]

    memn = rmsnorm(memx, w["g_mem"], name="mem_norm")
    params, kvs = [], []
    for l in range(DEPTH):
        p = {n: full[n][l] for n in big if n != "w_in"}
        p["w_in_seg"] = _split_w_in(full["w_in"][l])
        p["w_gate"] = jnp.pad(gate_full[l], ((0, LR_PAD - GLA_GATE_RANK), (0, 0))).astype(BF16)
        for n in ("g_mix", "g_cross", "g_mlp"):
            p[n] = w[n][l]
        for n in ("s5_d", "b_glu", "b_gla_gate", "g_gla_out"):
            p[n] = w[n][l].reshape(1, -1)
        p["disc_in"], p["lam"], p["bd"], p["cd"] = s5_prepare(*[w[n][l] for n in WEIGHTS[2:9]])
        params.append(p)
        kvs.append(matmul(memn, p["w_xkv"], out_dtype=BF16, name="mem_kv"))

    xl, saved = x0, []
    for l in range(DEPTH):
        xl, sv = _layer_fwd(xl, params[l], kvs[l])
        saved.append(sv)
    loss8, dx, dg_final = loss_head(xl, w["g_final"], target)
    grads, dmem = [None] * DEPTH, []
    for l in reversed(range(DEPTH)):
        dx, grads[l], dm = _layer_bwd(dx, params[l], kvs[l], memn, saved[l])
        dmem.append(dm)
    _, dg_mem = rmsnorm_bwd(memx, w["g_mem"], dmem, jnp.zeros_like(memx), name="mem_norm_bwd")

    core = lax.axis_index("c")

    def halves(n):
        ax = SHARDED[n]
        keep, send = [], []
        for j in range(N_CHIPS):
            p0, p1 = [jnp.split(grads[l][n], N_CHIPS, axis=ax)[j] for l in range(DEPTH)]
            keep.append(jnp.where(core == 0, p0, p1))
            send.append(jnp.where(core == 0, p1, p0))
        flat = lambda ps: jnp.stack(ps).reshape(N_CHIPS, -1, ps[0].shape[-1])
        return flat(keep), flat(send)

    kept, sent = zip(*[halves(n) for n in big])
    theirs = sibling_swap(list(sent), name="swap_layers")
    chip_sums = [add_pair(a, b, name="add_cores") for a, b in zip(kept, theirs)]
    landed = chip_exchange(chip_sums, name="exchange_grads")
    reduced = [sum_blocks(t, name="sum_grads") for t in landed]
    other = sibling_swap([t[None] for t in reduced], name="pair_layers")
    out_g = {}
    for n, mine, theirs in zip(big, reduced, other):
        both = jnp.where(core == 0, jnp.stack([mine, theirs[0]]), jnp.stack([theirs[0], mine]))
        out_g[n] = both.reshape(w[n].shape)

    small = [n for n in WEIGHTS if n not in SHARDED]
    local_small = []
    for n in small:
        if n == "g_mem":
            local_small.append(dg_mem.reshape(w[n].shape))
        elif n == "g_final":
            local_small.append(dg_final.reshape(w[n].shape))
        elif n == "w_gla_gate":
            local_small.append(jnp.stack([grads[l][n] for l in range(DEPTH)]))
        else:
            local_small.append(jnp.stack([grads[l][n].reshape(w[n].shape[1:]) for l in range(DEPTH)]))
    packed = _pack(local_small + [loss8[0, :1]])
    (every,) = all_gather([packed], name="gather_small")
    summed = _unpack(sum_blocks(every, name="sum_small"), local_small + [loss8[0, :1]])
    loss = summed[-1].reshape(())
    for n, t in zip(small, summed[:-1]):
        if n == "w_gla_gate":
            t = lax.dynamic_slice_in_dim(t, chip * w[n].shape[2], w[n].shape[2], axis=2)
        out_g[n] = t

    delta, new_m, new_v = {}, {}, {}
    for n in big:
        as3d = lambda t: t.reshape(DEPTH, -1, t.shape[-1])
        r = [t.reshape(w[n].shape) for t in adamw(as3d(w[n]), as3d(out_g[n]), as3d(m[n]), as3d(v[n]), name="adamw")]
        delta[n], new_m[n], new_v[n] = r
    like = [w[n] for n in small]
    r = adamw(_pack(like), _pack([out_g[n] for n in small]), _pack([m[n] for n in small]), _pack([v[n] for n in small]),
              name="adamw_small")
    for d, t in zip((delta, new_m, new_v), r):
        d.update(zip(small, _unpack(t, like)))
    return (loss, dx[None], *[out_g[n] for n in WEIGHTS], *[delta[n] for n in WEIGHTS], *[new_m[n] for n in WEIGHTS],
            *[new_v[n] for n in WEIGHTS])
```

```python
import functools
import math

import jax
import jax.numpy as jnp
import numpy as np
from jax import lax
from jax.experimental import pallas as pl
from jax.experimental.pallas import tpu as pltpu

F32 = jnp.float32
BF16 = jnp.bfloat16

VMEM_LIMIT_BYTES = 56 * 1024 * 1024
MATMUL_VMEM_BYTES = 36 * 1024 * 1024


def _cparams(sem):
    return pltpu.CompilerParams(dimension_semantics=sem, vmem_limit_bytes=VMEM_LIMIT_BYTES)


def _dot(a, b, ca=1, cb=0):
    return lax.dot_general(a.astype(BF16), b.astype(BF16), (((ca,), (cb,)), ((), ())), preferred_element_type=F32)


def _pick(n, prefs):
    for p in prefs:
        if n % p == 0:
            return p
    return n


def matmul(a, b, *, ta=False, tb=False, out_dtype=F32, name="mm", a_col=None, b_col=None):
    a_off, a_w = a_col if a_col is not None else (0, a.shape[1])
    b_off, b_w = b_col if b_col is not None else (0, b.shape[1])
    if ta:
        kk, m = a.shape[0], a_w
    else:
        m, kk = a.shape[0], a_w
    if tb:
        n, kb = b.shape[0], b_w
    else:
        kb, n = b.shape[0], b_w
    assert kk == kb, (a.shape, b.shape, ta, tb)
    tm = _pick(m, (512, 256, 128))
    tn = _pick(n, (1024, 512, 256, 128))
    sa, sb, so = a.dtype.itemsize, b.dtype.itemsize, jnp.dtype(out_dtype).itemsize
    fits = lambda t: 2 * (tm * t * sa + t * tn * sb) + tm * tn * (4 + 2 * so) <= MATMUL_VMEM_BYTES
    tk = max([t for t in range(128, kk + 1, 128) if kk % t == 0 and fits(t)] or [_pick(kk, (128,))])
    nk = kk // tk
    a_bytes, b_bytes = m * kk * sa, kk * n * sb
    n_outer = nk == 1 and b_bytes + a_bytes * (n // tn) < a_bytes + b_bytes * (m // tm)
    ij = (lambda g0, g1: (g1, g0)) if n_outer else (lambda g0, g1: (g0, g1))
    if ta:
        assert a_off % tm == 0
        a_spec = pl.BlockSpec((tk, tm), lambda g0, g1, k, o=a_off // tm: (k, ij(g0, g1)[0] + o))
    else:
        assert a_off % tk == 0
        a_spec = pl.BlockSpec((tm, tk), lambda g0, g1, k, o=a_off // tk: (ij(g0, g1)[0], k + o))
    if tb:
        assert b_off % tk == 0
        b_spec = pl.BlockSpec((tn, tk), lambda g0, g1, k, o=b_off // tk: (ij(g0, g1)[1], k + o))
    else:
        assert b_off % tn == 0
        b_spec = pl.BlockSpec((tk, tn), lambda g0, g1, k, o=b_off // tn: (k, ij(g0, g1)[1] + o))

    def body(a_ref, b_ref, o_ref, *acc):
        k = pl.program_id(2)
        p = _dot(a_ref[...], b_ref[...], 0 if ta else 1, 1 if tb else 0)
        if nk == 1:
            o_ref[...] = p.astype(o_ref.dtype)
            return
        (acc_ref,) = acc

        @pl.when(k == 0)
        def _():
            acc_ref[...] = p

        @pl.when(jnp.logical_and(k > 0, k < nk - 1))
        def _():
            acc_ref[...] += p

        @pl.when(k == nk - 1)
        def _():
            o_ref[...] = (acc_ref[...] + p).astype(o_ref.dtype)

    return pl.pallas_call(
        body,
        name=name,
        grid=(n // tn, m // tm, nk) if n_outer else (m // tm, n // tn, nk),
        in_specs=[a_spec, b_spec],
        out_specs=pl.BlockSpec((tm, tn), lambda g0, g1, k: ij(g0, g1)),
        out_shape=jax.ShapeDtypeStruct((m, n), out_dtype),
        scratch_shapes=[pltpu.VMEM((tm, tn), F32)] if nk > 1 else [],
        compiler_params=_cparams(("parallel", "parallel", "arbitrary")),
    )(a, b)


RMS_EPS = 1e-6
ROW_TILE = 512


def _rms_fwd(x, g):
    r = lax.rsqrt(jnp.mean(x * x, axis=-1, keepdims=True) + RMS_EPS)
    return x * r * g


def _rms_bwd(x, g, dh):
    r = lax.rsqrt(jnp.mean(x * x, axis=-1, keepdims=True) + RMS_EPS)
    xh = x * r
    dg = jnp.sum(dh * xh, axis=0, keepdims=True)
    dxh = dh * g
    dx = r * (dxh - xh * jnp.mean(dxh * xh, axis=-1, keepdims=True))
    return dx, dg


def _accum(ref, val, first):
    @pl.when(first)
    def _():
        ref[...] = val

    @pl.when(jnp.logical_not(first))
    def _():
        ref[...] += val


def rmsnorm(x, g, *, out_dtype=BF16, name="rmsnorm"):
    s, d = x.shape
    tm = _pick(s, (ROW_TILE, 256, 128, 8))

    def body(x_ref, g_ref, o_ref):
        o_ref[...] = _rms_fwd(x_ref[...], g_ref[...]).astype(o_ref.dtype)

    return pl.pallas_call(
        body,
        name=name,
        grid=(s // tm,),
        in_specs=[pl.BlockSpec((tm, d), lambda i: (i, 0)), pl.BlockSpec((1, d), lambda i: (0, 0))],
        out_specs=pl.BlockSpec((tm, d), lambda i: (i, 0)),
        out_shape=jax.ShapeDtypeStruct((s, d), out_dtype),
        compiler_params=_cparams(("arbitrary",)),
    )(x, g.reshape(1, d))


def rmsnorm_bwd(x, g, dhs, dres, *, name="rmsnorm_bwd"):
    s, d = x.shape
    tm = _pick(s, (ROW_TILE, 256, 128, 8))
    n = len(dhs)

    def body(x_ref, g_ref, *refs):
        dh_refs, dres_ref, dx_ref, dg_ref = refs[:n], refs[n], refs[n + 1], refs[n + 2]
        dh = dh_refs[0][...].astype(F32)
        for r in dh_refs[1:]:
            dh = dh + r[...].astype(F32)
        dx, dg = _rms_bwd(x_ref[...], g_ref[...], dh)
        dx_ref[...] = dres_ref[...] + dx
        _accum(dg_ref, dg, pl.program_id(0) == 0)

    row = pl.BlockSpec((tm, d), lambda i: (i, 0))
    vec = pl.BlockSpec((1, d), lambda i: (0, 0))
    return pl.pallas_call(
        body,
        name=name,
        grid=(s // tm,),
        in_specs=[row, vec] + [row] * n + [row],
        out_specs=[row, vec],
        out_shape=[jax.ShapeDtypeStruct((s, d), F32), jax.ShapeDtypeStruct((1, d), F32)],
        compiler_params=_cparams(("arbitrary",)),
    )(x, g.reshape(1, d), *dhs, dres)


def loss_head(x, g, target, *, name="loss_head"):
    s, d = x.shape
    tm = _pick(s, (ROW_TILE, 256, 128, 8))

    def body(x_ref, g_ref, t_ref, l_ref, dx_ref, dg_ref):
        x_, g_ = x_ref[...], g_ref[...]
        e = _rms_fwd(x_, g_) - t_ref[...]
        part = 0.5 * jnp.sum(jnp.sum(e * e, axis=-1, keepdims=True), axis=0, keepdims=True) / d
        dx, dg = _rms_bwd(x_, g_, e * (1.0 / d))
        dx_ref[...] = dx
        first = pl.program_id(0) == 0
        _accum(dg_ref, dg, first)
        _accum(l_ref, jnp.broadcast_to(part, (8, 128)), first)

    row = pl.BlockSpec((tm, d), lambda i: (i, 0))
    vec = pl.BlockSpec((1, d), lambda i: (0, 0))
    return pl.pallas_call(
        body,
        name=name,
        grid=(s // tm,),
        in_specs=[row, vec, row],
        out_specs=[pl.BlockSpec((8, 128), lambda i: (0, 0)), row, vec],
        out_shape=[jax.ShapeDtypeStruct((8, 128), F32), jax.ShapeDtypeStruct((s, d), F32), jax.ShapeDtypeStruct((1, d), F32)],
        compiler_params=_cparams(("arbitrary",)),
    )(x, g.reshape(1, d), target)


FF_TILE = 1024


def mlp_fwd(x, g, w_up, w_down, *, name="mlp_fwd"):
    s, d = x.shape
    ff = w_up.shape[1]
    tm, tf = _pick(s, (ROW_TILE, 256, 128)), _pick(ff, (FF_TILE, 512, 256, 128))
    nf = ff // tf

    def body(x_ref, g_ref, wu_ref, wd_ref, o_ref, h_ref, acc_ref):
        f = pl.program_id(1)

        @pl.when(f == 0)
        def _():
            h_ref[...] = _rms_fwd(x_ref[...], g_ref[...]).astype(BF16)
            acc_ref[...] = x_ref[...]

        up = jnp.maximum(_dot(h_ref[...], wu_ref[...]), 0.0)
        acc_ref[...] += _dot(up * up, wd_ref[...])

        @pl.when(f == nf - 1)
        def _():
            o_ref[...] = acc_ref[...]

    return pl.pallas_call(
        body,
        name=name,
        grid=(s // tm, nf),
        in_specs=[
            pl.BlockSpec((tm, d), lambda i, f: (i, 0)),
            pl.BlockSpec((1, d), lambda i, f: (0, 0)),
            pl.BlockSpec((d, tf), lambda i, f: (0, f)),
            pl.BlockSpec((tf, d), lambda i, f: (f, 0)),
        ],
        out_specs=pl.BlockSpec((tm, d), lambda i, f: (i, 0)),
        out_shape=jax.ShapeDtypeStruct((s, d), F32),
        scratch_shapes=[pltpu.VMEM((tm, d), BF16), pltpu.VMEM((tm, d), F32)],
        compiler_params=_cparams(("arbitrary", "arbitrary")),
    )(x, g.reshape(1, d), w_up, w_down)


def mlp_bwd(x, g, w_up, w_down, dy, *, name="mlp_bwd"):
    s, d = x.shape
    ff = w_up.shape[1]
    tm, tf = _pick(s, (ROW_TILE, 256, 128)), _pick(ff, (FF_TILE, 512, 256, 128))
    nf = ff // tf

    def body(x_ref, g_ref, wu_ref, wd_ref, dy_ref, dx_ref, dg_ref, h_ref, dup_ref, act_ref, dyb_ref, acc_ref):
        i, f = pl.program_id(0), pl.program_id(1)

        @pl.when(f == 0)
        def _():
            h_ref[...] = _rms_fwd(x_ref[...], g_ref[...]).astype(BF16)
            dyb_ref[...] = dy_ref[...].astype(BF16)
            acc_ref[...] = jnp.zeros_like(acc_ref)

        up = jnp.maximum(_dot(h_ref[...], wu_ref[...]), 0.0)
        dact = _dot(dyb_ref[...], wd_ref[...], 1, 1)
        dup = (2.0 * up * dact).astype(BF16)
        dup_ref[...] = dup
        act_ref[...] = (up * up).astype(BF16)
        acc_ref[...] += _dot(dup, wu_ref[...], 1, 1)

        @pl.when(f == nf - 1)
        def _():
            dx, dg = _rms_bwd(x_ref[...], g_ref[...], acc_ref[...])
            dx_ref[...] = dy_ref[...] + dx
            _accum(dg_ref, dg, i == 0)

    row = pl.BlockSpec((tm, d), lambda i, f: (i, 0))
    vec = pl.BlockSpec((1, d), lambda i, f: (0, 0))
    wide = pl.BlockSpec((tm, tf), lambda i, f: (i, f))
    return pl.pallas_call(
        body,
        name=name,
        grid=(s // tm, nf),
        in_specs=[row, vec, pl.BlockSpec((d, tf), lambda i, f: (0, f)), pl.BlockSpec((tf, d), lambda i, f: (f, 0)), row],
        out_specs=[row, vec, row, wide, wide, row],
        out_shape=[
            jax.ShapeDtypeStruct((s, d), F32),
            jax.ShapeDtypeStruct((1, d), F32),
            jax.ShapeDtypeStruct((s, d), BF16),
            jax.ShapeDtypeStruct((s, ff), BF16),
            jax.ShapeDtypeStruct((s, ff), BF16),
            jax.ShapeDtypeStruct((s, d), BF16),
        ],
        scratch_shapes=[pltpu.VMEM((tm, d), F32)],
        compiler_params=_cparams(("arbitrary", "arbitrary")),
    )(x, g.reshape(1, d), w_up, w_down, dy)


X_HEADS = 4


def _softmax_rows(s):
    m = jnp.max(s, axis=-1, keepdims=True)
    e = jnp.exp(s - m)
    return e / jnp.sum(e, axis=-1, keepdims=True)


def cross_fwd(x, g, wq, wo, kv, *, name="cross_fwd"):
    s, d = x.shape
    ml = kv.shape[0]
    dh = d // X_HEADS
    tm = _pick(s, (ROW_TILE, 256, 128))
    scale = dh**-0.5

    def body(x_ref, g_ref, wq_ref, wo_ref, kv_ref, o_ref):
        x_ = x_ref[...]
        q = _dot(_rms_fwd(x_, g_ref[...]), wq_ref[...])
        outs = []
        for hd in range(X_HEADS):
            kh = kv_ref[:, hd * dh : (hd + 1) * dh]
            vh = kv_ref[:, d + hd * dh : d + (hd + 1) * dh]
            p = _softmax_rows(_dot(q[:, hd * dh : (hd + 1) * dh], kh, 1, 1) * scale)
            outs.append(_dot(p, vh))
        o_ref[...] = x_ + _dot(jnp.concatenate(outs, axis=-1), wo_ref[...])

    row = pl.BlockSpec((tm, d), lambda i: (i, 0))
    full = lambda shp: pl.BlockSpec(shp, lambda i: (0, 0))
    return pl.pallas_call(
        body,
        name=name,
        grid=(s // tm,),
        in_specs=[row, full((1, d)), full((d, d)), full((d, d)), full((ml, 2 * d))],
        out_specs=row,
        out_shape=jax.ShapeDtypeStruct((s, d), F32),
        compiler_params=_cparams(("arbitrary",)),
    )(x, g.reshape(1, d), wq, wo, kv)


def cross_bwd(x, g, wq, wo, kv, dy, *, name="cross_bwd"):
    s, d = x.shape
    ml = kv.shape[0]
    dh = d // X_HEADS
    tm = _pick(s, (ROW_TILE, 256, 128))
    scale = dh**-0.5

    def body(x_ref, g_ref, wq_ref, wo_ref, kv_ref, dy_ref, dx_ref, dg_ref, h_ref, dq_ref, o_ref, dkv_ref):
        first = pl.program_id(0) == 0
        x_, g_ = x_ref[...], g_ref[...]
        h = _rms_fwd(x_, g_).astype(BF16)
        h_ref[...] = h
        q = _dot(h, wq_ref[...])
        dy_ = dy_ref[...]
        do = _dot(dy_, wo_ref[...], 1, 1)
        outs, dqs, dks, dvs = [], [], [], []
        for hd in range(X_HEADS):
            sl = slice(hd * dh, (hd + 1) * dh)
            kh = kv_ref[:, sl]
            vh = kv_ref[:, d + hd * dh : d + (hd + 1) * dh]
            qh = q[:, sl]
            p = _softmax_rows(_dot(qh, kh, 1, 1) * scale)
            outs.append(_dot(p, vh))
            doh = do[:, sl]
            dp = _dot(doh, vh, 1, 1)
            ds = p * (dp - jnp.sum(dp * p, axis=-1, keepdims=True)) * scale
            dqs.append(_dot(ds, kh))
            dks.append(_dot(ds, qh, 0, 0))
            dvs.append(_dot(p, doh, 0, 0))
        o_ref[...] = jnp.concatenate(outs, axis=-1).astype(BF16)
        dq = jnp.concatenate(dqs, axis=-1).astype(BF16)
        dq_ref[...] = dq
        dx, dg = _rms_bwd(x_, g_, _dot(dq, wq_ref[...], 1, 1))
        dx_ref[...] = dy_ + dx
        _accum(dkv_ref, jnp.concatenate(dks + dvs, axis=-1), first)
        _accum(dg_ref, dg, first)

    row = pl.BlockSpec((tm, d), lambda i: (i, 0))
    full = lambda shp: pl.BlockSpec(shp, lambda i: (0, 0))
    rowb = jax.ShapeDtypeStruct((s, d), BF16)
    return pl.pallas_call(
        body,
        name=name,
        grid=(s // tm,),
        in_specs=[row, full((1, d)), full((d, d)), full((d, d)), full((ml, 2 * d)), row],
        out_specs=[row, full((1, d)), row, row, row, full((ml, 2 * d))],
        out_shape=[jax.ShapeDtypeStruct((s, d), F32), jax.ShapeDtypeStruct((1, d), F32), rowb, rowb, rowb,
                   jax.ShapeDtypeStruct((ml, 2 * d), F32)],
        compiler_params=_cparams(("arbitrary",)),
    )(x, g.reshape(1, d), wq, wo, kv, dy)


N_BRANCH = 3
MIX_W = 512
ATT_DH = 128
ATT_HG = 4
MERGE_TILE = 256


def _group_weights(l0, l1, l2):
    m = jnp.maximum(jnp.maximum(l0, l1), l2)
    e = [jnp.exp(l0 - m), jnp.exp(l1 - m), jnp.exp(l2 - m)]
    inv = 1.0 / (e[0] + e[1] + e[2])
    return [t * inv for t in e]


def _sigmoid(x):
    return 0.5 * (jnp.tanh(0.5 * x) + 1.0)


def merge_fwd(x, proj, ya, yb, outs, lses, wb, wout, *, name="merge_fwd"):
    s, d = x.shape
    tm = _pick(s, (MERGE_TILE, 128))

    def body(x_ref, g0, g1, g2, ya_ref, yb_ref, o0, o1, o2, l0, l1, l2, wb_ref, wo_ref, x1_ref, mg_ref, yc_ref):
        w = _group_weights(l0[...], l1[...], l2[...])
        o = (o0[...], o1[...], o2[...])
        heads = []
        for hd in range(ATT_HG):
            sl = slice(hd * ATT_DH, (hd + 1) * ATT_DH)
            heads.append(sum(w[gi][:, hd : hd + 1] * o[gi][:, sl] for gi in range(3)))
        yc = jnp.concatenate(heads, axis=-1)
        yc_ref[...] = yc.astype(BF16)
        merged = None
        for n, (y, gl) in enumerate(((ya_ref[...], g0), (yb_ref[...], g1), (yc, g2))):
            t = _sigmoid(gl[...].astype(F32)) * _dot(y, wb_ref[n])
            merged = t if merged is None else merged + t
        mb = merged.astype(BF16)
        mg_ref[...] = mb
        x1_ref[...] = x_ref[...] + _dot(mb, wo_ref[...])

    row = pl.BlockSpec((tm, d), lambda i: (i, 0))
    half = pl.BlockSpec((tm, MIX_W), lambda i: (i, 0))
    perhead = pl.BlockSpec((tm, ATT_HG), lambda i: (i, 0))
    gate = [pl.BlockSpec((tm, d), lambda i, n=n: (i, n)) for n in range(N_BRANCH)]
    return pl.pallas_call(
        body,
        name=name,
        grid=(s // tm,),
        in_specs=[row] + gate + [half] * 5 + [perhead] * 3
        + [pl.BlockSpec((N_BRANCH, MIX_W, d), lambda i: (0, 0, 0)), pl.BlockSpec((d, d), lambda i: (0, 0))],
        out_specs=[row, row, half],
        out_shape=[jax.ShapeDtypeStruct((s, d), F32), jax.ShapeDtypeStruct((s, d), BF16), jax.ShapeDtypeStruct((s, MIX_W), BF16)],
        compiler_params=_cparams(("arbitrary",)),
    )(x, proj, proj, proj, ya, yb, *outs, *lses, wb, wout)


def merge_bwd(dx1, proj, ya, yb, yc, outs, lses, wb, wout, *, name="merge_bwd"):
    s, d = dx1.shape
    tm = _pick(s, (MERGE_TILE, 128))

    def body(dx_ref, g0, g1, g2, ya_ref, yb_ref, yc_ref, o0, o1, o2, l0, l1, l2, wb_ref, wo_ref,
             dgl_ref, dz0, dz1, dz2, dya_ref, dyb_ref, do0, do1, do2, dl0, dl1, dl2):
        dm = _dot(dx_ref[...], wo_ref[...], 1, 1)
        dys = []
        for n, (y, gl, dz_ref) in enumerate(((ya_ref, g0, dz0), (yb_ref, g1, dz1), (yc_ref, g2, dz2))):
            z = _dot(y[...], wb_ref[n])
            sg = _sigmoid(gl[...].astype(F32))
            dz = (dm * sg).astype(BF16)
            dz_ref[...] = dz
            dgl_ref[:, n * d : (n + 1) * d] = (dm * z * sg * (1.0 - sg)).astype(BF16)
            dys.append(_dot(dz, wb_ref[n], 1, 1))
        dya_ref[...] = dys[0]
        dyb_ref[...] = dys[1]
        dyc = dys[2]
        w = _group_weights(l0[...], l1[...], l2[...])
        o = [o0[...], o1[...], o2[...]]
        for hd in range(ATT_HG):
            sl = slice(hd * ATT_DH, (hd + 1) * ATT_DH)
            col = slice(hd, hd + 1)
            wh = [w[gi][:, col] for gi in range(3)]
            for gi, r in enumerate((do0, do1, do2)):
                r[:, sl] = wh[gi] * dyc[:, sl]
            t = [jnp.sum(dyc[:, sl] * o[gi][:, sl], axis=-1, keepdims=True) for gi in range(3)]
            tbar = wh[0] * t[0] + wh[1] * t[1] + wh[2] * t[2]
            for gi, r in enumerate((dl0, dl1, dl2)):
                r[:, col] = wh[gi] * (t[gi] - tbar)

    row = pl.BlockSpec((tm, d), lambda i: (i, 0))
    half = pl.BlockSpec((tm, MIX_W), lambda i: (i, 0))
    perhead = pl.BlockSpec((tm, ATT_HG), lambda i: (i, 0))
    gate = [pl.BlockSpec((tm, d), lambda i, n=n: (i, n)) for n in range(N_BRANCH)]
    rb = jax.ShapeDtypeStruct((s, d), BF16)
    hf = jax.ShapeDtypeStruct((s, MIX_W), F32)
    ph = jax.ShapeDtypeStruct((s, ATT_HG), F32)
    return pl.pallas_call(
        body,
        name=name,
        grid=(s // tm,),
        in_specs=[row] + gate + [half] * 6 + [perhead] * 3
        + [pl.BlockSpec((N_BRANCH, MIX_W, d), lambda i: (0, 0, 0)), pl.BlockSpec((d, d), lambda i: (0, 0))],
        out_specs=[pl.BlockSpec((tm, N_BRANCH * d), lambda i: (i, 0)), row, row, row] + [half] * 5 + [perhead] * 3,
        out_shape=[jax.ShapeDtypeStruct((s, N_BRANCH * d), BF16), rb, rb, rb] + [hf] * 5 + [ph] * 3,
        compiler_params=_cparams(("arbitrary",)),
    )(dx1, proj, proj, proj, ya, yb, yc, *outs, *lses, wb, wout)


ATT_BLOCK = 128
ATT_STEP = 4 * ATT_BLOCK
ATT_GROUPS = ((128, 1), (512, 4), (2048, 16))
N_ATT_HEADS = ATT_HG * len(ATT_GROUPS)
ALIBI_MAX_EXP = 8.0
MASKED = -1e30


def _att_slopes(group):
    return [2.0 ** (-ALIBI_MAX_EXP * (group * ATT_HG + h + 1) / N_ATT_HEADS) for h in range(ATT_HG)]


def _att_scores(q, kk, slope_dil, has_prev):
    scale = ATT_DH**-0.5
    qi = lax.broadcasted_iota(jnp.int32, (ATT_BLOCK, 2 * ATT_BLOCK), 0)
    kj = lax.broadcasted_iota(jnp.int32, (ATT_BLOCK, 2 * ATT_BLOCK), 1)
    dist = qi + ATT_BLOCK - kj
    valid = jnp.logical_and(jnp.logical_and(dist >= 0, dist <= ATT_BLOCK), jnp.logical_or(kj >= ATT_BLOCK, has_prev))
    return jnp.where(valid, _dot(q, kk, 1, 1) * scale - slope_dil * dist.astype(F32), MASKED)


def _att_window(kp_ref, kc_ref, buf):
    buf[:ATT_BLOCK, :] = kp_ref[...]
    buf[ATT_BLOCK:, :] = kc_ref[...]


def attn_fwd(qkv, col, group, *, name="attn_fwd"):
    s = qkv.shape[0]
    window, dil = ATT_GROUPS[group]
    assert window // dil == ATT_BLOCK and s % (dil * ATT_BLOCK) == 0 and col % MIX_W == 0
    nb = s // dil // ATT_BLOCK
    nq = ATT_STEP // ATT_BLOCK
    ng = s // ATT_STEP
    slopes = _att_slopes(group)
    c0 = col // MIX_W

    def body(q_ref, kp_ref, kc_ref, vp_ref, vc_ref, o_ref, l_ref, kbuf, vbuf):
        n = pl.program_id(0)
        _att_window(kp_ref, kc_ref, kbuf)
        _att_window(vp_ref, vc_ref, vbuf)
        for i in range(nq):
            r = slice(i * ATT_BLOCK, (i + 1) * ATT_BLOCK)
            win = slice(i * ATT_BLOCK, (i + 2) * ATT_BLOCK)
            has_prev = ((n * nq + i) % nb) != 0
            for hd in range(ATT_HG):
                sl = slice(hd * ATT_DH, (hd + 1) * ATT_DH)
                sc = _att_scores(q_ref[r, sl], kbuf[win, sl], slopes[hd] * dil, has_prev)
                m = jnp.max(sc, axis=-1, keepdims=True)
                e = jnp.exp(sc - m)
                den = jnp.sum(e, axis=-1, keepdims=True)
                o_ref[r, sl] = _dot(e * (1.0 / den), vbuf[win, sl])
                l_ref[r, hd : hd + 1] = m + jnp.log(den)

    cur = lambda c: pl.BlockSpec((ATT_STEP, MIX_W), lambda n, c=c: (n, c))
    prev = lambda c: pl.BlockSpec((ATT_BLOCK, MIX_W), lambda n, c=c: (jnp.maximum(n * nq - 1, 0), c))
    blk = pl.BlockSpec((ATT_STEP, MIX_W), lambda n: (n, 0))
    return pl.pallas_call(
        body,
        name=name,
        grid=(ng,),
        in_specs=[cur(c0), prev(c0 + 1), cur(c0 + 1), prev(c0 + 2), cur(c0 + 2)],
        out_specs=[blk, pl.BlockSpec((ATT_STEP, ATT_HG), lambda n: (n, 0))],
        out_shape=[jax.ShapeDtypeStruct((s, MIX_W), F32), jax.ShapeDtypeStruct((s, ATT_HG), F32)],
        scratch_shapes=[pltpu.VMEM((ATT_BLOCK + ATT_STEP, MIX_W), qkv.dtype)] * 2,
        compiler_params=_cparams(("arbitrary",)),
    )(qkv, qkv, qkv, qkv, qkv)


def attn_bwd(qkv, col, group, out, lse, dout, dlse, *, name="attn_bwd"):
    s = qkv.shape[0]
    window, dil = ATT_GROUPS[group]
    nb = s // dil // ATT_BLOCK
    nq = ATT_STEP // ATT_BLOCK
    ng = s // ATT_STEP
    slopes = _att_slopes(group)
    c0 = col // MIX_W
    scale = ATT_DH**-0.5
    tail = slice((nq - 1) * ATT_BLOCK, nq * ATT_BLOCK)

    def body(q_ref, kp_ref, kc_ref, vp_ref, vc_ref, o_ref, l_ref, do_ref, dl_ref, dq_ref, dk_ref, dv_ref, acck, accv,
             kbuf, vbuf):
        n = pl.program_id(0)
        live = n < ng

        @pl.when(n == 0)
        def _():
            acck[...] = jnp.zeros_like(acck)
            accv[...] = jnp.zeros_like(accv)

        _att_window(kp_ref, kc_ref, kbuf)
        _att_window(vp_ref, vc_ref, vbuf)
        dk_ref[: (nq - 1) * ATT_BLOCK, :] = acck[: (nq - 1) * ATT_BLOCK, :].astype(BF16)
        dv_ref[: (nq - 1) * ATT_BLOCK, :] = accv[: (nq - 1) * ATT_BLOCK, :].astype(BF16)
        for i in range(nq):
            r = slice(i * ATT_BLOCK, (i + 1) * ATT_BLOCK)
            rp = slice((i - 1) * ATT_BLOCK, i * ATT_BLOCK)
            win = slice(i * ATT_BLOCK, (i + 2) * ATT_BLOCK)
            has_prev = ((jnp.minimum(n, ng - 1) * nq + i) % nb) != 0
            for hd in range(ATT_HG):
                sl = slice(hd * ATT_DH, (hd + 1) * ATT_DH)
                q, do, kk, vv = q_ref[r, sl], do_ref[r, sl], kbuf[win, sl], vbuf[win, sl]
                p = jnp.exp(_att_scores(q, kk, slopes[hd] * dil, has_prev) - l_ref[r, hd : hd + 1])
                corr = dl_ref[r, hd : hd + 1] - jnp.sum(do * o_ref[r, sl], axis=-1, keepdims=True)
                ds = p * (_dot(do, vv, 1, 1) + corr) * scale
                dq_ref[r, sl] = _dot(ds, kk).astype(BF16)
                dkk = jnp.where(live, _dot(ds, q, 0, 0), 0.0)
                dvv = jnp.where(live, _dot(p, do, 0, 0), 0.0)
                if i == 0:
                    dk_ref[tail, sl] = (acck[tail, sl] + dkk[:ATT_BLOCK]).astype(BF16)
                    dv_ref[tail, sl] = (accv[tail, sl] + dvv[:ATT_BLOCK]).astype(BF16)
                else:
                    acck[rp, sl] += dkk[:ATT_BLOCK]
                    accv[rp, sl] += dvv[:ATT_BLOCK]
                acck[r, sl] = dkk[ATT_BLOCK:]
                accv[r, sl] = dvv[ATT_BLOCK:]

    last = ng - 1
    cur = lambda c: pl.BlockSpec((ATT_STEP, MIX_W), lambda n, c=c: (jnp.minimum(n, last), c))
    prev = lambda c: pl.BlockSpec((ATT_BLOCK, MIX_W), lambda n, c=c: (jnp.maximum(jnp.minimum(n, last) * nq - 1, 0), c))
    behind = pl.BlockSpec((ATT_STEP, MIX_W), lambda n: (jnp.clip(n - 1, 0, last), 0))
    perhead = pl.BlockSpec((ATT_STEP, ATT_HG), lambda n: (jnp.minimum(n, last), 0))
    sds = jax.ShapeDtypeStruct((s, MIX_W), BF16)
    return pl.pallas_call(
        body,
        name=name,
        grid=(ng + 1,),
        in_specs=[cur(c0), prev(c0 + 1), cur(c0 + 1), prev(c0 + 2), cur(c0 + 2), cur(0), perhead, cur(0), perhead],
        out_specs=[cur(0), behind, behind],
        out_shape=[sds, sds, sds],
        scratch_shapes=[pltpu.VMEM((ATT_STEP, MIX_W), F32)] * 2 + [pltpu.VMEM((ATT_BLOCK + ATT_STEP, MIX_W), qkv.dtype)] * 2,
        compiler_params=_cparams(("arbitrary",)),
    )(qkv, qkv, qkv, qkv, qkv, out, lse, dout, dlse)


GLA_HEADS = 4
GLA_DK = 64
GLA_DV = 128
GLA_CHUNK = 64
GLA_TAU = 16.0
GLA_QK = GLA_HEADS * GLA_DK
LR_PAD = 128


def _dot_exact(a, b, ca=1, cb=0, precision=lax.Precision.HIGHEST):
    return lax.dot_general(a, b, (((ca,), (cb,)), ((), ())), precision=precision, preferred_element_type=F32)


GLA_STEP = 4 * GLA_CHUNK


@jax.custom_vjp
def _score_dot(a, b):
    return _dot_exact(a, b, 1, 1)


def _score_dot_fwd(a, b):
    return _dot_exact(a, b, 1, 1), (a, b)


def _score_dot_bwd(res, g):
    a, b = res
    return _dot(g, b), _dot(g, a, 0, 0)


_score_dot.defvjp(_score_dot_fwd, _score_dot_bwd)


def _gla_group(q, k, v, lr, r, state, wg, bg, go):
    c, n = GLA_CHUNK, q.shape[0]
    z = _dot(lr, wg) + bg
    la = (jnp.minimum(z, 0.0) - jnp.log(1.0 + jnp.exp(-jnp.abs(z)))) * (1.0 / GLA_TAU)
    ri = lax.broadcasted_iota(jnp.int32, (n, n), 0)
    ci = lax.broadcasted_iota(jnp.int32, (n, n), 1)
    shift = c.bit_length() - 1
    in_chunk = jnp.logical_and(ri >= ci, jnp.right_shift(ri, shift) == jnp.right_shift(ci, shift))
    ball = _dot_exact(in_chunk.astype(F32), la, precision=lax.Precision.HIGH)
    ri = lax.broadcasted_iota(jnp.int32, (c, c), 0)
    ci = lax.broadcasted_iota(jnp.int32, (c, c), 1)
    causal = ri >= ci
    qs = q * GLA_DK**-0.5
    ys = []
    for j in range(n // c):
        rows = slice(j * c, (j + 1) * c)
        b = ball[rows]
        bmid = b[c // 2 : c // 2 + 1, :]
        bend = b[c - 1 : c, :]
        q_in = qs[rows] * jnp.exp(b)
        q_mid = qs[rows] * jnp.exp(b - bmid)
        k_mid = k[rows] * jnp.exp(bmid - b)
        k_end = k[rows] * jnp.exp(bend - b)
        yh, upd = [], []
        for h in range(GLA_HEADS):
            sk = slice(h * GLA_DK, (h + 1) * GLA_DK)
            sv = slice(h * GLA_DV, (h + 1) * GLA_DV)
            vh = v[rows, sv]
            inter = _dot(q_in[:, sk], state[:, sk], 1, 1)
            sc = jnp.where(causal, _score_dot(q_mid[:, sk], k_mid[:, sk]), 0.0)
            o = inter + _dot(sc, vh)
            o = o * lax.rsqrt(jnp.mean(o * o, axis=-1, keepdims=True) + RMS_EPS) * go
            rh = r[rows, sv]
            yh.append(o * rh * _sigmoid(rh))
            upd.append(_dot(vh, k_end[:, sk], 0, 0))
        state = jnp.exp(bend) * state + jnp.concatenate(upd, axis=-1)
        ys.append(jnp.concatenate(yh, axis=-1))
    return jnp.concatenate(ys, axis=0), state


def _gla_in_specs(cols, rev, nc):
    c = GLA_STEP
    row = (lambda i: nc - 1 - i) if rev else (lambda i: i)
    qc, kc, vc, lc, rc = cols
    assert qc % GLA_QK == 0 and kc % GLA_QK == 0 and vc % MIX_W == 0 and rc % MIX_W == 0 and lc % LR_PAD == 0
    return [
        pl.BlockSpec((c, GLA_QK), lambda i: (row(i), qc // GLA_QK)),
        pl.BlockSpec((c, GLA_QK), lambda i: (row(i), kc // GLA_QK)),
        pl.BlockSpec((c, MIX_W), lambda i: (row(i), vc // MIX_W)),
        pl.BlockSpec((c, LR_PAD), lambda i: (row(i), lc // LR_PAD)),
        pl.BlockSpec((c, MIX_W), lambda i: (row(i), rc // MIX_W)),
    ], row


def gla_fwd(proj, cols, wg, bg, go, *, name="gla_fwd"):
    s = proj.shape[0]
    nc = s // GLA_STEP
    specs, row = _gla_in_specs(cols, False, nc)

    def body(q_ref, k_ref, v_ref, lr_ref, r_ref, wg_ref, bg_ref, go_ref, y_ref, st_ref, state):
        @pl.when(pl.program_id(0) == 0)
        def _():
            state[...] = jnp.zeros_like(state)

        st = state[...]
        st_ref[...] = st
        y, new = _gla_group(q_ref[...], k_ref[...], v_ref[...], lr_ref[...], r_ref[...], st, wg_ref[...], bg_ref[...], go_ref[...])
        y_ref[...] = y
        state[...] = new

    full = lambda shp: pl.BlockSpec(shp, lambda i: (0, 0))
    return pl.pallas_call(
        body,
        name=name,
        grid=(nc,),
        in_specs=specs + [full((LR_PAD, GLA_QK)), full((1, GLA_QK)), full((1, GLA_DV))],
        out_specs=[pl.BlockSpec((GLA_STEP, MIX_W), lambda i: (i, 0)), pl.BlockSpec((GLA_DV, GLA_QK), lambda i: (i, 0))],
        out_shape=[jax.ShapeDtypeStruct((s, MIX_W), F32), jax.ShapeDtypeStruct((nc * GLA_DV, GLA_QK), F32)],
        scratch_shapes=[pltpu.VMEM((GLA_DV, GLA_QK), F32)],
        compiler_params=_cparams(("arbitrary",)),
    )(proj, proj, proj, proj, proj, wg, bg, go)


def gla_bwd(proj, cols, wg, bg, go, states, dy, *, name="gla_bwd"):
    s = proj.shape[0]
    nc = s // GLA_STEP
    specs, row = _gla_in_specs(cols, True, nc)

    def body(q_ref, k_ref, v_ref, lr_ref, r_ref, wg_ref, bg_ref, go_ref, st_ref, dy_ref,
             dq_ref, dk_ref, dv_ref, dlr_ref, dr_ref, dwg_ref, dbg_ref, dgo_ref, dstate):
        first = pl.program_id(0) == 0

        @pl.when(first)
        def _():
            dstate[...] = jnp.zeros_like(dstate)

        _, vjp = jax.vjp(_gla_group, q_ref[...], k_ref[...], v_ref[...], lr_ref[...], r_ref[...], st_ref[...],
                         wg_ref[...].astype(F32), bg_ref[...], go_ref[...])
        dq, dk, dv, dlr, dr, dst, dwg, dbg, dgo = vjp((dy_ref[...], dstate[...]))
        dq_ref[...] = dq.astype(BF16)
        dk_ref[...] = dk.astype(BF16)
        dv_ref[...] = dv.astype(BF16)
        dlr_ref[...] = dlr.astype(BF16)
        dr_ref[...] = dr.astype(BF16)
        dstate[...] = dst
        _accum(dwg_ref, dwg, first)
        _accum(dbg_ref, dbg, first)
        _accum(dgo_ref, dgo, first)

    c = GLA_STEP
    full = lambda shp: pl.BlockSpec(shp, lambda i: (0, 0))
    rows = lambda w: pl.BlockSpec((c, w), lambda i: (row(i), 0))
    return pl.pallas_call(
        body,
        name=name,
        grid=(nc,),
        in_specs=specs + [full((LR_PAD, GLA_QK)), full((1, GLA_QK)), full((1, GLA_DV)),
                          pl.BlockSpec((GLA_DV, GLA_QK), lambda i: (row(i), 0)), rows(MIX_W)],
        out_specs=[rows(GLA_QK), rows(GLA_QK), rows(MIX_W), rows(LR_PAD), rows(MIX_W),
                   full((LR_PAD, GLA_QK)), full((1, GLA_QK)), full((1, GLA_DV))],
        out_shape=[jax.ShapeDtypeStruct((s, GLA_QK), BF16), jax.ShapeDtypeStruct((s, GLA_QK), BF16),
                   jax.ShapeDtypeStruct((s, MIX_W), BF16), jax.ShapeDtypeStruct((s, LR_PAD), BF16),
                   jax.ShapeDtypeStruct((s, MIX_W), BF16), jax.ShapeDtypeStruct((LR_PAD, GLA_QK), F32),
                   jax.ShapeDtypeStruct((1, GLA_QK), F32), jax.ShapeDtypeStruct((1, GLA_DV), F32)],
        scratch_shapes=[pltpu.VMEM((GLA_DV, GLA_QK), F32)],
        compiler_params=_cparams(("arbitrary",)),
    )(proj, proj, proj, proj, proj, wg, bg, go, states, dy)


S5_G = 32
S5_P = 64
S5_C = 16
S5_N = S5_G * S5_P
S5_TC = 256
SUB = 8


def _s5_disc(a_re, a_im, ls, b_re, b_im):
    step = jnp.exp(ls)
    mag = jnp.exp(a_re * step)
    lr, li = mag * jnp.cos(a_im * step), mag * jnp.sin(a_im * step)
    inv = 1.0 / (a_re * a_re + a_im * a_im)
    nr, ni = lr - 1.0, li
    cr, ci = (nr * a_re + ni * a_im) * inv, (ni * a_re - nr * a_im) * inv
    return lr, li, cr * b_re - ci * b_im, cr * b_im + ci * b_re


def s5_disc_fwd(a_re, a_im, ls, b_re, b_im, *, name="s5_disc"):
    def body(ar, ai, l, br, bi, o0, o1, o2, o3):
        for o, val in zip((o0, o1, o2, o3), _s5_disc(ar[...], ai[...], l[...], br[...], bi[...])):
            o[...] = val

    sds = jax.ShapeDtypeStruct(a_re.shape, F32)
    return pl.pallas_call(body, name=name, out_shape=[sds] * 4)(a_re, a_im, ls, b_re, b_im)


def s5_disc_bwd(a_re, a_im, ls, b_re, b_im, cts, *, name="s5_disc_bwd"):
    def body(ar, ai, l, br, bi, c0, c1, c2, c3, dar, dai, dl, dbr, dbi):
        _, vjp = jax.vjp(_s5_disc, ar[...], ai[...], l[...], br[...], bi[...])
        g = vjp((c0[...], c1[...], c2[...], c3[...]))
        for o, val in zip((dar, dai, dl), g[:3]):
            o[...] = jnp.sum(val, axis=-1, keepdims=True)
        dbr[...] = g[3]
        dbi[...] = g[4]

    col = jax.ShapeDtypeStruct((a_re.shape[0], 1), F32)
    sds = jax.ShapeDtypeStruct(a_re.shape, F32)
    return pl.pallas_call(body, name=name, out_shape=[col, col, col, sds, sds])(a_re, a_im, ls, b_re, b_im, *cts)


def _gelu(y):
    return 0.5 * y * (1.0 + jnp.tanh(0.7978845608028654 * (y + 0.044715 * (y * y * y))))


def _s5_powers(lam, conj):
    lr, li = lam[:, :S5_N], lam[:, S5_N:]
    if conj:
        li = -li
    rows, pr, pi = [], lr, li
    for _ in range(SUB):
        rows.append((pr, pi))
        pr, pi = pr * lr - pi * li, pr * li + pi * lr
    return rows


def _s5_table(rows, reverse):
    ridx = lax.broadcasted_iota(jnp.int32, (SUB, S5_N), 0)
    tr = jnp.zeros((SUB, S5_N), F32)
    ti = jnp.zeros((SUB, S5_N), F32)
    for i in range(SUB):
        pr, pi = rows[SUB - 1 - i] if reverse else rows[i]
        tr = jnp.where(ridx == i, pr, tr)
        ti = jnp.where(ridx == i, pi, ti)
    return tr, ti


def _s5_scan(buf, lam, carry_ref, reverse):
    tc = buf.shape[0]
    nblk = tc // SUB
    rows = _s5_powers(lam, reverse)
    tr, ti = _s5_table(rows, reverse)
    ridx = lax.broadcasted_iota(jnp.int32, (SUB, S5_N), 0)
    steps = []
    for sft, (pr, pi) in ((1, rows[0]), (2, rows[1]), (4, rows[3])):
        keep = (ridx < SUB - sft) if reverse else (ridx >= sft)
        steps.append((SUB - sft if reverse else sft, jnp.where(keep, pr, 0.0), jnp.where(keep, pi, 0.0)))

    def block(j, carry):
        jj = (nblk - 1 - j) if reverse else j
        at = pl.ds(pl.multiple_of(jj * SUB, SUB), SUB)
        re, im = buf[at, :S5_N], buf[at, S5_N:]
        for rot, pr, pi in steps:
            sre, sim = pltpu.roll(re, rot, 0), pltpu.roll(im, rot, 0)
            re, im = re + pr * sre - pi * sim, im + pr * sim + pi * sre
        cr, ci = carry
        re, im = re + tr * cr - ti * ci, im + tr * ci + ti * cr
        buf[at, :S5_N] = re
        buf[at, S5_N:] = im
        edge = 0 if reverse else SUB - 1
        return re[edge : edge + 1, :], im[edge : edge + 1, :]

    c0 = (carry_ref[0:1, :S5_N], carry_ref[0:1, S5_N:])
    cr, ci = lax.fori_loop(0, nblk, block, c0)
    carry_ref[:, :S5_N] = jnp.broadcast_to(cr, (SUB, S5_N))
    carry_ref[:, S5_N:] = jnp.broadcast_to(ci, (SUB, S5_N))


def s5_fwd(proj, ucol, bd, cd, lam, dskip, wglu, bglu, *, name="s5_fwd"):
    s = proj.shape[0]
    tc = _pick(s, (S5_TC, 128, 64, 8))
    assert ucol % MIX_W == 0

    def body(u_ref, bd_ref, cd_ref, lam_ref, d_ref, w_ref, b_ref, y_ref, xs_ref, carry):
        @pl.when(pl.program_id(0) == 0)
        def _():
            carry[...] = jnp.zeros_like(carry)

        u = u_ref[...]
        xs_ref[...] = _dot(u, bd_ref[...])
        _s5_scan(xs_ref, lam_ref[...], carry, False)
        g = _gelu(_dot(xs_ref[...], cd_ref[...]) + d_ref[...] * u)
        y_ref[...] = g * _sigmoid(_dot(g, w_ref[...]) + b_ref[...])

    full = lambda shp: pl.BlockSpec(shp, lambda i: (0, 0))
    return pl.pallas_call(
        body,
        name=name,
        grid=(s // tc,),
        in_specs=[pl.BlockSpec((tc, MIX_W), lambda i: (i, ucol // MIX_W)), full((MIX_W, 2 * S5_N)), full((2 * S5_N, MIX_W)),
                  full((1, 2 * S5_N)), full((1, MIX_W)), full((MIX_W, MIX_W)), full((1, MIX_W))],
        out_specs=[pl.BlockSpec((tc, MIX_W), lambda i: (i, 0)), pl.BlockSpec((tc, 2 * S5_N), lambda i: (i, 0))],
        out_shape=[jax.ShapeDtypeStruct((s, MIX_W), F32), jax.ShapeDtypeStruct((s, 2 * S5_N), F32)],
        scratch_shapes=[pltpu.VMEM((SUB, 2 * S5_N), F32)],
        compiler_params=_cparams(("arbitrary",)),
    )(proj, bd, cd, lam, dskip, wglu, bglu)


def s5_bwd(proj, ucol, xs, bd, cd, lam, dskip, wglu, bglu, dya, *, name="s5_bwd"):
    s = proj.shape[0]
    tc = _pick(s, (S5_TC, 128, 64, 8))
    nch = s // tc
    per = tc // SUB

    def body(u_ref, xs_ref, xp_ref, bd_ref, cd_ref, lam_ref, d_ref, w_ref, b_ref, dya_ref,
             du_ref, adj_ref, g_ref, dpre_ref, dy_ref, dlam_ref, dd_ref, db_ref, buf, carry, lacc):
        i = pl.program_id(0)
        first = i == 0

        @pl.when(first)
        def _():
            carry[...] = jnp.zeros_like(carry)
            lacc[...] = jnp.zeros_like(lacc)

        u, x, dya_ = u_ref[...], xs_ref[...], dya_ref[...]
        y = _dot(x, cd_ref[...]) + d_ref[...] * u
        g, gelu_vjp = jax.vjp(_gelu, y)
        sg = _sigmoid(_dot(g, w_ref[...]) + b_ref[...])
        dpre = dya_ * g * sg * (1.0 - sg)
        (dy,) = gelu_vjp(dya_ * sg + _dot(dpre, w_ref[...], 1, 1))
        g_ref[...] = g.astype(BF16)
        dpre_ref[...] = dpre.astype(BF16)
        dy_ref[...] = dy.astype(BF16)
        db_part = jnp.sum(dpre, axis=0, keepdims=True)
        dd_part = jnp.sum(dy * u, axis=0, keepdims=True)
        buf[...] = _dot(dy, cd_ref[...], 1, 1)
        _s5_scan(buf, lam_ref[...], carry, True)
        a = buf[...]
        adj_ref[...] = a.astype(BF16)
        du_ref[...] = (dy * d_ref[...] + _dot(a, bd_ref[...], 1, 1)).astype(BF16)
        before = jnp.where(i == nch - 1, 0.0, xp_ref[SUB - 1 : SUB, :])
        ridx = lax.broadcasted_iota(jnp.int32, (tc, 2 * S5_N), 0)
        xprev = jnp.where(ridx == 0, before, pltpu.roll(x, 1, 0))
        ar, ai, xr, xi = a[:, :S5_N], a[:, S5_N:], xprev[:, :S5_N], xprev[:, S5_N:]
        lacc[:, :S5_N] += jnp.sum((ar * xr + ai * xi).reshape(per, SUB, S5_N), axis=0)
        lacc[:, S5_N:] += jnp.sum((ai * xr - ar * xi).reshape(per, SUB, S5_N), axis=0)
        _accum(db_ref, db_part, first)
        _accum(dd_ref, dd_part, first)

        @pl.when(i == nch - 1)
        def _():
            dlam_ref[...] = jnp.sum(lacc[...], axis=0, keepdims=True)

    rev = lambda i: nch - 1 - i
    full = lambda shp: pl.BlockSpec(shp, lambda i: (0, 0))
    rows = lambda w: pl.BlockSpec((tc, w), lambda i: (rev(i), 0))
    hb = jax.ShapeDtypeStruct((s, MIX_W), BF16)
    return pl.pallas_call(
        body,
        name=name,
        grid=(nch,),
        in_specs=[pl.BlockSpec((tc, MIX_W), lambda i: (rev(i), ucol // MIX_W)), rows(2 * S5_N),
                  pl.BlockSpec((SUB, 2 * S5_N), lambda i: (jnp.maximum(rev(i) * per - 1, 0), 0)),
                  full((MIX_W, 2 * S5_N)), full((2 * S5_N, MIX_W)), full((1, 2 * S5_N)), full((1, MIX_W)),
                  full((MIX_W, MIX_W)), full((1, MIX_W)), rows(MIX_W)],
        out_specs=[rows(MIX_W), rows(2 * S5_N), rows(MIX_W), rows(MIX_W), rows(MIX_W),
                   full((1, 2 * S5_N)), full((1, MIX_W)), full((1, MIX_W))],
        out_shape=[hb, jax.ShapeDtypeStruct((s, 2 * S5_N), BF16), hb, hb, hb,
                   jax.ShapeDtypeStruct((1, 2 * S5_N), F32), jax.ShapeDtypeStruct((1, MIX_W), F32),
                   jax.ShapeDtypeStruct((1, MIX_W), F32)],
        scratch_shapes=[pltpu.VMEM((tc, 2 * S5_N), F32), pltpu.VMEM((SUB, 2 * S5_N), F32), pltpu.VMEM((SUB, 2 * S5_N), F32)],
        compiler_params=_cparams(("arbitrary",)),
    )(proj, xs, xs, bd, cd, lam, dskip, wglu, bglu, dya)


def _bcast16(a):
    return jnp.broadcast_to(a.reshape(S5_N, 1), (S5_N, S5_C))


def _blockdiag(blocks):
    g, r, c = blocks.shape
    eye = jnp.eye(g, dtype=blocks.dtype)
    return (eye[:, None, :, None] * blocks[:, :, None, :]).reshape(g * r, g * c)


def _blockdiag_extract(dense, r, c):
    g = dense.shape[0] // r
    return jnp.einsum("grgc->grc", dense.reshape(g, r, g, c))


def s5_prepare(a_re, a_im, log_step, b_re, b_im, c_re, c_im):
    disc_in = (_bcast16(a_re), _bcast16(a_im), _bcast16(jnp.broadcast_to(log_step[:, None], (S5_G, S5_P))),
               b_re.reshape(S5_N, S5_C), b_im.reshape(S5_N, S5_C))
    lr, li, bbr, bbi = s5_disc_fwd(*disc_in)
    lam = jnp.concatenate([lr[:, 0], li[:, 0]]).reshape(1, 2 * S5_N)
    to_blocks = lambda t: _blockdiag(t.reshape(S5_G, S5_P, S5_C).transpose(0, 2, 1))
    bd = jnp.concatenate([to_blocks(bbr), to_blocks(bbi)], axis=1).astype(BF16)
    cd = jnp.concatenate([_blockdiag(c_re.transpose(0, 2, 1)), -_blockdiag(c_im.transpose(0, 2, 1))], axis=0).astype(BF16)
    return disc_in, lam, bd, cd


def s5_param_grads(disc_in, dlam, dbd, dcd):
    first_col = lambda v: jnp.pad(v.reshape(S5_N, 1), ((0, 0), (0, S5_C - 1)))
    from_blocks = lambda t: _blockdiag_extract(t, S5_C, S5_P).transpose(0, 2, 1).reshape(S5_N, S5_C)
    cts = (first_col(dlam[0, :S5_N]), first_col(dlam[0, S5_N:]), from_blocks(dbd[:, :S5_N]), from_blocks(dbd[:, S5_N:]))
    dar, dai, dls, dbr, dbi = s5_disc_bwd(*disc_in, cts)
    dcr = _blockdiag_extract(dcd[:S5_N], S5_P, S5_C).transpose(0, 2, 1)
    dci = -_blockdiag_extract(dcd[S5_N:], S5_P, S5_C).transpose(0, 2, 1)
    return (dar.reshape(S5_G, S5_P), dai.reshape(S5_G, S5_P), dls.reshape(S5_G, S5_P).sum(axis=1),
            dbr.reshape(S5_G, S5_P, S5_C), dbi.reshape(S5_G, S5_P, S5_C), dcr, dci)


N_DEV = 8
MESH_ID = pl.DeviceIdType.MESH
ANY = pl.BlockSpec(memory_space=pl.ANY)


def _me():
    return lax.axis_index("x"), lax.axis_index("y"), lax.axis_index("c")


def all_gather(blocks, *, by_core=False, name="all_gather"):
    na = len(blocks)
    shapes = [b.shape[1:] if by_core else b.shape for b in blocks]

    def body(*refs):
        ins, outs = refs[:na], refs[na : 2 * na]
        send_sems, recv_sems = refs[2 * na :]
        x, y, c = _me()
        me, sibling = (x, y, c), (x, y, 1 - c)
        xn, yn, diag = (1 - x, y, c), (x, 1 - y, c), (1 - x, 1 - y, c)
        slot = lambda p: 4 * p[0] + 2 * p[1] + p[2]

        def copy(a, k, block, to, half=None, src=None):
            dst = outs[a].at[slot(block)]
            if half is not None:
                rows = shapes[a][0] // 2
                dst = dst.at[pl.ds(half * rows, rows)]
            return pltpu.make_async_remote_copy(src_ref=dst if src is None else src, dst_ref=dst, send_sem=send_sems.at[a * 9 + k],
                                                recv_sem=recv_sems.at[a * 9 + k], device_id=to, device_id_type=MESH_ID)

        mine = [ins[a].at[c] if by_core else ins[a] for a in range(na)]
        sends = []

        def go(cp):
            cp.start()
            sends.append(cp)

        for a in range(na):
            go(copy(a, 0, me, sibling, src=mine[a]))
            go(copy(a, 1, me, xn, src=mine[a]))
            go(copy(a, 2, me, yn, src=mine[a]))
        for a in range(na):
            copy(a, 1, xn, me).wait_recv()
            go(copy(a, 3, xn, yn, half=0))
            go(copy(a, 5, xn, sibling))
            copy(a, 2, yn, me).wait_recv()
            go(copy(a, 4, yn, xn, half=1))
            go(copy(a, 6, yn, sibling))
        for a in range(na):
            copy(a, 3, diag, me, half=0).wait_recv()
            go(copy(a, 7, diag, sibling, half=0))
            copy(a, 4, diag, me, half=1).wait_recv()
            go(copy(a, 8, diag, sibling, half=1))
        for a in range(na):
            copy(a, 0, sibling, me).wait_recv()
            copy(a, 5, (1 - x, y, 1 - c), me).wait_recv()
            copy(a, 6, (x, 1 - y, 1 - c), me).wait_recv()
            copy(a, 7, (1 - x, 1 - y, 1 - c), me, half=0).wait_recv()
            copy(a, 8, (1 - x, 1 - y, 1 - c), me, half=1).wait_recv()
        for cp in sends:
            cp.wait_send()

    assert all(shp[0] % (64 // b.dtype.itemsize) == 0 for shp, b in zip(shapes, blocks)), shapes
    outs = pl.pallas_call(
        body,
        name=name,
        in_specs=[ANY] * na,
        out_specs=[ANY] * na,
        out_shape=[jax.ShapeDtypeStruct((N_DEV, *shp), b.dtype) for shp, b in zip(shapes, blocks)],
        scratch_shapes=[pltpu.SemaphoreType.DMA((na * 9,)), pltpu.SemaphoreType.DMA((na * 9,))],
    )(*blocks)
    x, y, c = _me()
    own = [lax.dynamic_index_in_dim(b, c, 0, keepdims=False) if by_core else b for b in blocks]
    return [lax.dynamic_update_index_in_dim(o, b, 4 * x + 2 * y + c, 0) for o, b in zip(outs, own)]


D2D_PIECES = 8


def _slabs(rows):
    n = D2D_PIECES if rows % (16 * D2D_PIECES) == 0 else 1
    return [pl.ds(i * (rows // n), rows // n) for i in range(n)]


def sibling_swap(arrs, *, name="sibling_swap"):
    na = len(arrs)
    pieces = [[(j, sl) for j in range(a.shape[0]) for sl in (_slabs(a.shape[1]) if a.shape[0] == 1 else _halves(a.shape[1]))]
              for a in arrs]
    base = np.cumsum([0] + [len(p) for p in pieces])

    def body(*refs):
        ins, outs = refs[:na], refs[na : 2 * na]
        send_sems, recv_sems = refs[2 * na :]
        x, y, c = _me()
        sends, recvs = [], []
        for a in range(na):
            for i, (j, sl) in enumerate(pieces[a]):
                k = int(base[a]) + i
                cp = pltpu.make_async_remote_copy(src_ref=ins[a].at[j, sl], dst_ref=outs[a].at[j, sl], send_sem=send_sems.at[k],
                                                  recv_sem=recv_sems.at[k], device_id=(x, y, 1 - c), device_id_type=MESH_ID)
                cp.start()
                sends.append(cp)
        for cp in sends:
            cp.wait_recv()
        for cp in sends:
            cp.wait_send()

    nsem = int(base[-1])
    return pl.pallas_call(
        body,
        name=name,
        in_specs=[ANY] * na,
        out_specs=[ANY] * na,
        out_shape=[jax.ShapeDtypeStruct(a.shape, a.dtype) for a in arrs],
        scratch_shapes=[pltpu.SemaphoreType.DMA((nsem,)), pltpu.SemaphoreType.DMA((nsem,))],
    )(*arrs)


def _halves(rows):
    return [pl.ds(0, rows // 2), pl.ds(rows // 2, rows // 2)] if rows % 32 == 0 else [pl.ds(0, rows)]


def chip_exchange(arrs, *, name="chip_exchange"):
    na = len(arrs)

    def body(*refs):
        ins, outs = refs[:na], refs[na : 2 * na]
        send_sems, recv_sems = refs[2 * na :]
        x, y, c = _me()
        my = 2 * x + y
        peers = []
        for k in range(1, N_CHIPS):
            px, py = (x + ((k >> 1) & 1)) % 2, (y + (k & 1)) % 2
            peers.append(((px, py, c), 2 * px + py))
        sends = []
        for a in range(na):
            for k, (peer, pidx) in enumerate(peers):
                cp = pltpu.make_async_remote_copy(src_ref=ins[a].at[pidx], dst_ref=outs[a].at[my], send_sem=send_sems.at[a * 3 + k],
                                                  recv_sem=recv_sems.at[a * 3 + k], device_id=peer, device_id_type=MESH_ID)
                cp.start()
                sends.append(cp)
        for a in range(na):
            for k, (peer, pidx) in enumerate(peers):
                pltpu.make_async_remote_copy(src_ref=ins[a].at[my], dst_ref=outs[a].at[pidx], send_sem=send_sems.at[a * 3 + k],
                                             recv_sem=recv_sems.at[a * 3 + k], device_id=peer, device_id_type=MESH_ID).wait_recv()
        for cp in sends:
            cp.wait_send()

    outs = pl.pallas_call(
        body,
        name=name,
        in_specs=[ANY] * na,
        out_specs=[ANY] * na,
        out_shape=[jax.ShapeDtypeStruct(a.shape, a.dtype) for a in arrs],
        scratch_shapes=[pltpu.SemaphoreType.DMA((na * 3,)), pltpu.SemaphoreType.DMA((na * 3,))],
    )(*arrs)
    chip = 2 * lax.axis_index("x") + lax.axis_index("y")
    return [lax.dynamic_update_index_in_dim(o, lax.dynamic_index_in_dim(a, chip, 0, keepdims=False), chip, 0)
            for o, a in zip(outs, arrs)]


def add_pair(a, b, *, name="add_pair"):
    n, r, c = a.shape
    tm = _row_tile(r, n * c * 2, 2 << 20)

    def body(a_ref, b_ref, o_ref):
        o_ref[...] = (a_ref[...].astype(F32) + b_ref[...].astype(F32)).astype(o_ref.dtype)

    blk = pl.BlockSpec((n, tm, c), lambda i: (0, i, 0))
    return pl.pallas_call(
        body, name=name, grid=(r // tm,), in_specs=[blk, blk], out_specs=blk, out_shape=jax.ShapeDtypeStruct(a.shape, a.dtype),
        compiler_params=_cparams(("arbitrary",)),
    )(a, b)


def _row_tile(rows, row_bytes, budget):
    best = None
    for t in range(8, rows + 1, 8):
        if rows % t == 0 and t * row_bytes <= budget:
            best = t
    return best or rows


def sum_blocks(a, *, name="sum_blocks"):
    n, r, c = a.shape
    tm = _row_tile(r, n * c * a.dtype.itemsize, 4 << 20)

    def body(a_ref, o_ref):
        acc = a_ref[0].astype(F32)
        for k in range(1, n):
            acc = acc + a_ref[k].astype(F32)
        o_ref[...] = acc

    return pl.pallas_call(
        body,
        name=name,
        grid=(r // tm,),
        in_specs=[pl.BlockSpec((n, tm, c), lambda i: (0, i, 0))],
        out_specs=pl.BlockSpec((tm, c), lambda i: (i, 0)),
        out_shape=jax.ShapeDtypeStruct((r, c), F32),
        compiler_params=_cparams(("arbitrary",)),
    )(a)


ADAM_LR = 0.001
ADAM_B1 = 0.9
ADAM_B2 = 0.999
ADAM_EPS = 1e-08
ADAM_WD = 0.01
ADAM_STEP = 10


def adamw(w, g, m, v, *, name="adamw"):
    r, c = w.shape[-2:]
    tm = _row_tile(r, c * 4, 1 << 20)
    if w.ndim == 3:
        blk = pl.BlockSpec((None, tm, c), lambda l, i: (l, i, 0))
        grid = (w.shape[0], r // tm)
    else:
        blk = pl.BlockSpec((tm, c), lambda i: (i, 0))
        grid = (r // tm,)

    def body(w_ref, g_ref, m_ref, v_ref, d_ref, nm_ref, nv_ref):
        g_ = g_ref[...]
        m_ = ADAM_B1 * m_ref[...] + (1.0 - ADAM_B1) * g_
        v_ = ADAM_B2 * v_ref[...] + (1.0 - ADAM_B2) * (g_ * g_)
        m_hat = m_ / (1.0 - ADAM_B1**ADAM_STEP)
        v_hat = v_ / (1.0 - ADAM_B2**ADAM_STEP)
        d_ref[...] = -ADAM_LR * (m_hat / (jnp.sqrt(v_hat) + ADAM_EPS) + ADAM_WD * w_ref[...])
        nm_ref[...] = m_
        nv_ref[...] = v_

    sds = jax.ShapeDtypeStruct(w.shape, F32)
    return pl.pallas_call(
        body, name=name, grid=grid, in_specs=[blk] * 4, out_specs=[blk] * 3, out_shape=[sds] * 3,
        compiler_params=_cparams(("arbitrary",) * len(grid)),
    )(w, g, m, v)


WEIGHTS = ["g_mix", "w_in", "s5_a_re", "s5_a_im", "s5_log_step", "s5_b_re", "s5_b_im", "s5_c_re", "s5_c_im", "s5_d", "w_glu",
           "b_glu", "w_gla_gate", "b_gla_gate", "g_gla_out", "w_branch", "w_out", "g_mem", "g_cross", "w_xq", "w_xkv", "w_xo",
           "g_mlp", "w_up", "w_down", "g_final"]
SHARDED = {"w_in": 1, "w_glu": 0, "w_branch": 2, "w_out": 0, "w_xq": 0, "w_xkv": 1, "w_xo": 0, "w_up": 1, "w_down": 0}
DEPTH = 2
N_CHIPS = 4
D_IN = 9744
C_U, C_QG, C_KG, C_VG, C_LR, C_RG, C_QA, C_KA, C_VA, C_GATE = 0, 512, 768, 1024, 1536, 1552, 2064, 3600, 5136, 6672
A_U, A_QG, A_KG, A_VG, A_RG, A_Q0, A_LR, A_W = 0, 512, 768, 1024, 1536, 2048, 3584, 4096
GLA_COLS = (A_QG, A_KG, A_VG, A_LR, A_RG)


def _split_w_in(w):
    att = lambda g: [w[:, c + MIX_W * g : c + MIX_W * (g + 1)] for c in (C_QA, C_KA, C_VA)]
    a = jnp.concatenate([w[:, C_U:C_QG], w[:, C_QG:C_KG], w[:, C_KG:C_VG], w[:, C_VG:C_LR], w[:, C_RG:C_QA],
                         *att(0), w[:, C_LR:C_RG], jnp.zeros((w.shape[0], A_W - A_LR - 16), w.dtype)], axis=1)
    return a, jnp.concatenate(att(1), axis=1), jnp.concatenate(att(2), axis=1), w[:, C_GATE:D_IN]


def _join_w_in(a, b, c, gates):
    att = lambda k: [a[:, A_Q0 + MIX_W * k : A_Q0 + MIX_W * (k + 1)], b[:, MIX_W * k : MIX_W * (k + 1)], c[:, MIX_W * k : MIX_W * (k + 1)]]
    return jnp.concatenate([a[:, A_U:A_QG], a[:, A_QG:A_KG], a[:, A_KG:A_VG], a[:, A_VG:A_RG], a[:, A_LR : A_LR + 16], a[:, A_RG:A_Q0],
                            *att(0), *att(1), *att(2), gates], axis=1)


def _by_residue(a, d):
    s = a.shape[0]
    return a if d == 1 else a.reshape(s // d, d, -1).transpose(1, 0, 2).reshape(s, -1)


def _in_order(a, d):
    s = a.shape[0]
    return a if d == 1 else a.reshape(d, s // d, -1).transpose(1, 0, 2).reshape(s, -1)


def _pack(arrs):
    flat = jnp.concatenate([a.reshape(-1) for a in arrs])
    n = flat.shape[0]
    rows = -(-n // (256 * 128)) * 256
    return jnp.pad(flat, (0, rows * 128 - n)).reshape(rows, 128)


def _unpack(packed, like):
    flat, out, o = packed.reshape(-1), [], 0
    for a in like:
        n = math.prod(a.shape)
        out.append(flat[o : o + n].reshape(a.shape))
        o += n
    return out


def _layer_fwd(x0, p, kv):
    h = rmsnorm(x0, p["g_mix"], name="mix_norm")
    hs = [h, _by_residue(h, 4), _by_residue(h, 16)]
    proj = [matmul(hs[g], p["w_in_seg"][g], out_dtype=F32 if g == 0 else BF16, name="in_proj") for g in range(3)]
    gates = matmul(h, p["w_in_seg"][3], out_dtype=BF16, name="in_proj")
    ya, xs = s5_fwd(proj[0], A_U, p["bd"], p["cd"], p["lam"], p["s5_d"], p["w_glu"], p["b_glu"])
    yb, gst = gla_fwd(proj[0], GLA_COLS, p["w_gate"], p["b_gla_gate"], p["g_gla_out"])
    att = [attn_fwd(proj[g], A_Q0 if g == 0 else 0, g) for g in range(3)]
    outs = [_in_order(att[g][0], ATT_GROUPS[g][1]) for g in range(3)]
    lses = [_in_order(att[g][1], ATT_GROUPS[g][1]) for g in range(3)]
    x1, merged, yc = merge_fwd(x0, gates, ya, yb, outs, lses, p["w_branch"], p["w_out"])
    x2 = cross_fwd(x1, p["g_cross"], p["w_xq"], p["w_xo"], kv)
    x3 = mlp_fwd(x2, p["g_mlp"], p["w_up"], p["w_down"])
    saved = dict(x0=x0, x1=x1, x2=x2, hs=hs, proj=proj, gates=gates, ya=ya, xs=xs, yb=yb, gst=gst, att=att, outs=outs,
                 lses=lses, merged=merged, yc=yc)
    return x3, saved


def _layer_bwd(dx3, p, kv, memn, sv):
    g = {}
    tn = lambda a, b, **kw: matmul(a, b, ta=True, out_dtype=BF16, name="wgrad", **kw)
    dx2, g["g_mlp"], h3, dup, act, dx3b = mlp_bwd(sv["x2"], p["g_mlp"], p["w_up"], p["w_down"], dx3)
    g["w_up"], g["w_down"] = tn(h3, dup), tn(act, dx3b)
    dx1, g["g_cross"], h2, dq, o, dkv = cross_bwd(sv["x1"], p["g_cross"], p["w_xq"], p["w_xo"], kv, dx2)
    g["w_xq"], g["w_xo"], g["w_xkv"] = tn(h2, dq), tn(o, dx2), tn(memn, dkv)
    dmemn = matmul(dkv, p["w_xkv"], tb=True, name="dmem")
    proj = sv["proj"]
    r = merge_bwd(dx1, sv["gates"], sv["ya"], sv["yb"], sv["yc"], sv["outs"], sv["lses"], p["w_branch"], p["w_out"])
    dgl, dz, dya, dyb, douts, dlses = r[0], r[1:4], r[4], r[5], r[6:9], r[9:12]
    g["w_branch"] = jnp.stack([tn(y, dz[n]) for n, y in enumerate((sv["ya"], sv["yb"], sv["yc"]))])
    g["w_out"] = tn(sv["merged"], dx1)
    datt = []
    for k in range(3):
        dil = ATT_GROUPS[k][1]
        datt.append(attn_bwd(proj[k], A_Q0 if k == 0 else 0, k, sv["att"][k][0], sv["att"][k][1],
                             _by_residue(douts[k], dil), _by_residue(dlses[k], dil)))
    du, adj, gg, dpre, dy, dlam, g["s5_d"], g["b_glu"] = s5_bwd(proj[0], A_U, sv["xs"], p["bd"], p["cd"], p["lam"], p["s5_d"],
                                                                 p["w_glu"], p["b_glu"], dya)
    dbd = matmul(proj[0], adj, ta=True, a_col=(A_U, MIX_W), name="s5_dbd")
    dcd = matmul(sv["xs"], dy, ta=True, name="s5_dcd")
    g["w_glu"] = tn(gg, dpre)
    (g["s5_a_re"], g["s5_a_im"], g["s5_log_step"], g["s5_b_re"], g["s5_b_im"], g["s5_c_re"],
     g["s5_c_im"]) = s5_param_grads(p["disc_in"], dlam, dbd, dcd)
    dqg, dkg, dvg, dlr, drg, dwg, g["b_gla_gate"], g["g_gla_out"] = gla_bwd(proj[0], GLA_COLS, p["w_gate"], p["b_gla_gate"],
                                                                             p["g_gla_out"], sv["gst"], dyb)
    g["w_gla_gate"] = dwg[:GLA_GATE_RANK]
    s = dx3.shape[0]
    dproj = [jnp.concatenate([du, dqg, dkg, dvg, drg, *datt[0], dlr, jnp.zeros((s, A_W - A_LR - LR_PAD), BF16)], axis=1),
             jnp.concatenate(datt[1], axis=1), jnp.concatenate(datt[2], axis=1)]
    g["w_in"] = _join_w_in(*[tn(sv["hs"][k], dproj[k]) for k in range(3)], tn(sv["hs"][0], dgl))
    dhs = [_in_order(matmul(dproj[k], p["w_in_seg"][k], tb=True, name="in_proj_bwd"), ATT_GROUPS[k][1]) for k in range(3)]
    dhs.append(matmul(dgl, p["w_in_seg"][3], tb=True, name="in_proj_bwd"))
    dx0, g["g_mix"] = rmsnorm_bwd(sv["x0"], p["g_mix"], dhs, dx1, name="mix_norm_bwd")
    return dx0, g, dmemn


GLA_GATE_RANK = 16


def kernel(x, mem, *rest):
    nw = len(WEIGHTS)
    w = dict(zip(WEIGHTS, rest[:nw]))
    target = rest[nw]
    m = dict(zip(WEIGHTS, rest[nw + 1 : 2 * nw + 1]))
    v = dict(zip(WEIGHTS, rest[2 * nw + 1 : 3 * nw + 1]))
    x0, memx, target = x[0], mem[0], target[0]
    chip = 2 * lax.axis_index("x") + lax.axis_index("y")

    big = list(SHARDED)
    gate_pad = jnp.pad(w["w_gla_gate"], ((0, 0), (0, 0), (0, LR_PAD - w["w_gla_gate"].shape[2])))
    flat2 = lambda t: t.reshape(DEPTH, -1, t.shape[-1])
    gathered = all_gather([flat2(w[n].astype(BF16)) for n in big] + [gate_pad], by_core=True, name="gather_weights")
    gathered = [t.reshape(N_CHIPS, DEPTH, *s.shape[1:]) for t, s in zip(gathered, [w[n] for n in big] + [gate_pad])]
    full = {n: [jnp.concatenate([t[j, l] for j in range(N_CHIPS)], axis=SHARDED[n]) for l in range(DEPTH)]
            for n, t in zip(big, gathered)}
    gate_full = [jnp.concatenate([gathered[-1][j, l][:, : w["w_gla_gate"].shape[2]] for j in range(N_CHIPS)], axis=1)
                 for l in range(DEPTH)]

    memn = rmsnorm(memx, w["g_mem"], name="mem_norm")
    params, kvs = [], []
    for l in range(DEPTH):
        p = {n: full[n][l] for n in big if n != "w_in"}
        p["w_in_seg"] = _split_w_in(full["w_in"][l])
        p["w_gate"] = jnp.pad(gate_full[l], ((0, LR_PAD - GLA_GATE_RANK), (0, 0))).astype(BF16)
        for n in ("g_mix", "g_cross", "g_mlp"):
            p[n] = w[n][l]
        for n in ("s5_d", "b_glu", "b_gla_gate", "g_gla_out"):
            p[n] = w[n][l].reshape(1, -1)
        p["disc_in"], p["lam"], p["bd"], p["cd"] = s5_prepare(*[w[n][l] for n in WEIGHTS[2:9]])
        params.append(p)
        kvs.append(matmul(memn, p["w_xkv"], out_dtype=BF16, name="mem_kv"))

    xl, saved = x0, []
    for l in range(DEPTH):
        xl, sv = _layer_fwd(xl, params[l], kvs[l])
        saved.append(sv)
    loss8, dx, dg_final = loss_head(xl, w["g_final"], target)
    grads, dmem = [None] * DEPTH, []
    for l in reversed(range(DEPTH)):
        dx, grads[l], dm = _layer_bwd(dx, params[l], kvs[l], memn, saved[l])
        dmem.append(dm)
    _, dg_mem = rmsnorm_bwd(memx, w["g_mem"], dmem, jnp.zeros_like(memx), name="mem_norm_bwd")

    core = lax.axis_index("c")

    def halves(n):
        ax = SHARDED[n]
        keep, send = [], []
        for j in range(N_CHIPS):
            p0, p1 = [jnp.split(grads[l][n], N_CHIPS, axis=ax)[j] for l in range(DEPTH)]
            keep.append(jnp.where(core == 0, p0, p1))
            send.append(jnp.where(core == 0, p1, p0))
        flat = lambda ps: jnp.stack(ps).reshape(N_CHIPS, -1, ps[0].shape[-1])
        return flat(keep), flat(send)

    kept, sent = zip(*[halves(n) for n in big])
    theirs = sibling_swap(list(sent), name="swap_layers")
    chip_sums = [add_pair(a, b, name="add_cores") for a, b in zip(kept, theirs)]
    landed = chip_exchange(chip_sums, name="exchange_grads")
    reduced = [sum_blocks(t, name="sum_grads") for t in landed]
    other = sibling_swap([t[None] for t in reduced], name="pair_layers")
    out_g = {}
    for n, mine, theirs in zip(big, reduced, other):
        both = jnp.where(core == 0, jnp.stack([mine, theirs[0]]), jnp.stack([theirs[0], mine]))
        out_g[n] = both.reshape(w[n].shape)

    small = [n for n in WEIGHTS if n not in SHARDED]
    local_small = []
    for n in small:
        if n == "g_mem":
            local_small.append(dg_mem.reshape(w[n].shape))
        elif n == "g_final":
            local_small.append(dg_final.reshape(w[n].shape))
        elif n == "w_gla_gate":
            local_small.append(jnp.stack([grads[l][n] for l in range(DEPTH)]))
        else:
            local_small.append(jnp.stack([grads[l][n].reshape(w[n].shape[1:]) for l in range(DEPTH)]))
    packed = _pack(local_small + [loss8[0, :1]])
    (every,) = all_gather([packed], name="gather_small")
    summed = _unpack(sum_blocks(every, name="sum_small"), local_small + [loss8[0, :1]])
    loss = summed[-1].reshape(())
    for n, t in zip(small, summed[:-1]):
        if n == "w_gla_gate":
            t = lax.dynamic_slice_in_dim(t, chip * w[n].shape[2], w[n].shape[2], axis=2)
        out_g[n] = t

    delta, new_m, new_v = {}, {}, {}
    for n in big:
        as3d = lambda t: t.reshape(DEPTH, -1, t.shape[-1])
        r = [t.reshape(w[n].shape) for t in adamw(as3d(w[n]), as3d(out_g[n]), as3d(m[n]), as3d(v[n]), name="adamw")]
        delta[n], new_m[n], new_v[n] = r
    like = [w[n] for n in small]
    r = adamw(_pack(like), _pack([out_g[n] for n in small]), _pack([m[n] for n in small]), _pack([v[n] for n in small]),
              name="adamw_small")
    for d, t in zip((delta, new_m, new_v), r):
        d.update(zip(small, _unpack(t, like)))
    return (loss, dx[None], *[out_g[n] for n in WEIGHTS], *[delta[n] for n in WEIGHTS], *[new_m[n] for n in WEIGHTS],
            *[new_v[n] for n in WEIGHTS])
```

```python
import functools
import math

import jax
import jax.numpy as jnp
import numpy as np
from jax import lax
from jax.experimental import pallas as pl
from jax.experimental.pallas import tpu as pltpu

F32 = jnp.float32
BF16 = jnp.bfloat16

VMEM_LIMIT_BYTES = 56 * 1024 * 1024
MATMUL_VMEM_BYTES = 36 * 1024 * 1024


def _cparams(sem):
    return pltpu.CompilerParams(dimension_semantics=sem, vmem_limit_bytes=VMEM_LIMIT_BYTES)


def _dot(a, b, ca=1, cb=0):
    return lax.dot_general(a.astype(BF16), b.astype(BF16), (((ca,), (cb,)), ((), ())), preferred_element_type=F32)


def _pick(n, prefs):
    for p in prefs:
        if n % p == 0:
            return p
    return n


def matmul(a, b, *, ta=False, tb=False, out_dtype=F32, name="mm", a_col=None, b_col=None):
    a_off, a_w = a_col if a_col is not None else (0, a.shape[1])
    b_off, b_w = b_col if b_col is not None else (0, b.shape[1])
    if ta:
        kk, m = a.shape[0], a_w
    else:
        m, kk = a.shape[0], a_w
    if tb:
        n, kb = b.shape[0], b_w
    else:
        kb, n = b.shape[0], b_w
    assert kk == kb, (a.shape, b.shape, ta, tb)
    tm = _pick(m, (512, 256, 128))
    tn = _pick(n, (1024, 512, 256, 128))
    sa, sb, so = a.dtype.itemsize, b.dtype.itemsize, jnp.dtype(out_dtype).itemsize
    fits = lambda t: 2 * (tm * t * sa + t * tn * sb) + tm * tn * (4 + 2 * so) <= MATMUL_VMEM_BYTES
    tk = max([t for t in range(128, kk + 1, 128) if kk % t == 0 and fits(t)] or [_pick(kk, (128,))])
    nk = kk // tk
    a_bytes, b_bytes = m * kk * sa, kk * n * sb
    n_outer = nk == 1 and b_bytes + a_bytes * (n // tn) < a_bytes + b_bytes * (m // tm)
    ij = (lambda g0, g1: (g1, g0)) if n_outer else (lambda g0, g1: (g0, g1))
    if ta:
        assert a_off % tm == 0
        a_spec = pl.BlockSpec((tk, tm), lambda g0, g1, k, o=a_off // tm: (k, ij(g0, g1)[0] + o))
    else:
        assert a_off % tk == 0
        a_spec = pl.BlockSpec((tm, tk), lambda g0, g1, k, o=a_off // tk: (ij(g0, g1)[0], k + o))
    if tb:
        assert b_off % tk == 0
        b_spec = pl.BlockSpec((tn, tk), lambda g0, g1, k, o=b_off // tk: (ij(g0, g1)[1], k + o))
    else:
        assert b_off % tn == 0
        b_spec = pl.BlockSpec((tk, tn), lambda g0, g1, k, o=b_off // tn: (k, ij(g0, g1)[1] + o))

    def body(a_ref, b_ref, o_ref, *acc):
        k = pl.program_id(2)
        p = _dot(a_ref[...], b_ref[...], 0 if ta else 1, 1 if tb else 0)
        if nk == 1:
            o_ref[...] = p.astype(o_ref.dtype)
            return
        (acc_ref,) = acc

        @pl.when(k == 0)
        def _():
            acc_ref[...] = p

        @pl.when(jnp.logical_and(k > 0, k < nk - 1))
        def _():
            acc_ref[...] += p

        @pl.when(k == nk - 1)
        def _():
            o_ref[...] = (acc_ref[...] + p).astype(o_ref.dtype)

    return pl.pallas_call(
        body,
        name=name,
        grid=(n // tn, m // tm, nk) if n_outer else (m // tm, n // tn, nk),
        in_specs=[a_spec, b_spec],
        out_specs=pl.BlockSpec((tm, tn), lambda g0, g1, k: ij(g0, g1)),
        out_shape=jax.ShapeDtypeStruct((m, n), out_dtype),
        scratch_shapes=[pltpu.VMEM((tm, tn), F32)] if nk > 1 else [],
        compiler_params=_cparams(("parallel", "parallel", "arbitrary")),
    )(a, b)


RMS_EPS = 1e-6
ROW_TILE = 512


def _rms_fwd(x, g):
    r = lax.rsqrt(jnp.mean(x * x, axis=-1, keepdims=True) + RMS_EPS)
    return x * r * g


def _rms_bwd(x, g, dh):
    r = lax.rsqrt(jnp.mean(x * x, axis=-1, keepdims=True) + RMS_EPS)
    xh = x * r
    dg = jnp.sum(dh * xh, axis=0, keepdims=True)
    dxh = dh * g
    dx = r * (dxh - xh * jnp.mean(dxh * xh, axis=-1, keepdims=True))
    return dx, dg


def _accum(ref, val, first):
    @pl.when(first)
    def _():
        ref[...] = val

    @pl.when(jnp.logical_not(first))
    def _():
        ref[...] += val


def rmsnorm(x, g, *, out_dtype=BF16, name="rmsnorm"):
    s, d = x.shape
    tm = _pick(s, (ROW_TILE, 256, 128, 8))

    def body(x_ref, g_ref, o_ref):
        o_ref[...] = _rms_fwd(x_ref[...], g_ref[...]).astype(o_ref.dtype)

    return pl.pallas_call(
        body,
        name=name,
        grid=(s // tm,),
        in_specs=[pl.BlockSpec((tm, d), lambda i: (i, 0)), pl.BlockSpec((1, d), lambda i: (0, 0))],
        out_specs=pl.BlockSpec((tm, d), lambda i: (i, 0)),
        out_shape=jax.ShapeDtypeStruct((s, d), out_dtype),
        compiler_params=_cparams(("arbitrary",)),
    )(x, g.reshape(1, d))


def rmsnorm_bwd(x, g, dhs, dres, *, name="rmsnorm_bwd"):
    s, d = x.shape
    tm = _pick(s, (ROW_TILE, 256, 128, 8))
    n = len(dhs)

    def body(x_ref, g_ref, *refs):
        dh_refs, dres_ref, dx_ref, dg_ref = refs[:n], refs[n], refs[n + 1], refs[n + 2]
        dh = dh_refs[0][...].astype(F32)
        for r in dh_refs[1:]:
            dh = dh + r[...].astype(F32)
        dx, dg = _rms_bwd(x_ref[...], g_ref[...], dh)
        dx_ref[...] = dres_ref[...] + dx
        _accum(dg_ref, dg, pl.program_id(0) == 0)

    row = pl.BlockSpec((tm, d), lambda i: (i, 0))
    vec = pl.BlockSpec((1, d), lambda i: (0, 0))
    return pl.pallas_call(
        body,
        name=name,
        grid=(s // tm,),
        in_specs=[row, vec] + [row] * n + [row],
        out_specs=[row, vec],
        out_shape=[jax.ShapeDtypeStruct((s, d), F32), jax.ShapeDtypeStruct((1, d), F32)],
        compiler_params=_cparams(("arbitrary",)),
    )(x, g.reshape(1, d), *dhs, dres)


def loss_head(x, g, target, *, name="loss_head"):
    s, d = x.shape
    tm = _pick(s, (ROW_TILE, 256, 128, 8))

    def body(x_ref, g_ref, t_ref, l_ref, dx_ref, dg_ref):
        x_, g_ = x_ref[...], g_ref[...]
        e = _rms_fwd(x_, g_) - t_ref[...]
        part = 0.5 * jnp.sum(jnp.sum(e * e, axis=-1, keepdims=True), axis=0, keepdims=True) / d
        dx, dg = _rms_bwd(x_, g_, e * (1.0 / d))
        dx_ref[...] = dx
        first = pl.program_id(0) == 0
        _accum(dg_ref, dg, first)
        _accum(l_ref, jnp.broadcast_to(part, (8, 128)), first)

    row = pl.BlockSpec((tm, d), lambda i: (i, 0))
    vec = pl.BlockSpec((1, d), lambda i: (0, 0))
    return pl.pallas_call(
        body,
        name=name,
        grid=(s // tm,),
        in_specs=[row, vec, row],
        out_specs=[pl.BlockSpec((8, 128), lambda i: (0, 0)), row, vec],
        out_shape=[jax.ShapeDtypeStruct((8, 128), F32), jax.ShapeDtypeStruct((s, d), F32), jax.ShapeDtypeStruct((1, d), F32)],
        compiler_params=_cparams(("arbitrary",)),
    )(x, g.reshape(1, d), target)


FF_TILE = 2048


def mlp_fwd(x, g, w_up, w_down, *, name="mlp_fwd"):
    s, d = x.shape
    ff = w_up.shape[1]
    tm, tf = _pick(s, (ROW_TILE, 256, 128)), _pick(ff, (FF_TILE, 512, 256, 128))
    nf = ff // tf

    def body(x_ref, g_ref, wu_ref, wd_ref, o_ref, h_ref, acc_ref):
        f = pl.program_id(1)

        @pl.when(f == 0)
        def _():
            h_ref[...] = _rms_fwd(x_ref[...], g_ref[...]).astype(BF16)
            acc_ref[...] = x_ref[...]

        up = jnp.maximum(_dot(h_ref[...], wu_ref[...]), 0.0)
        acc_ref[...] += _dot(up * up, wd_ref[...])

        @pl.when(f == nf - 1)
        def _():
            o_ref[...] = acc_ref[...]

    return pl.pallas_call(
        body,
        name=name,
        grid=(s // tm, nf),
        in_specs=[
            pl.BlockSpec((tm, d), lambda i, f: (i, 0)),
            pl.BlockSpec((1, d), lambda i, f: (0, 0)),
            pl.BlockSpec((d, tf), lambda i, f: (0, f)),
            pl.BlockSpec((tf, d), lambda i, f: (f, 0)),
        ],
        out_specs=pl.BlockSpec((tm, d), lambda i, f: (i, 0)),
        out_shape=jax.ShapeDtypeStruct((s, d), F32),
        scratch_shapes=[pltpu.VMEM((tm, d), BF16), pltpu.VMEM((tm, d), F32)],
        compiler_params=_cparams(("arbitrary", "arbitrary")),
    )(x, g.reshape(1, d), w_up, w_down)


def mlp_bwd(x, g, w_up, w_down, dy, *, name="mlp_bwd"):
    s, d = x.shape
    ff = w_up.shape[1]
    tm, tf = _pick(s, (ROW_TILE, 256, 128)), _pick(ff, (FF_TILE, 512, 256, 128))
    nf = ff // tf

    def body(x_ref, g_ref, wu_ref, wd_ref, dy_ref, dx_ref, dg_ref, h_ref, dup_ref, act_ref, dyb_ref, acc_ref):
        i, f = pl.program_id(0), pl.program_id(1)

        @pl.when(f == 0)
        def _():
            h_ref[...] = _rms_fwd(x_ref[...], g_ref[...]).astype(BF16)
            dyb_ref[...] = dy_ref[...].astype(BF16)
            acc_ref[...] = jnp.zeros_like(acc_ref)

        up = jnp.maximum(_dot(h_ref[...], wu_ref[...]), 0.0)
        dact = _dot(dyb_ref[...], wd_ref[...], 1, 1)
        dup = (2.0 * up * dact).astype(BF16)
        dup_ref[...] = dup
        act_ref[...] = (up * up).astype(BF16)
        acc_ref[...] += _dot(dup, wu_ref[...], 1, 1)

        @pl.when(f == nf - 1)
        def _():
            dx, dg = _rms_bwd(x_ref[...], g_ref[...], acc_ref[...])
            dx_ref[...] = dy_ref[...] + dx
            _accum(dg_ref, dg, i == 0)

    row = pl.BlockSpec((tm, d), lambda i, f: (i, 0))
    vec = pl.BlockSpec((1, d), lambda i, f: (0, 0))
    wide = pl.BlockSpec((tm, tf), lambda i, f: (i, f))
    return pl.pallas_call(
        body,
        name=name,
        grid=(s // tm, nf),
        in_specs=[row, vec, pl.BlockSpec((d, tf), lambda i, f: (0, f)), pl.BlockSpec((tf, d), lambda i, f: (f, 0)), row],
        out_specs=[row, vec, row, wide, wide, row],
        out_shape=[
            jax.ShapeDtypeStruct((s, d), F32),
            jax.ShapeDtypeStruct((1, d), F32),
            jax.ShapeDtypeStruct((s, d), BF16),
            jax.ShapeDtypeStruct((s, ff), BF16),
            jax.ShapeDtypeStruct((s, ff), BF16),
            jax.ShapeDtypeStruct((s, d), BF16),
        ],
        scratch_shapes=[pltpu.VMEM((tm, d), F32)],
        compiler_params=_cparams(("arbitrary", "arbitrary")),
    )(x, g.reshape(1, d), w_up, w_down, dy)


X_HEADS = 4


def _softmax_rows(s):
    m = jnp.max(s, axis=-1, keepdims=True)
    e = jnp.exp(s - m)
    return e / jnp.sum(e, axis=-1, keepdims=True)


def cross_fwd(x, g, wq, wo, kv, *, name="cross_fwd"):
    s, d = x.shape
    ml = kv.shape[0]
    dh = d // X_HEADS
    tm = _pick(s, (ROW_TILE, 256, 128))
    scale = dh**-0.5

    def body(x_ref, g_ref, wq_ref, wo_ref, kv_ref, o_ref):
        x_ = x_ref[...]
        q = _dot(_rms_fwd(x_, g_ref[...]), wq_ref[...])
        outs = []
        for hd in range(X_HEADS):
            kh = kv_ref[:, hd * dh : (hd + 1) * dh]
            vh = kv_ref[:, d + hd * dh : d + (hd + 1) * dh]
            p = _softmax_rows(_dot(q[:, hd * dh : (hd + 1) * dh], kh, 1, 1) * scale)
            outs.append(_dot(p, vh))
        o_ref[...] = x_ + _dot(jnp.concatenate(outs, axis=-1), wo_ref[...])

    row = pl.BlockSpec((tm, d), lambda i: (i, 0))
    full = lambda shp: pl.BlockSpec(shp, lambda i: (0, 0))
    return pl.pallas_call(
        body,
        name=name,
        grid=(s // tm,),
        in_specs=[row, full((1, d)), full((d, d)), full((d, d)), full((ml, 2 * d))],
        out_specs=row,
        out_shape=jax.ShapeDtypeStruct((s, d), F32),
        compiler_params=_cparams(("arbitrary",)),
    )(x, g.reshape(1, d), wq, wo, kv)


def cross_bwd(x, g, wq, wo, kv, dy, *, name="cross_bwd"):
    s, d = x.shape
    ml = kv.shape[0]
    dh = d // X_HEADS
    tm = _pick(s, (ROW_TILE, 256, 128))
    scale = dh**-0.5

    def body(x_ref, g_ref, wq_ref, wo_ref, kv_ref, dy_ref, dx_ref, dg_ref, h_ref, dq_ref, o_ref, dkv_ref):
        first = pl.program_id(0) == 0
        x_, g_ = x_ref[...], g_ref[...]
        h = _rms_fwd(x_, g_).astype(BF16)
        h_ref[...] = h
        q = _dot(h, wq_ref[...])
        dy_ = dy_ref[...]
        do = _dot(dy_, wo_ref[...], 1, 1)
        outs, dqs, dks, dvs = [], [], [], []
        for hd in range(X_HEADS):
            sl = slice(hd * dh, (hd + 1) * dh)
            kh = kv_ref[:, sl]
            vh = kv_ref[:, d + hd * dh : d + (hd + 1) * dh]
            qh = q[:, sl]
            p = _softmax_rows(_dot(qh, kh, 1, 1) * scale)
            outs.append(_dot(p, vh))
            doh = do[:, sl]
            dp = _dot(doh, vh, 1, 1)
            ds = p * (dp - jnp.sum(dp * p, axis=-1, keepdims=True)) * scale
            dqs.append(_dot(ds, kh))
            dks.append(_dot(ds, qh, 0, 0))
            dvs.append(_dot(p, doh, 0, 0))
        o_ref[...] = jnp.concatenate(outs, axis=-1).astype(BF16)
        dq = jnp.concatenate(dqs, axis=-1).astype(BF16)
        dq_ref[...] = dq
        dx, dg = _rms_bwd(x_, g_, _dot(dq, wq_ref[...], 1, 1))
        dx_ref[...] = dy_ + dx
        _accum(dkv_ref, jnp.concatenate(dks + dvs, axis=-1), first)
        _accum(dg_ref, dg, first)

    row = pl.BlockSpec((tm, d), lambda i: (i, 0))
    full = lambda shp: pl.BlockSpec(shp, lambda i: (0, 0))
    rowb = jax.ShapeDtypeStruct((s, d), BF16)
    return pl.pallas_call(
        body,
        name=name,
        grid=(s // tm,),
        in_specs=[row, full((1, d)), full((d, d)), full((d, d)), full((ml, 2 * d)), row],
        out_specs=[row, full((1, d)), row, row, row, full((ml, 2 * d))],
        out_shape=[jax.ShapeDtypeStruct((s, d), F32), jax.ShapeDtypeStruct((1, d), F32), rowb, rowb, rowb,
                   jax.ShapeDtypeStruct((ml, 2 * d), F32)],
        compiler_params=_cparams(("arbitrary",)),
    )(x, g.reshape(1, d), wq, wo, kv, dy)


N_BRANCH = 3
MIX_W = 512
ATT_DH = 128
ATT_HG = 4
MERGE_TILE = 256


def _group_weights(l0, l1, l2):
    m = jnp.maximum(jnp.maximum(l0, l1), l2)
    e = [jnp.exp(l0 - m), jnp.exp(l1 - m), jnp.exp(l2 - m)]
    inv = 1.0 / (e[0] + e[1] + e[2])
    return [t * inv for t in e]


def _sigmoid(x):
    return 0.5 * (jnp.tanh(0.5 * x) + 1.0)


def merge_fwd(x, proj, ya, yb, outs, lses, wb, wout, *, name="merge_fwd"):
    s, d = x.shape
    tm = _pick(s, (MERGE_TILE, 128))

    def body(x_ref, g0, g1, g2, ya_ref, yb_ref, o0, o1, o2, l0, l1, l2, wb_ref, wo_ref, x1_ref, mg_ref, yc_ref):
        w = _group_weights(l0[...], l1[...], l2[...])
        o = (o0[...], o1[...], o2[...])
        heads = []
        for hd in range(ATT_HG):
            sl = slice(hd * ATT_DH, (hd + 1) * ATT_DH)
            heads.append(sum(w[gi][:, hd : hd + 1] * o[gi][:, sl] for gi in range(3)))
        yc = jnp.concatenate(heads, axis=-1)
        yc_ref[...] = yc.astype(BF16)
        merged = None
        for n, (y, gl) in enumerate(((ya_ref[...], g0), (yb_ref[...], g1), (yc, g2))):
            t = _sigmoid(gl[...].astype(F32)) * _dot(y, wb_ref[n])
            merged = t if merged is None else merged + t
        mb = merged.astype(BF16)
        mg_ref[...] = mb
        x1_ref[...] = x_ref[...] + _dot(mb, wo_ref[...])

    row = pl.BlockSpec((tm, d), lambda i: (i, 0))
    half = pl.BlockSpec((tm, MIX_W), lambda i: (i, 0))
    perhead = pl.BlockSpec((tm, ATT_HG), lambda i: (i, 0))
    gate = [pl.BlockSpec((tm, d), lambda i, n=n: (i, n)) for n in range(N_BRANCH)]
    return pl.pallas_call(
        body,
        name=name,
        grid=(s // tm,),
        in_specs=[row] + gate + [half] * 5 + [perhead] * 3
        + [pl.BlockSpec((N_BRANCH, MIX_W, d), lambda i: (0, 0, 0)), pl.BlockSpec((d, d), lambda i: (0, 0))],
        out_specs=[row, row, half],
        out_shape=[jax.ShapeDtypeStruct((s, d), F32), jax.ShapeDtypeStruct((s, d), BF16), jax.ShapeDtypeStruct((s, MIX_W), BF16)],
        compiler_params=_cparams(("arbitrary",)),
    )(x, proj, proj, proj, ya, yb, *outs, *lses, wb, wout)


def merge_bwd(dx1, proj, ya, yb, yc, outs, lses, wb, wout, *, name="merge_bwd"):
    s, d = dx1.shape
    tm = _pick(s, (MERGE_TILE, 128))

    def body(dx_ref, g0, g1, g2, ya_ref, yb_ref, yc_ref, o0, o1, o2, l0, l1, l2, wb_ref, wo_ref,
             dgl_ref, dz0, dz1, dz2, dya_ref, dyb_ref, do0, do1, do2, dl0, dl1, dl2):
        dm = _dot(dx_ref[...], wo_ref[...], 1, 1)
        dys = []
        for n, (y, gl, dz_ref) in enumerate(((ya_ref, g0, dz0), (yb_ref, g1, dz1), (yc_ref, g2, dz2))):
            z = _dot(y[...], wb_ref[n])
            sg = _sigmoid(gl[...].astype(F32))
            dz = (dm * sg).astype(BF16)
            dz_ref[...] = dz
            dgl_ref[:, n * d : (n + 1) * d] = (dm * z * sg * (1.0 - sg)).astype(BF16)
            dys.append(_dot(dz, wb_ref[n], 1, 1))
        dya_ref[...] = dys[0]
        dyb_ref[...] = dys[1]
        dyc = dys[2]
        w = _group_weights(l0[...], l1[...], l2[...])
        o = [o0[...], o1[...], o2[...]]
        for hd in range(ATT_HG):
            sl = slice(hd * ATT_DH, (hd + 1) * ATT_DH)
            col = slice(hd, hd + 1)
            wh = [w[gi][:, col] for gi in range(3)]
            for gi, r in enumerate((do0, do1, do2)):
                r[:, sl] = wh[gi] * dyc[:, sl]
            t = [jnp.sum(dyc[:, sl] * o[gi][:, sl], axis=-1, keepdims=True) for gi in range(3)]
            tbar = wh[0] * t[0] + wh[1] * t[1] + wh[2] * t[2]
            for gi, r in enumerate((dl0, dl1, dl2)):
                r[:, col] = wh[gi] * (t[gi] - tbar)

    row = pl.BlockSpec((tm, d), lambda i: (i, 0))
    half = pl.BlockSpec((tm, MIX_W), lambda i: (i, 0))
    perhead = pl.BlockSpec((tm, ATT_HG), lambda i: (i, 0))
    gate = [pl.BlockSpec((tm, d), lambda i, n=n: (i, n)) for n in range(N_BRANCH)]
    rb = jax.ShapeDtypeStruct((s, d), BF16)
    hf = jax.ShapeDtypeStruct((s, MIX_W), F32)
    ph = jax.ShapeDtypeStruct((s, ATT_HG), F32)
    return pl.pallas_call(
        body,
        name=name,
        grid=(s // tm,),
        in_specs=[row] + gate + [half] * 6 + [perhead] * 3
        + [pl.BlockSpec((N_BRANCH, MIX_W, d), lambda i: (0, 0, 0)), pl.BlockSpec((d, d), lambda i: (0, 0))],
        out_specs=[pl.BlockSpec((tm, N_BRANCH * d), lambda i: (i, 0)), row, row, row] + [half] * 5 + [perhead] * 3,
        out_shape=[jax.ShapeDtypeStruct((s, N_BRANCH * d), BF16), rb, rb, rb] + [hf] * 5 + [ph] * 3,
        compiler_params=_cparams(("arbitrary",)),
    )(dx1, proj, proj, proj, ya, yb, yc, *outs, *lses, wb, wout)


ATT_BLOCK = 128
ATT_STEP = 4 * ATT_BLOCK
ATT_GROUPS = ((128, 1), (512, 4), (2048, 16))
N_ATT_HEADS = ATT_HG * len(ATT_GROUPS)
ALIBI_MAX_EXP = 8.0
MASKED = -1e30


def _att_slopes(group):
    return [2.0 ** (-ALIBI_MAX_EXP * (group * ATT_HG + h + 1) / N_ATT_HEADS) for h in range(ATT_HG)]


def _att_scores(q, kk, slope_dil, has_prev):
    scale = ATT_DH**-0.5
    qi = lax.broadcasted_iota(jnp.int32, (ATT_BLOCK, 2 * ATT_BLOCK), 0)
    kj = lax.broadcasted_iota(jnp.int32, (ATT_BLOCK, 2 * ATT_BLOCK), 1)
    dist = qi + ATT_BLOCK - kj
    valid = jnp.logical_and(jnp.logical_and(dist >= 0, dist <= ATT_BLOCK), jnp.logical_or(kj >= ATT_BLOCK, has_prev))
    return jnp.where(valid, _dot(q, kk, 1, 1) * scale - slope_dil * dist.astype(F32), MASKED)


def _att_window(kp_ref, kc_ref, buf):
    buf[:ATT_BLOCK, :] = kp_ref[...]
    buf[ATT_BLOCK:, :] = kc_ref[...]


def attn_fwd(qkv, col, group, *, name="attn_fwd"):
    s = qkv.shape[0]
    window, dil = ATT_GROUPS[group]
    assert window // dil == ATT_BLOCK and s % (dil * ATT_BLOCK) == 0 and col % MIX_W == 0
    nb = s // dil // ATT_BLOCK
    nq = ATT_STEP // ATT_BLOCK
    ng = s // ATT_STEP
    slopes = _att_slopes(group)
    c0 = col // MIX_W

    def body(q_ref, kp_ref, kc_ref, vp_ref, vc_ref, o_ref, l_ref, kbuf, vbuf):
        n = pl.program_id(0)
        _att_window(kp_ref, kc_ref, kbuf)
        _att_window(vp_ref, vc_ref, vbuf)
        for i in range(nq):
            r = slice(i * ATT_BLOCK, (i + 1) * ATT_BLOCK)
            win = slice(i * ATT_BLOCK, (i + 2) * ATT_BLOCK)
            has_prev = ((n * nq + i) % nb) != 0
            for hd in range(ATT_HG):
                sl = slice(hd * ATT_DH, (hd + 1) * ATT_DH)
                sc = _att_scores(q_ref[r, sl], kbuf[win, sl], slopes[hd] * dil, has_prev)
                m = jnp.max(sc, axis=-1, keepdims=True)
                e = jnp.exp(sc - m)
                den = jnp.sum(e, axis=-1, keepdims=True)
                o_ref[r, sl] = _dot(e * (1.0 / den), vbuf[win, sl])
                l_ref[r, hd : hd + 1] = m + jnp.log(den)

    cur = lambda c: pl.BlockSpec((ATT_STEP, MIX_W), lambda n, c=c: (n, c))
    prev = lambda c: pl.BlockSpec((ATT_BLOCK, MIX_W), lambda n, c=c: (jnp.maximum(n * nq - 1, 0), c))
    blk = pl.BlockSpec((ATT_STEP, MIX_W), lambda n: (n, 0))
    return pl.pallas_call(
        body,
        name=name,
        grid=(ng,),
        in_specs=[cur(c0), prev(c0 + 1), cur(c0 + 1), prev(c0 + 2), cur(c0 + 2)],
        out_specs=[blk, pl.BlockSpec((ATT_STEP, ATT_HG), lambda n: (n, 0))],
        out_shape=[jax.ShapeDtypeStruct((s, MIX_W), F32), jax.ShapeDtypeStruct((s, ATT_HG), F32)],
        scratch_shapes=[pltpu.VMEM((ATT_BLOCK + ATT_STEP, MIX_W), qkv.dtype)] * 2,
        compiler_params=_cparams(("arbitrary",)),
    )(qkv, qkv, qkv, qkv, qkv)


def attn_bwd(qkv, col, group, out, lse, dout, dlse, *, name="attn_bwd"):
    s = qkv.shape[0]
    window, dil = ATT_GROUPS[group]
    nb = s // dil // ATT_BLOCK
    nq = ATT_STEP // ATT_BLOCK
    ng = s // ATT_STEP
    slopes = _att_slopes(group)
    c0 = col // MIX_W
    scale = ATT_DH**-0.5
    tail = slice((nq - 1) * ATT_BLOCK, nq * ATT_BLOCK)

    def body(q_ref, kp_ref, kc_ref, vp_ref, vc_ref, o_ref, l_ref, do_ref, dl_ref, dq_ref, dk_ref, dv_ref, acck, accv,
             kbuf, vbuf):
        n = pl.program_id(0)
        live = n < ng

        @pl.when(n == 0)
        def _():
            acck[...] = jnp.zeros_like(acck)
            accv[...] = jnp.zeros_like(accv)

        _att_window(kp_ref, kc_ref, kbuf)
        _att_window(vp_ref, vc_ref, vbuf)
        dk_ref[: (nq - 1) * ATT_BLOCK, :] = acck[: (nq - 1) * ATT_BLOCK, :].astype(BF16)
        dv_ref[: (nq - 1) * ATT_BLOCK, :] = accv[: (nq - 1) * ATT_BLOCK, :].astype(BF16)
        for i in range(nq):
            r = slice(i * ATT_BLOCK, (i + 1) * ATT_BLOCK)
            rp = slice((i - 1) * ATT_BLOCK, i * ATT_BLOCK)
            win = slice(i * ATT_BLOCK, (i + 2) * ATT_BLOCK)
            has_prev = ((jnp.minimum(n, ng - 1) * nq + i) % nb) != 0
            for hd in range(ATT_HG):
                sl = slice(hd * ATT_DH, (hd + 1) * ATT_DH)
                q, do, kk, vv = q_ref[r, sl], do_ref[r, sl], kbuf[win, sl], vbuf[win, sl]
                p = jnp.exp(_att_scores(q, kk, slopes[hd] * dil, has_prev) - l_ref[r, hd : hd + 1])
                corr = dl_ref[r, hd : hd + 1] - jnp.sum(do * o_ref[r, sl], axis=-1, keepdims=True)
                ds = p * (_dot(do, vv, 1, 1) + corr) * scale
                dq_ref[r, sl] = _dot(ds, kk).astype(BF16)
                dkk = jnp.where(live, _dot(ds, q, 0, 0), 0.0)
                dvv = jnp.where(live, _dot(p, do, 0, 0), 0.0)
                if i == 0:
                    dk_ref[tail, sl] = (acck[tail, sl] + dkk[:ATT_BLOCK]).astype(BF16)
                    dv_ref[tail, sl] = (accv[tail, sl] + dvv[:ATT_BLOCK]).astype(BF16)
                else:
                    acck[rp, sl] += dkk[:ATT_BLOCK]
                    accv[rp, sl] += dvv[:ATT_BLOCK]
                acck[r, sl] = dkk[ATT_BLOCK:]
                accv[r, sl] = dvv[ATT_BLOCK:]

    last = ng - 1
    cur = lambda c: pl.BlockSpec((ATT_STEP, MIX_W), lambda n, c=c: (jnp.minimum(n, last), c))
    prev = lambda c: pl.BlockSpec((ATT_BLOCK, MIX_W), lambda n, c=c: (jnp.maximum(jnp.minimum(n, last) * nq - 1, 0), c))
    behind = pl.BlockSpec((ATT_STEP, MIX_W), lambda n: (jnp.clip(n - 1, 0, last), 0))
    perhead = pl.BlockSpec((ATT_STEP, ATT_HG), lambda n: (jnp.minimum(n, last), 0))
    sds = jax.ShapeDtypeStruct((s, MIX_W), BF16)
    return pl.pallas_call(
        body,
        name=name,
        grid=(ng + 1,),
        in_specs=[cur(c0), prev(c0 + 1), cur(c0 + 1), prev(c0 + 2), cur(c0 + 2), cur(0), perhead, cur(0), perhead],
        out_specs=[cur(0), behind, behind],
        out_shape=[sds, sds, sds],
        scratch_shapes=[pltpu.VMEM((ATT_STEP, MIX_W), F32)] * 2 + [pltpu.VMEM((ATT_BLOCK + ATT_STEP, MIX_W), qkv.dtype)] * 2,
        compiler_params=_cparams(("arbitrary",)),
    )(qkv, qkv, qkv, qkv, qkv, out, lse, dout, dlse)


GLA_HEADS = 4
GLA_DK = 64
GLA_DV = 128
GLA_CHUNK = 64
GLA_TAU = 16.0
GLA_QK = GLA_HEADS * GLA_DK
LR_PAD = 128


def _dot_exact(a, b, ca=1, cb=0, precision=lax.Precision.HIGHEST):
    return lax.dot_general(a, b, (((ca,), (cb,)), ((), ())), precision=precision, preferred_element_type=F32)


GLA_STEP = 4 * GLA_CHUNK


@jax.custom_vjp
def _score_dot(a, b):
    return _dot_exact(a, b, 1, 1)


def _score_dot_fwd(a, b):
    return _dot_exact(a, b, 1, 1), (a, b)


def _score_dot_bwd(res, g):
    a, b = res
    return _dot(g, b), _dot(g, a, 0, 0)


_score_dot.defvjp(_score_dot_fwd, _score_dot_bwd)


def _gla_group(q, k, v, lr, r, state, wg, bg, go):
    c, n = GLA_CHUNK, q.shape[0]
    z = _dot(lr, wg) + bg
    la = (jnp.minimum(z, 0.0) - jnp.log(1.0 + jnp.exp(-jnp.abs(z)))) * (1.0 / GLA_TAU)
    ri = lax.broadcasted_iota(jnp.int32, (n, n), 0)
    ci = lax.broadcasted_iota(jnp.int32, (n, n), 1)
    shift = c.bit_length() - 1
    in_chunk = jnp.logical_and(ri >= ci, jnp.right_shift(ri, shift) == jnp.right_shift(ci, shift))
    ball = _dot_exact(in_chunk.astype(F32), la, precision=lax.Precision.HIGH)
    ri = lax.broadcasted_iota(jnp.int32, (c, c), 0)
    ci = lax.broadcasted_iota(jnp.int32, (c, c), 1)
    causal = ri >= ci
    qs = q * GLA_DK**-0.5
    ys = []
    for j in range(n // c):
        rows = slice(j * c, (j + 1) * c)
        b = ball[rows]
        bmid = b[c // 2 : c // 2 + 1, :]
        bend = b[c - 1 : c, :]
        q_in = qs[rows] * jnp.exp(b)
        q_mid = qs[rows] * jnp.exp(b - bmid)
        k_mid = k[rows] * jnp.exp(bmid - b)
        k_end = k[rows] * jnp.exp(bend - b)
        yh, upd = [], []
        for h in range(GLA_HEADS):
            sk = slice(h * GLA_DK, (h + 1) * GLA_DK)
            sv = slice(h * GLA_DV, (h + 1) * GLA_DV)
            vh = v[rows, sv]
            inter = _dot(q_in[:, sk], state[:, sk], 1, 1)
            sc = jnp.where(causal, _score_dot(q_mid[:, sk], k_mid[:, sk]), 0.0)
            o = inter + _dot(sc, vh)
            o = o * lax.rsqrt(jnp.mean(o * o, axis=-1, keepdims=True) + RMS_EPS) * go
            rh = r[rows, sv]
            yh.append(o * rh * _sigmoid(rh))
            upd.append(_dot(vh, k_end[:, sk], 0, 0))
        state = jnp.exp(bend) * state + jnp.concatenate(upd, axis=-1)
        ys.append(jnp.concatenate(yh, axis=-1))
    return jnp.concatenate(ys, axis=0), state


def _gla_in_specs(cols, rev, nc):
    c = GLA_STEP
    row = (lambda i: nc - 1 - i) if rev else (lambda i: i)
    qc, kc, vc, lc, rc = cols
    assert qc % GLA_QK == 0 and kc % GLA_QK == 0 and vc % MIX_W == 0 and rc % MIX_W == 0 and lc % LR_PAD == 0
    return [
        pl.BlockSpec((c, GLA_QK), lambda i: (row(i), qc // GLA_QK)),
        pl.BlockSpec((c, GLA_QK), lambda i: (row(i), kc // GLA_QK)),
        pl.BlockSpec((c, MIX_W), lambda i: (row(i), vc // MIX_W)),
        pl.BlockSpec((c, LR_PAD), lambda i: (row(i), lc // LR_PAD)),
        pl.BlockSpec((c, MIX_W), lambda i: (row(i), rc // MIX_W)),
    ], row


def gla_fwd(proj, cols, wg, bg, go, *, name="gla_fwd"):
    s = proj.shape[0]
    nc = s // GLA_STEP
    specs, row = _gla_in_specs(cols, False, nc)

    def body(q_ref, k_ref, v_ref, lr_ref, r_ref, wg_ref, bg_ref, go_ref, y_ref, st_ref, state):
        @pl.when(pl.program_id(0) == 0)
        def _():
            state[...] = jnp.zeros_like(state)

        st = state[...]
        st_ref[...] = st
        y, new = _gla_group(q_ref[...], k_ref[...], v_ref[...], lr_ref[...], r_ref[...], st, wg_ref[...], bg_ref[...], go_ref[...])
        y_ref[...] = y
        state[...] = new

    full = lambda shp: pl.BlockSpec(shp, lambda i: (0, 0))
    return pl.pallas_call(
        body,
        name=name,
        grid=(nc,),
        in_specs=specs + [full((LR_PAD, GLA_QK)), full((1, GLA_QK)), full((1, GLA_DV))],
        out_specs=[pl.BlockSpec((GLA_STEP, MIX_W), lambda i: (i, 0)), pl.BlockSpec((GLA_DV, GLA_QK), lambda i: (i, 0))],
        out_shape=[jax.ShapeDtypeStruct((s, MIX_W), F32), jax.ShapeDtypeStruct((nc * GLA_DV, GLA_QK), F32)],
        scratch_shapes=[pltpu.VMEM((GLA_DV, GLA_QK), F32)],
        compiler_params=_cparams(("arbitrary",)),
    )(proj, proj, proj, proj, proj, wg, bg, go)


def gla_bwd(proj, cols, wg, bg, go, states, dy, *, name="gla_bwd"):
    s = proj.shape[0]
    nc = s // GLA_STEP
    specs, row = _gla_in_specs(cols, True, nc)

    def body(q_ref, k_ref, v_ref, lr_ref, r_ref, wg_ref, bg_ref, go_ref, st_ref, dy_ref,
             dq_ref, dk_ref, dv_ref, dlr_ref, dr_ref, dwg_ref, dbg_ref, dgo_ref, dstate):
        first = pl.program_id(0) == 0

        @pl.when(first)
        def _():
            dstate[...] = jnp.zeros_like(dstate)

        _, vjp = jax.vjp(_gla_group, q_ref[...], k_ref[...], v_ref[...], lr_ref[...], r_ref[...], st_ref[...],
                         wg_ref[...].astype(F32), bg_ref[...], go_ref[...])
        dq, dk, dv, dlr, dr, dst, dwg, dbg, dgo = vjp((dy_ref[...], dstate[...]))
        dq_ref[...] = dq.astype(BF16)
        dk_ref[...] = dk.astype(BF16)
        dv_ref[...] = dv.astype(BF16)
        dlr_ref[...] = dlr.astype(BF16)
        dr_ref[...] = dr.astype(BF16)
        dstate[...] = dst
        _accum(dwg_ref, dwg, first)
        _accum(dbg_ref, dbg, first)
        _accum(dgo_ref, dgo, first)

    c = GLA_STEP
    full = lambda shp: pl.BlockSpec(shp, lambda i: (0, 0))
    rows = lambda w: pl.BlockSpec((c, w), lambda i: (row(i), 0))
    return pl.pallas_call(
        body,
        name=name,
        grid=(nc,),
        in_specs=specs + [full((LR_PAD, GLA_QK)), full((1, GLA_QK)), full((1, GLA_DV)),
                          pl.BlockSpec((GLA_DV, GLA_QK), lambda i: (row(i), 0)), rows(MIX_W)],
        out_specs=[rows(GLA_QK), rows(GLA_QK), rows(MIX_W), rows(LR_PAD), rows(MIX_W),
                   full((LR_PAD, GLA_QK)), full((1, GLA_QK)), full((1, GLA_DV))],
        out_shape=[jax.ShapeDtypeStruct((s, GLA_QK), BF16), jax.ShapeDtypeStruct((s, GLA_QK), BF16),
                   jax.ShapeDtypeStruct((s, MIX_W), BF16), jax.ShapeDtypeStruct((s, LR_PAD), BF16),
                   jax.ShapeDtypeStruct((s, MIX_W), BF16), jax.ShapeDtypeStruct((LR_PAD, GLA_QK), F32),
                   jax.ShapeDtypeStruct((1, GLA_QK), F32), jax.ShapeDtypeStruct((1, GLA_DV), F32)],
        scratch_shapes=[pltpu.VMEM((GLA_DV, GLA_QK), F32)],
        compiler_params=_cparams(("arbitrary",)),
    )(proj, proj, proj, proj, proj, wg, bg, go, states, dy)


S5_G = 32
S5_P = 64
S5_C = 16
S5_N = S5_G * S5_P
S5_TC = 256
SUB = 8


def _s5_disc(a_re, a_im, ls, b_re, b_im):
    step = jnp.exp(ls)
    mag = jnp.exp(a_re * step)
    lr, li = mag * jnp.cos(a_im * step), mag * jnp.sin(a_im * step)
    inv = 1.0 / (a_re * a_re + a_im * a_im)
    nr, ni = lr - 1.0, li
    cr, ci = (nr * a_re + ni * a_im) * inv, (ni * a_re - nr * a_im) * inv
    return lr, li, cr * b_re - ci * b_im, cr * b_im + ci * b_re


def s5_disc_fwd(a_re, a_im, ls, b_re, b_im, *, name="s5_disc"):
    def body(ar, ai, l, br, bi, o0, o1, o2, o3):
        for o, val in zip((o0, o1, o2, o3), _s5_disc(ar[...], ai[...], l[...], br[...], bi[...])):
            o[...] = val

    sds = jax.ShapeDtypeStruct(a_re.shape, F32)
    return pl.pallas_call(body, name=name, out_shape=[sds] * 4)(a_re, a_im, ls, b_re, b_im)


def s5_disc_bwd(a_re, a_im, ls, b_re, b_im, cts, *, name="s5_disc_bwd"):
    def body(ar, ai, l, br, bi, c0, c1, c2, c3, dar, dai, dl, dbr, dbi):
        _, vjp = jax.vjp(_s5_disc, ar[...], ai[...], l[...], br[...], bi[...])
        g = vjp((c0[...], c1[...], c2[...], c3[...]))
        for o, val in zip((dar, dai, dl), g[:3]):
            o[...] = jnp.sum(val, axis=-1, keepdims=True)
        dbr[...] = g[3]
        dbi[...] = g[4]

    col = jax.ShapeDtypeStruct((a_re.shape[0], 1), F32)
    sds = jax.ShapeDtypeStruct(a_re.shape, F32)
    return pl.pallas_call(body, name=name, out_shape=[col, col, col, sds, sds])(a_re, a_im, ls, b_re, b_im, *cts)


def _gelu(y):
    return 0.5 * y * (1.0 + jnp.tanh(0.7978845608028654 * (y + 0.044715 * (y * y * y))))


def _s5_powers(lam, conj):
    lr, li = lam[:, :S5_N], lam[:, S5_N:]
    if conj:
        li = -li
    rows, pr, pi = [], lr, li
    for _ in range(SUB):
        rows.append((pr, pi))
        pr, pi = pr * lr - pi * li, pr * li + pi * lr
    return rows


def _s5_table(rows, reverse):
    ridx = lax.broadcasted_iota(jnp.int32, (SUB, S5_N), 0)
    tr = jnp.zeros((SUB, S5_N), F32)
    ti = jnp.zeros((SUB, S5_N), F32)
    for i in range(SUB):
        pr, pi = rows[SUB - 1 - i] if reverse else rows[i]
        tr = jnp.where(ridx == i, pr, tr)
        ti = jnp.where(ridx == i, pi, ti)
    return tr, ti


def _s5_scan(buf, lam, carry_ref, reverse):
    tc = buf.shape[0]
    nblk = tc // SUB
    rows = _s5_powers(lam, reverse)
    tr, ti = _s5_table(rows, reverse)
    ridx = lax.broadcasted_iota(jnp.int32, (SUB, S5_N), 0)
    steps = []
    for sft, (pr, pi) in ((1, rows[0]), (2, rows[1]), (4, rows[3])):
        keep = (ridx < SUB - sft) if reverse else (ridx >= sft)
        steps.append((SUB - sft if reverse else sft, jnp.where(keep, pr, 0.0), jnp.where(keep, pi, 0.0)))

    def block(j, carry):
        jj = (nblk - 1 - j) if reverse else j
        at = pl.ds(pl.multiple_of(jj * SUB, SUB), SUB)
        re, im = buf[at, :S5_N], buf[at, S5_N:]
        for rot, pr, pi in steps:
            sre, sim = pltpu.roll(re, rot, 0), pltpu.roll(im, rot, 0)
            re, im = re + pr * sre - pi * sim, im + pr * sim + pi * sre
        cr, ci = carry
        re, im = re + tr * cr - ti * ci, im + tr * ci + ti * cr
        buf[at, :S5_N] = re
        buf[at, S5_N:] = im
        edge = 0 if reverse else SUB - 1
        return re[edge : edge + 1, :], im[edge : edge + 1, :]

    c0 = (carry_ref[0:1, :S5_N], carry_ref[0:1, S5_N:])
    cr, ci = lax.fori_loop(0, nblk, block, c0)
    carry_ref[:, :S5_N] = jnp.broadcast_to(cr, (SUB, S5_N))
    carry_ref[:, S5_N:] = jnp.broadcast_to(ci, (SUB, S5_N))


def s5_fwd(proj, ucol, bd, cd, lam, dskip, wglu, bglu, *, name="s5_fwd"):
    s = proj.shape[0]
    tc = _pick(s, (S5_TC, 128, 64, 8))
    assert ucol % MIX_W == 0

    def body(u_ref, bd_ref, cd_ref, lam_ref, d_ref, w_ref, b_ref, y_ref, xs_ref, carry):
        @pl.when(pl.program_id(0) == 0)
        def _():
            carry[...] = jnp.zeros_like(carry)

        u = u_ref[...]
        xs_ref[...] = _dot(u, bd_ref[...])
        _s5_scan(xs_ref, lam_ref[...], carry, False)
        g = _gelu(_dot(xs_ref[...], cd_ref[...]) + d_ref[...] * u)
        y_ref[...] = g * _sigmoid(_dot(g, w_ref[...]) + b_ref[...])

    full = lambda shp: pl.BlockSpec(shp, lambda i: (0, 0))
    return pl.pallas_call(
        body,
        name=name,
        grid=(s // tc,),
        in_specs=[pl.BlockSpec((tc, MIX_W), lambda i: (i, ucol // MIX_W)), full((MIX_W, 2 * S5_N)), full((2 * S5_N, MIX_W)),
                  full((1, 2 * S5_N)), full((1, MIX_W)), full((MIX_W, MIX_W)), full((1, MIX_W))],
        out_specs=[pl.BlockSpec((tc, MIX_W), lambda i: (i, 0)), pl.BlockSpec((tc, 2 * S5_N), lambda i: (i, 0))],
        out_shape=[jax.ShapeDtypeStruct((s, MIX_W), F32), jax.ShapeDtypeStruct((s, 2 * S5_N), F32)],
        scratch_shapes=[pltpu.VMEM((SUB, 2 * S5_N), F32)],
        compiler_params=_cparams(("arbitrary",)),
    )(proj, bd, cd, lam, dskip, wglu, bglu)


def s5_bwd(proj, ucol, xs, bd, cd, lam, dskip, wglu, bglu, dya, *, name="s5_bwd"):
    s = proj.shape[0]
    tc = _pick(s, (S5_TC, 128, 64, 8))
    nch = s // tc
    per = tc // SUB

    def body(u_ref, xs_ref, xp_ref, bd_ref, cd_ref, lam_ref, d_ref, w_ref, b_ref, dya_ref,
             du_ref, adj_ref, g_ref, dpre_ref, dy_ref, dlam_ref, dd_ref, db_ref, buf, carry, lacc):
        i = pl.program_id(0)
        first = i == 0

        @pl.when(first)
        def _():
            carry[...] = jnp.zeros_like(carry)
            lacc[...] = jnp.zeros_like(lacc)

        u, x, dya_ = u_ref[...], xs_ref[...], dya_ref[...]
        y = _dot(x, cd_ref[...]) + d_ref[...] * u
        g, gelu_vjp = jax.vjp(_gelu, y)
        sg = _sigmoid(_dot(g, w_ref[...]) + b_ref[...])
        dpre = dya_ * g * sg * (1.0 - sg)
        (dy,) = gelu_vjp(dya_ * sg + _dot(dpre, w_ref[...], 1, 1))
        g_ref[...] = g.astype(BF16)
        dpre_ref[...] = dpre.astype(BF16)
        dy_ref[...] = dy.astype(BF16)
        db_part = jnp.sum(dpre, axis=0, keepdims=True)
        dd_part = jnp.sum(dy * u, axis=0, keepdims=True)
        buf[...] = _dot(dy, cd_ref[...], 1, 1)
        _s5_scan(buf, lam_ref[...], carry, True)
        a = buf[...]
        adj_ref[...] = a.astype(BF16)
        du_ref[...] = (dy * d_ref[...] + _dot(a, bd_ref[...], 1, 1)).astype(BF16)
        before = jnp.where(i == nch - 1, 0.0, xp_ref[SUB - 1 : SUB, :])
        ridx = lax.broadcasted_iota(jnp.int32, (tc, 2 * S5_N), 0)
        xprev = jnp.where(ridx == 0, before, pltpu.roll(x, 1, 0))
        ar, ai, xr, xi = a[:, :S5_N], a[:, S5_N:], xprev[:, :S5_N], xprev[:, S5_N:]
        lacc[:, :S5_N] += jnp.sum((ar * xr + ai * xi).reshape(per, SUB, S5_N), axis=0)
        lacc[:, S5_N:] += jnp.sum((ai * xr - ar * xi).reshape(per, SUB, S5_N), axis=0)
        _accum(db_ref, db_part, first)
        _accum(dd_ref, dd_part, first)

        @pl.when(i == nch - 1)
        def _():
            dlam_ref[...] = jnp.sum(lacc[...], axis=0, keepdims=True)

    rev = lambda i: nch - 1 - i
    full = lambda shp: pl.BlockSpec(shp, lambda i: (0, 0))
    rows = lambda w: pl.BlockSpec((tc, w), lambda i: (rev(i), 0))
    hb = jax.ShapeDtypeStruct((s, MIX_W), BF16)
    return pl.pallas_call(
        body,
        name=name,
        grid=(nch,),
        in_specs=[pl.BlockSpec((tc, MIX_W), lambda i: (rev(i), ucol // MIX_W)), rows(2 * S5_N),
                  pl.BlockSpec((SUB, 2 * S5_N), lambda i: (jnp.maximum(rev(i) * per - 1, 0), 0)),
                  full((MIX_W, 2 * S5_N)), full((2 * S5_N, MIX_W)), full((1, 2 * S5_N)), full((1, MIX_W)),
                  full((MIX_W, MIX_W)), full((1, MIX_W)), rows(MIX_W)],
        out_specs=[rows(MIX_W), rows(2 * S5_N), rows(MIX_W), rows(MIX_W), rows(MIX_W),
                   full((1, 2 * S5_N)), full((1, MIX_W)), full((1, MIX_W))],
        out_shape=[hb, jax.ShapeDtypeStruct((s, 2 * S5_N), BF16), hb, hb, hb,
                   jax.ShapeDtypeStruct((1, 2 * S5_N), F32), jax.ShapeDtypeStruct((1, MIX_W), F32),
                   jax.ShapeDtypeStruct((1, MIX_W), F32)],
        scratch_shapes=[pltpu.VMEM((tc, 2 * S5_N), F32), pltpu.VMEM((SUB, 2 * S5_N), F32), pltpu.VMEM((SUB, 2 * S5_N), F32)],
        compiler_params=_cparams(("arbitrary",)),
    )(proj, xs, xs, bd, cd, lam, dskip, wglu, bglu, dya)


def _bcast16(a):
    return jnp.broadcast_to(a.reshape(S5_N, 1), (S5_N, S5_C))


def _blockdiag(blocks):
    g, r, c = blocks.shape
    eye = jnp.eye(g, dtype=blocks.dtype)
    return (eye[:, None, :, None] * blocks[:, :, None, :]).reshape(g * r, g * c)


def _blockdiag_extract(dense, r, c):
    g = dense.shape[0] // r
    return jnp.einsum("grgc->grc", dense.reshape(g, r, g, c))


def s5_prepare(a_re, a_im, log_step, b_re, b_im, c_re, c_im):
    disc_in = (_bcast16(a_re), _bcast16(a_im), _bcast16(jnp.broadcast_to(log_step[:, None], (S5_G, S5_P))),
               b_re.reshape(S5_N, S5_C), b_im.reshape(S5_N, S5_C))
    lr, li, bbr, bbi = s5_disc_fwd(*disc_in)
    lam = jnp.concatenate([lr[:, 0], li[:, 0]]).reshape(1, 2 * S5_N)
    to_blocks = lambda t: _blockdiag(t.reshape(S5_G, S5_P, S5_C).transpose(0, 2, 1))
    bd = jnp.concatenate([to_blocks(bbr), to_blocks(bbi)], axis=1).astype(BF16)
    cd = jnp.concatenate([_blockdiag(c_re.transpose(0, 2, 1)), -_blockdiag(c_im.transpose(0, 2, 1))], axis=0).astype(BF16)
    return disc_in, lam, bd, cd


def s5_param_grads(disc_in, dlam, dbd, dcd):
    first_col = lambda v: jnp.pad(v.reshape(S5_N, 1), ((0, 0), (0, S5_C - 1)))
    from_blocks = lambda t: _blockdiag_extract(t, S5_C, S5_P).transpose(0, 2, 1).reshape(S5_N, S5_C)
    cts = (first_col(dlam[0, :S5_N]), first_col(dlam[0, S5_N:]), from_blocks(dbd[:, :S5_N]), from_blocks(dbd[:, S5_N:]))
    dar, dai, dls, dbr, dbi = s5_disc_bwd(*disc_in, cts)
    dcr = _blockdiag_extract(dcd[:S5_N], S5_P, S5_C).transpose(0, 2, 1)
    dci = -_blockdiag_extract(dcd[S5_N:], S5_P, S5_C).transpose(0, 2, 1)
    return (dar.reshape(S5_G, S5_P), dai.reshape(S5_G, S5_P), dls.reshape(S5_G, S5_P).sum(axis=1),
            dbr.reshape(S5_G, S5_P, S5_C), dbi.reshape(S5_G, S5_P, S5_C), dcr, dci)


N_DEV = 8
MESH_ID = pl.DeviceIdType.MESH
ANY = pl.BlockSpec(memory_space=pl.ANY)


def _me():
    return lax.axis_index("x"), lax.axis_index("y"), lax.axis_index("c")


def all_gather(blocks, *, by_core=False, name="all_gather"):
    na = len(blocks)
    shapes = [b.shape[1:] if by_core else b.shape for b in blocks]

    def body(*refs):
        ins, outs = refs[:na], refs[na : 2 * na]
        send_sems, recv_sems = refs[2 * na :]
        x, y, c = _me()
        me, sibling = (x, y, c), (x, y, 1 - c)
        xn, yn, diag = (1 - x, y, c), (x, 1 - y, c), (1 - x, 1 - y, c)
        slot = lambda p: 4 * p[0] + 2 * p[1] + p[2]

        def copy(a, k, block, to, half=None, src=None):
            dst = outs[a].at[slot(block)]
            if half is not None:
                rows = shapes[a][0] // 2
                dst = dst.at[pl.ds(half * rows, rows)]
            return pltpu.make_async_remote_copy(src_ref=dst if src is None else src, dst_ref=dst, send_sem=send_sems.at[a * 9 + k],
                                                recv_sem=recv_sems.at[a * 9 + k], device_id=to, device_id_type=MESH_ID)

        mine = [ins[a].at[c] if by_core else ins[a] for a in range(na)]
        sends = []

        def go(cp):
            cp.start()
            sends.append(cp)

        for a in range(na):
            go(copy(a, 0, me, sibling, src=mine[a]))
            go(copy(a, 1, me, xn, src=mine[a]))
            go(copy(a, 2, me, yn, src=mine[a]))
        for a in range(na):
            copy(a, 1, xn, me).wait_recv()
            go(copy(a, 3, xn, yn, half=0))
            go(copy(a, 5, xn, sibling))
            copy(a, 2, yn, me).wait_recv()
            go(copy(a, 4, yn, xn, half=1))
            go(copy(a, 6, yn, sibling))
        for a in range(na):
            copy(a, 3, diag, me, half=0).wait_recv()
            go(copy(a, 7, diag, sibling, half=0))
            copy(a, 4, diag, me, half=1).wait_recv()
            go(copy(a, 8, diag, sibling, half=1))
        for a in range(na):
            copy(a, 0, sibling, me).wait_recv()
            copy(a, 5, (1 - x, y, 1 - c), me).wait_recv()
            copy(a, 6, (x, 1 - y, 1 - c), me).wait_recv()
            copy(a, 7, (1 - x, 1 - y, 1 - c), me, half=0).wait_recv()
            copy(a, 8, (1 - x, 1 - y, 1 - c), me, half=1).wait_recv()
        for cp in sends:
            cp.wait_send()

    assert all(shp[0] % (64 // b.dtype.itemsize) == 0 for shp, b in zip(shapes, blocks)), shapes
    outs = pl.pallas_call(
        body,
        name=name,
        in_specs=[ANY] * na,
        out_specs=[ANY] * na,
        out_shape=[jax.ShapeDtypeStruct((N_DEV, *shp), b.dtype) for shp, b in zip(shapes, blocks)],
        scratch_shapes=[pltpu.SemaphoreType.DMA((na * 9,)), pltpu.SemaphoreType.DMA((na * 9,))],
    )(*blocks)
    x, y, c = _me()
    own = [lax.dynamic_index_in_dim(b, c, 0, keepdims=False) if by_core else b for b in blocks]
    return [lax.dynamic_update_index_in_dim(o, b, 4 * x + 2 * y + c, 0) for o, b in zip(outs, own)]


D2D_PIECES = 8


def _slabs(rows):
    n = D2D_PIECES if rows % (16 * D2D_PIECES) == 0 else 1
    return [pl.ds(i * (rows // n), rows // n) for i in range(n)]


def sibling_swap(arrs, *, name="sibling_swap"):
    na = len(arrs)
    pieces = [[(j, sl) for j in range(a.shape[0]) for sl in (_slabs(a.shape[1]) if a.shape[0] == 1 else _halves(a.shape[1]))]
              for a in arrs]
    base = np.cumsum([0] + [len(p) for p in pieces])

    def body(*refs):
        ins, outs = refs[:na], refs[na : 2 * na]
        send_sems, recv_sems = refs[2 * na :]
        x, y, c = _me()
        sends, recvs = [], []
        for a in range(na):
            for i, (j, sl) in enumerate(pieces[a]):
                k = int(base[a]) + i
                cp = pltpu.make_async_remote_copy(src_ref=ins[a].at[j, sl], dst_ref=outs[a].at[j, sl], send_sem=send_sems.at[k],
                                                  recv_sem=recv_sems.at[k], device_id=(x, y, 1 - c), device_id_type=MESH_ID)
                cp.start()
                sends.append(cp)
        for cp in sends:
            cp.wait_recv()
        for cp in sends:
            cp.wait_send()

    nsem = int(base[-1])
    return pl.pallas_call(
        body,
        name=name,
        in_specs=[ANY] * na,
        out_specs=[ANY] * na,
        out_shape=[jax.ShapeDtypeStruct(a.shape, a.dtype) for a in arrs],
        scratch_shapes=[pltpu.SemaphoreType.DMA((nsem,)), pltpu.SemaphoreType.DMA((nsem,))],
    )(*arrs)


def _halves(rows):
    return [pl.ds(0, rows // 2), pl.ds(rows // 2, rows // 2)] if rows % 32 == 0 else [pl.ds(0, rows)]


def chip_exchange(arrs, *, name="chip_exchange"):
    na = len(arrs)

    def body(*refs):
        ins, outs = refs[:na], refs[na : 2 * na]
        send_sems, recv_sems = refs[2 * na :]
        x, y, c = _me()
        my = 2 * x + y
        peers = []
        for k in range(1, N_CHIPS):
            px, py = (x + ((k >> 1) & 1)) % 2, (y + (k & 1)) % 2
            peers.append(((px, py, c), 2 * px + py))
        sends = []
        for a in range(na):
            for k, (peer, pidx) in enumerate(peers):
                cp = pltpu.make_async_remote_copy(src_ref=ins[a].at[pidx], dst_ref=outs[a].at[my], send_sem=send_sems.at[a * 3 + k],
                                                  recv_sem=recv_sems.at[a * 3 + k], device_id=peer, device_id_type=MESH_ID)
                cp.start()
                sends.append(cp)
        for a in range(na):
            for k, (peer, pidx) in enumerate(peers):
                pltpu.make_async_remote_copy(src_ref=ins[a].at[my], dst_ref=outs[a].at[pidx], send_sem=send_sems.at[a * 3 + k],
                                             recv_sem=recv_sems.at[a * 3 + k], device_id=peer, device_id_type=MESH_ID).wait_recv()
        for cp in sends:
            cp.wait_send()

    outs = pl.pallas_call(
        body,
        name=name,
        in_specs=[ANY] * na,
        out_specs=[ANY] * na,
        out_shape=[jax.ShapeDtypeStruct(a.shape, a.dtype) for a in arrs],
        scratch_shapes=[pltpu.SemaphoreType.DMA((na * 3,)), pltpu.SemaphoreType.DMA((na * 3,))],
    )(*arrs)
    chip = 2 * lax.axis_index("x") + lax.axis_index("y")
    return [lax.dynamic_update_index_in_dim(o, lax.dynamic_index_in_dim(a, chip, 0, keepdims=False), chip, 0)
            for o, a in zip(outs, arrs)]


def add_pair(a, b, *, name="add_pair"):
    n, r, c = a.shape
    tm = _row_tile(r, n * c * 2, 2 << 20)

    def body(a_ref, b_ref, o_ref):
        o_ref[...] = (a_ref[...].astype(F32) + b_ref[...].astype(F32)).astype(o_ref.dtype)

    blk = pl.BlockSpec((n, tm, c), lambda i: (0, i, 0))
    return pl.pallas_call(
        body, name=name, grid=(r // tm,), in_specs=[blk, blk], out_specs=blk, out_shape=jax.ShapeDtypeStruct(a.shape, a.dtype),
        compiler_params=_cparams(("arbitrary",)),
    )(a, b)


def _row_tile(rows, row_bytes, budget):
    best = None
    for t in range(8, rows + 1, 8):
        if rows % t == 0 and t * row_bytes <= budget:
            best = t
    return best or rows


def sum_blocks(a, *, name="sum_blocks"):
    n, r, c = a.shape
    tm = _row_tile(r, n * c * a.dtype.itemsize, 4 << 20)

    def body(a_ref, o_ref):
        acc = a_ref[0].astype(F32)
        for k in range(1, n):
            acc = acc + a_ref[k].astype(F32)
        o_ref[...] = acc

    return pl.pallas_call(
        body,
        name=name,
        grid=(r // tm,),
        in_specs=[pl.BlockSpec((n, tm, c), lambda i: (0, i, 0))],
        out_specs=pl.BlockSpec((tm, c), lambda i: (i, 0)),
        out_shape=jax.ShapeDtypeStruct((r, c), F32),
        compiler_params=_cparams(("arbitrary",)),
    )(a)


ADAM_LR = 0.001
ADAM_B1 = 0.9
ADAM_B2 = 0.999
ADAM_EPS = 1e-08
ADAM_WD = 0.01
ADAM_STEP = 10


def adamw(w, g, m, v, *, name="adamw"):
    r, c = w.shape[-2:]
    tm = _row_tile(r, c * 4, 1 << 20)
    if w.ndim == 3:
        blk = pl.BlockSpec((None, tm, c), lambda l, i: (l, i, 0))
        grid = (w.shape[0], r // tm)
    else:
        blk = pl.BlockSpec((tm, c), lambda i: (i, 0))
        grid = (r // tm,)

    def body(w_ref, g_ref, m_ref, v_ref, d_ref, nm_ref, nv_ref):
        g_ = g_ref[...]
        m_ = ADAM_B1 * m_ref[...] + (1.0 - ADAM_B1) * g_
        v_ = ADAM_B2 * v_ref[...] + (1.0 - ADAM_B2) * (g_ * g_)
        m_hat = m_ / (1.0 - ADAM_B1**ADAM_STEP)
        v_hat = v_ / (1.0 - ADAM_B2**ADAM_STEP)
        d_ref[...] = -ADAM_LR * (m_hat / (jnp.sqrt(v_hat) + ADAM_EPS) + ADAM_WD * w_ref[...])
        nm_ref[...] = m_
        nv_ref[...] = v_

    sds = jax.ShapeDtypeStruct(w.shape, F32)
    return pl.pallas_call(
        body, name=name, grid=grid, in_specs=[blk] * 4, out_specs=[blk] * 3, out_shape=[sds] * 3,
        compiler_params=_cparams(("arbitrary",) * len(grid)),
    )(w, g, m, v)


WEIGHTS = ["g_mix", "w_in", "s5_a_re", "s5_a_im", "s5_log_step", "s5_b_re", "s5_b_im", "s5_c_re", "s5_c_im", "s5_d", "w_glu",
           "b_glu", "w_gla_gate", "b_gla_gate", "g_gla_out", "w_branch", "w_out", "g_mem", "g_cross", "w_xq", "w_xkv", "w_xo",
           "g_mlp", "w_up", "w_down", "g_final"]
SHARDED = {"w_in": 1, "w_glu": 0, "w_branch": 2, "w_out": 0, "w_xq": 0, "w_xkv": 1, "w_xo": 0, "w_up": 1, "w_down": 0}
DEPTH = 2
N_CHIPS = 4
D_IN = 9744
C_U, C_QG, C_KG, C_VG, C_LR, C_RG, C_QA, C_KA, C_VA, C_GATE = 0, 512, 768, 1024, 1536, 1552, 2064, 3600, 5136, 6672
A_U, A_QG, A_KG, A_VG, A_RG, A_Q0, A_LR, A_W = 0, 512, 768, 1024, 1536, 2048, 3584, 4096
GLA_COLS = (A_QG, A_KG, A_VG, A_LR, A_RG)


def _split_w_in(w):
    att = lambda g: [w[:, c + MIX_W * g : c + MIX_W * (g + 1)] for c in (C_QA, C_KA, C_VA)]
    a = jnp.concatenate([w[:, C_U:C_QG], w[:, C_QG:C_KG], w[:, C_KG:C_VG], w[:, C_VG:C_LR], w[:, C_RG:C_QA],
                         *att(0), w[:, C_LR:C_RG], jnp.zeros((w.shape[0], A_W - A_LR - 16), w.dtype)], axis=1)
    return a, jnp.concatenate(att(1), axis=1), jnp.concatenate(att(2), axis=1), w[:, C_GATE:D_IN]


def _join_w_in(a, b, c, gates):
    att = lambda k: [a[:, A_Q0 + MIX_W * k : A_Q0 + MIX_W * (k + 1)], b[:, MIX_W * k : MIX_W * (k + 1)], c[:, MIX_W * k : MIX_W * (k + 1)]]
    return jnp.concatenate([a[:, A_U:A_QG], a[:, A_QG:A_KG], a[:, A_KG:A_VG], a[:, A_VG:A_RG], a[:, A_LR : A_LR + 16], a[:, A_RG:A_Q0],
                            *att(0), *att(1), *att(2), gates], axis=1)


def _by_residue(a, d):
    s = a.shape[0]
    return a if d == 1 else a.reshape(s // d, d, -1).transpose(1, 0, 2).reshape(s, -1)


def _in_order(a, d):
    s = a.shape[0]
    return a if d == 1 else a.reshape(d, s // d, -1).transpose(1, 0, 2).reshape(s, -1)


def _pack(arrs):
    flat = jnp.concatenate([a.reshape(-1) for a in arrs])
    n = flat.shape[0]
    rows = -(-n // (256 * 128)) * 256
    return jnp.pad(flat, (0, rows * 128 - n)).reshape(rows, 128)


def _unpack(packed, like):
    flat, out, o = packed.reshape(-1), [], 0
    for a in like:
        n = math.prod(a.shape)
        out.append(flat[o : o + n].reshape(a.shape))
        o += n
    return out


def _layer_fwd(x0, p, kv):
    h = rmsnorm(x0, p["g_mix"], name="mix_norm")
    hs = [h, _by_residue(h, 4), _by_residue(h, 16)]
    proj = [matmul(hs[g], p["w_in_seg"][g], out_dtype=F32 if g == 0 else BF16, name="in_proj") for g in range(3)]
    gates = matmul(h, p["w_in_seg"][3], out_dtype=BF16, name="in_proj")
    ya, xs = s5_fwd(proj[0], A_U, p["bd"], p["cd"], p["lam"], p["s5_d"], p["w_glu"], p["b_glu"])
    yb, gst = gla_fwd(proj[0], GLA_COLS, p["w_gate"], p["b_gla_gate"], p["g_gla_out"])
    att = [attn_fwd(proj[g], A_Q0 if g == 0 else 0, g) for g in range(3)]
    outs = [_in_order(att[g][0], ATT_GROUPS[g][1]) for g in range(3)]
    lses = [_in_order(att[g][1], ATT_GROUPS[g][1]) for g in range(3)]
    x1, merged, yc = merge_fwd(x0, gates, ya, yb, outs, lses, p["w_branch"], p["w_out"])
    x2 = cross_fwd(x1, p["g_cross"], p["w_xq"], p["w_xo"], kv)
    x3 = mlp_fwd(x2, p["g_mlp"], p["w_up"], p["w_down"])
    saved = dict(x0=x0, x1=x1, x2=x2, hs=hs, proj=proj, gates=gates, ya=ya, xs=xs, yb=yb, gst=gst, att=att, outs=outs,
                 lses=lses, merged=merged, yc=yc)
    return x3, saved


def _layer_bwd(dx3, p, kv, memn, sv):
    g = {}
    tn = lambda a, b, **kw: matmul(a, b, ta=True, out_dtype=BF16, name="wgrad", **kw)
    dx2, g["g_mlp"], h3, dup, act, dx3b = mlp_bwd(sv["x2"], p["g_mlp"], p["w_up"], p["w_down"], dx3)
    g["w_up"], g["w_down"] = tn(h3, dup), tn(act, dx3b)
    dx1, g["g_cross"], h2, dq, o, dkv = cross_bwd(sv["x1"], p["g_cross"], p["w_xq"], p["w_xo"], kv, dx2)
    g["w_xq"], g["w_xo"], g["w_xkv"] = tn(h2, dq), tn(o, dx2), tn(memn, dkv)
    dmemn = matmul(dkv, p["w_xkv"], tb=True, name="dmem")
    proj = sv["proj"]
    r = merge_bwd(dx1, sv["gates"], sv["ya"], sv["yb"], sv["yc"], sv["outs"], sv["lses"], p["w_branch"], p["w_out"])
    dgl, dz, dya, dyb, douts, dlses = r[0], r[1:4], r[4], r[5], r[6:9], r[9:12]
    g["w_branch"] = jnp.stack([tn(y, dz[n]) for n, y in enumerate((sv["ya"], sv["yb"], sv["yc"]))])
    g["w_out"] = tn(sv["merged"], dx1)
    datt = []
    for k in range(3):
        dil = ATT_GROUPS[k][1]
        datt.append(attn_bwd(proj[k], A_Q0 if k == 0 else 0, k, sv["att"][k][0], sv["att"][k][1],
                             _by_residue(douts[k], dil), _by_residue(dlses[k], dil)))
    du, adj, gg, dpre, dy, dlam, g["s5_d"], g["b_glu"] = s5_bwd(proj[0], A_U, sv["xs"], p["bd"], p["cd"], p["lam"], p["s5_d"],
                                                                 p["w_glu"], p["b_glu"], dya)
    dbd = matmul(proj[0], adj, ta=True, a_col=(A_U, MIX_W), name="s5_dbd")
    dcd = matmul(sv["xs"], dy, ta=True, name="s5_dcd")
    g["w_glu"] = tn(gg, dpre)
    (g["s5_a_re"], g["s5_a_im"], g["s5_log_step"], g["s5_b_re"], g["s5_b_im"], g["s5_c_re"],
     g["s5_c_im"]) = s5_param_grads(p["disc_in"], dlam, dbd, dcd)
    dqg, dkg, dvg, dlr, drg, dwg, g["b_gla_gate"], g["g_gla_out"] = gla_bwd(proj[0], GLA_COLS, p["w_gate"], p["b_gla_gate"],
                                                                             p["g_gla_out"], sv["gst"], dyb)
    g["w_gla_gate"] = dwg[:GLA_GATE_RANK]
    s = dx3.shape[0]
    dproj = [jnp.concatenate([du, dqg, dkg, dvg, drg, *datt[0], dlr, jnp.zeros((s, A_W - A_LR - LR_PAD), BF16)], axis=1),
             jnp.concatenate(datt[1], axis=1), jnp.concatenate(datt[2], axis=1)]
    g["w_in"] = _join_w_in(*[tn(sv["hs"][k], dproj[k]) for k in range(3)], tn(sv["hs"][0], dgl))
    dhs = [_in_order(matmul(dproj[k], p["w_in_seg"][k], tb=True, name="in_proj_bwd"), ATT_GROUPS[k][1]) for k in range(3)]
    dhs.append(matmul(dgl, p["w_in_seg"][3], tb=True, name="in_proj_bwd"))
    dx0, g["g_mix"] = rmsnorm_bwd(sv["x0"], p["g_mix"], dhs, dx1, name="mix_norm_bwd")
    return dx0, g, dmemn


GLA_GATE_RANK = 16


def kernel(x, mem, *rest):
    nw = len(WEIGHTS)
    w = dict(zip(WEIGHTS, rest[:nw]))
    target = rest[nw]
    m = dict(zip(WEIGHTS, rest[nw + 1 : 2 * nw + 1]))
    v = dict(zip(WEIGHTS, rest[2 * nw + 1 : 3 * nw + 1]))
    x0, memx, target = x[0], mem[0], target[0]
    chip = 2 * lax.axis_index("x") + lax.axis_index("y")

    big = list(SHARDED)
    gate_pad = jnp.pad(w["w_gla_gate"], ((0, 0), (0, 0), (0, LR_PAD - w["w_gla_gate"].shape[2])))
    flat2 = lambda t: t.reshape(DEPTH, -1, t.shape[-1])
    gathered = all_gather([flat2(w[n].astype(BF16)) for n in big] + [gate_pad], by_core=True, name="gather_weights")
    gathered = [t.reshape(N_CHIPS, DEPTH, *s.shape[1:]) for t, s in zip(gathered, [w[n] for n in big] + [gate_pad])]
    full = {n: [jnp.concatenate([t[j, l] for j in range(N_CHIPS)], axis=SHARDED[n]) for l in range(DEPTH)]
            for n, t in zip(big, gathered)}
    gate_full = [jnp.concatenate([gathered[-1][j, l][:, : w["w_gla_gate"].shape[2]] for j in range(N_CHIPS)], axis=1)
                 for l in range(DEPTH)]

    memn = rmsnorm(memx, w["g_mem"], name="mem_norm")
    params, kvs = [], []
    for l in range(DEPTH):
        p = {n: full[n][l] for n in big if n != "w_in"}
        p["w_in_seg"] = _split_w_in(full["w_in"][l])
        p["w_gate"] = jnp.pad(gate_full[l], ((0, LR_PAD - GLA_GATE_RANK), (0, 0))).astype(BF16)
        for n in ("g_mix", "g_cross", "g_mlp"):
            p[n] = w[n][l]
        for n in ("s5_d", "b_glu", "b_gla_gate", "g_gla_out"):
            p[n] = w[n][l].reshape(1, -1)
        p["disc_in"], p["lam"], p["bd"], p["cd"] = s5_prepare(*[w[n][l] for n in WEIGHTS[2:9]])
        params.append(p)
        kvs.append(matmul(memn, p["w_xkv"], out_dtype=BF16, name="mem_kv"))

    xl, saved = x0, []
    for l in range(DEPTH):
        xl, sv = _layer_fwd(xl, params[l], kvs[l])
        saved.append(sv)
    loss8, dx, dg_final = loss_head(xl, w["g_final"], target)
    grads, dmem = [None] * DEPTH, []
    for l in reversed(range(DEPTH)):
        dx, grads[l], dm = _layer_bwd(dx, params[l], kvs[l], memn, saved[l])
        dmem.append(dm)
    _, dg_mem = rmsnorm_bwd(memx, w["g_mem"], dmem, jnp.zeros_like(memx), name="mem_norm_bwd")

    core = lax.axis_index("c")

    def halves(n):
        ax = SHARDED[n]
        keep, send = [], []
        for j in range(N_CHIPS):
            p0, p1 = [jnp.split(grads[l][n], N_CHIPS, axis=ax)[j] for l in range(DEPTH)]
            keep.append(jnp.where(core == 0, p0, p1))
            send.append(jnp.where(core == 0, p1, p0))
        flat = lambda ps: jnp.stack(ps).reshape(N_CHIPS, -1, ps[0].shape[-1])
        return flat(keep), flat(send)

    kept, sent = zip(*[halves(n) for n in big])
    theirs = sibling_swap(list(sent), name="swap_layers")
    chip_sums = [add_pair(a, b, name="add_cores") for a, b in zip(kept, theirs)]
    landed = chip_exchange(chip_sums, name="exchange_grads")
    reduced = [sum_blocks(t, name="sum_grads") for t in landed]
    other = sibling_swap([t[None] for t in reduced], name="pair_layers")
    out_g = {}
    for n, mine, theirs in zip(big, reduced, other):
        both = jnp.where(core == 0, jnp.stack([mine, theirs[0]]), jnp.stack([theirs[0], mine]))
        out_g[n] = both.reshape(w[n].shape)

    small = [n for n in WEIGHTS if n not in SHARDED]
    local_small = []
    for n in small:
        if n == "g_mem":
            local_small.append(dg_mem.reshape(w[n].shape))
        elif n == "g_final":
            local_small.append(dg_final.reshape(w[n].shape))
        elif n == "w_gla_gate":
            local_small.append(jnp.stack([grads[l][n] for l in range(DEPTH)]))
        else:
            local_small.append(jnp.stack([grads[l][n].reshape(w[n].shape[1:]) for l in range(DEPTH)]))
    packed = _pack(local_small + [loss8[0, :1]])
    (every,) = all_gather([packed], name="gather_small")
    summed = _unpack(sum_blocks(every, name="sum_small"), local_small + [loss8[0, :1]])
    loss = summed[-1].reshape(())
    for n, t in zip(small, summed[:-1]):
        if n == "w_gla_gate":
            t = lax.dynamic_slice_in_dim(t, chip * w[n].shape[2], w[n].shape[2], axis=2)
        out_g[n] = t

    delta, new_m, new_v = {}, {}, {}
    for n in big:
        as3d = lambda t: t.reshape(DEPTH, -1, t.shape[-1])
        r = [t.reshape(w[n].shape) for t in adamw(as3d(w[n]), as3d(out_g[n]), as3d(m[n]), as3d(v[n]), name="adamw")]
        delta[n], new_m[n], new_v[n] = r
    like = [w[n] for n in small]
    r = adamw(_pack(like), _pack([out_g[n] for n in small]), _pack([m[n] for n in small]), _pack([v[n] for n in small]),
              name="adamw_small")
    for d, t in zip((delta, new_m, new_v), r):
        d.update(zip(small, _unpack(t, like)))
    return (loss, dx[None], *[out_g[n] for n in WEIGHTS], *[delta[n] for n in WEIGHTS], *[new_m[n] for n in WEIGHTS],
            *[new_v[n] for n in WEIGHTS])
```

```python
import functools
import math

import jax
import jax.numpy as jnp
import numpy as np
from jax import lax
from jax.experimental import pallas as pl
from jax.experimental.pallas import tpu as pltpu

F32 = jnp.float32
BF16 = jnp.bfloat16

VMEM_LIMIT_BYTES = 56 * 1024 * 1024
MATMUL_VMEM_BYTES = 36 * 1024 * 1024


def _cparams(sem):
    return pltpu.CompilerParams(dimension_semantics=sem, vmem_limit_bytes=VMEM_LIMIT_BYTES)


def _dot(a, b, ca=1, cb=0):
    return lax.dot_general(a.astype(BF16), b.astype(BF16), (((ca,), (cb,)), ((), ())), preferred_element_type=F32)


def _pick(n, prefs):
    for p in prefs:
        if n % p == 0:
            return p
    return n


def matmul(a, b, *, ta=False, tb=False, out_dtype=F32, name="mm", a_col=None, b_col=None):
    a_off, a_w = a_col if a_col is not None else (0, a.shape[1])
    b_off, b_w = b_col if b_col is not None else (0, b.shape[1])
    if ta:
        kk, m = a.shape[0], a_w
    else:
        m, kk = a.shape[0], a_w
    if tb:
        n, kb = b.shape[0], b_w
    else:
        kb, n = b.shape[0], b_w
    assert kk == kb, (a.shape, b.shape, ta, tb)
    tm = _pick(m, (512, 256, 128))
    tn = _pick(n, (1024, 512, 256, 128))
    sa, sb, so = a.dtype.itemsize, b.dtype.itemsize, jnp.dtype(out_dtype).itemsize
    fits = lambda t: 2 * (tm * t * sa + t * tn * sb) + tm * tn * (4 + 2 * so) <= MATMUL_VMEM_BYTES
    tk = max([t for t in range(128, kk + 1, 128) if kk % t == 0 and fits(t)] or [_pick(kk, (128,))])
    nk = kk // tk
    a_bytes, b_bytes = m * kk * sa, kk * n * sb
    n_outer = nk == 1 and b_bytes + a_bytes * (n // tn) < a_bytes + b_bytes * (m // tm)
    ij = (lambda g0, g1: (g1, g0)) if n_outer else (lambda g0, g1: (g0, g1))
    if ta:
        assert a_off % tm == 0
        a_spec = pl.BlockSpec((tk, tm), lambda g0, g1, k, o=a_off // tm: (k, ij(g0, g1)[0] + o))
    else:
        assert a_off % tk == 0
        a_spec = pl.BlockSpec((tm, tk), lambda g0, g1, k, o=a_off // tk: (ij(g0, g1)[0], k + o))
    if tb:
        assert b_off % tk == 0
        b_spec = pl.BlockSpec((tn, tk), lambda g0, g1, k, o=b_off // tk: (ij(g0, g1)[1], k + o))
    else:
        assert b_off % tn == 0
        b_spec = pl.BlockSpec((tk, tn), lambda g0, g1, k, o=b_off // tn: (k, ij(g0, g1)[1] + o))

    def body(a_ref, b_ref, o_ref, *acc):
        k = pl.program_id(2)
        p = _dot(a_ref[...], b_ref[...], 0 if ta else 1, 1 if tb else 0)
        if nk == 1:
            o_ref[...] = p.astype(o_ref.dtype)
            return
        (acc_ref,) = acc

        @pl.when(k == 0)
        def _():
            acc_ref[...] = p

        @pl.when(jnp.logical_and(k > 0, k < nk - 1))
        def _():
            acc_ref[...] += p

        @pl.when(k == nk - 1)
        def _():
            o_ref[...] = (acc_ref[...] + p).astype(o_ref.dtype)

    return pl.pallas_call(
        body,
        name=name,
        grid=(n // tn, m // tm, nk) if n_outer else (m // tm, n // tn, nk),
        in_specs=[a_spec, b_spec],
        out_specs=pl.BlockSpec((tm, tn), lambda g0, g1, k: ij(g0, g1)),
        out_shape=jax.ShapeDtypeStruct((m, n), out_dtype),
        scratch_shapes=[pltpu.VMEM((tm, tn), F32)] if nk > 1 else [],
        compiler_params=_cparams(("parallel", "parallel", "arbitrary")),
    )(a, b)


RMS_EPS = 1e-6
ROW_TILE = 512


def _rms_fwd(x, g):
    r = lax.rsqrt(jnp.mean(x * x, axis=-1, keepdims=True) + RMS_EPS)
    return x * r * g


def _rms_bwd(x, g, dh):
    r = lax.rsqrt(jnp.mean(x * x, axis=-1, keepdims=True) + RMS_EPS)
    xh = x * r
    dg = jnp.sum(dh * xh, axis=0, keepdims=True)
    dxh = dh * g
    dx = r * (dxh - xh * jnp.mean(dxh * xh, axis=-1, keepdims=True))
    return dx, dg


def _accum(ref, val, first):
    @pl.when(first)
    def _():
        ref[...] = val

    @pl.when(jnp.logical_not(first))
    def _():
        ref[...] += val


def rmsnorm(x, g, *, out_dtype=BF16, name="rmsnorm"):
    s, d = x.shape
    tm = _pick(s, (ROW_TILE, 256, 128, 8))

    def body(x_ref, g_ref, o_ref):
        o_ref[...] = _rms_fwd(x_ref[...], g_ref[...]).astype(o_ref.dtype)

    return pl.pallas_call(
        body,
        name=name,
        grid=(s // tm,),
        in_specs=[pl.BlockSpec((tm, d), lambda i: (i, 0)), pl.BlockSpec((1, d), lambda i: (0, 0))],
        out_specs=pl.BlockSpec((tm, d), lambda i: (i, 0)),
        out_shape=jax.ShapeDtypeStruct((s, d), out_dtype),
        compiler_params=_cparams(("arbitrary",)),
    )(x, g.reshape(1, d))


def rmsnorm_bwd(x, g, dhs, dres, *, name="rmsnorm_bwd"):
    s, d = x.shape
    tm = _pick(s, (ROW_TILE, 256, 128, 8))
    n = len(dhs)

    def body(x_ref, g_ref, *refs):
        dh_refs, dres_ref, dx_ref, dg_ref = refs[:n], refs[n], refs[n + 1], refs[n + 2]
        dh = dh_refs[0][...].astype(F32)
        for r in dh_refs[1:]:
            dh = dh + r[...].astype(F32)
        dx, dg = _rms_bwd(x_ref[...], g_ref[...], dh)
        dx_ref[...] = dres_ref[...] + dx
        _accum(dg_ref, dg, pl.program_id(0) == 0)

    row = pl.BlockSpec((tm, d), lambda i: (i, 0))
    vec = pl.BlockSpec((1, d), lambda i: (0, 0))
    return pl.pallas_call(
        body,
        name=name,
        grid=(s // tm,),
        in_specs=[row, vec] + [row] * n + [row],
        out_specs=[row, vec],
        out_shape=[jax.ShapeDtypeStruct((s, d), F32), jax.ShapeDtypeStruct((1, d), F32)],
        compiler_params=_cparams(("arbitrary",)),
    )(x, g.reshape(1, d), *dhs, dres)


def loss_head(x, g, target, *, name="loss_head"):
    s, d = x.shape
    tm = _pick(s, (ROW_TILE, 256, 128, 8))

    def body(x_ref, g_ref, t_ref, l_ref, dx_ref, dg_ref):
        x_, g_ = x_ref[...], g_ref[...]
        e = _rms_fwd(x_, g_) - t_ref[...]
        part = 0.5 * jnp.sum(jnp.sum(e * e, axis=-1, keepdims=True), axis=0, keepdims=True) / d
        dx, dg = _rms_bwd(x_, g_, e * (1.0 / d))
        dx_ref[...] = dx
        first = pl.program_id(0) == 0
        _accum(dg_ref, dg, first)
        _accum(l_ref, jnp.broadcast_to(part, (8, 128)), first)

    row = pl.BlockSpec((tm, d), lambda i: (i, 0))
    vec = pl.BlockSpec((1, d), lambda i: (0, 0))
    return pl.pallas_call(
        body,
        name=name,
        grid=(s // tm,),
        in_specs=[row, vec, row],
        out_specs=[pl.BlockSpec((8, 128), lambda i: (0, 0)), row, vec],
        out_shape=[jax.ShapeDtypeStruct((8, 128), F32), jax.ShapeDtypeStruct((s, d), F32), jax.ShapeDtypeStruct((1, d), F32)],
        compiler_params=_cparams(("arbitrary",)),
    )(x, g.reshape(1, d), target)


FF_TILE = 2048


def mlp_fwd(x, g, w_up, w_down, *, name="mlp_fwd"):
    s, d = x.shape
    ff = w_up.shape[1]
    tm, tf = _pick(s, (ROW_TILE, 256, 128)), _pick(ff, (FF_TILE, 512, 256, 128))
    nf = ff // tf

    def body(x_ref, g_ref, wu_ref, wd_ref, o_ref, h_ref, acc_ref):
        f = pl.program_id(1)

        @pl.when(f == 0)
        def _():
            h_ref[...] = _rms_fwd(x_ref[...], g_ref[...]).astype(BF16)
            acc_ref[...] = x_ref[...]

        up = jnp.maximum(_dot(h_ref[...], wu_ref[...]), 0.0)
        acc_ref[...] += _dot(up * up, wd_ref[...])

        @pl.when(f == nf - 1)
        def _():
            o_ref[...] = acc_ref[...]

    return pl.pallas_call(
        body,
        name=name,
        grid=(s // tm, nf),
        in_specs=[
            pl.BlockSpec((tm, d), lambda i, f: (i, 0)),
            pl.BlockSpec((1, d), lambda i, f: (0, 0)),
            pl.BlockSpec((d, tf), lambda i, f: (0, f)),
            pl.BlockSpec((tf, d), lambda i, f: (f, 0)),
        ],
        out_specs=pl.BlockSpec((tm, d), lambda i, f: (i, 0)),
        out_shape=jax.ShapeDtypeStruct((s, d), F32),
        scratch_shapes=[pltpu.VMEM((tm, d), BF16), pltpu.VMEM((tm, d), F32)],
        compiler_params=_cparams(("arbitrary", "arbitrary")),
    )(x, g.reshape(1, d), w_up, w_down)


def mlp_bwd(x, g, w_up, w_down, dy, *, name="mlp_bwd"):
    s, d = x.shape
    ff = w_up.shape[1]
    tm, tf = _pick(s, (ROW_TILE, 256, 128)), _pick(ff, (FF_TILE, 512, 256, 128))
    nf = ff // tf

    def body(x_ref, g_ref, wu_ref, wd_ref, dy_ref, dx_ref, dg_ref, h_ref, dup_ref, act_ref, dyb_ref, acc_ref):
        i, f = pl.program_id(0), pl.program_id(1)

        @pl.when(f == 0)
        def _():
            h_ref[...] = _rms_fwd(x_ref[...], g_ref[...]).astype(BF16)
            dyb_ref[...] = dy_ref[...].astype(BF16)
            acc_ref[...] = jnp.zeros_like(acc_ref)

        up = jnp.maximum(_dot(h_ref[...], wu_ref[...]), 0.0)
        dact = _dot(dyb_ref[...], wd_ref[...], 1, 1)
        dup = (2.0 * up * dact).astype(BF16)
        dup_ref[...] = dup
        act_ref[...] = (up * up).astype(BF16)
        acc_ref[...] += _dot(dup, wu_ref[...], 1, 1)

        @pl.when(f == nf - 1)
        def _():
            dx, dg = _rms_bwd(x_ref[...], g_ref[...], acc_ref[...])
            dx_ref[...] = dy_ref[...] + dx
            _accum(dg_ref, dg, i == 0)

    row = pl.BlockSpec((tm, d), lambda i, f: (i, 0))
    vec = pl.BlockSpec((1, d), lambda i, f: (0, 0))
    wide = pl.BlockSpec((tm, tf), lambda i, f: (i, f))
    return pl.pallas_call(
        body,
        name=name,
        grid=(s // tm, nf),
        in_specs=[row, vec, pl.BlockSpec((d, tf), lambda i, f: (0, f)), pl.BlockSpec((tf, d), lambda i, f: (f, 0)), row],
        out_specs=[row, vec, row, wide, wide, row],
        out_shape=[
            jax.ShapeDtypeStruct((s, d), F32),
            jax.ShapeDtypeStruct((1, d), F32),
            jax.ShapeDtypeStruct((s, d), BF16),
            jax.ShapeDtypeStruct((s, ff), BF16),
            jax.ShapeDtypeStruct((s, ff), BF16),
            jax.ShapeDtypeStruct((s, d), BF16),
        ],
        scratch_shapes=[pltpu.VMEM((tm, d), F32)],
        compiler_params=_cparams(("arbitrary", "arbitrary")),
    )(x, g.reshape(1, d), w_up, w_down, dy)


X_HEADS = 4


def _softmax_rows(s):
    m = jnp.max(s, axis=-1, keepdims=True)
    e = jnp.exp(s - m)
    return e / jnp.sum(e, axis=-1, keepdims=True)


def cross_fwd(x, g, wq, wo, kv, *, name="cross_fwd"):
    s, d = x.shape
    ml = kv.shape[0]
    dh = d // X_HEADS
    tm = _pick(s, (ROW_TILE, 256, 128))
    scale = dh**-0.5

    def body(x_ref, g_ref, wq_ref, wo_ref, kv_ref, o_ref):
        x_ = x_ref[...]
        q = _dot(_rms_fwd(x_, g_ref[...]), wq_ref[...])
        outs = []
        for hd in range(X_HEADS):
            kh = kv_ref[:, hd * dh : (hd + 1) * dh]
            vh = kv_ref[:, d + hd * dh : d + (hd + 1) * dh]
            p = _softmax_rows(_dot(q[:, hd * dh : (hd + 1) * dh], kh, 1, 1) * scale)
            outs.append(_dot(p, vh))
        o_ref[...] = x_ + _dot(jnp.concatenate(outs, axis=-1), wo_ref[...])

    row = pl.BlockSpec((tm, d), lambda i: (i, 0))
    full = lambda shp: pl.BlockSpec(shp, lambda i: (0, 0))
    return pl.pallas_call(
        body,
        name=name,
        grid=(s // tm,),
        in_specs=[row, full((1, d)), full((d, d)), full((d, d)), full((ml, 2 * d))],
        out_specs=row,
        out_shape=jax.ShapeDtypeStruct((s, d), F32),
        compiler_params=_cparams(("arbitrary",)),
    )(x, g.reshape(1, d), wq, wo, kv)


def cross_bwd(x, g, wq, wo, kv, dy, *, name="cross_bwd"):
    s, d = x.shape
    ml = kv.shape[0]
    dh = d // X_HEADS
    tm = _pick(s, (ROW_TILE, 256, 128))
    scale = dh**-0.5

    def body(x_ref, g_ref, wq_ref, wo_ref, kv_ref, dy_ref, dx_ref, dg_ref, h_ref, dq_ref, o_ref, dkv_ref):
        first = pl.program_id(0) == 0
        x_, g_ = x_ref[...], g_ref[...]
        h = _rms_fwd(x_, g_).astype(BF16)
        h_ref[...] = h
        q = _dot(h, wq_ref[...])
        dy_ = dy_ref[...]
        do = _dot(dy_, wo_ref[...], 1, 1)
        outs, dqs, dks, dvs = [], [], [], []
        for hd in range(X_HEADS):
            sl = slice(hd * dh, (hd + 1) * dh)
            kh = kv_ref[:, sl]
            vh = kv_ref[:, d + hd * dh : d + (hd + 1) * dh]
            qh = q[:, sl]
            p = _softmax_rows(_dot(qh, kh, 1, 1) * scale)
            outs.append(_dot(p, vh))
            doh = do[:, sl]
            dp = _dot(doh, vh, 1, 1)
            ds = p * (dp - jnp.sum(dp * p, axis=-1, keepdims=True)) * scale
            dqs.append(_dot(ds, kh))
            dks.append(_dot(ds, qh, 0, 0))
            dvs.append(_dot(p, doh, 0, 0))
        o_ref[...] = jnp.concatenate(outs, axis=-1).astype(BF16)
        dq = jnp.concatenate(dqs, axis=-1).astype(BF16)
        dq_ref[...] = dq
        dx, dg = _rms_bwd(x_, g_, _dot(dq, wq_ref[...], 1, 1))
        dx_ref[...] = dy_ + dx
        _accum(dkv_ref, jnp.concatenate(dks + dvs, axis=-1), first)
        _accum(dg_ref, dg, first)

    row = pl.BlockSpec((tm, d), lambda i: (i, 0))
    full = lambda shp: pl.BlockSpec(shp, lambda i: (0, 0))
    rowb = jax.ShapeDtypeStruct((s, d), BF16)
    return pl.pallas_call(
        body,
        name=name,
        grid=(s // tm,),
        in_specs=[row, full((1, d)), full((d, d)), full((d, d)), full((ml, 2 * d)), row],
        out_specs=[row, full((1, d)), row, row, row, full((ml, 2 * d))],
        out_shape=[jax.ShapeDtypeStruct((s, d), F32), jax.ShapeDtypeStruct((1, d), F32), rowb, rowb, rowb,
                   jax.ShapeDtypeStruct((ml, 2 * d), F32)],
        compiler_params=_cparams(("arbitrary",)),
    )(x, g.reshape(1, d), wq, wo, kv, dy)


N_BRANCH = 3
MIX_W = 512
ATT_DH = 128
ATT_HG = 4
MERGE_TILE = 256


def _group_weights(l0, l1, l2):
    m = jnp.maximum(jnp.maximum(l0, l1), l2)
    e = [jnp.exp(l0 - m), jnp.exp(l1 - m), jnp.exp(l2 - m)]
    inv = 1.0 / (e[0] + e[1] + e[2])
    return [t * inv for t in e]


def _sigmoid(x):
    return 0.5 * (jnp.tanh(0.5 * x) + 1.0)


def merge_fwd(x, proj, ya, yb, outs, lses, wb, wout, *, name="merge_fwd"):
    s, d = x.shape
    tm = _pick(s, (MERGE_TILE, 128))

    def body(x_ref, g0, g1, g2, ya_ref, yb_ref, o0, o1, o2, l0, l1, l2, wb_ref, wo_ref, x1_ref, mg_ref, yc_ref):
        w = _group_weights(l0[...], l1[...], l2[...])
        o = (o0[...], o1[...], o2[...])
        heads = []
        for hd in range(ATT_HG):
            sl = slice(hd * ATT_DH, (hd + 1) * ATT_DH)
            heads.append(sum(w[gi][:, hd : hd + 1] * o[gi][:, sl] for gi in range(3)))
        yc = jnp.concatenate(heads, axis=-1)
        yc_ref[...] = yc.astype(BF16)
        merged = None
        for n, (y, gl) in enumerate(((ya_ref[...], g0), (yb_ref[...], g1), (yc, g2))):
            t = _sigmoid(gl[...].astype(F32)) * _dot(y, wb_ref[n])
            merged = t if merged is None else merged + t
        mb = merged.astype(BF16)
        mg_ref[...] = mb
        x1_ref[...] = x_ref[...] + _dot(mb, wo_ref[...])

    row = pl.BlockSpec((tm, d), lambda i: (i, 0))
    half = pl.BlockSpec((tm, MIX_W), lambda i: (i, 0))
    perhead = pl.BlockSpec((tm, ATT_HG), lambda i: (i, 0))
    gate = [pl.BlockSpec((tm, d), lambda i, n=n: (i, n)) for n in range(N_BRANCH)]
    return pl.pallas_call(
        body,
        name=name,
        grid=(s // tm,),
        in_specs=[row] + gate + [half] * 5 + [perhead] * 3
        + [pl.BlockSpec((N_BRANCH, MIX_W, d), lambda i: (0, 0, 0)), pl.BlockSpec((d, d), lambda i: (0, 0))],
        out_specs=[row, row, half],
        out_shape=[jax.ShapeDtypeStruct((s, d), F32), jax.ShapeDtypeStruct((s, d), BF16), jax.ShapeDtypeStruct((s, MIX_W), BF16)],
        compiler_params=_cparams(("arbitrary",)),
    )(x, proj, proj, proj, ya, yb, *outs, *lses, wb, wout)


def merge_bwd(dx1, proj, ya, yb, yc, outs, lses, wb, wout, *, name="merge_bwd"):
    s, d = dx1.shape
    tm = _pick(s, (MERGE_TILE, 128))

    def body(dx_ref, g0, g1, g2, ya_ref, yb_ref, yc_ref, o0, o1, o2, l0, l1, l2, wb_ref, wo_ref,
             dgl_ref, dz0, dz1, dz2, dya_ref, dyb_ref, do0, do1, do2, dl0, dl1, dl2):
        dm = _dot(dx_ref[...], wo_ref[...], 1, 1)
        dys = []
        for n, (y, gl, dz_ref) in enumerate(((ya_ref, g0, dz0), (yb_ref, g1, dz1), (yc_ref, g2, dz2))):
            z = _dot(y[...], wb_ref[n])
            sg = _sigmoid(gl[...].astype(F32))
            dz = (dm * sg).astype(BF16)
            dz_ref[...] = dz
            dgl_ref[:, n * d : (n + 1) * d] = (dm * z * sg * (1.0 - sg)).astype(BF16)
            dys.append(_dot(dz, wb_ref[n], 1, 1))
        dya_ref[...] = dys[0]
        dyb_ref[...] = dys[1]
        dyc = dys[2]
        w = _group_weights(l0[...], l1[...], l2[...])
        o = [o0[...], o1[...], o2[...]]
        for hd in range(ATT_HG):
            sl = slice(hd * ATT_DH, (hd + 1) * ATT_DH)
            col = slice(hd, hd + 1)
            wh = [w[gi][:, col] for gi in range(3)]
            for gi, r in enumerate((do0, do1, do2)):
                r[:, sl] = wh[gi] * dyc[:, sl]
            t = [jnp.sum(dyc[:, sl] * o[gi][:, sl], axis=-1, keepdims=True) for gi in range(3)]
            tbar = wh[0] * t[0] + wh[1] * t[1] + wh[2] * t[2]
            for gi, r in enumerate((dl0, dl1, dl2)):
                r[:, col] = wh[gi] * (t[gi] - tbar)

    row = pl.BlockSpec((tm, d), lambda i: (i, 0))
    half = pl.BlockSpec((tm, MIX_W), lambda i: (i, 0))
    perhead = pl.BlockSpec((tm, ATT_HG), lambda i: (i, 0))
    gate = [pl.BlockSpec((tm, d), lambda i, n=n: (i, n)) for n in range(N_BRANCH)]
    rb = jax.ShapeDtypeStruct((s, d), BF16)
    hf = jax.ShapeDtypeStruct((s, MIX_W), F32)
    ph = jax.ShapeDtypeStruct((s, ATT_HG), F32)
    return pl.pallas_call(
        body,
        name=name,
        grid=(s // tm,),
        in_specs=[row] + gate + [half] * 6 + [perhead] * 3
        + [pl.BlockSpec((N_BRANCH, MIX_W, d), lambda i: (0, 0, 0)), pl.BlockSpec((d, d), lambda i: (0, 0))],
        out_specs=[pl.BlockSpec((tm, N_BRANCH * d), lambda i: (i, 0)), row, row, row] + [half] * 5 + [perhead] * 3,
        out_shape=[jax.ShapeDtypeStruct((s, N_BRANCH * d), BF16), rb, rb, rb] + [hf] * 5 + [ph] * 3,
        compiler_params=_cparams(("arbitrary",)),
    )(dx1, proj, proj, proj, ya, yb, yc, *outs, *lses, wb, wout)


ATT_BLOCK = 128
ATT_STEP = 4 * ATT_BLOCK
ATT_GROUPS = ((128, 1), (512, 4), (2048, 16))
N_ATT_HEADS = ATT_HG * len(ATT_GROUPS)
ALIBI_MAX_EXP = 8.0
MASKED = -1e30


def _att_slopes(group):
    return [2.0 ** (-ALIBI_MAX_EXP * (group * ATT_HG + h + 1) / N_ATT_HEADS) for h in range(ATT_HG)]


def _att_scores(q, kk, slope_dil, has_prev):
    scale = ATT_DH**-0.5
    qi = lax.broadcasted_iota(jnp.int32, (ATT_BLOCK, 2 * ATT_BLOCK), 0)
    kj = lax.broadcasted_iota(jnp.int32, (ATT_BLOCK, 2 * ATT_BLOCK), 1)
    dist = qi + ATT_BLOCK - kj
    valid = jnp.logical_and(jnp.logical_and(dist >= 0, dist <= ATT_BLOCK), jnp.logical_or(kj >= ATT_BLOCK, has_prev))
    return jnp.where(valid, _dot(q, kk, 1, 1) * scale - slope_dil * dist.astype(F32), MASKED)


def _att_window(kp_ref, kc_ref, buf):
    buf[:ATT_BLOCK, :] = kp_ref[...]
    buf[ATT_BLOCK:, :] = kc_ref[...]


def attn_fwd(qkv, col, group, *, name="attn_fwd"):
    s = qkv.shape[0]
    window, dil = ATT_GROUPS[group]
    assert window // dil == ATT_BLOCK and s % (dil * ATT_BLOCK) == 0 and col % MIX_W == 0
    nb = s // dil // ATT_BLOCK
    nq = ATT_STEP // ATT_BLOCK
    ng = s // ATT_STEP
    slopes = _att_slopes(group)
    c0 = col // MIX_W

    def body(q_ref, kp_ref, kc_ref, vp_ref, vc_ref, o_ref, l_ref, kbuf, vbuf):
        n = pl.program_id(0)
        _att_window(kp_ref, kc_ref, kbuf)
        _att_window(vp_ref, vc_ref, vbuf)
        for i in range(nq):
            r = slice(i * ATT_BLOCK, (i + 1) * ATT_BLOCK)
            win = slice(i * ATT_BLOCK, (i + 2) * ATT_BLOCK)
            has_prev = ((n * nq + i) % nb) != 0
            for hd in range(ATT_HG):
                sl = slice(hd * ATT_DH, (hd + 1) * ATT_DH)
                sc = _att_scores(q_ref[r, sl], kbuf[win, sl], slopes[hd] * dil, has_prev)
                m = jnp.max(sc, axis=-1, keepdims=True)
                e = jnp.exp(sc - m)
                den = jnp.sum(e, axis=-1, keepdims=True)
                o_ref[r, sl] = _dot(e * (1.0 / den), vbuf[win, sl])
                l_ref[r, hd : hd + 1] = m + jnp.log(den)

    cur = lambda c: pl.BlockSpec((ATT_STEP, MIX_W), lambda n, c=c: (n, c))
    prev = lambda c: pl.BlockSpec((ATT_BLOCK, MIX_W), lambda n, c=c: (jnp.maximum(n * nq - 1, 0), c))
    blk = pl.BlockSpec((ATT_STEP, MIX_W), lambda n: (n, 0))
    return pl.pallas_call(
        body,
        name=name,
        grid=(ng,),
        in_specs=[cur(c0), prev(c0 + 1), cur(c0 + 1), prev(c0 + 2), cur(c0 + 2)],
        out_specs=[blk, pl.BlockSpec((ATT_STEP, ATT_HG), lambda n: (n, 0))],
        out_shape=[jax.ShapeDtypeStruct((s, MIX_W), F32), jax.ShapeDtypeStruct((s, ATT_HG), F32)],
        scratch_shapes=[pltpu.VMEM((ATT_BLOCK + ATT_STEP, MIX_W), qkv.dtype)] * 2,
        compiler_params=_cparams(("arbitrary",)),
    )(qkv, qkv, qkv, qkv, qkv)


def attn_bwd(qkv, col, group, out, lse, dout, dlse, *, name="attn_bwd"):
    s = qkv.shape[0]
    window, dil = ATT_GROUPS[group]
    nb = s // dil // ATT_BLOCK
    nq = ATT_STEP // ATT_BLOCK
    ng = s // ATT_STEP
    slopes = _att_slopes(group)
    c0 = col // MIX_W
    scale = ATT_DH**-0.5
    tail = slice((nq - 1) * ATT_BLOCK, nq * ATT_BLOCK)

    def body(q_ref, kp_ref, kc_ref, vp_ref, vc_ref, o_ref, l_ref, do_ref, dl_ref, dq_ref, dk_ref, dv_ref, acck, accv,
             kbuf, vbuf):
        n = pl.program_id(0)
        live = n < ng

        @pl.when(n == 0)
        def _():
            acck[...] = jnp.zeros_like(acck)
            accv[...] = jnp.zeros_like(accv)

        _att_window(kp_ref, kc_ref, kbuf)
        _att_window(vp_ref, vc_ref, vbuf)
        dk_ref[: (nq - 1) * ATT_BLOCK, :] = acck[: (nq - 1) * ATT_BLOCK, :].astype(BF16)
        dv_ref[: (nq - 1) * ATT_BLOCK, :] = accv[: (nq - 1) * ATT_BLOCK, :].astype(BF16)
        for i in range(nq):
            r = slice(i * ATT_BLOCK, (i + 1) * ATT_BLOCK)
            rp = slice((i - 1) * ATT_BLOCK, i * ATT_BLOCK)
            win = slice(i * ATT_BLOCK, (i + 2) * ATT_BLOCK)
            has_prev = ((jnp.minimum(n, ng - 1) * nq + i) % nb) != 0
            for hd in range(ATT_HG):
                sl = slice(hd * ATT_DH, (hd + 1) * ATT_DH)
                q, do, kk, vv = q_ref[r, sl], do_ref[r, sl], kbuf[win, sl], vbuf[win, sl]
                p = jnp.exp(_att_scores(q, kk, slopes[hd] * dil, has_prev) - l_ref[r, hd : hd + 1])
                corr = dl_ref[r, hd : hd + 1] - jnp.sum(do * o_ref[r, sl], axis=-1, keepdims=True)
                ds = p * (_dot(do, vv, 1, 1) + corr) * scale
                dq_ref[r, sl] = _dot(ds, kk).astype(BF16)
                dkk = jnp.where(live, _dot(ds, q, 0, 0), 0.0)
                dvv = jnp.where(live, _dot(p, do, 0, 0), 0.0)
                if i == 0:
                    dk_ref[tail, sl] = (acck[tail, sl] + dkk[:ATT_BLOCK]).astype(BF16)
                    dv_ref[tail, sl] = (accv[tail, sl] + dvv[:ATT_BLOCK]).astype(BF16)
                else:
                    acck[rp, sl] += dkk[:ATT_BLOCK]
                    accv[rp, sl] += dvv[:ATT_BLOCK]
                acck[r, sl] = dkk[ATT_BLOCK:]
                accv[r, sl] = dvv[ATT_BLOCK:]

    last = ng - 1
    cur = lambda c: pl.BlockSpec((ATT_STEP, MIX_W), lambda n, c=c: (jnp.minimum(n, last), c))
    prev = lambda c: pl.BlockSpec((ATT_BLOCK, MIX_W), lambda n, c=c: (jnp.maximum(jnp.minimum(n, last) * nq - 1, 0), c))
    behind = pl.BlockSpec((ATT_STEP, MIX_W), lambda n: (jnp.clip(n - 1, 0, last), 0))
    perhead = pl.BlockSpec((ATT_STEP, ATT_HG), lambda n: (jnp.minimum(n, last), 0))
    sds = jax.ShapeDtypeStruct((s, MIX_W), BF16)
    return pl.pallas_call(
        body,
        name=name,
        grid=(ng + 1,),
        in_specs=[cur(c0), prev(c0 + 1), cur(c0 + 1), prev(c0 + 2), cur(c0 + 2), cur(0), perhead, cur(0), perhead],
        out_specs=[cur(0), behind, behind],
        out_shape=[sds, sds, sds],
        scratch_shapes=[pltpu.VMEM((ATT_STEP, MIX_W), F32)] * 2 + [pltpu.VMEM((ATT_BLOCK + ATT_STEP, MIX_W), qkv.dtype)] * 2,
        compiler_params=_cparams(("arbitrary",)),
    )(qkv, qkv, qkv, qkv, qkv, out, lse, dout, dlse)


GLA_HEADS = 4
GLA_DK = 64
GLA_DV = 128
GLA_CHUNK = 64
GLA_TAU = 16.0
GLA_QK = GLA_HEADS * GLA_DK
LR_PAD = 128


def _dot_exact(a, b, ca=1, cb=0, precision=lax.Precision.HIGHEST):
    return lax.dot_general(a, b, (((ca,), (cb,)), ((), ())), precision=precision, preferred_element_type=F32)


GLA_STEP = 4 * GLA_CHUNK


@jax.custom_vjp
def _score_dot(a, b):
    return _dot_exact(a, b, 1, 1)


def _score_dot_fwd(a, b):
    return _dot_exact(a, b, 1, 1), (a, b)


def _score_dot_bwd(res, g):
    a, b = res
    return _dot(g, b), _dot(g, a, 0, 0)


_score_dot.defvjp(_score_dot_fwd, _score_dot_bwd)


def _gla_group(q, k, v, lr, r, state, wg, bg, go):
    c, n = GLA_CHUNK, q.shape[0]
    z = _dot(lr, wg) + bg
    la = (jnp.minimum(z, 0.0) - jnp.log(1.0 + jnp.exp(-jnp.abs(z)))) * (1.0 / GLA_TAU)
    ri = lax.broadcasted_iota(jnp.int32, (n, n), 0)
    ci = lax.broadcasted_iota(jnp.int32, (n, n), 1)
    shift = c.bit_length() - 1
    in_chunk = jnp.logical_and(ri >= ci, jnp.right_shift(ri, shift) == jnp.right_shift(ci, shift))
    ball = _dot_exact(in_chunk.astype(F32), la, precision=lax.Precision.HIGH)
    ri = lax.broadcasted_iota(jnp.int32, (c, c), 0)
    ci = lax.broadcasted_iota(jnp.int32, (c, c), 1)
    causal = ri >= ci
    qs = q * GLA_DK**-0.5
    ys = []
    for j in range(n // c):
        rows = slice(j * c, (j + 1) * c)
        b = ball[rows]
        bmid = b[c // 2 : c // 2 + 1, :]
        bend = b[c - 1 : c, :]
        q_in = qs[rows] * jnp.exp(b)
        q_mid = qs[rows] * jnp.exp(b - bmid)
        k_mid = k[rows] * jnp.exp(bmid - b)
        k_end = k[rows] * jnp.exp(bend - b)
        yh, upd = [], []
        for h in range(GLA_HEADS):
            sk = slice(h * GLA_DK, (h + 1) * GLA_DK)
            sv = slice(h * GLA_DV, (h + 1) * GLA_DV)
            vh = v[rows, sv]
            inter = _dot(q_in[:, sk], state[:, sk], 1, 1)
            sc = jnp.where(causal, _score_dot(q_mid[:, sk], k_mid[:, sk]), 0.0)
            o = inter + _dot(sc, vh)
            o = o * lax.rsqrt(jnp.mean(o * o, axis=-1, keepdims=True) + RMS_EPS) * go
            rh = r[rows, sv]
            yh.append(o * rh * _sigmoid(rh))
            upd.append(_dot(vh, k_end[:, sk], 0, 0))
        state = jnp.exp(bend) * state + jnp.concatenate(upd, axis=-1)
        ys.append(jnp.concatenate(yh, axis=-1))
    return jnp.concatenate(ys, axis=0), state


def _gla_in_specs(cols, rev, nc):
    c = GLA_STEP
    row = (lambda i: nc - 1 - i) if rev else (lambda i: i)
    qc, kc, vc, lc, rc = cols
    assert qc % GLA_QK == 0 and kc % GLA_QK == 0 and vc % MIX_W == 0 and rc % MIX_W == 0 and lc % LR_PAD == 0
    return [
        pl.BlockSpec((c, GLA_QK), lambda i: (row(i), qc // GLA_QK)),
        pl.BlockSpec((c, GLA_QK), lambda i: (row(i), kc // GLA_QK)),
        pl.BlockSpec((c, MIX_W), lambda i: (row(i), vc // MIX_W)),
        pl.BlockSpec((c, LR_PAD), lambda i: (row(i), lc // LR_PAD)),
        pl.BlockSpec((c, MIX_W), lambda i: (row(i), rc // MIX_W)),
    ], row


def gla_fwd(proj, cols, wg, bg, go, *, name="gla_fwd"):
    s = proj.shape[0]
    nc = s // GLA_STEP
    specs, row = _gla_in_specs(cols, False, nc)

    def body(q_ref, k_ref, v_ref, lr_ref, r_ref, wg_ref, bg_ref, go_ref, y_ref, st_ref, state):
        @pl.when(pl.program_id(0) == 0)
        def _():
            state[...] = jnp.zeros_like(state)

        st = state[...]
        st_ref[...] = st
        y, new = _gla_group(q_ref[...], k_ref[...], v_ref[...], lr_ref[...], r_ref[...], st, wg_ref[...], bg_ref[...], go_ref[...])
        y_ref[...] = y
        state[...] = new

    full = lambda shp: pl.BlockSpec(shp, lambda i: (0, 0))
    return pl.pallas_call(
        body,
        name=name,
        grid=(nc,),
        in_specs=specs + [full((LR_PAD, GLA_QK)), full((1, GLA_QK)), full((1, GLA_DV))],
        out_specs=[pl.BlockSpec((GLA_STEP, MIX_W), lambda i: (i, 0)), pl.BlockSpec((GLA_DV, GLA_QK), lambda i: (i, 0))],
        out_shape=[jax.ShapeDtypeStruct((s, MIX_W), F32), jax.ShapeDtypeStruct((nc * GLA_DV, GLA_QK), F32)],
        scratch_shapes=[pltpu.VMEM((GLA_DV, GLA_QK), F32)],
        compiler_params=_cparams(("arbitrary",)),
    )(proj, proj, proj, proj, proj, wg, bg, go)


def gla_bwd(proj, cols, wg, bg, go, states, dy, *, name="gla_bwd"):
    s = proj.shape[0]
    nc = s // GLA_STEP
    specs, row = _gla_in_specs(cols, True, nc)

    def body(q_ref, k_ref, v_ref, lr_ref, r_ref, wg_ref, bg_ref, go_ref, st_ref, dy_ref,
             dq_ref, dk_ref, dv_ref, dlr_ref, dr_ref, dwg_ref, dbg_ref, dgo_ref, dstate):
        first = pl.program_id(0) == 0

        @pl.when(first)
        def _():
            dstate[...] = jnp.zeros_like(dstate)

        _, vjp = jax.vjp(_gla_group, q_ref[...], k_ref[...], v_ref[...], lr_ref[...], r_ref[...], st_ref[...],
                         wg_ref[...].astype(F32), bg_ref[...], go_ref[...])
        dq, dk, dv, dlr, dr, dst, dwg, dbg, dgo = vjp((dy_ref[...], dstate[...]))
        dq_ref[...] = dq.astype(BF16)
        dk_ref[...] = dk.astype(BF16)
        dv_ref[...] = dv.astype(BF16)
        dlr_ref[...] = dlr.astype(BF16)
        dr_ref[...] = dr.astype(BF16)
        dstate[...] = dst
        _accum(dwg_ref, dwg, first)
        _accum(dbg_ref, dbg, first)
        _accum(dgo_ref, dgo, first)

    c = GLA_STEP
    full = lambda shp: pl.BlockSpec(shp, lambda i: (0, 0))
    rows = lambda w: pl.BlockSpec((c, w), lambda i: (row(i), 0))
    return pl.pallas_call(
        body,
        name=name,
        grid=(nc,),
        in_specs=specs + [full((LR_PAD, GLA_QK)), full((1, GLA_QK)), full((1, GLA_DV)),
                          pl.BlockSpec((GLA_DV, GLA_QK), lambda i: (row(i), 0)), rows(MIX_W)],
        out_specs=[rows(GLA_QK), rows(GLA_QK), rows(MIX_W), rows(LR_PAD), rows(MIX_W),
                   full((LR_PAD, GLA_QK)), full((1, GLA_QK)), full((1, GLA_DV))],
        out_shape=[jax.ShapeDtypeStruct((s, GLA_QK), BF16), jax.ShapeDtypeStruct((s, GLA_QK), BF16),
                   jax.ShapeDtypeStruct((s, MIX_W), BF16), jax.ShapeDtypeStruct((s, LR_PAD), BF16),
                   jax.ShapeDtypeStruct((s, MIX_W), BF16), jax.ShapeDtypeStruct((LR_PAD, GLA_QK), F32),
                   jax.ShapeDtypeStruct((1, GLA_QK), F32), jax.ShapeDtypeStruct((1, GLA_DV), F32)],
        scratch_shapes=[pltpu.VMEM((GLA_DV, GLA_QK), F32)],
        compiler_params=_cparams(("arbitrary",)),
    )(proj, proj, proj, proj, proj, wg, bg, go, states, dy)


S5_G = 32
S5_P = 64
S5_C = 16
S5_N = S5_G * S5_P
S5_TC = 256
SUB = 8


def _s5_disc(a_re, a_im, ls, b_re, b_im):
    step = jnp.exp(ls)
    mag = jnp.exp(a_re * step)
    lr, li = mag * jnp.cos(a_im * step), mag * jnp.sin(a_im * step)
    inv = 1.0 / (a_re * a_re + a_im * a_im)
    nr, ni = lr - 1.0, li
    cr, ci = (nr * a_re + ni * a_im) * inv, (ni * a_re - nr * a_im) * inv
    return lr, li, cr * b_re - ci * b_im, cr * b_im + ci * b_re


def s5_disc_fwd(a_re, a_im, ls, b_re, b_im, *, name="s5_disc"):
    def body(ar, ai, l, br, bi, o0, o1, o2, o3):
        for o, val in zip((o0, o1, o2, o3), _s5_disc(ar[...], ai[...], l[...], br[...], bi[...])):
            o[...] = val

    sds = jax.ShapeDtypeStruct(a_re.shape, F32)
    return pl.pallas_call(body, name=name, out_shape=[sds] * 4)(a_re, a_im, ls, b_re, b_im)


def s5_disc_bwd(a_re, a_im, ls, b_re, b_im, cts, *, name="s5_disc_bwd"):
    def body(ar, ai, l, br, bi, c0, c1, c2, c3, dar, dai, dl, dbr, dbi):
        _, vjp = jax.vjp(_s5_disc, ar[...], ai[...], l[...], br[...], bi[...])
        g = vjp((c0[...], c1[...], c2[...], c3[...]))
        for o, val in zip((dar, dai, dl), g[:3]):
            o[...] = jnp.sum(val, axis=-1, keepdims=True)
        dbr[...] = g[3]
        dbi[...] = g[4]

    col = jax.ShapeDtypeStruct((a_re.shape[0], 1), F32)
    sds = jax.ShapeDtypeStruct(a_re.shape, F32)
    return pl.pallas_call(body, name=name, out_shape=[col, col, col, sds, sds])(a_re, a_im, ls, b_re, b_im, *cts)


def _gelu(y):
    return 0.5 * y * (1.0 + jnp.tanh(0.7978845608028654 * (y + 0.044715 * (y * y * y))))


def _s5_powers(lam, conj):
    lr, li = lam[:, :S5_N], lam[:, S5_N:]
    if conj:
        li = -li
    rows, pr, pi = [], lr, li
    for _ in range(SUB):
        rows.append((pr, pi))
        pr, pi = pr * lr - pi * li, pr * li + pi * lr
    return rows


def _s5_table(rows, reverse):
    ridx = lax.broadcasted_iota(jnp.int32, (SUB, S5_N), 0)
    tr = jnp.zeros((SUB, S5_N), F32)
    ti = jnp.zeros((SUB, S5_N), F32)
    for i in range(SUB):
        pr, pi = rows[SUB - 1 - i] if reverse else rows[i]
        tr = jnp.where(ridx == i, pr, tr)
        ti = jnp.where(ridx == i, pi, ti)
    return tr, ti


def _s5_scan(buf, lam, carry_ref, reverse):
    tc = buf.shape[0]
    nblk = tc // SUB
    rows = _s5_powers(lam, reverse)
    tr, ti = _s5_table(rows, reverse)
    ridx = lax.broadcasted_iota(jnp.int32, (SUB, S5_N), 0)
    steps = []
    for sft, (pr, pi) in ((1, rows[0]), (2, rows[1]), (4, rows[3])):
        keep = (ridx < SUB - sft) if reverse else (ridx >= sft)
        steps.append((SUB - sft if reverse else sft, jnp.where(keep, pr, 0.0), jnp.where(keep, pi, 0.0)))

    def block(j, carry):
        jj = (nblk - 1 - j) if reverse else j
        at = pl.ds(pl.multiple_of(jj * SUB, SUB), SUB)
        re, im = buf[at, :S5_N], buf[at, S5_N:]
        for rot, pr, pi in steps:
            sre, sim = pltpu.roll(re, rot, 0), pltpu.roll(im, rot, 0)
            re, im = re + pr * sre - pi * sim, im + pr * sim + pi * sre
        cr, ci = carry
        re, im = re + tr * cr - ti * ci, im + tr * ci + ti * cr
        buf[at, :S5_N] = re
        buf[at, S5_N:] = im
        edge = 0 if reverse else SUB - 1
        return re[edge : edge + 1, :], im[edge : edge + 1, :]

    c0 = (carry_ref[0:1, :S5_N], carry_ref[0:1, S5_N:])
    cr, ci = lax.fori_loop(0, nblk, block, c0)
    carry_ref[:, :S5_N] = jnp.broadcast_to(cr, (SUB, S5_N))
    carry_ref[:, S5_N:] = jnp.broadcast_to(ci, (SUB, S5_N))


def s5_fwd(proj, ucol, bd, cd, lam, dskip, wglu, bglu, *, name="s5_fwd"):
    s = proj.shape[0]
    tc = _pick(s, (S5_TC, 128, 64, 8))
    assert ucol % MIX_W == 0

    def body(u_ref, bd_ref, cd_ref, lam_ref, d_ref, w_ref, b_ref, y_ref, xs_ref, carry):
        @pl.when(pl.program_id(0) == 0)
        def _():
            carry[...] = jnp.zeros_like(carry)

        u = u_ref[...]
        xs_ref[...] = _dot(u, bd_ref[...])
        _s5_scan(xs_ref, lam_ref[...], carry, False)
        g = _gelu(_dot(xs_ref[...], cd_ref[...]) + d_ref[...] * u)
        y_ref[...] = g * _sigmoid(_dot(g, w_ref[...]) + b_ref[...])

    full = lambda shp: pl.BlockSpec(shp, lambda i: (0, 0))
    return pl.pallas_call(
        body,
        name=name,
        grid=(s // tc,),
        in_specs=[pl.BlockSpec((tc, MIX_W), lambda i: (i, ucol // MIX_W)), full((MIX_W, 2 * S5_N)), full((2 * S5_N, MIX_W)),
                  full((1, 2 * S5_N)), full((1, MIX_W)), full((MIX_W, MIX_W)), full((1, MIX_W))],
        out_specs=[pl.BlockSpec((tc, MIX_W), lambda i: (i, 0)), pl.BlockSpec((tc, 2 * S5_N), lambda i: (i, 0))],
        out_shape=[jax.ShapeDtypeStruct((s, MIX_W), F32), jax.ShapeDtypeStruct((s, 2 * S5_N), F32)],
        scratch_shapes=[pltpu.VMEM((SUB, 2 * S5_N), F32)],
        compiler_params=_cparams(("arbitrary",)),
    )(proj, bd, cd, lam, dskip, wglu, bglu)


def s5_bwd(proj, ucol, xs, bd, cd, lam, dskip, wglu, bglu, dya, *, name="s5_bwd"):
    s = proj.shape[0]
    tc = _pick(s, (S5_TC, 128, 64, 8))
    nch = s // tc
    per = tc // SUB

    def body(u_ref, xs_ref, xp_ref, bd_ref, cd_ref, lam_ref, d_ref, w_ref, b_ref, dya_ref,
             du_ref, adj_ref, g_ref, dpre_ref, dy_ref, dlam_ref, dd_ref, db_ref, buf, carry, lacc):
        i = pl.program_id(0)
        first = i == 0

        @pl.when(first)
        def _():
            carry[...] = jnp.zeros_like(carry)
            lacc[...] = jnp.zeros_like(lacc)

        u, x, dya_ = u_ref[...], xs_ref[...], dya_ref[...]
        y = _dot(x, cd_ref[...]) + d_ref[...] * u
        g, gelu_vjp = jax.vjp(_gelu, y)
        sg = _sigmoid(_dot(g, w_ref[...]) + b_ref[...])
        dpre = dya_ * g * sg * (1.0 - sg)
        (dy,) = gelu_vjp(dya_ * sg + _dot(dpre, w_ref[...], 1, 1))
        g_ref[...] = g.astype(BF16)
        dpre_ref[...] = dpre.astype(BF16)
        dy_ref[...] = dy.astype(BF16)
        db_part = jnp.sum(dpre, axis=0, keepdims=True)
        dd_part = jnp.sum(dy * u, axis=0, keepdims=True)
        buf[...] = _dot(dy, cd_ref[...], 1, 1)
        _s5_scan(buf, lam_ref[...], carry, True)
        a = buf[...]
        adj_ref[...] = a.astype(BF16)
        du_ref[...] = (dy * d_ref[...] + _dot(a, bd_ref[...], 1, 1)).astype(BF16)
        before = jnp.where(i == nch - 1, 0.0, xp_ref[SUB - 1 : SUB, :])
        ridx = lax.broadcasted_iota(jnp.int32, (tc, 2 * S5_N), 0)
        xprev = jnp.where(ridx == 0, before, pltpu.roll(x, 1, 0))
        ar, ai, xr, xi = a[:, :S5_N], a[:, S5_N:], xprev[:, :S5_N], xprev[:, S5_N:]
        lacc[:, :S5_N] += jnp.sum((ar * xr + ai * xi).reshape(per, SUB, S5_N), axis=0)
        lacc[:, S5_N:] += jnp.sum((ai * xr - ar * xi).reshape(per, SUB, S5_N), axis=0)
        _accum(db_ref, db_part, first)
        _accum(dd_ref, dd_part, first)

        @pl.when(i == nch - 1)
        def _():
            dlam_ref[...] = jnp.sum(lacc[...], axis=0, keepdims=True)

    rev = lambda i: nch - 1 - i
    full = lambda shp: pl.BlockSpec(shp, lambda i: (0, 0))
    rows = lambda w: pl.BlockSpec((tc, w), lambda i: (rev(i), 0))
    hb = jax.ShapeDtypeStruct((s, MIX_W), BF16)
    return pl.pallas_call(
        body,
        name=name,
        grid=(nch,),
        in_specs=[pl.BlockSpec((tc, MIX_W), lambda i: (rev(i), ucol // MIX_W)), rows(2 * S5_N),
                  pl.BlockSpec((SUB, 2 * S5_N), lambda i: (jnp.maximum(rev(i) * per - 1, 0), 0)),
                  full((MIX_W, 2 * S5_N)), full((2 * S5_N, MIX_W)), full((1, 2 * S5_N)), full((1, MIX_W)),
                  full((MIX_W, MIX_W)), full((1, MIX_W)), rows(MIX_W)],
        out_specs=[rows(MIX_W), rows(2 * S5_N), rows(MIX_W), rows(MIX_W), rows(MIX_W),
                   full((1, 2 * S5_N)), full((1, MIX_W)), full((1, MIX_W))],
        out_shape=[hb, jax.ShapeDtypeStruct((s, 2 * S5_N), BF16), hb, hb, hb,
                   jax.ShapeDtypeStruct((1, 2 * S5_N), F32), jax.ShapeDtypeStruct((1, MIX_W), F32),
                   jax.ShapeDtypeStruct((1, MIX_W), F32)],
        scratch_shapes=[pltpu.VMEM((tc, 2 * S5_N), F32), pltpu.VMEM((SUB, 2 * S5_N), F32), pltpu.VMEM((SUB, 2 * S5_N), F32)],
        compiler_params=_cparams(("arbitrary",)),
    )(proj, xs, xs, bd, cd, lam, dskip, wglu, bglu, dya)


def _bcast16(a):
    return jnp.broadcast_to(a.reshape(S5_N, 1), (S5_N, S5_C))


def _blockdiag(blocks):
    g, r, c = blocks.shape
    eye = jnp.eye(g, dtype=blocks.dtype)
    return (eye[:, None, :, None] * blocks[:, :, None, :]).reshape(g * r, g * c)


def _blockdiag_extract(dense, r, c):
    g = dense.shape[0] // r
    return jnp.einsum("grgc->grc", dense.reshape(g, r, g, c))


def s5_prepare(a_re, a_im, log_step, b_re, b_im, c_re, c_im):
    disc_in = (_bcast16(a_re), _bcast16(a_im), _bcast16(jnp.broadcast_to(log_step[:, None], (S5_G, S5_P))),
               b_re.reshape(S5_N, S5_C), b_im.reshape(S5_N, S5_C))
    lr, li, bbr, bbi = s5_disc_fwd(*disc_in)
    lam = jnp.concatenate([lr[:, 0], li[:, 0]]).reshape(1, 2 * S5_N)
    to_blocks = lambda t: _blockdiag(t.reshape(S5_G, S5_P, S5_C).transpose(0, 2, 1))
    bd = jnp.concatenate([to_blocks(bbr), to_blocks(bbi)], axis=1).astype(BF16)
    cd = jnp.concatenate([_blockdiag(c_re.transpose(0, 2, 1)), -_blockdiag(c_im.transpose(0, 2, 1))], axis=0).astype(BF16)
    return disc_in, lam, bd, cd


def s5_param_grads(disc_in, dlam, dbd, dcd):
    first_col = lambda v: jnp.pad(v.reshape(S5_N, 1), ((0, 0), (0, S5_C - 1)))
    from_blocks = lambda t: _blockdiag_extract(t, S5_C, S5_P).transpose(0, 2, 1).reshape(S5_N, S5_C)
    cts = (first_col(dlam[0, :S5_N]), first_col(dlam[0, S5_N:]), from_blocks(dbd[:, :S5_N]), from_blocks(dbd[:, S5_N:]))
    dar, dai, dls, dbr, dbi = s5_disc_bwd(*disc_in, cts)
    dcr = _blockdiag_extract(dcd[:S5_N], S5_P, S5_C).transpose(0, 2, 1)
    dci = -_blockdiag_extract(dcd[S5_N:], S5_P, S5_C).transpose(0, 2, 1)
    return (dar.reshape(S5_G, S5_P), dai.reshape(S5_G, S5_P), dls.reshape(S5_G, S5_P).sum(axis=1),
            dbr.reshape(S5_G, S5_P, S5_C), dbi.reshape(S5_G, S5_P, S5_C), dcr, dci)


N_DEV = 8
MESH_ID = pl.DeviceIdType.MESH
ANY = pl.BlockSpec(memory_space=pl.ANY)


def _me():
    return lax.axis_index("x"), lax.axis_index("y"), lax.axis_index("c")


def all_gather(blocks, *, by_core=False, name="all_gather"):
    na = len(blocks)
    shapes = [b.shape[1:] if by_core else b.shape for b in blocks]

    def body(*refs):
        ins, outs = refs[:na], refs[na : 2 * na]
        send_sems, recv_sems = refs[2 * na :]
        x, y, c = _me()
        me, sibling = (x, y, c), (x, y, 1 - c)
        xn, yn, diag = (1 - x, y, c), (x, 1 - y, c), (1 - x, 1 - y, c)
        slot = lambda p: 4 * p[0] + 2 * p[1] + p[2]

        def copy(a, k, block, to, half=None, src=None):
            dst = outs[a].at[slot(block)]
            if half is not None:
                rows = shapes[a][0] // 2
                dst = dst.at[pl.ds(half * rows, rows)]
            return pltpu.make_async_remote_copy(src_ref=dst if src is None else src, dst_ref=dst, send_sem=send_sems.at[a * 9 + k],
                                                recv_sem=recv_sems.at[a * 9 + k], device_id=to, device_id_type=MESH_ID)

        mine = [ins[a].at[c] if by_core else ins[a] for a in range(na)]
        sends = []

        def go(cp):
            cp.start()
            sends.append(cp)

        for a in range(na):
            go(copy(a, 0, me, sibling, src=mine[a]))
            go(copy(a, 1, me, xn, src=mine[a]))
            go(copy(a, 2, me, yn, src=mine[a]))
        for a in range(na):
            copy(a, 1, xn, me).wait_recv()
            go(copy(a, 3, xn, yn, half=0))
            go(copy(a, 5, xn, sibling))
            copy(a, 2, yn, me).wait_recv()
            go(copy(a, 4, yn, xn, half=1))
            go(copy(a, 6, yn, sibling))
        for a in range(na):
            copy(a, 3, diag, me, half=0).wait_recv()
            go(copy(a, 7, diag, sibling, half=0))
            copy(a, 4, diag, me, half=1).wait_recv()
            go(copy(a, 8, diag, sibling, half=1))
        for a in range(na):
            copy(a, 0, sibling, me).wait_recv()
            copy(a, 5, (1 - x, y, 1 - c), me).wait_recv()
            copy(a, 6, (x, 1 - y, 1 - c), me).wait_recv()
            copy(a, 7, (1 - x, 1 - y, 1 - c), me, half=0).wait_recv()
            copy(a, 8, (1 - x, 1 - y, 1 - c), me, half=1).wait_recv()
        for cp in sends:
            cp.wait_send()

    assert all(shp[0] % (64 // b.dtype.itemsize) == 0 for shp, b in zip(shapes, blocks)), shapes
    outs = pl.pallas_call(
        body,
        name=name,
        in_specs=[ANY] * na,
        out_specs=[ANY] * na,
        out_shape=[jax.ShapeDtypeStruct((N_DEV, *shp), b.dtype) for shp, b in zip(shapes, blocks)],
        scratch_shapes=[pltpu.SemaphoreType.DMA((na * 9,)), pltpu.SemaphoreType.DMA((na * 9,))],
    )(*blocks)
    x, y, c = _me()
    own = [lax.dynamic_index_in_dim(b, c, 0, keepdims=False) if by_core else b for b in blocks]
    return [lax.dynamic_update_index_in_dim(o, b, 4 * x + 2 * y + c, 0) for o, b in zip(outs, own)]


D2D_PIECES = 8


def _slabs(rows):
    n = D2D_PIECES if rows % (16 * D2D_PIECES) == 0 else 1
    return [pl.ds(i * (rows // n), rows // n) for i in range(n)]


def sibling_swap(arrs, *, name="sibling_swap"):
    na = len(arrs)
    pieces = [[(j, sl) for j in range(a.shape[0]) for sl in (_slabs(a.shape[1]) if a.shape[0] == 1 else _halves(a.shape[1]))]
              for a in arrs]
    base = np.cumsum([0] + [len(p) for p in pieces])

    def body(*refs):
        ins, outs = refs[:na], refs[na : 2 * na]
        send_sems, recv_sems = refs[2 * na :]
        x, y, c = _me()
        sends, recvs = [], []
        for a in range(na):
            for i, (j, sl) in enumerate(pieces[a]):
                k = int(base[a]) + i
                cp = pltpu.make_async_remote_copy(src_ref=ins[a].at[j, sl], dst_ref=outs[a].at[j, sl], send_sem=send_sems.at[k],
                                                  recv_sem=recv_sems.at[k], device_id=(x, y, 1 - c), device_id_type=MESH_ID)
                cp.start()
                sends.append(cp)
        for cp in sends:
            cp.wait_recv()
        for cp in sends:
            cp.wait_send()

    nsem = int(base[-1])
    return pl.pallas_call(
        body,
        name=name,
        in_specs=[ANY] * na,
        out_specs=[ANY] * na,
        out_shape=[jax.ShapeDtypeStruct(a.shape, a.dtype) for a in arrs],
        scratch_shapes=[pltpu.SemaphoreType.DMA((nsem,)), pltpu.SemaphoreType.DMA((nsem,))],
    )(*arrs)


def _halves(rows):
    return [pl.ds(0, rows // 2), pl.ds(rows // 2, rows // 2)] if rows % 32 == 0 else [pl.ds(0, rows)]


def chip_exchange(arrs, *, name="chip_exchange"):
    na = len(arrs)

    def body(*refs):
        ins, outs = refs[:na], refs[na : 2 * na]
        send_sems, recv_sems = refs[2 * na :]
        x, y, c = _me()
        my = 2 * x + y
        peers = []
        for k in range(1, N_CHIPS):
            px, py = (x + ((k >> 1) & 1)) % 2, (y + (k & 1)) % 2
            peers.append(((px, py, c), 2 * px + py))
        sends = []
        for a in range(na):
            for k, (peer, pidx) in enumerate(peers):
                cp = pltpu.make_async_remote_copy(src_ref=ins[a].at[pidx], dst_ref=outs[a].at[my], send_sem=send_sems.at[a * 3 + k],
                                                  recv_sem=recv_sems.at[a * 3 + k], device_id=peer, device_id_type=MESH_ID)
                cp.start()
                sends.append(cp)
        for a in range(na):
            for k, (peer, pidx) in enumerate(peers):
                pltpu.make_async_remote_copy(src_ref=ins[a].at[my], dst_ref=outs[a].at[pidx], send_sem=send_sems.at[a * 3 + k],
                                             recv_sem=recv_sems.at[a * 3 + k], device_id=peer, device_id_type=MESH_ID).wait_recv()
        for cp in sends:
            cp.wait_send()

    outs = pl.pallas_call(
        body,
        name=name,
        in_specs=[ANY] * na,
        out_specs=[ANY] * na,
        out_shape=[jax.ShapeDtypeStruct(a.shape, a.dtype) for a in arrs],
        scratch_shapes=[pltpu.SemaphoreType.DMA((na * 3,)), pltpu.SemaphoreType.DMA((na * 3,))],
    )(*arrs)
    chip = 2 * lax.axis_index("x") + lax.axis_index("y")
    return [lax.dynamic_update_index_in_dim(o, lax.dynamic_index_in_dim(a, chip, 0, keepdims=False), chip, 0)
            for o, a in zip(outs, arrs)]


def add_pair(a, b, *, name="add_pair"):
    n, r, c = a.shape
    tm = _row_tile(r, n * c * 2, 2 << 20)

    def body(a_ref, b_ref, o_ref):
        o_ref[...] = (a_ref[...].astype(F32) + b_ref[...].astype(F32)).astype(o_ref.dtype)

    blk = pl.BlockSpec((n, tm, c), lambda i: (0, i, 0))
    return pl.pallas_call(
        body, name=name, grid=(r // tm,), in_specs=[blk, blk], out_specs=blk, out_shape=jax.ShapeDtypeStruct(a.shape, a.dtype),
        compiler_params=_cparams(("arbitrary",)),
    )(a, b)


def _row_tile(rows, row_bytes, budget):
    best = None
    for t in range(8, rows + 1, 8):
        if rows % t == 0 and t * row_bytes <= budget:
            best = t
    return best or rows


def sum_blocks(a, *, name="sum_blocks"):
    n, r, c = a.shape
    tm = _row_tile(r, n * c * a.dtype.itemsize, 4 << 20)

    def body(a_ref, o_ref):
        acc = a_ref[0].astype(F32)
        for k in range(1, n):
            acc = acc + a_ref[k].astype(F32)
        o_ref[...] = acc

    return pl.pallas_call(
        body,
        name=name,
        grid=(r // tm,),
        in_specs=[pl.BlockSpec((n, tm, c), lambda i: (0, i, 0))],
        out_specs=pl.BlockSpec((tm, c), lambda i: (i, 0)),
        out_shape=jax.ShapeDtypeStruct((r, c), F32),
        compiler_params=_cparams(("arbitrary",)),
    )(a)


ADAM_LR = 0.001
ADAM_B1 = 0.9
ADAM_B2 = 0.999
ADAM_EPS = 1e-08
ADAM_WD = 0.01
ADAM_STEP = 10


def adamw(w, g, m, v, *, name="adamw"):
    r, c = w.shape[-2:]
    tm = _row_tile(r, c * 4, 1 << 20)
    if w.ndim == 3:
        blk = pl.BlockSpec((None, tm, c), lambda l, i: (l, i, 0))
        grid = (w.shape[0], r // tm)
    else:
        blk = pl.BlockSpec((tm, c), lambda i: (i, 0))
        grid = (r // tm,)

    def body(w_ref, g_ref, m_ref, v_ref, d_ref, nm_ref, nv_ref):
        g_ = g_ref[...]
        m_ = ADAM_B1 * m_ref[...] + (1.0 - ADAM_B1) * g_
        v_ = ADAM_B2 * v_ref[...] + (1.0 - ADAM_B2) * (g_ * g_)
        m_hat = m_ / (1.0 - ADAM_B1**ADAM_STEP)
        v_hat = v_ / (1.0 - ADAM_B2**ADAM_STEP)
        d_ref[...] = -ADAM_LR * (m_hat / (jnp.sqrt(v_hat) + ADAM_EPS) + ADAM_WD * w_ref[...])
        nm_ref[...] = m_
        nv_ref[...] = v_

    sds = jax.ShapeDtypeStruct(w.shape, F32)
    return pl.pallas_call(
        body, name=name, grid=grid, in_specs=[blk] * 4, out_specs=[blk] * 3, out_shape=[sds] * 3,
        compiler_params=_cparams(("arbitrary",) * len(grid)),
    )(w, g, m, v)


def adamw_paired(w, mine, theirs, m, v, *, name="adamw_paired"):
    _, r, c = w.shape
    tm = _row_tile(r, c * 4, 1 << 20)

    def body(w_ref, a_ref, b_ref, m_ref, v_ref, g_ref, d_ref, nm_ref, nv_ref):
        g_ = jnp.where(pl.program_id(0) == lax.axis_index("c"), a_ref[...], b_ref[...])
        g_ref[...] = g_
        m_ = ADAM_B1 * m_ref[...] + (1.0 - ADAM_B1) * g_
        v_ = ADAM_B2 * v_ref[...] + (1.0 - ADAM_B2) * (g_ * g_)
        m_hat = m_ / (1.0 - ADAM_B1**ADAM_STEP)
        v_hat = v_ / (1.0 - ADAM_B2**ADAM_STEP)
        d_ref[...] = -ADAM_LR * (m_hat / (jnp.sqrt(v_hat) + ADAM_EPS) + ADAM_WD * w_ref[...])
        nm_ref[...] = m_
        nv_ref[...] = v_

    blk = pl.BlockSpec((None, tm, c), lambda l, i: (l, i, 0))
    flat = pl.BlockSpec((tm, c), lambda l, i: (i, 0))
    sds = jax.ShapeDtypeStruct(w.shape, F32)
    return pl.pallas_call(
        body, name=name, grid=(2, r // tm), in_specs=[blk, flat, flat, blk, blk], out_specs=[blk] * 4, out_shape=[sds] * 4,
        compiler_params=_cparams(("arbitrary", "arbitrary")),
    )(w, mine, theirs, m, v)


WEIGHTS = ["g_mix", "w_in", "s5_a_re", "s5_a_im", "s5_log_step", "s5_b_re", "s5_b_im", "s5_c_re", "s5_c_im", "s5_d", "w_glu",
           "b_glu", "w_gla_gate", "b_gla_gate", "g_gla_out", "w_branch", "w_out", "g_mem", "g_cross", "w_xq", "w_xkv", "w_xo",
           "g_mlp", "w_up", "w_down", "g_final"]
SHARDED = {"w_in": 1, "w_glu": 0, "w_branch": 2, "w_out": 0, "w_xq": 0, "w_xkv": 1, "w_xo": 0, "w_up": 1, "w_down": 0}
DEPTH = 2
N_CHIPS = 4
D_IN = 9744
C_U, C_QG, C_KG, C_VG, C_LR, C_RG, C_QA, C_KA, C_VA, C_GATE = 0, 512, 768, 1024, 1536, 1552, 2064, 3600, 5136, 6672
A_U, A_QG, A_KG, A_VG, A_RG, A_Q0, A_LR, A_W = 0, 512, 768, 1024, 1536, 2048, 3584, 4096
GLA_COLS = (A_QG, A_KG, A_VG, A_LR, A_RG)


def _split_w_in(w):
    att = lambda g: [w[:, c + MIX_W * g : c + MIX_W * (g + 1)] for c in (C_QA, C_KA, C_VA)]
    a = jnp.concatenate([w[:, C_U:C_QG], w[:, C_QG:C_KG], w[:, C_KG:C_VG], w[:, C_VG:C_LR], w[:, C_RG:C_QA],
                         *att(0), w[:, C_LR:C_RG], jnp.zeros((w.shape[0], A_W - A_LR - 16), w.dtype)], axis=1)
    return a, jnp.concatenate(att(1), axis=1), jnp.concatenate(att(2), axis=1), w[:, C_GATE:D_IN]


def _join_w_in(a, b, c, gates):
    att = lambda k: [a[:, A_Q0 + MIX_W * k : A_Q0 + MIX_W * (k + 1)], b[:, MIX_W * k : MIX_W * (k + 1)], c[:, MIX_W * k : MIX_W * (k + 1)]]
    return jnp.concatenate([a[:, A_U:A_QG], a[:, A_QG:A_KG], a[:, A_KG:A_VG], a[:, A_VG:A_RG], a[:, A_LR : A_LR + 16], a[:, A_RG:A_Q0],
                            *att(0), *att(1), *att(2), gates], axis=1)


def _by_residue(a, d):
    s = a.shape[0]
    return a if d == 1 else a.reshape(s // d, d, -1).transpose(1, 0, 2).reshape(s, -1)


def _in_order(a, d):
    s = a.shape[0]
    return a if d == 1 else a.reshape(d, s // d, -1).transpose(1, 0, 2).reshape(s, -1)


def _pack(arrs):
    flat = jnp.concatenate([a.reshape(-1) for a in arrs])
    n = flat.shape[0]
    rows = -(-n // (256 * 128)) * 256
    return jnp.pad(flat, (0, rows * 128 - n)).reshape(rows, 128)


def _unpack(packed, like):
    flat, out, o = packed.reshape(-1), [], 0
    for a in like:
        n = math.prod(a.shape)
        out.append(flat[o : o + n].reshape(a.shape))
        o += n
    return out


def _layer_fwd(x0, p, kv):
    h = rmsnorm(x0, p["g_mix"], name="mix_norm")
    hs = [h, _by_residue(h, 4), _by_residue(h, 16)]
    proj = [matmul(hs[g], p["w_in_seg"][g], out_dtype=F32 if g == 0 else BF16, name="in_proj") for g in range(3)]
    gates = matmul(h, p["w_in_seg"][3], out_dtype=BF16, name="in_proj")
    ya, xs = s5_fwd(proj[0], A_U, p["bd"], p["cd"], p["lam"], p["s5_d"], p["w_glu"], p["b_glu"])
    yb, gst = gla_fwd(proj[0], GLA_COLS, p["w_gate"], p["b_gla_gate"], p["g_gla_out"])
    att = [attn_fwd(proj[g], A_Q0 if g == 0 else 0, g) for g in range(3)]
    outs = [_in_order(att[g][0], ATT_GROUPS[g][1]) for g in range(3)]
    lses = [_in_order(att[g][1], ATT_GROUPS[g][1]) for g in range(3)]
    x1, merged, yc = merge_fwd(x0, gates, ya, yb, outs, lses, p["w_branch"], p["w_out"])
    x2 = cross_fwd(x1, p["g_cross"], p["w_xq"], p["w_xo"], kv)
    x3 = mlp_fwd(x2, p["g_mlp"], p["w_up"], p["w_down"])
    saved = dict(x0=x0, x1=x1, x2=x2, hs=hs, proj=proj, gates=gates, ya=ya, xs=xs, yb=yb, gst=gst, att=att, outs=outs,
                 lses=lses, merged=merged, yc=yc)
    return x3, saved


def _layer_bwd(dx3, p, kv, memn, sv):
    g = {}
    tn = lambda a, b, **kw: matmul(a, b, ta=True, out_dtype=BF16, name="wgrad", **kw)
    dx2, g["g_mlp"], h3, dup, act, dx3b = mlp_bwd(sv["x2"], p["g_mlp"], p["w_up"], p["w_down"], dx3)
    g["w_up"], g["w_down"] = tn(h3, dup), tn(act, dx3b)
    dx1, g["g_cross"], h2, dq, o, dkv = cross_bwd(sv["x1"], p["g_cross"], p["w_xq"], p["w_xo"], kv, dx2)
    g["w_xq"], g["w_xo"], g["w_xkv"] = tn(h2, dq), tn(o, dx2), tn(memn, dkv)
    dmemn = matmul(dkv, p["w_xkv"], tb=True, name="dmem")
    proj = sv["proj"]
    r = merge_bwd(dx1, sv["gates"], sv["ya"], sv["yb"], sv["yc"], sv["outs"], sv["lses"], p["w_branch"], p["w_out"])
    dgl, dz, dya, dyb, douts, dlses = r[0], r[1:4], r[4], r[5], r[6:9], r[9:12]
    g["w_branch"] = jnp.stack([tn(y, dz[n]) for n, y in enumerate((sv["ya"], sv["yb"], sv["yc"]))])
    g["w_out"] = tn(sv["merged"], dx1)
    datt = []
    for k in range(3):
        dil = ATT_GROUPS[k][1]
        datt.append(attn_bwd(proj[k], A_Q0 if k == 0 else 0, k, sv["att"][k][0], sv["att"][k][1],
                             _by_residue(douts[k], dil), _by_residue(dlses[k], dil)))
    du, adj, gg, dpre, dy, dlam, g["s5_d"], g["b_glu"] = s5_bwd(proj[0], A_U, sv["xs"], p["bd"], p["cd"], p["lam"], p["s5_d"],
                                                                 p["w_glu"], p["b_glu"], dya)
    dbd = matmul(proj[0], adj, ta=True, a_col=(A_U, MIX_W), name="s5_dbd")
    dcd = matmul(sv["xs"], dy, ta=True, name="s5_dcd")
    g["w_glu"] = tn(gg, dpre)
    (g["s5_a_re"], g["s5_a_im"], g["s5_log_step"], g["s5_b_re"], g["s5_b_im"], g["s5_c_re"],
     g["s5_c_im"]) = s5_param_grads(p["disc_in"], dlam, dbd, dcd)
    dqg, dkg, dvg, dlr, drg, dwg, g["b_gla_gate"], g["g_gla_out"] = gla_bwd(proj[0], GLA_COLS, p["w_gate"], p["b_gla_gate"],
                                                                             p["g_gla_out"], sv["gst"], dyb)
    g["w_gla_gate"] = dwg[:GLA_GATE_RANK]
    s = dx3.shape[0]
    dproj = [jnp.concatenate([du, dqg, dkg, dvg, drg, *datt[0], dlr, jnp.zeros((s, A_W - A_LR - LR_PAD), BF16)], axis=1),
             jnp.concatenate(datt[1], axis=1), jnp.concatenate(datt[2], axis=1)]
    g["w_in"] = _join_w_in(*[tn(sv["hs"][k], dproj[k]) for k in range(3)], tn(sv["hs"][0], dgl))
    dhs = [_in_order(matmul(dproj[k], p["w_in_seg"][k], tb=True, name="in_proj_bwd"), ATT_GROUPS[k][1]) for k in range(3)]
    dhs.append(matmul(dgl, p["w_in_seg"][3], tb=True, name="in_proj_bwd"))
    dx0, g["g_mix"] = rmsnorm_bwd(sv["x0"], p["g_mix"], dhs, dx1, name="mix_norm_bwd")
    return dx0, g, dmemn


GLA_GATE_RANK = 16


def kernel(x, mem, *rest):
    nw = len(WEIGHTS)
    w = dict(zip(WEIGHTS, rest[:nw]))
    target = rest[nw]
    m = dict(zip(WEIGHTS, rest[nw + 1 : 2 * nw + 1]))
    v = dict(zip(WEIGHTS, rest[2 * nw + 1 : 3 * nw + 1]))
    x0, memx, target = x[0], mem[0], target[0]
    chip = 2 * lax.axis_index("x") + lax.axis_index("y")

    big = list(SHARDED)
    gate_pad = jnp.pad(w["w_gla_gate"], ((0, 0), (0, 0), (0, LR_PAD - w["w_gla_gate"].shape[2])))
    flat2 = lambda t: t.reshape(DEPTH, -1, t.shape[-1])
    gathered = all_gather([flat2(w[n].astype(BF16)) for n in big] + [gate_pad], by_core=True, name="gather_weights")
    gathered = [t.reshape(N_CHIPS, DEPTH, *s.shape[1:]) for t, s in zip(gathered, [w[n] for n in big] + [gate_pad])]
    full = {n: [jnp.concatenate([t[j, l] for j in range(N_CHIPS)], axis=SHARDED[n]) for l in range(DEPTH)]
            for n, t in zip(big, gathered)}
    gate_full = [jnp.concatenate([gathered[-1][j, l][:, : w["w_gla_gate"].shape[2]] for j in range(N_CHIPS)], axis=1)
                 for l in range(DEPTH)]

    memn = rmsnorm(memx, w["g_mem"], name="mem_norm")
    params, kvs = [], []
    for l in range(DEPTH):
        p = {n: full[n][l] for n in big if n != "w_in"}
        p["w_in_seg"] = _split_w_in(full["w_in"][l])
        p["w_gate"] = jnp.pad(gate_full[l], ((0, LR_PAD - GLA_GATE_RANK), (0, 0))).astype(BF16)
        for n in ("g_mix", "g_cross", "g_mlp"):
            p[n] = w[n][l]
        for n in ("s5_d", "b_glu", "b_gla_gate", "g_gla_out"):
            p[n] = w[n][l].reshape(1, -1)
        p["disc_in"], p["lam"], p["bd"], p["cd"] = s5_prepare(*[w[n][l] for n in WEIGHTS[2:9]])
        params.append(p)
        kvs.append(matmul(memn, p["w_xkv"], out_dtype=BF16, name="mem_kv"))

    xl, saved = x0, []
    for l in range(DEPTH):
        xl, sv = _layer_fwd(xl, params[l], kvs[l])
        saved.append(sv)
    loss8, dx, dg_final = loss_head(xl, w["g_final"], target)
    grads, dmem = [None] * DEPTH, []
    for l in reversed(range(DEPTH)):
        dx, grads[l], dm = _layer_bwd(dx, params[l], kvs[l], memn, saved[l])
        dmem.append(dm)
    _, dg_mem = rmsnorm_bwd(memx, w["g_mem"], dmem, jnp.zeros_like(memx), name="mem_norm_bwd")

    core = lax.axis_index("c")

    def halves(n):
        ax = SHARDED[n]
        keep, send = [], []
        for j in range(N_CHIPS):
            p0, p1 = [jnp.split(grads[l][n], N_CHIPS, axis=ax)[j] for l in range(DEPTH)]
            keep.append(jnp.where(core == 0, p0, p1))
            send.append(jnp.where(core == 0, p1, p0))
        flat = lambda ps: jnp.stack(ps).reshape(N_CHIPS, -1, ps[0].shape[-1])
        return flat(keep), flat(send)

    kept, sent = zip(*[halves(n) for n in big])
    theirs = sibling_swap(list(sent), name="swap_layers")
    chip_sums = [add_pair(a, b, name="add_cores") for a, b in zip(kept, theirs)]
    landed = chip_exchange(chip_sums, name="exchange_grads")
    reduced = [sum_blocks(t, name="sum_grads") for t in landed]
    other = sibling_swap([t[None] for t in reduced], name="pair_layers")
    out_g, delta, new_m, new_v = {}, {}, {}, {}
    as3d = lambda t: t.reshape(DEPTH, -1, t.shape[-1])
    for n, mine, theirs in zip(big, reduced, other):
        r = adamw_paired(as3d(w[n]), mine, theirs[0], as3d(m[n]), as3d(v[n]), name="adamw")
        out_g[n], delta[n], new_m[n], new_v[n] = [t.reshape(w[n].shape) for t in r]

    small = [n for n in WEIGHTS if n not in SHARDED]
    local_small = []
    for n in small:
        if n == "g_mem":
            local_small.append(dg_mem.reshape(w[n].shape))
        elif n == "g_final":
            local_small.append(dg_final.reshape(w[n].shape))
        elif n == "w_gla_gate":
            local_small.append(jnp.stack([grads[l][n] for l in range(DEPTH)]))
        else:
            local_small.append(jnp.stack([grads[l][n].reshape(w[n].shape[1:]) for l in range(DEPTH)]))
    packed = _pack(local_small + [loss8[0, :1]])
    (every,) = all_gather([packed], name="gather_small")
    summed = _unpack(sum_blocks(every, name="sum_small"), local_small + [loss8[0, :1]])
    loss = summed[-1].reshape(())
    for n, t in zip(small, summed[:-1]):
        if n == "w_gla_gate":
            t = lax.dynamic_slice_in_dim(t, chip * w[n].shape[2], w[n].shape[2], axis=2)
        out_g[n] = t

    like = [w[n] for n in small]
    r = adamw(_pack(like), _pack([out_g[n] for n in small]), _pack([m[n] for n in small]), _pack([v[n] for n in small]),
              name="adamw_small")
    for d, t in zip((delta, new_m, new_v), r):
        d.update(zip(small, _unpack(t, like)))
    return (loss, dx[None], *[out_g[n] for n in WEIGHTS], *[delta[n] for n in WEIGHTS], *[new_m[n] for n in WEIGHTS],
            *[new_v[n] for n in WEIGHTS])
```
